```python
import math
import jax, jax.numpy as jnp
from jax import lax
import numpy as np

D_MODEL = 2048
BATCH = 8
SEQ = 8192
DEPTH = 1

HEAD_DIM = 128
HEADS_PER_GROUP = 4
ATTN_GROUPS = ((128, 1), (512, 4), (2048, 16))
N_ATTN_HEADS = HEADS_PER_GROUP * len(ATTN_GROUPS)
ATTN_OUT_WIDTH = HEADS_PER_GROUP * HEAD_DIM
HQ = N_ATTN_HEADS * HEAD_DIM
SSM_WIDTH = D_MODEL // 2
SSM_GROUP = 16
SSM_GROUPS = SSM_WIDTH // SSM_GROUP
SSM_STATE = 64
D_FF = -(-8 * D_MODEL // (3 * 256)) * 256
IN_SIZES = (HQ, HQ, HQ, SSM_WIDTH, D_MODEL, D_MODEL)
IN_WIDTH = sum(IN_SIZES)
IN_SPLITS = tuple(int(i) for i in np.cumsum(IN_SIZES)[:-1])
EPS = 1e-6
DT_MIN = 1e-3
DT_MAX = 1e-1

kernel_name = 'hybrid_dilated_attn_s5_gated'


def rms_norm(x, gain):
    xf = x.astype(jnp.float32)
    y = xf * lax.rsqrt(jnp.mean(xf * xf, axis=-1, keepdims=True) + EPS)
    return (y * gain.astype(jnp.float32)).astype(x.dtype)


def alibi_slopes(n):
    return jnp.exp2(-8.0 * jnp.arange(1, n + 1, dtype=jnp.float32) / n)


def dilated_window_attention(q, k, v, slopes, window, dilation):
    b, s, h, e = q.shape
    blk = window // dilation
    sub_len = -(-s // dilation)
    n_blk = -(-sub_len // blk)
    s_pad = n_blk * blk * dilation

    def to_blocks(t):
        t = jnp.pad(t, ((0, 0), (0, s_pad - s), (0, 0), (0, 0)))
        t = t.reshape(b, n_blk * blk, dilation, h, e).transpose(0, 2, 1, 3, 4)
        return t.reshape(b, dilation, n_blk, blk, h, e)

    def with_prev(t):
        prev = jnp.pad(t, ((0, 0), (0, 0), (1, 0), (0, 0), (0, 0), (0, 0)))[:, :, :-1]
        return jnp.concatenate([prev, t], axis=3)

    def from_blocks(t):
        f = t.shape[-1]
        t = t.reshape(b, dilation, n_blk * blk, h, f).transpose(0, 2, 1, 3, 4)
        return t.reshape(b, s_pad, h, f)[:, :s]

    qb = to_blocks(q)
    kw = with_prev(to_blocks(k))
    vw = with_prev(to_blocks(v))
    scores = jnp.einsum('brnqhe,brnkhe->brnhqk', qb, kw).astype(jnp.float32) * (e ** -0.5)
    qi = jnp.arange(blk)[:, None]
    ki = jnp.arange(2 * blk)[None, :]
    dist = blk + qi - ki
    blk_idx = jnp.arange(n_blk)[:, None, None]
    valid = ((dist >= 0) & (dist <= blk))[None] & (blk_idx * blk - blk + ki[None] >= 0)
    bias = -slopes[:, None, None] * (dist * dilation).astype(jnp.float32)[None]
    scores = scores + bias[None, None, None]
    scores = jnp.where(valid[None, None, :, None], scores, -jnp.inf)
    lse = jax.nn.logsumexp(scores, axis=-1)
    probs = jnp.exp(scores - lse[..., None])
    out = jnp.einsum('brnhqk,brnkhe->brnqhe', probs.astype(v.dtype), vw)
    lse = from_blocks(lse.transpose(0, 1, 2, 4, 3)[..., None])[..., 0]
    return from_blocks(out), lse


def s5_ssm(u, a_re, a_im, log_dt, b_re, b_im, c_re, c_im, d_skip):
    bsz, s, _ = u.shape
    uf = u.astype(jnp.float32)
    ug = uf.reshape(bsz, s, SSM_GROUPS, SSM_GROUP)
    lam = lax.complex(a_re.astype(jnp.float32), a_im.astype(jnp.float32))
    dt = jnp.exp(log_dt.astype(jnp.float32))[:, None]
    lam_bar = jnp.exp(lam * dt)
    b_cplx = lax.complex(b_re.astype(jnp.float32), b_im.astype(jnp.float32))
    b_bar = ((lam_bar - 1.0) / lam)[..., None] * b_cplx
    bu = lax.complex(jnp.einsum('bsgc,gpc->sbgp', ug, jnp.real(b_bar)),
                     jnp.einsum('bsgc,gpc->sbgp', ug, jnp.imag(b_bar)))
    a_seq = jnp.broadcast_to(lam_bar[None, None], (s, 1) + lam_bar.shape)

    def combine(left, right):
        a_l, b_l = left
        a_r, b_r = right
        return a_r * a_l, a_r * b_l + b_r

    _, states = lax.associative_scan(combine, (a_seq, bu), axis=0)
    y = (jnp.einsum('sbgp,gcp->bsgc', jnp.real(states), c_re.astype(jnp.float32))
         - jnp.einsum('sbgp,gcp->bsgc', jnp.imag(states), c_im.astype(jnp.float32)))
    y = y.reshape(bsz, s, SSM_WIDTH) + d_skip.astype(jnp.float32) * uf
    return y.astype(u.dtype)


def hybrid_layer(x, norm_mix_pre, w_in, w_attn_up, ssm_a_re, ssm_a_im, ssm_log_dt, ssm_b_re, ssm_b_im,
                 ssm_c_re, ssm_c_im, ssm_d, w_glu_v, w_glu_g, w_out, norm_mix_post, norm_ffn_pre,
                 w_ffn_gate, w_ffn_up, w_ffn_down, norm_ffn_post):
    bsz, s, _ = x.shape
    h = rms_norm(x, norm_mix_pre)
    q, k, v, u, gate_a, gate_s = jnp.split(h @ w_in, IN_SPLITS, axis=-1)
    q = q.reshape(bsz, s, N_ATTN_HEADS, HEAD_DIM)
    k = k.reshape(bsz, s, N_ATTN_HEADS, HEAD_DIM)
    v = v.reshape(bsz, s, N_ATTN_HEADS, HEAD_DIM)
    slopes = alibi_slopes(N_ATTN_HEADS)
    outs, lses = [], []
    for g, (window, dilation) in enumerate(ATTN_GROUPS):
        hs = slice(g * HEADS_PER_GROUP, (g + 1) * HEADS_PER_GROUP)
        o, l = dilated_window_attention(q[:, :, hs], k[:, :, hs], v[:, :, hs], slopes[hs], window, dilation)
        outs.append(o)
        lses.append(l)
    mix_w = jax.nn.softmax(jnp.stack(lses), axis=0)
    attn = jnp.sum(mix_w[..., None] * jnp.stack(outs).astype(jnp.float32), axis=0)
    attn_branch = attn.reshape(bsz, s, ATTN_OUT_WIDTH).astype(x.dtype) @ w_attn_up
    y = jax.nn.gelu(s5_ssm(u, ssm_a_re, ssm_a_im, ssm_log_dt, ssm_b_re, ssm_b_im, ssm_c_re, ssm_c_im, ssm_d))
    ssm_branch = (y @ w_glu_v) * jax.nn.sigmoid(y @ w_glu_g)
    merged = jax.nn.sigmoid(gate_a) * attn_branch + jax.nn.sigmoid(gate_s) * ssm_branch
    x = x + rms_norm(merged @ w_out, norm_mix_post)
    h = rms_norm(x, norm_ffn_pre)
    f = (jax.nn.silu(h @ w_ffn_gate) * (h @ w_ffn_up)) @ w_ffn_down
    return x + rms_norm(f, norm_ffn_post)


def _fwd_setup_inputs(seed: int = 0) -> dict:
    key = jax.random.key(seed)
    ks = jax.random.split(key, 24)

    def nrm(k, shape, scale):
        return jax.random.normal(k, shape, jnp.float32) * scale

    def gain(k, n):
        return 1.0 + 0.02 * jax.random.normal(k, (DEPTH, n), jnp.float32)

    g, p, c = SSM_GROUPS, SSM_STATE, SSM_GROUP
    return {
        'x': nrm(ks[0], (BATCH, SEQ, D_MODEL), 1.0),
        'norm_mix_pre': gain(ks[1], D_MODEL),
        'w_in': nrm(ks[2], (DEPTH, D_MODEL, IN_WIDTH), D_MODEL ** -0.5),
        'w_attn_up': nrm(ks[3], (DEPTH, ATTN_OUT_WIDTH, D_MODEL), ATTN_OUT_WIDTH ** -0.5),
        'ssm_a_re': -0.5 + 0.01 * jax.random.normal(ks[4], (DEPTH, g, p), jnp.float32),
        'ssm_a_im': jnp.pi * jnp.arange(p, dtype=jnp.float32)[None, None, :] + 0.01 * jax.random.normal(ks[5], (DEPTH, g, p), jnp.float32),
        'ssm_log_dt': jax.random.uniform(ks[6], (DEPTH, g), jnp.float32, math.log(DT_MIN), math.log(DT_MAX)),
        'ssm_b_re': nrm(ks[7], (DEPTH, g, p, c), (2 * c) ** -0.5),
        'ssm_b_im': nrm(ks[8], (DEPTH, g, p, c), (2 * c) ** -0.5),
        'ssm_c_re': nrm(ks[9], (DEPTH, g, c, p), (2 * p) ** -0.5 * 4.0),
        'ssm_c_im': nrm(ks[10], (DEPTH, g, c, p), (2 * p) ** -0.5 * 4.0),
        'ssm_d': nrm(ks[11], (DEPTH, SSM_WIDTH), 1.0),
        'w_glu_v': nrm(ks[12], (DEPTH, SSM_WIDTH, D_MODEL), SSM_WIDTH ** -0.5),
        'w_glu_g': nrm(ks[13], (DEPTH, SSM_WIDTH, D_MODEL), SSM_WIDTH ** -0.5),
        'w_out': nrm(ks[14], (DEPTH, D_MODEL, D_MODEL), D_MODEL ** -0.5),
        'norm_mix_post': gain(ks[15], D_MODEL),
        'norm_ffn_pre': gain(ks[16], D_MODEL),
        'w_ffn_gate': nrm(ks[17], (DEPTH, D_MODEL, D_FF), D_MODEL ** -0.5),
        'w_ffn_up': nrm(ks[18], (DEPTH, D_MODEL, D_FF), D_MODEL ** -0.5),
        'w_ffn_down': nrm(ks[19], (DEPTH, D_FF, D_MODEL), D_FF ** -0.5),
        'norm_ffn_post': gain(ks[20], D_MODEL),
    }


def _fwd_reference(x, norm_mix_pre, w_in, w_attn_up, ssm_a_re, ssm_a_im, ssm_log_dt, ssm_b_re, ssm_b_im,
              ssm_c_re, ssm_c_im, ssm_d, w_glu_v, w_glu_g, w_out, norm_mix_post, norm_ffn_pre,
              w_ffn_gate, w_ffn_up, w_ffn_down, norm_ffn_post):
    for i in range(DEPTH):
        x = hybrid_layer(x, norm_mix_pre[i], w_in[i], w_attn_up[i], ssm_a_re[i], ssm_a_im[i], ssm_log_dt[i],
                         ssm_b_re[i], ssm_b_im[i], ssm_c_re[i], ssm_c_im[i], ssm_d[i], w_glu_v[i], w_glu_g[i],
                         w_out[i], norm_mix_post[i], norm_ffn_pre[i], w_ffn_gate[i], w_ffn_up[i],
                         w_ffn_down[i], norm_ffn_post[i])
    return x


import jax as _jax
import jax.numpy as _jnp

TWIN_FORMAT = 'train_step'
FWD_PARAMS = ['x', 'norm_mix_pre', 'w_in', 'w_attn_up', 'ssm_a_re', 'ssm_a_im', 'ssm_log_dt', 'ssm_b_re', 'ssm_b_im', 'ssm_c_re', 'ssm_c_im', 'ssm_d', 'w_glu_v', 'w_glu_g', 'w_out', 'norm_mix_post', 'norm_ffn_pre', 'w_ffn_gate', 'w_ffn_up', 'w_ffn_down', 'norm_ffn_post']
TWIN_WEIGHTS = ['norm_mix_pre', 'w_in', 'w_attn_up', 'ssm_a_re', 'ssm_a_im', 'ssm_log_dt', 'ssm_b_re', 'ssm_b_im', 'ssm_c_re', 'ssm_c_im', 'ssm_d', 'w_glu_v', 'w_glu_g', 'w_out', 'norm_mix_post', 'norm_ffn_pre', 'w_ffn_gate', 'w_ffn_up', 'w_ffn_down', 'norm_ffn_post']
TWIN_DIFF_INPUT = 'x'
TWIN_INPUTS = ['x', 'norm_mix_pre', 'w_in', 'w_attn_up', 'ssm_a_re', 'ssm_a_im', 'ssm_log_dt', 'ssm_b_re', 'ssm_b_im', 'ssm_c_re', 'ssm_c_im', 'ssm_d', 'w_glu_v', 'w_glu_g', 'w_out', 'norm_mix_post', 'norm_ffn_pre', 'w_ffn_gate', 'w_ffn_up', 'w_ffn_down', 'norm_ffn_post', 'loss_target', 'm_norm_mix_pre', 'm_w_in', 'm_w_attn_up', 'm_ssm_a_re', 'm_ssm_a_im', 'm_ssm_log_dt', 'm_ssm_b_re', 'm_ssm_b_im', 'm_ssm_c_re', 'm_ssm_c_im', 'm_ssm_d', 'm_w_glu_v', 'm_w_glu_g', 'm_w_out', 'm_norm_mix_post', 'm_norm_ffn_pre', 'm_w_ffn_gate', 'm_w_ffn_up', 'm_w_ffn_down', 'm_norm_ffn_post', 'v_norm_mix_pre', 'v_w_in', 'v_w_attn_up', 'v_ssm_a_re', 'v_ssm_a_im', 'v_ssm_log_dt', 'v_ssm_b_re', 'v_ssm_b_im', 'v_ssm_c_re', 'v_ssm_c_im', 'v_ssm_d', 'v_w_glu_v', 'v_w_glu_g', 'v_w_out', 'v_norm_mix_post', 'v_norm_ffn_pre', 'v_w_ffn_gate', 'v_w_ffn_up', 'v_w_ffn_down', 'v_norm_ffn_post']
TWIN_OUTPUTS = ['loss', 'grad_x', 'grad_norm_mix_pre', 'grad_w_in', 'grad_w_attn_up', 'grad_ssm_a_re', 'grad_ssm_a_im', 'grad_ssm_log_dt', 'grad_ssm_b_re', 'grad_ssm_b_im', 'grad_ssm_c_re', 'grad_ssm_c_im', 'grad_ssm_d', 'grad_w_glu_v', 'grad_w_glu_g', 'grad_w_out', 'grad_norm_mix_post', 'grad_norm_ffn_pre', 'grad_w_ffn_gate', 'grad_w_ffn_up', 'grad_w_ffn_down', 'grad_norm_ffn_post', 'delta_norm_mix_pre', 'delta_w_in', 'delta_w_attn_up', 'delta_ssm_a_re', 'delta_ssm_a_im', 'delta_ssm_log_dt', 'delta_ssm_b_re', 'delta_ssm_b_im', 'delta_ssm_c_re', 'delta_ssm_c_im', 'delta_ssm_d', 'delta_w_glu_v', 'delta_w_glu_g', 'delta_w_out', 'delta_norm_mix_post', 'delta_norm_ffn_pre', 'delta_w_ffn_gate', 'delta_w_ffn_up', 'delta_w_ffn_down', 'delta_norm_ffn_post', 'new_m_norm_mix_pre', 'new_m_w_in', 'new_m_w_attn_up', 'new_m_ssm_a_re', 'new_m_ssm_a_im', 'new_m_ssm_log_dt', 'new_m_ssm_b_re', 'new_m_ssm_b_im', 'new_m_ssm_c_re', 'new_m_ssm_c_im', 'new_m_ssm_d', 'new_m_w_glu_v', 'new_m_w_glu_g', 'new_m_w_out', 'new_m_norm_mix_post', 'new_m_norm_ffn_pre', 'new_m_w_ffn_gate', 'new_m_w_ffn_up', 'new_m_w_ffn_down', 'new_m_norm_ffn_post', 'new_v_norm_mix_pre', 'new_v_w_in', 'new_v_w_attn_up', 'new_v_ssm_a_re', 'new_v_ssm_a_im', 'new_v_ssm_log_dt', 'new_v_ssm_b_re', 'new_v_ssm_b_im', 'new_v_ssm_c_re', 'new_v_ssm_c_im', 'new_v_ssm_d', 'new_v_w_glu_v', 'new_v_w_glu_g', 'new_v_w_out', 'new_v_norm_mix_post', 'new_v_norm_ffn_pre', 'new_v_w_ffn_gate', 'new_v_w_ffn_up', 'new_v_w_ffn_down', 'new_v_norm_ffn_post']
TWIN_LEAF_KINDS = {'loss': 'loss', 'grad_x': 'grad_x', 'grad_norm_mix_pre': 'grad_w', 'grad_w_in': 'grad_w', 'grad_w_attn_up': 'grad_w', 'grad_ssm_a_re': 'grad_w', 'grad_ssm_a_im': 'grad_w', 'grad_ssm_log_dt': 'grad_w', 'grad_ssm_b_re': 'grad_w', 'grad_ssm_b_im': 'grad_w', 'grad_ssm_c_re': 'grad_w', 'grad_ssm_c_im': 'grad_w', 'grad_ssm_d': 'grad_w', 'grad_w_glu_v': 'grad_w', 'grad_w_glu_g': 'grad_w', 'grad_w_out': 'grad_w', 'grad_norm_mix_post': 'grad_w', 'grad_norm_ffn_pre': 'grad_w', 'grad_w_ffn_gate': 'grad_w', 'grad_w_ffn_up': 'grad_w', 'grad_w_ffn_down': 'grad_w', 'grad_norm_ffn_post': 'grad_w', 'delta_norm_mix_pre': 'delta_w', 'delta_w_in': 'delta_w', 'delta_w_attn_up': 'delta_w', 'delta_ssm_a_re': 'delta_w', 'delta_ssm_a_im': 'delta_w', 'delta_ssm_log_dt': 'delta_w', 'delta_ssm_b_re': 'delta_w', 'delta_ssm_b_im': 'delta_w', 'delta_ssm_c_re': 'delta_w', 'delta_ssm_c_im': 'delta_w', 'delta_ssm_d': 'delta_w', 'delta_w_glu_v': 'delta_w', 'delta_w_glu_g': 'delta_w', 'delta_w_out': 'delta_w', 'delta_norm_mix_post': 'delta_w', 'delta_norm_ffn_pre': 'delta_w', 'delta_w_ffn_gate': 'delta_w', 'delta_w_ffn_up': 'delta_w', 'delta_w_ffn_down': 'delta_w', 'delta_norm_ffn_post': 'delta_w', 'new_m_norm_mix_pre': 'new_m', 'new_m_w_in': 'new_m', 'new_m_w_attn_up': 'new_m', 'new_m_ssm_a_re': 'new_m', 'new_m_ssm_a_im': 'new_m', 'new_m_ssm_log_dt': 'new_m', 'new_m_ssm_b_re': 'new_m', 'new_m_ssm_b_im': 'new_m', 'new_m_ssm_c_re': 'new_m', 'new_m_ssm_c_im': 'new_m', 'new_m_ssm_d': 'new_m', 'new_m_w_glu_v': 'new_m', 'new_m_w_glu_g': 'new_m', 'new_m_w_out': 'new_m', 'new_m_norm_mix_post': 'new_m', 'new_m_norm_ffn_pre': 'new_m', 'new_m_w_ffn_gate': 'new_m', 'new_m_w_ffn_up': 'new_m', 'new_m_w_ffn_down': 'new_m', 'new_m_norm_ffn_post': 'new_m', 'new_v_norm_mix_pre': 'new_v', 'new_v_w_in': 'new_v', 'new_v_w_attn_up': 'new_v', 'new_v_ssm_a_re': 'new_v', 'new_v_ssm_a_im': 'new_v', 'new_v_ssm_log_dt': 'new_v', 'new_v_ssm_b_re': 'new_v', 'new_v_ssm_b_im': 'new_v', 'new_v_ssm_c_re': 'new_v', 'new_v_ssm_c_im': 'new_v', 'new_v_ssm_d': 'new_v', 'new_v_w_glu_v': 'new_v', 'new_v_w_glu_g': 'new_v', 'new_v_w_out': 'new_v', 'new_v_norm_mix_post': 'new_v', 'new_v_norm_ffn_pre': 'new_v', 'new_v_w_ffn_gate': 'new_v', 'new_v_w_ffn_up': 'new_v', 'new_v_w_ffn_down': 'new_v', 'new_v_norm_ffn_post': 'new_v'}


def _forward(args):
    return _fwd_reference(*[args[k] for k in FWD_PARAMS])


def _output_shape():
    def fwd():
        inp = _fwd_setup_inputs(0)
        return _fwd_reference(*[inp[k] for k in FWD_PARAMS])
    out = _jax.eval_shape(fwd)
    return out.shape, out.dtype

N_MICROBATCH = 1
ADAM_LR = 0.001
ADAM_B1 = 0.9
ADAM_B2 = 0.999
ADAM_EPS = 1e-08
ADAM_WD = 0.01
ADAM_STEP = 10
PER_EXAMPLE_BATCH_AXIS = {'x': 0, 'loss_target': 0}
SHARED_INPUTS = []
_WEIGHT_DTYPES = {'norm_mix_pre': _jnp.float32, 'w_in': _jnp.float32, 'w_attn_up': _jnp.float32, 'ssm_a_re': _jnp.float32, 'ssm_a_im': _jnp.float32, 'ssm_log_dt': _jnp.float32, 'ssm_b_re': _jnp.float32, 'ssm_b_im': _jnp.float32, 'ssm_c_re': _jnp.float32, 'ssm_c_im': _jnp.float32, 'ssm_d': _jnp.float32, 'w_glu_v': _jnp.float32, 'w_glu_g': _jnp.float32, 'w_out': _jnp.float32, 'norm_mix_post': _jnp.float32, 'norm_ffn_pre': _jnp.float32, 'w_ffn_gate': _jnp.float32, 'w_ffn_up': _jnp.float32, 'w_ffn_down': _jnp.float32, 'norm_ffn_post': _jnp.float32}
MOMENT_SCALE = {'norm_mix_pre': 5.198283e-01, 'w_in': 2.197842e-01, 'w_attn_up': 2.334349e-01, 'ssm_a_re': 8.376648e-02, 'ssm_a_im': 8.217452e-02, 'ssm_log_dt': 8.713547e+01, 'ssm_b_re': 5.478267e-02, 'ssm_b_im': 5.500518e-02, 'ssm_c_re': 2.797359e-02, 'ssm_c_im': 2.754138e-02, 'ssm_d': 1.993872e+00, 'w_glu_v': 1.444100e+00, 'w_glu_g': 2.080947e-01, 'w_out': 1.381568e+00, 'norm_mix_post': 3.249835e+01, 'norm_ffn_pre': 1.147110e+00, 'w_ffn_gate': 3.147027e-01, 'w_ffn_up': 5.984578e-01, 'w_ffn_down': 1.015656e+00, 'norm_ffn_post': 3.204477e+01}


def _to_microbatches(a, axis):
    t = _jnp.moveaxis(a, axis, 0)
    t = t.reshape((N_MICROBATCH, t.shape[0] // N_MICROBATCH) + t.shape[1:])
    return _jnp.moveaxis(t, 1, axis + 1)


def setup_inputs(seed: int = 0) -> dict:
    inp = _fwd_setup_inputs(seed)
    key = _jax.random.fold_in(_jax.random.key(seed), 7919)
    shape, _ = _output_shape()
    out = dict(inp)
    out["loss_target"] = _jax.random.normal(_jax.random.fold_in(key, 0), shape, _jnp.float32)
    for i, name in enumerate(TWIN_WEIGHTS):
        w = inp[name].astype(_jnp.float32)
        if MOMENT_SCALE is None:
            s = _jnp.sqrt(_jnp.mean(_jnp.square(w)) + 1e-30)
        else:
            s = MOMENT_SCALE[name]
        km, kv = _jax.random.split(_jax.random.fold_in(key, i + 1))
        out[name] = w
        out["m_" + name] = s * _jax.random.normal(km, w.shape, _jnp.float32)
        out["v_" + name] = (s * s) * _jax.random.uniform(kv, w.shape, _jnp.float32, 0.5, 1.5)
    if N_MICROBATCH > 1:
        for name, axis in PER_EXAMPLE_BATCH_AXIS.items():
            out[name] = _to_microbatches(out[name], axis)
    return {'x': out['x'], 'norm_mix_pre': out['norm_mix_pre'], 'w_in': out['w_in'], 'w_attn_up': out['w_attn_up'], 'ssm_a_re': out['ssm_a_re'], 'ssm_a_im': out['ssm_a_im'], 'ssm_log_dt': out['ssm_log_dt'], 'ssm_b_re': out['ssm_b_re'], 'ssm_b_im': out['ssm_b_im'], 'ssm_c_re': out['ssm_c_re'], 'ssm_c_im': out['ssm_c_im'], 'ssm_d': out['ssm_d'], 'w_glu_v': out['w_glu_v'], 'w_glu_g': out['w_glu_g'], 'w_out': out['w_out'], 'norm_mix_post': out['norm_mix_post'], 'norm_ffn_pre': out['norm_ffn_pre'], 'w_ffn_gate': out['w_ffn_gate'], 'w_ffn_up': out['w_ffn_up'], 'w_ffn_down': out['w_ffn_down'], 'norm_ffn_post': out['norm_ffn_post'], 'loss_target': out['loss_target'], 'm_norm_mix_pre': out['m_norm_mix_pre'], 'm_w_in': out['m_w_in'], 'm_w_attn_up': out['m_w_attn_up'], 'm_ssm_a_re': out['m_ssm_a_re'], 'm_ssm_a_im': out['m_ssm_a_im'], 'm_ssm_log_dt': out['m_ssm_log_dt'], 'm_ssm_b_re': out['m_ssm_b_re'], 'm_ssm_b_im': out['m_ssm_b_im'], 'm_ssm_c_re': out['m_ssm_c_re'], 'm_ssm_c_im': out['m_ssm_c_im'], 'm_ssm_d': out['m_ssm_d'], 'm_w_glu_v': out['m_w_glu_v'], 'm_w_glu_g': out['m_w_glu_g'], 'm_w_out': out['m_w_out'], 'm_norm_mix_post': out['m_norm_mix_post'], 'm_norm_ffn_pre': out['m_norm_ffn_pre'], 'm_w_ffn_gate': out['m_w_ffn_gate'], 'm_w_ffn_up': out['m_w_ffn_up'], 'm_w_ffn_down': out['m_w_ffn_down'], 'm_norm_ffn_post': out['m_norm_ffn_post'], 'v_norm_mix_pre': out['v_norm_mix_pre'], 'v_w_in': out['v_w_in'], 'v_w_attn_up': out['v_w_attn_up'], 'v_ssm_a_re': out['v_ssm_a_re'], 'v_ssm_a_im': out['v_ssm_a_im'], 'v_ssm_log_dt': out['v_ssm_log_dt'], 'v_ssm_b_re': out['v_ssm_b_re'], 'v_ssm_b_im': out['v_ssm_b_im'], 'v_ssm_c_re': out['v_ssm_c_re'], 'v_ssm_c_im': out['v_ssm_c_im'], 'v_ssm_d': out['v_ssm_d'], 'v_w_glu_v': out['v_w_glu_v'], 'v_w_glu_g': out['v_w_glu_g'], 'v_w_out': out['v_w_out'], 'v_norm_mix_post': out['v_norm_mix_post'], 'v_norm_ffn_pre': out['v_norm_ffn_pre'], 'v_w_ffn_gate': out['v_w_ffn_gate'], 'v_w_ffn_up': out['v_w_ffn_up'], 'v_w_ffn_down': out['v_w_ffn_down'], 'v_norm_ffn_post': out['v_norm_ffn_post']}


def _loss(weights, diff, rest, loss_target):
    with _jax.named_scope("forward"):
        args = {**rest, TWIN_DIFF_INPUT: diff, **{k: w.astype(_WEIGHT_DTYPES[k]) for k, w in weights.items()}}
        y = _forward(args)
    with _jax.named_scope("loss_head"):
        err = _jnp.square(y.astype(_jnp.float32) - loss_target)
        return 0.5 * _jnp.sum(_jnp.mean(err, axis=-1)) if err.ndim else 0.5 * err


def _adamw(w, g, m, v):
    m = ADAM_B1 * m + (1.0 - ADAM_B1) * g
    v = ADAM_B2 * v + (1.0 - ADAM_B2) * _jnp.square(g)
    m_hat = m / (1.0 - ADAM_B1 ** ADAM_STEP)
    v_hat = v / (1.0 - ADAM_B2 ** ADAM_STEP)
    delta = -ADAM_LR * (m_hat / (_jnp.sqrt(v_hat) + ADAM_EPS) + ADAM_WD * w)
    return delta, m, v


def reference(x, norm_mix_pre, w_in, w_attn_up, ssm_a_re, ssm_a_im, ssm_log_dt, ssm_b_re, ssm_b_im, ssm_c_re, ssm_c_im, ssm_d, w_glu_v, w_glu_g, w_out, norm_mix_post, norm_ffn_pre, w_ffn_gate, w_ffn_up, w_ffn_down, norm_ffn_post, loss_target, m_norm_mix_pre, m_w_in, m_w_attn_up, m_ssm_a_re, m_ssm_a_im, m_ssm_log_dt, m_ssm_b_re, m_ssm_b_im, m_ssm_c_re, m_ssm_c_im, m_ssm_d, m_w_glu_v, m_w_glu_g, m_w_out, m_norm_mix_post, m_norm_ffn_pre, m_w_ffn_gate, m_w_ffn_up, m_w_ffn_down, m_norm_ffn_post, v_norm_mix_pre, v_w_in, v_w_attn_up, v_ssm_a_re, v_ssm_a_im, v_ssm_log_dt, v_ssm_b_re, v_ssm_b_im, v_ssm_c_re, v_ssm_c_im, v_ssm_d, v_w_glu_v, v_w_glu_g, v_w_out, v_norm_mix_post, v_norm_ffn_pre, v_w_ffn_gate, v_w_ffn_up, v_w_ffn_down, v_norm_ffn_post):
    given = dict(x=x, norm_mix_pre=norm_mix_pre, w_in=w_in, w_attn_up=w_attn_up, ssm_a_re=ssm_a_re, ssm_a_im=ssm_a_im, ssm_log_dt=ssm_log_dt, ssm_b_re=ssm_b_re, ssm_b_im=ssm_b_im, ssm_c_re=ssm_c_re, ssm_c_im=ssm_c_im, ssm_d=ssm_d, w_glu_v=w_glu_v, w_glu_g=w_glu_g, w_out=w_out, norm_mix_post=norm_mix_post, norm_ffn_pre=norm_ffn_pre, w_ffn_gate=w_ffn_gate, w_ffn_up=w_ffn_up, w_ffn_down=w_ffn_down, norm_ffn_post=norm_ffn_post, loss_target=loss_target, m_norm_mix_pre=m_norm_mix_pre, m_w_in=m_w_in, m_w_attn_up=m_w_attn_up, m_ssm_a_re=m_ssm_a_re, m_ssm_a_im=m_ssm_a_im, m_ssm_log_dt=m_ssm_log_dt, m_ssm_b_re=m_ssm_b_re, m_ssm_b_im=m_ssm_b_im, m_ssm_c_re=m_ssm_c_re, m_ssm_c_im=m_ssm_c_im, m_ssm_d=m_ssm_d, m_w_glu_v=m_w_glu_v, m_w_glu_g=m_w_glu_g, m_w_out=m_w_out, m_norm_mix_post=m_norm_mix_post, m_norm_ffn_pre=m_norm_ffn_pre, m_w_ffn_gate=m_w_ffn_gate, m_w_ffn_up=m_w_ffn_up, m_w_ffn_down=m_w_ffn_down, m_norm_ffn_post=m_norm_ffn_post, v_norm_mix_pre=v_norm_mix_pre, v_w_in=v_w_in, v_w_attn_up=v_w_attn_up, v_ssm_a_re=v_ssm_a_re, v_ssm_a_im=v_ssm_a_im, v_ssm_log_dt=v_ssm_log_dt, v_ssm_b_re=v_ssm_b_re, v_ssm_b_im=v_ssm_b_im, v_ssm_c_re=v_ssm_c_re, v_ssm_c_im=v_ssm_c_im, v_ssm_d=v_ssm_d, v_w_glu_v=v_w_glu_v, v_w_glu_g=v_w_glu_g, v_w_out=v_w_out, v_norm_mix_post=v_norm_mix_post, v_norm_ffn_pre=v_norm_ffn_pre, v_w_ffn_gate=v_w_ffn_gate, v_w_ffn_up=v_w_ffn_up, v_w_ffn_down=v_w_ffn_down, v_norm_ffn_post=v_norm_ffn_post)
    weights = {n: given[n] for n in TWIN_WEIGHTS}
    shared = {n: given[n] for n in SHARED_INPUTS}
    per_example = {n: given[n] for n in ['x']}
    grad_fn = _jax.value_and_grad(_loss, argnums=(0, 1))

    def one_microbatch(ex, loss_target):
        ex = dict(ex)
        diff = ex.pop(TWIN_DIFF_INPUT)
        return grad_fn(weights, diff, {**shared, **ex}, loss_target)

    if N_MICROBATCH == 1:
        loss, (grad_w, grad_x) = one_microbatch(per_example, given["loss_target"])
    else:
        def body(carry, xs):
            loss_sum, grad_sum = carry
            l_k, (gw_k, gx_k) = one_microbatch(xs[0], xs[1])
            with _jax.named_scope("update"):
                return (loss_sum + l_k, _jax.tree.map(_jnp.add, grad_sum, gw_k)), gx_k

        init = (_jnp.zeros((), _jnp.float32), _jax.tree.map(_jnp.zeros_like, weights))
        (loss, grad_w), grad_x = _jax.lax.scan(body, init, (per_example, given["loss_target"]))
    with _jax.named_scope("update"):
        delta_w, new_m, new_v = {}, {}, {}
        for n in TWIN_WEIGHTS:
            delta_w[n], new_m[n], new_v[n] = _adamw(weights[n], grad_w[n], given["m_" + n], given["v_" + n])
    return (loss, grad_x, *[grad_w[n] for n in TWIN_WEIGHTS], *[delta_w[n] for n in TWIN_WEIGHTS],
            *[new_m[n] for n in TWIN_WEIGHTS], *[new_v[n] for n in TWIN_WEIGHTS])
```

```python
import functools
import math

import numpy as np
import jax
import jax.numpy as jnp
from jax import lax
from jax.experimental import pallas as pl
from jax.experimental.pallas import tpu as pltpu

F32 = jnp.float32
BF16 = jnp.bfloat16

D_MODEL = 2048
HEAD_DIM = 128
HEADS_PER_GROUP = 4
ATTN_GROUPS = ((128, 1), (512, 4), (2048, 16))
N_HEADS = HEADS_PER_GROUP * len(ATTN_GROUPS)
GROUP_W = HEADS_PER_GROUP * HEAD_DIM
HQ = N_HEADS * HEAD_DIM
SSM_W = 1024
SSM_GROUP = 16
SSM_GROUPS = 64
SSM_STATE = 64
STATE_W = SSM_GROUPS * SSM_STATE
D_FF = 5632
EPS = 1e-6
N_DEV = 8
SEGS = 8
BD = 8

ADAM_LR, ADAM_B1, ADAM_B2, ADAM_EPS, ADAM_WD, ADAM_STEP = 0.001, 0.9, 0.999, 1e-08, 0.01, 10

VMEM_LIMIT = 56 * 1024 * 1024
HBM_SPEC = pl.BlockSpec(memory_space=pltpu.HBM)
MESH_ID = pl.DeviceIdType.MESH
NEG = -1e30


def _pcall(body, **kw):
    return pl.pallas_call(body, **kw)


def _cparams(sem=None):
    if sem is None:
        return pltpu.CompilerParams(vmem_limit_bytes=VMEM_LIMIT)
    return pltpu.CompilerParams(vmem_limit_bytes=VMEM_LIMIT, dimension_semantics=sem)


_DN = {"nn": (((1,), (0,)), ((), ())), "nt": (((1,), (1,)), ((), ())), "tn": (((0,), (0,)), ((), ()))}


def mm(pairs, mode, out_dtype, name, tm=512, tn=512, tk=512):
    a0, b0 = pairs[0]
    if mode == "nn":
        (m, k), n = a0.shape, b0.shape[1]
    elif mode == "nt":
        (m, k), n = a0.shape, b0.shape[0]
    else:
        (k, m), n = a0.shape, b0.shape[1]
    tm, tn, tk = min(tm, m), min(tn, n), min(tk, k)
    assert m % tm == 0 and n % tn == 0 and k % tk == 0, (name, m, n, k)
    nk = k // tk
    npairs = len(pairs)

    def body(*refs):
        o_ref, acc = refs[2 * npairs], refs[2 * npairs + 1]
        kk = pl.program_id(2)

        @pl.when(kk == 0)
        def _():
            acc[...] = jnp.zeros_like(acc)

        tot = None
        for p in range(npairs):
            a = refs[2 * p][...].astype(BF16)
            b = refs[2 * p + 1][...].astype(BF16)
            d = lax.dot_general(a, b, _DN[mode], preferred_element_type=F32)
            tot = d if tot is None else tot + d
        acc[...] += tot

        @pl.when(kk == nk - 1)
        def _():
            o_ref[...] = acc[...].astype(o_ref.dtype)

    if mode == "nn":
        sp = [pl.BlockSpec((tm, tk), lambda i, j, kk: (i, kk)), pl.BlockSpec((tk, tn), lambda i, j, kk: (kk, j))]
    elif mode == "nt":
        sp = [pl.BlockSpec((tm, tk), lambda i, j, kk: (i, kk)), pl.BlockSpec((tn, tk), lambda i, j, kk: (j, kk))]
    else:
        sp = [pl.BlockSpec((tk, tm), lambda i, j, kk: (kk, i)), pl.BlockSpec((tk, tn), lambda i, j, kk: (kk, j))]
    return _pcall(
        body, name=name, grid=(m // tm, n // tn, nk), in_specs=sp * npairs,
        out_specs=pl.BlockSpec((tm, tn), lambda i, j, kk: (i, j)),
        out_shape=jax.ShapeDtypeStruct((m, n), out_dtype),
        scratch_shapes=[pltpu.VMEM((tm, tn), F32)],
        compiler_params=_cparams(("parallel", "parallel", "arbitrary")),
    )(*[t for pr in pairs for t in pr])


def bdmm(pairs, name, add=None, ts=1024):
    a0, w0 = pairs[0]
    s = a0.shape[0]
    ka, kn = w0.shape[1], w0.shape[2]
    ts = min(ts, s)
    npairs = len(pairs)

    def body(*refs):
        o_ref = refs[-1]
        tot = None
        for p in range(npairs):
            d = jnp.dot(refs[2 * p][...].astype(BF16), refs[2 * p + 1][...].astype(BF16), preferred_element_type=F32)
            tot = d if tot is None else tot + d
        if add is not None:
            tot = tot + refs[2 * npairs][...]
        o_ref[...] = tot

    sp = []
    args = []
    for a, w in pairs:
        sp += [pl.BlockSpec((ts, w.shape[1]), lambda i, j: (i, j)), pl.BlockSpec((None, w.shape[1], kn), lambda i, j: (j, 0, 0))]
        args += [a, w]
    if add is not None:
        sp.append(pl.BlockSpec((ts, kn), lambda i, j: (i, j)))
        args.append(add)
    return _pcall(
        body, name=name, grid=(s // ts, BD), in_specs=sp, out_specs=pl.BlockSpec((ts, kn), lambda i, j: (i, j)),
        out_shape=jax.ShapeDtypeStruct((s, BD * kn), F32), compiler_params=_cparams(("parallel", "parallel")),
    )(*args)


def bd_tn(a, b, name, ts=512):
    s = a.shape[0]
    ka, kb = a.shape[1] // BD, b.shape[1] // BD
    ts = min(ts, s)
    ns = s // ts

    def body(a_ref, b_ref, o_ref, acc):
        kk = pl.program_id(1)

        @pl.when(kk == 0)
        def _():
            acc[...] = jnp.zeros_like(acc)

        acc[...] += lax.dot_general(a_ref[...].astype(BF16), b_ref[...].astype(BF16), _DN["tn"], preferred_element_type=F32)

        @pl.when(kk == ns - 1)
        def _():
            o_ref[...] = acc[...]

    return _pcall(
        body, name=name, grid=(BD, ns),
        in_specs=[pl.BlockSpec((ts, ka), lambda j, kk: (kk, j)), pl.BlockSpec((ts, kb), lambda j, kk: (kk, j))],
        out_specs=pl.BlockSpec((None, ka, kb), lambda j, kk: (j, 0, 0)),
        out_shape=jax.ShapeDtypeStruct((BD, ka, kb), F32), scratch_shapes=[pltpu.VMEM((ka, kb), F32)],
        compiler_params=_cparams(("parallel", "arbitrary")),
    )(a, b)


def rowwise(name, fn, row_ins, const_ins, row_outs, acc_outs=(), ts=None):
    s = row_ins[0].shape[0]
    if ts is None:
        per_row = sum(a.shape[1] * a.dtype.itemsize for a in row_ins) + sum(w * jnp.dtype(dt).itemsize for w, dt in row_outs)
        ts = 512
        while ts > 8 and 2 * ts * per_row > 20 * 1024 * 1024:
            ts //= 2
    ts = min(ts, s)
    assert s % ts == 0
    nr, nc, no, na = len(row_ins), len(const_ins), len(row_outs), len(acc_outs)

    def body(*refs):
        rows = [r[...] for r in refs[:nr]]
        consts = [r[...] for r in refs[nr:nr + nc]]
        outs, accs = fn(rows, consts)
        for r, v in zip(refs[nr + nc:nr + nc + no], outs):
            r[...] = v.astype(r.dtype)
        if na:
            first = pl.program_id(0) == 0
            for r, v in zip(refs[nr + nc + no:], accs):
                @pl.when(first)
                def _(r=r, v=v):
                    r[...] = v

                @pl.when(jnp.logical_not(first))
                def _(r=r, v=v):
                    r[...] += v

    in_specs = [pl.BlockSpec((ts, a.shape[1]), lambda i: (i, 0)) for a in row_ins]
    in_specs += [pl.BlockSpec(c.shape, lambda i, nd=c.ndim: (0,) * nd) for c in const_ins]
    out_specs = [pl.BlockSpec((ts, w), lambda i: (i, 0)) for w, _ in row_outs]
    out_specs += [pl.BlockSpec(shp, lambda i, nd=len(shp): (0,) * nd) for shp in acc_outs]
    out_shape = [jax.ShapeDtypeStruct((s, w), dt) for w, dt in row_outs]
    out_shape += [jax.ShapeDtypeStruct(shp, F32) for shp in acc_outs]
    return _pcall(
        body, name=name, grid=(s // ts,), in_specs=in_specs, out_specs=out_specs, out_shape=out_shape,
        compiler_params=_cparams(("arbitrary",)),
    )(*row_ins, *const_ins)


def _rms(x, gain):
    r = lax.rsqrt(jnp.mean(x * x, axis=-1, keepdims=True) + EPS)
    n = x * r
    return n * gain, n, r


def _rms_bwd(dy, n, r, gain):
    dn = dy * gain
    dx = r * (dn - n * jnp.mean(dn * n, axis=-1, keepdims=True))
    return dx, jnp.sum(dy * n, axis=0, keepdims=True)


def _sigmoid(x):
    return 1.0 / (1.0 + jnp.exp(-x))


_GELU_K = math.sqrt(2.0 / math.pi)


def _gelu(x):
    t = jnp.tanh(_GELU_K * (x + 0.044715 * x * x * x))
    return 0.5 * x * (1.0 + t), t


def _gelu_grad(x, t):
    return 0.5 * (1.0 + t) + 0.5 * x * (1.0 - t * t) * _GELU_K * (1.0 + 3.0 * 0.044715 * x * x)


def _head_sum(x):
    parts = []
    for h in range(HEADS_PER_GROUP):
        sl = x[:, h * HEAD_DIM:(h + 1) * HEAD_DIM]
        parts.append(jnp.broadcast_to(jnp.sum(sl, axis=-1, keepdims=True), sl.shape))
    return jnp.concatenate(parts, axis=-1)


def _mix_weights(l0, l1, l2):
    mx = jnp.maximum(jnp.maximum(l0, l1), l2)
    e0, e1, e2 = jnp.exp(l0 - mx), jnp.exp(l1 - mx), jnp.exp(l2 - mx)
    inv = 1.0 / (e0 + e1 + e2)
    return e0 * inv, e1 * inv, e2 * inv


BLK = 128


def _slopes(g):
    return [2.0 ** (-8.0 * (g * HEADS_PER_GROUP + h + 1) / N_HEADS) for h in range(HEADS_PER_GROUP)]


def _attn_masks(dil):
    qi = lax.broadcasted_iota(jnp.int32, (BLK, BLK), 0)
    ki = lax.broadcasted_iota(jnp.int32, (BLK, BLK), 1)
    dist_c = qi - ki
    dist_p = BLK + qi - ki
    return dist_c >= 0, dist_p <= BLK, (dist_c * dil).astype(F32), (dist_p * dil).astype(F32)


def attn_fwd(qkv, g, name):
    dil, length, _ = qkv.shape
    scale = HEAD_DIM ** -0.5
    slopes = _slopes(g)

    def body(q_ref, kc_ref, vc_ref, kp_ref, vp_ref, o_ref, l_ref):
        n = pl.program_id(1)
        ok_c, ok_p, dc, dp = _attn_masks(dil)
        ok_p = jnp.logical_and(ok_p, n > 0)
        for h in range(HEADS_PER_GROUP):
            sl = slice(h * HEAD_DIM, (h + 1) * HEAD_DIM)
            q = q_ref[:, sl]
            s_c = lax.dot_general(q, kc_ref[:, sl], _DN["nt"], preferred_element_type=F32) * scale - slopes[h] * dc
            s_p = lax.dot_general(q, kp_ref[:, sl], _DN["nt"], preferred_element_type=F32) * scale - slopes[h] * dp
            s_c = jnp.where(ok_c, s_c, NEG)
            s_p = jnp.where(ok_p, s_p, NEG)
            mx = jnp.maximum(jnp.max(s_c, axis=-1, keepdims=True), jnp.max(s_p, axis=-1, keepdims=True))
            p_c = jnp.exp(s_c - mx)
            p_p = jnp.exp(s_p - mx)
            den = jnp.sum(p_c, axis=-1, keepdims=True) + jnp.sum(p_p, axis=-1, keepdims=True)
            acc = jnp.dot(p_c.astype(BF16), vc_ref[:, sl], preferred_element_type=F32)
            acc += jnp.dot(p_p.astype(BF16), vp_ref[:, sl], preferred_element_type=F32)
            o_ref[:, sl] = acc / den
            l_ref[:, sl] = jnp.broadcast_to(mx + jnp.log(den), (BLK, HEAD_DIM))

    def spec(col, prev):
        if prev:
            return pl.BlockSpec((None, BLK, GROUP_W), lambda r, n: (r, jnp.maximum(n - 1, 0), col))
        return pl.BlockSpec((None, BLK, GROUP_W), lambda r, n: (r, n, col))

    out_spec = pl.BlockSpec((None, BLK, GROUP_W), lambda r, n: (r, n, 0))
    return _pcall(
        body, name=name, grid=(dil, length // BLK),
        in_specs=[spec(0, False), spec(1, False), spec(2, False), spec(1, True), spec(2, True)],
        out_specs=[out_spec, out_spec],
        out_shape=[jax.ShapeDtypeStruct((dil, length, GROUP_W), F32)] * 2,
        compiler_params=_cparams(("parallel", "parallel")),
    )(qkv, qkv, qkv, qkv, qkv)


def attn_bwd(qkv, dout, lse, dd, g, name):
    dil, length, _ = qkv.shape
    nblk = length // BLK
    scale = HEAD_DIM ** -0.5
    slopes = _slopes(g)

    def body(q_ref, kc_ref, vc_ref, kp_ref, vp_ref, qn_ref, do_ref, don_ref, l_ref, ln_ref, d_ref, dn_ref, o_ref):
        n = pl.program_id(1)
        ok_c, ok_p, dc, dp = _attn_masks(dil)
        ok_prev = jnp.logical_and(ok_p, n > 0)
        ok_next = jnp.logical_and(ok_p, n < nblk - 1)
        for h in range(HEADS_PER_GROUP):
            sl = slice(h * HEAD_DIM, (h + 1) * HEAD_DIM)
            q, kc, vc, kp, vp, qn = q_ref[:, sl], kc_ref[:, sl], vc_ref[:, sl], kp_ref[:, sl], vp_ref[:, sl], qn_ref[:, sl]
            do, don = do_ref[:, sl], don_ref[:, sl]
            lse_q, lse_n, dd_q, dd_n = l_ref[:, sl], ln_ref[:, sl], d_ref[:, sl], dn_ref[:, sl]

            def probs(qq, kk, dist, ok, lse_t):
                s = lax.dot_general(qq, kk, _DN["nt"], preferred_element_type=F32) * scale - slopes[h] * dist
                return jnp.where(ok, jnp.exp(jnp.where(ok, s, NEG) - lse_t), 0.0)

            p_c = probs(q, kc, dc, ok_c, lse_q)
            p_p = probs(q, kp, dp, ok_prev, lse_q)
            p_x = probs(qn, kc, dp, ok_next, lse_n)
            ds_c = p_c * (lax.dot_general(do, vc, _DN["nt"], preferred_element_type=F32) - dd_q)
            ds_p = p_p * (lax.dot_general(do, vp, _DN["nt"], preferred_element_type=F32) - dd_q)
            ds_x = p_x * (lax.dot_general(don, vc, _DN["nt"], preferred_element_type=F32) - dd_n)
            ds_c16, ds_p16, ds_x16 = ds_c.astype(BF16), ds_p.astype(BF16), ds_x.astype(BF16)
            dq = jnp.dot(ds_c16, kc, preferred_element_type=F32) + jnp.dot(ds_p16, kp, preferred_element_type=F32)
            dk = lax.dot_general(ds_c16, q, _DN["tn"], preferred_element_type=F32)
            dk += lax.dot_general(ds_x16, qn, _DN["tn"], preferred_element_type=F32)
            dv = lax.dot_general(p_c.astype(BF16), do, _DN["tn"], preferred_element_type=F32)
            dv += lax.dot_general(p_x.astype(BF16), don, _DN["tn"], preferred_element_type=F32)
            o_ref[:, h * HEAD_DIM:(h + 1) * HEAD_DIM] = (dq * scale).astype(BF16)
            o_ref[:, GROUP_W + h * HEAD_DIM:GROUP_W + (h + 1) * HEAD_DIM] = (dk * scale).astype(BF16)
            o_ref[:, 2 * GROUP_W + h * HEAD_DIM:2 * GROUP_W + (h + 1) * HEAD_DIM] = dv.astype(BF16)

    def spec(col, which):
        if which == "prev":
            return pl.BlockSpec((None, BLK, GROUP_W), lambda r, n: (r, jnp.maximum(n - 1, 0), col))
        if which == "next":
            return pl.BlockSpec((None, BLK, GROUP_W), lambda r, n: (r, jnp.minimum(n + 1, nblk - 1), col))
        return pl.BlockSpec((None, BLK, GROUP_W), lambda r, n: (r, n, col))

    return _pcall(
        body, name=name, grid=(dil, nblk),
        in_specs=[spec(0, "cur"), spec(1, "cur"), spec(2, "cur"), spec(1, "prev"), spec(2, "prev"), spec(0, "next"),
                  spec(0, "cur"), spec(0, "next"), spec(0, "cur"), spec(0, "next"), spec(0, "cur"), spec(0, "next")],
        out_specs=pl.BlockSpec((None, BLK, 3 * GROUP_W), lambda r, n: (r, n, 0)),
        out_shape=jax.ShapeDtypeStruct((dil, length, 3 * GROUP_W), BF16),
        compiler_params=_cparams(("parallel", "parallel")),
    )(qkv, qkv, qkv, qkv, qkv, qkv, dout, dout, lse, lse, dd, dd)


def _ssm_prep_values(are, aim, logdt):
    dt = jnp.exp(logdt)
    mag = jnp.exp(are * dt)
    lb_re, lb_im = mag * jnp.cos(aim * dt), mag * jnp.sin(aim * dt)
    inv = 1.0 / (are * are + aim * aim)
    n_re, n_im = lb_re - 1.0, lb_im
    f_re = (n_re * are + n_im * aim) * inv
    f_im = (n_im * are - n_re * aim) * inv
    return dt, lb_re, lb_im, f_re, f_im, inv


PREP_G = 8


def _group_specs(are, logdt, bre):
    def spec(a):
        return pl.BlockSpec((PREP_G,) + a.shape[1:], lambda i: (i, 0, 0))
    return spec(are), spec(logdt), spec(bre)


def ssm_prep(are, aim, logdt, bre, bim):
    def body(are_r, aim_r, ldt_r, bre_r, bim_r, lre_o, lim_o, bbre_o, bbim_o):
        _, lb_re, lb_im, f_re, f_im, _ = _ssm_prep_values(are_r[...], aim_r[...], ldt_r[...])
        lre_o[...] = lb_re
        lim_o[...] = lb_im
        bbre_o[...] = f_re * bre_r[...] - f_im * bim_r[...]
        bbim_o[...] = f_re * bim_r[...] + f_im * bre_r[...]

    sh1 = jax.ShapeDtypeStruct(are.shape, F32)
    shb = jax.ShapeDtypeStruct(bre.shape, F32)
    s1, sd, sb = _group_specs(are, logdt, bre)
    return _pcall(body, name="ssm_prep", grid=(SSM_GROUPS // PREP_G,), in_specs=[s1, s1, sd, sb, sb], out_specs=[s1, s1, sb, sb],
                  out_shape=[sh1, sh1, shb, shb], compiler_params=_cparams(("parallel",)))(are, aim, logdt, bre, bim)


def ssm_prep_bwd(are, aim, logdt, bre, bim, dbbre, dbbim, dlre, dlim):
    def body(are_r, aim_r, ldt_r, bre_r, bim_r, dbbre_r, dbbim_r, dlre_r, dlim_r, dare_o, daim_o, dldt_o, dbre_o, dbim_o):
        are_v, aim_v = are_r[...], aim_r[...]
        dt, lb_re, lb_im, f_re, f_im, inv = _ssm_prep_values(are_v, aim_v, ldt_r[...])
        b_re, b_im, g_re, g_im = bre_r[...], bim_r[...], dbbre_r[...], dbbim_r[...]
        dbre_o[...] = f_re * g_re + f_im * g_im
        dbim_o[...] = f_re * g_im - f_im * g_re
        df_re = jnp.sum(b_re * g_re + b_im * g_im, axis=-1, keepdims=True)
        df_im = jnp.sum(b_re * g_im - b_im * g_re, axis=-1, keepdims=True)
        il_re, il_im = are_v * inv, -aim_v * inv
        cl_re = dlre_r[...] + il_re * df_re + il_im * df_im
        cl_im = dlim_r[...] + il_re * df_im - il_im * df_re
        q_re = -(f_re * il_re - f_im * il_im)
        q_im = -(f_re * il_im + f_im * il_re)
        ca_re = q_re * df_re + q_im * df_im
        ca_im = q_re * df_im - q_im * df_re
        cz_re = lb_re * cl_re + lb_im * cl_im
        cz_im = lb_re * cl_im - lb_im * cl_re
        dare_o[...] = ca_re + dt * cz_re
        daim_o[...] = ca_im + dt * cz_im
        dldt_o[...] = dt * jnp.sum(are_v * cz_re + aim_v * cz_im, axis=1, keepdims=True)

    sh1 = jax.ShapeDtypeStruct(are.shape, F32)
    shb = jax.ShapeDtypeStruct(bre.shape, F32)
    s1, sd, sb = _group_specs(are, logdt, bre)
    return _pcall(
        body, name="ssm_prep_bwd", grid=(SSM_GROUPS // PREP_G,), in_specs=[s1, s1, sd, sb, sb, sb, sb, s1, s1],
        out_specs=[s1, s1, sd, sb, sb], out_shape=[sh1, sh1, jax.ShapeDtypeStruct(logdt.shape, F32), shb, shb],
        compiler_params=_cparams(("parallel",)),
    )(are, aim, logdt, bre, bim, dbbre, dbbim, dlre, dlim)


SCAN_WC = 512


def ssm_scan(xre, xim, lre, lim, name, reverse=False, states=None):
    s = xre.shape[0]
    steps = s // SEGS
    assert steps & (steps - 1) == 0
    tt = min(128, steps)
    nch = steps // tt
    rows = tt * SEGS
    with_dl = states is not None
    nsq = int(math.log2(steps))

    def body(*refs):
        xre_r, xim_r, lre_r, lim_r = refs[:4]
        k = 4
        if with_dl:
            hre_r, him_r, pre_r, pim_r, cre_r, cim_r = refs[k:k + 6]
            k += 6
        ore_r, oim_r, hin_re_o, hin_im_o = refs[k:k + 4]
        k += 4
        if with_dl:
            dlre_o, dlim_o = refs[k:k + 2]
            k += 2
        st_re, st_im = refs[k], refs[k + 1]
        ps, ch = pl.program_id(1), pl.program_id(2)
        a_re = jnp.broadcast_to(lre_r[...], (SEGS, SCAN_WC))
        a_im = jnp.broadcast_to(lim_r[...], (SEGS, SCAN_WC))
        if reverse:
            a_im = -a_im

        @pl.when(jnp.logical_and(ps == 0, ch == 0))
        def _():
            st_re[...] = jnp.zeros_like(st_re)
            st_im[...] = jnp.zeros_like(st_im)

        @pl.when(jnp.logical_and(ps == 1, ch == 0))
        def _():
            p_re, p_im = a_re, a_im
            for _ in range(nsq):
                p_re, p_im = p_re * p_re - p_im * p_im, 2.0 * p_re * p_im
            e_re, e_im = st_re[...], st_im[...]
            row = lax.broadcasted_iota(jnp.int32, (SEGS, SCAN_WC), 0)
            edge = (row == SEGS - 1) if reverse else (row == 0)
            c_re, c_im = jnp.zeros_like(e_re), jnp.zeros_like(e_im)
            for _ in range(SEGS - 1):
                n_re = p_re * c_re - p_im * c_im + e_re
                n_im = p_re * c_im + p_im * c_re + e_im
                sh = SEGS - 1 if reverse else 1
                c_re = jnp.where(edge, 0.0, pltpu.roll(n_re, sh, 0))
                c_im = jnp.where(edge, 0.0, pltpu.roll(n_im, sh, 0))
            st_re[...] = c_re
            st_im[...] = c_im
            hin_re_o[...] = c_re
            hin_im_o[...] = c_im
            if with_dl:
                dlre_o[...] = jnp.zeros_like(dlre_o)
                dlim_o[...] = jnp.zeros_like(dlim_o)

        def run(store):
            def step(i, carry):
                if with_dl and store:
                    h_re, h_im, d_re, d_im = carry
                else:
                    h_re, h_im = carry
                t = (tt - 1 - i) if reverse else i
                off = pl.multiple_of(t * SEGS, SEGS)
                n_re = a_re * h_re - a_im * h_im + xre_r[pl.ds(off, SEGS), :]
                n_im = a_re * h_im + a_im * h_re + xim_r[pl.ds(off, SEGS), :]
                if store:
                    ore_r[pl.ds(off, SEGS), :] = n_re
                    oim_r[pl.ds(off, SEGS), :] = n_im
                if with_dl and store:
                    offp = pl.multiple_of(jnp.maximum(t - 1, 0) * SEGS, SEGS)
                    in_re, in_im = hre_r[pl.ds(offp, SEGS), :], him_r[pl.ds(offp, SEGS), :]
                    first_chunk = ch == nch - 1
                    edge_re = jnp.where(first_chunk, cre_r[...], pre_r[...])
                    edge_im = jnp.where(first_chunk, cim_r[...], pim_r[...])
                    hp_re = jnp.where(t == 0, edge_re, in_re)
                    hp_im = jnp.where(t == 0, edge_im, in_im)
                    d_re = d_re + hp_re * n_re + hp_im * n_im
                    d_im = d_im + hp_re * n_im - hp_im * n_re
                    return n_re, n_im, d_re, d_im
                return n_re, n_im

            init = (st_re[...], st_im[...])
            if with_dl and store:
                init = init + (dlre_o[...], dlim_o[...])
            fin = lax.fori_loop(0, tt, step, init)
            st_re[...] = fin[0]
            st_im[...] = fin[1]
            if with_dl and store:
                dlre_o[...] = fin[2]
                dlim_o[...] = fin[3]

        @pl.when(ps == 0)
        def _():
            run(False)

        @pl.when(ps == 1)
        def _():
            run(True)

    def chunk(c):
        return (nch - 1 - c) if reverse else c

    x_spec = pl.BlockSpec((rows, SCAN_WC), lambda j, ps, c: (chunk(c), j))
    l_spec = pl.BlockSpec((1, SCAN_WC), lambda j, ps, c: (0, j))
    o_spec = pl.BlockSpec((rows, SCAN_WC), lambda j, ps, c: (jnp.where(ps == 1, chunk(c), chunk(0)), j))
    e_spec = pl.BlockSpec((SEGS, SCAN_WC), lambda j, ps, c: (0, j))
    in_specs = [x_spec, x_spec, l_spec, l_spec]
    args = [xre, xim, lre, lim]
    out_specs = [o_spec, o_spec, e_spec, e_spec]
    out_shape = [jax.ShapeDtypeStruct((s, STATE_W), F32)] * 2 + [jax.ShapeDtypeStruct((SEGS, STATE_W), F32)] * 2
    if with_dl:
        prev_spec = pl.BlockSpec((SEGS, SCAN_WC), lambda j, ps, c: (jnp.maximum(chunk(c) * tt - 1, 0), j))
        in_specs += [x_spec, x_spec, prev_spec, prev_spec, e_spec, e_spec]
        args += [states[0], states[1], states[0], states[1], states[2], states[3]]
        out_specs += [e_spec, e_spec]
        out_shape += [jax.ShapeDtypeStruct((SEGS, STATE_W), F32)] * 2
    return _pcall(
        body, name=name, grid=(STATE_W // SCAN_WC, 2, nch), in_specs=in_specs, out_specs=out_specs, out_shape=out_shape,
        scratch_shapes=[pltpu.VMEM((SEGS, SCAN_WC), F32)] * 2,
        compiler_params=_cparams(("parallel", "arbitrary", "arbitrary")),
    )(*args)


def _block_diag(m):
    g, r, c = m.shape
    m = m.reshape(BD, g // BD, r, c)
    eye = jnp.eye(g // BD, dtype=m.dtype)
    return jnp.einsum("jarc,ab->jarbc", m, eye).reshape(BD, (g // BD) * r, (g // BD) * c)


def _block_diag_extract(m, r, c):
    per = m.shape[1] // r
    m = m.reshape(BD, per, r, per, c)
    return jnp.einsum("jarac->jarc", m).reshape(BD * per, r, c)


def dilate(a, d):
    s, w = a.shape
    if d == 1:
        return a.reshape(1, s, w)
    return a.reshape(s // d, d, w).transpose(1, 0, 2)


def undilate(a):
    d, length, w = a.shape
    if d == 1:
        return a.reshape(length, w)
    return a.transpose(1, 0, 2).reshape(d * length, w)


def to_segments(a):
    s, w = a.shape
    return a.reshape(SEGS, s // SEGS, w).transpose(1, 0, 2).reshape(s, w)


def from_segments(a):
    s, w = a.shape
    return a.reshape(s // SEGS, SEGS, w).transpose(1, 0, 2).reshape(s, w)


def local_step(x, target, wts, small):
    s = x.shape[0]
    g1, g2, g3, g4 = (small[k].reshape(1, D_MODEL) for k in ("norm_mix_pre", "norm_mix_post", "norm_ffn_pre", "norm_ffn_post"))
    dvec = small["ssm_d"].reshape(1, SSM_W)

    (h,) = rowwise("rms_in", lambda r, c: ([_rms(r[0], c[0])[0]], []), [x], [g1], [(D_MODEL, BF16)])
    hd = [h.reshape(1, s, D_MODEL), dilate(h, 4), dilate(h, 16)]
    qkv = [mm([(hd[g].reshape(s, D_MODEL), wts["qkv"][g])], "nn", BF16, f"mm_qkv{g}") for g in range(3)]
    u = mm([(h, wts["u"])], "nn", F32, "mm_u")
    gates = mm([(h, wts["gates"])], "nn", F32, "mm_gates")

    outs, lses = [], []
    for g, (_, dil) in enumerate(ATTN_GROUPS):
        o, l = attn_fwd(qkv[g].reshape(dil, s // dil, 3 * GROUP_W), g, f"attn_fwd{g}")
        outs.append(undilate(o))
        lses.append(undilate(l))

    def merge_fn(r, c):
        w0, w1, w2 = _mix_weights(r[3], r[4], r[5])
        return [w0 * r[0] + w1 * r[1] + w2 * r[2]], []

    (attn,) = rowwise("attn_merge", merge_fn, outs + lses, [], [(GROUP_W, BF16)])
    attn_branch = mm([(attn, wts["up"])], "nn", F32, "mm_up")

    are3 = small["ssm_a_re"].reshape(SSM_GROUPS, SSM_STATE, 1)
    aim3 = small["ssm_a_im"].reshape(SSM_GROUPS, SSM_STATE, 1)
    ldt3 = small["ssm_log_dt"].reshape(SSM_GROUPS, 1, 1)
    bre3 = small["ssm_b_re"].reshape(SSM_GROUPS, SSM_STATE, SSM_GROUP)
    bim3 = small["ssm_b_im"].reshape(SSM_GROUPS, SSM_STATE, SSM_GROUP)
    cre3 = small["ssm_c_re"].reshape(SSM_GROUPS, SSM_GROUP, SSM_STATE)
    cim3 = small["ssm_c_im"].reshape(SSM_GROUPS, SSM_GROUP, SSM_STATE)
    lre3, lim3, bbre, bbim = ssm_prep(are3, aim3, ldt3, bre3, bim3)
    lre, lim = lre3.reshape(1, STATE_W), lim3.reshape(1, STATE_W)
    w_bre = _block_diag(bbre.transpose(0, 2, 1)).astype(BF16)
    w_bim = _block_diag(bbim.transpose(0, 2, 1)).astype(BF16)
    w_cre = _block_diag(cre3.transpose(0, 2, 1)).astype(BF16)
    w_cim = _block_diag(cim3.transpose(0, 2, 1)).astype(BF16)
    u_s = to_segments(u)
    bu_re = bdmm([(u_s, w_bre)], "ssm_bu_re")
    bu_im = bdmm([(u_s, w_bim)], "ssm_bu_im")
    h_re, h_im, hin_re, hin_im = ssm_scan(bu_re, bu_im, lre, lim, "ssm_scan_fwd")
    y_lin = bdmm([(h_re, w_cre), (h_im, -w_cim)], "ssm_y")

    def gelu_fn(r, c):
        y = r[0] + c[0] * r[1]
        return [_gelu(y)[0], y], []

    yg_s, y_ssm = rowwise("ssm_gelu", gelu_fn, [y_lin, u_s], [dvec], [(SSM_W, BF16), (SSM_W, F32)])
    yg = from_segments(yg_s)
    gv = mm([(yg, wts["glu_v"])], "nn", F32, "mm_glu_v")
    gg = mm([(yg, wts["glu_g"])], "nn", F32, "mm_glu_g")

    def gate_fn(r, c):
        gts, ab, gv_, gg_ = r
        sa, ss = _sigmoid(gts[:, :D_MODEL]), _sigmoid(gts[:, D_MODEL:])
        return [sa * ab + ss * (gv_ * _sigmoid(gg_))], []

    (merged,) = rowwise("gate_merge", gate_fn, [gates, attn_branch, gv, gg], [], [(D_MODEL, BF16)])
    o_mix = mm([(merged, wts["out"])], "nn", F32, "mm_out")

    def mid_fn(r, c):
        x1 = r[0] + _rms(r[1], c[0])[0]
        return [x1, _rms(x1, c[1])[0]], []

    x1, h2 = rowwise("rms_mid", mid_fn, [x, o_mix], [g2, g3], [(D_MODEL, F32), (D_MODEL, BF16)])
    fa = mm([(h2, wts["ffn_gate"])], "nn", F32, "mm_ffn_gate")
    fb = mm([(h2, wts["ffn_up"])], "nn", F32, "mm_ffn_up")
    (fin,) = rowwise("swiglu", lambda r, c: ([r[0] * _sigmoid(r[0]) * r[1]], []), [fa, fb], [], [(D_FF, BF16)])
    f = mm([(fin, wts["ffn_down"])], "nn", F32, "mm_ffn_down")

    def loss_fn(r, c):
        x1_, f_, tgt = r
        y, n, rr = _rms(f_, c[0])
        err = x1_ + y - tgt
        dout = err * (1.0 / D_MODEL)
        df, dg = _rms_bwd(dout, n, rr, c[0])
        lp = 0.5 * jnp.sum(jnp.sum(err * err, axis=-1, keepdims=True) * (1.0 / D_MODEL), axis=0, keepdims=True)
        return [df, dout], [dg, lp]

    df, dout, dg4, loss_part = rowwise("loss_bwd", loss_fn, [x1, f, target], [g4], [(D_MODEL, BF16), (D_MODEL, F32)],
                                       acc_outs=[(1, D_MODEL), (1, 1)])
    dfin = mm([(df, wts["ffn_down"])], "nt", F32, "mm_d_fin")
    dw_ffn_down = mm([(fin, df)], "tn", BF16, "mm_dw_ffn_down")

    def swiglu_bwd(r, c):
        dfin_, a, b = r
        sg = _sigmoid(a)
        return [dfin_ * b * (sg * (1.0 + a * (1.0 - sg))), dfin_ * a * sg], []

    da, db = rowwise("swiglu_bwd", swiglu_bwd, [dfin, fa, fb], [], [(D_FF, BF16), (D_FF, BF16)])
    dh2 = mm([(da, wts["ffn_gate"]), (db, wts["ffn_up"])], "nt", F32, "mm_d_h2")
    dw_ffn_gate = mm([(h2, da)], "tn", BF16, "mm_dw_ffn_gate")
    dw_ffn_up = mm([(h2, db)], "tn", BF16, "mm_dw_ffn_up")

    def mid_bwd(r, c):
        dh2_, dout_, x1_, o_ = r
        _, n3, r3 = _rms(x1_, c[1])
        dx1, dg3_ = _rms_bwd(dh2_, n3, r3, c[1])
        dx1 = dx1 + dout_
        _, n2, r2 = _rms(o_, c[0])
        do_, dg2_ = _rms_bwd(dx1, n2, r2, c[0])
        return [dx1, do_], [dg2_, dg3_]

    dx1, do_mix, dg2, dg3 = rowwise("rms_mid_bwd", mid_bwd, [dh2, dout, x1, o_mix], [g2, g3], [(D_MODEL, F32), (D_MODEL, BF16)],
                                    acc_outs=[(1, D_MODEL), (1, D_MODEL)])
    dmerged = mm([(do_mix, wts["out"])], "nt", F32, "mm_d_merged")
    dw_out = mm([(merged, do_mix)], "tn", BF16, "mm_dw_out")

    def gate_bwd(r, c):
        dm, gts, ab, gv_, gg_ = r
        sa, ss, sg = _sigmoid(gts[:, :D_MODEL]), _sigmoid(gts[:, D_MODEL:]), _sigmoid(gg_)
        branch = gv_ * sg
        dbranch = dm * ss
        dgates = jnp.concatenate([dm * ab * sa * (1.0 - sa), dm * branch * ss * (1.0 - ss)], axis=-1)
        return [dgates, dm * sa, dbranch * sg, dbranch * gv_ * sg * (1.0 - sg)], []

    dgates, dab, dgv, dgg = rowwise("gate_bwd", gate_bwd, [dmerged, gates, attn_branch, gv, gg], [],
                                    [(2 * D_MODEL, BF16), (D_MODEL, BF16), (D_MODEL, BF16), (D_MODEL, BF16)])
    dattn = mm([(dab, wts["up"])], "nt", F32, "mm_d_attn")
    dw_up = mm([(attn, dab)], "tn", BF16, "mm_dw_up")
    dyg = mm([(dgv, wts["glu_v"]), (dgg, wts["glu_g"])], "nt", F32, "mm_d_yg")
    dw_glu_v = mm([(yg, dgv)], "tn", BF16, "mm_dw_glu_v")
    dw_glu_g = mm([(yg, dgg)], "tn", BF16, "mm_dw_glu_g")

    def gelu_bwd(r, c):
        dyg_, y, us = r
        dy = dyg_ * _gelu_grad(y, _gelu(y)[1])
        return [dy, c[0] * dy], [jnp.sum(dy * us, axis=0, keepdims=True)]

    dy_ssm, du_skip, dd_ssm = rowwise("ssm_gelu_bwd", gelu_bwd, [to_segments(dyg), y_ssm, u_s], [dvec],
                                      [(SSM_W, F32), (SSM_W, F32)], acc_outs=[(1, SSM_W)])
    w_cre_t, w_cim_t = w_cre.transpose(0, 2, 1), w_cim.transpose(0, 2, 1)
    gin_re = bdmm([(dy_ssm, w_cre_t)], "ssm_gin_re")
    gin_im = bdmm([(dy_ssm, -w_cim_t)], "ssm_gin_im")
    g_re, g_im, _, _, dl_re8, dl_im8 = ssm_scan(gin_re, gin_im, lre, lim, "ssm_scan_bwd", reverse=True,
                                                 states=(h_re, h_im, hin_re, hin_im))
    du_s = bdmm([(g_re, w_bre.transpose(0, 2, 1)), (g_im, w_bim.transpose(0, 2, 1))], "ssm_du", add=du_skip)
    dbb_re = _block_diag_extract(bd_tn(u_s, g_re, "ssm_dbb_re"), SSM_GROUP, SSM_STATE).transpose(0, 2, 1)
    dbb_im = _block_diag_extract(bd_tn(u_s, g_im, "ssm_dbb_im"), SSM_GROUP, SSM_STATE).transpose(0, 2, 1)
    dc_re = _block_diag_extract(bd_tn(h_re, dy_ssm, "ssm_dc_re"), SSM_STATE, SSM_GROUP).transpose(0, 2, 1)
    dc_im = -_block_diag_extract(bd_tn(h_im, dy_ssm, "ssm_dc_im"), SSM_STATE, SSM_GROUP).transpose(0, 2, 1)

    def fold8(r, c):
        return [], [jnp.sum(r[0], axis=0, keepdims=True), jnp.sum(r[1], axis=0, keepdims=True)]

    dl_re, dl_im = rowwise("ssm_dl_fold", fold8, [dl_re8, dl_im8], [], [], acc_outs=[(1, STATE_W), (1, STATE_W)], ts=SEGS)
    da_re, da_im, dldt, db_re, db_im = ssm_prep_bwd(
        are3, aim3, ldt3, bre3, bim3, dbb_re, dbb_im,
        dl_re.reshape(SSM_GROUPS, SSM_STATE, 1), dl_im.reshape(SSM_GROUPS, SSM_STATE, 1))
    du = from_segments(du_s)

    def merge_bwd(r, c):
        dat, o0, o1, o2, l0, l1, l2 = r
        w0, w1, w2 = _mix_weights(l0, l1, l2)
        tot = _head_sum(dat * (w0 * o0 + w1 * o1 + w2 * o2))
        return [w0 * dat, w1 * dat, w2 * dat, w0 * tot, w1 * tot, w2 * tot], []

    mb = rowwise("attn_merge_bwd", merge_bwd, [dattn] + outs + lses, [], [(GROUP_W, BF16)] * 3 + [(GROUP_W, F32)] * 3)
    dh_parts = []
    dw_qkv = []
    for g, (_, dil) in enumerate(ATTN_GROUPS):
        dq = attn_bwd(qkv[g].reshape(dil, s // dil, 3 * GROUP_W), dilate(mb[g], dil), dilate(lses[g], dil),
                      dilate(mb[3 + g], dil), g, f"attn_bwd{g}").reshape(s, 3 * GROUP_W)
        dh_g = mm([(dq, wts["qkv"][g])], "nt", F32, f"mm_d_h_qkv{g}")
        dh_parts.append(undilate(dh_g.reshape(dil, s // dil, D_MODEL)))
        dw_qkv.append(mm([(hd[g].reshape(s, D_MODEL), dq)], "tn", BF16, f"mm_dw_qkv{g}"))
    dh_parts.append(mm([(du, wts["u"])], "nt", F32, "mm_d_h_u"))
    dh_parts.append(mm([(dgates, wts["gates"])], "nt", F32, "mm_d_h_gates"))
    dw_u = mm([(h, du)], "tn", BF16, "mm_dw_u")
    dw_gates = mm([(h, dgates)], "tn", BF16, "mm_dw_gates")

    def in_bwd(r, c):
        dh = r[0] + r[1] + r[2] + r[3] + r[4]
        _, n1, r1 = _rms(r[6], c[0])
        dx, dg1_ = _rms_bwd(dh, n1, r1, c[0])
        return [dx + r[5]], [dg1_]

    grad_x, dg1 = rowwise("rms_in_bwd", in_bwd, dh_parts + [dx1, x], [g1], [(D_MODEL, F32)], acc_outs=[(1, D_MODEL)])

    dw = dict(qkv=dw_qkv, u=dw_u, gates=dw_gates, up=dw_up, glu_v=dw_glu_v, glu_g=dw_glu_g, out=dw_out,
              ffn_gate=dw_ffn_gate, ffn_up=dw_ffn_up, ffn_down=dw_ffn_down)
    dsmall = dict(norm_mix_pre=dg1, ssm_a_re=da_re, ssm_a_im=da_im, ssm_log_dt=dldt, ssm_b_re=db_re, ssm_b_im=db_im,
                  ssm_c_re=dc_re, ssm_c_im=dc_im, ssm_d=dd_ssm, norm_mix_post=dg2, norm_ffn_pre=dg3, norm_ffn_post=dg4)
    return loss_part, grad_x, dw, dsmall


def _my_coords():
    return lax.axis_index("x"), lax.axis_index("y"), lax.axis_index("c")


def all_gather_rows(x, name):
    r, c = x.shape

    def body(x_ref, out_ref, send_sems, recv_sems, local_sem):
        mx, my, mc = _my_coords()
        me, sibling = (mx, my, mc), (mx, my, 1 - mc)
        chips = [(1 - mx, my), (mx, 1 - my), (1 - mx, 1 - my)]

        def slot(px, py, pc):
            return out_ref.at[4 * px + 2 * py + pc]

        def copy(k, block, to, src=None):
            return pltpu.make_async_remote_copy(
                src_ref=slot(*block) if src is None else src, dst_ref=slot(*block),
                send_sem=send_sems.at[k], recv_sem=recv_sems.at[k], device_id=to, device_id_type=MESH_ID)

        mine = pltpu.make_async_copy(x_ref, slot(*me), local_sem)
        mine.start()
        first = [copy(0, me, sibling, src=x_ref)]
        first += [copy(1 + j, me, (*chip, mc), src=x_ref) for j, chip in enumerate(chips)]
        for cp in first:
            cp.start()
        passed = [copy(4 + j, (*chip, mc), sibling) for j, chip in enumerate(chips)]
        for j, chip in enumerate(chips):
            copy(1 + j, (*chip, mc), me).wait_recv()
            passed[j].start()
        copy(0, sibling, me).wait_recv()
        for j, chip in enumerate(chips):
            copy(4 + j, (*chip, 1 - mc), me).wait_recv()
        for cp in first + passed:
            cp.wait_send()
        mine.wait()

    return _pcall(
        body, name=name, out_shape=jax.ShapeDtypeStruct((N_DEV, r, c), x.dtype), in_specs=[HBM_SPEC], out_specs=HBM_SPEC,
        scratch_shapes=[pltpu.SemaphoreType.DMA((7,)), pltpu.SemaphoreType.DMA((7,)), pltpu.SemaphoreType.DMA],
    )(x)


def all_to_all_rows(p, name):
    _, r, c = p.shape

    def body(p_ref, out_ref, send_sems, recv_sems, local_sem):
        mx, my, mc = _my_coords()
        me = 4 * mx + 2 * my + mc
        mine = pltpu.make_async_copy(p_ref.at[me], out_ref.at[me], local_sem)
        mine.start()
        copies = []
        for k in range(1, N_DEV):
            px, py, pc = mx ^ ((k >> 2) & 1), my ^ ((k >> 1) & 1), mc ^ (k & 1)
            copies.append(pltpu.make_async_remote_copy(
                src_ref=p_ref.at[4 * px + 2 * py + pc], dst_ref=out_ref.at[me],
                send_sem=send_sems.at[k - 1], recv_sem=recv_sems.at[k - 1], device_id=(px, py, pc), device_id_type=MESH_ID))
        for cp in copies:
            cp.start()
        for cp in copies:
            cp.wait_recv()
        for cp in copies:
            cp.wait_send()
        mine.wait()

    return _pcall(
        body, name=name, out_shape=jax.ShapeDtypeStruct(p.shape, p.dtype), in_specs=[HBM_SPEC], out_specs=HBM_SPEC,
        scratch_shapes=[pltpu.SemaphoreType.DMA((7,)), pltpu.SemaphoreType.DMA((7,)), pltpu.SemaphoreType.DMA],
    )(p)


def adamw(parts, w, m, v, name):
    r, c = w.shape
    tr = r
    while tr > 8 and tr % 2 == 0 and tr * c * (8 * parts.dtype.itemsize + 28) * 2 > 24 * 1024 * 1024:
        tr //= 2
    assert r % tr == 0 and (tr % 8 == 0 or tr == r)
    c1, c2 = 1.0 / (1.0 - ADAM_B1 ** ADAM_STEP), 1.0 / (1.0 - ADAM_B2 ** ADAM_STEP)

    def body(p_ref, w_ref, m_ref, v_ref, g_o, d_o, m_o, v_o):
        g = p_ref[0].astype(F32)
        for i in range(1, N_DEV):
            g = g + p_ref[i].astype(F32)
        mn = ADAM_B1 * m_ref[...] + (1.0 - ADAM_B1) * g
        vn = ADAM_B2 * v_ref[...] + (1.0 - ADAM_B2) * (g * g)
        g_o[...] = g
        m_o[...] = mn
        v_o[...] = vn
        d_o[...] = -ADAM_LR * ((mn * c1) / (jnp.sqrt(vn * c2) + ADAM_EPS) + ADAM_WD * w_ref[...])

    blk = pl.BlockSpec((tr, c), lambda i: (i, 0))
    return _pcall(
        body, name=name, grid=(r // tr,), in_specs=[pl.BlockSpec((N_DEV, tr, c), lambda i: (0, i, 0)), blk, blk, blk],
        out_specs=[blk] * 4, out_shape=[jax.ShapeDtypeStruct((r, c), F32)] * 4, compiler_params=_cparams(("parallel",)),
    )(parts, w, m, v)


PACK_C = 1024
SHARDED = ("w_in", "w_attn_up", "w_glu_v", "w_glu_g", "w_out", "w_ffn_gate", "w_ffn_up", "w_ffn_down")
ROW_SHARDED = ("w_out", "w_ffn_down")
SMALL = ("norm_mix_pre", "ssm_a_re", "ssm_a_im", "ssm_log_dt", "ssm_b_re", "ssm_b_im", "ssm_c_re", "ssm_c_im", "ssm_d",
         "norm_mix_post", "norm_ffn_pre", "norm_ffn_post")
WEIGHTS = ("norm_mix_pre", "w_in", "w_attn_up", "ssm_a_re", "ssm_a_im", "ssm_log_dt", "ssm_b_re", "ssm_b_im", "ssm_c_re",
           "ssm_c_im", "ssm_d", "w_glu_v", "w_glu_g", "w_out", "norm_mix_post", "norm_ffn_pre", "w_ffn_gate", "w_ffn_up",
           "w_ffn_down", "norm_ffn_post")


def _pack(arrs, dtype, pad_rows_to=64):
    flat = jnp.concatenate([a.reshape(-1).astype(dtype) for a in arrs])
    n = flat.shape[0]
    rows = -(-n // PACK_C)
    rows = -(-rows // pad_rows_to) * pad_rows_to
    return jnp.pad(flat, (0, rows * PACK_C - n)).reshape(rows, PACK_C)


def _unpack(flat2d, shapes):
    flat = flat2d.reshape(-1)
    out, off = [], 0
    for shp in shapes:
        n = int(np.prod(shp))
        out.append(flat[off:off + n].reshape(shp))
        off += n
    return out


def _full_from_gathered(gathered, name):
    if name in ROW_SHARDED:
        return gathered.reshape(-1, gathered.shape[2])
    return gathered.transpose(1, 0, 2).reshape(gathered.shape[1], -1)


def _split_for_devices(full, name):
    if name in ROW_SHARDED:
        return full.reshape(N_DEV, -1, full.shape[1])
    return full.reshape(full.shape[0], N_DEV, -1).transpose(1, 0, 2)


def kernel(x, norm_mix_pre, w_in, w_attn_up, ssm_a_re, ssm_a_im, ssm_log_dt, ssm_b_re, ssm_b_im, ssm_c_re, ssm_c_im, ssm_d, w_glu_v, w_glu_g, w_out, norm_mix_post, norm_ffn_pre, w_ffn_gate, w_ffn_up, w_ffn_down, norm_ffn_post, loss_target, m_norm_mix_pre, m_w_in, m_w_attn_up, m_ssm_a_re, m_ssm_a_im, m_ssm_log_dt, m_ssm_b_re, m_ssm_b_im, m_ssm_c_re, m_ssm_c_im, m_ssm_d, m_w_glu_v, m_w_glu_g, m_w_out, m_norm_mix_post, m_norm_ffn_pre, m_w_ffn_gate, m_w_ffn_up, m_w_ffn_down, m_norm_ffn_post, v_norm_mix_pre, v_w_in, v_w_attn_up, v_ssm_a_re, v_ssm_a_im, v_ssm_log_dt, v_ssm_b_re, v_ssm_b_im, v_ssm_c_re, v_ssm_c_im, v_ssm_d, v_w_glu_v, v_w_glu_g, v_w_out, v_norm_mix_post, v_norm_ffn_pre, v_w_ffn_gate, v_w_ffn_up, v_w_ffn_down, v_norm_ffn_post):
    args = dict(locals())
    wv = {n: args[n][0] for n in WEIGHTS}
    mv = {n: args["m_" + n][0] for n in WEIGHTS}
    vv = {n: args["v_" + n][0] for n in WEIGHTS}
    shard_shapes = [wv[n].shape for n in SHARDED]

    gathered = all_gather_rows(_pack([wv[n] for n in SHARDED], BF16), "gather_weights")
    per_dev = [_unpack(gathered[i], shard_shapes) for i in range(N_DEV)]
    full = {n: _full_from_gathered(jnp.stack([per_dev[i][k] for i in range(N_DEV)]), n) for k, n in enumerate(SHARDED)}
    w_in_f = full["w_in"]
    wts = dict(
        qkv=[jnp.concatenate([w_in_f[:, o + g * GROUP_W:o + (g + 1) * GROUP_W] for o in (0, HQ, 2 * HQ)], axis=1) for g in range(3)],
        u=w_in_f[:, 3 * HQ:3 * HQ + SSM_W], gates=w_in_f[:, 3 * HQ + SSM_W:],
        up=full["w_attn_up"], glu_v=full["w_glu_v"], glu_g=full["w_glu_g"], out=full["w_out"],
        ffn_gate=full["w_ffn_gate"], ffn_up=full["w_ffn_up"], ffn_down=full["w_ffn_down"])
    small = {n: wv[n] for n in SMALL}

    loss_part, grad_x, dw, dsmall = local_step(x[0], loss_target[0], wts, small)

    dq = dw["qkv"]
    dw_in = jnp.concatenate(
        [dq[g][:, o * GROUP_W:(o + 1) * GROUP_W] for o in range(3) for g in range(3)] + [dw["u"], dw["gates"]], axis=1)
    dfull = dict(w_in=dw_in, w_attn_up=dw["up"], w_glu_v=dw["glu_v"], w_glu_g=dw["glu_g"], w_out=dw["out"],
                 w_ffn_gate=dw["ffn_gate"], w_ffn_up=dw["ffn_up"], w_ffn_down=dw["ffn_down"])
    split = {n: _split_for_devices(dfull[n], n) for n in SHARDED}
    send = jnp.stack([_pack([split[n][j] for n in SHARDED], BF16) for j in range(N_DEV)])
    recv = all_to_all_rows(send, "scatter_weight_grads")
    parts_dev = [_unpack(recv[i], shard_shapes) for i in range(N_DEV)]

    res = {}
    for k, n in enumerate(SHARDED):
        parts = jnp.stack([parts_dev[i][k] for i in range(N_DEV)])
        res[n] = adamw(parts, wv[n], mv[n], vv[n], "adamw_" + n)

    small_shapes = [wv[n].shape for n in SMALL]
    sgather = all_gather_rows(_pack([dsmall[n] for n in SMALL], F32), "gather_small_grads")
    sres = adamw(sgather, _pack([wv[n] for n in SMALL], F32), _pack([mv[n] for n in SMALL], F32),
                 _pack([vv[n] for n in SMALL], F32), "adamw_small")
    sun = [_unpack(t, small_shapes) for t in sres]
    for k, n in enumerate(SMALL):
        res[n] = tuple(sun[t][k] for t in range(4))

    loss = lax.psum(loss_part[0, 0], ("x", "y", "c"))
    outs = [loss, grad_x[None]]
    for t in range(4):
        outs += [res[n][t][None] for n in WEIGHTS]
    return tuple(outs)
```

```python
import functools
import math

import numpy as np
import jax
import jax.numpy as jnp
from jax import lax
from jax.experimental import pallas as pl
from jax.experimental.pallas import tpu as pltpu

F32 = jnp.float32
BF16 = jnp.bfloat16

D_MODEL = 2048
HEAD_DIM = 128
HEADS_PER_GROUP = 4
ATTN_GROUPS = ((128, 1), (512, 4), (2048, 16))
N_HEADS = HEADS_PER_GROUP * len(ATTN_GROUPS)
GROUP_W = HEADS_PER_GROUP * HEAD_DIM
HQ = N_HEADS * HEAD_DIM
SSM_W = 1024
SSM_GROUP = 16
SSM_GROUPS = 64
SSM_STATE = 64
STATE_W = SSM_GROUPS * SSM_STATE
D_FF = 5632
EPS = 1e-6
N_DEV = 8
SEGS = 8
BD = 8

ADAM_LR, ADAM_B1, ADAM_B2, ADAM_EPS, ADAM_WD, ADAM_STEP = 0.001, 0.9, 0.999, 1e-08, 0.01, 10

VMEM_LIMIT = 56 * 1024 * 1024
HBM_SPEC = pl.BlockSpec(memory_space=pltpu.HBM)
MESH_ID = pl.DeviceIdType.MESH
NEG = -1e30


def _pcall(body, **kw):
    return pl.pallas_call(body, **kw)


def _cparams(sem=None):
    if sem is None:
        return pltpu.CompilerParams(vmem_limit_bytes=VMEM_LIMIT)
    return pltpu.CompilerParams(vmem_limit_bytes=VMEM_LIMIT, dimension_semantics=sem)


_DN = {"nn": (((1,), (0,)), ((), ())), "nt": (((1,), (1,)), ((), ())), "tn": (((0,), (0,)), ((), ()))}


LANE = 128
MM_TM, MM_TN, MM_TK = 1024, 1536, 2048


def _tile(n, cap):
    for t in range(min(cap, n) // LANE * LANE, 0, -LANE):
        if n % t == 0:
            return t
    raise ValueError(n)


def mm(pairs, mode, out_dtype, name, tm=MM_TM, tn=MM_TN, tk=MM_TK):
    a0, b0 = pairs[0]
    if mode == "nn":
        (m, k), n = a0.shape, b0.shape[1]
    elif mode == "nt":
        (m, k), n = a0.shape, b0.shape[0]
    else:
        (k, m), n = a0.shape, b0.shape[1]
    tm, tn, tk = _tile(m, tm), _tile(n, tn), _tile(k, tk)
    nk = k // tk
    npairs = len(pairs)

    def body(*refs):
        o_ref = refs[2 * npairs]
        tot = None
        for p in range(npairs):
            a = refs[2 * p][...].astype(BF16)
            b = refs[2 * p + 1][...].astype(BF16)
            d = lax.dot_general(a, b, _DN[mode], preferred_element_type=F32)
            tot = d if tot is None else tot + d
        if nk == 1:
            o_ref[...] = tot.astype(o_ref.dtype)
            return
        acc = refs[2 * npairs + 1]
        kk = pl.program_id(2)

        @pl.when(kk == 0)
        def _():
            acc[...] = tot

        @pl.when(kk > 0)
        def _():
            acc[...] += tot

        @pl.when(kk == nk - 1)
        def _():
            o_ref[...] = acc[...].astype(o_ref.dtype)

    if mode == "nn":
        sp = [pl.BlockSpec((tm, tk), lambda i, j, kk: (i, kk)), pl.BlockSpec((tk, tn), lambda i, j, kk: (kk, j))]
    elif mode == "nt":
        sp = [pl.BlockSpec((tm, tk), lambda i, j, kk: (i, kk)), pl.BlockSpec((tn, tk), lambda i, j, kk: (j, kk))]
    else:
        sp = [pl.BlockSpec((tk, tm), lambda i, j, kk: (kk, i)), pl.BlockSpec((tk, tn), lambda i, j, kk: (kk, j))]
    return _pcall(
        body, name=name, grid=(m // tm, n // tn, nk), in_specs=sp * npairs,
        out_specs=pl.BlockSpec((tm, tn), lambda i, j, kk: (i, j)),
        out_shape=jax.ShapeDtypeStruct((m, n), out_dtype),
        scratch_shapes=[pltpu.VMEM((tm, tn), F32)] if nk > 1 else [],
        compiler_params=_cparams(("parallel", "parallel", "arbitrary")),
    )(*[t for pr in pairs for t in pr])


def bdmm(pairs, name, add=None, ts=1024):
    a0, w0 = pairs[0]
    s = a0.shape[0]
    ka, kn = w0.shape[1], w0.shape[2]
    ts = min(ts, s)
    npairs = len(pairs)

    def body(*refs):
        o_ref = refs[-1]
        tot = None
        for p in range(npairs):
            d = jnp.dot(refs[2 * p][...].astype(BF16), refs[2 * p + 1][...].astype(BF16), preferred_element_type=F32)
            tot = d if tot is None else tot + d
        if add is not None:
            tot = tot + refs[2 * npairs][...]
        o_ref[...] = tot

    sp = []
    args = []
    for a, w in pairs:
        sp += [pl.BlockSpec((ts, w.shape[1]), lambda i, j: (i, j)), pl.BlockSpec((None, w.shape[1], kn), lambda i, j: (j, 0, 0))]
        args += [a, w]
    if add is not None:
        sp.append(pl.BlockSpec((ts, kn), lambda i, j: (i, j)))
        args.append(add)
    return _pcall(
        body, name=name, grid=(s // ts, BD), in_specs=sp, out_specs=pl.BlockSpec((ts, kn), lambda i, j: (i, j)),
        out_shape=jax.ShapeDtypeStruct((s, BD * kn), F32), compiler_params=_cparams(("parallel", "parallel")),
    )(*args)


def bd_tn(a, b, name, ts=512):
    s = a.shape[0]
    ka, kb = a.shape[1] // BD, b.shape[1] // BD
    ts = min(ts, s)
    ns = s // ts

    def body(a_ref, b_ref, o_ref, acc):
        kk = pl.program_id(1)

        @pl.when(kk == 0)
        def _():
            acc[...] = jnp.zeros_like(acc)

        acc[...] += lax.dot_general(a_ref[...].astype(BF16), b_ref[...].astype(BF16), _DN["tn"], preferred_element_type=F32)

        @pl.when(kk == ns - 1)
        def _():
            o_ref[...] = acc[...]

    return _pcall(
        body, name=name, grid=(BD, ns),
        in_specs=[pl.BlockSpec((ts, ka), lambda j, kk: (kk, j)), pl.BlockSpec((ts, kb), lambda j, kk: (kk, j))],
        out_specs=pl.BlockSpec((None, ka, kb), lambda j, kk: (j, 0, 0)),
        out_shape=jax.ShapeDtypeStruct((BD, ka, kb), F32), scratch_shapes=[pltpu.VMEM((ka, kb), F32)],
        compiler_params=_cparams(("parallel", "arbitrary")),
    )(a, b)


def rowwise(name, fn, row_ins, const_ins, row_outs, acc_outs=(), ts=None):
    s = row_ins[0].shape[0]
    if ts is None:
        per_row = sum(a.shape[1] * a.dtype.itemsize for a in row_ins) + sum(w * jnp.dtype(dt).itemsize for w, dt in row_outs)
        ts = 512
        while ts > 8 and 2 * ts * per_row > 20 * 1024 * 1024:
            ts //= 2
    ts = min(ts, s)
    assert s % ts == 0
    nr, nc, no, na = len(row_ins), len(const_ins), len(row_outs), len(acc_outs)

    def body(*refs):
        rows = [r[...] for r in refs[:nr]]
        consts = [r[...] for r in refs[nr:nr + nc]]
        outs, accs = fn(rows, consts)
        for r, v in zip(refs[nr + nc:nr + nc + no], outs):
            r[...] = v.astype(r.dtype)
        if na:
            first = pl.program_id(0) == 0
            for r, v in zip(refs[nr + nc + no:], accs):
                @pl.when(first)
                def _(r=r, v=v):
                    r[...] = v

                @pl.when(jnp.logical_not(first))
                def _(r=r, v=v):
                    r[...] += v

    in_specs = [pl.BlockSpec((ts, a.shape[1]), lambda i: (i, 0)) for a in row_ins]
    in_specs += [pl.BlockSpec(c.shape, lambda i, nd=c.ndim: (0,) * nd) for c in const_ins]
    out_specs = [pl.BlockSpec((ts, w), lambda i: (i, 0)) for w, _ in row_outs]
    out_specs += [pl.BlockSpec(shp, lambda i, nd=len(shp): (0,) * nd) for shp in acc_outs]
    out_shape = [jax.ShapeDtypeStruct((s, w), dt) for w, dt in row_outs]
    out_shape += [jax.ShapeDtypeStruct(shp, F32) for shp in acc_outs]
    return _pcall(
        body, name=name, grid=(s // ts,), in_specs=in_specs, out_specs=out_specs, out_shape=out_shape,
        compiler_params=_cparams(("arbitrary",)),
    )(*row_ins, *const_ins)


def _rms(x, gain):
    r = lax.rsqrt(jnp.mean(x * x, axis=-1, keepdims=True) + EPS)
    n = x * r
    return n * gain, n, r


def _rms_bwd(dy, n, r, gain):
    dn = dy * gain
    dx = r * (dn - n * jnp.mean(dn * n, axis=-1, keepdims=True))
    return dx, jnp.sum(dy * n, axis=0, keepdims=True)


def _sigmoid(x):
    return 1.0 / (1.0 + jnp.exp(-x))


_GELU_K = math.sqrt(2.0 / math.pi)


def _gelu(x):
    t = jnp.tanh(_GELU_K * (x + 0.044715 * x * x * x))
    return 0.5 * x * (1.0 + t), t


def _gelu_grad(x, t):
    return 0.5 * (1.0 + t) + 0.5 * x * (1.0 - t * t) * _GELU_K * (1.0 + 3.0 * 0.044715 * x * x)


def _head_sum(x):
    parts = []
    for h in range(HEADS_PER_GROUP):
        sl = x[:, h * HEAD_DIM:(h + 1) * HEAD_DIM]
        parts.append(jnp.broadcast_to(jnp.sum(sl, axis=-1, keepdims=True), sl.shape))
    return jnp.concatenate(parts, axis=-1)


def _mix_weights(l0, l1, l2):
    mx = jnp.maximum(jnp.maximum(l0, l1), l2)
    e0, e1, e2 = jnp.exp(l0 - mx), jnp.exp(l1 - mx), jnp.exp(l2 - mx)
    inv = 1.0 / (e0 + e1 + e2)
    return e0 * inv, e1 * inv, e2 * inv


BLK = 128


def _slopes(g):
    return [2.0 ** (-8.0 * (g * HEADS_PER_GROUP + h + 1) / N_HEADS) for h in range(HEADS_PER_GROUP)]


def _attn_masks(dil):
    qi = lax.broadcasted_iota(jnp.int32, (BLK, BLK), 0)
    ki = lax.broadcasted_iota(jnp.int32, (BLK, BLK), 1)
    dist_c = qi - ki
    dist_p = BLK + qi - ki
    return dist_c >= 0, dist_p <= BLK, (dist_c * dil).astype(F32), (dist_p * dil).astype(F32)


def attn_fwd(qkv, g, name):
    dil, length, _ = qkv.shape
    scale = HEAD_DIM ** -0.5
    slopes = _slopes(g)

    def body(q_ref, kc_ref, vc_ref, kp_ref, vp_ref, o_ref, l_ref):
        n = pl.program_id(1)
        ok_c, ok_p, dc, dp = _attn_masks(dil)
        ok_p = jnp.logical_and(ok_p, n > 0)
        for h in range(HEADS_PER_GROUP):
            sl = slice(h * HEAD_DIM, (h + 1) * HEAD_DIM)
            q = q_ref[:, sl]
            s_c = lax.dot_general(q, kc_ref[:, sl], _DN["nt"], preferred_element_type=F32) * scale - slopes[h] * dc
            s_p = lax.dot_general(q, kp_ref[:, sl], _DN["nt"], preferred_element_type=F32) * scale - slopes[h] * dp
            s_c = jnp.where(ok_c, s_c, NEG)
            s_p = jnp.where(ok_p, s_p, NEG)
            mx = jnp.maximum(jnp.max(s_c, axis=-1, keepdims=True), jnp.max(s_p, axis=-1, keepdims=True))
            p_c = jnp.exp(s_c - mx)
            p_p = jnp.exp(s_p - mx)
            den = jnp.sum(p_c, axis=-1, keepdims=True) + jnp.sum(p_p, axis=-1, keepdims=True)
            acc = jnp.dot(p_c.astype(BF16), vc_ref[:, sl], preferred_element_type=F32)
            acc += jnp.dot(p_p.astype(BF16), vp_ref[:, sl], preferred_element_type=F32)
            o_ref[:, sl] = acc / den
            l_ref[:, sl] = jnp.broadcast_to(mx + jnp.log(den), (BLK, HEAD_DIM))

    def spec(col, prev):
        if prev:
            return pl.BlockSpec((None, BLK, GROUP_W), lambda r, n: (r, jnp.maximum(n - 1, 0), col))
        return pl.BlockSpec((None, BLK, GROUP_W), lambda r, n: (r, n, col))

    out_spec = pl.BlockSpec((None, BLK, GROUP_W), lambda r, n: (r, n, 0))
    return _pcall(
        body, name=name, grid=(dil, length // BLK),
        in_specs=[spec(0, False), spec(1, False), spec(2, False), spec(1, True), spec(2, True)],
        out_specs=[out_spec, out_spec],
        out_shape=[jax.ShapeDtypeStruct((dil, length, GROUP_W), F32)] * 2,
        compiler_params=_cparams(("parallel", "parallel")),
    )(qkv, qkv, qkv, qkv, qkv)


def attn_bwd(qkv, dout, lse, dd, g, name):
    dil, length, _ = qkv.shape
    nblk = length // BLK
    scale = HEAD_DIM ** -0.5
    slopes = _slopes(g)

    def body(q_ref, kc_ref, vc_ref, kp_ref, vp_ref, qn_ref, do_ref, don_ref, l_ref, ln_ref, d_ref, dn_ref, o_ref):
        n = pl.program_id(1)
        ok_c, ok_p, dc, dp = _attn_masks(dil)
        ok_prev = jnp.logical_and(ok_p, n > 0)
        ok_next = jnp.logical_and(ok_p, n < nblk - 1)
        for h in range(HEADS_PER_GROUP):
            sl = slice(h * HEAD_DIM, (h + 1) * HEAD_DIM)
            q, kc, vc, kp, vp, qn = q_ref[:, sl], kc_ref[:, sl], vc_ref[:, sl], kp_ref[:, sl], vp_ref[:, sl], qn_ref[:, sl]
            do, don = do_ref[:, sl], don_ref[:, sl]
            lse_q, lse_n, dd_q, dd_n = l_ref[:, sl], ln_ref[:, sl], d_ref[:, sl], dn_ref[:, sl]

            def probs(qq, kk, dist, ok, lse_t):
                s = lax.dot_general(qq, kk, _DN["nt"], preferred_element_type=F32) * scale - slopes[h] * dist
                return jnp.where(ok, jnp.exp(jnp.where(ok, s, NEG) - lse_t), 0.0)

            p_c = probs(q, kc, dc, ok_c, lse_q)
            p_p = probs(q, kp, dp, ok_prev, lse_q)
            p_x = probs(qn, kc, dp, ok_next, lse_n)
            ds_c = p_c * (lax.dot_general(do, vc, _DN["nt"], preferred_element_type=F32) - dd_q)
            ds_p = p_p * (lax.dot_general(do, vp, _DN["nt"], preferred_element_type=F32) - dd_q)
            ds_x = p_x * (lax.dot_general(don, vc, _DN["nt"], preferred_element_type=F32) - dd_n)
            ds_c16, ds_p16, ds_x16 = ds_c.astype(BF16), ds_p.astype(BF16), ds_x.astype(BF16)
            dq = jnp.dot(ds_c16, kc, preferred_element_type=F32) + jnp.dot(ds_p16, kp, preferred_element_type=F32)
            dk = lax.dot_general(ds_c16, q, _DN["tn"], preferred_element_type=F32)
            dk += lax.dot_general(ds_x16, qn, _DN["tn"], preferred_element_type=F32)
            dv = lax.dot_general(p_c.astype(BF16), do, _DN["tn"], preferred_element_type=F32)
            dv += lax.dot_general(p_x.astype(BF16), don, _DN["tn"], preferred_element_type=F32)
            o_ref[:, h * HEAD_DIM:(h + 1) * HEAD_DIM] = (dq * scale).astype(BF16)
            o_ref[:, GROUP_W + h * HEAD_DIM:GROUP_W + (h + 1) * HEAD_DIM] = (dk * scale).astype(BF16)
            o_ref[:, 2 * GROUP_W + h * HEAD_DIM:2 * GROUP_W + (h + 1) * HEAD_DIM] = dv.astype(BF16)

    def spec(col, which):
        if which == "prev":
            return pl.BlockSpec((None, BLK, GROUP_W), lambda r, n: (r, jnp.maximum(n - 1, 0), col))
        if which == "next":
            return pl.BlockSpec((None, BLK, GROUP_W), lambda r, n: (r, jnp.minimum(n + 1, nblk - 1), col))
        return pl.BlockSpec((None, BLK, GROUP_W), lambda r, n: (r, n, col))

    return _pcall(
        body, name=name, grid=(dil, nblk),
        in_specs=[spec(0, "cur"), spec(1, "cur"), spec(2, "cur"), spec(1, "prev"), spec(2, "prev"), spec(0, "next"),
                  spec(0, "cur"), spec(0, "next"), spec(0, "cur"), spec(0, "next"), spec(0, "cur"), spec(0, "next")],
        out_specs=pl.BlockSpec((None, BLK, 3 * GROUP_W), lambda r, n: (r, n, 0)),
        out_shape=jax.ShapeDtypeStruct((dil, length, 3 * GROUP_W), BF16),
        compiler_params=_cparams(("parallel", "parallel")),
    )(qkv, qkv, qkv, qkv, qkv, qkv, dout, dout, lse, lse, dd, dd)


def _ssm_prep_values(are, aim, logdt):
    dt = jnp.exp(logdt)
    mag = jnp.exp(are * dt)
    lb_re, lb_im = mag * jnp.cos(aim * dt), mag * jnp.sin(aim * dt)
    inv = 1.0 / (are * are + aim * aim)
    n_re, n_im = lb_re - 1.0, lb_im
    f_re = (n_re * are + n_im * aim) * inv
    f_im = (n_im * are - n_re * aim) * inv
    return dt, lb_re, lb_im, f_re, f_im, inv


PREP_G = 8


def _group_specs(are, logdt, bre):
    def spec(a):
        return pl.BlockSpec((PREP_G,) + a.shape[1:], lambda i: (i, 0, 0))
    return spec(are), spec(logdt), spec(bre)


def ssm_prep(are, aim, logdt, bre, bim):
    def body(are_r, aim_r, ldt_r, bre_r, bim_r, lre_o, lim_o, bbre_o, bbim_o):
        _, lb_re, lb_im, f_re, f_im, _ = _ssm_prep_values(are_r[...], aim_r[...], ldt_r[...])
        lre_o[...] = lb_re
        lim_o[...] = lb_im
        bbre_o[...] = f_re * bre_r[...] - f_im * bim_r[...]
        bbim_o[...] = f_re * bim_r[...] + f_im * bre_r[...]

    sh1 = jax.ShapeDtypeStruct(are.shape, F32)
    shb = jax.ShapeDtypeStruct(bre.shape, F32)
    s1, sd, sb = _group_specs(are, logdt, bre)
    return _pcall(body, name="ssm_prep", grid=(SSM_GROUPS // PREP_G,), in_specs=[s1, s1, sd, sb, sb], out_specs=[s1, s1, sb, sb],
                  out_shape=[sh1, sh1, shb, shb], compiler_params=_cparams(("parallel",)))(are, aim, logdt, bre, bim)


def ssm_prep_bwd(are, aim, logdt, bre, bim, dbbre, dbbim, dlre, dlim):
    def body(are_r, aim_r, ldt_r, bre_r, bim_r, dbbre_r, dbbim_r, dlre_r, dlim_r, dare_o, daim_o, dldt_o, dbre_o, dbim_o):
        are_v, aim_v = are_r[...], aim_r[...]
        dt, lb_re, lb_im, f_re, f_im, inv = _ssm_prep_values(are_v, aim_v, ldt_r[...])
        b_re, b_im, g_re, g_im = bre_r[...], bim_r[...], dbbre_r[...], dbbim_r[...]
        dbre_o[...] = f_re * g_re + f_im * g_im
        dbim_o[...] = f_re * g_im - f_im * g_re
        df_re = jnp.sum(b_re * g_re + b_im * g_im, axis=-1, keepdims=True)
        df_im = jnp.sum(b_re * g_im - b_im * g_re, axis=-1, keepdims=True)
        il_re, il_im = are_v * inv, -aim_v * inv
        cl_re = dlre_r[...] + il_re * df_re + il_im * df_im
        cl_im = dlim_r[...] + il_re * df_im - il_im * df_re
        q_re = -(f_re * il_re - f_im * il_im)
        q_im = -(f_re * il_im + f_im * il_re)
        ca_re = q_re * df_re + q_im * df_im
        ca_im = q_re * df_im - q_im * df_re
        cz_re = lb_re * cl_re + lb_im * cl_im
        cz_im = lb_re * cl_im - lb_im * cl_re
        dare_o[...] = ca_re + dt * cz_re
        daim_o[...] = ca_im + dt * cz_im
        dldt_o[...] = dt * jnp.sum(are_v * cz_re + aim_v * cz_im, axis=1, keepdims=True)

    sh1 = jax.ShapeDtypeStruct(are.shape, F32)
    shb = jax.ShapeDtypeStruct(bre.shape, F32)
    s1, sd, sb = _group_specs(are, logdt, bre)
    return _pcall(
        body, name="ssm_prep_bwd", grid=(SSM_GROUPS // PREP_G,), in_specs=[s1, s1, sd, sb, sb, sb, sb, s1, s1],
        out_specs=[s1, s1, sd, sb, sb], out_shape=[sh1, sh1, jax.ShapeDtypeStruct(logdt.shape, F32), shb, shb],
        compiler_params=_cparams(("parallel",)),
    )(are, aim, logdt, bre, bim, dbbre, dbbim, dlre, dlim)


SCAN_WC = 512


def ssm_scan(xre, xim, lre, lim, name, reverse=False, states=None):
    s = xre.shape[0]
    steps = s // SEGS
    assert steps & (steps - 1) == 0
    tt = min(128, steps)
    nch = steps // tt
    rows = tt * SEGS
    with_dl = states is not None
    nsq = int(math.log2(steps))

    def body(*refs):
        xre_r, xim_r, lre_r, lim_r = refs[:4]
        k = 4
        if with_dl:
            hre_r, him_r, pre_r, pim_r, cre_r, cim_r = refs[k:k + 6]
            k += 6
        ore_r, oim_r, hin_re_o, hin_im_o = refs[k:k + 4]
        k += 4
        if with_dl:
            dlre_o, dlim_o = refs[k:k + 2]
            k += 2
        st_re, st_im = refs[k], refs[k + 1]
        ps, ch = pl.program_id(1), pl.program_id(2)
        a_re = jnp.broadcast_to(lre_r[...], (SEGS, SCAN_WC))
        a_im = jnp.broadcast_to(lim_r[...], (SEGS, SCAN_WC))
        if reverse:
            a_im = -a_im

        @pl.when(jnp.logical_and(ps == 0, ch == 0))
        def _():
            st_re[...] = jnp.zeros_like(st_re)
            st_im[...] = jnp.zeros_like(st_im)

        @pl.when(jnp.logical_and(ps == 1, ch == 0))
        def _():
            p_re, p_im = a_re, a_im
            for _ in range(nsq):
                p_re, p_im = p_re * p_re - p_im * p_im, 2.0 * p_re * p_im
            e_re, e_im = st_re[...], st_im[...]
            row = lax.broadcasted_iota(jnp.int32, (SEGS, SCAN_WC), 0)
            edge = (row == SEGS - 1) if reverse else (row == 0)
            c_re, c_im = jnp.zeros_like(e_re), jnp.zeros_like(e_im)
            for _ in range(SEGS - 1):
                n_re = p_re * c_re - p_im * c_im + e_re
                n_im = p_re * c_im + p_im * c_re + e_im
                sh = SEGS - 1 if reverse else 1
                c_re = jnp.where(edge, 0.0, pltpu.roll(n_re, sh, 0))
                c_im = jnp.where(edge, 0.0, pltpu.roll(n_im, sh, 0))
            st_re[...] = c_re
            st_im[...] = c_im
            hin_re_o[...] = c_re
            hin_im_o[...] = c_im
            if with_dl:
                dlre_o[...] = jnp.zeros_like(dlre_o)
                dlim_o[...] = jnp.zeros_like(dlim_o)

        def run(store):
            def step(i, carry):
                if with_dl and store:
                    h_re, h_im, d_re, d_im = carry
                else:
                    h_re, h_im = carry
                t = (tt - 1 - i) if reverse else i
                off = pl.multiple_of(t * SEGS, SEGS)
                n_re = a_re * h_re - a_im * h_im + xre_r[pl.ds(off, SEGS), :]
                n_im = a_re * h_im + a_im * h_re + xim_r[pl.ds(off, SEGS), :]
                if store:
                    ore_r[pl.ds(off, SEGS), :] = n_re
                    oim_r[pl.ds(off, SEGS), :] = n_im
                if with_dl and store:
                    offp = pl.multiple_of(jnp.maximum(t - 1, 0) * SEGS, SEGS)
                    in_re, in_im = hre_r[pl.ds(offp, SEGS), :], him_r[pl.ds(offp, SEGS), :]
                    first_chunk = ch == nch - 1
                    edge_re = jnp.where(first_chunk, cre_r[...], pre_r[...])
                    edge_im = jnp.where(first_chunk, cim_r[...], pim_r[...])
                    hp_re = jnp.where(t == 0, edge_re, in_re)
                    hp_im = jnp.where(t == 0, edge_im, in_im)
                    d_re = d_re + hp_re * n_re + hp_im * n_im
                    d_im = d_im + hp_re * n_im - hp_im * n_re
                    return n_re, n_im, d_re, d_im
                return n_re, n_im

            init = (st_re[...], st_im[...])
            if with_dl and store:
                init = init + (dlre_o[...], dlim_o[...])
            fin = lax.fori_loop(0, tt, step, init)
            st_re[...] = fin[0]
            st_im[...] = fin[1]
            if with_dl and store:
                dlre_o[...] = fin[2]
                dlim_o[...] = fin[3]

        @pl.when(ps == 0)
        def _():
            run(False)

        @pl.when(ps == 1)
        def _():
            run(True)

    def chunk(c):
        return (nch - 1 - c) if reverse else c

    x_spec = pl.BlockSpec((rows, SCAN_WC), lambda j, ps, c: (chunk(c), j))
    l_spec = pl.BlockSpec((1, SCAN_WC), lambda j, ps, c: (0, j))
    o_spec = pl.BlockSpec((rows, SCAN_WC), lambda j, ps, c: (jnp.where(ps == 1, chunk(c), chunk(0)), j))
    e_spec = pl.BlockSpec((SEGS, SCAN_WC), lambda j, ps, c: (0, j))
    in_specs = [x_spec, x_spec, l_spec, l_spec]
    args = [xre, xim, lre, lim]
    out_specs = [o_spec, o_spec, e_spec, e_spec]
    out_shape = [jax.ShapeDtypeStruct((s, STATE_W), F32)] * 2 + [jax.ShapeDtypeStruct((SEGS, STATE_W), F32)] * 2
    if with_dl:
        prev_spec = pl.BlockSpec((SEGS, SCAN_WC), lambda j, ps, c: (jnp.maximum(chunk(c) * tt - 1, 0), j))
        in_specs += [x_spec, x_spec, prev_spec, prev_spec, e_spec, e_spec]
        args += [states[0], states[1], states[0], states[1], states[2], states[3]]
        out_specs += [e_spec, e_spec]
        out_shape += [jax.ShapeDtypeStruct((SEGS, STATE_W), F32)] * 2
    return _pcall(
        body, name=name, grid=(STATE_W // SCAN_WC, 2, nch), in_specs=in_specs, out_specs=out_specs, out_shape=out_shape,
        scratch_shapes=[pltpu.VMEM((SEGS, SCAN_WC), F32)] * 2,
        compiler_params=_cparams(("parallel", "arbitrary", "arbitrary")),
    )(*args)


def _block_diag(m):
    g, r, c = m.shape
    m = m.reshape(BD, g // BD, r, c)
    eye = jnp.eye(g // BD, dtype=m.dtype)
    return jnp.einsum("jarc,ab->jarbc", m, eye).reshape(BD, (g // BD) * r, (g // BD) * c)


def _block_diag_extract(m, r, c):
    per = m.shape[1] // r
    m = m.reshape(BD, per, r, per, c)
    return jnp.einsum("jarac->jarc", m).reshape(BD * per, r, c)


def dilate(a, d):
    s, w = a.shape
    if d == 1:
        return a.reshape(1, s, w)
    return a.reshape(s // d, d, w).transpose(1, 0, 2)


def undilate(a):
    d, length, w = a.shape
    if d == 1:
        return a.reshape(length, w)
    return a.transpose(1, 0, 2).reshape(d * length, w)


def to_segments(a):
    s, w = a.shape
    return a.reshape(SEGS, s // SEGS, w).transpose(1, 0, 2).reshape(s, w)


def from_segments(a):
    s, w = a.shape
    return a.reshape(s // SEGS, SEGS, w).transpose(1, 0, 2).reshape(s, w)


def local_step(x, target, wts, small):
    s = x.shape[0]
    g1, g2, g3, g4 = (small[k].reshape(1, D_MODEL) for k in ("norm_mix_pre", "norm_mix_post", "norm_ffn_pre", "norm_ffn_post"))
    dvec = small["ssm_d"].reshape(1, SSM_W)

    (h,) = rowwise("rms_in", lambda r, c: ([_rms(r[0], c[0])[0]], []), [x], [g1], [(D_MODEL, BF16)])
    hd = [h.reshape(1, s, D_MODEL), dilate(h, 4), dilate(h, 16)]
    qkv = [mm([(hd[g].reshape(s, D_MODEL), wts["qkv"][g])], "nn", BF16, f"mm_qkv{g}") for g in range(3)]
    u = mm([(h, wts["u"])], "nn", F32, "mm_u")
    gates = mm([(h, wts["gates"])], "nn", F32, "mm_gates")

    outs, lses = [], []
    for g, (_, dil) in enumerate(ATTN_GROUPS):
        o, l = attn_fwd(qkv[g].reshape(dil, s // dil, 3 * GROUP_W), g, f"attn_fwd{g}")
        outs.append(undilate(o))
        lses.append(undilate(l))

    def merge_fn(r, c):
        w0, w1, w2 = _mix_weights(r[3], r[4], r[5])
        return [w0 * r[0] + w1 * r[1] + w2 * r[2]], []

    (attn,) = rowwise("attn_merge", merge_fn, outs + lses, [], [(GROUP_W, BF16)])
    attn_branch = mm([(attn, wts["up"])], "nn", F32, "mm_up")

    are3 = small["ssm_a_re"].reshape(SSM_GROUPS, SSM_STATE, 1)
    aim3 = small["ssm_a_im"].reshape(SSM_GROUPS, SSM_STATE, 1)
    ldt3 = small["ssm_log_dt"].reshape(SSM_GROUPS, 1, 1)
    bre3 = small["ssm_b_re"].reshape(SSM_GROUPS, SSM_STATE, SSM_GROUP)
    bim3 = small["ssm_b_im"].reshape(SSM_GROUPS, SSM_STATE, SSM_GROUP)
    cre3 = small["ssm_c_re"].reshape(SSM_GROUPS, SSM_GROUP, SSM_STATE)
    cim3 = small["ssm_c_im"].reshape(SSM_GROUPS, SSM_GROUP, SSM_STATE)
    lre3, lim3, bbre, bbim = ssm_prep(are3, aim3, ldt3, bre3, bim3)
    lre, lim = lre3.reshape(1, STATE_W), lim3.reshape(1, STATE_W)
    w_bre = _block_diag(bbre.transpose(0, 2, 1)).astype(BF16)
    w_bim = _block_diag(bbim.transpose(0, 2, 1)).astype(BF16)
    w_cre = _block_diag(cre3.transpose(0, 2, 1)).astype(BF16)
    w_cim = _block_diag(cim3.transpose(0, 2, 1)).astype(BF16)
    u_s = to_segments(u)
    bu_re = bdmm([(u_s, w_bre)], "ssm_bu_re")
    bu_im = bdmm([(u_s, w_bim)], "ssm_bu_im")
    h_re, h_im, hin_re, hin_im = ssm_scan(bu_re, bu_im, lre, lim, "ssm_scan_fwd")
    y_lin = bdmm([(h_re, w_cre), (h_im, -w_cim)], "ssm_y")

    def gelu_fn(r, c):
        y = r[0] + c[0] * r[1]
        return [_gelu(y)[0], y], []

    yg_s, y_ssm = rowwise("ssm_gelu", gelu_fn, [y_lin, u_s], [dvec], [(SSM_W, BF16), (SSM_W, F32)])
    yg = from_segments(yg_s)
    gv = mm([(yg, wts["glu_v"])], "nn", F32, "mm_glu_v")
    gg = mm([(yg, wts["glu_g"])], "nn", F32, "mm_glu_g")

    def gate_fn(r, c):
        gts, ab, gv_, gg_ = r
        sa, ss = _sigmoid(gts[:, :D_MODEL]), _sigmoid(gts[:, D_MODEL:])
        return [sa * ab + ss * (gv_ * _sigmoid(gg_))], []

    (merged,) = rowwise("gate_merge", gate_fn, [gates, attn_branch, gv, gg], [], [(D_MODEL, BF16)])
    o_mix = mm([(merged, wts["out"])], "nn", F32, "mm_out")

    def mid_fn(r, c):
        x1 = r[0] + _rms(r[1], c[0])[0]
        return [x1, _rms(x1, c[1])[0]], []

    x1, h2 = rowwise("rms_mid", mid_fn, [x, o_mix], [g2, g3], [(D_MODEL, F32), (D_MODEL, BF16)])
    fa = mm([(h2, wts["ffn_gate"])], "nn", F32, "mm_ffn_gate")
    fb = mm([(h2, wts["ffn_up"])], "nn", F32, "mm_ffn_up")
    (fin,) = rowwise("swiglu", lambda r, c: ([r[0] * _sigmoid(r[0]) * r[1]], []), [fa, fb], [], [(D_FF, BF16)])
    f = mm([(fin, wts["ffn_down"])], "nn", F32, "mm_ffn_down")

    def loss_fn(r, c):
        x1_, f_, tgt = r
        y, n, rr = _rms(f_, c[0])
        err = x1_ + y - tgt
        dout = err * (1.0 / D_MODEL)
        df, dg = _rms_bwd(dout, n, rr, c[0])
        lp = 0.5 * jnp.sum(jnp.sum(err * err, axis=-1, keepdims=True) * (1.0 / D_MODEL), axis=0, keepdims=True)
        return [df, dout], [dg, lp]

    df, dout, dg4, loss_part = rowwise("loss_bwd", loss_fn, [x1, f, target], [g4], [(D_MODEL, BF16), (D_MODEL, F32)],
                                       acc_outs=[(1, D_MODEL), (1, 1)])
    dfin = mm([(df, wts["ffn_down"])], "nt", F32, "mm_d_fin")
    dw_ffn_down = mm([(fin, df)], "tn", BF16, "mm_dw_ffn_down")

    def swiglu_bwd(r, c):
        dfin_, a, b = r
        sg = _sigmoid(a)
        return [dfin_ * b * (sg * (1.0 + a * (1.0 - sg))), dfin_ * a * sg], []

    da, db = rowwise("swiglu_bwd", swiglu_bwd, [dfin, fa, fb], [], [(D_FF, BF16), (D_FF, BF16)])
    dh2 = mm([(da, wts["ffn_gate"]), (db, wts["ffn_up"])], "nt", F32, "mm_d_h2")
    dw_ffn_gate = mm([(h2, da)], "tn", BF16, "mm_dw_ffn_gate")
    dw_ffn_up = mm([(h2, db)], "tn", BF16, "mm_dw_ffn_up")

    def mid_bwd(r, c):
        dh2_, dout_, x1_, o_ = r
        _, n3, r3 = _rms(x1_, c[1])
        dx1, dg3_ = _rms_bwd(dh2_, n3, r3, c[1])
        dx1 = dx1 + dout_
        _, n2, r2 = _rms(o_, c[0])
        do_, dg2_ = _rms_bwd(dx1, n2, r2, c[0])
        return [dx1, do_], [dg2_, dg3_]

    dx1, do_mix, dg2, dg3 = rowwise("rms_mid_bwd", mid_bwd, [dh2, dout, x1, o_mix], [g2, g3], [(D_MODEL, F32), (D_MODEL, BF16)],
                                    acc_outs=[(1, D_MODEL), (1, D_MODEL)])
    dmerged = mm([(do_mix, wts["out"])], "nt", F32, "mm_d_merged")
    dw_out = mm([(merged, do_mix)], "tn", BF16, "mm_dw_out")

    def gate_bwd(r, c):
        dm, gts, ab, gv_, gg_ = r
        sa, ss, sg = _sigmoid(gts[:, :D_MODEL]), _sigmoid(gts[:, D_MODEL:]), _sigmoid(gg_)
        branch = gv_ * sg
        dbranch = dm * ss
        dgates = jnp.concatenate([dm * ab * sa * (1.0 - sa), dm * branch * ss * (1.0 - ss)], axis=-1)
        return [dgates, dm * sa, dbranch * sg, dbranch * gv_ * sg * (1.0 - sg)], []

    dgates, dab, dgv, dgg = rowwise("gate_bwd", gate_bwd, [dmerged, gates, attn_branch, gv, gg], [],
                                    [(2 * D_MODEL, BF16), (D_MODEL, BF16), (D_MODEL, BF16), (D_MODEL, BF16)])
    dattn = mm([(dab, wts["up"])], "nt", F32, "mm_d_attn")
    dw_up = mm([(attn, dab)], "tn", BF16, "mm_dw_up")
    dyg = mm([(dgv, wts["glu_v"]), (dgg, wts["glu_g"])], "nt", F32, "mm_d_yg")
    dw_glu_v = mm([(yg, dgv)], "tn", BF16, "mm_dw_glu_v")
    dw_glu_g = mm([(yg, dgg)], "tn", BF16, "mm_dw_glu_g")

    def gelu_bwd(r, c):
        dyg_, y, us = r
        dy = dyg_ * _gelu_grad(y, _gelu(y)[1])
        return [dy, c[0] * dy], [jnp.sum(dy * us, axis=0, keepdims=True)]

    dy_ssm, du_skip, dd_ssm = rowwise("ssm_gelu_bwd", gelu_bwd, [to_segments(dyg), y_ssm, u_s], [dvec],
                                      [(SSM_W, F32), (SSM_W, F32)], acc_outs=[(1, SSM_W)])
    w_cre_t, w_cim_t = w_cre.transpose(0, 2, 1), w_cim.transpose(0, 2, 1)
    gin_re = bdmm([(dy_ssm, w_cre_t)], "ssm_gin_re")
    gin_im = bdmm([(dy_ssm, -w_cim_t)], "ssm_gin_im")
    g_re, g_im, _, _, dl_re8, dl_im8 = ssm_scan(gin_re, gin_im, lre, lim, "ssm_scan_bwd", reverse=True,
                                                 states=(h_re, h_im, hin_re, hin_im))
    du_s = bdmm([(g_re, w_bre.transpose(0, 2, 1)), (g_im, w_bim.transpose(0, 2, 1))], "ssm_du", add=du_skip)
    dbb_re = _block_diag_extract(bd_tn(u_s, g_re, "ssm_dbb_re"), SSM_GROUP, SSM_STATE).transpose(0, 2, 1)
    dbb_im = _block_diag_extract(bd_tn(u_s, g_im, "ssm_dbb_im"), SSM_GROUP, SSM_STATE).transpose(0, 2, 1)
    dc_re = _block_diag_extract(bd_tn(h_re, dy_ssm, "ssm_dc_re"), SSM_STATE, SSM_GROUP).transpose(0, 2, 1)
    dc_im = -_block_diag_extract(bd_tn(h_im, dy_ssm, "ssm_dc_im"), SSM_STATE, SSM_GROUP).transpose(0, 2, 1)

    def fold8(r, c):
        return [], [jnp.sum(r[0], axis=0, keepdims=True), jnp.sum(r[1], axis=0, keepdims=True)]

    dl_re, dl_im = rowwise("ssm_dl_fold", fold8, [dl_re8, dl_im8], [], [], acc_outs=[(1, STATE_W), (1, STATE_W)], ts=SEGS)
    da_re, da_im, dldt, db_re, db_im = ssm_prep_bwd(
        are3, aim3, ldt3, bre3, bim3, dbb_re, dbb_im,
        dl_re.reshape(SSM_GROUPS, SSM_STATE, 1), dl_im.reshape(SSM_GROUPS, SSM_STATE, 1))
    du = from_segments(du_s)

    def merge_bwd(r, c):
        dat, o0, o1, o2, l0, l1, l2 = r
        w0, w1, w2 = _mix_weights(l0, l1, l2)
        tot = _head_sum(dat * (w0 * o0 + w1 * o1 + w2 * o2))
        return [w0 * dat, w1 * dat, w2 * dat, w0 * tot, w1 * tot, w2 * tot], []

    mb = rowwise("attn_merge_bwd", merge_bwd, [dattn] + outs + lses, [], [(GROUP_W, BF16)] * 3 + [(GROUP_W, F32)] * 3)
    dh_parts = []
    dw_qkv = []
    for g, (_, dil) in enumerate(ATTN_GROUPS):
        dq = attn_bwd(qkv[g].reshape(dil, s // dil, 3 * GROUP_W), dilate(mb[g], dil), dilate(lses[g], dil),
                      dilate(mb[3 + g], dil), g, f"attn_bwd{g}").reshape(s, 3 * GROUP_W)
        dh_g = mm([(dq, wts["qkv"][g])], "nt", F32, f"mm_d_h_qkv{g}")
        dh_parts.append(undilate(dh_g.reshape(dil, s // dil, D_MODEL)))
        dw_qkv.append(mm([(hd[g].reshape(s, D_MODEL), dq)], "tn", BF16, f"mm_dw_qkv{g}"))
    dh_parts.append(mm([(du, wts["u"])], "nt", F32, "mm_d_h_u"))
    dh_parts.append(mm([(dgates, wts["gates"])], "nt", F32, "mm_d_h_gates"))
    dw_u = mm([(h, du)], "tn", BF16, "mm_dw_u")
    dw_gates = mm([(h, dgates)], "tn", BF16, "mm_dw_gates")

    def in_bwd(r, c):
        dh = r[0] + r[1] + r[2] + r[3] + r[4]
        _, n1, r1 = _rms(r[6], c[0])
        dx, dg1_ = _rms_bwd(dh, n1, r1, c[0])
        return [dx + r[5]], [dg1_]

    grad_x, dg1 = rowwise("rms_in_bwd", in_bwd, dh_parts + [dx1, x], [g1], [(D_MODEL, F32)], acc_outs=[(1, D_MODEL)])

    dw = dict(qkv=dw_qkv, u=dw_u, gates=dw_gates, up=dw_up, glu_v=dw_glu_v, glu_g=dw_glu_g, out=dw_out,
              ffn_gate=dw_ffn_gate, ffn_up=dw_ffn_up, ffn_down=dw_ffn_down)
    dsmall = dict(norm_mix_pre=dg1, ssm_a_re=da_re, ssm_a_im=da_im, ssm_log_dt=dldt, ssm_b_re=db_re, ssm_b_im=db_im,
                  ssm_c_re=dc_re, ssm_c_im=dc_im, ssm_d=dd_ssm, norm_mix_post=dg2, norm_ffn_pre=dg3, norm_ffn_post=dg4)
    return loss_part, grad_x, dw, dsmall


def _my_coords():
    return lax.axis_index("x"), lax.axis_index("y"), lax.axis_index("c")


def all_gather_blocks(xs, name):
    na = len(xs)

    def body(*refs):
        x_refs, out_refs = refs[:na], refs[na:2 * na]
        send_sems, recv_sems, local_sems = refs[2 * na:]
        mx, my, mc = _my_coords()
        me, sibling = (mx, my, mc), (mx, my, 1 - mc)
        chips = [(1 - mx, my), (mx, 1 - my), (1 - mx, 1 - my)]

        def slot(a, px, py, pc):
            return out_refs[a].at[4 * px + 2 * py + pc]

        def copy(a, k, block, to, src=None):
            return pltpu.make_async_remote_copy(
                src_ref=slot(a, *block) if src is None else src, dst_ref=slot(a, *block),
                send_sem=send_sems.at[7 * a + k], recv_sem=recv_sems.at[7 * a + k], device_id=to, device_id_type=MESH_ID)

        sent, local = [], []
        for a in range(na):
            mine = pltpu.make_async_copy(x_refs[a], slot(a, *me), local_sems.at[a])
            mine.start()
            local.append(mine)
            first = [copy(a, 0, me, sibling, src=x_refs[a])]
            first += [copy(a, 1 + j, me, (*chip, mc), src=x_refs[a]) for j, chip in enumerate(chips)]
            for cp in first:
                cp.start()
            sent += first
        for a in range(na):
            for j, chip in enumerate(chips):
                copy(a, 1 + j, (*chip, mc), me).wait_recv()
                fwd = copy(a, 4 + j, (*chip, mc), sibling)
                fwd.start()
                sent.append(fwd)
        for a in range(na):
            copy(a, 0, sibling, me).wait_recv()
            for j, chip in enumerate(chips):
                copy(a, 4 + j, (*chip, 1 - mc), me).wait_recv()
        for cp in sent:
            cp.wait_send()
        for mine in local:
            mine.wait()

    return _pcall(
        body, name=name, out_shape=[jax.ShapeDtypeStruct((N_DEV,) + x.shape, x.dtype) for x in xs],
        in_specs=[HBM_SPEC] * na, out_specs=[HBM_SPEC] * na,
        scratch_shapes=[pltpu.SemaphoreType.DMA((7 * na,)), pltpu.SemaphoreType.DMA((7 * na,)), pltpu.SemaphoreType.DMA((na,))],
    )(*xs)


def all_to_all_blocks(ps, name):
    na = len(ps)

    def body(*refs):
        p_refs, out_refs = refs[:na], refs[na:2 * na]
        send_sems, recv_sems, local_sems = refs[2 * na:]
        mx, my, mc = _my_coords()
        me = 4 * mx + 2 * my + mc
        locals_, copies = [], []
        for a in range(na):
            mine = pltpu.make_async_copy(p_refs[a].at[me], out_refs[a].at[me], local_sems.at[a])
            mine.start()
            locals_.append(mine)
            for k in range(1, N_DEV):
                px, py, pc = mx ^ ((k >> 2) & 1), my ^ ((k >> 1) & 1), mc ^ (k & 1)
                cp = pltpu.make_async_remote_copy(
                    src_ref=p_refs[a].at[4 * px + 2 * py + pc], dst_ref=out_refs[a].at[me],
                    send_sem=send_sems.at[7 * a + k - 1], recv_sem=recv_sems.at[7 * a + k - 1],
                    device_id=(px, py, pc), device_id_type=MESH_ID)
                cp.start()
                copies.append(cp)
        for cp in copies:
            cp.wait_recv()
        for cp in copies:
            cp.wait_send()
        for mine in locals_:
            mine.wait()

    return _pcall(
        body, name=name, out_shape=[jax.ShapeDtypeStruct(p.shape, p.dtype) for p in ps],
        in_specs=[HBM_SPEC] * na, out_specs=[HBM_SPEC] * na,
        scratch_shapes=[pltpu.SemaphoreType.DMA((7 * na,)), pltpu.SemaphoreType.DMA((7 * na,)), pltpu.SemaphoreType.DMA((na,))],
    )(*ps)


def adamw(parts, w, m, v, name):
    r, c = w.shape
    tr = r
    while tr > 8 and tr % 2 == 0 and tr * c * (8 * parts.dtype.itemsize + 28) * 2 > 24 * 1024 * 1024:
        tr //= 2
    assert r % tr == 0 and (tr % 8 == 0 or tr == r)
    c1, c2 = 1.0 / (1.0 - ADAM_B1 ** ADAM_STEP), 1.0 / (1.0 - ADAM_B2 ** ADAM_STEP)

    def body(p_ref, w_ref, m_ref, v_ref, g_o, d_o, m_o, v_o):
        g = p_ref[0].astype(F32)
        for i in range(1, N_DEV):
            g = g + p_ref[i].astype(F32)
        mn = ADAM_B1 * m_ref[...] + (1.0 - ADAM_B1) * g
        vn = ADAM_B2 * v_ref[...] + (1.0 - ADAM_B2) * (g * g)
        g_o[...] = g
        m_o[...] = mn
        v_o[...] = vn
        d_o[...] = -ADAM_LR * ((mn * c1) / (jnp.sqrt(vn * c2) + ADAM_EPS) + ADAM_WD * w_ref[...])

    blk = pl.BlockSpec((tr, c), lambda i: (i, 0))
    return _pcall(
        body, name=name, grid=(r // tr,), in_specs=[pl.BlockSpec((N_DEV, tr, c), lambda i: (0, i, 0)), blk, blk, blk],
        out_specs=[blk] * 4, out_shape=[jax.ShapeDtypeStruct((r, c), F32)] * 4, compiler_params=_cparams(("parallel",)),
    )(parts, w, m, v)


PACK_C = 1024
SHARDED = ("w_in", "w_attn_up", "w_glu_v", "w_glu_g", "w_out", "w_ffn_gate", "w_ffn_up", "w_ffn_down")
ROW_SHARDED = ("w_out", "w_ffn_down")
SMALL = ("norm_mix_pre", "ssm_a_re", "ssm_a_im", "ssm_log_dt", "ssm_b_re", "ssm_b_im", "ssm_c_re", "ssm_c_im", "ssm_d",
         "norm_mix_post", "norm_ffn_pre", "norm_ffn_post")
WEIGHTS = ("norm_mix_pre", "w_in", "w_attn_up", "ssm_a_re", "ssm_a_im", "ssm_log_dt", "ssm_b_re", "ssm_b_im", "ssm_c_re",
           "ssm_c_im", "ssm_d", "w_glu_v", "w_glu_g", "w_out", "norm_mix_post", "norm_ffn_pre", "w_ffn_gate", "w_ffn_up",
           "w_ffn_down", "norm_ffn_post")


def _pack(arrs, dtype, pad_rows_to=64):
    flat = jnp.concatenate([a.reshape(-1).astype(dtype) for a in arrs])
    n = flat.shape[0]
    rows = -(-n // PACK_C)
    rows = -(-rows // pad_rows_to) * pad_rows_to
    return jnp.pad(flat, (0, rows * PACK_C - n)).reshape(rows, PACK_C)


def _unpack(flat2d, shapes):
    flat = flat2d.reshape(-1)
    out, off = [], 0
    for shp in shapes:
        n = int(np.prod(shp))
        out.append(flat[off:off + n].reshape(shp))
        off += n
    return out


def _full_from_gathered(gathered, name):
    if name in ROW_SHARDED:
        return gathered.reshape(-1, gathered.shape[2])
    return gathered.transpose(1, 0, 2).reshape(gathered.shape[1], -1)


def _split_for_devices(full, name):
    if name in ROW_SHARDED:
        return full.reshape(N_DEV, -1, full.shape[1])
    return full.reshape(full.shape[0], N_DEV, -1).transpose(1, 0, 2)


def kernel(x, norm_mix_pre, w_in, w_attn_up, ssm_a_re, ssm_a_im, ssm_log_dt, ssm_b_re, ssm_b_im, ssm_c_re, ssm_c_im, ssm_d, w_glu_v, w_glu_g, w_out, norm_mix_post, norm_ffn_pre, w_ffn_gate, w_ffn_up, w_ffn_down, norm_ffn_post, loss_target, m_norm_mix_pre, m_w_in, m_w_attn_up, m_ssm_a_re, m_ssm_a_im, m_ssm_log_dt, m_ssm_b_re, m_ssm_b_im, m_ssm_c_re, m_ssm_c_im, m_ssm_d, m_w_glu_v, m_w_glu_g, m_w_out, m_norm_mix_post, m_norm_ffn_pre, m_w_ffn_gate, m_w_ffn_up, m_w_ffn_down, m_norm_ffn_post, v_norm_mix_pre, v_w_in, v_w_attn_up, v_ssm_a_re, v_ssm_a_im, v_ssm_log_dt, v_ssm_b_re, v_ssm_b_im, v_ssm_c_re, v_ssm_c_im, v_ssm_d, v_w_glu_v, v_w_glu_g, v_w_out, v_norm_mix_post, v_norm_ffn_pre, v_w_ffn_gate, v_w_ffn_up, v_w_ffn_down, v_norm_ffn_post):
    args = dict(locals())
    wv = {n: args[n][0] for n in WEIGHTS}
    mv = {n: args["m_" + n][0] for n in WEIGHTS}
    vv = {n: args["v_" + n][0] for n in WEIGHTS}

    gathered = all_gather_blocks([wv[n].astype(BF16) for n in SHARDED], "gather_weights")
    full = {n: _full_from_gathered(gathered[k], n) for k, n in enumerate(SHARDED)}
    w_in_f = full["w_in"]
    wts = dict(
        qkv=[jnp.concatenate([w_in_f[:, o + g * GROUP_W:o + (g + 1) * GROUP_W] for o in (0, HQ, 2 * HQ)], axis=1) for g in range(3)],
        u=w_in_f[:, 3 * HQ:3 * HQ + SSM_W], gates=w_in_f[:, 3 * HQ + SSM_W:],
        up=full["w_attn_up"], glu_v=full["w_glu_v"], glu_g=full["w_glu_g"], out=full["w_out"],
        ffn_gate=full["w_ffn_gate"], ffn_up=full["w_ffn_up"], ffn_down=full["w_ffn_down"])
    small = {n: wv[n] for n in SMALL}

    loss_part, grad_x, dw, dsmall = local_step(x[0], loss_target[0], wts, small)

    dq = dw["qkv"]
    dw_in = jnp.concatenate(
        [dq[g][:, o * GROUP_W:(o + 1) * GROUP_W] for o in range(3) for g in range(3)] + [dw["u"], dw["gates"]], axis=1)
    dfull = dict(w_in=dw_in, w_attn_up=dw["up"], w_glu_v=dw["glu_v"], w_glu_g=dw["glu_g"], w_out=dw["out"],
                 w_ffn_gate=dw["ffn_gate"], w_ffn_up=dw["ffn_up"], w_ffn_down=dw["ffn_down"])
    recv = all_to_all_blocks([_split_for_devices(dfull[n], n) for n in SHARDED], "scatter_weight_grads")

    res = {}
    for k, n in enumerate(SHARDED):
        res[n] = adamw(recv[k], wv[n], mv[n], vv[n], "adamw_" + n)

    small_shapes = [wv[n].shape for n in SMALL]
    (sgather,) = all_gather_blocks([_pack([dsmall[n] for n in SMALL], F32)], "gather_small_grads")
    sres = adamw(sgather, _pack([wv[n] for n in SMALL], F32), _pack([mv[n] for n in SMALL], F32),
                 _pack([vv[n] for n in SMALL], F32), "adamw_small")
    sun = [_unpack(t, small_shapes) for t in sres]
    for k, n in enumerate(SMALL):
        res[n] = tuple(sun[t][k] for t in range(4))

    loss = lax.psum(loss_part[0, 0], ("x", "y", "c"))
    outs = [loss, grad_x[None]]
    for t in range(4):
        outs += [res[n][t][None] for n in WEIGHTS]
    return tuple(outs)
```

```python
import functools
import math

import numpy as np
import jax
import jax.numpy as jnp
from jax import lax
from jax.experimental import pallas as pl
from jax.experimental.pallas import tpu as pltpu

F32 = jnp.float32
BF16 = jnp.bfloat16

D_MODEL = 2048
HEAD_DIM = 128
HEADS_PER_GROUP = 4
ATTN_GROUPS = ((128, 1), (512, 4), (2048, 16))
N_HEADS = HEADS_PER_GROUP * len(ATTN_GROUPS)
GROUP_W = HEADS_PER_GROUP * HEAD_DIM
HQ = N_HEADS * HEAD_DIM
SSM_W = 1024
SSM_GROUP = 16
SSM_GROUPS = 64
SSM_STATE = 64
STATE_W = SSM_GROUPS * SSM_STATE
D_FF = 5632
EPS = 1e-6
N_DEV = 8
SEGS = 8
BD = 8

ADAM_LR, ADAM_B1, ADAM_B2, ADAM_EPS, ADAM_WD, ADAM_STEP = 0.001, 0.9, 0.999, 1e-08, 0.01, 10

VMEM_LIMIT = 56 * 1024 * 1024
HBM_SPEC = pl.BlockSpec(memory_space=pltpu.HBM)
MESH_ID = pl.DeviceIdType.MESH
NEG = -1e30


def _pcall(body, **kw):
    return pl.pallas_call(body, **kw)


def _cparams(sem=None):
    if sem is None:
        return pltpu.CompilerParams(vmem_limit_bytes=VMEM_LIMIT)
    return pltpu.CompilerParams(vmem_limit_bytes=VMEM_LIMIT, dimension_semantics=sem)


def _my_coords():
    return lax.axis_index("x"), lax.axis_index("y"), lax.axis_index("c")


class Gather:
    def __init__(self, xs):
        self.arrays = list(xs)
        self.out_shapes = [jax.ShapeDtypeStruct((N_DEV,) + x.shape, x.dtype) for x in xs]

    def _ctx(self, out_refs, send_sems, recv_sems):
        mx, my, mc = _my_coords()
        me, sibling = (mx, my, mc), (mx, my, 1 - mc)
        chips = [(1 - mx, my), (mx, 1 - my), (1 - mx, 1 - my)]

        def slot(a, px, py, pc):
            return out_refs[a].at[4 * px + 2 * py + pc]

        def copy(a, k, block, to, src=None):
            return pltpu.make_async_remote_copy(
                src_ref=slot(a, *block) if src is None else src, dst_ref=slot(a, *block),
                send_sem=send_sems.at[7 * a + k], recv_sem=recv_sems.at[7 * a + k], device_id=to, device_id_type=MESH_ID)

        return me, sibling, chips, mc, slot, copy

    def _first(self, a, x_refs, ctx):
        me, sibling, chips, mc, slot, copy = ctx
        return [copy(a, 0, me, sibling, src=x_refs[a])] + [copy(a, 1 + j, me, (*chip, mc), src=x_refs[a]) for j, chip in enumerate(chips)]

    def start(self, x_refs, out_refs, send_sems, recv_sems, local_sems):
        ctx = self._ctx(out_refs, send_sems, recv_sems)
        me, slot = ctx[0], ctx[4]
        for a in range(len(self.arrays)):
            pltpu.make_async_copy(x_refs[a], slot(a, *me), local_sems.at[a]).start()
            for cp in self._first(a, x_refs, ctx):
                cp.start()

    def finish(self, x_refs, out_refs, send_sems, recv_sems, local_sems):
        ctx = self._ctx(out_refs, send_sems, recv_sems)
        me, sibling, chips, mc, slot, copy = ctx
        na = len(self.arrays)
        passed = []
        for a in range(na):
            for j, chip in enumerate(chips):
                copy(a, 1 + j, (*chip, mc), me).wait_recv()
                fwd = copy(a, 4 + j, (*chip, mc), sibling)
                fwd.start()
                passed.append(fwd)
        for a in range(na):
            copy(a, 0, sibling, me).wait_recv()
            for j, chip in enumerate(chips):
                copy(a, 4 + j, (*chip, 1 - mc), me).wait_recv()
        for a in range(na):
            for cp in self._first(a, x_refs, ctx):
                cp.wait_send()
        for cp in passed:
            cp.wait_send()
        for a in range(na):
            pltpu.make_async_copy(x_refs[a], slot(a, *me), local_sems.at[a]).wait()


class AllToAll:
    def __init__(self, ps):
        self.arrays = list(ps)
        self.out_shapes = [jax.ShapeDtypeStruct(p.shape, p.dtype) for p in ps]

    def _copies(self, p_refs, out_refs, send_sems, recv_sems, local_sems):
        mx, my, mc = _my_coords()
        me = 4 * mx + 2 * my + mc
        local, remote = [], []
        for a in range(len(self.arrays)):
            local.append(pltpu.make_async_copy(p_refs[a].at[me], out_refs[a].at[me], local_sems.at[a]))
            for k in range(1, N_DEV):
                px, py, pc = mx ^ ((k >> 2) & 1), my ^ ((k >> 1) & 1), mc ^ (k & 1)
                remote.append(pltpu.make_async_remote_copy(
                    src_ref=p_refs[a].at[4 * px + 2 * py + pc], dst_ref=out_refs[a].at[me],
                    send_sem=send_sems.at[7 * a + k - 1], recv_sem=recv_sems.at[7 * a + k - 1],
                    device_id=(px, py, pc), device_id_type=MESH_ID))
        return local, remote

    def start(self, *refs):
        local, remote = self._copies(*refs)
        for cp in local + remote:
            cp.start()

    def finish(self, *refs):
        local, remote = self._copies(*refs)
        for cp in remote:
            cp.wait_recv()
        for cp in remote:
            cp.wait_send()
        for cp in local:
            cp.wait()


def _run(body, args, carry=None, **kw):
    if carry is None:
        return _pcall(body, **kw)(*args)
    grid = kw["grid"]
    single = not isinstance(kw["out_shape"], (list, tuple))
    in_specs = list(kw["in_specs"])
    out_specs = [kw["out_specs"]] if single else list(kw["out_specs"])
    out_shape = [kw["out_shape"]] if single else list(kw["out_shape"])
    scratch = list(kw.get("scratch_shapes", []))
    na, nin, nout, nscr = len(carry.arrays), len(in_specs), len(out_specs), len(scratch)

    def carried(*refs):
        ins, cin = refs[:nin], refs[nin:nin + na]
        outs, cout = refs[nin + na:nin + na + nout], refs[nin + na + nout:nin + 2 * na + nout]
        scr = refs[nin + 2 * na + nout:nin + 2 * na + nout + nscr]
        sems = refs[nin + 2 * na + nout + nscr:]
        ids = [pl.program_id(i) for i in range(len(grid))]
        first, last = ids[0] == 0, ids[0] == grid[0] - 1
        for i in range(1, len(grid)):
            first = jnp.logical_and(first, ids[i] == 0)
            last = jnp.logical_and(last, ids[i] == grid[i] - 1)

        @pl.when(first)
        def _():
            carry.start(cin, cout, *sems)

        body(*ins, *outs, *scr)

        @pl.when(last)
        def _():
            carry.finish(cin, cout, *sems)

    res = _pcall(
        carried, name=kw["name"], grid=grid, in_specs=in_specs + [HBM_SPEC] * na, out_specs=out_specs + [HBM_SPEC] * na,
        out_shape=out_shape + carry.out_shapes,
        scratch_shapes=scratch + [pltpu.SemaphoreType.DMA((7 * na,)), pltpu.SemaphoreType.DMA((7 * na,)), pltpu.SemaphoreType.DMA((na,))],
        compiler_params=_cparams(("arbitrary",) * len(grid)),
    )(*args, *carry.arrays)
    main = res[:nout]
    return (main[0] if single else main), list(res[nout:])


def exchange(carry, name):
    def body():
        pass

    return _run(body, [], carry=carry, name=name, grid=(1,), in_specs=[], out_specs=[], out_shape=[])[1]


_DN = {"nn": (((1,), (0,)), ((), ())), "nt": (((1,), (1,)), ((), ())), "tn": (((0,), (0,)), ((), ()))}


LANE = 128
MM_TM, MM_TN, MM_TK = 1024, 1536, 2048


def _tile(n, cap):
    for t in range(min(cap, n) // LANE * LANE, 0, -LANE):
        if n % t == 0:
            return t
    raise ValueError(n)


def mm(pairs, mode, out_dtype, name, tm=MM_TM, tn=MM_TN, tk=MM_TK, carry=None):
    a0, b0 = pairs[0]
    if mode == "nn":
        (m, k), n = a0.shape, b0.shape[1]
    elif mode == "nt":
        (m, k), n = a0.shape, b0.shape[0]
    else:
        (k, m), n = a0.shape, b0.shape[1]
    tm, tn, tk = _tile(m, tm), _tile(n, tn), _tile(k, tk)
    nk = k // tk
    npairs = len(pairs)

    def body(*refs):
        o_ref = refs[2 * npairs]
        tot = None
        for p in range(npairs):
            a = refs[2 * p][...].astype(BF16)
            b = refs[2 * p + 1][...].astype(BF16)
            d = lax.dot_general(a, b, _DN[mode], preferred_element_type=F32)
            tot = d if tot is None else tot + d
        if nk == 1:
            o_ref[...] = tot.astype(o_ref.dtype)
            return
        acc = refs[2 * npairs + 1]
        kk = pl.program_id(2)

        @pl.when(kk == 0)
        def _():
            acc[...] = tot

        @pl.when(kk > 0)
        def _():
            acc[...] += tot

        @pl.when(kk == nk - 1)
        def _():
            o_ref[...] = acc[...].astype(o_ref.dtype)

    if mode == "nn":
        sp = [pl.BlockSpec((tm, tk), lambda i, j, kk: (i, kk)), pl.BlockSpec((tk, tn), lambda i, j, kk: (kk, j))]
    elif mode == "nt":
        sp = [pl.BlockSpec((tm, tk), lambda i, j, kk: (i, kk)), pl.BlockSpec((tn, tk), lambda i, j, kk: (j, kk))]
    else:
        sp = [pl.BlockSpec((tk, tm), lambda i, j, kk: (kk, i)), pl.BlockSpec((tk, tn), lambda i, j, kk: (kk, j))]
    return _run(
        body, [t for pr in pairs for t in pr], carry=carry, name=name, grid=(m // tm, n // tn, nk), in_specs=sp * npairs,
        out_specs=pl.BlockSpec((tm, tn), lambda i, j, kk: (i, j)),
        out_shape=jax.ShapeDtypeStruct((m, n), out_dtype),
        scratch_shapes=[pltpu.VMEM((tm, tn), F32)] if nk > 1 else [],
        compiler_params=_cparams(("parallel", "parallel", "arbitrary")),
    )


def bdmm(pairs, name, add=None, ts=1024):
    a0, w0 = pairs[0]
    s = a0.shape[0]
    ka, kn = w0.shape[1], w0.shape[2]
    ts = min(ts, s)
    npairs = len(pairs)

    def body(*refs):
        o_ref = refs[-1]
        tot = None
        for p in range(npairs):
            d = jnp.dot(refs[2 * p][...].astype(BF16), refs[2 * p + 1][...].astype(BF16), preferred_element_type=F32)
            tot = d if tot is None else tot + d
        if add is not None:
            tot = tot + refs[2 * npairs][...]
        o_ref[...] = tot

    sp = []
    args = []
    for a, w in pairs:
        sp += [pl.BlockSpec((ts, w.shape[1]), lambda i, j: (i, j)), pl.BlockSpec((None, w.shape[1], kn), lambda i, j: (j, 0, 0))]
        args += [a, w]
    if add is not None:
        sp.append(pl.BlockSpec((ts, kn), lambda i, j: (i, j)))
        args.append(add)
    return _pcall(
        body, name=name, grid=(s // ts, BD), in_specs=sp, out_specs=pl.BlockSpec((ts, kn), lambda i, j: (i, j)),
        out_shape=jax.ShapeDtypeStruct((s, BD * kn), F32), compiler_params=_cparams(("parallel", "parallel")),
    )(*args)


def bd_tn(a, b, name, ts=512):
    s = a.shape[0]
    ka, kb = a.shape[1] // BD, b.shape[1] // BD
    ts = min(ts, s)
    ns = s // ts

    def body(a_ref, b_ref, o_ref, acc):
        kk = pl.program_id(1)

        @pl.when(kk == 0)
        def _():
            acc[...] = jnp.zeros_like(acc)

        acc[...] += lax.dot_general(a_ref[...].astype(BF16), b_ref[...].astype(BF16), _DN["tn"], preferred_element_type=F32)

        @pl.when(kk == ns - 1)
        def _():
            o_ref[...] = acc[...]

    return _pcall(
        body, name=name, grid=(BD, ns),
        in_specs=[pl.BlockSpec((ts, ka), lambda j, kk: (kk, j)), pl.BlockSpec((ts, kb), lambda j, kk: (kk, j))],
        out_specs=pl.BlockSpec((None, ka, kb), lambda j, kk: (j, 0, 0)),
        out_shape=jax.ShapeDtypeStruct((BD, ka, kb), F32), scratch_shapes=[pltpu.VMEM((ka, kb), F32)],
        compiler_params=_cparams(("parallel", "arbitrary")),
    )(a, b)


def rowwise(name, fn, row_ins, const_ins, row_outs, acc_outs=(), ts=None, carry=None):
    s = row_ins[0].shape[0]
    if ts is None:
        per_row = sum(a.shape[1] * a.dtype.itemsize for a in row_ins) + sum(w * jnp.dtype(dt).itemsize for w, dt in row_outs)
        ts = 512
        while ts > 8 and 2 * ts * per_row > 20 * 1024 * 1024:
            ts //= 2
    ts = min(ts, s)
    assert s % ts == 0
    nr, nc, no, na = len(row_ins), len(const_ins), len(row_outs), len(acc_outs)

    def body(*refs):
        rows = [r[...] for r in refs[:nr]]
        consts = [r[...] for r in refs[nr:nr + nc]]
        outs, accs = fn(rows, consts)
        for r, v in zip(refs[nr + nc:nr + nc + no], outs):
            r[...] = v.astype(r.dtype)
        if na:
            first = pl.program_id(0) == 0
            for r, v in zip(refs[nr + nc + no:], accs):
                @pl.when(first)
                def _(r=r, v=v):
                    r[...] = v

                @pl.when(jnp.logical_not(first))
                def _(r=r, v=v):
                    r[...] += v

    in_specs = [pl.BlockSpec((ts, a.shape[1]), lambda i: (i, 0)) for a in row_ins]
    in_specs += [pl.BlockSpec(c.shape, lambda i, nd=c.ndim: (0,) * nd) for c in const_ins]
    out_specs = [pl.BlockSpec((ts, w), lambda i: (i, 0)) for w, _ in row_outs]
    out_specs += [pl.BlockSpec(shp, lambda i, nd=len(shp): (0,) * nd) for shp in acc_outs]
    out_shape = [jax.ShapeDtypeStruct((s, w), dt) for w, dt in row_outs]
    out_shape += [jax.ShapeDtypeStruct(shp, F32) for shp in acc_outs]
    return _run(
        body, [*row_ins, *const_ins], carry=carry, name=name, grid=(s // ts,), in_specs=in_specs, out_specs=out_specs,
        out_shape=out_shape, compiler_params=_cparams(("arbitrary",)),
    )


def _rms(x, gain):
    r = lax.rsqrt(jnp.mean(x * x, axis=-1, keepdims=True) + EPS)
    n = x * r
    return n * gain, n, r


def _rms_bwd(dy, n, r, gain):
    dn = dy * gain
    dx = r * (dn - n * jnp.mean(dn * n, axis=-1, keepdims=True))
    return dx, jnp.sum(dy * n, axis=0, keepdims=True)


def _sigmoid(x):
    return 1.0 / (1.0 + jnp.exp(-x))


_GELU_K = math.sqrt(2.0 / math.pi)


def _gelu(x):
    t = jnp.tanh(_GELU_K * (x + 0.044715 * x * x * x))
    return 0.5 * x * (1.0 + t), t


def _gelu_grad(x, t):
    return 0.5 * (1.0 + t) + 0.5 * x * (1.0 - t * t) * _GELU_K * (1.0 + 3.0 * 0.044715 * x * x)


def _head_sum(x):
    parts = []
    for h in range(HEADS_PER_GROUP):
        sl = x[:, h * HEAD_DIM:(h + 1) * HEAD_DIM]
        parts.append(jnp.broadcast_to(jnp.sum(sl, axis=-1, keepdims=True), sl.shape))
    return jnp.concatenate(parts, axis=-1)


def _mix_weights(l0, l1, l2):
    mx = jnp.maximum(jnp.maximum(l0, l1), l2)
    e0, e1, e2 = jnp.exp(l0 - mx), jnp.exp(l1 - mx), jnp.exp(l2 - mx)
    inv = 1.0 / (e0 + e1 + e2)
    return e0 * inv, e1 * inv, e2 * inv


BLK = 128


def _slopes(g):
    return [2.0 ** (-8.0 * (g * HEADS_PER_GROUP + h + 1) / N_HEADS) for h in range(HEADS_PER_GROUP)]


def _attn_masks(dil):
    qi = lax.broadcasted_iota(jnp.int32, (BLK, BLK), 0)
    ki = lax.broadcasted_iota(jnp.int32, (BLK, BLK), 1)
    dist_c = qi - ki
    dist_p = BLK + qi - ki
    return dist_c >= 0, dist_p <= BLK, (dist_c * dil).astype(F32), (dist_p * dil).astype(F32)


def attn_fwd(qkv, g, name):
    dil, length, _ = qkv.shape
    scale = HEAD_DIM ** -0.5
    slopes = _slopes(g)

    def body(q_ref, kc_ref, vc_ref, kp_ref, vp_ref, o_ref, l_ref):
        n = pl.program_id(1)
        ok_c, ok_p, dc, dp = _attn_masks(dil)
        ok_p = jnp.logical_and(ok_p, n > 0)
        for h in range(HEADS_PER_GROUP):
            sl = slice(h * HEAD_DIM, (h + 1) * HEAD_DIM)
            q = q_ref[:, sl]
            s_c = lax.dot_general(q, kc_ref[:, sl], _DN["nt"], preferred_element_type=F32) * scale - slopes[h] * dc
            s_p = lax.dot_general(q, kp_ref[:, sl], _DN["nt"], preferred_element_type=F32) * scale - slopes[h] * dp
            s_c = jnp.where(ok_c, s_c, NEG)
            s_p = jnp.where(ok_p, s_p, NEG)
            mx = jnp.maximum(jnp.max(s_c, axis=-1, keepdims=True), jnp.max(s_p, axis=-1, keepdims=True))
            p_c = jnp.exp(s_c - mx)
            p_p = jnp.exp(s_p - mx)
            den = jnp.sum(p_c, axis=-1, keepdims=True) + jnp.sum(p_p, axis=-1, keepdims=True)
            acc = jnp.dot(p_c.astype(BF16), vc_ref[:, sl], preferred_element_type=F32)
            acc += jnp.dot(p_p.astype(BF16), vp_ref[:, sl], preferred_element_type=F32)
            o_ref[:, sl] = acc / den
            l_ref[:, sl] = jnp.broadcast_to(mx + jnp.log(den), (BLK, HEAD_DIM))

    def spec(col, prev):
        if prev:
            return pl.BlockSpec((None, BLK, GROUP_W), lambda r, n: (r, jnp.maximum(n - 1, 0), col))
        return pl.BlockSpec((None, BLK, GROUP_W), lambda r, n: (r, n, col))

    out_spec = pl.BlockSpec((None, BLK, GROUP_W), lambda r, n: (r, n, 0))
    return _pcall(
        body, name=name, grid=(dil, length // BLK),
        in_specs=[spec(0, False), spec(1, False), spec(2, False), spec(1, True), spec(2, True)],
        out_specs=[out_spec, out_spec],
        out_shape=[jax.ShapeDtypeStruct((dil, length, GROUP_W), F32)] * 2,
        compiler_params=_cparams(("parallel", "parallel")),
    )(qkv, qkv, qkv, qkv, qkv)


def attn_bwd(qkv, dout, lse, dd, g, name, carry=None):
    dil, length, _ = qkv.shape
    nblk = length // BLK
    scale = HEAD_DIM ** -0.5
    slopes = _slopes(g)

    def body(q_ref, kc_ref, vc_ref, kp_ref, vp_ref, qn_ref, do_ref, don_ref, l_ref, ln_ref, d_ref, dn_ref, o_ref):
        n = pl.program_id(1)
        ok_c, ok_p, dc, dp = _attn_masks(dil)
        ok_prev = jnp.logical_and(ok_p, n > 0)
        ok_next = jnp.logical_and(ok_p, n < nblk - 1)
        for h in range(HEADS_PER_GROUP):
            sl = slice(h * HEAD_DIM, (h + 1) * HEAD_DIM)
            q, kc, vc, kp, vp, qn = q_ref[:, sl], kc_ref[:, sl], vc_ref[:, sl], kp_ref[:, sl], vp_ref[:, sl], qn_ref[:, sl]
            do, don = do_ref[:, sl], don_ref[:, sl]
            lse_q, lse_n, dd_q, dd_n = l_ref[:, sl], ln_ref[:, sl], d_ref[:, sl], dn_ref[:, sl]

            def probs(qq, kk, dist, ok, lse_t):
                s = lax.dot_general(qq, kk, _DN["nt"], preferred_element_type=F32) * scale - slopes[h] * dist
                return jnp.where(ok, jnp.exp(jnp.where(ok, s, NEG) - lse_t), 0.0)

            p_c = probs(q, kc, dc, ok_c, lse_q)
            p_p = probs(q, kp, dp, ok_prev, lse_q)
            p_x = probs(qn, kc, dp, ok_next, lse_n)
            ds_c = p_c * (lax.dot_general(do, vc, _DN["nt"], preferred_element_type=F32) - dd_q)
            ds_p = p_p * (lax.dot_general(do, vp, _DN["nt"], preferred_element_type=F32) - dd_q)
            ds_x = p_x * (lax.dot_general(don, vc, _DN["nt"], preferred_element_type=F32) - dd_n)
            ds_c16, ds_p16, ds_x16 = ds_c.astype(BF16), ds_p.astype(BF16), ds_x.astype(BF16)
            dq = jnp.dot(ds_c16, kc, preferred_element_type=F32) + jnp.dot(ds_p16, kp, preferred_element_type=F32)
            dk = lax.dot_general(ds_c16, q, _DN["tn"], preferred_element_type=F32)
            dk += lax.dot_general(ds_x16, qn, _DN["tn"], preferred_element_type=F32)
            dv = lax.dot_general(p_c.astype(BF16), do, _DN["tn"], preferred_element_type=F32)
            dv += lax.dot_general(p_x.astype(BF16), don, _DN["tn"], preferred_element_type=F32)
            o_ref[:, h * HEAD_DIM:(h + 1) * HEAD_DIM] = (dq * scale).astype(BF16)
            o_ref[:, GROUP_W + h * HEAD_DIM:GROUP_W + (h + 1) * HEAD_DIM] = (dk * scale).astype(BF16)
            o_ref[:, 2 * GROUP_W + h * HEAD_DIM:2 * GROUP_W + (h + 1) * HEAD_DIM] = dv.astype(BF16)

    def spec(col, which):
        if which == "prev":
            return pl.BlockSpec((None, BLK, GROUP_W), lambda r, n: (r, jnp.maximum(n - 1, 0), col))
        if which == "next":
            return pl.BlockSpec((None, BLK, GROUP_W), lambda r, n: (r, jnp.minimum(n + 1, nblk - 1), col))
        return pl.BlockSpec((None, BLK, GROUP_W), lambda r, n: (r, n, col))

    return _run(
        body, [qkv, qkv, qkv, qkv, qkv, qkv, dout, dout, lse, lse, dd, dd], carry=carry, name=name, grid=(dil, nblk),
        in_specs=[spec(0, "cur"), spec(1, "cur"), spec(2, "cur"), spec(1, "prev"), spec(2, "prev"), spec(0, "next"),
                  spec(0, "cur"), spec(0, "next"), spec(0, "cur"), spec(0, "next"), spec(0, "cur"), spec(0, "next")],
        out_specs=pl.BlockSpec((None, BLK, 3 * GROUP_W), lambda r, n: (r, n, 0)),
        out_shape=jax.ShapeDtypeStruct((dil, length, 3 * GROUP_W), BF16),
        compiler_params=_cparams(("parallel", "parallel")),
    )


def _ssm_prep_values(are, aim, logdt):
    dt = jnp.exp(logdt)
    mag = jnp.exp(are * dt)
    lb_re, lb_im = mag * jnp.cos(aim * dt), mag * jnp.sin(aim * dt)
    inv = 1.0 / (are * are + aim * aim)
    n_re, n_im = lb_re - 1.0, lb_im
    f_re = (n_re * are + n_im * aim) * inv
    f_im = (n_im * are - n_re * aim) * inv
    return dt, lb_re, lb_im, f_re, f_im, inv


PREP_G = 8


def _group_specs(are, logdt, bre):
    def spec(a):
        return pl.BlockSpec((PREP_G,) + a.shape[1:], lambda i: (i, 0, 0))
    return spec(are), spec(logdt), spec(bre)


def ssm_prep(are, aim, logdt, bre, bim):
    def body(are_r, aim_r, ldt_r, bre_r, bim_r, lre_o, lim_o, bbre_o, bbim_o):
        _, lb_re, lb_im, f_re, f_im, _ = _ssm_prep_values(are_r[...], aim_r[...], ldt_r[...])
        lre_o[...] = lb_re
        lim_o[...] = lb_im
        bbre_o[...] = f_re * bre_r[...] - f_im * bim_r[...]
        bbim_o[...] = f_re * bim_r[...] + f_im * bre_r[...]

    sh1 = jax.ShapeDtypeStruct(are.shape, F32)
    shb = jax.ShapeDtypeStruct(bre.shape, F32)
    s1, sd, sb = _group_specs(are, logdt, bre)
    return _pcall(body, name="ssm_prep", grid=(SSM_GROUPS // PREP_G,), in_specs=[s1, s1, sd, sb, sb], out_specs=[s1, s1, sb, sb],
                  out_shape=[sh1, sh1, shb, shb], compiler_params=_cparams(("parallel",)))(are, aim, logdt, bre, bim)


def ssm_prep_bwd(are, aim, logdt, bre, bim, dbbre, dbbim, dlre, dlim):
    def body(are_r, aim_r, ldt_r, bre_r, bim_r, dbbre_r, dbbim_r, dlre_r, dlim_r, dare_o, daim_o, dldt_o, dbre_o, dbim_o):
        are_v, aim_v = are_r[...], aim_r[...]
        dt, lb_re, lb_im, f_re, f_im, inv = _ssm_prep_values(are_v, aim_v, ldt_r[...])
        b_re, b_im, g_re, g_im = bre_r[...], bim_r[...], dbbre_r[...], dbbim_r[...]
        dbre_o[...] = f_re * g_re + f_im * g_im
        dbim_o[...] = f_re * g_im - f_im * g_re
        df_re = jnp.sum(b_re * g_re + b_im * g_im, axis=-1, keepdims=True)
        df_im = jnp.sum(b_re * g_im - b_im * g_re, axis=-1, keepdims=True)
        il_re, il_im = are_v * inv, -aim_v * inv
        cl_re = dlre_r[...] + il_re * df_re + il_im * df_im
        cl_im = dlim_r[...] + il_re * df_im - il_im * df_re
        q_re = -(f_re * il_re - f_im * il_im)
        q_im = -(f_re * il_im + f_im * il_re)
        ca_re = q_re * df_re + q_im * df_im
        ca_im = q_re * df_im - q_im * df_re
        cz_re = lb_re * cl_re + lb_im * cl_im
        cz_im = lb_re * cl_im - lb_im * cl_re
        dare_o[...] = ca_re + dt * cz_re
        daim_o[...] = ca_im + dt * cz_im
        dldt_o[...] = dt * jnp.sum(are_v * cz_re + aim_v * cz_im, axis=1, keepdims=True)

    sh1 = jax.ShapeDtypeStruct(are.shape, F32)
    shb = jax.ShapeDtypeStruct(bre.shape, F32)
    s1, sd, sb = _group_specs(are, logdt, bre)
    return _pcall(
        body, name="ssm_prep_bwd", grid=(SSM_GROUPS // PREP_G,), in_specs=[s1, s1, sd, sb, sb, sb, sb, s1, s1],
        out_specs=[s1, s1, sd, sb, sb], out_shape=[sh1, sh1, jax.ShapeDtypeStruct(logdt.shape, F32), shb, shb],
        compiler_params=_cparams(("parallel",)),
    )(are, aim, logdt, bre, bim, dbbre, dbbim, dlre, dlim)


SCAN_WC = 512


def ssm_scan(xre, xim, lre, lim, name, reverse=False, states=None, carry=None):
    s = xre.shape[0]
    steps = s // SEGS
    assert steps & (steps - 1) == 0
    tt = min(128, steps)
    nch = steps // tt
    rows = tt * SEGS
    with_dl = states is not None
    nsq = int(math.log2(steps))

    def body(*refs):
        xre_r, xim_r, lre_r, lim_r = refs[:4]
        k = 4
        if with_dl:
            hre_r, him_r, pre_r, pim_r, cre_r, cim_r = refs[k:k + 6]
            k += 6
        ore_r, oim_r, hin_re_o, hin_im_o = refs[k:k + 4]
        k += 4
        if with_dl:
            dlre_o, dlim_o = refs[k:k + 2]
            k += 2
        st_re, st_im = refs[k], refs[k + 1]
        ps, ch = pl.program_id(1), pl.program_id(2)
        a_re = jnp.broadcast_to(lre_r[...], (SEGS, SCAN_WC))
        a_im = jnp.broadcast_to(lim_r[...], (SEGS, SCAN_WC))
        if reverse:
            a_im = -a_im

        @pl.when(jnp.logical_and(ps == 0, ch == 0))
        def _():
            st_re[...] = jnp.zeros_like(st_re)
            st_im[...] = jnp.zeros_like(st_im)

        @pl.when(jnp.logical_and(ps == 1, ch == 0))
        def _():
            p_re, p_im = a_re, a_im
            for _ in range(nsq):
                p_re, p_im = p_re * p_re - p_im * p_im, 2.0 * p_re * p_im
            e_re, e_im = st_re[...], st_im[...]
            row = lax.broadcasted_iota(jnp.int32, (SEGS, SCAN_WC), 0)
            edge = (row == SEGS - 1) if reverse else (row == 0)
            c_re, c_im = jnp.zeros_like(e_re), jnp.zeros_like(e_im)
            for _ in range(SEGS - 1):
                n_re = p_re * c_re - p_im * c_im + e_re
                n_im = p_re * c_im + p_im * c_re + e_im
                sh = SEGS - 1 if reverse else 1
                c_re = jnp.where(edge, 0.0, pltpu.roll(n_re, sh, 0))
                c_im = jnp.where(edge, 0.0, pltpu.roll(n_im, sh, 0))
            st_re[...] = c_re
            st_im[...] = c_im
            hin_re_o[...] = c_re
            hin_im_o[...] = c_im
            if with_dl:
                dlre_o[...] = jnp.zeros_like(dlre_o)
                dlim_o[...] = jnp.zeros_like(dlim_o)

        def run(store):
            def step(i, carry):
                if with_dl and store:
                    h_re, h_im, d_re, d_im = carry
                else:
                    h_re, h_im = carry
                t = (tt - 1 - i) if reverse else i
                off = pl.multiple_of(t * SEGS, SEGS)
                n_re = a_re * h_re - a_im * h_im + xre_r[pl.ds(off, SEGS), :]
                n_im = a_re * h_im + a_im * h_re + xim_r[pl.ds(off, SEGS), :]
                if store:
                    ore_r[pl.ds(off, SEGS), :] = n_re
                    oim_r[pl.ds(off, SEGS), :] = n_im
                if with_dl and store:
                    offp = pl.multiple_of(jnp.maximum(t - 1, 0) * SEGS, SEGS)
                    in_re, in_im = hre_r[pl.ds(offp, SEGS), :], him_r[pl.ds(offp, SEGS), :]
                    first_chunk = ch == nch - 1
                    edge_re = jnp.where(first_chunk, cre_r[...], pre_r[...])
                    edge_im = jnp.where(first_chunk, cim_r[...], pim_r[...])
                    hp_re = jnp.where(t == 0, edge_re, in_re)
                    hp_im = jnp.where(t == 0, edge_im, in_im)
                    d_re = d_re + hp_re * n_re + hp_im * n_im
                    d_im = d_im + hp_re * n_im - hp_im * n_re
                    return n_re, n_im, d_re, d_im
                return n_re, n_im

            init = (st_re[...], st_im[...])
            if with_dl and store:
                init = init + (dlre_o[...], dlim_o[...])
            fin = lax.fori_loop(0, tt, step, init)
            st_re[...] = fin[0]
            st_im[...] = fin[1]
            if with_dl and store:
                dlre_o[...] = fin[2]
                dlim_o[...] = fin[3]

        @pl.when(ps == 0)
        def _():
            run(False)

        @pl.when(ps == 1)
        def _():
            run(True)

    def chunk(c):
        return (nch - 1 - c) if reverse else c

    x_spec = pl.BlockSpec((rows, SCAN_WC), lambda j, ps, c: (chunk(c), j))
    l_spec = pl.BlockSpec((1, SCAN_WC), lambda j, ps, c: (0, j))
    o_spec = pl.BlockSpec((rows, SCAN_WC), lambda j, ps, c: (jnp.where(ps == 1, chunk(c), chunk(0)), j))
    e_spec = pl.BlockSpec((SEGS, SCAN_WC), lambda j, ps, c: (0, j))
    in_specs = [x_spec, x_spec, l_spec, l_spec]
    args = [xre, xim, lre, lim]
    out_specs = [o_spec, o_spec, e_spec, e_spec]
    out_shape = [jax.ShapeDtypeStruct((s, STATE_W), F32)] * 2 + [jax.ShapeDtypeStruct((SEGS, STATE_W), F32)] * 2
    if with_dl:
        prev_spec = pl.BlockSpec((SEGS, SCAN_WC), lambda j, ps, c: (jnp.maximum(chunk(c) * tt - 1, 0), j))
        in_specs += [x_spec, x_spec, prev_spec, prev_spec, e_spec, e_spec]
        args += [states[0], states[1], states[0], states[1], states[2], states[3]]
        out_specs += [e_spec, e_spec]
        out_shape += [jax.ShapeDtypeStruct((SEGS, STATE_W), F32)] * 2
    return _run(
        body, args, carry=carry, name=name, grid=(STATE_W // SCAN_WC, 2, nch), in_specs=in_specs, out_specs=out_specs,
        out_shape=out_shape, scratch_shapes=[pltpu.VMEM((SEGS, SCAN_WC), F32)] * 2,
        compiler_params=_cparams(("parallel", "arbitrary", "arbitrary")),
    )


def _block_diag(m):
    g, r, c = m.shape
    m = m.reshape(BD, g // BD, r, c)
    eye = jnp.eye(g // BD, dtype=m.dtype)
    return jnp.einsum("jarc,ab->jarbc", m, eye).reshape(BD, (g // BD) * r, (g // BD) * c)


def _block_diag_extract(m, r, c):
    per = m.shape[1] // r
    m = m.reshape(BD, per, r, per, c)
    return jnp.einsum("jarac->jarc", m).reshape(BD * per, r, c)


def dilate(a, d):
    s, w = a.shape
    if d == 1:
        return a.reshape(1, s, w)
    return a.reshape(s // d, d, w).transpose(1, 0, 2)


def undilate(a):
    d, length, w = a.shape
    if d == 1:
        return a.reshape(length, w)
    return a.transpose(1, 0, 2).reshape(d * length, w)


def to_segments(a):
    s, w = a.shape
    return a.reshape(SEGS, s // SEGS, w).transpose(1, 0, 2).reshape(s, w)


def from_segments(a):
    s, w = a.shape
    return a.reshape(s // SEGS, SEGS, w).transpose(1, 0, 2).reshape(s, w)


W_IN_CHUNKS = 4


def local_step(x, target, shards, small):
    s = x.shape[0]
    g1, g2, g3, g4 = (small[k].reshape(1, D_MODEL) for k in ("norm_mix_pre", "norm_mix_post", "norm_ffn_pre", "norm_ffn_post"))
    dvec = small["ssm_d"].reshape(1, SSM_W)
    wts, recv = {}, {}

    def gathered(names, blocks):
        for n, b in zip(names, blocks):
            wts[n] = _full_from_gathered(b, n)

    (h,), got = rowwise("rms_in", lambda r, c: ([_rms(r[0], c[0])[0]], []), [x], [g1], [(D_MODEL, BF16)],
                        carry=Gather([shards["w_in"]]))
    w_in_f = _full_from_gathered(got[0], "w_in")
    w_qkv = [jnp.concatenate([w_in_f[:, o + g * GROUP_W:o + (g + 1) * GROUP_W] for o in (0, HQ, 2 * HQ)], axis=1) for g in range(3)]
    w_u, w_gates = w_in_f[:, 3 * HQ:3 * HQ + SSM_W], w_in_f[:, 3 * HQ + SSM_W:]
    hd = [h.reshape(1, s, D_MODEL), dilate(h, 4), dilate(h, 16)]
    qkv = [None] * 3
    names = ("w_attn_up", "w_glu_v", "w_glu_g")
    qkv[0], got = mm([(hd[0].reshape(s, D_MODEL), w_qkv[0])], "nn", BF16, "mm_qkv0", carry=Gather([shards[n] for n in names]))
    gathered(names, got)
    qkv[1], got = mm([(hd[1].reshape(s, D_MODEL), w_qkv[1])], "nn", BF16, "mm_qkv1", carry=Gather([shards["w_out"]]))
    gathered(("w_out",), got)
    qkv[2] = mm([(hd[2].reshape(s, D_MODEL), w_qkv[2])], "nn", BF16, "mm_qkv2")
    u = mm([(h, w_u)], "nn", F32, "mm_u")
    gates, got = mm([(h, w_gates)], "nn", F32, "mm_gates", carry=Gather([shards["w_ffn_gate"]]))
    gathered(("w_ffn_gate",), got)

    outs, lses = [], []
    for g, (_, dil) in enumerate(ATTN_GROUPS):
        o, l = attn_fwd(qkv[g].reshape(dil, s // dil, 3 * GROUP_W), g, f"attn_fwd{g}")
        outs.append(undilate(o))
        lses.append(undilate(l))

    def merge_fn(r, c):
        w0, w1, w2 = _mix_weights(r[3], r[4], r[5])
        return [w0 * r[0] + w1 * r[1] + w2 * r[2]], []

    (attn,) = rowwise("attn_merge", merge_fn, outs + lses, [], [(GROUP_W, BF16)])
    attn_branch = mm([(attn, wts["w_attn_up"])], "nn", F32, "mm_up")

    are3 = small["ssm_a_re"].reshape(SSM_GROUPS, SSM_STATE, 1)
    aim3 = small["ssm_a_im"].reshape(SSM_GROUPS, SSM_STATE, 1)
    ldt3 = small["ssm_log_dt"].reshape(SSM_GROUPS, 1, 1)
    bre3 = small["ssm_b_re"].reshape(SSM_GROUPS, SSM_STATE, SSM_GROUP)
    bim3 = small["ssm_b_im"].reshape(SSM_GROUPS, SSM_STATE, SSM_GROUP)
    cre3 = small["ssm_c_re"].reshape(SSM_GROUPS, SSM_GROUP, SSM_STATE)
    cim3 = small["ssm_c_im"].reshape(SSM_GROUPS, SSM_GROUP, SSM_STATE)
    lre3, lim3, bbre, bbim = ssm_prep(are3, aim3, ldt3, bre3, bim3)
    lre, lim = lre3.reshape(1, STATE_W), lim3.reshape(1, STATE_W)
    w_bre = _block_diag(bbre.transpose(0, 2, 1)).astype(BF16)
    w_bim = _block_diag(bbim.transpose(0, 2, 1)).astype(BF16)
    w_cre = _block_diag(cre3.transpose(0, 2, 1)).astype(BF16)
    w_cim = _block_diag(cim3.transpose(0, 2, 1)).astype(BF16)
    u_s = to_segments(u)
    bu_re = bdmm([(u_s, w_bre)], "ssm_bu_re")
    bu_im = bdmm([(u_s, w_bim)], "ssm_bu_im")
    names = ("w_ffn_up", "w_ffn_down")
    (h_re, h_im, hin_re, hin_im), got = ssm_scan(bu_re, bu_im, lre, lim, "ssm_scan_fwd", carry=Gather([shards[n] for n in names]))
    gathered(names, got)
    y_lin = bdmm([(h_re, w_cre), (h_im, -w_cim)], "ssm_y")

    def gelu_fn(r, c):
        y = r[0] + c[0] * r[1]
        return [_gelu(y)[0], y], []

    yg_s, y_ssm = rowwise("ssm_gelu", gelu_fn, [y_lin, u_s], [dvec], [(SSM_W, BF16), (SSM_W, F32)])
    yg = from_segments(yg_s)
    gv = mm([(yg, wts["w_glu_v"])], "nn", F32, "mm_glu_v")
    gg = mm([(yg, wts["w_glu_g"])], "nn", F32, "mm_glu_g")

    def gate_fn(r, c):
        gts, ab, gv_, gg_ = r
        sa, ss = _sigmoid(gts[:, :D_MODEL]), _sigmoid(gts[:, D_MODEL:])
        return [sa * ab + ss * (gv_ * _sigmoid(gg_))], []

    (merged,) = rowwise("gate_merge", gate_fn, [gates, attn_branch, gv, gg], [], [(D_MODEL, BF16)])
    o_mix = mm([(merged, wts["w_out"])], "nn", F32, "mm_out")

    def mid_fn(r, c):
        x1 = r[0] + _rms(r[1], c[0])[0]
        return [x1, _rms(x1, c[1])[0]], []

    x1, h2 = rowwise("rms_mid", mid_fn, [x, o_mix], [g2, g3], [(D_MODEL, F32), (D_MODEL, BF16)])
    fa = mm([(h2, wts["w_ffn_gate"])], "nn", F32, "mm_ffn_gate")
    fb = mm([(h2, wts["w_ffn_up"])], "nn", F32, "mm_ffn_up")
    (fin,) = rowwise("swiglu", lambda r, c: ([r[0] * _sigmoid(r[0]) * r[1]], []), [fa, fb], [], [(D_FF, BF16)])
    f = mm([(fin, wts["w_ffn_down"])], "nn", F32, "mm_ffn_down")

    def loss_fn(r, c):
        x1_, f_, tgt = r
        y, n, rr = _rms(f_, c[0])
        err = x1_ + y - tgt
        dout = err * (1.0 / D_MODEL)
        df, dg = _rms_bwd(dout, n, rr, c[0])
        lp = 0.5 * jnp.sum(jnp.sum(err * err, axis=-1, keepdims=True) * (1.0 / D_MODEL), axis=0, keepdims=True)
        return [df, dout], [dg, lp]

    df, dout, dg4, loss_part = rowwise("loss_bwd", loss_fn, [x1, f, target], [g4], [(D_MODEL, BF16), (D_MODEL, F32)],
                                       acc_outs=[(1, D_MODEL), (1, 1)])
    def sent(names, blocks):
        for n, b in zip(names, blocks):
            recv[n] = b

    def to_owners(names, dws):
        return AllToAll([_split_for_devices(d, n) for n, d in zip(names, dws)])

    dfin = mm([(df, wts["w_ffn_down"])], "nt", F32, "mm_d_fin")
    dw_ffn_down = mm([(fin, df)], "tn", BF16, "mm_dw_ffn_down")

    def swiglu_bwd(r, c):
        dfin_, a, b = r
        sg = _sigmoid(a)
        return [dfin_ * b * (sg * (1.0 + a * (1.0 - sg))), dfin_ * a * sg], []

    da, db = rowwise("swiglu_bwd", swiglu_bwd, [dfin, fa, fb], [], [(D_FF, BF16), (D_FF, BF16)])
    dw_ffn_gate, got = mm([(h2, da)], "tn", BF16, "mm_dw_ffn_gate", carry=to_owners(["w_ffn_down"], [dw_ffn_down]))
    sent(["w_ffn_down"], got)
    dw_ffn_up, got = mm([(h2, db)], "tn", BF16, "mm_dw_ffn_up", carry=to_owners(["w_ffn_gate"], [dw_ffn_gate]))
    sent(["w_ffn_gate"], got)
    dh2, got = mm([(da, wts["w_ffn_gate"]), (db, wts["w_ffn_up"])], "nt", F32, "mm_d_h2", carry=to_owners(["w_ffn_up"], [dw_ffn_up]))
    sent(["w_ffn_up"], got)

    def mid_bwd(r, c):
        dh2_, dout_, x1_, o_ = r
        _, n3, r3 = _rms(x1_, c[1])
        dx1, dg3_ = _rms_bwd(dh2_, n3, r3, c[1])
        dx1 = dx1 + dout_
        _, n2, r2 = _rms(o_, c[0])
        do_, dg2_ = _rms_bwd(dx1, n2, r2, c[0])
        return [dx1, do_], [dg2_, dg3_]

    dx1, do_mix, dg2, dg3 = rowwise("rms_mid_bwd", mid_bwd, [dh2, dout, x1, o_mix], [g2, g3], [(D_MODEL, F32), (D_MODEL, BF16)],
                                    acc_outs=[(1, D_MODEL), (1, D_MODEL)])
    dmerged = mm([(do_mix, wts["w_out"])], "nt", F32, "mm_d_merged")
    dw_out = mm([(merged, do_mix)], "tn", BF16, "mm_dw_out")

    def gate_bwd(r, c):
        dm, gts, ab, gv_, gg_ = r
        sa, ss, sg = _sigmoid(gts[:, :D_MODEL]), _sigmoid(gts[:, D_MODEL:]), _sigmoid(gg_)
        branch = gv_ * sg
        dbranch = dm * ss
        dgates = jnp.concatenate([dm * ab * sa * (1.0 - sa), dm * branch * ss * (1.0 - ss)], axis=-1)
        return [dgates, dm * sa, dbranch * sg, dbranch * gv_ * sg * (1.0 - sg)], []

    dgates, dab, dgv, dgg = rowwise("gate_bwd", gate_bwd, [dmerged, gates, attn_branch, gv, gg], [],
                                    [(2 * D_MODEL, BF16), (D_MODEL, BF16), (D_MODEL, BF16), (D_MODEL, BF16)])
    dattn = mm([(dab, wts["w_attn_up"])], "nt", F32, "mm_d_attn")
    dw_up = mm([(attn, dab)], "tn", BF16, "mm_dw_up")
    dyg = mm([(dgv, wts["w_glu_v"]), (dgg, wts["w_glu_g"])], "nt", F32, "mm_d_yg")
    dw_glu_v = mm([(yg, dgv)], "tn", BF16, "mm_dw_glu_v")
    dw_glu_g = mm([(yg, dgg)], "tn", BF16, "mm_dw_glu_g")

    def gelu_bwd(r, c):
        dyg_, y, us = r
        dy = dyg_ * _gelu_grad(y, _gelu(y)[1])
        return [dy, c[0] * dy], [jnp.sum(dy * us, axis=0, keepdims=True)]

    dy_ssm, du_skip, dd_ssm = rowwise("ssm_gelu_bwd", gelu_bwd, [to_segments(dyg), y_ssm, u_s], [dvec],
                                      [(SSM_W, F32), (SSM_W, F32)], acc_outs=[(1, SSM_W)])
    w_cre_t, w_cim_t = w_cre.transpose(0, 2, 1), w_cim.transpose(0, 2, 1)
    gin_re = bdmm([(dy_ssm, w_cre_t)], "ssm_gin_re")
    gin_im = bdmm([(dy_ssm, -w_cim_t)], "ssm_gin_im")
    names = ["w_out", "w_attn_up", "w_glu_v", "w_glu_g"]
    (g_re, g_im, _, _, dl_re8, dl_im8), got = ssm_scan(
        gin_re, gin_im, lre, lim, "ssm_scan_bwd", reverse=True, states=(h_re, h_im, hin_re, hin_im),
        carry=to_owners(names, [dw_out, dw_up, dw_glu_v, dw_glu_g]))
    sent(names, got)
    du_s = bdmm([(g_re, w_bre.transpose(0, 2, 1)), (g_im, w_bim.transpose(0, 2, 1))], "ssm_du", add=du_skip)
    dbb_re = _block_diag_extract(bd_tn(u_s, g_re, "ssm_dbb_re"), SSM_GROUP, SSM_STATE).transpose(0, 2, 1)
    dbb_im = _block_diag_extract(bd_tn(u_s, g_im, "ssm_dbb_im"), SSM_GROUP, SSM_STATE).transpose(0, 2, 1)
    dc_re = _block_diag_extract(bd_tn(h_re, dy_ssm, "ssm_dc_re"), SSM_STATE, SSM_GROUP).transpose(0, 2, 1)
    dc_im = -_block_diag_extract(bd_tn(h_im, dy_ssm, "ssm_dc_im"), SSM_STATE, SSM_GROUP).transpose(0, 2, 1)

    def fold8(r, c):
        return [], [jnp.sum(r[0], axis=0, keepdims=True), jnp.sum(r[1], axis=0, keepdims=True)]

    dl_re, dl_im = rowwise("ssm_dl_fold", fold8, [dl_re8, dl_im8], [], [], acc_outs=[(1, STATE_W), (1, STATE_W)], ts=SEGS)
    da_re, da_im, dldt, db_re, db_im = ssm_prep_bwd(
        are3, aim3, ldt3, bre3, bim3, dbb_re, dbb_im,
        dl_re.reshape(SSM_GROUPS, SSM_STATE, 1), dl_im.reshape(SSM_GROUPS, SSM_STATE, 1))
    du = from_segments(du_s)

    def merge_bwd(r, c):
        dat, o0, o1, o2, l0, l1, l2 = r
        w0, w1, w2 = _mix_weights(l0, l1, l2)
        tot = _head_sum(dat * (w0 * o0 + w1 * o1 + w2 * o2))
        return [w0 * dat, w1 * dat, w2 * dat, w0 * tot, w1 * tot, w2 * tot], []

    mb = rowwise("attn_merge_bwd", merge_bwd, [dattn] + outs + lses, [], [(GROUP_W, BF16)] * 3 + [(GROUP_W, F32)] * 3)
    dqs, dw_qkv = [], []
    for g, (_, dil) in enumerate(ATTN_GROUPS):
        dq = attn_bwd(qkv[g].reshape(dil, s // dil, 3 * GROUP_W), dilate(mb[g], dil), dilate(lses[g], dil),
                      dilate(mb[3 + g], dil), g, f"attn_bwd{g}").reshape(s, 3 * GROUP_W)
        dqs.append(dq)
        dw_qkv.append(mm([(hd[g].reshape(s, D_MODEL), dq)], "tn", BF16, f"mm_dw_qkv{g}"))
    dw_u = mm([(h, du)], "tn", BF16, "mm_dw_u")
    dw_gates = mm([(h, dgates)], "tn", BF16, "mm_dw_gates")
    dw_in = jnp.concatenate(
        [dw_qkv[g][:, o * GROUP_W:(o + 1) * GROUP_W] for o in range(3) for g in range(3)] + [dw_u, dw_gates], axis=1)
    dw_in_split = _split_for_devices(dw_in, "w_in")
    rows = D_MODEL // W_IN_CHUNKS
    chunks = [AllToAll([dw_in_split[:, i * rows:(i + 1) * rows]]) for i in range(W_IN_CHUNKS)]
    dh_parts, got_chunks = [], []
    for g, (_, dil) in enumerate(ATTN_GROUPS):
        dh_g, got = mm([(dqs[g], w_qkv[g])], "nt", F32, f"mm_d_h_qkv{g}", carry=chunks[g])
        got_chunks.append(got[0])
        dh_parts.append(undilate(dh_g.reshape(dil, s // dil, D_MODEL)))
    dh_parts.append(mm([(du, w_u)], "nt", F32, "mm_d_h_u"))
    dh_gates, got = mm([(dgates, w_gates)], "nt", F32, "mm_d_h_gates", carry=chunks[3])
    got_chunks.append(got[0])
    dh_parts.append(dh_gates)
    recv["w_in"] = jnp.concatenate(got_chunks, axis=1)

    def in_bwd(r, c):
        dh = r[0] + r[1] + r[2] + r[3] + r[4]
        _, n1, r1 = _rms(r[6], c[0])
        dx, dg1_ = _rms_bwd(dh, n1, r1, c[0])
        return [dx + r[5]], [dg1_]

    grad_x, dg1 = rowwise("rms_in_bwd", in_bwd, dh_parts + [dx1, x], [g1], [(D_MODEL, F32)], acc_outs=[(1, D_MODEL)])

    dsmall = dict(norm_mix_pre=dg1, ssm_a_re=da_re, ssm_a_im=da_im, ssm_log_dt=dldt, ssm_b_re=db_re, ssm_b_im=db_im,
                  ssm_c_re=dc_re, ssm_c_im=dc_im, ssm_d=dd_ssm, norm_mix_post=dg2, norm_ffn_pre=dg3, norm_ffn_post=dg4)
    return loss_part, grad_x, recv, dsmall


def adamw(parts, w, m, v, name):
    r, c = w.shape
    tr = r
    while tr > 8 and tr % 2 == 0 and tr * c * (8 * parts.dtype.itemsize + 28) * 2 > 24 * 1024 * 1024:
        tr //= 2
    assert r % tr == 0 and (tr % 8 == 0 or tr == r)
    c1, c2 = 1.0 / (1.0 - ADAM_B1 ** ADAM_STEP), 1.0 / (1.0 - ADAM_B2 ** ADAM_STEP)

    def body(p_ref, w_ref, m_ref, v_ref, g_o, d_o, m_o, v_o):
        g = p_ref[0].astype(F32)
        for i in range(1, N_DEV):
            g = g + p_ref[i].astype(F32)
        mn = ADAM_B1 * m_ref[...] + (1.0 - ADAM_B1) * g
        vn = ADAM_B2 * v_ref[...] + (1.0 - ADAM_B2) * (g * g)
        g_o[...] = g
        m_o[...] = mn
        v_o[...] = vn
        d_o[...] = -ADAM_LR * ((mn * c1) / (jnp.sqrt(vn * c2) + ADAM_EPS) + ADAM_WD * w_ref[...])

    blk = pl.BlockSpec((tr, c), lambda i: (i, 0))
    return _pcall(
        body, name=name, grid=(r // tr,), in_specs=[pl.BlockSpec((N_DEV, tr, c), lambda i: (0, i, 0)), blk, blk, blk],
        out_specs=[blk] * 4, out_shape=[jax.ShapeDtypeStruct((r, c), F32)] * 4, compiler_params=_cparams(("parallel",)),
    )(parts, w, m, v)


PACK_C = 1024
SHARDED = ("w_in", "w_attn_up", "w_glu_v", "w_glu_g", "w_out", "w_ffn_gate", "w_ffn_up", "w_ffn_down")
ROW_SHARDED = ("w_out", "w_ffn_down")
SMALL = ("norm_mix_pre", "ssm_a_re", "ssm_a_im", "ssm_log_dt", "ssm_b_re", "ssm_b_im", "ssm_c_re", "ssm_c_im", "ssm_d",
         "norm_mix_post", "norm_ffn_pre", "norm_ffn_post")
WEIGHTS = ("norm_mix_pre", "w_in", "w_attn_up", "ssm_a_re", "ssm_a_im", "ssm_log_dt", "ssm_b_re", "ssm_b_im", "ssm_c_re",
           "ssm_c_im", "ssm_d", "w_glu_v", "w_glu_g", "w_out", "norm_mix_post", "norm_ffn_pre", "w_ffn_gate", "w_ffn_up",
           "w_ffn_down", "norm_ffn_post")


def _pack(arrs, dtype, pad_rows_to=64):
    flat = jnp.concatenate([a.reshape(-1).astype(dtype) for a in arrs])
    n = flat.shape[0]
    rows = -(-n // PACK_C)
    rows = -(-rows // pad_rows_to) * pad_rows_to
    return jnp.pad(flat, (0, rows * PACK_C - n)).reshape(rows, PACK_C)


def _unpack(flat2d, shapes):
    flat = flat2d.reshape(-1)
    out, off = [], 0
    for shp in shapes:
        n = int(np.prod(shp))
        out.append(flat[off:off + n].reshape(shp))
        off += n
    return out


def _full_from_gathered(gathered, name):
    if name in ROW_SHARDED:
        return gathered.reshape(-1, gathered.shape[2])
    return gathered.transpose(1, 0, 2).reshape(gathered.shape[1], -1)


def _split_for_devices(full, name):
    if name in ROW_SHARDED:
        return full.reshape(N_DEV, -1, full.shape[1])
    return full.reshape(full.shape[0], N_DEV, -1).transpose(1, 0, 2)


def kernel(x, norm_mix_pre, w_in, w_attn_up, ssm_a_re, ssm_a_im, ssm_log_dt, ssm_b_re, ssm_b_im, ssm_c_re, ssm_c_im, ssm_d, w_glu_v, w_glu_g, w_out, norm_mix_post, norm_ffn_pre, w_ffn_gate, w_ffn_up, w_ffn_down, norm_ffn_post, loss_target, m_norm_mix_pre, m_w_in, m_w_attn_up, m_ssm_a_re, m_ssm_a_im, m_ssm_log_dt, m_ssm_b_re, m_ssm_b_im, m_ssm_c_re, m_ssm_c_im, m_ssm_d, m_w_glu_v, m_w_glu_g, m_w_out, m_norm_mix_post, m_norm_ffn_pre, m_w_ffn_gate, m_w_ffn_up, m_w_ffn_down, m_norm_ffn_post, v_norm_mix_pre, v_w_in, v_w_attn_up, v_ssm_a_re, v_ssm_a_im, v_ssm_log_dt, v_ssm_b_re, v_ssm_b_im, v_ssm_c_re, v_ssm_c_im, v_ssm_d, v_w_glu_v, v_w_glu_g, v_w_out, v_norm_mix_post, v_norm_ffn_pre, v_w_ffn_gate, v_w_ffn_up, v_w_ffn_down, v_norm_ffn_post):
    args = dict(locals())
    wv = {n: args[n][0] for n in WEIGHTS}
    mv = {n: args["m_" + n][0] for n in WEIGHTS}
    vv = {n: args["v_" + n][0] for n in WEIGHTS}

    shards = {n: wv[n].astype(BF16) for n in SHARDED}
    small = {n: wv[n] for n in SMALL}
    loss_part, grad_x, recv, dsmall = local_step(x[0], loss_target[0], shards, small)

    res = {}
    for n in SHARDED:
        res[n] = adamw(recv[n], wv[n], mv[n], vv[n], "adamw_" + n)

    small_shapes = [wv[n].shape for n in SMALL]
    (sgather,) = exchange(Gather([_pack([dsmall[n] for n in SMALL], F32)]), "gather_small_grads")
    sres = adamw(sgather, _pack([wv[n] for n in SMALL], F32), _pack([mv[n] for n in SMALL], F32),
                 _pack([vv[n] for n in SMALL], F32), "adamw_small")
    sun = [_unpack(t, small_shapes) for t in sres]
    for k, n in enumerate(SMALL):
        res[n] = tuple(sun[t][k] for t in range(4))

    loss = lax.psum(loss_part[0, 0], ("x", "y", "c"))
    outs = [loss, grad_x[None]]
    for t in range(4):
        outs += [res[n][t][None] for n in WEIGHTS]
    return tuple(outs)
```

```python
import functools
import math

import numpy as np
import jax
import jax.numpy as jnp
from jax import lax
from jax.experimental import pallas as pl
from jax.experimental.pallas import tpu as pltpu

F32 = jnp.float32
BF16 = jnp.bfloat16

D_MODEL = 2048
HEAD_DIM = 128
HEADS_PER_GROUP = 4
ATTN_GROUPS = ((128, 1), (512, 4), (2048, 16))
N_HEADS = HEADS_PER_GROUP * len(ATTN_GROUPS)
GROUP_W = HEADS_PER_GROUP * HEAD_DIM
HQ = N_HEADS * HEAD_DIM
SSM_W = 1024
SSM_GROUP = 16
SSM_GROUPS = 64
SSM_STATE = 64
STATE_W = SSM_GROUPS * SSM_STATE
D_FF = 5632
EPS = 1e-6
N_DEV = 8
SEGS = 8
BD = 8

ADAM_LR, ADAM_B1, ADAM_B2, ADAM_EPS, ADAM_WD, ADAM_STEP = 0.001, 0.9, 0.999, 1e-08, 0.01, 10

VMEM_LIMIT = 56 * 1024 * 1024
HBM_SPEC = pl.BlockSpec(memory_space=pltpu.HBM)
MESH_ID = pl.DeviceIdType.MESH
NEG = -1e30


def _pcall(body, **kw):
    return pl.pallas_call(body, **kw)


def _cparams(sem=None):
    if sem is None:
        return pltpu.CompilerParams(vmem_limit_bytes=VMEM_LIMIT)
    return pltpu.CompilerParams(vmem_limit_bytes=VMEM_LIMIT, dimension_semantics=sem)


def _my_coords():
    return lax.axis_index("x"), lax.axis_index("y"), lax.axis_index("c")


class Gather:
    def __init__(self, xs):
        self.arrays = list(xs)
        self.out_shapes = [jax.ShapeDtypeStruct((N_DEV,) + x.shape, x.dtype) for x in xs]

    def _ctx(self, out_refs, send_sems, recv_sems):
        mx, my, mc = _my_coords()
        me, sibling = (mx, my, mc), (mx, my, 1 - mc)
        chips = [(1 - mx, my), (mx, 1 - my), (1 - mx, 1 - my)]

        def slot(a, px, py, pc):
            return out_refs[a].at[4 * px + 2 * py + pc]

        def copy(a, k, block, to, src=None):
            return pltpu.make_async_remote_copy(
                src_ref=slot(a, *block) if src is None else src, dst_ref=slot(a, *block),
                send_sem=send_sems.at[7 * a + k], recv_sem=recv_sems.at[7 * a + k], device_id=to, device_id_type=MESH_ID)

        return me, sibling, chips, mc, slot, copy

    def _first(self, a, x_refs, ctx):
        me, sibling, chips, mc, slot, copy = ctx
        return [copy(a, 0, me, sibling, src=x_refs[a])] + [copy(a, 1 + j, me, (*chip, mc), src=x_refs[a]) for j, chip in enumerate(chips)]

    def start(self, x_refs, out_refs, send_sems, recv_sems, local_sems):
        ctx = self._ctx(out_refs, send_sems, recv_sems)
        me, slot = ctx[0], ctx[4]
        for a in range(len(self.arrays)):
            pltpu.make_async_copy(x_refs[a], slot(a, *me), local_sems.at[a]).start()
            for cp in self._first(a, x_refs, ctx):
                cp.start()

    def finish(self, x_refs, out_refs, send_sems, recv_sems, local_sems):
        ctx = self._ctx(out_refs, send_sems, recv_sems)
        me, sibling, chips, mc, slot, copy = ctx
        na = len(self.arrays)
        passed = []
        for a in range(na):
            for j, chip in enumerate(chips):
                copy(a, 1 + j, (*chip, mc), me).wait_recv()
                fwd = copy(a, 4 + j, (*chip, mc), sibling)
                fwd.start()
                passed.append(fwd)
        for a in range(na):
            copy(a, 0, sibling, me).wait_recv()
            for j, chip in enumerate(chips):
                copy(a, 4 + j, (*chip, 1 - mc), me).wait_recv()
        for a in range(na):
            for cp in self._first(a, x_refs, ctx):
                cp.wait_send()
        for cp in passed:
            cp.wait_send()
        for a in range(na):
            pltpu.make_async_copy(x_refs[a], slot(a, *me), local_sems.at[a]).wait()


class AllToAll:
    def __init__(self, ps):
        self.arrays = list(ps)
        self.out_shapes = [jax.ShapeDtypeStruct(p.shape, p.dtype) for p in ps]

    def _copies(self, p_refs, out_refs, send_sems, recv_sems, local_sems):
        mx, my, mc = _my_coords()
        me = 4 * mx + 2 * my + mc
        local, remote = [], []
        for a in range(len(self.arrays)):
            local.append(pltpu.make_async_copy(p_refs[a].at[me], out_refs[a].at[me], local_sems.at[a]))
            for k in range(1, N_DEV):
                px, py, pc = mx ^ ((k >> 2) & 1), my ^ ((k >> 1) & 1), mc ^ (k & 1)
                remote.append(pltpu.make_async_remote_copy(
                    src_ref=p_refs[a].at[4 * px + 2 * py + pc], dst_ref=out_refs[a].at[me],
                    send_sem=send_sems.at[7 * a + k - 1], recv_sem=recv_sems.at[7 * a + k - 1],
                    device_id=(px, py, pc), device_id_type=MESH_ID))
        return local, remote

    def start(self, *refs):
        local, remote = self._copies(*refs)
        for cp in local + remote:
            cp.start()

    def finish(self, *refs):
        local, remote = self._copies(*refs)
        for cp in remote:
            cp.wait_recv()
        for cp in remote:
            cp.wait_send()
        for cp in local:
            cp.wait()


def _run(body, args, carry=None, **kw):
    if carry is None:
        return _pcall(body, **kw)(*args)
    grid = kw["grid"]
    single = not isinstance(kw["out_shape"], (list, tuple))
    in_specs = list(kw["in_specs"])
    out_specs = [kw["out_specs"]] if single else list(kw["out_specs"])
    out_shape = [kw["out_shape"]] if single else list(kw["out_shape"])
    scratch = list(kw.get("scratch_shapes", []))
    na, nin, nout, nscr = len(carry.arrays), len(in_specs), len(out_specs), len(scratch)

    def carried(*refs):
        ins, cin = refs[:nin], refs[nin:nin + na]
        outs, cout = refs[nin + na:nin + na + nout], refs[nin + na + nout:nin + 2 * na + nout]
        scr = refs[nin + 2 * na + nout:nin + 2 * na + nout + nscr]
        sems = refs[nin + 2 * na + nout + nscr:]
        ids = [pl.program_id(i) for i in range(len(grid))]
        first, last = ids[0] == 0, ids[0] == grid[0] - 1
        for i in range(1, len(grid)):
            first = jnp.logical_and(first, ids[i] == 0)
            last = jnp.logical_and(last, ids[i] == grid[i] - 1)

        @pl.when(first)
        def _():
            carry.start(cin, cout, *sems)

        body(*ins, *outs, *scr)

        @pl.when(last)
        def _():
            carry.finish(cin, cout, *sems)

    res = _pcall(
        carried, name=kw["name"], grid=grid, in_specs=in_specs + [HBM_SPEC] * na, out_specs=out_specs + [HBM_SPEC] * na,
        out_shape=out_shape + carry.out_shapes,
        scratch_shapes=scratch + [pltpu.SemaphoreType.DMA((7 * na,)), pltpu.SemaphoreType.DMA((7 * na,)), pltpu.SemaphoreType.DMA((na,))],
        compiler_params=_cparams(("arbitrary",) * len(grid)),
    )(*args, *carry.arrays)
    main = res[:nout]
    return (main[0] if single else main), list(res[nout:])


def exchange(carry, name):
    def body():
        pass

    return _run(body, [], carry=carry, name=name, grid=(1,), in_specs=[], out_specs=[], out_shape=[])[1]


_DN = {"nn": (((1,), (0,)), ((), ())), "nt": (((1,), (1,)), ((), ())), "tn": (((0,), (0,)), ((), ()))}


LANE = 128
MM_TM, MM_TN, MM_TK = 1024, 1536, 2048


def _tile(n, cap):
    for t in range(min(cap, n) // LANE * LANE, 0, -LANE):
        if n % t == 0:
            return t
    raise ValueError(n)


def mm(pairs, mode, out_dtype, name, tm=MM_TM, tn=MM_TN, tk=MM_TK, carry=None, epilogue=None, extras=()):
    a0, b0 = pairs[0]
    if mode == "nn":
        (m, k), n = a0.shape, b0.shape[1]
    elif mode == "nt":
        (m, k), n = a0.shape, b0.shape[0]
    else:
        (k, m), n = a0.shape, b0.shape[1]
    tm, tn, tk = _tile(m, tm), _tile(n, tn), _tile(k, tk)
    nk = k // tk
    npairs = len(pairs)
    nex = len(extras)
    fused = epilogue is not None
    assert not fused or nk == 1
    out_dtypes = list(out_dtype) if fused else [out_dtype]

    def body(*refs):
        prods = []
        for p in range(npairs):
            a = refs[2 * p][...].astype(BF16) if (p == 0 or pairs[p][0] is not pairs[p - 1][0]) else a
            b = refs[2 * p + 1][...].astype(BF16)
            prods.append(lax.dot_general(a, b, _DN[mode], preferred_element_type=F32))
        if fused:
            ex = [refs[2 * npairs + e][...].astype(F32) for e in range(nex)]
            for o_ref, val in zip(refs[2 * npairs + nex:], epilogue(prods, ex)):
                o_ref[...] = val.astype(o_ref.dtype)
            return
        o_ref = refs[2 * npairs]
        tot = prods[0]
        for d in prods[1:]:
            tot = tot + d
        if nk == 1:
            o_ref[...] = tot.astype(o_ref.dtype)
            return
        acc = refs[2 * npairs + 1]
        kk = pl.program_id(2)

        @pl.when(kk == 0)
        def _():
            acc[...] = tot

        @pl.when(kk > 0)
        def _():
            acc[...] += tot

        @pl.when(kk == nk - 1)
        def _():
            o_ref[...] = acc[...].astype(o_ref.dtype)

    if mode == "nn":
        sp = [pl.BlockSpec((tm, tk), lambda i, j, kk: (i, kk)), pl.BlockSpec((tk, tn), lambda i, j, kk: (kk, j))]
    elif mode == "nt":
        sp = [pl.BlockSpec((tm, tk), lambda i, j, kk: (i, kk)), pl.BlockSpec((tn, tk), lambda i, j, kk: (j, kk))]
    else:
        sp = [pl.BlockSpec((tk, tm), lambda i, j, kk: (kk, i)), pl.BlockSpec((tk, tn), lambda i, j, kk: (kk, j))]
    o_spec = pl.BlockSpec((tm, tn), lambda i, j, kk: (i, j))
    out_shapes = [jax.ShapeDtypeStruct((m, n), dt) for dt in out_dtypes]
    return _run(
        body, [t for pr in pairs for t in pr] + list(extras), carry=carry, name=name, grid=(m // tm, n // tn, nk),
        in_specs=sp * npairs + [o_spec] * nex,
        out_specs=[o_spec] * len(out_shapes) if fused else o_spec,
        out_shape=out_shapes if fused else out_shapes[0],
        scratch_shapes=[pltpu.VMEM((tm, tn), F32)] if nk > 1 else [],
        compiler_params=_cparams(("parallel", "parallel", "arbitrary")),
    )


def bdmm(pairs, name, add=None, ts=1024):
    a0, w0 = pairs[0]
    s = a0.shape[0]
    ka, kn = w0.shape[1], w0.shape[2]
    ts = min(ts, s)
    npairs = len(pairs)

    def body(*refs):
        o_ref = refs[-1]
        tot = None
        for p in range(npairs):
            d = jnp.dot(refs[2 * p][...].astype(BF16), refs[2 * p + 1][...].astype(BF16), preferred_element_type=F32)
            tot = d if tot is None else tot + d
        if add is not None:
            tot = tot + refs[2 * npairs][...]
        o_ref[...] = tot

    sp = []
    args = []
    for a, w in pairs:
        sp += [pl.BlockSpec((ts, w.shape[1]), lambda i, j: (i, j)), pl.BlockSpec((None, w.shape[1], kn), lambda i, j: (j, 0, 0))]
        args += [a, w]
    if add is not None:
        sp.append(pl.BlockSpec((ts, kn), lambda i, j: (i, j)))
        args.append(add)
    return _pcall(
        body, name=name, grid=(s // ts, BD), in_specs=sp, out_specs=pl.BlockSpec((ts, kn), lambda i, j: (i, j)),
        out_shape=jax.ShapeDtypeStruct((s, BD * kn), F32), compiler_params=_cparams(("parallel", "parallel")),
    )(*args)


def bd_tn(a, b, name, ts=512):
    s = a.shape[0]
    ka, kb = a.shape[1] // BD, b.shape[1] // BD
    ts = min(ts, s)
    ns = s // ts

    def body(a_ref, b_ref, o_ref, acc):
        kk = pl.program_id(1)

        @pl.when(kk == 0)
        def _():
            acc[...] = jnp.zeros_like(acc)

        acc[...] += lax.dot_general(a_ref[...].astype(BF16), b_ref[...].astype(BF16), _DN["tn"], preferred_element_type=F32)

        @pl.when(kk == ns - 1)
        def _():
            o_ref[...] = acc[...]

    return _pcall(
        body, name=name, grid=(BD, ns),
        in_specs=[pl.BlockSpec((ts, ka), lambda j, kk: (kk, j)), pl.BlockSpec((ts, kb), lambda j, kk: (kk, j))],
        out_specs=pl.BlockSpec((None, ka, kb), lambda j, kk: (j, 0, 0)),
        out_shape=jax.ShapeDtypeStruct((BD, ka, kb), F32), scratch_shapes=[pltpu.VMEM((ka, kb), F32)],
        compiler_params=_cparams(("parallel", "arbitrary")),
    )(a, b)


def rowwise(name, fn, row_ins, const_ins, row_outs, acc_outs=(), ts=None, carry=None):
    s = row_ins[0].shape[0]
    if ts is None:
        per_row = sum(a.shape[1] * a.dtype.itemsize for a in row_ins) + sum(w * jnp.dtype(dt).itemsize for w, dt in row_outs)
        ts = 512
        while ts > 8 and 2 * ts * per_row > 20 * 1024 * 1024:
            ts //= 2
    ts = min(ts, s)
    assert s % ts == 0
    nr, nc, no, na = len(row_ins), len(const_ins), len(row_outs), len(acc_outs)

    def body(*refs):
        rows = [r[...].astype(F32) for r in refs[:nr]]
        consts = [r[...] for r in refs[nr:nr + nc]]
        outs, accs = fn(rows, consts)
        for r, v in zip(refs[nr + nc:nr + nc + no], outs):
            r[...] = v.astype(r.dtype)
        if na:
            first = pl.program_id(0) == 0
            for r, v in zip(refs[nr + nc + no:], accs):
                @pl.when(first)
                def _(r=r, v=v):
                    r[...] = v

                @pl.when(jnp.logical_not(first))
                def _(r=r, v=v):
                    r[...] += v

    in_specs = [pl.BlockSpec((ts, a.shape[1]), lambda i: (i, 0)) for a in row_ins]
    in_specs += [pl.BlockSpec(c.shape, lambda i, nd=c.ndim: (0,) * nd) for c in const_ins]
    out_specs = [pl.BlockSpec((ts, w), lambda i: (i, 0)) for w, _ in row_outs]
    out_specs += [pl.BlockSpec(shp, lambda i, nd=len(shp): (0,) * nd) for shp in acc_outs]
    out_shape = [jax.ShapeDtypeStruct((s, w), dt) for w, dt in row_outs]
    out_shape += [jax.ShapeDtypeStruct(shp, F32) for shp in acc_outs]
    return _run(
        body, [*row_ins, *const_ins], carry=carry, name=name, grid=(s // ts,), in_specs=in_specs, out_specs=out_specs,
        out_shape=out_shape, compiler_params=_cparams(("arbitrary",)),
    )


def _rms(x, gain):
    r = lax.rsqrt(jnp.mean(x * x, axis=-1, keepdims=True) + EPS)
    n = x * r
    return n * gain, n, r


def _rms_bwd(dy, n, r, gain):
    dn = dy * gain
    dx = r * (dn - n * jnp.mean(dn * n, axis=-1, keepdims=True))
    return dx, jnp.sum(dy * n, axis=0, keepdims=True)


def _sigmoid(x):
    return 1.0 / (1.0 + jnp.exp(-x))


_GELU_K = math.sqrt(2.0 / math.pi)


def _gelu(x):
    t = jnp.tanh(_GELU_K * (x + 0.044715 * x * x * x))
    return 0.5 * x * (1.0 + t), t


def _gelu_grad(x, t):
    return 0.5 * (1.0 + t) + 0.5 * x * (1.0 - t * t) * _GELU_K * (1.0 + 3.0 * 0.044715 * x * x)


def _head_sum(x):
    parts = []
    for h in range(HEADS_PER_GROUP):
        sl = x[:, h * HEAD_DIM:(h + 1) * HEAD_DIM]
        parts.append(jnp.broadcast_to(jnp.sum(sl, axis=-1, keepdims=True), sl.shape))
    return jnp.concatenate(parts, axis=-1)


def _mix_weights(l0, l1, l2):
    mx = jnp.maximum(jnp.maximum(l0, l1), l2)
    e0, e1, e2 = jnp.exp(l0 - mx), jnp.exp(l1 - mx), jnp.exp(l2 - mx)
    inv = 1.0 / (e0 + e1 + e2)
    return e0 * inv, e1 * inv, e2 * inv


BLK = 128


def _slopes(g):
    return [2.0 ** (-8.0 * (g * HEADS_PER_GROUP + h + 1) / N_HEADS) for h in range(HEADS_PER_GROUP)]


def _attn_masks(dil):
    qi = lax.broadcasted_iota(jnp.int32, (BLK, BLK), 0)
    ki = lax.broadcasted_iota(jnp.int32, (BLK, BLK), 1)
    dist_c = qi - ki
    dist_p = BLK + qi - ki
    return dist_c >= 0, dist_p <= BLK, (dist_c * dil).astype(F32), (dist_p * dil).astype(F32)


def attn_fwd(qkv, g, name):
    dil, length, _ = qkv.shape
    scale = HEAD_DIM ** -0.5
    slopes = _slopes(g)

    def body(q_ref, kc_ref, vc_ref, kp_ref, vp_ref, o_ref, l_ref):
        n = pl.program_id(1)
        ok_c, ok_p, dc, dp = _attn_masks(dil)
        ok_p = jnp.logical_and(ok_p, n > 0)
        for h in range(HEADS_PER_GROUP):
            sl = slice(h * HEAD_DIM, (h + 1) * HEAD_DIM)
            q = q_ref[:, sl]
            s_c = lax.dot_general(q, kc_ref[:, sl], _DN["nt"], preferred_element_type=F32) * scale - slopes[h] * dc
            s_p = lax.dot_general(q, kp_ref[:, sl], _DN["nt"], preferred_element_type=F32) * scale - slopes[h] * dp
            s_c = jnp.where(ok_c, s_c, NEG)
            s_p = jnp.where(ok_p, s_p, NEG)
            mx = jnp.maximum(jnp.max(s_c, axis=-1, keepdims=True), jnp.max(s_p, axis=-1, keepdims=True))
            p_c = jnp.exp(s_c - mx)
            p_p = jnp.exp(s_p - mx)
            den = jnp.sum(p_c, axis=-1, keepdims=True) + jnp.sum(p_p, axis=-1, keepdims=True)
            acc = jnp.dot(p_c.astype(BF16), vc_ref[:, sl], preferred_element_type=F32)
            acc += jnp.dot(p_p.astype(BF16), vp_ref[:, sl], preferred_element_type=F32)
            o_ref[:, sl] = acc / den
            l_ref[:, sl] = jnp.broadcast_to(mx + jnp.log(den), (BLK, HEAD_DIM))

    def spec(col, prev):
        if prev:
            return pl.BlockSpec((None, BLK, GROUP_W), lambda r, n: (r, jnp.maximum(n - 1, 0), col))
        return pl.BlockSpec((None, BLK, GROUP_W), lambda r, n: (r, n, col))

    out_spec = pl.BlockSpec((None, BLK, GROUP_W), lambda r, n: (r, n, 0))
    return _pcall(
        body, name=name, grid=(dil, length // BLK),
        in_specs=[spec(0, False), spec(1, False), spec(2, False), spec(1, True), spec(2, True)],
        out_specs=[out_spec, out_spec],
        out_shape=[jax.ShapeDtypeStruct((dil, length, GROUP_W), F32)] * 2,
        compiler_params=_cparams(("parallel", "parallel")),
    )(qkv, qkv, qkv, qkv, qkv)


def attn_bwd(qkv, dout, lse, dd, g, name, carry=None):
    dil, length, _ = qkv.shape
    nblk = length // BLK
    scale = HEAD_DIM ** -0.5
    slopes = _slopes(g)

    def body(q_ref, kc_ref, vc_ref, kp_ref, vp_ref, qn_ref, do_ref, don_ref, l_ref, ln_ref, d_ref, dn_ref, o_ref):
        n = pl.program_id(1)
        ok_c, ok_p, dc, dp = _attn_masks(dil)
        ok_prev = jnp.logical_and(ok_p, n > 0)
        ok_next = jnp.logical_and(ok_p, n < nblk - 1)
        for h in range(HEADS_PER_GROUP):
            sl = slice(h * HEAD_DIM, (h + 1) * HEAD_DIM)
            q, kc, vc, kp, vp, qn = q_ref[:, sl], kc_ref[:, sl], vc_ref[:, sl], kp_ref[:, sl], vp_ref[:, sl], qn_ref[:, sl]
            do, don = do_ref[:, sl], don_ref[:, sl]
            lse_q, lse_n, dd_q, dd_n = l_ref[:, sl], ln_ref[:, sl], d_ref[:, sl], dn_ref[:, sl]

            def probs(qq, kk, dist, ok, lse_t):
                s = lax.dot_general(qq, kk, _DN["nt"], preferred_element_type=F32) * scale - slopes[h] * dist
                return jnp.where(ok, jnp.exp(jnp.where(ok, s, NEG) - lse_t), 0.0)

            p_c = probs(q, kc, dc, ok_c, lse_q)
            p_p = probs(q, kp, dp, ok_prev, lse_q)
            p_x = probs(qn, kc, dp, ok_next, lse_n)
            ds_c = p_c * (lax.dot_general(do, vc, _DN["nt"], preferred_element_type=F32) - dd_q)
            ds_p = p_p * (lax.dot_general(do, vp, _DN["nt"], preferred_element_type=F32) - dd_q)
            ds_x = p_x * (lax.dot_general(don, vc, _DN["nt"], preferred_element_type=F32) - dd_n)
            ds_c16, ds_p16, ds_x16 = ds_c.astype(BF16), ds_p.astype(BF16), ds_x.astype(BF16)
            dq = jnp.dot(ds_c16, kc, preferred_element_type=F32) + jnp.dot(ds_p16, kp, preferred_element_type=F32)
            dk = lax.dot_general(ds_c16, q, _DN["tn"], preferred_element_type=F32)
            dk += lax.dot_general(ds_x16, qn, _DN["tn"], preferred_element_type=F32)
            dv = lax.dot_general(p_c.astype(BF16), do, _DN["tn"], preferred_element_type=F32)
            dv += lax.dot_general(p_x.astype(BF16), don, _DN["tn"], preferred_element_type=F32)
            o_ref[:, h * HEAD_DIM:(h + 1) * HEAD_DIM] = (dq * scale).astype(BF16)
            o_ref[:, GROUP_W + h * HEAD_DIM:GROUP_W + (h + 1) * HEAD_DIM] = (dk * scale).astype(BF16)
            o_ref[:, 2 * GROUP_W + h * HEAD_DIM:2 * GROUP_W + (h + 1) * HEAD_DIM] = dv.astype(BF16)

    def spec(col, which):
        if which == "prev":
            return pl.BlockSpec((None, BLK, GROUP_W), lambda r, n: (r, jnp.maximum(n - 1, 0), col))
        if which == "next":
            return pl.BlockSpec((None, BLK, GROUP_W), lambda r, n: (r, jnp.minimum(n + 1, nblk - 1), col))
        return pl.BlockSpec((None, BLK, GROUP_W), lambda r, n: (r, n, col))

    return _run(
        body, [qkv, qkv, qkv, qkv, qkv, qkv, dout, dout, lse, lse, dd, dd], carry=carry, name=name, grid=(dil, nblk),
        in_specs=[spec(0, "cur"), spec(1, "cur"), spec(2, "cur"), spec(1, "prev"), spec(2, "prev"), spec(0, "next"),
                  spec(0, "cur"), spec(0, "next"), spec(0, "cur"), spec(0, "next"), spec(0, "cur"), spec(0, "next")],
        out_specs=pl.BlockSpec((None, BLK, 3 * GROUP_W), lambda r, n: (r, n, 0)),
        out_shape=jax.ShapeDtypeStruct((dil, length, 3 * GROUP_W), BF16),
        compiler_params=_cparams(("parallel", "parallel")),
    )


def _ssm_prep_values(are, aim, logdt):
    dt = jnp.exp(logdt)
    mag = jnp.exp(are * dt)
    lb_re, lb_im = mag * jnp.cos(aim * dt), mag * jnp.sin(aim * dt)
    inv = 1.0 / (are * are + aim * aim)
    n_re, n_im = lb_re - 1.0, lb_im
    f_re = (n_re * are + n_im * aim) * inv
    f_im = (n_im * are - n_re * aim) * inv
    return dt, lb_re, lb_im, f_re, f_im, inv


PREP_G = 8


def _group_specs(are, logdt, bre):
    def spec(a):
        return pl.BlockSpec((PREP_G,) + a.shape[1:], lambda i: (i, 0, 0))
    return spec(are), spec(logdt), spec(bre)


def ssm_prep(are, aim, logdt, bre, bim):
    def body(are_r, aim_r, ldt_r, bre_r, bim_r, lre_o, lim_o, bbre_o, bbim_o):
        _, lb_re, lb_im, f_re, f_im, _ = _ssm_prep_values(are_r[...], aim_r[...], ldt_r[...])
        lre_o[...] = lb_re
        lim_o[...] = lb_im
        bbre_o[...] = f_re * bre_r[...] - f_im * bim_r[...]
        bbim_o[...] = f_re * bim_r[...] + f_im * bre_r[...]

    sh1 = jax.ShapeDtypeStruct(are.shape, F32)
    shb = jax.ShapeDtypeStruct(bre.shape, F32)
    s1, sd, sb = _group_specs(are, logdt, bre)
    return _pcall(body, name="ssm_prep", grid=(SSM_GROUPS // PREP_G,), in_specs=[s1, s1, sd, sb, sb], out_specs=[s1, s1, sb, sb],
                  out_shape=[sh1, sh1, shb, shb], compiler_params=_cparams(("parallel",)))(are, aim, logdt, bre, bim)


def ssm_prep_bwd(are, aim, logdt, bre, bim, dbbre, dbbim, dlre, dlim):
    def body(are_r, aim_r, ldt_r, bre_r, bim_r, dbbre_r, dbbim_r, dlre_r, dlim_r, dare_o, daim_o, dldt_o, dbre_o, dbim_o):
        are_v, aim_v = are_r[...], aim_r[...]
        dt, lb_re, lb_im, f_re, f_im, inv = _ssm_prep_values(are_v, aim_v, ldt_r[...])
        b_re, b_im, g_re, g_im = bre_r[...], bim_r[...], dbbre_r[...], dbbim_r[...]
        dbre_o[...] = f_re * g_re + f_im * g_im
        dbim_o[...] = f_re * g_im - f_im * g_re
        df_re = jnp.sum(b_re * g_re + b_im * g_im, axis=-1, keepdims=True)
        df_im = jnp.sum(b_re * g_im - b_im * g_re, axis=-1, keepdims=True)
        il_re, il_im = are_v * inv, -aim_v * inv
        cl_re = dlre_r[...] + il_re * df_re + il_im * df_im
        cl_im = dlim_r[...] + il_re * df_im - il_im * df_re
        q_re = -(f_re * il_re - f_im * il_im)
        q_im = -(f_re * il_im + f_im * il_re)
        ca_re = q_re * df_re + q_im * df_im
        ca_im = q_re * df_im - q_im * df_re
        cz_re = lb_re * cl_re + lb_im * cl_im
        cz_im = lb_re * cl_im - lb_im * cl_re
        dare_o[...] = ca_re + dt * cz_re
        daim_o[...] = ca_im + dt * cz_im
        dldt_o[...] = dt * jnp.sum(are_v * cz_re + aim_v * cz_im, axis=1, keepdims=True)

    sh1 = jax.ShapeDtypeStruct(are.shape, F32)
    shb = jax.ShapeDtypeStruct(bre.shape, F32)
    s1, sd, sb = _group_specs(are, logdt, bre)
    return _pcall(
        body, name="ssm_prep_bwd", grid=(SSM_GROUPS // PREP_G,), in_specs=[s1, s1, sd, sb, sb, sb, sb, s1, s1],
        out_specs=[s1, s1, sd, sb, sb], out_shape=[sh1, sh1, jax.ShapeDtypeStruct(logdt.shape, F32), shb, shb],
        compiler_params=_cparams(("parallel",)),
    )(are, aim, logdt, bre, bim, dbbre, dbbim, dlre, dlim)


SCAN_WC = 512


def ssm_scan(xre, xim, lre, lim, name, reverse=False, states=None, carry=None):
    s = xre.shape[0]
    steps = s // SEGS
    assert steps & (steps - 1) == 0
    tt = min(128, steps)
    nch = steps // tt
    rows = tt * SEGS
    with_dl = states is not None
    nsq = int(math.log2(steps))

    def body(*refs):
        xre_r, xim_r, lre_r, lim_r = refs[:4]
        k = 4
        if with_dl:
            hre_r, him_r, pre_r, pim_r, cre_r, cim_r = refs[k:k + 6]
            k += 6
        ore_r, oim_r, hin_re_o, hin_im_o = refs[k:k + 4]
        k += 4
        if with_dl:
            dlre_o, dlim_o = refs[k:k + 2]
            k += 2
        st_re, st_im = refs[k], refs[k + 1]
        ps, ch = pl.program_id(1), pl.program_id(2)
        a_re = jnp.broadcast_to(lre_r[...], (SEGS, SCAN_WC))
        a_im = jnp.broadcast_to(lim_r[...], (SEGS, SCAN_WC))
        if reverse:
            a_im = -a_im

        @pl.when(jnp.logical_and(ps == 0, ch == 0))
        def _():
            st_re[...] = jnp.zeros_like(st_re)
            st_im[...] = jnp.zeros_like(st_im)

        @pl.when(jnp.logical_and(ps == 1, ch == 0))
        def _():
            p_re, p_im = a_re, a_im
            for _ in range(nsq):
                p_re, p_im = p_re * p_re - p_im * p_im, 2.0 * p_re * p_im
            e_re, e_im = st_re[...], st_im[...]
            row = lax.broadcasted_iota(jnp.int32, (SEGS, SCAN_WC), 0)
            edge = (row == SEGS - 1) if reverse else (row == 0)
            c_re, c_im = jnp.zeros_like(e_re), jnp.zeros_like(e_im)
            for _ in range(SEGS - 1):
                n_re = p_re * c_re - p_im * c_im + e_re
                n_im = p_re * c_im + p_im * c_re + e_im
                sh = SEGS - 1 if reverse else 1
                c_re = jnp.where(edge, 0.0, pltpu.roll(n_re, sh, 0))
                c_im = jnp.where(edge, 0.0, pltpu.roll(n_im, sh, 0))
            st_re[...] = c_re
            st_im[...] = c_im
            hin_re_o[...] = c_re
            hin_im_o[...] = c_im
            if with_dl:
                dlre_o[...] = jnp.zeros_like(dlre_o)
                dlim_o[...] = jnp.zeros_like(dlim_o)

        def run(store):
            def step(i, carry):
                if with_dl and store:
                    h_re, h_im, d_re, d_im = carry
                else:
                    h_re, h_im = carry
                t = (tt - 1 - i) if reverse else i
                off = pl.multiple_of(t * SEGS, SEGS)
                n_re = a_re * h_re - a_im * h_im + xre_r[pl.ds(off, SEGS), :]
                n_im = a_re * h_im + a_im * h_re + xim_r[pl.ds(off, SEGS), :]
                if store:
                    ore_r[pl.ds(off, SEGS), :] = n_re
                    oim_r[pl.ds(off, SEGS), :] = n_im
                if with_dl and store:
                    offp = pl.multiple_of(jnp.maximum(t - 1, 0) * SEGS, SEGS)
                    in_re, in_im = hre_r[pl.ds(offp, SEGS), :], him_r[pl.ds(offp, SEGS), :]
                    first_chunk = ch == nch - 1
                    edge_re = jnp.where(first_chunk, cre_r[...], pre_r[...])
                    edge_im = jnp.where(first_chunk, cim_r[...], pim_r[...])
                    hp_re = jnp.where(t == 0, edge_re, in_re)
                    hp_im = jnp.where(t == 0, edge_im, in_im)
                    d_re = d_re + hp_re * n_re + hp_im * n_im
                    d_im = d_im + hp_re * n_im - hp_im * n_re
                    return n_re, n_im, d_re, d_im
                return n_re, n_im

            init = (st_re[...], st_im[...])
            if with_dl and store:
                init = init + (dlre_o[...], dlim_o[...])
            fin = lax.fori_loop(0, tt, step, init)
            st_re[...] = fin[0]
            st_im[...] = fin[1]
            if with_dl and store:
                dlre_o[...] = fin[2]
                dlim_o[...] = fin[3]

        @pl.when(ps == 0)
        def _():
            run(False)

        @pl.when(ps == 1)
        def _():
            run(True)

    def chunk(c):
        return (nch - 1 - c) if reverse else c

    x_spec = pl.BlockSpec((rows, SCAN_WC), lambda j, ps, c: (chunk(c), j))
    l_spec = pl.BlockSpec((1, SCAN_WC), lambda j, ps, c: (0, j))
    o_spec = pl.BlockSpec((rows, SCAN_WC), lambda j, ps, c: (jnp.where(ps == 1, chunk(c), chunk(0)), j))
    e_spec = pl.BlockSpec((SEGS, SCAN_WC), lambda j, ps, c: (0, j))
    in_specs = [x_spec, x_spec, l_spec, l_spec]
    args = [xre, xim, lre, lim]
    out_specs = [o_spec, o_spec, e_spec, e_spec]
    out_shape = [jax.ShapeDtypeStruct((s, STATE_W), F32)] * 2 + [jax.ShapeDtypeStruct((SEGS, STATE_W), F32)] * 2
    if with_dl:
        prev_spec = pl.BlockSpec((SEGS, SCAN_WC), lambda j, ps, c: (jnp.maximum(chunk(c) * tt - 1, 0), j))
        in_specs += [x_spec, x_spec, prev_spec, prev_spec, e_spec, e_spec]
        args += [states[0], states[1], states[0], states[1], states[2], states[3]]
        out_specs += [e_spec, e_spec]
        out_shape += [jax.ShapeDtypeStruct((SEGS, STATE_W), F32)] * 2
    return _run(
        body, args, carry=carry, name=name, grid=(STATE_W // SCAN_WC, 2, nch), in_specs=in_specs, out_specs=out_specs,
        out_shape=out_shape, scratch_shapes=[pltpu.VMEM((SEGS, SCAN_WC), F32)] * 2,
        compiler_params=_cparams(("parallel", "arbitrary", "arbitrary")),
    )


def _block_diag(m):
    g, r, c = m.shape
    m = m.reshape(BD, g // BD, r, c)
    eye = jnp.eye(g // BD, dtype=m.dtype)
    return jnp.einsum("jarc,ab->jarbc", m, eye).reshape(BD, (g // BD) * r, (g // BD) * c)


def _block_diag_extract(m, r, c):
    per = m.shape[1] // r
    m = m.reshape(BD, per, r, per, c)
    return jnp.einsum("jarac->jarc", m).reshape(BD * per, r, c)


def dilate(a, d):
    s, w = a.shape
    if d == 1:
        return a.reshape(1, s, w)
    return a.reshape(s // d, d, w).transpose(1, 0, 2)


def undilate(a):
    d, length, w = a.shape
    if d == 1:
        return a.reshape(length, w)
    return a.transpose(1, 0, 2).reshape(d * length, w)


def to_segments(a):
    s, w = a.shape
    return a.reshape(SEGS, s // SEGS, w).transpose(1, 0, 2).reshape(s, w)


def from_segments(a):
    s, w = a.shape
    return a.reshape(s // SEGS, SEGS, w).transpose(1, 0, 2).reshape(s, w)


W_IN_CHUNKS = 4
FFN_TN = 512


def local_step(x, target, shards, small):
    s = x.shape[0]
    g1, g2, g3, g4 = (small[k].reshape(1, D_MODEL) for k in ("norm_mix_pre", "norm_mix_post", "norm_ffn_pre", "norm_ffn_post"))
    dvec = small["ssm_d"].reshape(1, SSM_W)
    wts, recv = {}, {}

    def gathered(names, blocks):
        for n, b in zip(names, blocks):
            wts[n] = _full_from_gathered(b, n)

    (h,), got = rowwise("rms_in", lambda r, c: ([_rms(r[0], c[0])[0]], []), [x], [g1], [(D_MODEL, BF16)],
                        carry=Gather([shards["w_in"]]))
    w_in_f = _full_from_gathered(got[0], "w_in")
    w_qkv = [jnp.concatenate([w_in_f[:, o + g * GROUP_W:o + (g + 1) * GROUP_W] for o in (0, HQ, 2 * HQ)], axis=1) for g in range(3)]
    w_u, w_gates = w_in_f[:, 3 * HQ:3 * HQ + SSM_W], w_in_f[:, 3 * HQ + SSM_W:]
    hd = [h.reshape(1, s, D_MODEL), dilate(h, 4), dilate(h, 16)]
    qkv = [None] * 3
    names = ("w_attn_up", "w_glu_v", "w_glu_g")
    qkv[0], got = mm([(hd[0].reshape(s, D_MODEL), w_qkv[0])], "nn", BF16, "mm_qkv0", carry=Gather([shards[n] for n in names]))
    gathered(names, got)
    qkv[1], got = mm([(hd[1].reshape(s, D_MODEL), w_qkv[1])], "nn", BF16, "mm_qkv1", carry=Gather([shards["w_out"]]))
    gathered(("w_out",), got)
    qkv[2] = mm([(hd[2].reshape(s, D_MODEL), w_qkv[2])], "nn", BF16, "mm_qkv2")
    u = mm([(h, w_u)], "nn", F32, "mm_u")
    gates, got = mm([(h, w_gates)], "nn", BF16, "mm_gates", carry=Gather([shards["w_ffn_gate"]]))
    gathered(("w_ffn_gate",), got)

    outs, lses = [], []
    for g, (_, dil) in enumerate(ATTN_GROUPS):
        o, l = attn_fwd(qkv[g].reshape(dil, s // dil, 3 * GROUP_W), g, f"attn_fwd{g}")
        outs.append(undilate(o))
        lses.append(undilate(l))

    def merge_fn(r, c):
        w0, w1, w2 = _mix_weights(r[3], r[4], r[5])
        return [w0 * r[0] + w1 * r[1] + w2 * r[2]], []

    (attn,) = rowwise("attn_merge", merge_fn, outs + lses, [], [(GROUP_W, BF16)])
    attn_branch = mm([(attn, wts["w_attn_up"])], "nn", BF16, "mm_up")

    are3 = small["ssm_a_re"].reshape(SSM_GROUPS, SSM_STATE, 1)
    aim3 = small["ssm_a_im"].reshape(SSM_GROUPS, SSM_STATE, 1)
    ldt3 = small["ssm_log_dt"].reshape(SSM_GROUPS, 1, 1)
    bre3 = small["ssm_b_re"].reshape(SSM_GROUPS, SSM_STATE, SSM_GROUP)
    bim3 = small["ssm_b_im"].reshape(SSM_GROUPS, SSM_STATE, SSM_GROUP)
    cre3 = small["ssm_c_re"].reshape(SSM_GROUPS, SSM_GROUP, SSM_STATE)
    cim3 = small["ssm_c_im"].reshape(SSM_GROUPS, SSM_GROUP, SSM_STATE)
    lre3, lim3, bbre, bbim = ssm_prep(are3, aim3, ldt3, bre3, bim3)
    lre, lim = lre3.reshape(1, STATE_W), lim3.reshape(1, STATE_W)
    w_bre = _block_diag(bbre.transpose(0, 2, 1)).astype(BF16)
    w_bim = _block_diag(bbim.transpose(0, 2, 1)).astype(BF16)
    w_cre = _block_diag(cre3.transpose(0, 2, 1)).astype(BF16)
    w_cim = _block_diag(cim3.transpose(0, 2, 1)).astype(BF16)
    u_s = to_segments(u)
    bu_re = bdmm([(u_s, w_bre)], "ssm_bu_re")
    bu_im = bdmm([(u_s, w_bim)], "ssm_bu_im")
    names = ("w_ffn_up", "w_ffn_down")
    (h_re, h_im, hin_re, hin_im), got = ssm_scan(bu_re, bu_im, lre, lim, "ssm_scan_fwd", carry=Gather([shards[n] for n in names]))
    gathered(names, got)
    y_lin = bdmm([(h_re, w_cre), (h_im, -w_cim)], "ssm_y")

    def gelu_fn(r, c):
        y = r[0] + c[0] * r[1]
        return [_gelu(y)[0], y], []

    yg_s, y_ssm = rowwise("ssm_gelu", gelu_fn, [y_lin, u_s], [dvec], [(SSM_W, BF16), (SSM_W, F32)])
    yg = from_segments(yg_s)
    gv = mm([(yg, wts["w_glu_v"])], "nn", BF16, "mm_glu_v")
    gg = mm([(yg, wts["w_glu_g"])], "nn", BF16, "mm_glu_g")

    def gate_fn(r, c):
        gts, ab, gv_, gg_ = r
        sa, ss = _sigmoid(gts[:, :D_MODEL]), _sigmoid(gts[:, D_MODEL:])
        return [sa * ab + ss * (gv_ * _sigmoid(gg_))], []

    (merged,) = rowwise("gate_merge", gate_fn, [gates, attn_branch, gv, gg], [], [(D_MODEL, BF16)])
    o_mix = mm([(merged, wts["w_out"])], "nn", F32, "mm_out")

    def mid_fn(r, c):
        x1 = r[0] + _rms(r[1], c[0])[0]
        return [x1, _rms(x1, c[1])[0]], []

    x1, h2 = rowwise("rms_mid", mid_fn, [x, o_mix], [g2, g3], [(D_MODEL, F32), (D_MODEL, BF16)])
    fa, fb, fin = mm([(h2, wts["w_ffn_gate"]), (h2, wts["w_ffn_up"])], "nn", [BF16, BF16, BF16], "mm_ffn_in", tn=FFN_TN,
                     epilogue=lambda p, e: [p[0], p[1], p[0] * _sigmoid(p[0]) * p[1]])
    f = mm([(fin, wts["w_ffn_down"])], "nn", F32, "mm_ffn_down")

    def loss_fn(r, c):
        x1_, f_, tgt = r
        y, n, rr = _rms(f_, c[0])
        err = x1_ + y - tgt
        dout = err * (1.0 / D_MODEL)
        df, dg = _rms_bwd(dout, n, rr, c[0])
        lp = 0.5 * jnp.sum(jnp.sum(err * err, axis=-1, keepdims=True) * (1.0 / D_MODEL), axis=0, keepdims=True)
        return [df, dout], [dg, lp]

    df, dout, dg4, loss_part = rowwise("loss_bwd", loss_fn, [x1, f, target], [g4], [(D_MODEL, BF16), (D_MODEL, F32)],
                                       acc_outs=[(1, D_MODEL), (1, 1)])
    def sent(names, blocks):
        for n, b in zip(names, blocks):
            recv[n] = b

    def to_owners(names, dws):
        return AllToAll([_split_for_devices(d, n) for n, d in zip(names, dws)])

    def swiglu_bwd(p, e):
        dfin_, (a, b) = p[0], e
        sg = _sigmoid(a)
        return [dfin_ * b * (sg * (1.0 + a * (1.0 - sg))), dfin_ * a * sg]

    da, db = mm([(df, wts["w_ffn_down"])], "nt", [BF16, BF16], "mm_d_fin", tn=FFN_TN, epilogue=swiglu_bwd, extras=[fa, fb])
    dw_ffn_down = mm([(fin, df)], "tn", BF16, "mm_dw_ffn_down")
    dw_ffn_gate, got = mm([(h2, da)], "tn", BF16, "mm_dw_ffn_gate", carry=to_owners(["w_ffn_down"], [dw_ffn_down]))
    sent(["w_ffn_down"], got)
    dw_ffn_up, got = mm([(h2, db)], "tn", BF16, "mm_dw_ffn_up", carry=to_owners(["w_ffn_gate"], [dw_ffn_gate]))
    sent(["w_ffn_gate"], got)
    dh2, got = mm([(da, wts["w_ffn_gate"]), (db, wts["w_ffn_up"])], "nt", F32, "mm_d_h2", carry=to_owners(["w_ffn_up"], [dw_ffn_up]))
    sent(["w_ffn_up"], got)

    def mid_bwd(r, c):
        dh2_, dout_, x1_, o_ = r
        _, n3, r3 = _rms(x1_, c[1])
        dx1, dg3_ = _rms_bwd(dh2_, n3, r3, c[1])
        dx1 = dx1 + dout_
        _, n2, r2 = _rms(o_, c[0])
        do_, dg2_ = _rms_bwd(dx1, n2, r2, c[0])
        return [dx1, do_], [dg2_, dg3_]

    dx1, do_mix, dg2, dg3 = rowwise("rms_mid_bwd", mid_bwd, [dh2, dout, x1, o_mix], [g2, g3], [(D_MODEL, F32), (D_MODEL, BF16)],
                                    acc_outs=[(1, D_MODEL), (1, D_MODEL)])
    dmerged = mm([(do_mix, wts["w_out"])], "nt", BF16, "mm_d_merged")
    dw_out = mm([(merged, do_mix)], "tn", BF16, "mm_dw_out")

    def gate_bwd(r, c):
        dm, gts, ab, gv_, gg_ = r
        sa, ss, sg = _sigmoid(gts[:, :D_MODEL]), _sigmoid(gts[:, D_MODEL:]), _sigmoid(gg_)
        branch = gv_ * sg
        dbranch = dm * ss
        dgates = jnp.concatenate([dm * ab * sa * (1.0 - sa), dm * branch * ss * (1.0 - ss)], axis=-1)
        return [dgates, dm * sa, dbranch * sg, dbranch * gv_ * sg * (1.0 - sg)], []

    dgates, dab, dgv, dgg = rowwise("gate_bwd", gate_bwd, [dmerged, gates, attn_branch, gv, gg], [],
                                    [(2 * D_MODEL, BF16), (D_MODEL, BF16), (D_MODEL, BF16), (D_MODEL, BF16)])
    dattn = mm([(dab, wts["w_attn_up"])], "nt", F32, "mm_d_attn")
    dw_up = mm([(attn, dab)], "tn", BF16, "mm_dw_up")
    dyg = mm([(dgv, wts["w_glu_v"]), (dgg, wts["w_glu_g"])], "nt", F32, "mm_d_yg")
    dw_glu_v = mm([(yg, dgv)], "tn", BF16, "mm_dw_glu_v")
    dw_glu_g = mm([(yg, dgg)], "tn", BF16, "mm_dw_glu_g")

    def gelu_bwd(r, c):
        dyg_, y, us = r
        dy = dyg_ * _gelu_grad(y, _gelu(y)[1])
        return [dy, c[0] * dy], [jnp.sum(dy * us, axis=0, keepdims=True)]

    dy_ssm, du_skip, dd_ssm = rowwise("ssm_gelu_bwd", gelu_bwd, [to_segments(dyg), y_ssm, u_s], [dvec],
                                      [(SSM_W, F32), (SSM_W, F32)], acc_outs=[(1, SSM_W)])
    w_cre_t, w_cim_t = w_cre.transpose(0, 2, 1), w_cim.transpose(0, 2, 1)
    gin_re = bdmm([(dy_ssm, w_cre_t)], "ssm_gin_re")
    gin_im = bdmm([(dy_ssm, -w_cim_t)], "ssm_gin_im")
    names = ["w_out", "w_attn_up", "w_glu_v", "w_glu_g"]
    (g_re, g_im, _, _, dl_re8, dl_im8), got = ssm_scan(
        gin_re, gin_im, lre, lim, "ssm_scan_bwd", reverse=True, states=(h_re, h_im, hin_re, hin_im),
        carry=to_owners(names, [dw_out, dw_up, dw_glu_v, dw_glu_g]))
    sent(names, got)
    du_s = bdmm([(g_re, w_bre.transpose(0, 2, 1)), (g_im, w_bim.transpose(0, 2, 1))], "ssm_du", add=du_skip)
    dbb_re = _block_diag_extract(bd_tn(u_s, g_re, "ssm_dbb_re"), SSM_GROUP, SSM_STATE).transpose(0, 2, 1)
    dbb_im = _block_diag_extract(bd_tn(u_s, g_im, "ssm_dbb_im"), SSM_GROUP, SSM_STATE).transpose(0, 2, 1)
    dc_re = _block_diag_extract(bd_tn(h_re, dy_ssm, "ssm_dc_re"), SSM_STATE, SSM_GROUP).transpose(0, 2, 1)
    dc_im = -_block_diag_extract(bd_tn(h_im, dy_ssm, "ssm_dc_im"), SSM_STATE, SSM_GROUP).transpose(0, 2, 1)

    def fold8(r, c):
        return [], [jnp.sum(r[0], axis=0, keepdims=True), jnp.sum(r[1], axis=0, keepdims=True)]

    dl_re, dl_im = rowwise("ssm_dl_fold", fold8, [dl_re8, dl_im8], [], [], acc_outs=[(1, STATE_W), (1, STATE_W)], ts=SEGS)
    da_re, da_im, dldt, db_re, db_im = ssm_prep_bwd(
        are3, aim3, ldt3, bre3, bim3, dbb_re, dbb_im,
        dl_re.reshape(SSM_GROUPS, SSM_STATE, 1), dl_im.reshape(SSM_GROUPS, SSM_STATE, 1))
    du = from_segments(du_s)

    def merge_bwd(r, c):
        dat, o0, o1, o2, l0, l1, l2 = r
        w0, w1, w2 = _mix_weights(l0, l1, l2)
        tot = _head_sum(dat * (w0 * o0 + w1 * o1 + w2 * o2))
        return [w0 * dat, w1 * dat, w2 * dat, w0 * tot, w1 * tot, w2 * tot], []

    mb = rowwise("attn_merge_bwd", merge_bwd, [dattn] + outs + lses, [], [(GROUP_W, BF16)] * 3 + [(GROUP_W, F32)] * 3)
    dqs, dw_qkv = [], []
    for g, (_, dil) in enumerate(ATTN_GROUPS):
        dq = attn_bwd(qkv[g].reshape(dil, s // dil, 3 * GROUP_W), dilate(mb[g], dil), dilate(lses[g], dil),
                      dilate(mb[3 + g], dil), g, f"attn_bwd{g}").reshape(s, 3 * GROUP_W)
        dqs.append(dq)
        dw_qkv.append(mm([(hd[g].reshape(s, D_MODEL), dq)], "tn", BF16, f"mm_dw_qkv{g}"))
    dw_u = mm([(h, du)], "tn", BF16, "mm_dw_u")
    dw_gates = mm([(h, dgates)], "tn", BF16, "mm_dw_gates")
    dw_in = jnp.concatenate(
        [dw_qkv[g][:, o * GROUP_W:(o + 1) * GROUP_W] for o in range(3) for g in range(3)] + [dw_u, dw_gates], axis=1)
    dw_in_split = _split_for_devices(dw_in, "w_in")
    rows = D_MODEL // W_IN_CHUNKS
    chunks = [AllToAll([dw_in_split[:, i * rows:(i + 1) * rows]]) for i in range(W_IN_CHUNKS)]
    dh_parts, got_chunks = [], []
    for g, (_, dil) in enumerate(ATTN_GROUPS):
        dh_g, got = mm([(dqs[g], w_qkv[g])], "nt", BF16, f"mm_d_h_qkv{g}", carry=chunks[g])
        got_chunks.append(got[0])
        dh_parts.append(undilate(dh_g.reshape(dil, s // dil, D_MODEL)))
    dh_parts.append(mm([(du, w_u)], "nt", BF16, "mm_d_h_u"))
    dh_gates, got = mm([(dgates, w_gates)], "nt", BF16, "mm_d_h_gates", carry=chunks[3])
    got_chunks.append(got[0])
    dh_parts.append(dh_gates)
    recv["w_in"] = jnp.concatenate(got_chunks, axis=1)

    def in_bwd(r, c):
        dh = r[0] + r[1] + r[2] + r[3] + r[4]
        _, n1, r1 = _rms(r[6], c[0])
        dx, dg1_ = _rms_bwd(dh, n1, r1, c[0])
        return [dx + r[5]], [dg1_]

    grad_x, dg1 = rowwise("rms_in_bwd", in_bwd, dh_parts + [dx1, x], [g1], [(D_MODEL, F32)], acc_outs=[(1, D_MODEL)])

    dsmall = dict(norm_mix_pre=dg1, ssm_a_re=da_re, ssm_a_im=da_im, ssm_log_dt=dldt, ssm_b_re=db_re, ssm_b_im=db_im,
                  ssm_c_re=dc_re, ssm_c_im=dc_im, ssm_d=dd_ssm, norm_mix_post=dg2, norm_ffn_pre=dg3, norm_ffn_post=dg4)
    return loss_part, grad_x, recv, dsmall


def adamw(parts, w, m, v, name):
    r, c = w.shape
    tr = r
    while tr > 8 and tr % 2 == 0 and tr * c * (8 * parts.dtype.itemsize + 28) * 2 > 24 * 1024 * 1024:
        tr //= 2
    assert r % tr == 0 and (tr % 8 == 0 or tr == r)
    c1, c2 = 1.0 / (1.0 - ADAM_B1 ** ADAM_STEP), 1.0 / (1.0 - ADAM_B2 ** ADAM_STEP)

    def body(p_ref, w_ref, m_ref, v_ref, g_o, d_o, m_o, v_o):
        g = p_ref[0].astype(F32)
        for i in range(1, N_DEV):
            g = g + p_ref[i].astype(F32)
        mn = ADAM_B1 * m_ref[...] + (1.0 - ADAM_B1) * g
        vn = ADAM_B2 * v_ref[...] + (1.0 - ADAM_B2) * (g * g)
        g_o[...] = g
        m_o[...] = mn
        v_o[...] = vn
        d_o[...] = -ADAM_LR * ((mn * c1) / (jnp.sqrt(vn * c2) + ADAM_EPS) + ADAM_WD * w_ref[...])

    blk = pl.BlockSpec((tr, c), lambda i: (i, 0))
    return _pcall(
        body, name=name, grid=(r // tr,), in_specs=[pl.BlockSpec((N_DEV, tr, c), lambda i: (0, i, 0)), blk, blk, blk],
        out_specs=[blk] * 4, out_shape=[jax.ShapeDtypeStruct((r, c), F32)] * 4, compiler_params=_cparams(("parallel",)),
    )(parts, w, m, v)


PACK_C = 1024
SHARDED = ("w_in", "w_attn_up", "w_glu_v", "w_glu_g", "w_out", "w_ffn_gate", "w_ffn_up", "w_ffn_down")
ROW_SHARDED = ("w_out", "w_ffn_down")
SMALL = ("norm_mix_pre", "ssm_a_re", "ssm_a_im", "ssm_log_dt", "ssm_b_re", "ssm_b_im", "ssm_c_re", "ssm_c_im", "ssm_d",
         "norm_mix_post", "norm_ffn_pre", "norm_ffn_post")
WEIGHTS = ("norm_mix_pre", "w_in", "w_attn_up", "ssm_a_re", "ssm_a_im", "ssm_log_dt", "ssm_b_re", "ssm_b_im", "ssm_c_re",
           "ssm_c_im", "ssm_d", "w_glu_v", "w_glu_g", "w_out", "norm_mix_post", "norm_ffn_pre", "w_ffn_gate", "w_ffn_up",
           "w_ffn_down", "norm_ffn_post")


def _pack(arrs, dtype, pad_rows_to=64):
    flat = jnp.concatenate([a.reshape(-1).astype(dtype) for a in arrs])
    n = flat.shape[0]
    rows = -(-n // PACK_C)
    rows = -(-rows // pad_rows_to) * pad_rows_to
    return jnp.pad(flat, (0, rows * PACK_C - n)).reshape(rows, PACK_C)


def _unpack(flat2d, shapes):
    flat = flat2d.reshape(-1)
    out, off = [], 0
    for shp in shapes:
        n = int(np.prod(shp))
        out.append(flat[off:off + n].reshape(shp))
        off += n
    return out


def _full_from_gathered(gathered, name):
    if name in ROW_SHARDED:
        return gathered.reshape(-1, gathered.shape[2])
    return gathered.transpose(1, 0, 2).reshape(gathered.shape[1], -1)


def _split_for_devices(full, name):
    if name in ROW_SHARDED:
        return full.reshape(N_DEV, -1, full.shape[1])
    return full.reshape(full.shape[0], N_DEV, -1).transpose(1, 0, 2)


def kernel(x, norm_mix_pre, w_in, w_attn_up, ssm_a_re, ssm_a_im, ssm_log_dt, ssm_b_re, ssm_b_im, ssm_c_re, ssm_c_im, ssm_d, w_glu_v, w_glu_g, w_out, norm_mix_post, norm_ffn_pre, w_ffn_gate, w_ffn_up, w_ffn_down, norm_ffn_post, loss_target, m_norm_mix_pre, m_w_in, m_w_attn_up, m_ssm_a_re, m_ssm_a_im, m_ssm_log_dt, m_ssm_b_re, m_ssm_b_im, m_ssm_c_re, m_ssm_c_im, m_ssm_d, m_w_glu_v, m_w_glu_g, m_w_out, m_norm_mix_post, m_norm_ffn_pre, m_w_ffn_gate, m_w_ffn_up, m_w_ffn_down, m_norm_ffn_post, v_norm_mix_pre, v_w_in, v_w_attn_up, v_ssm_a_re, v_ssm_a_im, v_ssm_log_dt, v_ssm_b_re, v_ssm_b_im, v_ssm_c_re, v_ssm_c_im, v_ssm_d, v_w_glu_v, v_w_glu_g, v_w_out, v_norm_mix_post, v_norm_ffn_pre, v_w_ffn_gate, v_w_ffn_up, v_w_ffn_down, v_norm_ffn_post):
    args = dict(locals())
    wv = {n: args[n][0] for n in WEIGHTS}
    mv = {n: args["m_" + n][0] for n in WEIGHTS}
    vv = {n: args["v_" + n][0] for n in WEIGHTS}

    shards = {n: wv[n].astype(BF16) for n in SHARDED}
    small = {n: wv[n] for n in SMALL}
    loss_part, grad_x, recv, dsmall = local_step(x[0], loss_target[0], shards, small)

    res = {}
    for n in SHARDED:
        res[n] = adamw(recv[n], wv[n], mv[n], vv[n], "adamw_" + n)

    small_shapes = [wv[n].shape for n in SMALL]
    (sgather,) = exchange(Gather([_pack([dsmall[n] for n in SMALL], F32)]), "gather_small_grads")
    sres = adamw(sgather, _pack([wv[n] for n in SMALL], F32), _pack([mv[n] for n in SMALL], F32),
                 _pack([vv[n] for n in SMALL], F32), "adamw_small")
    sun = [_unpack(t, small_shapes) for t in sres]
    for k, n in enumerate(SMALL):
        res[n] = tuple(sun[t][k] for t in range(4))

    loss = lax.psum(loss_part[0, 0], ("x", "y", "c"))
    outs = [loss, grad_x[None]]
    for t in range(4):
        outs += [res[n][t][None] for n in WEIGHTS]
    return tuple(outs)
```

```python
import functools
import math

import numpy as np
import jax
import jax.numpy as jnp
from jax import lax
from jax.experimental import pallas as pl
from jax.experimental.pallas import tpu as pltpu

F32 = jnp.float32
BF16 = jnp.bfloat16

D_MODEL = 2048
HEAD_DIM = 128
HEADS_PER_GROUP = 4
ATTN_GROUPS = ((128, 1), (512, 4), (2048, 16))
N_HEADS = HEADS_PER_GROUP * len(ATTN_GROUPS)
GROUP_W = HEADS_PER_GROUP * HEAD_DIM
HQ = N_HEADS * HEAD_DIM
SSM_W = 1024
SSM_GROUP = 16
SSM_GROUPS = 64
SSM_STATE = 64
STATE_W = SSM_GROUPS * SSM_STATE
D_FF = 5632
EPS = 1e-6
N_DEV = 8
SEGS = 8
BD = 8

ADAM_LR, ADAM_B1, ADAM_B2, ADAM_EPS, ADAM_WD, ADAM_STEP = 0.001, 0.9, 0.999, 1e-08, 0.01, 10

VMEM_LIMIT = 56 * 1024 * 1024
HBM_SPEC = pl.BlockSpec(memory_space=pltpu.HBM)
MESH_ID = pl.DeviceIdType.MESH
NEG = -1e30


def _pcall(body, **kw):
    return pl.pallas_call(body, **kw)


def _cparams(sem=None):
    if sem is None:
        return pltpu.CompilerParams(vmem_limit_bytes=VMEM_LIMIT)
    return pltpu.CompilerParams(vmem_limit_bytes=VMEM_LIMIT, dimension_semantics=sem)


def _my_coords():
    return lax.axis_index("x"), lax.axis_index("y"), lax.axis_index("c")


class Gather:
    def __init__(self, xs):
        self.arrays = list(xs)
        self.out_shapes = [jax.ShapeDtypeStruct((N_DEV,) + x.shape, x.dtype) for x in xs]

    def _ctx(self, out_refs, send_sems, recv_sems):
        mx, my, mc = _my_coords()
        me, sibling = (mx, my, mc), (mx, my, 1 - mc)
        chips = [(1 - mx, my), (mx, 1 - my), (1 - mx, 1 - my)]

        def slot(a, px, py, pc):
            return out_refs[a].at[4 * px + 2 * py + pc]

        def copy(a, k, block, to, src=None):
            return pltpu.make_async_remote_copy(
                src_ref=slot(a, *block) if src is None else src, dst_ref=slot(a, *block),
                send_sem=send_sems.at[7 * a + k], recv_sem=recv_sems.at[7 * a + k], device_id=to, device_id_type=MESH_ID)

        return me, sibling, chips, mc, slot, copy

    def _first(self, a, x_refs, ctx):
        me, sibling, chips, mc, slot, copy = ctx
        return [copy(a, 0, me, sibling, src=x_refs[a])] + [copy(a, 1 + j, me, (*chip, mc), src=x_refs[a]) for j, chip in enumerate(chips)]

    def start(self, x_refs, out_refs, send_sems, recv_sems, local_sems):
        ctx = self._ctx(out_refs, send_sems, recv_sems)
        me, slot = ctx[0], ctx[4]
        for a in range(len(self.arrays)):
            pltpu.make_async_copy(x_refs[a], slot(a, *me), local_sems.at[a]).start()
            for cp in self._first(a, x_refs, ctx):
                cp.start()

    def finish(self, x_refs, out_refs, send_sems, recv_sems, local_sems):
        ctx = self._ctx(out_refs, send_sems, recv_sems)
        me, sibling, chips, mc, slot, copy = ctx
        na = len(self.arrays)
        passed = []
        for a in range(na):
            for j, chip in enumerate(chips):
                copy(a, 1 + j, (*chip, mc), me).wait_recv()
                fwd = copy(a, 4 + j, (*chip, mc), sibling)
                fwd.start()
                passed.append(fwd)
        for a in range(na):
            copy(a, 0, sibling, me).wait_recv()
            for j, chip in enumerate(chips):
                copy(a, 4 + j, (*chip, 1 - mc), me).wait_recv()
        for a in range(na):
            for cp in self._first(a, x_refs, ctx):
                cp.wait_send()
        for cp in passed:
            cp.wait_send()
        for a in range(na):
            pltpu.make_async_copy(x_refs[a], slot(a, *me), local_sems.at[a]).wait()


class AllToAll:
    def __init__(self, ps):
        self.arrays = list(ps)
        self.out_shapes = [jax.ShapeDtypeStruct(p.shape, p.dtype) for p in ps]

    def _copies(self, p_refs, out_refs, send_sems, recv_sems, local_sems):
        mx, my, mc = _my_coords()
        me = 4 * mx + 2 * my + mc
        local, remote = [], []
        for a in range(len(self.arrays)):
            local.append(pltpu.make_async_copy(p_refs[a].at[me], out_refs[a].at[me], local_sems.at[a]))
            for k in range(1, N_DEV):
                px, py, pc = mx ^ ((k >> 2) & 1), my ^ ((k >> 1) & 1), mc ^ (k & 1)
                remote.append(pltpu.make_async_remote_copy(
                    src_ref=p_refs[a].at[4 * px + 2 * py + pc], dst_ref=out_refs[a].at[me],
                    send_sem=send_sems.at[7 * a + k - 1], recv_sem=recv_sems.at[7 * a + k - 1],
                    device_id=(px, py, pc), device_id_type=MESH_ID))
        return local, remote

    def start(self, *refs):
        local, remote = self._copies(*refs)
        for cp in local + remote:
            cp.start()

    def finish(self, *refs):
        local, remote = self._copies(*refs)
        for cp in remote:
            cp.wait_recv()
        for cp in remote:
            cp.wait_send()
        for cp in local:
            cp.wait()


def _run(body, args, carry=None, **kw):
    if carry is None:
        return _pcall(body, **kw)(*args)
    grid = kw["grid"]
    single = not isinstance(kw["out_shape"], (list, tuple))
    in_specs = list(kw["in_specs"])
    out_specs = [kw["out_specs"]] if single else list(kw["out_specs"])
    out_shape = [kw["out_shape"]] if single else list(kw["out_shape"])
    scratch = list(kw.get("scratch_shapes", []))
    na, nin, nout, nscr = len(carry.arrays), len(in_specs), len(out_specs), len(scratch)

    def carried(*refs):
        ins, cin = refs[:nin], refs[nin:nin + na]
        outs, cout = refs[nin + na:nin + na + nout], refs[nin + na + nout:nin + 2 * na + nout]
        scr = refs[nin + 2 * na + nout:nin + 2 * na + nout + nscr]
        sems = refs[nin + 2 * na + nout + nscr:]
        ids = [pl.program_id(i) for i in range(len(grid))]
        first, last = ids[0] == 0, ids[0] == grid[0] - 1
        for i in range(1, len(grid)):
            first = jnp.logical_and(first, ids[i] == 0)
            last = jnp.logical_and(last, ids[i] == grid[i] - 1)

        @pl.when(first)
        def _():
            carry.start(cin, cout, *sems)

        body(*ins, *outs, *scr)

        @pl.when(last)
        def _():
            carry.finish(cin, cout, *sems)

    res = _pcall(
        carried, name=kw["name"], grid=grid, in_specs=in_specs + [HBM_SPEC] * na, out_specs=out_specs + [HBM_SPEC] * na,
        out_shape=out_shape + carry.out_shapes,
        scratch_shapes=scratch + [pltpu.SemaphoreType.DMA((7 * na,)), pltpu.SemaphoreType.DMA((7 * na,)), pltpu.SemaphoreType.DMA((na,))],
        compiler_params=_cparams(("arbitrary",) * len(grid)),
    )(*args, *carry.arrays)
    main = res[:nout]
    return (main[0] if single else main), list(res[nout:])


_DN = {"nn": (((1,), (0,)), ((), ())), "nt": (((1,), (1,)), ((), ())), "tn": (((0,), (0,)), ((), ()))}


LANE = 128
MM_TM, MM_TN, MM_TK = 1024, 1536, 2048


def _tile(n, cap):
    for t in range(min(cap, n) // LANE * LANE, 0, -LANE):
        if n % t == 0:
            return t
    raise ValueError(n)


def mm(pairs, mode, out_dtype, name, tm=MM_TM, tn=MM_TN, tk=MM_TK, carry=None, epilogue=None, extras=()):
    a0, b0 = pairs[0]
    if mode == "nn":
        (m, k), n = a0.shape, b0.shape[1]
    elif mode == "nt":
        (m, k), n = a0.shape, b0.shape[0]
    else:
        (k, m), n = a0.shape, b0.shape[1]
    tm, tn, tk = _tile(m, tm), _tile(n, tn), _tile(k, tk)
    nk = k // tk
    npairs = len(pairs)
    nex = len(extras)
    fused = epilogue is not None
    assert not fused or nk == 1
    out_dtypes = list(out_dtype) if fused else [out_dtype]

    def body(*refs):
        prods = []
        for p in range(npairs):
            a = refs[2 * p][...].astype(BF16) if (p == 0 or pairs[p][0] is not pairs[p - 1][0]) else a
            b = refs[2 * p + 1][...].astype(BF16)
            prods.append(lax.dot_general(a, b, _DN[mode], preferred_element_type=F32))
        if fused:
            ex = [refs[2 * npairs + e][...].astype(F32) for e in range(nex)]
            for o_ref, val in zip(refs[2 * npairs + nex:], epilogue(prods, ex)):
                o_ref[...] = val.astype(o_ref.dtype)
            return
        o_ref = refs[2 * npairs]
        tot = prods[0]
        for d in prods[1:]:
            tot = tot + d
        if nk == 1:
            o_ref[...] = tot.astype(o_ref.dtype)
            return
        acc = refs[2 * npairs + 1]
        kk = pl.program_id(2)

        @pl.when(kk == 0)
        def _():
            acc[...] = tot

        @pl.when(kk > 0)
        def _():
            acc[...] += tot

        @pl.when(kk == nk - 1)
        def _():
            o_ref[...] = acc[...].astype(o_ref.dtype)

    if mode == "nn":
        sp = [pl.BlockSpec((tm, tk), lambda i, j, kk: (i, kk)), pl.BlockSpec((tk, tn), lambda i, j, kk: (kk, j))]
    elif mode == "nt":
        sp = [pl.BlockSpec((tm, tk), lambda i, j, kk: (i, kk)), pl.BlockSpec((tn, tk), lambda i, j, kk: (j, kk))]
    else:
        sp = [pl.BlockSpec((tk, tm), lambda i, j, kk: (kk, i)), pl.BlockSpec((tk, tn), lambda i, j, kk: (kk, j))]
    o_spec = pl.BlockSpec((tm, tn), lambda i, j, kk: (i, j))
    out_shapes = [jax.ShapeDtypeStruct((m, n), dt) for dt in out_dtypes]
    return _run(
        body, [t for pr in pairs for t in pr] + list(extras), carry=carry, name=name, grid=(m // tm, n // tn, nk),
        in_specs=sp * npairs + [o_spec] * nex,
        out_specs=[o_spec] * len(out_shapes) if fused else o_spec,
        out_shape=out_shapes if fused else out_shapes[0],
        scratch_shapes=[pltpu.VMEM((tm, tn), F32)] if nk > 1 else [],
        compiler_params=_cparams(("parallel", "parallel", "arbitrary")),
    )


def rowwise(name, fn, row_ins, const_ins, row_outs, acc_outs=(), ts=None, carry=None):
    s = row_ins[0].shape[0]
    if ts is None:
        per_row = sum(a.shape[1] * a.dtype.itemsize for a in row_ins) + sum(w * jnp.dtype(dt).itemsize for w, dt in row_outs)
        ts = 512
        while ts > 8 and 2 * ts * per_row > 20 * 1024 * 1024:
            ts //= 2
    ts = min(ts, s)
    assert s % ts == 0
    nr, nc, no, na = len(row_ins), len(const_ins), len(row_outs), len(acc_outs)

    def body(*refs):
        rows = [r[...].astype(F32) for r in refs[:nr]]
        consts = [r[...] for r in refs[nr:nr + nc]]
        outs, accs = fn(rows, consts)
        for r, v in zip(refs[nr + nc:nr + nc + no], outs):
            r[...] = v.astype(r.dtype)
        if na:
            first = pl.program_id(0) == 0
            for r, v in zip(refs[nr + nc + no:], accs):
                @pl.when(first)
                def _(r=r, v=v):
                    r[...] = v

                @pl.when(jnp.logical_not(first))
                def _(r=r, v=v):
                    r[...] += v

    in_specs = [pl.BlockSpec((ts, a.shape[1]), lambda i: (i, 0)) for a in row_ins]
    in_specs += [pl.BlockSpec(c.shape, lambda i, nd=c.ndim: (0,) * nd) for c in const_ins]
    out_specs = [pl.BlockSpec((ts, w), lambda i: (i, 0)) for w, _ in row_outs]
    out_specs += [pl.BlockSpec(shp, lambda i, nd=len(shp): (0,) * nd) for shp in acc_outs]
    out_shape = [jax.ShapeDtypeStruct((s, w), dt) for w, dt in row_outs]
    out_shape += [jax.ShapeDtypeStruct(shp, F32) for shp in acc_outs]
    return _run(
        body, [*row_ins, *const_ins], carry=carry, name=name, grid=(s // ts,), in_specs=in_specs, out_specs=out_specs,
        out_shape=out_shape, compiler_params=_cparams(("arbitrary",)),
    )


def _rms(x, gain):
    r = lax.rsqrt(jnp.mean(x * x, axis=-1, keepdims=True) + EPS)
    n = x * r
    return n * gain, n, r


def _rms_bwd(dy, n, r, gain):
    dn = dy * gain
    dx = r * (dn - n * jnp.mean(dn * n, axis=-1, keepdims=True))
    return dx, jnp.sum(dy * n, axis=0, keepdims=True)


def _sigmoid(x):
    return 1.0 / (1.0 + jnp.exp(-x))


_GELU_K = math.sqrt(2.0 / math.pi)


def _gelu(x):
    t = jnp.tanh(_GELU_K * (x + 0.044715 * x * x * x))
    return 0.5 * x * (1.0 + t), t


def _gelu_grad(x, t):
    return 0.5 * (1.0 + t) + 0.5 * x * (1.0 - t * t) * _GELU_K * (1.0 + 3.0 * 0.044715 * x * x)


def _head_sum(x):
    parts = []
    for h in range(HEADS_PER_GROUP):
        sl = x[:, h * HEAD_DIM:(h + 1) * HEAD_DIM]
        parts.append(jnp.broadcast_to(jnp.sum(sl, axis=-1, keepdims=True), sl.shape))
    return jnp.concatenate(parts, axis=-1)


def _mix_weights(l0, l1, l2):
    mx = jnp.maximum(jnp.maximum(l0, l1), l2)
    e0, e1, e2 = jnp.exp(l0 - mx), jnp.exp(l1 - mx), jnp.exp(l2 - mx)
    inv = 1.0 / (e0 + e1 + e2)
    return e0 * inv, e1 * inv, e2 * inv


BLK = 128


def _slopes(g):
    return [2.0 ** (-8.0 * (g * HEADS_PER_GROUP + h + 1) / N_HEADS) for h in range(HEADS_PER_GROUP)]


def _attn_masks(dil):
    qi = lax.broadcasted_iota(jnp.int32, (BLK, BLK), 0)
    ki = lax.broadcasted_iota(jnp.int32, (BLK, BLK), 1)
    dist_c = qi - ki
    dist_p = BLK + qi - ki
    return dist_c >= 0, dist_p <= BLK, (dist_c * dil).astype(F32), (dist_p * dil).astype(F32)


def attn_fwd(qkv, g, name):
    dil, length, _ = qkv.shape
    scale = HEAD_DIM ** -0.5
    slopes = _slopes(g)

    def body(q_ref, kc_ref, vc_ref, kp_ref, vp_ref, o_ref, l_ref):
        n = pl.program_id(1)
        ok_c, ok_p, dc, dp = _attn_masks(dil)
        ok_p = jnp.logical_and(ok_p, n > 0)
        for h in range(HEADS_PER_GROUP):
            sl = slice(h * HEAD_DIM, (h + 1) * HEAD_DIM)
            q = q_ref[:, sl]
            s_c = lax.dot_general(q, kc_ref[:, sl], _DN["nt"], preferred_element_type=F32) * scale - slopes[h] * dc
            s_p = lax.dot_general(q, kp_ref[:, sl], _DN["nt"], preferred_element_type=F32) * scale - slopes[h] * dp
            s_c = jnp.where(ok_c, s_c, NEG)
            s_p = jnp.where(ok_p, s_p, NEG)
            mx = jnp.maximum(jnp.max(s_c, axis=-1, keepdims=True), jnp.max(s_p, axis=-1, keepdims=True))
            p_c = jnp.exp(s_c - mx)
            p_p = jnp.exp(s_p - mx)
            den = jnp.sum(p_c, axis=-1, keepdims=True) + jnp.sum(p_p, axis=-1, keepdims=True)
            acc = jnp.dot(p_c.astype(BF16), vc_ref[:, sl], preferred_element_type=F32)
            acc += jnp.dot(p_p.astype(BF16), vp_ref[:, sl], preferred_element_type=F32)
            o_ref[:, sl] = acc / den
            l_ref[:, sl] = jnp.broadcast_to(mx + jnp.log(den), (BLK, HEAD_DIM))

    def spec(col, prev):
        if prev:
            return pl.BlockSpec((None, BLK, GROUP_W), lambda r, n: (r, jnp.maximum(n - 1, 0), col))
        return pl.BlockSpec((None, BLK, GROUP_W), lambda r, n: (r, n, col))

    out_spec = pl.BlockSpec((None, BLK, GROUP_W), lambda r, n: (r, n, 0))
    return _pcall(
        body, name=name, grid=(dil, length // BLK),
        in_specs=[spec(0, False), spec(1, False), spec(2, False), spec(1, True), spec(2, True)],
        out_specs=[out_spec, out_spec],
        out_shape=[jax.ShapeDtypeStruct((dil, length, GROUP_W), F32)] * 2,
        compiler_params=_cparams(("parallel", "parallel")),
    )(qkv, qkv, qkv, qkv, qkv)


def attn_bwd(qkv, dout, lse, dd, g, name, carry=None):
    dil, length, _ = qkv.shape
    nblk = length // BLK
    scale = HEAD_DIM ** -0.5
    slopes = _slopes(g)

    def body(q_ref, kc_ref, vc_ref, kp_ref, vp_ref, qn_ref, do_ref, don_ref, l_ref, ln_ref, d_ref, dn_ref, o_ref):
        n = pl.program_id(1)
        ok_c, ok_p, dc, dp = _attn_masks(dil)
        ok_prev = jnp.logical_and(ok_p, n > 0)
        ok_next = jnp.logical_and(ok_p, n < nblk - 1)
        for h in range(HEADS_PER_GROUP):
            sl = slice(h * HEAD_DIM, (h + 1) * HEAD_DIM)
            q, kc, vc, kp, vp, qn = q_ref[:, sl], kc_ref[:, sl], vc_ref[:, sl], kp_ref[:, sl], vp_ref[:, sl], qn_ref[:, sl]
            do, don = do_ref[:, sl], don_ref[:, sl]
            lse_q, lse_n, dd_q, dd_n = l_ref[:, sl], ln_ref[:, sl], d_ref[:, sl], dn_ref[:, sl]

            def probs(qq, kk, dist, ok, lse_t):
                s = lax.dot_general(qq, kk, _DN["nt"], preferred_element_type=F32) * scale - slopes[h] * dist
                return jnp.where(ok, jnp.exp(jnp.where(ok, s, NEG) - lse_t), 0.0)

            p_c = probs(q, kc, dc, ok_c, lse_q)
            p_p = probs(q, kp, dp, ok_prev, lse_q)
            p_x = probs(qn, kc, dp, ok_next, lse_n)
            ds_c = p_c * (lax.dot_general(do, vc, _DN["nt"], preferred_element_type=F32) - dd_q)
            ds_p = p_p * (lax.dot_general(do, vp, _DN["nt"], preferred_element_type=F32) - dd_q)
            ds_x = p_x * (lax.dot_general(don, vc, _DN["nt"], preferred_element_type=F32) - dd_n)
            ds_c16, ds_p16, ds_x16 = ds_c.astype(BF16), ds_p.astype(BF16), ds_x.astype(BF16)
            dq = jnp.dot(ds_c16, kc, preferred_element_type=F32) + jnp.dot(ds_p16, kp, preferred_element_type=F32)
            dk = lax.dot_general(ds_c16, q, _DN["tn"], preferred_element_type=F32)
            dk += lax.dot_general(ds_x16, qn, _DN["tn"], preferred_element_type=F32)
            dv = lax.dot_general(p_c.astype(BF16), do, _DN["tn"], preferred_element_type=F32)
            dv += lax.dot_general(p_x.astype(BF16), don, _DN["tn"], preferred_element_type=F32)
            o_ref[:, h * HEAD_DIM:(h + 1) * HEAD_DIM] = (dq * scale).astype(BF16)
            o_ref[:, GROUP_W + h * HEAD_DIM:GROUP_W + (h + 1) * HEAD_DIM] = (dk * scale).astype(BF16)
            o_ref[:, 2 * GROUP_W + h * HEAD_DIM:2 * GROUP_W + (h + 1) * HEAD_DIM] = dv.astype(BF16)

    def spec(col, which):
        if which == "prev":
            return pl.BlockSpec((None, BLK, GROUP_W), lambda r, n: (r, jnp.maximum(n - 1, 0), col))
        if which == "next":
            return pl.BlockSpec((None, BLK, GROUP_W), lambda r, n: (r, jnp.minimum(n + 1, nblk - 1), col))
        return pl.BlockSpec((None, BLK, GROUP_W), lambda r, n: (r, n, col))

    return _run(
        body, [qkv, qkv, qkv, qkv, qkv, qkv, dout, dout, lse, lse, dd, dd], carry=carry, name=name, grid=(dil, nblk),
        in_specs=[spec(0, "cur"), spec(1, "cur"), spec(2, "cur"), spec(1, "prev"), spec(2, "prev"), spec(0, "next"),
                  spec(0, "cur"), spec(0, "next"), spec(0, "cur"), spec(0, "next"), spec(0, "cur"), spec(0, "next")],
        out_specs=pl.BlockSpec((None, BLK, 3 * GROUP_W), lambda r, n: (r, n, 0)),
        out_shape=jax.ShapeDtypeStruct((dil, length, 3 * GROUP_W), BF16),
        compiler_params=_cparams(("parallel", "parallel")),
    )


def _ssm_prep_values(are, aim, logdt):
    dt = jnp.exp(logdt)
    mag = jnp.exp(are * dt)
    lb_re, lb_im = mag * jnp.cos(aim * dt), mag * jnp.sin(aim * dt)
    inv = 1.0 / (are * are + aim * aim)
    n_re, n_im = lb_re - 1.0, lb_im
    f_re = (n_re * are + n_im * aim) * inv
    f_im = (n_im * are - n_re * aim) * inv
    return dt, lb_re, lb_im, f_re, f_im, inv


PREP_G = 8


def _group_specs(are, logdt, bre):
    def spec(a):
        return pl.BlockSpec((PREP_G,) + a.shape[1:], lambda i: (i, 0, 0))
    return spec(are), spec(logdt), spec(bre)


def ssm_prep(are, aim, logdt, bre, bim):
    def body(are_r, aim_r, ldt_r, bre_r, bim_r, lre_o, lim_o, bbre_o, bbim_o):
        _, lb_re, lb_im, f_re, f_im, _ = _ssm_prep_values(are_r[...], aim_r[...], ldt_r[...])
        lre_o[...] = lb_re
        lim_o[...] = lb_im
        bbre_o[...] = f_re * bre_r[...] - f_im * bim_r[...]
        bbim_o[...] = f_re * bim_r[...] + f_im * bre_r[...]

    sh1 = jax.ShapeDtypeStruct(are.shape, F32)
    shb = jax.ShapeDtypeStruct(bre.shape, F32)
    s1, sd, sb = _group_specs(are, logdt, bre)
    return _pcall(body, name="ssm_prep", grid=(SSM_GROUPS // PREP_G,), in_specs=[s1, s1, sd, sb, sb], out_specs=[s1, s1, sb, sb],
                  out_shape=[sh1, sh1, shb, shb], compiler_params=_cparams(("parallel",)))(are, aim, logdt, bre, bim)


def ssm_prep_bwd(are, aim, logdt, bre, bim, dbbre, dbbim, dlre, dlim):
    def body(are_r, aim_r, ldt_r, bre_r, bim_r, dbbre_r, dbbim_r, dlre_r, dlim_r, dare_o, daim_o, dldt_o, dbre_o, dbim_o):
        are_v, aim_v = are_r[...], aim_r[...]
        dt, lb_re, lb_im, f_re, f_im, inv = _ssm_prep_values(are_v, aim_v, ldt_r[...])
        b_re, b_im, g_re, g_im = bre_r[...], bim_r[...], dbbre_r[...], dbbim_r[...]
        dbre_o[...] = f_re * g_re + f_im * g_im
        dbim_o[...] = f_re * g_im - f_im * g_re
        df_re = jnp.sum(b_re * g_re + b_im * g_im, axis=-1, keepdims=True)
        df_im = jnp.sum(b_re * g_im - b_im * g_re, axis=-1, keepdims=True)
        il_re, il_im = are_v * inv, -aim_v * inv
        cl_re = dlre_r[...] + il_re * df_re + il_im * df_im
        cl_im = dlim_r[...] + il_re * df_im - il_im * df_re
        q_re = -(f_re * il_re - f_im * il_im)
        q_im = -(f_re * il_im + f_im * il_re)
        ca_re = q_re * df_re + q_im * df_im
        ca_im = q_re * df_im - q_im * df_re
        cz_re = lb_re * cl_re + lb_im * cl_im
        cz_im = lb_re * cl_im - lb_im * cl_re
        dare_o[...] = ca_re + dt * cz_re
        daim_o[...] = ca_im + dt * cz_im
        dldt_o[...] = dt * jnp.sum(are_v * cz_re + aim_v * cz_im, axis=1, keepdims=True)

    sh1 = jax.ShapeDtypeStruct(are.shape, F32)
    shb = jax.ShapeDtypeStruct(bre.shape, F32)
    s1, sd, sb = _group_specs(are, logdt, bre)
    return _pcall(
        body, name="ssm_prep_bwd", grid=(SSM_GROUPS // PREP_G,), in_specs=[s1, s1, sd, sb, sb, sb, sb, s1, s1],
        out_specs=[s1, s1, sd, sb, sb], out_shape=[sh1, sh1, jax.ShapeDtypeStruct(logdt.shape, F32), shb, shb],
        compiler_params=_cparams(("parallel",)),
    )(are, aim, logdt, bre, bim, dbbre, dbbim, dlre, dlim)


SCAN_WC = 512


def _chain_segments(a_re, a_im, e_re, e_im, nsq, reverse):
    p_re, p_im = a_re, a_im
    for _ in range(nsq):
        p_re, p_im = p_re * p_re - p_im * p_im, 2.0 * p_re * p_im
    row = lax.broadcasted_iota(jnp.int32, e_re.shape, 0)
    edge = (row == SEGS - 1) if reverse else (row == 0)
    shift = SEGS - 1 if reverse else 1
    c_re, c_im = jnp.zeros_like(e_re), jnp.zeros_like(e_im)
    for _ in range(SEGS - 1):
        n_re = p_re * c_re - p_im * c_im + e_re
        n_im = p_re * c_im + p_im * c_re + e_im
        c_re = jnp.where(edge, 0.0, pltpu.roll(n_re, shift, 0))
        c_im = jnp.where(edge, 0.0, pltpu.roll(n_im, shift, 0))
    return c_re, c_im


def _scan_dims(s):
    steps = s // SEGS
    assert steps & (steps - 1) == 0
    tt = min(128, steps)
    return steps, tt, steps // tt, tt * SEGS, int(math.log2(steps))


U_BLK = SSM_W // BD


def ssm_fwd(u_s, dvec, w_bre, w_bim, w_cre, w_cim_neg, lre, lim, name, carry=None):
    s = u_s.shape[0]
    steps, tt, nch, rows, nsq = _scan_dims(s)

    def body(u_r, d_r, bre_r, bim_r, cre_r, cim_r, lre_r, lim_r, yg_o, ys_o, hre_o, him_o, hin_re_o, hin_im_o,
             st_re, st_im, x_re, x_im, h_re, h_im):
        ps, ch = pl.program_id(1), pl.program_id(2)
        a_re = jnp.broadcast_to(lre_r[...], (SEGS, SCAN_WC))
        a_im = jnp.broadcast_to(lim_r[...], (SEGS, SCAN_WC))
        ub = u_r[...]
        ub16 = ub.astype(BF16)
        x_re[...] = jnp.dot(ub16, bre_r[...], preferred_element_type=F32)
        x_im[...] = jnp.dot(ub16, bim_r[...], preferred_element_type=F32)

        @pl.when(jnp.logical_and(ps == 0, ch == 0))
        def _():
            st_re[...] = jnp.zeros_like(st_re)
            st_im[...] = jnp.zeros_like(st_im)

        @pl.when(jnp.logical_and(ps == 1, ch == 0))
        def _():
            c_re, c_im = _chain_segments(a_re, a_im, st_re[...], st_im[...], nsq, False)
            st_re[...] = c_re
            st_im[...] = c_im
            hin_re_o[...] = c_re
            hin_im_o[...] = c_im

        def run(store):
            def step(t, hc):
                off = pl.multiple_of(t * SEGS, SEGS)
                n_re = a_re * hc[0] - a_im * hc[1] + x_re[pl.ds(off, SEGS), :]
                n_im = a_re * hc[1] + a_im * hc[0] + x_im[pl.ds(off, SEGS), :]
                if store:
                    h_re[pl.ds(off, SEGS), :] = n_re
                    h_im[pl.ds(off, SEGS), :] = n_im
                return n_re, n_im

            fin = lax.fori_loop(0, tt, step, (st_re[...], st_im[...]))
            st_re[...] = fin[0]
            st_im[...] = fin[1]

        @pl.when(ps == 0)
        def _():
            run(False)

        @pl.when(ps == 1)
        def _():
            run(True)
            hr16, hi16 = h_re[...].astype(BF16), h_im[...].astype(BF16)
            hre_o[...] = hr16
            him_o[...] = hi16
            y = jnp.dot(hr16, cre_r[...], preferred_element_type=F32) + jnp.dot(hi16, cim_r[...], preferred_element_type=F32)
            y = y + d_r[...] * ub
            ys_o[...] = y
            yg_o[...] = _gelu(y)[0].astype(BF16)

    def pass1(ps, c):
        return jnp.where(ps == 1, c, 0)

    u_spec = pl.BlockSpec((rows, U_BLK), lambda j, ps, c: (c, j))
    d_spec = pl.BlockSpec((1, U_BLK), lambda j, ps, c: (0, j))
    b_spec = pl.BlockSpec((None, U_BLK, SCAN_WC), lambda j, ps, c: (j, 0, 0))
    c_spec = pl.BlockSpec((None, SCAN_WC, U_BLK), lambda j, ps, c: (j, 0, 0))
    l_spec = pl.BlockSpec((1, SCAN_WC), lambda j, ps, c: (0, j))
    y_spec = pl.BlockSpec((rows, U_BLK), lambda j, ps, c: (pass1(ps, c), j))
    h_spec = pl.BlockSpec((rows, SCAN_WC), lambda j, ps, c: (pass1(ps, c), j))
    e_spec = pl.BlockSpec((SEGS, SCAN_WC), lambda j, ps, c: (0, j))
    return _run(
        body, [u_s, dvec, w_bre, w_bim, w_cre, w_cim_neg, lre, lim], carry=carry, name=name, grid=(BD, 2, nch),
        in_specs=[u_spec, d_spec, b_spec, b_spec, c_spec, c_spec, l_spec, l_spec],
        out_specs=[y_spec, y_spec, h_spec, h_spec, e_spec, e_spec],
        out_shape=[jax.ShapeDtypeStruct((s, SSM_W), BF16), jax.ShapeDtypeStruct((s, SSM_W), F32),
                   jax.ShapeDtypeStruct((s, STATE_W), BF16), jax.ShapeDtypeStruct((s, STATE_W), BF16),
                   jax.ShapeDtypeStruct((SEGS, STATE_W), F32), jax.ShapeDtypeStruct((SEGS, STATE_W), F32)],
        scratch_shapes=[pltpu.VMEM((SEGS, SCAN_WC), F32)] * 2 + [pltpu.VMEM((rows, SCAN_WC), F32)] * 4,
        compiler_params=_cparams(("parallel", "arbitrary", "arbitrary")),
    )


def ssm_bwd(dyg_s, ys, u_s, h_re, h_im, hin_re, hin_im, dvec, w_bre_t, w_bim_t, w_cre_t, w_cim_neg_t, lre, lim, name, carry=None):
    s = u_s.shape[0]
    steps, tt, nch, rows, nsq = _scan_dims(s)
    half = 2 * SEGS

    def body(dyg_r, ys_r, u_r, hre_r, him_r, pre_r, pim_r, cin_re_r, cin_im_r, d_r, bre_r, bim_r, cre_r, cim_r, lre_r, lim_r,
             du_o, dbre_o, dbim_o, dcre_o, dcim_o, dlre_o, dlim_o, dd_o, st_re, st_im, x_re, x_im, g_re, g_im, hf_re, hf_im):
        ps, ch = pl.program_id(1), pl.program_id(2)
        a_re = jnp.broadcast_to(lre_r[...], (SEGS, SCAN_WC))
        a_im = -jnp.broadcast_to(lim_r[...], (SEGS, SCAN_WC))
        ub, y = u_r[...], ys_r[...]
        dy = dyg_r[...] * _gelu_grad(y, _gelu(y)[1])
        dy16 = dy.astype(BF16)
        x_re[...] = jnp.dot(dy16, cre_r[...], preferred_element_type=F32)
        x_im[...] = jnp.dot(dy16, cim_r[...], preferred_element_type=F32)

        @pl.when(jnp.logical_and(ps == 0, ch == 0))
        def _():
            st_re[...] = jnp.zeros_like(st_re)
            st_im[...] = jnp.zeros_like(st_im)

        @pl.when(jnp.logical_and(ps == 1, ch == 0))
        def _():
            c_re, c_im = _chain_segments(a_re, a_im, st_re[...], st_im[...], nsq, True)
            st_re[...] = c_re
            st_im[...] = c_im
            dlre_o[...] = jnp.zeros_like(dlre_o)
            dlim_o[...] = jnp.zeros_like(dlim_o)

        @pl.when(ps == 0)
        def _():
            def step(i, hc):
                off = pl.multiple_of((tt - 1 - i) * SEGS, SEGS)
                return (a_re * hc[0] - a_im * hc[1] + x_re[pl.ds(off, SEGS), :],
                        a_re * hc[1] + a_im * hc[0] + x_im[pl.ds(off, SEGS), :])

            fin = lax.fori_loop(0, tt, step, (st_re[...], st_im[...]))
            st_re[...] = fin[0]
            st_im[...] = fin[1]

        @pl.when(ps == 1)
        def _():
            hf_re[...] = hre_r[...].astype(F32)
            hf_im[...] = him_r[...].astype(F32)
            first_chunk = ch == nch - 1
            edge_re = jnp.where(first_chunk, cin_re_r[...], pre_r[...].astype(F32)[SEGS:, :])
            edge_im = jnp.where(first_chunk, cin_im_r[...], pim_r[...].astype(F32)[SEGS:, :])

            def step(i, hc):
                t = tt - 1 - i
                off = pl.multiple_of(t * SEGS, SEGS)
                n_re = a_re * hc[0] - a_im * hc[1] + x_re[pl.ds(off, SEGS), :]
                n_im = a_re * hc[1] + a_im * hc[0] + x_im[pl.ds(off, SEGS), :]
                g_re[pl.ds(off, SEGS), :] = n_re
                g_im[pl.ds(off, SEGS), :] = n_im
                offp = pl.multiple_of(jnp.maximum(t - 1, 0) * SEGS, SEGS)
                hp_re = jnp.where(t == 0, edge_re, hf_re[pl.ds(offp, SEGS), :])
                hp_im = jnp.where(t == 0, edge_im, hf_im[pl.ds(offp, SEGS), :])
                return n_re, n_im, hc[2] + hp_re * n_re + hp_im * n_im, hc[3] + hp_re * n_im - hp_im * n_re

            fin = lax.fori_loop(0, tt, step, (st_re[...], st_im[...], dlre_o[...], dlim_o[...]))
            st_re[...] = fin[0]
            st_im[...] = fin[1]
            dlre_o[...] = fin[2]
            dlim_o[...] = fin[3]

            gr16, gi16 = g_re[...].astype(BF16), g_im[...].astype(BF16)
            du = jnp.dot(gr16, bre_r[...], preferred_element_type=F32) + jnp.dot(gi16, bim_r[...], preferred_element_type=F32)
            du_o[...] = du + d_r[...] * dy
            ub16 = ub.astype(BF16)
            parts = [
                (dbre_o, lax.dot_general(ub16, gr16, _DN["tn"], preferred_element_type=F32)),
                (dbim_o, lax.dot_general(ub16, gi16, _DN["tn"], preferred_element_type=F32)),
                (dcre_o, lax.dot_general(hre_r[...], dy16, _DN["tn"], preferred_element_type=F32)),
                (dcim_o, lax.dot_general(him_r[...], dy16, _DN["tn"], preferred_element_type=F32)),
                (dd_o, jnp.sum(dy * ub, axis=0, keepdims=True)),
            ]
            for ref, val in parts:
                @pl.when(ch == 0)
                def _(ref=ref, val=val):
                    ref[...] = val

                @pl.when(ch > 0)
                def _(ref=ref, val=val):
                    ref[...] += val

    def chunk(c):
        return nch - 1 - c

    def pass1(ps, c):
        return jnp.where(ps == 1, chunk(c), chunk(0))

    u_spec = pl.BlockSpec((rows, U_BLK), lambda j, ps, c: (chunk(c), j))
    h_spec = pl.BlockSpec((rows, SCAN_WC), lambda j, ps, c: (pass1(ps, c), j))
    prev_spec = pl.BlockSpec((half, SCAN_WC), lambda j, ps, c: (jnp.maximum(pass1(ps, c) * (rows // half) - 1, 0), j))
    e_spec = pl.BlockSpec((SEGS, SCAN_WC), lambda j, ps, c: (0, j))
    d_spec = pl.BlockSpec((1, U_BLK), lambda j, ps, c: (0, j))
    bt_spec = pl.BlockSpec((None, SCAN_WC, U_BLK), lambda j, ps, c: (j, 0, 0))
    ct_spec = pl.BlockSpec((None, U_BLK, SCAN_WC), lambda j, ps, c: (j, 0, 0))
    l_spec = pl.BlockSpec((1, SCAN_WC), lambda j, ps, c: (0, j))
    du_spec = pl.BlockSpec((rows, U_BLK), lambda j, ps, c: (pass1(ps, c), j))
    return _run(
        body, [dyg_s, ys, u_s, h_re, h_im, h_re, h_im, hin_re, hin_im, dvec, w_bre_t, w_bim_t, w_cre_t, w_cim_neg_t, lre, lim],
        carry=carry, name=name, grid=(BD, 2, nch),
        in_specs=[u_spec, u_spec, u_spec, h_spec, h_spec, prev_spec, prev_spec, e_spec, e_spec, d_spec, bt_spec, bt_spec,
                  ct_spec, ct_spec, l_spec, l_spec],
        out_specs=[du_spec, ct_spec, ct_spec, bt_spec, bt_spec, e_spec, e_spec, d_spec],
        out_shape=[jax.ShapeDtypeStruct((s, SSM_W), F32)] + [jax.ShapeDtypeStruct((BD, U_BLK, SCAN_WC), F32)] * 2
        + [jax.ShapeDtypeStruct((BD, SCAN_WC, U_BLK), F32)] * 2 + [jax.ShapeDtypeStruct((SEGS, STATE_W), F32)] * 2
        + [jax.ShapeDtypeStruct((1, SSM_W), F32)],
        scratch_shapes=[pltpu.VMEM((SEGS, SCAN_WC), F32)] * 2 + [pltpu.VMEM((rows, SCAN_WC), F32)] * 6,
        compiler_params=_cparams(("parallel", "arbitrary", "arbitrary")),
    )


def _block_diag(m):
    g, r, c = m.shape
    m = m.reshape(BD, g // BD, r, c)
    eye = jnp.eye(g // BD, dtype=m.dtype)
    return jnp.einsum("jarc,ab->jarbc", m, eye).reshape(BD, (g // BD) * r, (g // BD) * c)


def _block_diag_extract(m, r, c):
    per = m.shape[1] // r
    m = m.reshape(BD, per, r, per, c)
    return jnp.einsum("jarac->jarc", m).reshape(BD * per, r, c)


def dilate(a, d):
    s, w = a.shape
    if d == 1:
        return a.reshape(1, s, w)
    return a.reshape(s // d, d, w).transpose(1, 0, 2)


def undilate(a):
    d, length, w = a.shape
    if d == 1:
        return a.reshape(length, w)
    return a.transpose(1, 0, 2).reshape(d * length, w)


def to_segments(a):
    s, w = a.shape
    return a.reshape(SEGS, s // SEGS, w).transpose(1, 0, 2).reshape(s, w)


def from_segments(a):
    s, w = a.shape
    return a.reshape(s // SEGS, SEGS, w).transpose(1, 0, 2).reshape(s, w)


W_IN_CHUNKS = 4
FFN_TN = 512


def local_step(x, target, shards, small):
    s = x.shape[0]
    g1, g2, g3, g4 = (small[k].reshape(1, D_MODEL) for k in ("norm_mix_pre", "norm_mix_post", "norm_ffn_pre", "norm_ffn_post"))
    dvec = small["ssm_d"].reshape(1, SSM_W)
    wts, recv = {}, {}

    def gathered(names, blocks):
        for n, b in zip(names, blocks):
            wts[n] = _full_from_gathered(b, n)

    (h,), got = rowwise("rms_in", lambda r, c: ([_rms(r[0], c[0])[0]], []), [x], [g1], [(D_MODEL, BF16)],
                        carry=Gather([shards["w_in"]]))
    w_in_f = _full_from_gathered(got[0], "w_in")
    w_qkv = [jnp.concatenate([w_in_f[:, o + g * GROUP_W:o + (g + 1) * GROUP_W] for o in (0, HQ, 2 * HQ)], axis=1) for g in range(3)]
    w_u, w_gates = w_in_f[:, 3 * HQ:3 * HQ + SSM_W], w_in_f[:, 3 * HQ + SSM_W:]
    hd = [h.reshape(1, s, D_MODEL), dilate(h, 4), dilate(h, 16)]
    qkv = [None] * 3
    names = ("w_attn_up", "w_glu_v", "w_glu_g")
    qkv[0], got = mm([(hd[0].reshape(s, D_MODEL), w_qkv[0])], "nn", BF16, "mm_qkv0", carry=Gather([shards[n] for n in names]))
    gathered(names, got)
    qkv[1], got = mm([(hd[1].reshape(s, D_MODEL), w_qkv[1])], "nn", BF16, "mm_qkv1", carry=Gather([shards["w_out"]]))
    gathered(("w_out",), got)
    qkv[2] = mm([(hd[2].reshape(s, D_MODEL), w_qkv[2])], "nn", BF16, "mm_qkv2")
    u = mm([(h, w_u)], "nn", F32, "mm_u")
    gates, got = mm([(h, w_gates)], "nn", BF16, "mm_gates", carry=Gather([shards["w_ffn_gate"]]))
    gathered(("w_ffn_gate",), got)

    outs, lses = [], []
    for g, (_, dil) in enumerate(ATTN_GROUPS):
        o, l = attn_fwd(qkv[g].reshape(dil, s // dil, 3 * GROUP_W), g, f"attn_fwd{g}")
        outs.append(undilate(o))
        lses.append(undilate(l))

    def merge_fn(r, c):
        w0, w1, w2 = _mix_weights(r[3], r[4], r[5])
        return [w0 * r[0] + w1 * r[1] + w2 * r[2]], []

    (attn,) = rowwise("attn_merge", merge_fn, outs + lses, [], [(GROUP_W, BF16)])
    attn_branch = mm([(attn, wts["w_attn_up"])], "nn", BF16, "mm_up")

    are3 = small["ssm_a_re"].reshape(SSM_GROUPS, SSM_STATE, 1)
    aim3 = small["ssm_a_im"].reshape(SSM_GROUPS, SSM_STATE, 1)
    ldt3 = small["ssm_log_dt"].reshape(SSM_GROUPS, 1, 1)
    bre3 = small["ssm_b_re"].reshape(SSM_GROUPS, SSM_STATE, SSM_GROUP)
    bim3 = small["ssm_b_im"].reshape(SSM_GROUPS, SSM_STATE, SSM_GROUP)
    cre3 = small["ssm_c_re"].reshape(SSM_GROUPS, SSM_GROUP, SSM_STATE)
    cim3 = small["ssm_c_im"].reshape(SSM_GROUPS, SSM_GROUP, SSM_STATE)
    lre3, lim3, bbre, bbim = ssm_prep(are3, aim3, ldt3, bre3, bim3)
    lre, lim = lre3.reshape(1, STATE_W), lim3.reshape(1, STATE_W)
    w_bre = _block_diag(bbre.transpose(0, 2, 1)).astype(BF16)
    w_bim = _block_diag(bbim.transpose(0, 2, 1)).astype(BF16)
    w_cre = _block_diag(cre3.transpose(0, 2, 1)).astype(BF16)
    w_cim = _block_diag(cim3.transpose(0, 2, 1)).astype(BF16)
    u_s = to_segments(u)
    names = ("w_ffn_up", "w_ffn_down")
    (yg_s, y_ssm, h_re, h_im, hin_re, hin_im), got = ssm_fwd(
        u_s, dvec, w_bre, w_bim, w_cre, -w_cim, lre, lim, "ssm_fwd", carry=Gather([shards[n] for n in names]))
    gathered(names, got)
    yg = from_segments(yg_s)
    gv = mm([(yg, wts["w_glu_v"])], "nn", BF16, "mm_glu_v")
    gg = mm([(yg, wts["w_glu_g"])], "nn", BF16, "mm_glu_g")

    def gate_fn(r, c):
        gts, ab, gv_, gg_ = r
        sa, ss = _sigmoid(gts[:, :D_MODEL]), _sigmoid(gts[:, D_MODEL:])
        return [sa * ab + ss * (gv_ * _sigmoid(gg_))], []

    (merged,) = rowwise("gate_merge", gate_fn, [gates, attn_branch, gv, gg], [], [(D_MODEL, BF16)])
    o_mix = mm([(merged, wts["w_out"])], "nn", F32, "mm_out")

    def mid_fn(r, c):
        x1 = r[0] + _rms(r[1], c[0])[0]
        return [x1, _rms(x1, c[1])[0]], []

    x1, h2 = rowwise("rms_mid", mid_fn, [x, o_mix], [g2, g3], [(D_MODEL, F32), (D_MODEL, BF16)])
    fa, fb, fin = mm([(h2, wts["w_ffn_gate"]), (h2, wts["w_ffn_up"])], "nn", [BF16, BF16, BF16], "mm_ffn_in", tn=FFN_TN,
                     epilogue=lambda p, e: [p[0], p[1], p[0] * _sigmoid(p[0]) * p[1]])
    f = mm([(fin, wts["w_ffn_down"])], "nn", F32, "mm_ffn_down")

    def loss_fn(r, c):
        x1_, f_, tgt = r
        y, n, rr = _rms(f_, c[0])
        err = x1_ + y - tgt
        dout = err * (1.0 / D_MODEL)
        df, dg = _rms_bwd(dout, n, rr, c[0])
        lp = 0.5 * jnp.sum(jnp.sum(err * err, axis=-1, keepdims=True) * (1.0 / D_MODEL), axis=0, keepdims=True)
        return [df, dout], [dg, lp]

    df, dout, dg4, loss_part = rowwise("loss_bwd", loss_fn, [x1, f, target], [g4], [(D_MODEL, BF16), (D_MODEL, F32)],
                                       acc_outs=[(1, D_MODEL), (1, 1)])
    def sent(names, blocks):
        for n, b in zip(names, blocks):
            recv[n] = b

    def to_owners(names, dws):
        return AllToAll([_split_for_devices(d, n) for n, d in zip(names, dws)])

    def swiglu_bwd(p, e):
        dfin_, (a, b) = p[0], e
        sg = _sigmoid(a)
        return [dfin_ * b * (sg * (1.0 + a * (1.0 - sg))), dfin_ * a * sg]

    da, db = mm([(df, wts["w_ffn_down"])], "nt", [BF16, BF16], "mm_d_fin", tn=FFN_TN, epilogue=swiglu_bwd, extras=[fa, fb])
    dw_ffn_down = mm([(fin, df)], "tn", BF16, "mm_dw_ffn_down")
    dw_ffn_gate, got = mm([(h2, da)], "tn", BF16, "mm_dw_ffn_gate", carry=to_owners(["w_ffn_down"], [dw_ffn_down]))
    sent(["w_ffn_down"], got)
    dw_ffn_up, got = mm([(h2, db)], "tn", BF16, "mm_dw_ffn_up", carry=to_owners(["w_ffn_gate"], [dw_ffn_gate]))
    sent(["w_ffn_gate"], got)
    dh2, got = mm([(da, wts["w_ffn_gate"]), (db, wts["w_ffn_up"])], "nt", F32, "mm_d_h2", carry=to_owners(["w_ffn_up"], [dw_ffn_up]))
    sent(["w_ffn_up"], got)

    def mid_bwd(r, c):
        dh2_, dout_, x1_, o_ = r
        _, n3, r3 = _rms(x1_, c[1])
        dx1, dg3_ = _rms_bwd(dh2_, n3, r3, c[1])
        dx1 = dx1 + dout_
        _, n2, r2 = _rms(o_, c[0])
        do_, dg2_ = _rms_bwd(dx1, n2, r2, c[0])
        return [dx1, do_], [dg2_, dg3_]

    dx1, do_mix, dg2, dg3 = rowwise("rms_mid_bwd", mid_bwd, [dh2, dout, x1, o_mix], [g2, g3], [(D_MODEL, F32), (D_MODEL, BF16)],
                                    acc_outs=[(1, D_MODEL), (1, D_MODEL)])
    dmerged = mm([(do_mix, wts["w_out"])], "nt", BF16, "mm_d_merged")
    dw_out = mm([(merged, do_mix)], "tn", BF16, "mm_dw_out")

    def gate_bwd(r, c):
        dm, gts, ab, gv_, gg_ = r
        sa, ss, sg = _sigmoid(gts[:, :D_MODEL]), _sigmoid(gts[:, D_MODEL:]), _sigmoid(gg_)
        branch = gv_ * sg
        dbranch = dm * ss
        dgates = jnp.concatenate([dm * ab * sa * (1.0 - sa), dm * branch * ss * (1.0 - ss)], axis=-1)
        return [dgates, dm * sa, dbranch * sg, dbranch * gv_ * sg * (1.0 - sg)], []

    dgates, dab, dgv, dgg = rowwise("gate_bwd", gate_bwd, [dmerged, gates, attn_branch, gv, gg], [],
                                    [(2 * D_MODEL, BF16), (D_MODEL, BF16), (D_MODEL, BF16), (D_MODEL, BF16)])
    dattn = mm([(dab, wts["w_attn_up"])], "nt", F32, "mm_d_attn")
    dw_up = mm([(attn, dab)], "tn", BF16, "mm_dw_up")
    dyg = mm([(dgv, wts["w_glu_v"]), (dgg, wts["w_glu_g"])], "nt", F32, "mm_d_yg")
    dw_glu_v = mm([(yg, dgv)], "tn", BF16, "mm_dw_glu_v")
    dw_glu_g = mm([(yg, dgg)], "tn", BF16, "mm_dw_glu_g")

    names = ["w_out", "w_attn_up", "w_glu_v", "w_glu_g"]
    (du_s, dbre_d, dbim_d, dcre_d, dcim_d, dl_re8, dl_im8, dd_ssm), got = ssm_bwd(
        to_segments(dyg), y_ssm, u_s, h_re, h_im, hin_re, hin_im, dvec, w_bre.transpose(0, 2, 1), w_bim.transpose(0, 2, 1),
        w_cre.transpose(0, 2, 1), -w_cim.transpose(0, 2, 1), lre, lim, "ssm_bwd",
        carry=to_owners(names, [dw_out, dw_up, dw_glu_v, dw_glu_g]))
    sent(names, got)
    dbb_re = _block_diag_extract(dbre_d, SSM_GROUP, SSM_STATE).transpose(0, 2, 1)
    dbb_im = _block_diag_extract(dbim_d, SSM_GROUP, SSM_STATE).transpose(0, 2, 1)
    dc_re = _block_diag_extract(dcre_d, SSM_STATE, SSM_GROUP).transpose(0, 2, 1)
    dc_im = -_block_diag_extract(dcim_d, SSM_STATE, SSM_GROUP).transpose(0, 2, 1)

    def fold8(r, c):
        return [], [jnp.sum(r[0], axis=0, keepdims=True), jnp.sum(r[1], axis=0, keepdims=True)]

    dl_re, dl_im = rowwise("ssm_dl_fold", fold8, [dl_re8, dl_im8], [], [], acc_outs=[(1, STATE_W), (1, STATE_W)], ts=SEGS)
    da_re, da_im, dldt, db_re, db_im = ssm_prep_bwd(
        are3, aim3, ldt3, bre3, bim3, dbb_re, dbb_im,
        dl_re.reshape(SSM_GROUPS, SSM_STATE, 1), dl_im.reshape(SSM_GROUPS, SSM_STATE, 1))
    du = from_segments(du_s)

    def merge_bwd(r, c):
        dat, o0, o1, o2, l0, l1, l2 = r
        w0, w1, w2 = _mix_weights(l0, l1, l2)
        tot = _head_sum(dat * (w0 * o0 + w1 * o1 + w2 * o2))
        return [w0 * dat, w1 * dat, w2 * dat, w0 * tot, w1 * tot, w2 * tot], []

    mb = rowwise("attn_merge_bwd", merge_bwd, [dattn] + outs + lses, [], [(GROUP_W, BF16)] * 3 + [(GROUP_W, F32)] * 3)
    dqs, dw_qkv = [], []
    for g, (_, dil) in enumerate(ATTN_GROUPS):
        dq = attn_bwd(qkv[g].reshape(dil, s // dil, 3 * GROUP_W), dilate(mb[g], dil), dilate(lses[g], dil),
                      dilate(mb[3 + g], dil), g, f"attn_bwd{g}").reshape(s, 3 * GROUP_W)
        dqs.append(dq)
        dw_qkv.append(mm([(hd[g].reshape(s, D_MODEL), dq)], "tn", BF16, f"mm_dw_qkv{g}"))
    dw_u = mm([(h, du)], "tn", BF16, "mm_dw_u")
    dw_gates = mm([(h, dgates)], "tn", BF16, "mm_dw_gates")
    dw_in = jnp.concatenate(
        [dw_qkv[g][:, o * GROUP_W:(o + 1) * GROUP_W] for o in range(3) for g in range(3)] + [dw_u, dw_gates], axis=1)
    dw_in_split = _split_for_devices(dw_in, "w_in")
    rows = D_MODEL // W_IN_CHUNKS
    chunks = [AllToAll([dw_in_split[:, i * rows:(i + 1) * rows]]) for i in range(W_IN_CHUNKS)]
    dh_parts, got_chunks = [], []
    for g, (_, dil) in enumerate(ATTN_GROUPS):
        dh_g, got = mm([(dqs[g], w_qkv[g])], "nt", BF16, f"mm_d_h_qkv{g}", carry=chunks[g])
        got_chunks.append(got[0])
        dh_parts.append(undilate(dh_g.reshape(dil, s // dil, D_MODEL)))
    dh_parts.append(mm([(du, w_u)], "nt", BF16, "mm_d_h_u"))
    dh_gates, got = mm([(dgates, w_gates)], "nt", BF16, "mm_d_h_gates", carry=chunks[3])
    got_chunks.append(got[0])
    dh_parts.append(dh_gates)
    recv["w_in"] = jnp.concatenate(got_chunks, axis=1)

    def in_bwd(r, c):
        dh = r[0] + r[1] + r[2] + r[3] + r[4]
        _, n1, r1 = _rms(r[6], c[0])
        dx, dg1_ = _rms_bwd(dh, n1, r1, c[0])
        return [dx + r[5]], [dg1_]

    grad_x, dg1 = rowwise("rms_in_bwd", in_bwd, dh_parts + [dx1, x], [g1], [(D_MODEL, F32)], acc_outs=[(1, D_MODEL)])

    dsmall = dict(norm_mix_pre=dg1, ssm_a_re=da_re, ssm_a_im=da_im, ssm_log_dt=dldt, ssm_b_re=db_re, ssm_b_im=db_im,
                  ssm_c_re=dc_re, ssm_c_im=dc_im, ssm_d=dd_ssm, norm_mix_post=dg2, norm_ffn_pre=dg3, norm_ffn_post=dg4)
    return loss_part, grad_x, recv, dsmall


def adamw(parts, w, m, v, name, carry=None):
    r, c = w.shape
    tr = r
    while tr > 8 and tr % 2 == 0 and tr * c * (8 * parts.dtype.itemsize + 28) * 2 > 24 * 1024 * 1024:
        tr //= 2
    assert r % tr == 0 and (tr % 8 == 0 or tr == r)
    c1, c2 = 1.0 / (1.0 - ADAM_B1 ** ADAM_STEP), 1.0 / (1.0 - ADAM_B2 ** ADAM_STEP)

    def body(p_ref, w_ref, m_ref, v_ref, g_o, d_o, m_o, v_o):
        g = p_ref[0].astype(F32)
        for i in range(1, N_DEV):
            g = g + p_ref[i].astype(F32)
        mn = ADAM_B1 * m_ref[...] + (1.0 - ADAM_B1) * g
        vn = ADAM_B2 * v_ref[...] + (1.0 - ADAM_B2) * (g * g)
        g_o[...] = g
        m_o[...] = mn
        v_o[...] = vn
        d_o[...] = -ADAM_LR * ((mn * c1) / (jnp.sqrt(vn * c2) + ADAM_EPS) + ADAM_WD * w_ref[...])

    blk = pl.BlockSpec((tr, c), lambda i: (i, 0))
    return _run(
        body, [parts, w, m, v], carry=carry, name=name, grid=(r // tr,),
        in_specs=[pl.BlockSpec((N_DEV, tr, c), lambda i: (0, i, 0)), blk, blk, blk],
        out_specs=[blk] * 4, out_shape=[jax.ShapeDtypeStruct((r, c), F32)] * 4, compiler_params=_cparams(("parallel",)),
    )


PACK_C = 1024
SHARDED = ("w_in", "w_attn_up", "w_glu_v", "w_glu_g", "w_out", "w_ffn_gate", "w_ffn_up", "w_ffn_down")
ROW_SHARDED = ("w_out", "w_ffn_down")
SMALL = ("norm_mix_pre", "ssm_a_re", "ssm_a_im", "ssm_log_dt", "ssm_b_re", "ssm_b_im", "ssm_c_re", "ssm_c_im", "ssm_d",
         "norm_mix_post", "norm_ffn_pre", "norm_ffn_post")
WEIGHTS = ("norm_mix_pre", "w_in", "w_attn_up", "ssm_a_re", "ssm_a_im", "ssm_log_dt", "ssm_b_re", "ssm_b_im", "ssm_c_re",
           "ssm_c_im", "ssm_d", "w_glu_v", "w_glu_g", "w_out", "norm_mix_post", "norm_ffn_pre", "w_ffn_gate", "w_ffn_up",
           "w_ffn_down", "norm_ffn_post")


def _pack(arrs, dtype, pad_rows_to=64):
    flat = jnp.concatenate([a.reshape(-1).astype(dtype) for a in arrs])
    n = flat.shape[0]
    rows = -(-n // PACK_C)
    rows = -(-rows // pad_rows_to) * pad_rows_to
    return jnp.pad(flat, (0, rows * PACK_C - n)).reshape(rows, PACK_C)


def _unpack(flat2d, shapes):
    flat = flat2d.reshape(-1)
    out, off = [], 0
    for shp in shapes:
        n = int(np.prod(shp))
        out.append(flat[off:off + n].reshape(shp))
        off += n
    return out


def _full_from_gathered(gathered, name):
    if name in ROW_SHARDED:
        return gathered.reshape(-1, gathered.shape[2])
    return gathered.transpose(1, 0, 2).reshape(gathered.shape[1], -1)


def _split_for_devices(full, name):
    if name in ROW_SHARDED:
        return full.reshape(N_DEV, -1, full.shape[1])
    return full.reshape(full.shape[0], N_DEV, -1).transpose(1, 0, 2)


def kernel(x, norm_mix_pre, w_in, w_attn_up, ssm_a_re, ssm_a_im, ssm_log_dt, ssm_b_re, ssm_b_im, ssm_c_re, ssm_c_im, ssm_d, w_glu_v, w_glu_g, w_out, norm_mix_post, norm_ffn_pre, w_ffn_gate, w_ffn_up, w_ffn_down, norm_ffn_post, loss_target, m_norm_mix_pre, m_w_in, m_w_attn_up, m_ssm_a_re, m_ssm_a_im, m_ssm_log_dt, m_ssm_b_re, m_ssm_b_im, m_ssm_c_re, m_ssm_c_im, m_ssm_d, m_w_glu_v, m_w_glu_g, m_w_out, m_norm_mix_post, m_norm_ffn_pre, m_w_ffn_gate, m_w_ffn_up, m_w_ffn_down, m_norm_ffn_post, v_norm_mix_pre, v_w_in, v_w_attn_up, v_ssm_a_re, v_ssm_a_im, v_ssm_log_dt, v_ssm_b_re, v_ssm_b_im, v_ssm_c_re, v_ssm_c_im, v_ssm_d, v_w_glu_v, v_w_glu_g, v_w_out, v_norm_mix_post, v_norm_ffn_pre, v_w_ffn_gate, v_w_ffn_up, v_w_ffn_down, v_norm_ffn_post):
    args = dict(locals())
    wv = {n: args[n][0] for n in WEIGHTS}
    mv = {n: args["m_" + n][0] for n in WEIGHTS}
    vv = {n: args["v_" + n][0] for n in WEIGHTS}

    shards = {n: wv[n].astype(BF16) for n in SHARDED}
    small = {n: wv[n] for n in SMALL}
    loss_part, grad_x, recv, dsmall = local_step(x[0], loss_target[0], shards, small)

    small_shapes = [wv[n].shape for n in SMALL]
    res = {}
    res["w_in"], (sgather,) = adamw(recv["w_in"], wv["w_in"], mv["w_in"], vv["w_in"], "adamw_w_in",
                                    carry=Gather([_pack([dsmall[n] for n in SMALL], F32)]))
    for n in SHARDED[1:]:
        res[n] = adamw(recv[n], wv[n], mv[n], vv[n], "adamw_" + n)
    sres = adamw(sgather, _pack([wv[n] for n in SMALL], F32), _pack([mv[n] for n in SMALL], F32),
                 _pack([vv[n] for n in SMALL], F32), "adamw_small")
    sun = [_unpack(t, small_shapes) for t in sres]
    for k, n in enumerate(SMALL):
        res[n] = tuple(sun[t][k] for t in range(4))

    loss = lax.psum(loss_part[0, 0], ("x", "y", "c"))
    outs = [loss, grad_x[None]]
    for t in range(4):
        outs += [res[n][t][None] for n in WEIGHTS]
    return tuple(outs)
```

```python
import functools
import math

import numpy as np
import jax
import jax.numpy as jnp
from jax import lax
from jax.experimental import pallas as pl
from jax.experimental.pallas import tpu as pltpu

F32 = jnp.float32
BF16 = jnp.bfloat16

D_MODEL = 2048
HEAD_DIM = 128
HEADS_PER_GROUP = 4
ATTN_GROUPS = ((128, 1), (512, 4), (2048, 16))
N_HEADS = HEADS_PER_GROUP * len(ATTN_GROUPS)
GROUP_W = HEADS_PER_GROUP * HEAD_DIM
HQ = N_HEADS * HEAD_DIM
SSM_W = 1024
SSM_GROUP = 16
SSM_GROUPS = 64
SSM_STATE = 64
STATE_W = SSM_GROUPS * SSM_STATE
D_FF = 5632
EPS = 1e-6
N_DEV = 8
SEGS = 8
BD = 8

ADAM_LR, ADAM_B1, ADAM_B2, ADAM_EPS, ADAM_WD, ADAM_STEP = 0.001, 0.9, 0.999, 1e-08, 0.01, 10

VMEM_LIMIT = 56 * 1024 * 1024
HBM_SPEC = pl.BlockSpec(memory_space=pltpu.HBM)
MESH_ID = pl.DeviceIdType.MESH
NEG = -1e30


def _pcall(body, **kw):
    return pl.pallas_call(body, **kw)


def _cparams(sem=None):
    if sem is None:
        return pltpu.CompilerParams(vmem_limit_bytes=VMEM_LIMIT)
    return pltpu.CompilerParams(vmem_limit_bytes=VMEM_LIMIT, dimension_semantics=sem)


def _my_coords():
    return lax.axis_index("x"), lax.axis_index("y"), lax.axis_index("c")


class Gather:
    def __init__(self, xs):
        self.arrays = list(xs)
        self.out_shapes = [jax.ShapeDtypeStruct((N_DEV,) + x.shape, x.dtype) for x in xs]

    def _ctx(self, out_refs, send_sems, recv_sems):
        mx, my, mc = _my_coords()
        me, sibling = (mx, my, mc), (mx, my, 1 - mc)
        chips = [(1 - mx, my), (mx, 1 - my), (1 - mx, 1 - my)]

        def slot(a, px, py, pc):
            return out_refs[a].at[4 * px + 2 * py + pc]

        def copy(a, k, block, to, src=None):
            return pltpu.make_async_remote_copy(
                src_ref=slot(a, *block) if src is None else src, dst_ref=slot(a, *block),
                send_sem=send_sems.at[7 * a + k], recv_sem=recv_sems.at[7 * a + k], device_id=to, device_id_type=MESH_ID)

        return me, sibling, chips, mc, slot, copy

    def _first(self, a, x_refs, ctx):
        me, sibling, chips, mc, slot, copy = ctx
        return [copy(a, 0, me, sibling, src=x_refs[a])] + [copy(a, 1 + j, me, (*chip, mc), src=x_refs[a]) for j, chip in enumerate(chips)]

    def start(self, x_refs, out_refs, send_sems, recv_sems, local_sems):
        ctx = self._ctx(out_refs, send_sems, recv_sems)
        me, slot = ctx[0], ctx[4]
        for a in range(len(self.arrays)):
            pltpu.make_async_copy(x_refs[a], slot(a, *me), local_sems.at[a]).start()
            for cp in self._first(a, x_refs, ctx):
                cp.start()

    def finish(self, x_refs, out_refs, send_sems, recv_sems, local_sems):
        ctx = self._ctx(out_refs, send_sems, recv_sems)
        me, sibling, chips, mc, slot, copy = ctx
        na = len(self.arrays)
        passed = []
        for a in range(na):
            for j, chip in enumerate(chips):
                copy(a, 1 + j, (*chip, mc), me).wait_recv()
                fwd = copy(a, 4 + j, (*chip, mc), sibling)
                fwd.start()
                passed.append(fwd)
        for a in range(na):
            copy(a, 0, sibling, me).wait_recv()
            for j, chip in enumerate(chips):
                copy(a, 4 + j, (*chip, 1 - mc), me).wait_recv()
        for a in range(na):
            for cp in self._first(a, x_refs, ctx):
                cp.wait_send()
        for cp in passed:
            cp.wait_send()
        for a in range(na):
            pltpu.make_async_copy(x_refs[a], slot(a, *me), local_sems.at[a]).wait()


class AllToAll:
    def __init__(self, ps):
        self.arrays = list(ps)
        self.out_shapes = [jax.ShapeDtypeStruct(p.shape, p.dtype) for p in ps]

    def _copies(self, p_refs, out_refs, send_sems, recv_sems, local_sems):
        mx, my, mc = _my_coords()
        me = 4 * mx + 2 * my + mc
        local, remote = [], []
        for a in range(len(self.arrays)):
            local.append(pltpu.make_async_copy(p_refs[a].at[me], out_refs[a].at[me], local_sems.at[a]))
            for k in range(1, N_DEV):
                px, py, pc = mx ^ ((k >> 2) & 1), my ^ ((k >> 1) & 1), mc ^ (k & 1)
                remote.append(pltpu.make_async_remote_copy(
                    src_ref=p_refs[a].at[4 * px + 2 * py + pc], dst_ref=out_refs[a].at[me],
                    send_sem=send_sems.at[7 * a + k - 1], recv_sem=recv_sems.at[7 * a + k - 1],
                    device_id=(px, py, pc), device_id_type=MESH_ID))
        return local, remote

    def start(self, *refs):
        local, remote = self._copies(*refs)
        for cp in local + remote:
            cp.start()

    def finish(self, *refs):
        local, remote = self._copies(*refs)
        for cp in remote:
            cp.wait_recv()
        for cp in remote:
            cp.wait_send()
        for cp in local:
            cp.wait()


def _run(body, args, carry=None, **kw):
    if carry is None:
        return _pcall(body, **kw)(*args)
    grid = kw["grid"]
    single = not isinstance(kw["out_shape"], (list, tuple))
    in_specs = list(kw["in_specs"])
    out_specs = [kw["out_specs"]] if single else list(kw["out_specs"])
    out_shape = [kw["out_shape"]] if single else list(kw["out_shape"])
    scratch = list(kw.get("scratch_shapes", []))
    na, nin, nout, nscr = len(carry.arrays), len(in_specs), len(out_specs), len(scratch)

    def carried(*refs):
        ins, cin = refs[:nin], refs[nin:nin + na]
        outs, cout = refs[nin + na:nin + na + nout], refs[nin + na + nout:nin + 2 * na + nout]
        scr = refs[nin + 2 * na + nout:nin + 2 * na + nout + nscr]
        sems = refs[nin + 2 * na + nout + nscr:]
        ids = [pl.program_id(i) for i in range(len(grid))]
        first, last = ids[0] == 0, ids[0] == grid[0] - 1
        for i in range(1, len(grid)):
            first = jnp.logical_and(first, ids[i] == 0)
            last = jnp.logical_and(last, ids[i] == grid[i] - 1)

        @pl.when(first)
        def _():
            carry.start(cin, cout, *sems)

        body(*ins, *outs, *scr)

        @pl.when(last)
        def _():
            carry.finish(cin, cout, *sems)

    res = _pcall(
        carried, name=kw["name"], grid=grid, in_specs=in_specs + [HBM_SPEC] * na, out_specs=out_specs + [HBM_SPEC] * na,
        out_shape=out_shape + carry.out_shapes,
        scratch_shapes=scratch + [pltpu.SemaphoreType.DMA((7 * na,)), pltpu.SemaphoreType.DMA((7 * na,)), pltpu.SemaphoreType.DMA((na,))],
        compiler_params=_cparams(("arbitrary",) * len(grid)),
    )(*args, *carry.arrays)
    main = res[:nout]
    return (main[0] if single else main), list(res[nout:])


_DN = {"nn": (((1,), (0,)), ((), ())), "nt": (((1,), (1,)), ((), ())), "tn": (((0,), (0,)), ((), ()))}


LANE = 128
MM_TM, MM_TN, MM_TK = 1024, 1536, 2048


def _tile(n, cap):
    for t in range(min(cap, n) // LANE * LANE, 0, -LANE):
        if n % t == 0:
            return t
    raise ValueError(n)


DW_TM, DW_TN, DW_TK = 512, 512, 8192


def mm(pairs, mode, out_dtype, name, tm=None, tn=None, tk=None, carry=None, epilogue=None, extras=()):
    a0, b0 = pairs[0]
    if mode == "nn":
        (m, k), n = a0.shape, b0.shape[1]
    elif mode == "nt":
        (m, k), n = a0.shape, b0.shape[0]
    else:
        (k, m), n = a0.shape, b0.shape[1]
    caps = (DW_TM, DW_TN, DW_TK) if mode == "tn" else (MM_TM, MM_TN, MM_TK)
    tm, tn, tk = _tile(m, tm or caps[0]), _tile(n, tn or caps[1]), _tile(k, tk or caps[2])
    nk = k // tk
    npairs = len(pairs)
    nex = len(extras)
    fused = epilogue is not None
    assert not fused or nk == 1
    out_dtypes = list(out_dtype) if fused else [out_dtype]

    def body(*refs):
        prods = []
        for p in range(npairs):
            a = refs[2 * p][...].astype(BF16) if (p == 0 or pairs[p][0] is not pairs[p - 1][0]) else a
            b = refs[2 * p + 1][...].astype(BF16)
            prods.append(lax.dot_general(a, b, _DN[mode], preferred_element_type=F32))
        if fused:
            ex = [refs[2 * npairs + e][...].astype(F32) for e in range(nex)]
            for o_ref, val in zip(refs[2 * npairs + nex:], epilogue(prods, ex)):
                o_ref[...] = val.astype(o_ref.dtype)
            return
        o_ref = refs[2 * npairs]
        tot = prods[0]
        for d in prods[1:]:
            tot = tot + d
        if nk == 1:
            o_ref[...] = tot.astype(o_ref.dtype)
            return
        acc = refs[2 * npairs + 1]
        kk = pl.program_id(2)

        @pl.when(kk == 0)
        def _():
            acc[...] = tot

        @pl.when(kk > 0)
        def _():
            acc[...] += tot

        @pl.when(kk == nk - 1)
        def _():
            o_ref[...] = acc[...].astype(o_ref.dtype)

    if mode == "nn":
        sp = [pl.BlockSpec((tm, tk), lambda i, j, kk: (i, kk)), pl.BlockSpec((tk, tn), lambda i, j, kk: (kk, j))]
    elif mode == "nt":
        sp = [pl.BlockSpec((tm, tk), lambda i, j, kk: (i, kk)), pl.BlockSpec((tn, tk), lambda i, j, kk: (j, kk))]
    else:
        sp = [pl.BlockSpec((tk, tm), lambda i, j, kk: (kk, i)), pl.BlockSpec((tk, tn), lambda i, j, kk: (kk, j))]
    o_spec = pl.BlockSpec((tm, tn), lambda i, j, kk: (i, j))
    out_shapes = [jax.ShapeDtypeStruct((m, n), dt) for dt in out_dtypes]
    return _run(
        body, [t for pr in pairs for t in pr] + list(extras), carry=carry, name=name, grid=(m // tm, n // tn, nk),
        in_specs=sp * npairs + [o_spec] * nex,
        out_specs=[o_spec] * len(out_shapes) if fused else o_spec,
        out_shape=out_shapes if fused else out_shapes[0],
        scratch_shapes=[pltpu.VMEM((tm, tn), F32)] if nk > 1 else [],
        compiler_params=_cparams(("parallel", "parallel", "arbitrary")),
    )


def rowwise(name, fn, row_ins, const_ins, row_outs, acc_outs=(), ts=None, carry=None):
    s = row_ins[0].shape[0]
    if ts is None:
        per_row = sum(a.shape[1] * a.dtype.itemsize for a in row_ins) + sum(w * jnp.dtype(dt).itemsize for w, dt in row_outs)
        ts = 512
        while ts > 8 and 2 * ts * per_row > 20 * 1024 * 1024:
            ts //= 2
    ts = min(ts, s)
    assert s % ts == 0
    nr, nc, no, na = len(row_ins), len(const_ins), len(row_outs), len(acc_outs)

    def body(*refs):
        rows = [r[...].astype(F32) for r in refs[:nr]]
        consts = [r[...] for r in refs[nr:nr + nc]]
        outs, accs = fn(rows, consts)
        for r, v in zip(refs[nr + nc:nr + nc + no], outs):
            r[...] = v.astype(r.dtype)
        if na:
            first = pl.program_id(0) == 0
            for r, v in zip(refs[nr + nc + no:], accs):
                @pl.when(first)
                def _(r=r, v=v):
                    r[...] = v

                @pl.when(jnp.logical_not(first))
                def _(r=r, v=v):
                    r[...] += v

    in_specs = [pl.BlockSpec((ts, a.shape[1]), lambda i: (i, 0)) for a in row_ins]
    in_specs += [pl.BlockSpec(c.shape, lambda i, nd=c.ndim: (0,) * nd) for c in const_ins]
    out_specs = [pl.BlockSpec((ts, w), lambda i: (i, 0)) for w, _ in row_outs]
    out_specs += [pl.BlockSpec(shp, lambda i, nd=len(shp): (0,) * nd) for shp in acc_outs]
    out_shape = [jax.ShapeDtypeStruct((s, w), dt) for w, dt in row_outs]
    out_shape += [jax.ShapeDtypeStruct(shp, F32) for shp in acc_outs]
    return _run(
        body, [*row_ins, *const_ins], carry=carry, name=name, grid=(s // ts,), in_specs=in_specs, out_specs=out_specs,
        out_shape=out_shape, compiler_params=_cparams(("arbitrary",)),
    )


def _rms(x, gain):
    r = lax.rsqrt(jnp.mean(x * x, axis=-1, keepdims=True) + EPS)
    n = x * r
    return n * gain, n, r


def _rms_bwd(dy, n, r, gain):
    dn = dy * gain
    dx = r * (dn - n * jnp.mean(dn * n, axis=-1, keepdims=True))
    return dx, jnp.sum(dy * n, axis=0, keepdims=True)


def _sigmoid(x):
    return 1.0 / (1.0 + jnp.exp(-x))


_GELU_K = math.sqrt(2.0 / math.pi)


def _gelu(x):
    t = jnp.tanh(_GELU_K * (x + 0.044715 * x * x * x))
    return 0.5 * x * (1.0 + t), t


def _gelu_grad(x, t):
    return 0.5 * (1.0 + t) + 0.5 * x * (1.0 - t * t) * _GELU_K * (1.0 + 3.0 * 0.044715 * x * x)


def _head_sum(x):
    parts = []
    for h in range(HEADS_PER_GROUP):
        sl = x[:, h * HEAD_DIM:(h + 1) * HEAD_DIM]
        parts.append(jnp.broadcast_to(jnp.sum(sl, axis=-1, keepdims=True), sl.shape))
    return jnp.concatenate(parts, axis=-1)


def _mix_weights(l0, l1, l2):
    mx = jnp.maximum(jnp.maximum(l0, l1), l2)
    e0, e1, e2 = jnp.exp(l0 - mx), jnp.exp(l1 - mx), jnp.exp(l2 - mx)
    inv = 1.0 / (e0 + e1 + e2)
    return e0 * inv, e1 * inv, e2 * inv


BLK = 128


def _slopes(g):
    return [2.0 ** (-8.0 * (g * HEADS_PER_GROUP + h + 1) / N_HEADS) for h in range(HEADS_PER_GROUP)]


def _attn_masks(dil):
    qi = lax.broadcasted_iota(jnp.int32, (BLK, BLK), 0)
    ki = lax.broadcasted_iota(jnp.int32, (BLK, BLK), 1)
    dist_c = qi - ki
    dist_p = BLK + qi - ki
    return dist_c >= 0, dist_p <= BLK, (dist_c * dil).astype(F32), (dist_p * dil).astype(F32)


def attn_fwd(qkv, g, name):
    dil, length, _ = qkv.shape
    scale = HEAD_DIM ** -0.5
    slopes = _slopes(g)

    def body(q_ref, kc_ref, vc_ref, kp_ref, vp_ref, o_ref, l_ref):
        n = pl.program_id(1)
        ok_c, ok_p, dc, dp = _attn_masks(dil)
        ok_p = jnp.logical_and(ok_p, n > 0)
        for h in range(HEADS_PER_GROUP):
            sl = slice(h * HEAD_DIM, (h + 1) * HEAD_DIM)
            q = q_ref[:, sl]
            s_c = lax.dot_general(q, kc_ref[:, sl], _DN["nt"], preferred_element_type=F32) * scale - slopes[h] * dc
            s_p = lax.dot_general(q, kp_ref[:, sl], _DN["nt"], preferred_element_type=F32) * scale - slopes[h] * dp
            s_c = jnp.where(ok_c, s_c, NEG)
            s_p = jnp.where(ok_p, s_p, NEG)
            mx = jnp.maximum(jnp.max(s_c, axis=-1, keepdims=True), jnp.max(s_p, axis=-1, keepdims=True))
            p_c = jnp.exp(s_c - mx)
            p_p = jnp.exp(s_p - mx)
            den = jnp.sum(p_c, axis=-1, keepdims=True) + jnp.sum(p_p, axis=-1, keepdims=True)
            acc = jnp.dot(p_c.astype(BF16), vc_ref[:, sl], preferred_element_type=F32)
            acc += jnp.dot(p_p.astype(BF16), vp_ref[:, sl], preferred_element_type=F32)
            o_ref[:, sl] = acc / den
            l_ref[:, sl] = jnp.broadcast_to(mx + jnp.log(den), (BLK, HEAD_DIM))

    def spec(col, prev):
        if prev:
            return pl.BlockSpec((None, BLK, GROUP_W), lambda r, n: (r, jnp.maximum(n - 1, 0), col))
        return pl.BlockSpec((None, BLK, GROUP_W), lambda r, n: (r, n, col))

    out_spec = pl.BlockSpec((None, BLK, GROUP_W), lambda r, n: (r, n, 0))
    return _pcall(
        body, name=name, grid=(dil, length // BLK),
        in_specs=[spec(0, False), spec(1, False), spec(2, False), spec(1, True), spec(2, True)],
        out_specs=[out_spec, out_spec],
        out_shape=[jax.ShapeDtypeStruct((dil, length, GROUP_W), F32)] * 2,
        compiler_params=_cparams(("parallel", "parallel")),
    )(qkv, qkv, qkv, qkv, qkv)


def attn_bwd(qkv, dout, lse, dd, g, name, carry=None):
    dil, length, _ = qkv.shape
    nblk = length // BLK
    scale = HEAD_DIM ** -0.5
    slopes = _slopes(g)

    def body(q_ref, kc_ref, vc_ref, kp_ref, vp_ref, qn_ref, do_ref, don_ref, l_ref, ln_ref, d_ref, dn_ref, o_ref):
        n = pl.program_id(1)
        ok_c, ok_p, dc, dp = _attn_masks(dil)
        ok_prev = jnp.logical_and(ok_p, n > 0)
        ok_next = jnp.logical_and(ok_p, n < nblk - 1)
        for h in range(HEADS_PER_GROUP):
            sl = slice(h * HEAD_DIM, (h + 1) * HEAD_DIM)
            q, kc, vc, kp, vp, qn = q_ref[:, sl], kc_ref[:, sl], vc_ref[:, sl], kp_ref[:, sl], vp_ref[:, sl], qn_ref[:, sl]
            do, don = do_ref[:, sl], don_ref[:, sl]
            lse_q, lse_n, dd_q, dd_n = l_ref[:, sl], ln_ref[:, sl], d_ref[:, sl], dn_ref[:, sl]

            def probs(qq, kk, dist, ok, lse_t):
                s = lax.dot_general(qq, kk, _DN["nt"], preferred_element_type=F32) * scale - slopes[h] * dist
                return jnp.where(ok, jnp.exp(jnp.where(ok, s, NEG) - lse_t), 0.0)

            p_c = probs(q, kc, dc, ok_c, lse_q)
            p_p = probs(q, kp, dp, ok_prev, lse_q)
            p_x = probs(qn, kc, dp, ok_next, lse_n)
            ds_c = p_c * (lax.dot_general(do, vc, _DN["nt"], preferred_element_type=F32) - dd_q)
            ds_p = p_p * (lax.dot_general(do, vp, _DN["nt"], preferred_element_type=F32) - dd_q)
            ds_x = p_x * (lax.dot_general(don, vc, _DN["nt"], preferred_element_type=F32) - dd_n)
            ds_c16, ds_p16, ds_x16 = ds_c.astype(BF16), ds_p.astype(BF16), ds_x.astype(BF16)
            dq = jnp.dot(ds_c16, kc, preferred_element_type=F32) + jnp.dot(ds_p16, kp, preferred_element_type=F32)
            dk = lax.dot_general(ds_c16, q, _DN["tn"], preferred_element_type=F32)
            dk += lax.dot_general(ds_x16, qn, _DN["tn"], preferred_element_type=F32)
            dv = lax.dot_general(p_c.astype(BF16), do, _DN["tn"], preferred_element_type=F32)
            dv += lax.dot_general(p_x.astype(BF16), don, _DN["tn"], preferred_element_type=F32)
            o_ref[:, h * HEAD_DIM:(h + 1) * HEAD_DIM] = (dq * scale).astype(BF16)
            o_ref[:, GROUP_W + h * HEAD_DIM:GROUP_W + (h + 1) * HEAD_DIM] = (dk * scale).astype(BF16)
            o_ref[:, 2 * GROUP_W + h * HEAD_DIM:2 * GROUP_W + (h + 1) * HEAD_DIM] = dv.astype(BF16)

    def spec(col, which):
        if which == "prev":
            return pl.BlockSpec((None, BLK, GROUP_W), lambda r, n: (r, jnp.maximum(n - 1, 0), col))
        if which == "next":
            return pl.BlockSpec((None, BLK, GROUP_W), lambda r, n: (r, jnp.minimum(n + 1, nblk - 1), col))
        return pl.BlockSpec((None, BLK, GROUP_W), lambda r, n: (r, n, col))

    return _run(
        body, [qkv, qkv, qkv, qkv, qkv, qkv, dout, dout, lse, lse, dd, dd], carry=carry, name=name, grid=(dil, nblk),
        in_specs=[spec(0, "cur"), spec(1, "cur"), spec(2, "cur"), spec(1, "prev"), spec(2, "prev"), spec(0, "next"),
                  spec(0, "cur"), spec(0, "next"), spec(0, "cur"), spec(0, "next"), spec(0, "cur"), spec(0, "next")],
        out_specs=pl.BlockSpec((None, BLK, 3 * GROUP_W), lambda r, n: (r, n, 0)),
        out_shape=jax.ShapeDtypeStruct((dil, length, 3 * GROUP_W), BF16),
        compiler_params=_cparams(("parallel", "parallel")),
    )


def _ssm_prep_values(are, aim, logdt):
    dt = jnp.exp(logdt)
    mag = jnp.exp(are * dt)
    lb_re, lb_im = mag * jnp.cos(aim * dt), mag * jnp.sin(aim * dt)
    inv = 1.0 / (are * are + aim * aim)
    n_re, n_im = lb_re - 1.0, lb_im
    f_re = (n_re * are + n_im * aim) * inv
    f_im = (n_im * are - n_re * aim) * inv
    return dt, lb_re, lb_im, f_re, f_im, inv


PREP_G = 8


def _group_specs(are, logdt, bre):
    def spec(a):
        return pl.BlockSpec((PREP_G,) + a.shape[1:], lambda i: (i, 0, 0))
    return spec(are), spec(logdt), spec(bre)


def ssm_prep(are, aim, logdt, bre, bim):
    def body(are_r, aim_r, ldt_r, bre_r, bim_r, lre_o, lim_o, bbre_o, bbim_o):
        _, lb_re, lb_im, f_re, f_im, _ = _ssm_prep_values(are_r[...], aim_r[...], ldt_r[...])
        lre_o[...] = lb_re
        lim_o[...] = lb_im
        bbre_o[...] = f_re * bre_r[...] - f_im * bim_r[...]
        bbim_o[...] = f_re * bim_r[...] + f_im * bre_r[...]

    sh1 = jax.ShapeDtypeStruct(are.shape, F32)
    shb = jax.ShapeDtypeStruct(bre.shape, F32)
    s1, sd, sb = _group_specs(are, logdt, bre)
    return _pcall(body, name="ssm_prep", grid=(SSM_GROUPS // PREP_G,), in_specs=[s1, s1, sd, sb, sb], out_specs=[s1, s1, sb, sb],
                  out_shape=[sh1, sh1, shb, shb], compiler_params=_cparams(("parallel",)))(are, aim, logdt, bre, bim)


def ssm_prep_bwd(are, aim, logdt, bre, bim, dbbre, dbbim, dlre, dlim):
    def body(are_r, aim_r, ldt_r, bre_r, bim_r, dbbre_r, dbbim_r, dlre_r, dlim_r, dare_o, daim_o, dldt_o, dbre_o, dbim_o):
        are_v, aim_v = are_r[...], aim_r[...]
        dt, lb_re, lb_im, f_re, f_im, inv = _ssm_prep_values(are_v, aim_v, ldt_r[...])
        b_re, b_im, g_re, g_im = bre_r[...], bim_r[...], dbbre_r[...], dbbim_r[...]
        dbre_o[...] = f_re * g_re + f_im * g_im
        dbim_o[...] = f_re * g_im - f_im * g_re
        df_re = jnp.sum(b_re * g_re + b_im * g_im, axis=-1, keepdims=True)
        df_im = jnp.sum(b_re * g_im - b_im * g_re, axis=-1, keepdims=True)
        il_re, il_im = are_v * inv, -aim_v * inv
        cl_re = dlre_r[...] + il_re * df_re + il_im * df_im
        cl_im = dlim_r[...] + il_re * df_im - il_im * df_re
        q_re = -(f_re * il_re - f_im * il_im)
        q_im = -(f_re * il_im + f_im * il_re)
        ca_re = q_re * df_re + q_im * df_im
        ca_im = q_re * df_im - q_im * df_re
        cz_re = lb_re * cl_re + lb_im * cl_im
        cz_im = lb_re * cl_im - lb_im * cl_re
        dare_o[...] = ca_re + dt * cz_re
        daim_o[...] = ca_im + dt * cz_im
        dldt_o[...] = dt * jnp.sum(are_v * cz_re + aim_v * cz_im, axis=1, keepdims=True)

    sh1 = jax.ShapeDtypeStruct(are.shape, F32)
    shb = jax.ShapeDtypeStruct(bre.shape, F32)
    s1, sd, sb = _group_specs(are, logdt, bre)
    return _pcall(
        body, name="ssm_prep_bwd", grid=(SSM_GROUPS // PREP_G,), in_specs=[s1, s1, sd, sb, sb, sb, sb, s1, s1],
        out_specs=[s1, s1, sd, sb, sb], out_shape=[sh1, sh1, jax.ShapeDtypeStruct(logdt.shape, F32), shb, shb],
        compiler_params=_cparams(("parallel",)),
    )(are, aim, logdt, bre, bim, dbbre, dbbim, dlre, dlim)


SCAN_WC = 512


def _chain_segments(a_re, a_im, e_re, e_im, nsq, reverse):
    p_re, p_im = a_re, a_im
    for _ in range(nsq):
        p_re, p_im = p_re * p_re - p_im * p_im, 2.0 * p_re * p_im
    row = lax.broadcasted_iota(jnp.int32, e_re.shape, 0)
    edge = (row == SEGS - 1) if reverse else (row == 0)
    shift = SEGS - 1 if reverse else 1
    c_re, c_im = jnp.zeros_like(e_re), jnp.zeros_like(e_im)
    for _ in range(SEGS - 1):
        n_re = p_re * c_re - p_im * c_im + e_re
        n_im = p_re * c_im + p_im * c_re + e_im
        c_re = jnp.where(edge, 0.0, pltpu.roll(n_re, shift, 0))
        c_im = jnp.where(edge, 0.0, pltpu.roll(n_im, shift, 0))
    return c_re, c_im


def _scan_dims(s):
    steps = s // SEGS
    assert steps & (steps - 1) == 0
    tt = min(128, steps)
    return steps, tt, steps // tt, tt * SEGS, int(math.log2(steps))


U_BLK = SSM_W // BD


def ssm_fwd(u_s, dvec, w_bre, w_bim, w_cre, w_cim_neg, lre, lim, name, carry=None):
    s = u_s.shape[0]
    steps, tt, nch, rows, nsq = _scan_dims(s)

    def body(u_r, d_r, bre_r, bim_r, cre_r, cim_r, lre_r, lim_r, yg_o, ys_o, hre_o, him_o, hin_re_o, hin_im_o,
             st_re, st_im, x_re, x_im, h_re, h_im):
        ps, ch = pl.program_id(1), pl.program_id(2)
        a_re = jnp.broadcast_to(lre_r[...], (SEGS, SCAN_WC))
        a_im = jnp.broadcast_to(lim_r[...], (SEGS, SCAN_WC))
        ub = u_r[...]
        ub16 = ub.astype(BF16)
        x_re[...] = jnp.dot(ub16, bre_r[...], preferred_element_type=F32)
        x_im[...] = jnp.dot(ub16, bim_r[...], preferred_element_type=F32)

        @pl.when(jnp.logical_and(ps == 0, ch == 0))
        def _():
            st_re[...] = jnp.zeros_like(st_re)
            st_im[...] = jnp.zeros_like(st_im)

        @pl.when(jnp.logical_and(ps == 1, ch == 0))
        def _():
            c_re, c_im = _chain_segments(a_re, a_im, st_re[...], st_im[...], nsq, False)
            st_re[...] = c_re
            st_im[...] = c_im
            hin_re_o[...] = c_re
            hin_im_o[...] = c_im

        def run(store):
            def step(t, hc):
                off = pl.multiple_of(t * SEGS, SEGS)
                n_re = a_re * hc[0] - a_im * hc[1] + x_re[pl.ds(off, SEGS), :]
                n_im = a_re * hc[1] + a_im * hc[0] + x_im[pl.ds(off, SEGS), :]
                if store:
                    h_re[pl.ds(off, SEGS), :] = n_re
                    h_im[pl.ds(off, SEGS), :] = n_im
                return n_re, n_im

            fin = lax.fori_loop(0, tt, step, (st_re[...], st_im[...]))
            st_re[...] = fin[0]
            st_im[...] = fin[1]

        @pl.when(ps == 0)
        def _():
            run(False)

        @pl.when(ps == 1)
        def _():
            run(True)
            hr16, hi16 = h_re[...].astype(BF16), h_im[...].astype(BF16)
            hre_o[...] = hr16
            him_o[...] = hi16
            y = jnp.dot(hr16, cre_r[...], preferred_element_type=F32) + jnp.dot(hi16, cim_r[...], preferred_element_type=F32)
            y = y + d_r[...] * ub
            ys_o[...] = y
            yg_o[...] = _gelu(y)[0].astype(BF16)

    def pass1(ps, c):
        return jnp.where(ps == 1, c, 0)

    u_spec = pl.BlockSpec((rows, U_BLK), lambda j, ps, c: (c, j))
    d_spec = pl.BlockSpec((1, U_BLK), lambda j, ps, c: (0, j))
    b_spec = pl.BlockSpec((None, U_BLK, SCAN_WC), lambda j, ps, c: (j, 0, 0))
    c_spec = pl.BlockSpec((None, SCAN_WC, U_BLK), lambda j, ps, c: (j, 0, 0))
    l_spec = pl.BlockSpec((1, SCAN_WC), lambda j, ps, c: (0, j))
    y_spec = pl.BlockSpec((rows, U_BLK), lambda j, ps, c: (pass1(ps, c), j))
    h_spec = pl.BlockSpec((rows, SCAN_WC), lambda j, ps, c: (pass1(ps, c), j))
    e_spec = pl.BlockSpec((SEGS, SCAN_WC), lambda j, ps, c: (0, j))
    return _run(
        body, [u_s, dvec, w_bre, w_bim, w_cre, w_cim_neg, lre, lim], carry=carry, name=name, grid=(BD, 2, nch),
        in_specs=[u_spec, d_spec, b_spec, b_spec, c_spec, c_spec, l_spec, l_spec],
        out_specs=[y_spec, y_spec, h_spec, h_spec, e_spec, e_spec],
        out_shape=[jax.ShapeDtypeStruct((s, SSM_W), BF16), jax.ShapeDtypeStruct((s, SSM_W), F32),
                   jax.ShapeDtypeStruct((s, STATE_W), BF16), jax.ShapeDtypeStruct((s, STATE_W), BF16),
                   jax.ShapeDtypeStruct((SEGS, STATE_W), F32), jax.ShapeDtypeStruct((SEGS, STATE_W), F32)],
        scratch_shapes=[pltpu.VMEM((SEGS, SCAN_WC), F32)] * 2 + [pltpu.VMEM((rows, SCAN_WC), F32)] * 4,
        compiler_params=_cparams(("parallel", "arbitrary", "arbitrary")),
    )


def ssm_bwd(dyg_s, ys, u_s, h_re, h_im, hin_re, hin_im, dvec, w_bre_t, w_bim_t, w_cre_t, w_cim_neg_t, lre, lim, name, carry=None):
    s = u_s.shape[0]
    steps, tt, nch, rows, nsq = _scan_dims(s)
    half = 2 * SEGS

    def body(dyg_r, ys_r, u_r, hre_r, him_r, pre_r, pim_r, cin_re_r, cin_im_r, d_r, bre_r, bim_r, cre_r, cim_r, lre_r, lim_r,
             du_o, dbre_o, dbim_o, dcre_o, dcim_o, dlre_o, dlim_o, dd_o, st_re, st_im, x_re, x_im, g_re, g_im, hf_re, hf_im):
        ps, ch = pl.program_id(1), pl.program_id(2)
        a_re = jnp.broadcast_to(lre_r[...], (SEGS, SCAN_WC))
        a_im = -jnp.broadcast_to(lim_r[...], (SEGS, SCAN_WC))
        ub, y = u_r[...], ys_r[...]
        dy = dyg_r[...] * _gelu_grad(y, _gelu(y)[1])
        dy16 = dy.astype(BF16)
        x_re[...] = jnp.dot(dy16, cre_r[...], preferred_element_type=F32)
        x_im[...] = jnp.dot(dy16, cim_r[...], preferred_element_type=F32)

        @pl.when(jnp.logical_and(ps == 0, ch == 0))
        def _():
            st_re[...] = jnp.zeros_like(st_re)
            st_im[...] = jnp.zeros_like(st_im)

        @pl.when(jnp.logical_and(ps == 1, ch == 0))
        def _():
            c_re, c_im = _chain_segments(a_re, a_im, st_re[...], st_im[...], nsq, True)
            st_re[...] = c_re
            st_im[...] = c_im
            dlre_o[...] = jnp.zeros_like(dlre_o)
            dlim_o[...] = jnp.zeros_like(dlim_o)

        @pl.when(ps == 0)
        def _():
            def step(i, hc):
                off = pl.multiple_of((tt - 1 - i) * SEGS, SEGS)
                return (a_re * hc[0] - a_im * hc[1] + x_re[pl.ds(off, SEGS), :],
                        a_re * hc[1] + a_im * hc[0] + x_im[pl.ds(off, SEGS), :])

            fin = lax.fori_loop(0, tt, step, (st_re[...], st_im[...]))
            st_re[...] = fin[0]
            st_im[...] = fin[1]

        @pl.when(ps == 1)
        def _():
            hf_re[...] = hre_r[...].astype(F32)
            hf_im[...] = him_r[...].astype(F32)
            first_chunk = ch == nch - 1
            edge_re = jnp.where(first_chunk, cin_re_r[...], pre_r[...].astype(F32)[SEGS:, :])
            edge_im = jnp.where(first_chunk, cin_im_r[...], pim_r[...].astype(F32)[SEGS:, :])

            def step(i, hc):
                t = tt - 1 - i
                off = pl.multiple_of(t * SEGS, SEGS)
                n_re = a_re * hc[0] - a_im * hc[1] + x_re[pl.ds(off, SEGS), :]
                n_im = a_re * hc[1] + a_im * hc[0] + x_im[pl.ds(off, SEGS), :]
                g_re[pl.ds(off, SEGS), :] = n_re
                g_im[pl.ds(off, SEGS), :] = n_im
                offp = pl.multiple_of(jnp.maximum(t - 1, 0) * SEGS, SEGS)
                hp_re = jnp.where(t == 0, edge_re, hf_re[pl.ds(offp, SEGS), :])
                hp_im = jnp.where(t == 0, edge_im, hf_im[pl.ds(offp, SEGS), :])
                return n_re, n_im, hc[2] + hp_re * n_re + hp_im * n_im, hc[3] + hp_re * n_im - hp_im * n_re

            fin = lax.fori_loop(0, tt, step, (st_re[...], st_im[...], dlre_o[...], dlim_o[...]))
            st_re[...] = fin[0]
            st_im[...] = fin[1]
            dlre_o[...] = fin[2]
            dlim_o[...] = fin[3]

            gr16, gi16 = g_re[...].astype(BF16), g_im[...].astype(BF16)
            du = jnp.dot(gr16, bre_r[...], preferred_element_type=F32) + jnp.dot(gi16, bim_r[...], preferred_element_type=F32)
            du_o[...] = du + d_r[...] * dy
            ub16 = ub.astype(BF16)
            parts = [
                (dbre_o, lax.dot_general(ub16, gr16, _DN["tn"], preferred_element_type=F32)),
                (dbim_o, lax.dot_general(ub16, gi16, _DN["tn"], preferred_element_type=F32)),
                (dcre_o, lax.dot_general(hre_r[...], dy16, _DN["tn"], preferred_element_type=F32)),
                (dcim_o, lax.dot_general(him_r[...], dy16, _DN["tn"], preferred_element_type=F32)),
                (dd_o, jnp.sum(dy * ub, axis=0, keepdims=True)),
            ]
            for ref, val in parts:
                @pl.when(ch == 0)
                def _(ref=ref, val=val):
                    ref[...] = val

                @pl.when(ch > 0)
                def _(ref=ref, val=val):
                    ref[...] += val

    def chunk(c):
        return nch - 1 - c

    def pass1(ps, c):
        return jnp.where(ps == 1, chunk(c), chunk(0))

    u_spec = pl.BlockSpec((rows, U_BLK), lambda j, ps, c: (chunk(c), j))
    h_spec = pl.BlockSpec((rows, SCAN_WC), lambda j, ps, c: (pass1(ps, c), j))
    prev_spec = pl.BlockSpec((half, SCAN_WC), lambda j, ps, c: (jnp.maximum(pass1(ps, c) * (rows // half) - 1, 0), j))
    e_spec = pl.BlockSpec((SEGS, SCAN_WC), lambda j, ps, c: (0, j))
    d_spec = pl.BlockSpec((1, U_BLK), lambda j, ps, c: (0, j))
    bt_spec = pl.BlockSpec((None, SCAN_WC, U_BLK), lambda j, ps, c: (j, 0, 0))
    ct_spec = pl.BlockSpec((None, U_BLK, SCAN_WC), lambda j, ps, c: (j, 0, 0))
    l_spec = pl.BlockSpec((1, SCAN_WC), lambda j, ps, c: (0, j))
    du_spec = pl.BlockSpec((rows, U_BLK), lambda j, ps, c: (pass1(ps, c), j))
    return _run(
        body, [dyg_s, ys, u_s, h_re, h_im, h_re, h_im, hin_re, hin_im, dvec, w_bre_t, w_bim_t, w_cre_t, w_cim_neg_t, lre, lim],
        carry=carry, name=name, grid=(BD, 2, nch),
        in_specs=[u_spec, u_spec, u_spec, h_spec, h_spec, prev_spec, prev_spec, e_spec, e_spec, d_spec, bt_spec, bt_spec,
                  ct_spec, ct_spec, l_spec, l_spec],
        out_specs=[du_spec, ct_spec, ct_spec, bt_spec, bt_spec, e_spec, e_spec, d_spec],
        out_shape=[jax.ShapeDtypeStruct((s, SSM_W), F32)] + [jax.ShapeDtypeStruct((BD, U_BLK, SCAN_WC), F32)] * 2
        + [jax.ShapeDtypeStruct((BD, SCAN_WC, U_BLK), F32)] * 2 + [jax.ShapeDtypeStruct((SEGS, STATE_W), F32)] * 2
        + [jax.ShapeDtypeStruct((1, SSM_W), F32)],
        scratch_shapes=[pltpu.VMEM((SEGS, SCAN_WC), F32)] * 2 + [pltpu.VMEM((rows, SCAN_WC), F32)] * 6,
        compiler_params=_cparams(("parallel", "arbitrary", "arbitrary")),
    )


def _block_diag(m):
    g, r, c = m.shape
    m = m.reshape(BD, g // BD, r, c)
    eye = jnp.eye(g // BD, dtype=m.dtype)
    return jnp.einsum("jarc,ab->jarbc", m, eye).reshape(BD, (g // BD) * r, (g // BD) * c)


def _block_diag_extract(m, r, c):
    per = m.shape[1] // r
    m = m.reshape(BD, per, r, per, c)
    return jnp.einsum("jarac->jarc", m).reshape(BD * per, r, c)


def dilate(a, d):
    s, w = a.shape
    if d == 1:
        return a.reshape(1, s, w)
    return a.reshape(s // d, d, w).transpose(1, 0, 2)


def undilate(a):
    d, length, w = a.shape
    if d == 1:
        return a.reshape(length, w)
    return a.transpose(1, 0, 2).reshape(d * length, w)


def to_segments(a):
    s, w = a.shape
    return a.reshape(SEGS, s // SEGS, w).transpose(1, 0, 2).reshape(s, w)


def from_segments(a):
    s, w = a.shape
    return a.reshape(s // SEGS, SEGS, w).transpose(1, 0, 2).reshape(s, w)


W_IN_CHUNKS = 4
FFN_TN = 512


def local_step(x, target, shards, small):
    s = x.shape[0]
    g1, g2, g3, g4 = (small[k].reshape(1, D_MODEL) for k in ("norm_mix_pre", "norm_mix_post", "norm_ffn_pre", "norm_ffn_post"))
    dvec = small["ssm_d"].reshape(1, SSM_W)
    wts, recv = {}, {}

    def gathered(names, blocks):
        for n, b in zip(names, blocks):
            wts[n] = _full_from_gathered(b, n)

    (h,), got = rowwise("rms_in", lambda r, c: ([_rms(r[0], c[0])[0]], []), [x], [g1], [(D_MODEL, BF16)],
                        carry=Gather([shards["w_in"]]))
    w_in_f = _full_from_gathered(got[0], "w_in")
    w_qkv = [jnp.concatenate([w_in_f[:, o + g * GROUP_W:o + (g + 1) * GROUP_W] for o in (0, HQ, 2 * HQ)], axis=1) for g in range(3)]
    w_u, w_gates = w_in_f[:, 3 * HQ:3 * HQ + SSM_W], w_in_f[:, 3 * HQ + SSM_W:]
    hd = [h.reshape(1, s, D_MODEL), dilate(h, 4), dilate(h, 16)]
    qkv = [None] * 3
    names = ("w_attn_up", "w_glu_v", "w_glu_g")
    qkv[0], got = mm([(hd[0].reshape(s, D_MODEL), w_qkv[0])], "nn", BF16, "mm_qkv0", carry=Gather([shards[n] for n in names]))
    gathered(names, got)
    qkv[1], got = mm([(hd[1].reshape(s, D_MODEL), w_qkv[1])], "nn", BF16, "mm_qkv1", carry=Gather([shards["w_out"]]))
    gathered(("w_out",), got)
    qkv[2] = mm([(hd[2].reshape(s, D_MODEL), w_qkv[2])], "nn", BF16, "mm_qkv2")
    u = mm([(h, w_u)], "nn", F32, "mm_u")
    gates, got = mm([(h, w_gates)], "nn", BF16, "mm_gates", carry=Gather([shards["w_ffn_gate"]]))
    gathered(("w_ffn_gate",), got)

    outs, lses, lses_dilated = [], [], []
    for g, (_, dil) in enumerate(ATTN_GROUPS):
        o, l = attn_fwd(qkv[g].reshape(dil, s // dil, 3 * GROUP_W), g, f"attn_fwd{g}")
        outs.append(undilate(o))
        lses.append(undilate(l))
        lses_dilated.append(l)

    def merge_fn(r, c):
        w0, w1, w2 = _mix_weights(r[3], r[4], r[5])
        return [w0 * r[0] + w1 * r[1] + w2 * r[2]], []

    (attn,) = rowwise("attn_merge", merge_fn, outs + lses, [], [(GROUP_W, BF16)])
    attn_branch = mm([(attn, wts["w_attn_up"])], "nn", BF16, "mm_up")

    are3 = small["ssm_a_re"].reshape(SSM_GROUPS, SSM_STATE, 1)
    aim3 = small["ssm_a_im"].reshape(SSM_GROUPS, SSM_STATE, 1)
    ldt3 = small["ssm_log_dt"].reshape(SSM_GROUPS, 1, 1)
    bre3 = small["ssm_b_re"].reshape(SSM_GROUPS, SSM_STATE, SSM_GROUP)
    bim3 = small["ssm_b_im"].reshape(SSM_GROUPS, SSM_STATE, SSM_GROUP)
    cre3 = small["ssm_c_re"].reshape(SSM_GROUPS, SSM_GROUP, SSM_STATE)
    cim3 = small["ssm_c_im"].reshape(SSM_GROUPS, SSM_GROUP, SSM_STATE)
    lre3, lim3, bbre, bbim = ssm_prep(are3, aim3, ldt3, bre3, bim3)
    lre, lim = lre3.reshape(1, STATE_W), lim3.reshape(1, STATE_W)
    w_bre = _block_diag(bbre.transpose(0, 2, 1)).astype(BF16)
    w_bim = _block_diag(bbim.transpose(0, 2, 1)).astype(BF16)
    w_cre = _block_diag(cre3.transpose(0, 2, 1)).astype(BF16)
    w_cim = _block_diag(cim3.transpose(0, 2, 1)).astype(BF16)
    u_s = to_segments(u)
    names = ("w_ffn_up", "w_ffn_down")
    (yg_s, y_ssm, h_re, h_im, hin_re, hin_im), got = ssm_fwd(
        u_s, dvec, w_bre, w_bim, w_cre, -w_cim, lre, lim, "ssm_fwd", carry=Gather([shards[n] for n in names]))
    gathered(names, got)
    yg = from_segments(yg_s)
    gv = mm([(yg, wts["w_glu_v"])], "nn", BF16, "mm_glu_v")
    gg = mm([(yg, wts["w_glu_g"])], "nn", BF16, "mm_glu_g")

    def gate_fn(r, c):
        gts, ab, gv_, gg_ = r
        sa, ss = _sigmoid(gts[:, :D_MODEL]), _sigmoid(gts[:, D_MODEL:])
        return [sa * ab + ss * (gv_ * _sigmoid(gg_))], []

    (merged,) = rowwise("gate_merge", gate_fn, [gates, attn_branch, gv, gg], [], [(D_MODEL, BF16)])
    o_mix = mm([(merged, wts["w_out"])], "nn", F32, "mm_out")

    def mid_fn(r, c):
        x1 = r[0] + _rms(r[1], c[0])[0]
        return [x1, _rms(x1, c[1])[0]], []

    x1, h2 = rowwise("rms_mid", mid_fn, [x, o_mix], [g2, g3], [(D_MODEL, F32), (D_MODEL, BF16)])
    fa, fb, fin = mm([(h2, wts["w_ffn_gate"]), (h2, wts["w_ffn_up"])], "nn", [BF16, BF16, BF16], "mm_ffn_in", tn=FFN_TN,
                     epilogue=lambda p, e: [p[0], p[1], p[0] * _sigmoid(p[0]) * p[1]])
    f = mm([(fin, wts["w_ffn_down"])], "nn", F32, "mm_ffn_down", tn=512, tk=D_FF)

    def loss_fn(r, c):
        x1_, f_, tgt = r
        y, n, rr = _rms(f_, c[0])
        err = x1_ + y - tgt
        dout = err * (1.0 / D_MODEL)
        df, dg = _rms_bwd(dout, n, rr, c[0])
        lp = 0.5 * jnp.sum(jnp.sum(err * err, axis=-1, keepdims=True) * (1.0 / D_MODEL), axis=0, keepdims=True)
        return [df, dout], [dg, lp]

    df, dout, dg4, loss_part = rowwise("loss_bwd", loss_fn, [x1, f, target], [g4], [(D_MODEL, BF16), (D_MODEL, F32)],
                                       acc_outs=[(1, D_MODEL), (1, 1)])
    def sent(names, blocks):
        for n, b in zip(names, blocks):
            recv[n] = b

    def to_owners(names, dws):
        return AllToAll([_split_for_devices(d, n) for n, d in zip(names, dws)])

    def swiglu_bwd(p, e):
        dfin_, (a, b) = p[0], e
        sg = _sigmoid(a)
        return [dfin_ * b * (sg * (1.0 + a * (1.0 - sg))), dfin_ * a * sg]

    da, db = mm([(df, wts["w_ffn_down"])], "nt", [BF16, BF16], "mm_d_fin", tn=FFN_TN, epilogue=swiglu_bwd, extras=[fa, fb])
    dw_ffn_down = mm([(fin, df)], "tn", BF16, "mm_dw_ffn_down")
    dw_ffn_gate, got = mm([(h2, da)], "tn", BF16, "mm_dw_ffn_gate", carry=to_owners(["w_ffn_down"], [dw_ffn_down]))
    sent(["w_ffn_down"], got)
    dw_ffn_up, got = mm([(h2, db)], "tn", BF16, "mm_dw_ffn_up", carry=to_owners(["w_ffn_gate"], [dw_ffn_gate]))
    sent(["w_ffn_gate"], got)
    dh2, got = mm([(da, wts["w_ffn_gate"]), (db, wts["w_ffn_up"])], "nt", F32, "mm_d_h2", tm=512, tn=1024, tk=D_FF // 2,
                  carry=to_owners(["w_ffn_up"], [dw_ffn_up]))
    sent(["w_ffn_up"], got)

    def mid_bwd(r, c):
        dh2_, dout_, x1_, o_ = r
        _, n3, r3 = _rms(x1_, c[1])
        dx1, dg3_ = _rms_bwd(dh2_, n3, r3, c[1])
        dx1 = dx1 + dout_
        _, n2, r2 = _rms(o_, c[0])
        do_, dg2_ = _rms_bwd(dx1, n2, r2, c[0])
        return [dx1, do_], [dg2_, dg3_]

    dx1, do_mix, dg2, dg3 = rowwise("rms_mid_bwd", mid_bwd, [dh2, dout, x1, o_mix], [g2, g3], [(D_MODEL, F32), (D_MODEL, BF16)],
                                    acc_outs=[(1, D_MODEL), (1, D_MODEL)])
    dmerged = mm([(do_mix, wts["w_out"])], "nt", BF16, "mm_d_merged")
    dw_out = mm([(merged, do_mix)], "tn", BF16, "mm_dw_out")

    def gate_bwd(r, c):
        dm, gts, ab, gv_, gg_ = r
        sa, ss, sg = _sigmoid(gts[:, :D_MODEL]), _sigmoid(gts[:, D_MODEL:]), _sigmoid(gg_)
        branch = gv_ * sg
        dbranch = dm * ss
        dgates = jnp.concatenate([dm * ab * sa * (1.0 - sa), dm * branch * ss * (1.0 - ss)], axis=-1)
        return [dgates, dm * sa, dbranch * sg, dbranch * gv_ * sg * (1.0 - sg)], []

    dgates, dab, dgv, dgg = rowwise("gate_bwd", gate_bwd, [dmerged, gates, attn_branch, gv, gg], [],
                                    [(2 * D_MODEL, BF16), (D_MODEL, BF16), (D_MODEL, BF16), (D_MODEL, BF16)])
    dattn = mm([(dab, wts["w_attn_up"])], "nt", F32, "mm_d_attn")
    dw_up = mm([(attn, dab)], "tn", BF16, "mm_dw_up")
    dyg = mm([(dgv, wts["w_glu_v"]), (dgg, wts["w_glu_g"])], "nt", F32, "mm_d_yg")
    dw_glu_v = mm([(yg, dgv)], "tn", BF16, "mm_dw_glu_v")
    dw_glu_g = mm([(yg, dgg)], "tn", BF16, "mm_dw_glu_g")

    names = ["w_out", "w_attn_up", "w_glu_v", "w_glu_g"]
    (du_s, dbre_d, dbim_d, dcre_d, dcim_d, dl_re8, dl_im8, dd_ssm), got = ssm_bwd(
        to_segments(dyg), y_ssm, u_s, h_re, h_im, hin_re, hin_im, dvec, w_bre.transpose(0, 2, 1), w_bim.transpose(0, 2, 1),
        w_cre.transpose(0, 2, 1), -w_cim.transpose(0, 2, 1), lre, lim, "ssm_bwd",
        carry=to_owners(names, [dw_out, dw_up, dw_glu_v, dw_glu_g]))
    sent(names, got)
    dbb_re = _block_diag_extract(dbre_d, SSM_GROUP, SSM_STATE).transpose(0, 2, 1)
    dbb_im = _block_diag_extract(dbim_d, SSM_GROUP, SSM_STATE).transpose(0, 2, 1)
    dc_re = _block_diag_extract(dcre_d, SSM_STATE, SSM_GROUP).transpose(0, 2, 1)
    dc_im = -_block_diag_extract(dcim_d, SSM_STATE, SSM_GROUP).transpose(0, 2, 1)

    def fold8(r, c):
        return [], [jnp.sum(r[0], axis=0, keepdims=True), jnp.sum(r[1], axis=0, keepdims=True)]

    dl_re, dl_im = rowwise("ssm_dl_fold", fold8, [dl_re8, dl_im8], [], [], acc_outs=[(1, STATE_W), (1, STATE_W)], ts=SEGS)
    da_re, da_im, dldt, db_re, db_im = ssm_prep_bwd(
        are3, aim3, ldt3, bre3, bim3, dbb_re, dbb_im,
        dl_re.reshape(SSM_GROUPS, SSM_STATE, 1), dl_im.reshape(SSM_GROUPS, SSM_STATE, 1))
    du = from_segments(du_s)

    def merge_bwd(r, c):
        dat, o0, o1, o2, l0, l1, l2 = r
        w0, w1, w2 = _mix_weights(l0, l1, l2)
        tot = _head_sum(dat * (w0 * o0 + w1 * o1 + w2 * o2))
        return [w0 * dat, w1 * dat, w2 * dat, w0 * tot, w1 * tot, w2 * tot], []

    mb = rowwise("attn_merge_bwd", merge_bwd, [dattn] + outs + lses, [], [(GROUP_W, BF16)] * 3 + [(GROUP_W, F32)] * 3)
    dqs, dw_qkv = [], []
    for g, (_, dil) in enumerate(ATTN_GROUPS):
        dq = attn_bwd(qkv[g].reshape(dil, s // dil, 3 * GROUP_W), dilate(mb[g], dil), lses_dilated[g],
                      dilate(mb[3 + g], dil), g, f"attn_bwd{g}").reshape(s, 3 * GROUP_W)
        dqs.append(dq)
        dw_qkv.append(mm([(hd[g].reshape(s, D_MODEL), dq)], "tn", BF16, f"mm_dw_qkv{g}"))
    dw_u = mm([(h, du)], "tn", BF16, "mm_dw_u")
    dw_gates = mm([(h, dgates)], "tn", BF16, "mm_dw_gates")
    dw_in = jnp.concatenate(
        [dw_qkv[g][:, o * GROUP_W:(o + 1) * GROUP_W] for o in range(3) for g in range(3)] + [dw_u, dw_gates], axis=1)
    dw_in_split = _split_for_devices(dw_in, "w_in")
    rows = D_MODEL // W_IN_CHUNKS
    chunks = [AllToAll([dw_in_split[:, i * rows:(i + 1) * rows]]) for i in range(W_IN_CHUNKS)]
    dh_parts, got_chunks = [], []
    for g, (_, dil) in enumerate(ATTN_GROUPS):
        dh_g, got = mm([(dqs[g], w_qkv[g])], "nt", BF16, f"mm_d_h_qkv{g}", carry=chunks[g])
        got_chunks.append(got[0])
        dh_parts.append(undilate(dh_g.reshape(dil, s // dil, D_MODEL)))
    dh_parts.append(mm([(du, w_u)], "nt", BF16, "mm_d_h_u"))
    dh_gates, got = mm([(dgates, w_gates)], "nt", BF16, "mm_d_h_gates", carry=chunks[3])
    got_chunks.append(got[0])
    dh_parts.append(dh_gates)
    recv["w_in"] = jnp.concatenate(got_chunks, axis=1)

    def in_bwd(r, c):
        dh = r[0] + r[1] + r[2] + r[3] + r[4]
        _, n1, r1 = _rms(r[6], c[0])
        dx, dg1_ = _rms_bwd(dh, n1, r1, c[0])
        return [dx + r[5]], [dg1_]

    grad_x, dg1 = rowwise("rms_in_bwd", in_bwd, dh_parts + [dx1, x], [g1], [(D_MODEL, F32)], acc_outs=[(1, D_MODEL)])

    dsmall = dict(norm_mix_pre=dg1, ssm_a_re=da_re, ssm_a_im=da_im, ssm_log_dt=dldt, ssm_b_re=db_re, ssm_b_im=db_im,
                  ssm_c_re=dc_re, ssm_c_im=dc_im, ssm_d=dd_ssm, norm_mix_post=dg2, norm_ffn_pre=dg3, norm_ffn_post=dg4)
    return loss_part, grad_x, recv, dsmall


def adamw(parts, w, m, v, name, carry=None):
    r, c = w.shape
    tr = r
    while tr > 8 and tr % 2 == 0 and tr * c * (8 * parts.dtype.itemsize + 28) * 2 > 24 * 1024 * 1024:
        tr //= 2
    assert r % tr == 0 and (tr % 8 == 0 or tr == r)
    c1, c2 = 1.0 / (1.0 - ADAM_B1 ** ADAM_STEP), 1.0 / (1.0 - ADAM_B2 ** ADAM_STEP)

    def body(p_ref, w_ref, m_ref, v_ref, g_o, d_o, m_o, v_o):
        g = p_ref[0].astype(F32)
        for i in range(1, N_DEV):
            g = g + p_ref[i].astype(F32)
        mn = ADAM_B1 * m_ref[...] + (1.0 - ADAM_B1) * g
        vn = ADAM_B2 * v_ref[...] + (1.0 - ADAM_B2) * (g * g)
        g_o[...] = g
        m_o[...] = mn
        v_o[...] = vn
        d_o[...] = -ADAM_LR * ((mn * c1) / (jnp.sqrt(vn * c2) + ADAM_EPS) + ADAM_WD * w_ref[...])

    blk = pl.BlockSpec((tr, c), lambda i: (i, 0))
    return _run(
        body, [parts, w, m, v], carry=carry, name=name, grid=(r // tr,),
        in_specs=[pl.BlockSpec((N_DEV, tr, c), lambda i: (0, i, 0)), blk, blk, blk],
        out_specs=[blk] * 4, out_shape=[jax.ShapeDtypeStruct((r, c), F32)] * 4, compiler_params=_cparams(("parallel",)),
    )


PACK_C = 1024
SHARDED = ("w_in", "w_attn_up", "w_glu_v", "w_glu_g", "w_out", "w_ffn_gate", "w_ffn_up", "w_ffn_down")
ROW_SHARDED = ("w_out", "w_ffn_down")
SMALL = ("norm_mix_pre", "ssm_a_re", "ssm_a_im", "ssm_log_dt", "ssm_b_re", "ssm_b_im", "ssm_c_re", "ssm_c_im", "ssm_d",
         "norm_mix_post", "norm_ffn_pre", "norm_ffn_post")
WEIGHTS = ("norm_mix_pre", "w_in", "w_attn_up", "ssm_a_re", "ssm_a_im", "ssm_log_dt", "ssm_b_re", "ssm_b_im", "ssm_c_re",
           "ssm_c_im", "ssm_d", "w_glu_v", "w_glu_g", "w_out", "norm_mix_post", "norm_ffn_pre", "w_ffn_gate", "w_ffn_up",
           "w_ffn_down", "norm_ffn_post")


def _pack(arrs, dtype, pad_rows_to=64):
    flat = jnp.concatenate([a.reshape(-1).astype(dtype) for a in arrs])
    n = flat.shape[0]
    rows = -(-n // PACK_C)
    rows = -(-rows // pad_rows_to) * pad_rows_to
    return jnp.pad(flat, (0, rows * PACK_C - n)).reshape(rows, PACK_C)


def _unpack(flat2d, shapes):
    flat = flat2d.reshape(-1)
    out, off = [], 0
    for shp in shapes:
        n = int(np.prod(shp))
        out.append(flat[off:off + n].reshape(shp))
        off += n
    return out


def _full_from_gathered(gathered, name):
    if name in ROW_SHARDED:
        return gathered.reshape(-1, gathered.shape[2])
    return gathered.transpose(1, 0, 2).reshape(gathered.shape[1], -1)


def _split_for_devices(full, name):
    if name in ROW_SHARDED:
        return full.reshape(N_DEV, -1, full.shape[1])
    return full.reshape(full.shape[0], N_DEV, -1).transpose(1, 0, 2)


def kernel(x, norm_mix_pre, w_in, w_attn_up, ssm_a_re, ssm_a_im, ssm_log_dt, ssm_b_re, ssm_b_im, ssm_c_re, ssm_c_im, ssm_d, w_glu_v, w_glu_g, w_out, norm_mix_post, norm_ffn_pre, w_ffn_gate, w_ffn_up, w_ffn_down, norm_ffn_post, loss_target, m_norm_mix_pre, m_w_in, m_w_attn_up, m_ssm_a_re, m_ssm_a_im, m_ssm_log_dt, m_ssm_b_re, m_ssm_b_im, m_ssm_c_re, m_ssm_c_im, m_ssm_d, m_w_glu_v, m_w_glu_g, m_w_out, m_norm_mix_post, m_norm_ffn_pre, m_w_ffn_gate, m_w_ffn_up, m_w_ffn_down, m_norm_ffn_post, v_norm_mix_pre, v_w_in, v_w_attn_up, v_ssm_a_re, v_ssm_a_im, v_ssm_log_dt, v_ssm_b_re, v_ssm_b_im, v_ssm_c_re, v_ssm_c_im, v_ssm_d, v_w_glu_v, v_w_glu_g, v_w_out, v_norm_mix_post, v_norm_ffn_pre, v_w_ffn_gate, v_w_ffn_up, v_w_ffn_down, v_norm_ffn_post):
    args = dict(locals())
    wv = {n: args[n][0] for n in WEIGHTS}
    mv = {n: args["m_" + n][0] for n in WEIGHTS}
    vv = {n: args["v_" + n][0] for n in WEIGHTS}

    shards = {n: wv[n].astype(BF16) for n in SHARDED}
    small = {n: wv[n] for n in SMALL}
    loss_part, grad_x, recv, dsmall = local_step(x[0], loss_target[0], shards, small)

    small_shapes = [wv[n].shape for n in SMALL]
    res = {}
    res["w_in"], (sgather,) = adamw(recv["w_in"], wv["w_in"], mv["w_in"], vv["w_in"], "adamw_w_in",
                                    carry=Gather([_pack([dsmall[n] for n in SMALL], F32)]))
    for n in SHARDED[1:]:
        res[n] = adamw(recv[n], wv[n], mv[n], vv[n], "adamw_" + n)
    sres = adamw(sgather, _pack([wv[n] for n in SMALL], F32), _pack([mv[n] for n in SMALL], F32),
                 _pack([vv[n] for n in SMALL], F32), "adamw_small")
    sun = [_unpack(t, small_shapes) for t in sres]
    for k, n in enumerate(SMALL):
        res[n] = tuple(sun[t][k] for t in range(4))

    loss = lax.psum(loss_part[0, 0], ("x", "y", "c"))
    outs = [loss, grad_x[None]]
    for t in range(4):
        outs += [res[n][t][None] for n in WEIGHTS]
    return tuple(outs)
```

```python
import functools
import math

import numpy as np
import jax
import jax.numpy as jnp
from jax import lax
from jax.experimental import pallas as pl
from jax.experimental.pallas import tpu as pltpu

F32 = jnp.float32
BF16 = jnp.bfloat16

D_MODEL = 2048
HEAD_DIM = 128
HEADS_PER_GROUP = 4
ATTN_GROUPS = ((128, 1), (512, 4), (2048, 16))
N_HEADS = HEADS_PER_GROUP * len(ATTN_GROUPS)
GROUP_W = HEADS_PER_GROUP * HEAD_DIM
HQ = N_HEADS * HEAD_DIM
SSM_W = 1024
SSM_GROUP = 16
SSM_GROUPS = 64
SSM_STATE = 64
STATE_W = SSM_GROUPS * SSM_STATE
D_FF = 5632
EPS = 1e-6
N_DEV = 8
SEGS = 8
BD = 8

ADAM_LR, ADAM_B1, ADAM_B2, ADAM_EPS, ADAM_WD, ADAM_STEP = 0.001, 0.9, 0.999, 1e-08, 0.01, 10

VMEM_LIMIT = 56 * 1024 * 1024
HBM_SPEC = pl.BlockSpec(memory_space=pltpu.HBM)
MESH_ID = pl.DeviceIdType.MESH
NEG = -1e30


def _pcall(body, **kw):
    return pl.pallas_call(body, **kw)


def _cparams(sem=None):
    if sem is None:
        return pltpu.CompilerParams(vmem_limit_bytes=VMEM_LIMIT)
    return pltpu.CompilerParams(vmem_limit_bytes=VMEM_LIMIT, dimension_semantics=sem)


def _my_coords():
    return lax.axis_index("x"), lax.axis_index("y"), lax.axis_index("c")


class Gather:
    def __init__(self, xs):
        self.arrays = list(xs)
        self.out_shapes = [jax.ShapeDtypeStruct((N_DEV,) + x.shape, x.dtype) for x in xs]

    def _ctx(self, out_refs, send_sems, recv_sems):
        mx, my, mc = _my_coords()
        me, sibling = (mx, my, mc), (mx, my, 1 - mc)
        chips = [(1 - mx, my), (mx, 1 - my), (1 - mx, 1 - my)]

        def slot(a, px, py, pc):
            return out_refs[a].at[4 * px + 2 * py + pc]

        def copy(a, k, block, to, src=None):
            return pltpu.make_async_remote_copy(
                src_ref=slot(a, *block) if src is None else src, dst_ref=slot(a, *block),
                send_sem=send_sems.at[7 * a + k], recv_sem=recv_sems.at[7 * a + k], device_id=to, device_id_type=MESH_ID)

        return me, sibling, chips, mc, slot, copy

    def _first(self, a, x_refs, ctx):
        me, sibling, chips, mc, slot, copy = ctx
        return [copy(a, 0, me, sibling, src=x_refs[a])] + [copy(a, 1 + j, me, (*chip, mc), src=x_refs[a]) for j, chip in enumerate(chips)]

    def start(self, x_refs, out_refs, send_sems, recv_sems, local_sems):
        ctx = self._ctx(out_refs, send_sems, recv_sems)
        me, slot = ctx[0], ctx[4]
        for a in range(len(self.arrays)):
            pltpu.make_async_copy(x_refs[a], slot(a, *me), local_sems.at[a]).start()
            for cp in self._first(a, x_refs, ctx):
                cp.start()

    def finish(self, x_refs, out_refs, send_sems, recv_sems, local_sems):
        ctx = self._ctx(out_refs, send_sems, recv_sems)
        me, sibling, chips, mc, slot, copy = ctx
        na = len(self.arrays)
        passed = []
        for a in range(na):
            for j, chip in enumerate(chips):
                copy(a, 1 + j, (*chip, mc), me).wait_recv()
                fwd = copy(a, 4 + j, (*chip, mc), sibling)
                fwd.start()
                passed.append(fwd)
        for a in range(na):
            copy(a, 0, sibling, me).wait_recv()
            for j, chip in enumerate(chips):
                copy(a, 4 + j, (*chip, 1 - mc), me).wait_recv()
        for a in range(na):
            for cp in self._first(a, x_refs, ctx):
                cp.wait_send()
        for cp in passed:
            cp.wait_send()
        for a in range(na):
            pltpu.make_async_copy(x_refs[a], slot(a, *me), local_sems.at[a]).wait()


class AllToAll:
    def __init__(self, ps):
        self.arrays = list(ps)
        self.out_shapes = [jax.ShapeDtypeStruct(p.shape, p.dtype) for p in ps]

    def _copies(self, p_refs, out_refs, send_sems, recv_sems, local_sems):
        mx, my, mc = _my_coords()
        me = 4 * mx + 2 * my + mc
        local, remote = [], []
        for a in range(len(self.arrays)):
            local.append(pltpu.make_async_copy(p_refs[a].at[me], out_refs[a].at[me], local_sems.at[a]))
            for k in range(1, N_DEV):
                px, py, pc = mx ^ ((k >> 2) & 1), my ^ ((k >> 1) & 1), mc ^ (k & 1)
                remote.append(pltpu.make_async_remote_copy(
                    src_ref=p_refs[a].at[4 * px + 2 * py + pc], dst_ref=out_refs[a].at[me],
                    send_sem=send_sems.at[7 * a + k - 1], recv_sem=recv_sems.at[7 * a + k - 1],
                    device_id=(px, py, pc), device_id_type=MESH_ID))
        return local, remote

    def start(self, *refs):
        local, remote = self._copies(*refs)
        for cp in local + remote:
            cp.start()

    def finish(self, *refs):
        local, remote = self._copies(*refs)
        for cp in remote:
            cp.wait_recv()
        for cp in remote:
            cp.wait_send()
        for cp in local:
            cp.wait()


def _run(body, args, carry=None, **kw):
    if carry is None:
        return _pcall(body, **kw)(*args)
    grid = kw["grid"]
    single = not isinstance(kw["out_shape"], (list, tuple))
    in_specs = list(kw["in_specs"])
    out_specs = [kw["out_specs"]] if single else list(kw["out_specs"])
    out_shape = [kw["out_shape"]] if single else list(kw["out_shape"])
    scratch = list(kw.get("scratch_shapes", []))
    na, nin, nout, nscr = len(carry.arrays), len(in_specs), len(out_specs), len(scratch)

    def carried(*refs):
        ins, cin = refs[:nin], refs[nin:nin + na]
        outs, cout = refs[nin + na:nin + na + nout], refs[nin + na + nout:nin + 2 * na + nout]
        scr = refs[nin + 2 * na + nout:nin + 2 * na + nout + nscr]
        sems = refs[nin + 2 * na + nout + nscr:]
        ids = [pl.program_id(i) for i in range(len(grid))]
        first, last = ids[0] == 0, ids[0] == grid[0] - 1
        for i in range(1, len(grid)):
            first = jnp.logical_and(first, ids[i] == 0)
            last = jnp.logical_and(last, ids[i] == grid[i] - 1)

        @pl.when(first)
        def _():
            carry.start(cin, cout, *sems)

        body(*ins, *outs, *scr)

        @pl.when(last)
        def _():
            carry.finish(cin, cout, *sems)

    res = _pcall(
        carried, name=kw["name"], grid=grid, in_specs=in_specs + [HBM_SPEC] * na, out_specs=out_specs + [HBM_SPEC] * na,
        out_shape=out_shape + carry.out_shapes,
        scratch_shapes=scratch + [pltpu.SemaphoreType.DMA((7 * na,)), pltpu.SemaphoreType.DMA((7 * na,)), pltpu.SemaphoreType.DMA((na,))],
        compiler_params=_cparams(("arbitrary",) * len(grid)),
    )(*args, *carry.arrays)
    main = res[:nout]
    return (main[0] if single else main), list(res[nout:])


_DN = {"nn": (((1,), (0,)), ((), ())), "nt": (((1,), (1,)), ((), ())), "tn": (((0,), (0,)), ((), ()))}


LANE = 128
MM_TM, MM_TN, MM_TK = 1024, 1536, 2048


def _tile(n, cap):
    for t in range(min(cap, n) // LANE * LANE, 0, -LANE):
        if n % t == 0:
            return t
    raise ValueError(n)


DW_TM, DW_TN, DW_TK = 512, 512, 8192


def mm(pairs, mode, out_dtype, name, tm=None, tn=None, tk=None, carry=None, epilogue=None, extras=()):
    a0, b0 = pairs[0]
    if mode == "nn":
        (m, k), n = a0.shape, b0.shape[1]
    elif mode == "nt":
        (m, k), n = a0.shape, b0.shape[0]
    else:
        (k, m), n = a0.shape, b0.shape[1]
    caps = (DW_TM, DW_TN, DW_TK) if mode == "tn" else (MM_TM, MM_TN, MM_TK)
    tm, tn, tk = _tile(m, tm or caps[0]), _tile(n, tn or caps[1]), _tile(k, tk or caps[2])
    nk = k // tk
    npairs = len(pairs)
    nex = len(extras)
    fused = epilogue is not None
    assert not fused or nk == 1
    out_dtypes = list(out_dtype) if fused else [out_dtype]

    def body(*refs):
        prods = []
        for p in range(npairs):
            a = refs[2 * p][...].astype(BF16) if (p == 0 or pairs[p][0] is not pairs[p - 1][0]) else a
            b = refs[2 * p + 1][...].astype(BF16)
            prods.append(lax.dot_general(a, b, _DN[mode], preferred_element_type=F32))
        if fused:
            ex = [refs[2 * npairs + e][...].astype(F32) for e in range(nex)]
            for o_ref, val in zip(refs[2 * npairs + nex:], epilogue(prods, ex)):
                o_ref[...] = val.astype(o_ref.dtype)
            return
        o_ref = refs[2 * npairs]
        tot = prods[0]
        for d in prods[1:]:
            tot = tot + d
        if nk == 1:
            o_ref[...] = tot.astype(o_ref.dtype)
            return
        acc = refs[2 * npairs + 1]
        kk = pl.program_id(2)

        @pl.when(kk == 0)
        def _():
            acc[...] = tot

        @pl.when(kk > 0)
        def _():
            acc[...] += tot

        @pl.when(kk == nk - 1)
        def _():
            o_ref[...] = acc[...].astype(o_ref.dtype)

    if mode == "nn":
        sp = [pl.BlockSpec((tm, tk), lambda i, j, kk: (i, kk)), pl.BlockSpec((tk, tn), lambda i, j, kk: (kk, j))]
    elif mode == "nt":
        sp = [pl.BlockSpec((tm, tk), lambda i, j, kk: (i, kk)), pl.BlockSpec((tn, tk), lambda i, j, kk: (j, kk))]
    else:
        sp = [pl.BlockSpec((tk, tm), lambda i, j, kk: (kk, i)), pl.BlockSpec((tk, tn), lambda i, j, kk: (kk, j))]
    o_spec = pl.BlockSpec((tm, tn), lambda i, j, kk: (i, j))
    out_shapes = [jax.ShapeDtypeStruct((m, n), dt) for dt in out_dtypes]
    return _run(
        body, [t for pr in pairs for t in pr] + list(extras), carry=carry, name=name, grid=(m // tm, n // tn, nk),
        in_specs=sp * npairs + [o_spec] * nex,
        out_specs=[o_spec] * len(out_shapes) if fused else o_spec,
        out_shape=out_shapes if fused else out_shapes[0],
        scratch_shapes=[pltpu.VMEM((tm, tn), F32)] if nk > 1 else [],
        compiler_params=_cparams(("parallel", "parallel", "arbitrary")),
    )


def rowwise(name, fn, row_ins, const_ins, row_outs, acc_outs=(), ts=None, carry=None):
    s = row_ins[0].shape[0]
    row_outs = [ro if len(ro) == 3 else (*ro, 1) for ro in row_outs]
    if ts is None:
        per_row = sum(a.shape[-1] * a.dtype.itemsize for a in row_ins) + sum(w * jnp.dtype(dt).itemsize for w, dt, _ in row_outs)
        ts = 512
        while ts > 8 and 2 * ts * per_row > 20 * 1024 * 1024:
            ts //= 2
    ts = min(ts, s)
    assert s % ts == 0
    nr, nc, no, na = len(row_ins), len(const_ins), len(row_outs), len(acc_outs)

    def body(*refs):
        rows = [r[...].reshape(ts, r.shape[-1]).astype(F32) for r in refs[:nr]]
        consts = [r[...] for r in refs[nr:nr + nc]]
        outs, accs = fn(rows, consts)
        for r, v in zip(refs[nr + nc:nr + nc + no], outs):
            r[...] = v.astype(r.dtype).reshape(r.shape)
        if na:
            first = pl.program_id(0) == 0
            for r, v in zip(refs[nr + nc + no:], accs):
                @pl.when(first)
                def _(r=r, v=v):
                    r[...] = v

                @pl.when(jnp.logical_not(first))
                def _(r=r, v=v):
                    r[...] += v

    def tile_spec(w, d):
        if d == 1:
            return pl.BlockSpec((ts, w), lambda i: (i, 0))
        return pl.BlockSpec((d, ts // d, w), lambda i: (0, i, 0))

    in_specs = [tile_spec(a.shape[-1], a.shape[0] if a.ndim == 3 else 1) for a in row_ins]
    in_specs += [pl.BlockSpec(c.shape, lambda i, nd=c.ndim: (0,) * nd) for c in const_ins]
    out_specs = [tile_spec(w, d) for w, _, d in row_outs]
    out_specs += [pl.BlockSpec(shp, lambda i, nd=len(shp): (0,) * nd) for shp in acc_outs]
    out_shape = [jax.ShapeDtypeStruct((s, w) if d == 1 else (d, s // d, w), dt) for w, dt, d in row_outs]
    out_shape += [jax.ShapeDtypeStruct(shp, F32) for shp in acc_outs]
    return _run(
        body, [*row_ins, *const_ins], carry=carry, name=name, grid=(s // ts,), in_specs=in_specs, out_specs=out_specs,
        out_shape=out_shape, compiler_params=_cparams(("arbitrary",)),
    )


PERM_TS = 256


def _perm_matrix(ts, d, inverse):
    i = lax.broadcasted_iota(jnp.int32, (ts, ts), 0)
    k = lax.broadcasted_iota(jnp.int32, (ts, ts), 1)
    per = ts // d
    src = (i % d) * per + i // d if inverse else (i % per) * d + i // per
    return jnp.where(k == src, 1.0, 0.0).astype(BF16)


def _permute(p, x):
    if x.dtype == BF16:
        return jnp.dot(p, x, preferred_element_type=F32)
    hi = x.astype(BF16)
    rest = x - hi.astype(F32)
    mid = rest.astype(BF16)
    lo = (rest - mid.astype(F32)).astype(BF16)
    out = jnp.dot(p, hi, preferred_element_type=F32) + jnp.dot(p, mid, preferred_element_type=F32)
    return out + jnp.dot(p, lo, preferred_element_type=F32)


def _rms(x, gain):
    r = lax.rsqrt(jnp.mean(x * x, axis=-1, keepdims=True) + EPS)
    n = x * r
    return n * gain, n, r


def _rms_bwd(dy, n, r, gain):
    dn = dy * gain
    dx = r * (dn - n * jnp.mean(dn * n, axis=-1, keepdims=True))
    return dx, jnp.sum(dy * n, axis=0, keepdims=True)


def _sigmoid(x):
    return 1.0 / (1.0 + jnp.exp(-x))


_GELU_K = math.sqrt(2.0 / math.pi)


def _gelu(x):
    t = jnp.tanh(_GELU_K * (x + 0.044715 * x * x * x))
    return 0.5 * x * (1.0 + t), t


def _gelu_grad(x, t):
    return 0.5 * (1.0 + t) + 0.5 * x * (1.0 - t * t) * _GELU_K * (1.0 + 3.0 * 0.044715 * x * x)


def _head_sum(x):
    parts = []
    for h in range(HEADS_PER_GROUP):
        sl = x[:, h * HEAD_DIM:(h + 1) * HEAD_DIM]
        parts.append(jnp.broadcast_to(jnp.sum(sl, axis=-1, keepdims=True), sl.shape))
    return jnp.concatenate(parts, axis=-1)


def _mix_weights(l0, l1, l2):
    mx = jnp.maximum(jnp.maximum(l0, l1), l2)
    e0, e1, e2 = jnp.exp(l0 - mx), jnp.exp(l1 - mx), jnp.exp(l2 - mx)
    inv = 1.0 / (e0 + e1 + e2)
    return e0 * inv, e1 * inv, e2 * inv


BLK = 128


def _slopes(g):
    return [2.0 ** (-8.0 * (g * HEADS_PER_GROUP + h + 1) / N_HEADS) for h in range(HEADS_PER_GROUP)]


def _attn_masks(dil):
    qi = lax.broadcasted_iota(jnp.int32, (BLK, BLK), 0)
    ki = lax.broadcasted_iota(jnp.int32, (BLK, BLK), 1)
    dist_c = qi - ki
    dist_p = BLK + qi - ki
    return dist_c >= 0, dist_p <= BLK, (dist_c * dil).astype(F32), (dist_p * dil).astype(F32)


def attn_fwd(qkv, g, name):
    dil, length, _ = qkv.shape
    scale = HEAD_DIM ** -0.5
    slopes = _slopes(g)

    def body(q_ref, kc_ref, vc_ref, kp_ref, vp_ref, o_ref, l_ref):
        n = pl.program_id(1)
        ok_c, ok_p, dc, dp = _attn_masks(dil)
        ok_p = jnp.logical_and(ok_p, n > 0)
        for h in range(HEADS_PER_GROUP):
            sl = slice(h * HEAD_DIM, (h + 1) * HEAD_DIM)
            q = q_ref[:, sl]
            s_c = lax.dot_general(q, kc_ref[:, sl], _DN["nt"], preferred_element_type=F32) * scale - slopes[h] * dc
            s_p = lax.dot_general(q, kp_ref[:, sl], _DN["nt"], preferred_element_type=F32) * scale - slopes[h] * dp
            s_c = jnp.where(ok_c, s_c, NEG)
            s_p = jnp.where(ok_p, s_p, NEG)
            mx = jnp.maximum(jnp.max(s_c, axis=-1, keepdims=True), jnp.max(s_p, axis=-1, keepdims=True))
            p_c = jnp.exp(s_c - mx)
            p_p = jnp.exp(s_p - mx)
            den = jnp.sum(p_c, axis=-1, keepdims=True) + jnp.sum(p_p, axis=-1, keepdims=True)
            acc = jnp.dot(p_c.astype(BF16), vc_ref[:, sl], preferred_element_type=F32)
            acc += jnp.dot(p_p.astype(BF16), vp_ref[:, sl], preferred_element_type=F32)
            o_ref[:, sl] = acc / den
            l_ref[:, sl] = jnp.broadcast_to(mx + jnp.log(den), (BLK, HEAD_DIM))

    def spec(col, prev):
        if prev:
            return pl.BlockSpec((None, BLK, GROUP_W), lambda r, n: (r, jnp.maximum(n - 1, 0), col))
        return pl.BlockSpec((None, BLK, GROUP_W), lambda r, n: (r, n, col))

    out_spec = pl.BlockSpec((None, BLK, GROUP_W), lambda r, n: (r, n, 0))
    return _pcall(
        body, name=name, grid=(dil, length // BLK),
        in_specs=[spec(0, False), spec(1, False), spec(2, False), spec(1, True), spec(2, True)],
        out_specs=[out_spec, out_spec],
        out_shape=[jax.ShapeDtypeStruct((dil, length, GROUP_W), F32)] * 2,
        compiler_params=_cparams(("parallel", "parallel")),
    )(qkv, qkv, qkv, qkv, qkv)


def attn_bwd(qkv, dout, lse, dd, g, name, carry=None):
    dil, length, _ = qkv.shape
    nblk = length // BLK
    scale = HEAD_DIM ** -0.5
    slopes = _slopes(g)

    def body(q_ref, kc_ref, vc_ref, kp_ref, vp_ref, qn_ref, do_ref, don_ref, l_ref, ln_ref, d_ref, dn_ref, o_ref):
        n = pl.program_id(1)
        ok_c, ok_p, dc, dp = _attn_masks(dil)
        ok_prev = jnp.logical_and(ok_p, n > 0)
        ok_next = jnp.logical_and(ok_p, n < nblk - 1)
        for h in range(HEADS_PER_GROUP):
            sl = slice(h * HEAD_DIM, (h + 1) * HEAD_DIM)
            q, kc, vc, kp, vp, qn = q_ref[:, sl], kc_ref[:, sl], vc_ref[:, sl], kp_ref[:, sl], vp_ref[:, sl], qn_ref[:, sl]
            do, don = do_ref[:, sl], don_ref[:, sl]
            lse_q, lse_n, dd_q, dd_n = l_ref[:, sl], ln_ref[:, sl], d_ref[:, sl], dn_ref[:, sl]

            def probs(qq, kk, dist, ok, lse_t):
                s = lax.dot_general(qq, kk, _DN["nt"], preferred_element_type=F32) * scale - slopes[h] * dist
                return jnp.where(ok, jnp.exp(jnp.where(ok, s, NEG) - lse_t), 0.0)

            p_c = probs(q, kc, dc, ok_c, lse_q)
            p_p = probs(q, kp, dp, ok_prev, lse_q)
            p_x = probs(qn, kc, dp, ok_next, lse_n)
            ds_c = p_c * (lax.dot_general(do, vc, _DN["nt"], preferred_element_type=F32) - dd_q)
            ds_p = p_p * (lax.dot_general(do, vp, _DN["nt"], preferred_element_type=F32) - dd_q)
            ds_x = p_x * (lax.dot_general(don, vc, _DN["nt"], preferred_element_type=F32) - dd_n)
            ds_c16, ds_p16, ds_x16 = ds_c.astype(BF16), ds_p.astype(BF16), ds_x.astype(BF16)
            dq = jnp.dot(ds_c16, kc, preferred_element_type=F32) + jnp.dot(ds_p16, kp, preferred_element_type=F32)
            dk = lax.dot_general(ds_c16, q, _DN["tn"], preferred_element_type=F32)
            dk += lax.dot_general(ds_x16, qn, _DN["tn"], preferred_element_type=F32)
            dv = lax.dot_general(p_c.astype(BF16), do, _DN["tn"], preferred_element_type=F32)
            dv += lax.dot_general(p_x.astype(BF16), don, _DN["tn"], preferred_element_type=F32)
            o_ref[:, h * HEAD_DIM:(h + 1) * HEAD_DIM] = (dq * scale).astype(BF16)
            o_ref[:, GROUP_W + h * HEAD_DIM:GROUP_W + (h + 1) * HEAD_DIM] = (dk * scale).astype(BF16)
            o_ref[:, 2 * GROUP_W + h * HEAD_DIM:2 * GROUP_W + (h + 1) * HEAD_DIM] = dv.astype(BF16)

    def spec(col, which):
        if which == "prev":
            return pl.BlockSpec((None, BLK, GROUP_W), lambda r, n: (r, jnp.maximum(n - 1, 0), col))
        if which == "next":
            return pl.BlockSpec((None, BLK, GROUP_W), lambda r, n: (r, jnp.minimum(n + 1, nblk - 1), col))
        return pl.BlockSpec((None, BLK, GROUP_W), lambda r, n: (r, n, col))

    return _run(
        body, [qkv, qkv, qkv, qkv, qkv, qkv, dout, dout, lse, lse, dd, dd], carry=carry, name=name, grid=(dil, nblk),
        in_specs=[spec(0, "cur"), spec(1, "cur"), spec(2, "cur"), spec(1, "prev"), spec(2, "prev"), spec(0, "next"),
                  spec(0, "cur"), spec(0, "next"), spec(0, "cur"), spec(0, "next"), spec(0, "cur"), spec(0, "next")],
        out_specs=pl.BlockSpec((None, BLK, 3 * GROUP_W), lambda r, n: (r, n, 0)),
        out_shape=jax.ShapeDtypeStruct((dil, length, 3 * GROUP_W), BF16),
        compiler_params=_cparams(("parallel", "parallel")),
    )


def _ssm_prep_values(are, aim, logdt):
    dt = jnp.exp(logdt)
    mag = jnp.exp(are * dt)
    lb_re, lb_im = mag * jnp.cos(aim * dt), mag * jnp.sin(aim * dt)
    inv = 1.0 / (are * are + aim * aim)
    n_re, n_im = lb_re - 1.0, lb_im
    f_re = (n_re * are + n_im * aim) * inv
    f_im = (n_im * are - n_re * aim) * inv
    return dt, lb_re, lb_im, f_re, f_im, inv


PREP_G = 8


def _group_specs(are, logdt, bre):
    def spec(a):
        return pl.BlockSpec((PREP_G,) + a.shape[1:], lambda i: (i, 0, 0))
    return spec(are), spec(logdt), spec(bre)


def ssm_prep(are, aim, logdt, bre, bim):
    def body(are_r, aim_r, ldt_r, bre_r, bim_r, lre_o, lim_o, bbre_o, bbim_o):
        _, lb_re, lb_im, f_re, f_im, _ = _ssm_prep_values(are_r[...], aim_r[...], ldt_r[...])
        lre_o[...] = lb_re
        lim_o[...] = lb_im
        bbre_o[...] = f_re * bre_r[...] - f_im * bim_r[...]
        bbim_o[...] = f_re * bim_r[...] + f_im * bre_r[...]

    sh1 = jax.ShapeDtypeStruct(are.shape, F32)
    shb = jax.ShapeDtypeStruct(bre.shape, F32)
    s1, sd, sb = _group_specs(are, logdt, bre)
    return _pcall(body, name="ssm_prep", grid=(SSM_GROUPS // PREP_G,), in_specs=[s1, s1, sd, sb, sb], out_specs=[s1, s1, sb, sb],
                  out_shape=[sh1, sh1, shb, shb], compiler_params=_cparams(("parallel",)))(are, aim, logdt, bre, bim)


def ssm_prep_bwd(are, aim, logdt, bre, bim, dbbre, dbbim, dlre, dlim):
    def body(are_r, aim_r, ldt_r, bre_r, bim_r, dbbre_r, dbbim_r, dlre_r, dlim_r, dare_o, daim_o, dldt_o, dbre_o, dbim_o):
        are_v, aim_v = are_r[...], aim_r[...]
        dt, lb_re, lb_im, f_re, f_im, inv = _ssm_prep_values(are_v, aim_v, ldt_r[...])
        b_re, b_im, g_re, g_im = bre_r[...], bim_r[...], dbbre_r[...], dbbim_r[...]
        dbre_o[...] = f_re * g_re + f_im * g_im
        dbim_o[...] = f_re * g_im - f_im * g_re
        df_re = jnp.sum(b_re * g_re + b_im * g_im, axis=-1, keepdims=True)
        df_im = jnp.sum(b_re * g_im - b_im * g_re, axis=-1, keepdims=True)
        il_re, il_im = are_v * inv, -aim_v * inv
        cl_re = dlre_r[...] + il_re * df_re + il_im * df_im
        cl_im = dlim_r[...] + il_re * df_im - il_im * df_re
        q_re = -(f_re * il_re - f_im * il_im)
        q_im = -(f_re * il_im + f_im * il_re)
        ca_re = q_re * df_re + q_im * df_im
        ca_im = q_re * df_im - q_im * df_re
        cz_re = lb_re * cl_re + lb_im * cl_im
        cz_im = lb_re * cl_im - lb_im * cl_re
        dare_o[...] = ca_re + dt * cz_re
        daim_o[...] = ca_im + dt * cz_im
        dldt_o[...] = dt * jnp.sum(are_v * cz_re + aim_v * cz_im, axis=1, keepdims=True)

    sh1 = jax.ShapeDtypeStruct(are.shape, F32)
    shb = jax.ShapeDtypeStruct(bre.shape, F32)
    s1, sd, sb = _group_specs(are, logdt, bre)
    return _pcall(
        body, name="ssm_prep_bwd", grid=(SSM_GROUPS // PREP_G,), in_specs=[s1, s1, sd, sb, sb, sb, sb, s1, s1],
        out_specs=[s1, s1, sd, sb, sb], out_shape=[sh1, sh1, jax.ShapeDtypeStruct(logdt.shape, F32), shb, shb],
        compiler_params=_cparams(("parallel",)),
    )(are, aim, logdt, bre, bim, dbbre, dbbim, dlre, dlim)


SCAN_WC = 512


def _chain_segments(a_re, a_im, e_re, e_im, nsq, reverse):
    p_re, p_im = a_re, a_im
    for _ in range(nsq):
        p_re, p_im = p_re * p_re - p_im * p_im, 2.0 * p_re * p_im
    row = lax.broadcasted_iota(jnp.int32, e_re.shape, 0)
    edge = (row == SEGS - 1) if reverse else (row == 0)
    shift = SEGS - 1 if reverse else 1
    c_re, c_im = jnp.zeros_like(e_re), jnp.zeros_like(e_im)
    for _ in range(SEGS - 1):
        n_re = p_re * c_re - p_im * c_im + e_re
        n_im = p_re * c_im + p_im * c_re + e_im
        c_re = jnp.where(edge, 0.0, pltpu.roll(n_re, shift, 0))
        c_im = jnp.where(edge, 0.0, pltpu.roll(n_im, shift, 0))
    return c_re, c_im


def _scan_dims(s):
    steps = s // SEGS
    assert steps & (steps - 1) == 0
    tt = min(128, steps)
    return steps, tt, steps // tt, tt * SEGS, int(math.log2(steps))


U_BLK = SSM_W // BD


def ssm_fwd(u_s, dvec, w_bre, w_bim, w_cre, w_cim_neg, lre, lim, name, carry=None):
    s = u_s.shape[0]
    steps, tt, nch, rows, nsq = _scan_dims(s)

    def body(u_r, d_r, bre_r, bim_r, cre_r, cim_r, lre_r, lim_r, yg_o, ys_o, hre_o, him_o, hin_re_o, hin_im_o,
             st_re, st_im, x_re, x_im, h_re, h_im):
        ps, ch = pl.program_id(1), pl.program_id(2)
        a_re = jnp.broadcast_to(lre_r[...], (SEGS, SCAN_WC))
        a_im = jnp.broadcast_to(lim_r[...], (SEGS, SCAN_WC))
        ub = u_r[...]
        ub16 = ub.astype(BF16)
        x_re[...] = jnp.dot(ub16, bre_r[...], preferred_element_type=F32)
        x_im[...] = jnp.dot(ub16, bim_r[...], preferred_element_type=F32)

        @pl.when(jnp.logical_and(ps == 0, ch == 0))
        def _():
            st_re[...] = jnp.zeros_like(st_re)
            st_im[...] = jnp.zeros_like(st_im)

        @pl.when(jnp.logical_and(ps == 1, ch == 0))
        def _():
            c_re, c_im = _chain_segments(a_re, a_im, st_re[...], st_im[...], nsq, False)
            st_re[...] = c_re
            st_im[...] = c_im
            hin_re_o[...] = c_re
            hin_im_o[...] = c_im

        def run(store):
            def step(t, hc):
                off = pl.multiple_of(t * SEGS, SEGS)
                n_re = a_re * hc[0] - a_im * hc[1] + x_re[pl.ds(off, SEGS), :]
                n_im = a_re * hc[1] + a_im * hc[0] + x_im[pl.ds(off, SEGS), :]
                if store:
                    h_re[pl.ds(off, SEGS), :] = n_re
                    h_im[pl.ds(off, SEGS), :] = n_im
                return n_re, n_im

            fin = lax.fori_loop(0, tt, step, (st_re[...], st_im[...]))
            st_re[...] = fin[0]
            st_im[...] = fin[1]

        @pl.when(ps == 0)
        def _():
            run(False)

        @pl.when(ps == 1)
        def _():
            run(True)
            hr16, hi16 = h_re[...].astype(BF16), h_im[...].astype(BF16)
            hre_o[...] = hr16
            him_o[...] = hi16
            y = jnp.dot(hr16, cre_r[...], preferred_element_type=F32) + jnp.dot(hi16, cim_r[...], preferred_element_type=F32)
            y = y + d_r[...] * ub
            ys_o[...] = y
            yg_o[...] = _gelu(y)[0].astype(BF16)

    def pass1(ps, c):
        return jnp.where(ps == 1, c, 0)

    u_spec = pl.BlockSpec((rows, U_BLK), lambda j, ps, c: (c, j))
    d_spec = pl.BlockSpec((1, U_BLK), lambda j, ps, c: (0, j))
    b_spec = pl.BlockSpec((None, U_BLK, SCAN_WC), lambda j, ps, c: (j, 0, 0))
    c_spec = pl.BlockSpec((None, SCAN_WC, U_BLK), lambda j, ps, c: (j, 0, 0))
    l_spec = pl.BlockSpec((1, SCAN_WC), lambda j, ps, c: (0, j))
    y_spec = pl.BlockSpec((rows, U_BLK), lambda j, ps, c: (pass1(ps, c), j))
    h_spec = pl.BlockSpec((rows, SCAN_WC), lambda j, ps, c: (pass1(ps, c), j))
    e_spec = pl.BlockSpec((SEGS, SCAN_WC), lambda j, ps, c: (0, j))
    return _run(
        body, [u_s, dvec, w_bre, w_bim, w_cre, w_cim_neg, lre, lim], carry=carry, name=name, grid=(BD, 2, nch),
        in_specs=[u_spec, d_spec, b_spec, b_spec, c_spec, c_spec, l_spec, l_spec],
        out_specs=[y_spec, y_spec, h_spec, h_spec, e_spec, e_spec],
        out_shape=[jax.ShapeDtypeStruct((s, SSM_W), BF16), jax.ShapeDtypeStruct((s, SSM_W), F32),
                   jax.ShapeDtypeStruct((s, STATE_W), BF16), jax.ShapeDtypeStruct((s, STATE_W), BF16),
                   jax.ShapeDtypeStruct((SEGS, STATE_W), F32), jax.ShapeDtypeStruct((SEGS, STATE_W), F32)],
        scratch_shapes=[pltpu.VMEM((SEGS, SCAN_WC), F32)] * 2 + [pltpu.VMEM((rows, SCAN_WC), F32)] * 4,
        compiler_params=_cparams(("parallel", "arbitrary", "arbitrary")),
    )


def ssm_bwd(dyg_s, ys, u_s, h_re, h_im, hin_re, hin_im, dvec, w_bre_t, w_bim_t, w_cre_t, w_cim_neg_t, lre, lim, name, carry=None):
    s = u_s.shape[0]
    steps, tt, nch, rows, nsq = _scan_dims(s)
    half = 2 * SEGS

    def body(dyg_r, ys_r, u_r, hre_r, him_r, pre_r, pim_r, cin_re_r, cin_im_r, d_r, bre_r, bim_r, cre_r, cim_r, lre_r, lim_r,
             du_o, dbre_o, dbim_o, dcre_o, dcim_o, dlre_o, dlim_o, dd_o, st_re, st_im, x_re, x_im, g_re, g_im, hf_re, hf_im):
        ps, ch = pl.program_id(1), pl.program_id(2)
        a_re = jnp.broadcast_to(lre_r[...], (SEGS, SCAN_WC))
        a_im = -jnp.broadcast_to(lim_r[...], (SEGS, SCAN_WC))
        ub, y = u_r[...], ys_r[...]
        dy = dyg_r[...] * _gelu_grad(y, _gelu(y)[1])
        dy16 = dy.astype(BF16)
        x_re[...] = jnp.dot(dy16, cre_r[...], preferred_element_type=F32)
        x_im[...] = jnp.dot(dy16, cim_r[...], preferred_element_type=F32)

        @pl.when(jnp.logical_and(ps == 0, ch == 0))
        def _():
            st_re[...] = jnp.zeros_like(st_re)
            st_im[...] = jnp.zeros_like(st_im)

        @pl.when(jnp.logical_and(ps == 1, ch == 0))
        def _():
            c_re, c_im = _chain_segments(a_re, a_im, st_re[...], st_im[...], nsq, True)
            st_re[...] = c_re
            st_im[...] = c_im
            dlre_o[...] = jnp.zeros_like(dlre_o)
            dlim_o[...] = jnp.zeros_like(dlim_o)

        @pl.when(ps == 0)
        def _():
            def step(i, hc):
                off = pl.multiple_of((tt - 1 - i) * SEGS, SEGS)
                return (a_re * hc[0] - a_im * hc[1] + x_re[pl.ds(off, SEGS), :],
                        a_re * hc[1] + a_im * hc[0] + x_im[pl.ds(off, SEGS), :])

            fin = lax.fori_loop(0, tt, step, (st_re[...], st_im[...]))
            st_re[...] = fin[0]
            st_im[...] = fin[1]

        @pl.when(ps == 1)
        def _():
            hf_re[...] = hre_r[...].astype(F32)
            hf_im[...] = him_r[...].astype(F32)
            first_chunk = ch == nch - 1
            edge_re = jnp.where(first_chunk, cin_re_r[...], pre_r[...].astype(F32)[SEGS:, :])
            edge_im = jnp.where(first_chunk, cin_im_r[...], pim_r[...].astype(F32)[SEGS:, :])

            def step(i, hc):
                t = tt - 1 - i
                off = pl.multiple_of(t * SEGS, SEGS)
                n_re = a_re * hc[0] - a_im * hc[1] + x_re[pl.ds(off, SEGS), :]
                n_im = a_re * hc[1] + a_im * hc[0] + x_im[pl.ds(off, SEGS), :]
                g_re[pl.ds(off, SEGS), :] = n_re
                g_im[pl.ds(off, SEGS), :] = n_im
                offp = pl.multiple_of(jnp.maximum(t - 1, 0) * SEGS, SEGS)
                hp_re = jnp.where(t == 0, edge_re, hf_re[pl.ds(offp, SEGS), :])
                hp_im = jnp.where(t == 0, edge_im, hf_im[pl.ds(offp, SEGS), :])
                return n_re, n_im, hc[2] + hp_re * n_re + hp_im * n_im, hc[3] + hp_re * n_im - hp_im * n_re

            fin = lax.fori_loop(0, tt, step, (st_re[...], st_im[...], dlre_o[...], dlim_o[...]))
            st_re[...] = fin[0]
            st_im[...] = fin[1]
            dlre_o[...] = fin[2]
            dlim_o[...] = fin[3]

            gr16, gi16 = g_re[...].astype(BF16), g_im[...].astype(BF16)
            du = jnp.dot(gr16, bre_r[...], preferred_element_type=F32) + jnp.dot(gi16, bim_r[...], preferred_element_type=F32)
            du_o[...] = du + d_r[...] * dy
            ub16 = ub.astype(BF16)
            parts = [
                (dbre_o, lax.dot_general(ub16, gr16, _DN["tn"], preferred_element_type=F32)),
                (dbim_o, lax.dot_general(ub16, gi16, _DN["tn"], preferred_element_type=F32)),
                (dcre_o, lax.dot_general(hre_r[...], dy16, _DN["tn"], preferred_element_type=F32)),
                (dcim_o, lax.dot_general(him_r[...], dy16, _DN["tn"], preferred_element_type=F32)),
                (dd_o, jnp.sum(dy * ub, axis=0, keepdims=True)),
            ]
            for ref, val in parts:
                @pl.when(ch == 0)
                def _(ref=ref, val=val):
                    ref[...] = val

                @pl.when(ch > 0)
                def _(ref=ref, val=val):
                    ref[...] += val

    def chunk(c):
        return nch - 1 - c

    def pass1(ps, c):
        return jnp.where(ps == 1, chunk(c), chunk(0))

    u_spec = pl.BlockSpec((rows, U_BLK), lambda j, ps, c: (chunk(c), j))
    h_spec = pl.BlockSpec((rows, SCAN_WC), lambda j, ps, c: (pass1(ps, c), j))
    prev_spec = pl.BlockSpec((half, SCAN_WC), lambda j, ps, c: (jnp.maximum(pass1(ps, c) * (rows // half) - 1, 0), j))
    e_spec = pl.BlockSpec((SEGS, SCAN_WC), lambda j, ps, c: (0, j))
    d_spec = pl.BlockSpec((1, U_BLK), lambda j, ps, c: (0, j))
    bt_spec = pl.BlockSpec((None, SCAN_WC, U_BLK), lambda j, ps, c: (j, 0, 0))
    ct_spec = pl.BlockSpec((None, U_BLK, SCAN_WC), lambda j, ps, c: (j, 0, 0))
    l_spec = pl.BlockSpec((1, SCAN_WC), lambda j, ps, c: (0, j))
    du_spec = pl.BlockSpec((rows, U_BLK), lambda j, ps, c: (pass1(ps, c), j))
    return _run(
        body, [dyg_s, ys, u_s, h_re, h_im, h_re, h_im, hin_re, hin_im, dvec, w_bre_t, w_bim_t, w_cre_t, w_cim_neg_t, lre, lim],
        carry=carry, name=name, grid=(BD, 2, nch),
        in_specs=[u_spec, u_spec, u_spec, h_spec, h_spec, prev_spec, prev_spec, e_spec, e_spec, d_spec, bt_spec, bt_spec,
                  ct_spec, ct_spec, l_spec, l_spec],
        out_specs=[du_spec, ct_spec, ct_spec, bt_spec, bt_spec, e_spec, e_spec, d_spec],
        out_shape=[jax.ShapeDtypeStruct((s, SSM_W), F32)] + [jax.ShapeDtypeStruct((BD, U_BLK, SCAN_WC), F32)] * 2
        + [jax.ShapeDtypeStruct((BD, SCAN_WC, U_BLK), F32)] * 2 + [jax.ShapeDtypeStruct((SEGS, STATE_W), F32)] * 2
        + [jax.ShapeDtypeStruct((1, SSM_W), F32)],
        scratch_shapes=[pltpu.VMEM((SEGS, SCAN_WC), F32)] * 2 + [pltpu.VMEM((rows, SCAN_WC), F32)] * 6,
        compiler_params=_cparams(("parallel", "arbitrary", "arbitrary")),
    )


def _block_diag(m):
    g, r, c = m.shape
    m = m.reshape(BD, g // BD, r, c)
    eye = jnp.eye(g // BD, dtype=m.dtype)
    return jnp.einsum("jarc,ab->jarbc", m, eye).reshape(BD, (g // BD) * r, (g // BD) * c)


def _block_diag_extract(m, r, c):
    per = m.shape[1] // r
    m = m.reshape(BD, per, r, per, c)
    return jnp.einsum("jarac->jarc", m).reshape(BD * per, r, c)


def to_segments(a):
    s, w = a.shape
    return a.reshape(SEGS, s // SEGS, w).transpose(1, 0, 2).reshape(s, w)


def from_segments(a):
    s, w = a.shape
    return a.reshape(s // SEGS, SEGS, w).transpose(1, 0, 2).reshape(s, w)


W_IN_CHUNKS = 4
FFN_TN = 512


def local_step(x, target, shards, small):
    s = x.shape[0]
    g1, g2, g3, g4 = (small[k].reshape(1, D_MODEL) for k in ("norm_mix_pre", "norm_mix_post", "norm_ffn_pre", "norm_ffn_post"))
    dvec = small["ssm_d"].reshape(1, SSM_W)
    wts, recv = {}, {}

    def gathered(names, blocks):
        for n, b in zip(names, blocks):
            wts[n] = _full_from_gathered(b, n)

    def rms_in_fn(r, c):
        hh = _rms(r[0], c[0])[0].astype(BF16)
        return [hh, _permute(_perm_matrix(PERM_TS, 4, False), hh), _permute(_perm_matrix(PERM_TS, 16, False), hh)], []

    (h, h4, h16), got = rowwise("rms_in", rms_in_fn, [x], [g1], [(D_MODEL, BF16), (D_MODEL, BF16, 4), (D_MODEL, BF16, 16)],
                                ts=PERM_TS, carry=Gather([shards["w_in"]]))
    w_in_f = _full_from_gathered(got[0], "w_in")
    w_qkv = [jnp.concatenate([w_in_f[:, o + g * GROUP_W:o + (g + 1) * GROUP_W] for o in (0, HQ, 2 * HQ)], axis=1) for g in range(3)]
    w_u, w_gates = w_in_f[:, 3 * HQ:3 * HQ + SSM_W], w_in_f[:, 3 * HQ + SSM_W:]
    hd = [h.reshape(1, s, D_MODEL), h4, h16]
    qkv = [None] * 3
    names = ("w_attn_up", "w_glu_v", "w_glu_g")
    qkv[0], got = mm([(hd[0].reshape(s, D_MODEL), w_qkv[0])], "nn", BF16, "mm_qkv0", carry=Gather([shards[n] for n in names]))
    gathered(names, got)
    qkv[1], got = mm([(hd[1].reshape(s, D_MODEL), w_qkv[1])], "nn", BF16, "mm_qkv1", carry=Gather([shards["w_out"]]))
    gathered(("w_out",), got)
    qkv[2] = mm([(hd[2].reshape(s, D_MODEL), w_qkv[2])], "nn", BF16, "mm_qkv2")
    u = mm([(h, w_u)], "nn", F32, "mm_u")
    gates, got = mm([(h, w_gates)], "nn", BF16, "mm_gates", carry=Gather([shards["w_ffn_gate"]]))
    gathered(("w_ffn_gate",), got)

    outs, lses = [], []
    for g, (_, dil) in enumerate(ATTN_GROUPS):
        o, l = attn_fwd(qkv[g].reshape(dil, s // dil, 3 * GROUP_W), g, f"attn_fwd{g}")
        outs.append(o.reshape(s, GROUP_W) if dil == 1 else o)
        lses.append(l.reshape(s, GROUP_W) if dil == 1 else l)

    def natural(r):
        back4, back16 = _perm_matrix(PERM_TS, 4, True), _perm_matrix(PERM_TS, 16, True)
        return r[0], _permute(back4, r[1]), _permute(back16, r[2]), r[3], _permute(back4, r[4]), _permute(back16, r[5])

    def merge_fn(r, c):
        o0, o1, o2, l0, l1, l2 = natural(r)
        w0, w1, w2 = _mix_weights(l0, l1, l2)
        return [w0 * o0 + w1 * o1 + w2 * o2], []

    (attn,) = rowwise("attn_merge", merge_fn, outs + lses, [], [(GROUP_W, BF16)], ts=PERM_TS)
    attn_branch = mm([(attn, wts["w_attn_up"])], "nn", BF16, "mm_up")

    are3 = small["ssm_a_re"].reshape(SSM_GROUPS, SSM_STATE, 1)
    aim3 = small["ssm_a_im"].reshape(SSM_GROUPS, SSM_STATE, 1)
    ldt3 = small["ssm_log_dt"].reshape(SSM_GROUPS, 1, 1)
    bre3 = small["ssm_b_re"].reshape(SSM_GROUPS, SSM_STATE, SSM_GROUP)
    bim3 = small["ssm_b_im"].reshape(SSM_GROUPS, SSM_STATE, SSM_GROUP)
    cre3 = small["ssm_c_re"].reshape(SSM_GROUPS, SSM_GROUP, SSM_STATE)
    cim3 = small["ssm_c_im"].reshape(SSM_GROUPS, SSM_GROUP, SSM_STATE)
    lre3, lim3, bbre, bbim = ssm_prep(are3, aim3, ldt3, bre3, bim3)
    lre, lim = lre3.reshape(1, STATE_W), lim3.reshape(1, STATE_W)
    w_bre = _block_diag(bbre.transpose(0, 2, 1)).astype(BF16)
    w_bim = _block_diag(bbim.transpose(0, 2, 1)).astype(BF16)
    w_cre = _block_diag(cre3.transpose(0, 2, 1)).astype(BF16)
    w_cim = _block_diag(cim3.transpose(0, 2, 1)).astype(BF16)
    u_s = to_segments(u)
    names = ("w_ffn_up", "w_ffn_down")
    (yg_s, y_ssm, h_re, h_im, hin_re, hin_im), got = ssm_fwd(
        u_s, dvec, w_bre, w_bim, w_cre, -w_cim, lre, lim, "ssm_fwd", carry=Gather([shards[n] for n in names]))
    gathered(names, got)
    yg = from_segments(yg_s)
    gv = mm([(yg, wts["w_glu_v"])], "nn", BF16, "mm_glu_v")
    gg = mm([(yg, wts["w_glu_g"])], "nn", BF16, "mm_glu_g")

    def gate_fn(r, c):
        gts, ab, gv_, gg_ = r
        sa, ss = _sigmoid(gts[:, :D_MODEL]), _sigmoid(gts[:, D_MODEL:])
        return [sa * ab + ss * (gv_ * _sigmoid(gg_))], []

    (merged,) = rowwise("gate_merge", gate_fn, [gates, attn_branch, gv, gg], [], [(D_MODEL, BF16)])
    o_mix = mm([(merged, wts["w_out"])], "nn", F32, "mm_out")

    def mid_fn(r, c):
        x1 = r[0] + _rms(r[1], c[0])[0]
        return [x1, _rms(x1, c[1])[0]], []

    x1, h2 = rowwise("rms_mid", mid_fn, [x, o_mix], [g2, g3], [(D_MODEL, F32), (D_MODEL, BF16)])
    fa, fb, fin = mm([(h2, wts["w_ffn_gate"]), (h2, wts["w_ffn_up"])], "nn", [BF16, BF16, BF16], "mm_ffn_in", tn=FFN_TN,
                     epilogue=lambda p, e: [p[0], p[1], p[0] * _sigmoid(p[0]) * p[1]])
    f = mm([(fin, wts["w_ffn_down"])], "nn", F32, "mm_ffn_down", tn=512, tk=D_FF)

    def loss_fn(r, c):
        x1_, f_, tgt = r
        y, n, rr = _rms(f_, c[0])
        err = x1_ + y - tgt
        dout = err * (1.0 / D_MODEL)
        df, dg = _rms_bwd(dout, n, rr, c[0])
        lp = 0.5 * jnp.sum(jnp.sum(err * err, axis=-1, keepdims=True) * (1.0 / D_MODEL), axis=0, keepdims=True)
        return [df, dout], [dg, lp]

    df, dout, dg4, loss_part = rowwise("loss_bwd", loss_fn, [x1, f, target], [g4], [(D_MODEL, BF16), (D_MODEL, F32)],
                                       acc_outs=[(1, D_MODEL), (1, 1)])
    def sent(names, blocks):
        for n, b in zip(names, blocks):
            recv[n] = b

    def to_owners(names, dws):
        return AllToAll([_split_for_devices(d, n) for n, d in zip(names, dws)])

    def swiglu_bwd(p, e):
        dfin_, (a, b) = p[0], e
        sg = _sigmoid(a)
        return [dfin_ * b * (sg * (1.0 + a * (1.0 - sg))), dfin_ * a * sg]

    da, db = mm([(df, wts["w_ffn_down"])], "nt", [BF16, BF16], "mm_d_fin", tn=FFN_TN, epilogue=swiglu_bwd, extras=[fa, fb])
    dw_ffn_down = mm([(fin, df)], "tn", BF16, "mm_dw_ffn_down")
    dw_ffn_gate, got = mm([(h2, da)], "tn", BF16, "mm_dw_ffn_gate", carry=to_owners(["w_ffn_down"], [dw_ffn_down]))
    sent(["w_ffn_down"], got)
    dw_ffn_up, got = mm([(h2, db)], "tn", BF16, "mm_dw_ffn_up", carry=to_owners(["w_ffn_gate"], [dw_ffn_gate]))
    sent(["w_ffn_gate"], got)
    dh2, got = mm([(da, wts["w_ffn_gate"]), (db, wts["w_ffn_up"])], "nt", F32, "mm_d_h2", tm=512, tn=1024, tk=D_FF // 2,
                  carry=to_owners(["w_ffn_up"], [dw_ffn_up]))
    sent(["w_ffn_up"], got)

    def mid_bwd(r, c):
        dh2_, dout_, x1_, o_ = r
        _, n3, r3 = _rms(x1_, c[1])
        dx1, dg3_ = _rms_bwd(dh2_, n3, r3, c[1])
        dx1 = dx1 + dout_
        _, n2, r2 = _rms(o_, c[0])
        do_, dg2_ = _rms_bwd(dx1, n2, r2, c[0])
        return [dx1, do_], [dg2_, dg3_]

    dx1, do_mix, dg2, dg3 = rowwise("rms_mid_bwd", mid_bwd, [dh2, dout, x1, o_mix], [g2, g3], [(D_MODEL, F32), (D_MODEL, BF16)],
                                    acc_outs=[(1, D_MODEL), (1, D_MODEL)])
    dmerged = mm([(do_mix, wts["w_out"])], "nt", BF16, "mm_d_merged")
    dw_out = mm([(merged, do_mix)], "tn", BF16, "mm_dw_out")

    def gate_bwd(r, c):
        dm, gts, ab, gv_, gg_ = r
        sa, ss, sg = _sigmoid(gts[:, :D_MODEL]), _sigmoid(gts[:, D_MODEL:]), _sigmoid(gg_)
        branch = gv_ * sg
        dbranch = dm * ss
        dgates = jnp.concatenate([dm * ab * sa * (1.0 - sa), dm * branch * ss * (1.0 - ss)], axis=-1)
        return [dgates, dm * sa, dbranch * sg, dbranch * gv_ * sg * (1.0 - sg)], []

    dgates, dab, dgv, dgg = rowwise("gate_bwd", gate_bwd, [dmerged, gates, attn_branch, gv, gg], [],
                                    [(2 * D_MODEL, BF16), (D_MODEL, BF16), (D_MODEL, BF16), (D_MODEL, BF16)])
    dattn = mm([(dab, wts["w_attn_up"])], "nt", F32, "mm_d_attn")
    dw_up = mm([(attn, dab)], "tn", BF16, "mm_dw_up")
    dyg = mm([(dgv, wts["w_glu_v"]), (dgg, wts["w_glu_g"])], "nt", F32, "mm_d_yg")
    dw_glu_v = mm([(yg, dgv)], "tn", BF16, "mm_dw_glu_v")
    dw_glu_g = mm([(yg, dgg)], "tn", BF16, "mm_dw_glu_g")

    names = ["w_out", "w_attn_up", "w_glu_v", "w_glu_g"]
    (du_s, dbre_d, dbim_d, dcre_d, dcim_d, dl_re8, dl_im8, dd_ssm), got = ssm_bwd(
        to_segments(dyg), y_ssm, u_s, h_re, h_im, hin_re, hin_im, dvec, w_bre.transpose(0, 2, 1), w_bim.transpose(0, 2, 1),
        w_cre.transpose(0, 2, 1), -w_cim.transpose(0, 2, 1), lre, lim, "ssm_bwd",
        carry=to_owners(names, [dw_out, dw_up, dw_glu_v, dw_glu_g]))
    sent(names, got)
    dbb_re = _block_diag_extract(dbre_d, SSM_GROUP, SSM_STATE).transpose(0, 2, 1)
    dbb_im = _block_diag_extract(dbim_d, SSM_GROUP, SSM_STATE).transpose(0, 2, 1)
    dc_re = _block_diag_extract(dcre_d, SSM_STATE, SSM_GROUP).transpose(0, 2, 1)
    dc_im = -_block_diag_extract(dcim_d, SSM_STATE, SSM_GROUP).transpose(0, 2, 1)

    def fold8(r, c):
        return [], [jnp.sum(r[0], axis=0, keepdims=True), jnp.sum(r[1], axis=0, keepdims=True)]

    dl_re, dl_im = rowwise("ssm_dl_fold", fold8, [dl_re8, dl_im8], [], [], acc_outs=[(1, STATE_W), (1, STATE_W)], ts=SEGS)
    da_re, da_im, dldt, db_re, db_im = ssm_prep_bwd(
        are3, aim3, ldt3, bre3, bim3, dbb_re, dbb_im,
        dl_re.reshape(SSM_GROUPS, SSM_STATE, 1), dl_im.reshape(SSM_GROUPS, SSM_STATE, 1))
    du = from_segments(du_s)

    def merge_bwd(r, c):
        dat = r[0]
        o0, o1, o2, l0, l1, l2 = natural(r[1:])
        w0, w1, w2 = _mix_weights(l0, l1, l2)
        tot = _head_sum(dat * (w0 * o0 + w1 * o1 + w2 * o2))
        to4, to16 = _perm_matrix(PERM_TS, 4, False), _perm_matrix(PERM_TS, 16, False)
        return [w0 * dat, _permute(to4, (w1 * dat).astype(BF16)), _permute(to16, (w2 * dat).astype(BF16)),
                w0 * tot, _permute(to4, w1 * tot), _permute(to16, w2 * tot)], []

    mb = rowwise("attn_merge_bwd", merge_bwd, [dattn] + outs + lses, [],
                 [(GROUP_W, BF16), (GROUP_W, BF16, 4), (GROUP_W, BF16, 16), (GROUP_W, F32), (GROUP_W, F32, 4), (GROUP_W, F32, 16)],
                 ts=PERM_TS)
    dqs, dw_qkv = [], []
    for g, (_, dil) in enumerate(ATTN_GROUPS):
        dq = attn_bwd(qkv[g].reshape(dil, s // dil, 3 * GROUP_W), mb[g].reshape(dil, s // dil, GROUP_W),
                      lses[g].reshape(dil, s // dil, GROUP_W), mb[3 + g].reshape(dil, s // dil, GROUP_W),
                      g, f"attn_bwd{g}").reshape(s, 3 * GROUP_W)
        dqs.append(dq)
        dw_qkv.append(mm([(hd[g].reshape(s, D_MODEL), dq)], "tn", BF16, f"mm_dw_qkv{g}"))
    dw_u = mm([(h, du)], "tn", BF16, "mm_dw_u")
    dw_gates = mm([(h, dgates)], "tn", BF16, "mm_dw_gates")
    dw_in = jnp.concatenate(
        [dw_qkv[g][:, o * GROUP_W:(o + 1) * GROUP_W] for o in range(3) for g in range(3)] + [dw_u, dw_gates], axis=1)
    dw_in_split = _split_for_devices(dw_in, "w_in")
    rows = D_MODEL // W_IN_CHUNKS
    chunks = [AllToAll([dw_in_split[:, i * rows:(i + 1) * rows]]) for i in range(W_IN_CHUNKS)]
    dh_parts, got_chunks = [], []
    for g, (_, dil) in enumerate(ATTN_GROUPS):
        dh_g, got = mm([(dqs[g], w_qkv[g])], "nt", BF16, f"mm_d_h_qkv{g}", carry=chunks[g])
        got_chunks.append(got[0])
        dh_parts.append(dh_g if dil == 1 else dh_g.reshape(dil, s // dil, D_MODEL))
    dh_parts.append(mm([(du, w_u)], "nt", BF16, "mm_d_h_u"))
    dh_gates, got = mm([(dgates, w_gates)], "nt", BF16, "mm_d_h_gates", carry=chunks[3])
    got_chunks.append(got[0])
    dh_parts.append(dh_gates)
    recv["w_in"] = jnp.concatenate(got_chunks, axis=1)

    def in_bwd(r, c):
        dh1 = _permute(_perm_matrix(PERM_TS, 4, True), r[1].astype(BF16))
        dh2_ = _permute(_perm_matrix(PERM_TS, 16, True), r[2].astype(BF16))
        dh = r[0] + dh1 + dh2_ + r[3] + r[4]
        _, n1, r1 = _rms(r[6], c[0])
        dx, dg1_ = _rms_bwd(dh, n1, r1, c[0])
        return [dx + r[5]], [dg1_]

    grad_x, dg1 = rowwise("rms_in_bwd", in_bwd, dh_parts + [dx1, x], [g1], [(D_MODEL, F32)], acc_outs=[(1, D_MODEL)], ts=PERM_TS)

    dsmall = dict(norm_mix_pre=dg1, ssm_a_re=da_re, ssm_a_im=da_im, ssm_log_dt=dldt, ssm_b_re=db_re, ssm_b_im=db_im,
                  ssm_c_re=dc_re, ssm_c_im=dc_im, ssm_d=dd_ssm, norm_mix_post=dg2, norm_ffn_pre=dg3, norm_ffn_post=dg4)
    return loss_part, grad_x, recv, dsmall


def adamw(parts, w, m, v, name, carry=None):
    r, c = w.shape
    tr = r
    while tr > 8 and tr % 2 == 0 and tr * c * (8 * parts.dtype.itemsize + 28) * 2 > 24 * 1024 * 1024:
        tr //= 2
    assert r % tr == 0 and (tr % 8 == 0 or tr == r)
    c1, c2 = 1.0 / (1.0 - ADAM_B1 ** ADAM_STEP), 1.0 / (1.0 - ADAM_B2 ** ADAM_STEP)

    def body(p_ref, w_ref, m_ref, v_ref, g_o, d_o, m_o, v_o):
        g = p_ref[0].astype(F32)
        for i in range(1, N_DEV):
            g = g + p_ref[i].astype(F32)
        mn = ADAM_B1 * m_ref[...] + (1.0 - ADAM_B1) * g
        vn = ADAM_B2 * v_ref[...] + (1.0 - ADAM_B2) * (g * g)
        g_o[...] = g
        m_o[...] = mn
        v_o[...] = vn
        d_o[...] = -ADAM_LR * ((mn * c1) / (jnp.sqrt(vn * c2) + ADAM_EPS) + ADAM_WD * w_ref[...])

    blk = pl.BlockSpec((tr, c), lambda i: (i, 0))
    return _run(
        body, [parts, w, m, v], carry=carry, name=name, grid=(r // tr,),
        in_specs=[pl.BlockSpec((N_DEV, tr, c), lambda i: (0, i, 0)), blk, blk, blk],
        out_specs=[blk] * 4, out_shape=[jax.ShapeDtypeStruct((r, c), F32)] * 4, compiler_params=_cparams(("parallel",)),
    )


PACK_C = 1024
SHARDED = ("w_in", "w_attn_up", "w_glu_v", "w_glu_g", "w_out", "w_ffn_gate", "w_ffn_up", "w_ffn_down")
ROW_SHARDED = ("w_out", "w_ffn_down")
SMALL = ("norm_mix_pre", "ssm_a_re", "ssm_a_im", "ssm_log_dt", "ssm_b_re", "ssm_b_im", "ssm_c_re", "ssm_c_im", "ssm_d",
         "norm_mix_post", "norm_ffn_pre", "norm_ffn_post")
WEIGHTS = ("norm_mix_pre", "w_in", "w_attn_up", "ssm_a_re", "ssm_a_im", "ssm_log_dt", "ssm_b_re", "ssm_b_im", "ssm_c_re",
           "ssm_c_im", "ssm_d", "w_glu_v", "w_glu_g", "w_out", "norm_mix_post", "norm_ffn_pre", "w_ffn_gate", "w_ffn_up",
           "w_ffn_down", "norm_ffn_post")


def _pack(arrs, dtype, pad_rows_to=64):
    flat = jnp.concatenate([a.reshape(-1).astype(dtype) for a in arrs])
    n = flat.shape[0]
    rows = -(-n // PACK_C)
    rows = -(-rows // pad_rows_to) * pad_rows_to
    return jnp.pad(flat, (0, rows * PACK_C - n)).reshape(rows, PACK_C)


def _unpack(flat2d, shapes):
    flat = flat2d.reshape(-1)
    out, off = [], 0
    for shp in shapes:
        n = int(np.prod(shp))
        out.append(flat[off:off + n].reshape(shp))
        off += n
    return out


def _full_from_gathered(gathered, name):
    if name in ROW_SHARDED:
        return gathered.reshape(-1, gathered.shape[2])
    return gathered.transpose(1, 0, 2).reshape(gathered.shape[1], -1)


def _split_for_devices(full, name):
    if name in ROW_SHARDED:
        return full.reshape(N_DEV, -1, full.shape[1])
    return full.reshape(full.shape[0], N_DEV, -1).transpose(1, 0, 2)


def kernel(x, norm_mix_pre, w_in, w_attn_up, ssm_a_re, ssm_a_im, ssm_log_dt, ssm_b_re, ssm_b_im, ssm_c_re, ssm_c_im, ssm_d, w_glu_v, w_glu_g, w_out, norm_mix_post, norm_ffn_pre, w_ffn_gate, w_ffn_up, w_ffn_down, norm_ffn_post, loss_target, m_norm_mix_pre, m_w_in, m_w_attn_up, m_ssm_a_re, m_ssm_a_im, m_ssm_log_dt, m_ssm_b_re, m_ssm_b_im, m_ssm_c_re, m_ssm_c_im, m_ssm_d, m_w_glu_v, m_w_glu_g, m_w_out, m_norm_mix_post, m_norm_ffn_pre, m_w_ffn_gate, m_w_ffn_up, m_w_ffn_down, m_norm_ffn_post, v_norm_mix_pre, v_w_in, v_w_attn_up, v_ssm_a_re, v_ssm_a_im, v_ssm_log_dt, v_ssm_b_re, v_ssm_b_im, v_ssm_c_re, v_ssm_c_im, v_ssm_d, v_w_glu_v, v_w_glu_g, v_w_out, v_norm_mix_post, v_norm_ffn_pre, v_w_ffn_gate, v_w_ffn_up, v_w_ffn_down, v_norm_ffn_post):
    args = dict(locals())
    wv = {n: args[n][0] for n in WEIGHTS}
    mv = {n: args["m_" + n][0] for n in WEIGHTS}
    vv = {n: args["v_" + n][0] for n in WEIGHTS}

    shards = {n: wv[n].astype(BF16) for n in SHARDED}
    small = {n: wv[n] for n in SMALL}
    loss_part, grad_x, recv, dsmall = local_step(x[0], loss_target[0], shards, small)

    small_shapes = [wv[n].shape for n in SMALL]
    res = {}
    res["w_in"], (sgather,) = adamw(recv["w_in"], wv["w_in"], mv["w_in"], vv["w_in"], "adamw_w_in",
                                    carry=Gather([_pack([dsmall[n] for n in SMALL], F32)]))
    for n in SHARDED[1:]:
        res[n] = adamw(recv[n], wv[n], mv[n], vv[n], "adamw_" + n)
    sres = adamw(sgather, _pack([wv[n] for n in SMALL], F32), _pack([mv[n] for n in SMALL], F32),
                 _pack([vv[n] for n in SMALL], F32), "adamw_small")
    sun = [_unpack(t, small_shapes) for t in sres]
    for k, n in enumerate(SMALL):
        res[n] = tuple(sun[t][k] for t in range(4))

    loss = lax.psum(loss_part[0, 0], ("x", "y", "c"))
    outs = [loss, grad_x[None]]
    for t in range(4):
        outs += [res[n][t][None] for n in WEIGHTS]
    return tuple(outs)
```

```python
import functools
import math

import numpy as np
import jax
import jax.numpy as jnp
from jax import lax
from jax.experimental import pallas as pl
from jax.experimental.pallas import tpu as pltpu

F32 = jnp.float32
BF16 = jnp.bfloat16

D_MODEL = 2048
HEAD_DIM = 128
HEADS_PER_GROUP = 4
ATTN_GROUPS = ((128, 1), (512, 4), (2048, 16))
N_HEADS = HEADS_PER_GROUP * len(ATTN_GROUPS)
GROUP_W = HEADS_PER_GROUP * HEAD_DIM
HQ = N_HEADS * HEAD_DIM
SSM_W = 1024
SSM_GROUP = 16
SSM_GROUPS = 64
SSM_STATE = 64
STATE_W = SSM_GROUPS * SSM_STATE
D_FF = 5632
EPS = 1e-6
N_DEV = 8
SEGS = 8
BD = 8

ADAM_LR, ADAM_B1, ADAM_B2, ADAM_EPS, ADAM_WD, ADAM_STEP = 0.001, 0.9, 0.999, 1e-08, 0.01, 10

VMEM_LIMIT = 56 * 1024 * 1024
HBM_SPEC = pl.BlockSpec(memory_space=pltpu.HBM)
MESH_ID = pl.DeviceIdType.MESH
NEG = -1e30


def _pcall(body, **kw):
    return pl.pallas_call(body, **kw)


def _cparams(sem=None):
    if sem is None:
        return pltpu.CompilerParams(vmem_limit_bytes=VMEM_LIMIT)
    return pltpu.CompilerParams(vmem_limit_bytes=VMEM_LIMIT, dimension_semantics=sem)


def _my_coords():
    return lax.axis_index("x"), lax.axis_index("y"), lax.axis_index("c")


class Gather:
    def __init__(self, xs):
        self.arrays = list(xs)
        self.out_shapes = [jax.ShapeDtypeStruct((N_DEV,) + x.shape, x.dtype) for x in xs]

    def _ctx(self, out_refs, send_sems, recv_sems):
        mx, my, mc = _my_coords()
        me, sibling = (mx, my, mc), (mx, my, 1 - mc)
        chips = [(1 - mx, my), (mx, 1 - my), (1 - mx, 1 - my)]

        def slot(a, px, py, pc):
            return out_refs[a].at[4 * px + 2 * py + pc]

        def copy(a, k, block, to, src=None):
            return pltpu.make_async_remote_copy(
                src_ref=slot(a, *block) if src is None else src, dst_ref=slot(a, *block),
                send_sem=send_sems.at[7 * a + k], recv_sem=recv_sems.at[7 * a + k], device_id=to, device_id_type=MESH_ID)

        return me, sibling, chips, mc, slot, copy

    def _first(self, a, x_refs, ctx):
        me, sibling, chips, mc, slot, copy = ctx
        return [copy(a, 0, me, sibling, src=x_refs[a])] + [copy(a, 1 + j, me, (*chip, mc), src=x_refs[a]) for j, chip in enumerate(chips)]

    def start(self, x_refs, out_refs, send_sems, recv_sems, local_sems):
        ctx = self._ctx(out_refs, send_sems, recv_sems)
        me, slot = ctx[0], ctx[4]
        for a in range(len(self.arrays)):
            pltpu.make_async_copy(x_refs[a], slot(a, *me), local_sems.at[a]).start()
            for cp in self._first(a, x_refs, ctx):
                cp.start()

    def finish(self, x_refs, out_refs, send_sems, recv_sems, local_sems):
        ctx = self._ctx(out_refs, send_sems, recv_sems)
        me, sibling, chips, mc, slot, copy = ctx
        na = len(self.arrays)
        passed = []
        for a in range(na):
            for j, chip in enumerate(chips):
                copy(a, 1 + j, (*chip, mc), me).wait_recv()
                fwd = copy(a, 4 + j, (*chip, mc), sibling)
                fwd.start()
                passed.append(fwd)
        for a in range(na):
            copy(a, 0, sibling, me).wait_recv()
            for j, chip in enumerate(chips):
                copy(a, 4 + j, (*chip, 1 - mc), me).wait_recv()
        for a in range(na):
            for cp in self._first(a, x_refs, ctx):
                cp.wait_send()
        for cp in passed:
            cp.wait_send()
        for a in range(na):
            pltpu.make_async_copy(x_refs[a], slot(a, *me), local_sems.at[a]).wait()


class AllToAll:
    def __init__(self, ps):
        self.arrays = list(ps)
        self.out_shapes = [jax.ShapeDtypeStruct(p.shape, p.dtype) for p in ps]

    def _copies(self, p_refs, out_refs, send_sems, recv_sems, local_sems):
        mx, my, mc = _my_coords()
        me = 4 * mx + 2 * my + mc
        local, remote = [], []
        for a in range(len(self.arrays)):
            local.append(pltpu.make_async_copy(p_refs[a].at[me], out_refs[a].at[me], local_sems.at[a]))
            for k in range(1, N_DEV):
                px, py, pc = mx ^ ((k >> 2) & 1), my ^ ((k >> 1) & 1), mc ^ (k & 1)
                remote.append(pltpu.make_async_remote_copy(
                    src_ref=p_refs[a].at[4 * px + 2 * py + pc], dst_ref=out_refs[a].at[me],
                    send_sem=send_sems.at[7 * a + k - 1], recv_sem=recv_sems.at[7 * a + k - 1],
                    device_id=(px, py, pc), device_id_type=MESH_ID))
        return local, remote

    def start(self, *refs):
        local, remote = self._copies(*refs)
        for cp in local + remote:
            cp.start()

    def finish(self, *refs):
        local, remote = self._copies(*refs)
        for cp in remote:
            cp.wait_recv()
        for cp in remote:
            cp.wait_send()
        for cp in local:
            cp.wait()


def _run(body, args, carry=None, **kw):
    if carry is None:
        return _pcall(body, **kw)(*args)
    grid = kw["grid"]
    single = not isinstance(kw["out_shape"], (list, tuple))
    in_specs = list(kw["in_specs"])
    out_specs = [kw["out_specs"]] if single else list(kw["out_specs"])
    out_shape = [kw["out_shape"]] if single else list(kw["out_shape"])
    scratch = list(kw.get("scratch_shapes", []))
    na, nin, nout, nscr = len(carry.arrays), len(in_specs), len(out_specs), len(scratch)

    def carried(*refs):
        ins, cin = refs[:nin], refs[nin:nin + na]
        outs, cout = refs[nin + na:nin + na + nout], refs[nin + na + nout:nin + 2 * na + nout]
        scr = refs[nin + 2 * na + nout:nin + 2 * na + nout + nscr]
        sems = refs[nin + 2 * na + nout + nscr:]
        ids = [pl.program_id(i) for i in range(len(grid))]
        first, last = ids[0] == 0, ids[0] == grid[0] - 1
        for i in range(1, len(grid)):
            first = jnp.logical_and(first, ids[i] == 0)
            last = jnp.logical_and(last, ids[i] == grid[i] - 1)

        @pl.when(first)
        def _():
            carry.start(cin, cout, *sems)

        body(*ins, *outs, *scr)

        @pl.when(last)
        def _():
            carry.finish(cin, cout, *sems)

    res = _pcall(
        carried, name=kw["name"], grid=grid, in_specs=in_specs + [HBM_SPEC] * na, out_specs=out_specs + [HBM_SPEC] * na,
        out_shape=out_shape + carry.out_shapes,
        scratch_shapes=scratch + [pltpu.SemaphoreType.DMA((7 * na,)), pltpu.SemaphoreType.DMA((7 * na,)), pltpu.SemaphoreType.DMA((na,))],
        compiler_params=_cparams(("arbitrary",) * len(grid)),
    )(*args, *carry.arrays)
    main = res[:nout]
    return (main[0] if single else main), list(res[nout:])


_DN = {"nn": (((1,), (0,)), ((), ())), "nt": (((1,), (1,)), ((), ())), "tn": (((0,), (0,)), ((), ()))}


LANE = 128
MM_TM, MM_TN, MM_TK = 1024, 1536, 2048


def _tile(n, cap):
    for t in range(min(cap, n) // LANE * LANE, 0, -LANE):
        if n % t == 0:
            return t
    raise ValueError(n)


DW_TM, DW_TN, DW_TK = 512, 512, 8192


def mm(pairs, mode, out_dtype, name, tm=None, tn=None, tk=None, carry=None, epilogue=None, extras=(), b_window=None):
    a0, b0 = pairs[0]
    if mode == "nn":
        (m, k), n = a0.shape, b0.shape[1]
    elif mode == "nt":
        (m, k), n = a0.shape, b0.shape[0]
    else:
        (k, m), n = a0.shape, b0.shape[1]
    if b_window is not None:
        assert mode in ("nn", "nt") and len(pairs) == 1
        if mode == "nt":
            n = b_window[0]
        else:
            assert k == b_window[0]
    caps = (DW_TM, DW_TN, DW_TK) if mode == "tn" else (MM_TM, MM_TN, MM_TK)
    tm, tn, tk = _tile(m, tm or caps[0]), _tile(n, tn or caps[1]), _tile(k, tk or caps[2])
    nk = k // tk
    npairs = len(pairs)
    nex = len(extras)
    fused = epilogue is not None
    assert not fused or nk == 1
    out_dtypes = list(out_dtype) if fused else [out_dtype]

    def body(*refs):
        prods = []
        for p in range(npairs):
            a = refs[2 * p][...].astype(BF16) if (p == 0 or pairs[p][0] is not pairs[p - 1][0]) else a
            b = refs[2 * p + 1][...].astype(BF16)
            prods.append(lax.dot_general(a, b, _DN[mode], preferred_element_type=F32))
        if fused:
            ex = [refs[2 * npairs + e][...].astype(F32) for e in range(nex)]
            for o_ref, val in zip(refs[2 * npairs + nex:], epilogue(prods, ex)):
                o_ref[...] = val.astype(o_ref.dtype)
            return
        o_ref = refs[2 * npairs]
        tot = prods[0]
        for d in prods[1:]:
            tot = tot + d
        if nk == 1:
            o_ref[...] = tot.astype(o_ref.dtype)
            return
        acc = refs[2 * npairs + 1]
        kk = pl.program_id(2)

        @pl.when(kk == 0)
        def _():
            acc[...] = tot

        @pl.when(kk > 0)
        def _():
            acc[...] += tot

        @pl.when(kk == nk - 1)
        def _():
            o_ref[...] = acc[...].astype(o_ref.dtype)

    rows_of = b_window[1] if b_window is not None else (lambda t: t)
    if mode == "nn":
        sp = [pl.BlockSpec((tm, tk), lambda i, j, kk: (i, kk)), pl.BlockSpec((tk, tn), lambda i, j, kk: (rows_of(kk), j))]
    elif mode == "nt":
        sp = [pl.BlockSpec((tm, tk), lambda i, j, kk: (i, kk)), pl.BlockSpec((tn, tk), lambda i, j, kk: (rows_of(j), kk))]
    else:
        sp = [pl.BlockSpec((tk, tm), lambda i, j, kk: (kk, i)), pl.BlockSpec((tk, tn), lambda i, j, kk: (kk, j))]
    o_spec = pl.BlockSpec((tm, tn), lambda i, j, kk: (i, j))
    out_shapes = [jax.ShapeDtypeStruct((m, n), dt) for dt in out_dtypes]
    return _run(
        body, [t for pr in pairs for t in pr] + list(extras), carry=carry, name=name, grid=(m // tm, n // tn, nk),
        in_specs=sp * npairs + [o_spec] * nex,
        out_specs=[o_spec] * len(out_shapes) if fused else o_spec,
        out_shape=out_shapes if fused else out_shapes[0],
        scratch_shapes=[pltpu.VMEM((tm, tn), F32)] if nk > 1 else [],
        compiler_params=_cparams(("parallel", "parallel", "arbitrary")),
    )


def rowwise(name, fn, row_ins, const_ins, row_outs, acc_outs=(), ts=None, carry=None):
    s = row_ins[0].shape[0]
    row_outs = [ro if len(ro) == 3 else (*ro, 1) for ro in row_outs]
    if ts is None:
        per_row = sum(a.shape[-1] * a.dtype.itemsize for a in row_ins) + sum(w * jnp.dtype(dt).itemsize for w, dt, _ in row_outs)
        ts = 512
        while ts > 8 and 2 * ts * per_row > 20 * 1024 * 1024:
            ts //= 2
    ts = min(ts, s)
    assert s % ts == 0
    nr, nc, no, na = len(row_ins), len(const_ins), len(row_outs), len(acc_outs)

    def body(*refs):
        rows = [r[...].reshape(ts, r.shape[-1]).astype(F32) for r in refs[:nr]]
        consts = [r[...] for r in refs[nr:nr + nc]]
        outs, accs = fn(rows, consts)
        for r, v in zip(refs[nr + nc:nr + nc + no], outs):
            r[...] = v.astype(r.dtype).reshape(r.shape)
        if na:
            first = pl.program_id(0) == 0
            for r, v in zip(refs[nr + nc + no:], accs):
                @pl.when(first)
                def _(r=r, v=v):
                    r[...] = v

                @pl.when(jnp.logical_not(first))
                def _(r=r, v=v):
                    r[...] += v

    def tile_spec(w, d):
        if d == 1:
            return pl.BlockSpec((ts, w), lambda i: (i, 0))
        return pl.BlockSpec((d, ts // d, w), lambda i: (0, i, 0))

    in_specs = [tile_spec(a.shape[-1], a.shape[0] if a.ndim == 3 else 1) for a in row_ins]
    in_specs += [pl.BlockSpec(c.shape, lambda i, nd=c.ndim: (0,) * nd) for c in const_ins]
    out_specs = [tile_spec(w, d) for w, _, d in row_outs]
    out_specs += [pl.BlockSpec(shp, lambda i, nd=len(shp): (0,) * nd) for shp in acc_outs]
    out_shape = [jax.ShapeDtypeStruct((s, w) if d == 1 else (d, s // d, w), dt) for w, dt, d in row_outs]
    out_shape += [jax.ShapeDtypeStruct(shp, F32) for shp in acc_outs]
    return _run(
        body, [*row_ins, *const_ins], carry=carry, name=name, grid=(s // ts,), in_specs=in_specs, out_specs=out_specs,
        out_shape=out_shape, compiler_params=_cparams(("arbitrary",)),
    )


PERM_TS = 256


def _perm_matrix(ts, d, inverse):
    i = lax.broadcasted_iota(jnp.int32, (ts, ts), 0)
    k = lax.broadcasted_iota(jnp.int32, (ts, ts), 1)
    per = ts // d
    src = (i % d) * per + i // d if inverse else (i % per) * d + i // per
    return jnp.where(k == src, 1.0, 0.0).astype(BF16)


def _permute(p, x):
    if x.dtype == BF16:
        return jnp.dot(p, x, preferred_element_type=F32)
    hi = x.astype(BF16)
    rest = x - hi.astype(F32)
    mid = rest.astype(BF16)
    lo = (rest - mid.astype(F32)).astype(BF16)
    out = jnp.dot(p, hi, preferred_element_type=F32) + jnp.dot(p, mid, preferred_element_type=F32)
    return out + jnp.dot(p, lo, preferred_element_type=F32)


def _rms(x, gain):
    r = lax.rsqrt(jnp.mean(x * x, axis=-1, keepdims=True) + EPS)
    n = x * r
    return n * gain, n, r


def _rms_bwd(dy, n, r, gain):
    dn = dy * gain
    dx = r * (dn - n * jnp.mean(dn * n, axis=-1, keepdims=True))
    return dx, jnp.sum(dy * n, axis=0, keepdims=True)


def _sigmoid(x):
    return 1.0 / (1.0 + jnp.exp(-x))


_GELU_K = math.sqrt(2.0 / math.pi)


def _gelu(x):
    t = jnp.tanh(_GELU_K * (x + 0.044715 * x * x * x))
    return 0.5 * x * (1.0 + t), t


def _gelu_grad(x, t):
    return 0.5 * (1.0 + t) + 0.5 * x * (1.0 - t * t) * _GELU_K * (1.0 + 3.0 * 0.044715 * x * x)


def _head_sum(x):
    parts = []
    for h in range(HEADS_PER_GROUP):
        sl = x[:, h * HEAD_DIM:(h + 1) * HEAD_DIM]
        parts.append(jnp.broadcast_to(jnp.sum(sl, axis=-1, keepdims=True), sl.shape))
    return jnp.concatenate(parts, axis=-1)


def _mix_weights(l0, l1, l2):
    mx = jnp.maximum(jnp.maximum(l0, l1), l2)
    e0, e1, e2 = jnp.exp(l0 - mx), jnp.exp(l1 - mx), jnp.exp(l2 - mx)
    inv = 1.0 / (e0 + e1 + e2)
    return e0 * inv, e1 * inv, e2 * inv


BLK = 128


def _slopes(g):
    return [2.0 ** (-8.0 * (g * HEADS_PER_GROUP + h + 1) / N_HEADS) for h in range(HEADS_PER_GROUP)]


def _attn_masks(dil):
    qi = lax.broadcasted_iota(jnp.int32, (BLK, BLK), 0)
    ki = lax.broadcasted_iota(jnp.int32, (BLK, BLK), 1)
    dist_c = qi - ki
    dist_p = BLK + qi - ki
    return dist_c >= 0, dist_p <= BLK, (dist_c * dil).astype(F32), (dist_p * dil).astype(F32)


def attn_fwd(qkv, g, name):
    dil, length, _ = qkv.shape
    scale = HEAD_DIM ** -0.5
    slopes = _slopes(g)

    def body(q_ref, kc_ref, vc_ref, kp_ref, vp_ref, o_ref, l_ref):
        n = pl.program_id(1)
        ok_c, ok_p, dc, dp = _attn_masks(dil)
        ok_p = jnp.logical_and(ok_p, n > 0)
        for h in range(HEADS_PER_GROUP):
            sl = slice(h * HEAD_DIM, (h + 1) * HEAD_DIM)
            q = q_ref[:, sl]
            s_c = lax.dot_general(q, kc_ref[:, sl], _DN["nt"], preferred_element_type=F32) * scale - slopes[h] * dc
            s_p = lax.dot_general(q, kp_ref[:, sl], _DN["nt"], preferred_element_type=F32) * scale - slopes[h] * dp
            s_c = jnp.where(ok_c, s_c, NEG)
            s_p = jnp.where(ok_p, s_p, NEG)
            mx = jnp.maximum(jnp.max(s_c, axis=-1, keepdims=True), jnp.max(s_p, axis=-1, keepdims=True))
            p_c = jnp.exp(s_c - mx)
            p_p = jnp.exp(s_p - mx)
            den = jnp.sum(p_c, axis=-1, keepdims=True) + jnp.sum(p_p, axis=-1, keepdims=True)
            acc = jnp.dot(p_c.astype(BF16), vc_ref[:, sl], preferred_element_type=F32)
            acc += jnp.dot(p_p.astype(BF16), vp_ref[:, sl], preferred_element_type=F32)
            o_ref[:, sl] = acc / den
            l_ref[:, sl] = jnp.broadcast_to(mx + jnp.log(den), (BLK, HEAD_DIM))

    def spec(col, prev):
        if prev:
            return pl.BlockSpec((None, BLK, GROUP_W), lambda r, n: (r, jnp.maximum(n - 1, 0), col))
        return pl.BlockSpec((None, BLK, GROUP_W), lambda r, n: (r, n, col))

    out_spec = pl.BlockSpec((None, BLK, GROUP_W), lambda r, n: (r, n, 0))
    return _pcall(
        body, name=name, grid=(dil, length // BLK),
        in_specs=[spec(0, False), spec(1, False), spec(2, False), spec(1, True), spec(2, True)],
        out_specs=[out_spec, out_spec],
        out_shape=[jax.ShapeDtypeStruct((dil, length, GROUP_W), F32)] * 2,
        compiler_params=_cparams(("parallel", "parallel")),
    )(qkv, qkv, qkv, qkv, qkv)


def attn_bwd(qkv, dout, lse, dd, g, name, carry=None):
    dil, length, _ = qkv.shape
    nblk = length // BLK
    scale = HEAD_DIM ** -0.5
    slopes = _slopes(g)

    def body(q_ref, kc_ref, vc_ref, kp_ref, vp_ref, qn_ref, do_ref, don_ref, l_ref, ln_ref, d_ref, dn_ref, o_ref):
        n = pl.program_id(1)
        ok_c, ok_p, dc, dp = _attn_masks(dil)
        ok_prev = jnp.logical_and(ok_p, n > 0)
        ok_next = jnp.logical_and(ok_p, n < nblk - 1)
        for h in range(HEADS_PER_GROUP):
            sl = slice(h * HEAD_DIM, (h + 1) * HEAD_DIM)
            q, kc, vc, kp, vp, qn = q_ref[:, sl], kc_ref[:, sl], vc_ref[:, sl], kp_ref[:, sl], vp_ref[:, sl], qn_ref[:, sl]
            do, don = do_ref[:, sl], don_ref[:, sl]
            lse_q, lse_n, dd_q, dd_n = l_ref[:, sl], ln_ref[:, sl], d_ref[:, sl], dn_ref[:, sl]

            def probs(qq, kk, dist, ok, lse_t):
                s = lax.dot_general(qq, kk, _DN["nt"], preferred_element_type=F32) * scale - slopes[h] * dist
                return jnp.where(ok, jnp.exp(jnp.where(ok, s, NEG) - lse_t), 0.0)

            p_c = probs(q, kc, dc, ok_c, lse_q)
            p_p = probs(q, kp, dp, ok_prev, lse_q)
            p_x = probs(qn, kc, dp, ok_next, lse_n)
            ds_c = p_c * (lax.dot_general(do, vc, _DN["nt"], preferred_element_type=F32) - dd_q)
            ds_p = p_p * (lax.dot_general(do, vp, _DN["nt"], preferred_element_type=F32) - dd_q)
            ds_x = p_x * (lax.dot_general(don, vc, _DN["nt"], preferred_element_type=F32) - dd_n)
            ds_c16, ds_p16, ds_x16 = ds_c.astype(BF16), ds_p.astype(BF16), ds_x.astype(BF16)
            dq = jnp.dot(ds_c16, kc, preferred_element_type=F32) + jnp.dot(ds_p16, kp, preferred_element_type=F32)
            dk = lax.dot_general(ds_c16, q, _DN["tn"], preferred_element_type=F32)
            dk += lax.dot_general(ds_x16, qn, _DN["tn"], preferred_element_type=F32)
            dv = lax.dot_general(p_c.astype(BF16), do, _DN["tn"], preferred_element_type=F32)
            dv += lax.dot_general(p_x.astype(BF16), don, _DN["tn"], preferred_element_type=F32)
            o_ref[:, h * HEAD_DIM:(h + 1) * HEAD_DIM] = (dq * scale).astype(BF16)
            o_ref[:, GROUP_W + h * HEAD_DIM:GROUP_W + (h + 1) * HEAD_DIM] = (dk * scale).astype(BF16)
            o_ref[:, 2 * GROUP_W + h * HEAD_DIM:2 * GROUP_W + (h + 1) * HEAD_DIM] = dv.astype(BF16)

    def spec(col, which):
        if which == "prev":
            return pl.BlockSpec((None, BLK, GROUP_W), lambda r, n: (r, jnp.maximum(n - 1, 0), col))
        if which == "next":
            return pl.BlockSpec((None, BLK, GROUP_W), lambda r, n: (r, jnp.minimum(n + 1, nblk - 1), col))
        return pl.BlockSpec((None, BLK, GROUP_W), lambda r, n: (r, n, col))

    return _run(
        body, [qkv, qkv, qkv, qkv, qkv, qkv, dout, dout, lse, lse, dd, dd], carry=carry, name=name, grid=(dil, nblk),
        in_specs=[spec(0, "cur"), spec(1, "cur"), spec(2, "cur"), spec(1, "prev"), spec(2, "prev"), spec(0, "next"),
                  spec(0, "cur"), spec(0, "next"), spec(0, "cur"), spec(0, "next"), spec(0, "cur"), spec(0, "next")],
        out_specs=pl.BlockSpec((None, BLK, 3 * GROUP_W), lambda r, n: (r, n, 0)),
        out_shape=jax.ShapeDtypeStruct((dil, length, 3 * GROUP_W), BF16),
        compiler_params=_cparams(("parallel", "parallel")),
    )


def _ssm_prep_values(are, aim, logdt):
    dt = jnp.exp(logdt)
    mag = jnp.exp(are * dt)
    lb_re, lb_im = mag * jnp.cos(aim * dt), mag * jnp.sin(aim * dt)
    inv = 1.0 / (are * are + aim * aim)
    n_re, n_im = lb_re - 1.0, lb_im
    f_re = (n_re * are + n_im * aim) * inv
    f_im = (n_im * are - n_re * aim) * inv
    return dt, lb_re, lb_im, f_re, f_im, inv


PREP_G = 8


def _group_specs(are, logdt, bre):
    def spec(a):
        return pl.BlockSpec((PREP_G,) + a.shape[1:], lambda i: (i, 0, 0))
    return spec(are), spec(logdt), spec(bre)


def ssm_prep(are, aim, logdt, bre, bim):
    def body(are_r, aim_r, ldt_r, bre_r, bim_r, lre_o, lim_o, bbre_o, bbim_o):
        _, lb_re, lb_im, f_re, f_im, _ = _ssm_prep_values(are_r[...], aim_r[...], ldt_r[...])
        lre_o[...] = lb_re
        lim_o[...] = lb_im
        bbre_o[...] = f_re * bre_r[...] - f_im * bim_r[...]
        bbim_o[...] = f_re * bim_r[...] + f_im * bre_r[...]

    sh1 = jax.ShapeDtypeStruct(are.shape, F32)
    shb = jax.ShapeDtypeStruct(bre.shape, F32)
    s1, sd, sb = _group_specs(are, logdt, bre)
    return _pcall(body, name="ssm_prep", grid=(SSM_GROUPS // PREP_G,), in_specs=[s1, s1, sd, sb, sb], out_specs=[s1, s1, sb, sb],
                  out_shape=[sh1, sh1, shb, shb], compiler_params=_cparams(("parallel",)))(are, aim, logdt, bre, bim)


def ssm_prep_bwd(are, aim, logdt, bre, bim, dbbre, dbbim, dlre, dlim):
    def body(are_r, aim_r, ldt_r, bre_r, bim_r, dbbre_r, dbbim_r, dlre_r, dlim_r, dare_o, daim_o, dldt_o, dbre_o, dbim_o):
        are_v, aim_v = are_r[...], aim_r[...]
        dt, lb_re, lb_im, f_re, f_im, inv = _ssm_prep_values(are_v, aim_v, ldt_r[...])
        b_re, b_im, g_re, g_im = bre_r[...], bim_r[...], dbbre_r[...], dbbim_r[...]
        dbre_o[...] = f_re * g_re + f_im * g_im
        dbim_o[...] = f_re * g_im - f_im * g_re
        df_re = jnp.sum(b_re * g_re + b_im * g_im, axis=-1, keepdims=True)
        df_im = jnp.sum(b_re * g_im - b_im * g_re, axis=-1, keepdims=True)
        il_re, il_im = are_v * inv, -aim_v * inv
        cl_re = dlre_r[...] + il_re * df_re + il_im * df_im
        cl_im = dlim_r[...] + il_re * df_im - il_im * df_re
        q_re = -(f_re * il_re - f_im * il_im)
        q_im = -(f_re * il_im + f_im * il_re)
        ca_re = q_re * df_re + q_im * df_im
        ca_im = q_re * df_im - q_im * df_re
        cz_re = lb_re * cl_re + lb_im * cl_im
        cz_im = lb_re * cl_im - lb_im * cl_re
        dare_o[...] = ca_re + dt * cz_re
        daim_o[...] = ca_im + dt * cz_im
        dldt_o[...] = dt * jnp.sum(are_v * cz_re + aim_v * cz_im, axis=1, keepdims=True)

    sh1 = jax.ShapeDtypeStruct(are.shape, F32)
    shb = jax.ShapeDtypeStruct(bre.shape, F32)
    s1, sd, sb = _group_specs(are, logdt, bre)
    return _pcall(
        body, name="ssm_prep_bwd", grid=(SSM_GROUPS // PREP_G,), in_specs=[s1, s1, sd, sb, sb, sb, sb, s1, s1],
        out_specs=[s1, s1, sd, sb, sb], out_shape=[sh1, sh1, jax.ShapeDtypeStruct(logdt.shape, F32), shb, shb],
        compiler_params=_cparams(("parallel",)),
    )(are, aim, logdt, bre, bim, dbbre, dbbim, dlre, dlim)


SCAN_WC = 512


def _chain_segments(a_re, a_im, e_re, e_im, nsq, reverse):
    p_re, p_im = a_re, a_im
    for _ in range(nsq):
        p_re, p_im = p_re * p_re - p_im * p_im, 2.0 * p_re * p_im
    row = lax.broadcasted_iota(jnp.int32, e_re.shape, 0)
    edge = (row == SEGS - 1) if reverse else (row == 0)
    shift = SEGS - 1 if reverse else 1
    c_re, c_im = jnp.zeros_like(e_re), jnp.zeros_like(e_im)
    for _ in range(SEGS - 1):
        n_re = p_re * c_re - p_im * c_im + e_re
        n_im = p_re * c_im + p_im * c_re + e_im
        c_re = jnp.where(edge, 0.0, pltpu.roll(n_re, shift, 0))
        c_im = jnp.where(edge, 0.0, pltpu.roll(n_im, shift, 0))
    return c_re, c_im


def _scan_dims(s):
    steps = s // SEGS
    assert steps & (steps - 1) == 0
    tt = min(128, steps)
    return steps, tt, steps // tt, tt * SEGS, int(math.log2(steps))


U_BLK = SSM_W // BD


def ssm_fwd(u_s, dvec, w_bre, w_bim, w_cre, w_cim_neg, lre, lim, name, carry=None):
    s = u_s.shape[0]
    steps, tt, nch, rows, nsq = _scan_dims(s)

    def body(u_r, d_r, bre_r, bim_r, cre_r, cim_r, lre_r, lim_r, yg_o, ys_o, hre_o, him_o, hin_re_o, hin_im_o,
             st_re, st_im, x_re, x_im, h_re, h_im):
        ps, ch = pl.program_id(1), pl.program_id(2)
        a_re = jnp.broadcast_to(lre_r[...], (SEGS, SCAN_WC))
        a_im = jnp.broadcast_to(lim_r[...], (SEGS, SCAN_WC))
        ub = u_r[...]
        ub16 = ub.astype(BF16)
        x_re[...] = jnp.dot(ub16, bre_r[...], preferred_element_type=F32)
        x_im[...] = jnp.dot(ub16, bim_r[...], preferred_element_type=F32)

        @pl.when(jnp.logical_and(ps == 0, ch == 0))
        def _():
            st_re[...] = jnp.zeros_like(st_re)
            st_im[...] = jnp.zeros_like(st_im)

        @pl.when(jnp.logical_and(ps == 1, ch == 0))
        def _():
            c_re, c_im = _chain_segments(a_re, a_im, st_re[...], st_im[...], nsq, False)
            st_re[...] = c_re
            st_im[...] = c_im
            hin_re_o[...] = c_re
            hin_im_o[...] = c_im

        def run(store):
            def step(t, hc):
                off = pl.multiple_of(t * SEGS, SEGS)
                n_re = a_re * hc[0] - a_im * hc[1] + x_re[pl.ds(off, SEGS), :]
                n_im = a_re * hc[1] + a_im * hc[0] + x_im[pl.ds(off, SEGS), :]
                if store:
                    h_re[pl.ds(off, SEGS), :] = n_re
                    h_im[pl.ds(off, SEGS), :] = n_im
                return n_re, n_im

            fin = lax.fori_loop(0, tt, step, (st_re[...], st_im[...]))
            st_re[...] = fin[0]
            st_im[...] = fin[1]

        @pl.when(ps == 0)
        def _():
            run(False)

        @pl.when(ps == 1)
        def _():
            run(True)
            hr16, hi16 = h_re[...].astype(BF16), h_im[...].astype(BF16)
            hre_o[...] = hr16
            him_o[...] = hi16
            y = jnp.dot(hr16, cre_r[...], preferred_element_type=F32) + jnp.dot(hi16, cim_r[...], preferred_element_type=F32)
            y = y + d_r[...] * ub
            ys_o[...] = y
            yg_o[...] = _gelu(y)[0].astype(BF16)

    def pass1(ps, c):
        return jnp.where(ps == 1, c, 0)

    u_spec = pl.BlockSpec((rows, U_BLK), lambda j, ps, c: (c, j))
    d_spec = pl.BlockSpec((1, U_BLK), lambda j, ps, c: (0, j))
    b_spec = pl.BlockSpec((None, U_BLK, SCAN_WC), lambda j, ps, c: (j, 0, 0))
    c_spec = pl.BlockSpec((None, SCAN_WC, U_BLK), lambda j, ps, c: (j, 0, 0))
    l_spec = pl.BlockSpec((1, SCAN_WC), lambda j, ps, c: (0, j))
    y_spec = pl.BlockSpec((rows, U_BLK), lambda j, ps, c: (pass1(ps, c), j))
    h_spec = pl.BlockSpec((rows, SCAN_WC), lambda j, ps, c: (pass1(ps, c), j))
    e_spec = pl.BlockSpec((SEGS, SCAN_WC), lambda j, ps, c: (0, j))
    return _run(
        body, [u_s, dvec, w_bre, w_bim, w_cre, w_cim_neg, lre, lim], carry=carry, name=name, grid=(BD, 2, nch),
        in_specs=[u_spec, d_spec, b_spec, b_spec, c_spec, c_spec, l_spec, l_spec],
        out_specs=[y_spec, y_spec, h_spec, h_spec, e_spec, e_spec],
        out_shape=[jax.ShapeDtypeStruct((s, SSM_W), BF16), jax.ShapeDtypeStruct((s, SSM_W), F32),
                   jax.ShapeDtypeStruct((s, STATE_W), BF16), jax.ShapeDtypeStruct((s, STATE_W), BF16),
                   jax.ShapeDtypeStruct((SEGS, STATE_W), F32), jax.ShapeDtypeStruct((SEGS, STATE_W), F32)],
        scratch_shapes=[pltpu.VMEM((SEGS, SCAN_WC), F32)] * 2 + [pltpu.VMEM((rows, SCAN_WC), F32)] * 4,
        compiler_params=_cparams(("parallel", "arbitrary", "arbitrary")),
    )


def ssm_bwd(dyg_s, ys, u_s, h_re, h_im, hin_re, hin_im, dvec, w_bre_t, w_bim_t, w_cre_t, w_cim_neg_t, lre, lim, name, carry=None):
    s = u_s.shape[0]
    steps, tt, nch, rows, nsq = _scan_dims(s)
    half = 2 * SEGS

    def body(dyg_r, ys_r, u_r, hre_r, him_r, pre_r, pim_r, cin_re_r, cin_im_r, d_r, bre_r, bim_r, cre_r, cim_r, lre_r, lim_r,
             du_o, dbre_o, dbim_o, dcre_o, dcim_o, dlre_o, dlim_o, dd_o, st_re, st_im, x_re, x_im, g_re, g_im, hf_re, hf_im):
        ps, ch = pl.program_id(1), pl.program_id(2)
        a_re = jnp.broadcast_to(lre_r[...], (SEGS, SCAN_WC))
        a_im = -jnp.broadcast_to(lim_r[...], (SEGS, SCAN_WC))
        ub, y = u_r[...], ys_r[...]
        dy = dyg_r[...] * _gelu_grad(y, _gelu(y)[1])
        dy16 = dy.astype(BF16)
        x_re[...] = jnp.dot(dy16, cre_r[...], preferred_element_type=F32)
        x_im[...] = jnp.dot(dy16, cim_r[...], preferred_element_type=F32)

        @pl.when(jnp.logical_and(ps == 0, ch == 0))
        def _():
            st_re[...] = jnp.zeros_like(st_re)
            st_im[...] = jnp.zeros_like(st_im)

        @pl.when(jnp.logical_and(ps == 1, ch == 0))
        def _():
            c_re, c_im = _chain_segments(a_re, a_im, st_re[...], st_im[...], nsq, True)
            st_re[...] = c_re
            st_im[...] = c_im
            dlre_o[...] = jnp.zeros_like(dlre_o)
            dlim_o[...] = jnp.zeros_like(dlim_o)

        @pl.when(ps == 0)
        def _():
            def step(i, hc):
                off = pl.multiple_of((tt - 1 - i) * SEGS, SEGS)
                return (a_re * hc[0] - a_im * hc[1] + x_re[pl.ds(off, SEGS), :],
                        a_re * hc[1] + a_im * hc[0] + x_im[pl.ds(off, SEGS), :])

            fin = lax.fori_loop(0, tt, step, (st_re[...], st_im[...]))
            st_re[...] = fin[0]
            st_im[...] = fin[1]

        @pl.when(ps == 1)
        def _():
            hf_re[...] = hre_r[...].astype(F32)
            hf_im[...] = him_r[...].astype(F32)
            first_chunk = ch == nch - 1
            edge_re = jnp.where(first_chunk, cin_re_r[...], pre_r[...].astype(F32)[SEGS:, :])
            edge_im = jnp.where(first_chunk, cin_im_r[...], pim_r[...].astype(F32)[SEGS:, :])

            def step(i, hc):
                t = tt - 1 - i
                off = pl.multiple_of(t * SEGS, SEGS)
                n_re = a_re * hc[0] - a_im * hc[1] + x_re[pl.ds(off, SEGS), :]
                n_im = a_re * hc[1] + a_im * hc[0] + x_im[pl.ds(off, SEGS), :]
                g_re[pl.ds(off, SEGS), :] = n_re
                g_im[pl.ds(off, SEGS), :] = n_im
                offp = pl.multiple_of(jnp.maximum(t - 1, 0) * SEGS, SEGS)
                hp_re = jnp.where(t == 0, edge_re, hf_re[pl.ds(offp, SEGS), :])
                hp_im = jnp.where(t == 0, edge_im, hf_im[pl.ds(offp, SEGS), :])
                return n_re, n_im, hc[2] + hp_re * n_re + hp_im * n_im, hc[3] + hp_re * n_im - hp_im * n_re

            fin = lax.fori_loop(0, tt, step, (st_re[...], st_im[...], dlre_o[...], dlim_o[...]))
            st_re[...] = fin[0]
            st_im[...] = fin[1]
            dlre_o[...] = fin[2]
            dlim_o[...] = fin[3]

            gr16, gi16 = g_re[...].astype(BF16), g_im[...].astype(BF16)
            du = jnp.dot(gr16, bre_r[...], preferred_element_type=F32) + jnp.dot(gi16, bim_r[...], preferred_element_type=F32)
            du_o[...] = du + d_r[...] * dy
            ub16 = ub.astype(BF16)
            parts = [
                (dbre_o, lax.dot_general(ub16, gr16, _DN["tn"], preferred_element_type=F32)),
                (dbim_o, lax.dot_general(ub16, gi16, _DN["tn"], preferred_element_type=F32)),
                (dcre_o, lax.dot_general(hre_r[...], dy16, _DN["tn"], preferred_element_type=F32)),
                (dcim_o, lax.dot_general(him_r[...], dy16, _DN["tn"], preferred_element_type=F32)),
                (dd_o, jnp.sum(dy * ub, axis=0, keepdims=True)),
            ]
            for ref, val in parts:
                @pl.when(ch == 0)
                def _(ref=ref, val=val):
                    ref[...] = val

                @pl.when(ch > 0)
                def _(ref=ref, val=val):
                    ref[...] += val

    def chunk(c):
        return nch - 1 - c

    def pass1(ps, c):
        return jnp.where(ps == 1, chunk(c), chunk(0))

    u_spec = pl.BlockSpec((rows, U_BLK), lambda j, ps, c: (chunk(c), j))
    h_spec = pl.BlockSpec((rows, SCAN_WC), lambda j, ps, c: (pass1(ps, c), j))
    prev_spec = pl.BlockSpec((half, SCAN_WC), lambda j, ps, c: (jnp.maximum(pass1(ps, c) * (rows // half) - 1, 0), j))
    e_spec = pl.BlockSpec((SEGS, SCAN_WC), lambda j, ps, c: (0, j))
    d_spec = pl.BlockSpec((1, U_BLK), lambda j, ps, c: (0, j))
    bt_spec = pl.BlockSpec((None, SCAN_WC, U_BLK), lambda j, ps, c: (j, 0, 0))
    ct_spec = pl.BlockSpec((None, U_BLK, SCAN_WC), lambda j, ps, c: (j, 0, 0))
    l_spec = pl.BlockSpec((1, SCAN_WC), lambda j, ps, c: (0, j))
    du_spec = pl.BlockSpec((rows, U_BLK), lambda j, ps, c: (pass1(ps, c), j))
    return _run(
        body, [dyg_s, ys, u_s, h_re, h_im, h_re, h_im, hin_re, hin_im, dvec, w_bre_t, w_bim_t, w_cre_t, w_cim_neg_t, lre, lim],
        carry=carry, name=name, grid=(BD, 2, nch),
        in_specs=[u_spec, u_spec, u_spec, h_spec, h_spec, prev_spec, prev_spec, e_spec, e_spec, d_spec, bt_spec, bt_spec,
                  ct_spec, ct_spec, l_spec, l_spec],
        out_specs=[du_spec, ct_spec, ct_spec, bt_spec, bt_spec, e_spec, e_spec, d_spec],
        out_shape=[jax.ShapeDtypeStruct((s, SSM_W), F32)] + [jax.ShapeDtypeStruct((BD, U_BLK, SCAN_WC), F32)] * 2
        + [jax.ShapeDtypeStruct((BD, SCAN_WC, U_BLK), F32)] * 2 + [jax.ShapeDtypeStruct((SEGS, STATE_W), F32)] * 2
        + [jax.ShapeDtypeStruct((1, SSM_W), F32)],
        scratch_shapes=[pltpu.VMEM((SEGS, SCAN_WC), F32)] * 2 + [pltpu.VMEM((rows, SCAN_WC), F32)] * 6,
        compiler_params=_cparams(("parallel", "arbitrary", "arbitrary")),
    )


def _block_diag(m):
    g, r, c = m.shape
    m = m.reshape(BD, g // BD, r, c)
    eye = jnp.eye(g // BD, dtype=m.dtype)
    return jnp.einsum("jarc,ab->jarbc", m, eye).reshape(BD, (g // BD) * r, (g // BD) * c)


def _block_diag_extract(m, r, c):
    per = m.shape[1] // r
    m = m.reshape(BD, per, r, per, c)
    return jnp.einsum("jarac->jarc", m).reshape(BD * per, r, c)


def to_segments(a):
    s, w = a.shape
    return a.reshape(SEGS, s // SEGS, w).transpose(1, 0, 2).reshape(s, w)


def from_segments(a):
    s, w = a.shape
    return a.reshape(s // SEGS, SEGS, w).transpose(1, 0, 2).reshape(s, w)


W_IN_CHUNKS = 4
FFN_TN = 512


def local_step(x, target, shards, small):
    s = x.shape[0]
    g1, g2, g3, g4 = (small[k].reshape(1, D_MODEL) for k in ("norm_mix_pre", "norm_mix_post", "norm_ffn_pre", "norm_ffn_post"))
    dvec = small["ssm_d"].reshape(1, SSM_W)
    wts, recv = {}, {}

    def gathered(names, blocks):
        for n, b in zip(names, blocks):
            wts[n] = _full_from_gathered(b, n)

    def rms_in_fn(r, c):
        hh = _rms(r[0], c[0])[0].astype(BF16)
        return [hh, _permute(_perm_matrix(PERM_TS, 4, False), hh), _permute(_perm_matrix(PERM_TS, 16, False), hh)], []

    (h, h4, h16), got = rowwise("rms_in", rms_in_fn, [x], [g1], [(D_MODEL, BF16), (D_MODEL, BF16, 4), (D_MODEL, BF16, 16)],
                                ts=PERM_TS, carry=Gather([shards["w_in"]]))
    w_in_t = _full_from_gathered(got[0], "w_in")
    w_u_t, w_gates_t = w_in_t[3 * HQ:3 * HQ + SSM_W], w_in_t[3 * HQ + SSM_W:]

    def qkv_rows(g):
        return 3 * GROUP_W, lambda t: 3 * t + g

    hd = [h.reshape(1, s, D_MODEL), h4, h16]
    qkv = [None] * 3
    names = ("w_attn_up", "w_glu_v", "w_glu_g")
    qkv[0], got = mm([(hd[0].reshape(s, D_MODEL), w_in_t)], "nt", BF16, "mm_qkv0", tn=GROUP_W, b_window=qkv_rows(0),
                     carry=Gather([shards[n] for n in names]))
    gathered(names, got)
    qkv[1], got = mm([(hd[1].reshape(s, D_MODEL), w_in_t)], "nt", BF16, "mm_qkv1", tn=GROUP_W, b_window=qkv_rows(1),
                     carry=Gather([shards["w_out"]]))
    gathered(("w_out",), got)
    qkv[2] = mm([(hd[2].reshape(s, D_MODEL), w_in_t)], "nt", BF16, "mm_qkv2", tn=GROUP_W, b_window=qkv_rows(2))
    u = mm([(h, w_u_t)], "nt", F32, "mm_u")
    gates, got = mm([(h, w_gates_t)], "nt", BF16, "mm_gates", carry=Gather([shards["w_ffn_gate"]]))
    gathered(("w_ffn_gate",), got)

    outs, lses = [], []
    for g, (_, dil) in enumerate(ATTN_GROUPS):
        o, l = attn_fwd(qkv[g].reshape(dil, s // dil, 3 * GROUP_W), g, f"attn_fwd{g}")
        outs.append(o.reshape(s, GROUP_W) if dil == 1 else o)
        lses.append(l.reshape(s, GROUP_W) if dil == 1 else l)

    def natural(r):
        back4, back16 = _perm_matrix(PERM_TS, 4, True), _perm_matrix(PERM_TS, 16, True)
        return r[0], _permute(back4, r[1]), _permute(back16, r[2]), r[3], _permute(back4, r[4]), _permute(back16, r[5])

    def merge_fn(r, c):
        o0, o1, o2, l0, l1, l2 = natural(r)
        w0, w1, w2 = _mix_weights(l0, l1, l2)
        return [w0 * o0 + w1 * o1 + w2 * o2], []

    (attn,) = rowwise("attn_merge", merge_fn, outs + lses, [], [(GROUP_W, BF16)], ts=PERM_TS)
    attn_branch = mm([(attn, wts["w_attn_up"])], "nn", BF16, "mm_up")

    are3 = small["ssm_a_re"].reshape(SSM_GROUPS, SSM_STATE, 1)
    aim3 = small["ssm_a_im"].reshape(SSM_GROUPS, SSM_STATE, 1)
    ldt3 = small["ssm_log_dt"].reshape(SSM_GROUPS, 1, 1)
    bre3 = small["ssm_b_re"].reshape(SSM_GROUPS, SSM_STATE, SSM_GROUP)
    bim3 = small["ssm_b_im"].reshape(SSM_GROUPS, SSM_STATE, SSM_GROUP)
    cre3 = small["ssm_c_re"].reshape(SSM_GROUPS, SSM_GROUP, SSM_STATE)
    cim3 = small["ssm_c_im"].reshape(SSM_GROUPS, SSM_GROUP, SSM_STATE)
    lre3, lim3, bbre, bbim = ssm_prep(are3, aim3, ldt3, bre3, bim3)
    lre, lim = lre3.reshape(1, STATE_W), lim3.reshape(1, STATE_W)
    w_bre = _block_diag(bbre.transpose(0, 2, 1)).astype(BF16)
    w_bim = _block_diag(bbim.transpose(0, 2, 1)).astype(BF16)
    w_cre = _block_diag(cre3.transpose(0, 2, 1)).astype(BF16)
    w_cim = _block_diag(cim3.transpose(0, 2, 1)).astype(BF16)
    u_s = to_segments(u)
    names = ("w_ffn_up", "w_ffn_down")
    (yg_s, y_ssm, h_re, h_im, hin_re, hin_im), got = ssm_fwd(
        u_s, dvec, w_bre, w_bim, w_cre, -w_cim, lre, lim, "ssm_fwd", carry=Gather([shards[n] for n in names]))
    gathered(names, got)
    yg = from_segments(yg_s)
    gv = mm([(yg, wts["w_glu_v"])], "nn", BF16, "mm_glu_v")
    gg = mm([(yg, wts["w_glu_g"])], "nn", BF16, "mm_glu_g")

    def gate_fn(r, c):
        gts, ab, gv_, gg_ = r
        sa, ss = _sigmoid(gts[:, :D_MODEL]), _sigmoid(gts[:, D_MODEL:])
        return [sa * ab + ss * (gv_ * _sigmoid(gg_))], []

    (merged,) = rowwise("gate_merge", gate_fn, [gates, attn_branch, gv, gg], [], [(D_MODEL, BF16)])
    o_mix = mm([(merged, wts["w_out"])], "nn", F32, "mm_out")

    def mid_fn(r, c):
        x1 = r[0] + _rms(r[1], c[0])[0]
        return [x1, _rms(x1, c[1])[0]], []

    x1, h2 = rowwise("rms_mid", mid_fn, [x, o_mix], [g2, g3], [(D_MODEL, F32), (D_MODEL, BF16)])
    fa, fb, fin = mm([(h2, wts["w_ffn_gate"]), (h2, wts["w_ffn_up"])], "nt", [BF16, BF16, BF16], "mm_ffn_in", tn=FFN_TN,
                     epilogue=lambda p, e: [p[0], p[1], p[0] * _sigmoid(p[0]) * p[1]])
    f = mm([(fin, wts["w_ffn_down"])], "nn", F32, "mm_ffn_down", tn=512, tk=D_FF)

    def loss_fn(r, c):
        x1_, f_, tgt = r
        y, n, rr = _rms(f_, c[0])
        err = x1_ + y - tgt
        dout = err * (1.0 / D_MODEL)
        df, dg = _rms_bwd(dout, n, rr, c[0])
        lp = 0.5 * jnp.sum(jnp.sum(err * err, axis=-1, keepdims=True) * (1.0 / D_MODEL), axis=0, keepdims=True)
        return [df, dout], [dg, lp]

    df, dout, dg4, loss_part = rowwise("loss_bwd", loss_fn, [x1, f, target], [g4], [(D_MODEL, BF16), (D_MODEL, F32)],
                                       acc_outs=[(1, D_MODEL), (1, 1)])
    def sent(names, blocks):
        for n, b in zip(names, blocks):
            recv[n] = b

    def to_owners(names, dws):
        return AllToAll([_split_for_devices(d, n) for n, d in zip(names, dws)])

    def swiglu_bwd(p, e):
        dfin_, (a, b) = p[0], e
        sg = _sigmoid(a)
        return [dfin_ * b * (sg * (1.0 + a * (1.0 - sg))), dfin_ * a * sg]

    da, db = mm([(df, wts["w_ffn_down"])], "nt", [BF16, BF16], "mm_d_fin", tn=FFN_TN, epilogue=swiglu_bwd, extras=[fa, fb])
    dw_ffn_down = mm([(fin, df)], "tn", BF16, "mm_dw_ffn_down")
    dh2, got = mm([(da, wts["w_ffn_gate"]), (db, wts["w_ffn_up"])], "nn", F32, "mm_d_h2", tm=512, tn=1024, tk=D_FF // 2,
                  carry=to_owners(["w_ffn_down"], [dw_ffn_down]))
    sent(["w_ffn_down"], got)
    dw_ffn_gate = mm([(da, h2)], "tn", BF16, "mm_dw_ffn_gate")
    dw_ffn_up, got = mm([(db, h2)], "tn", BF16, "mm_dw_ffn_up", carry=to_owners(["w_ffn_gate"], [dw_ffn_gate]))
    sent(["w_ffn_gate"], got)

    def mid_bwd(r, c):
        dh2_, dout_, x1_, o_ = r
        _, n3, r3 = _rms(x1_, c[1])
        dx1, dg3_ = _rms_bwd(dh2_, n3, r3, c[1])
        dx1 = dx1 + dout_
        _, n2, r2 = _rms(o_, c[0])
        do_, dg2_ = _rms_bwd(dx1, n2, r2, c[0])
        return [dx1, do_], [dg2_, dg3_]

    dx1, do_mix, dg2, dg3 = rowwise("rms_mid_bwd", mid_bwd, [dh2, dout, x1, o_mix], [g2, g3], [(D_MODEL, F32), (D_MODEL, BF16)],
                                    acc_outs=[(1, D_MODEL), (1, D_MODEL)])
    dmerged = mm([(do_mix, wts["w_out"])], "nt", BF16, "mm_d_merged")
    dw_out = mm([(merged, do_mix)], "tn", BF16, "mm_dw_out")

    def gate_bwd(r, c):
        dm, gts, ab, gv_, gg_ = r
        sa, ss, sg = _sigmoid(gts[:, :D_MODEL]), _sigmoid(gts[:, D_MODEL:]), _sigmoid(gg_)
        branch = gv_ * sg
        dbranch = dm * ss
        dgates = jnp.concatenate([dm * ab * sa * (1.0 - sa), dm * branch * ss * (1.0 - ss)], axis=-1)
        return [dgates, dm * sa, dbranch * sg, dbranch * gv_ * sg * (1.0 - sg)], []

    dgates, dab, dgv, dgg = rowwise("gate_bwd", gate_bwd, [dmerged, gates, attn_branch, gv, gg], [],
                                    [(2 * D_MODEL, BF16), (D_MODEL, BF16), (D_MODEL, BF16), (D_MODEL, BF16)])
    dattn = mm([(dab, wts["w_attn_up"])], "nt", F32, "mm_d_attn")
    dw_up = mm([(attn, dab)], "tn", BF16, "mm_dw_up")
    dyg = mm([(dgv, wts["w_glu_v"]), (dgg, wts["w_glu_g"])], "nt", F32, "mm_d_yg")
    dw_glu_v = mm([(yg, dgv)], "tn", BF16, "mm_dw_glu_v")
    dw_glu_g = mm([(yg, dgg)], "tn", BF16, "mm_dw_glu_g")

    names = ["w_ffn_up", "w_out", "w_attn_up", "w_glu_v", "w_glu_g"]
    (du_s, dbre_d, dbim_d, dcre_d, dcim_d, dl_re8, dl_im8, dd_ssm), got = ssm_bwd(
        to_segments(dyg), y_ssm, u_s, h_re, h_im, hin_re, hin_im, dvec, w_bre.transpose(0, 2, 1), w_bim.transpose(0, 2, 1),
        w_cre.transpose(0, 2, 1), -w_cim.transpose(0, 2, 1), lre, lim, "ssm_bwd",
        carry=to_owners(names, [dw_ffn_up, dw_out, dw_up, dw_glu_v, dw_glu_g]))
    sent(names, got)
    dbb_re = _block_diag_extract(dbre_d, SSM_GROUP, SSM_STATE).transpose(0, 2, 1)
    dbb_im = _block_diag_extract(dbim_d, SSM_GROUP, SSM_STATE).transpose(0, 2, 1)
    dc_re = _block_diag_extract(dcre_d, SSM_STATE, SSM_GROUP).transpose(0, 2, 1)
    dc_im = -_block_diag_extract(dcim_d, SSM_STATE, SSM_GROUP).transpose(0, 2, 1)

    def fold8(r, c):
        return [], [jnp.sum(r[0], axis=0, keepdims=True), jnp.sum(r[1], axis=0, keepdims=True)]

    dl_re, dl_im = rowwise("ssm_dl_fold", fold8, [dl_re8, dl_im8], [], [], acc_outs=[(1, STATE_W), (1, STATE_W)], ts=SEGS)
    da_re, da_im, dldt, db_re, db_im = ssm_prep_bwd(
        are3, aim3, ldt3, bre3, bim3, dbb_re, dbb_im,
        dl_re.reshape(SSM_GROUPS, SSM_STATE, 1), dl_im.reshape(SSM_GROUPS, SSM_STATE, 1))
    du = from_segments(du_s)

    def merge_bwd(r, c):
        dat = r[0]
        o0, o1, o2, l0, l1, l2 = natural(r[1:])
        w0, w1, w2 = _mix_weights(l0, l1, l2)
        tot = _head_sum(dat * (w0 * o0 + w1 * o1 + w2 * o2))
        to4, to16 = _perm_matrix(PERM_TS, 4, False), _perm_matrix(PERM_TS, 16, False)
        return [w0 * dat, _permute(to4, (w1 * dat).astype(BF16)), _permute(to16, (w2 * dat).astype(BF16)),
                w0 * tot, _permute(to4, w1 * tot), _permute(to16, w2 * tot)], []

    mb = rowwise("attn_merge_bwd", merge_bwd, [dattn] + outs + lses, [],
                 [(GROUP_W, BF16), (GROUP_W, BF16, 4), (GROUP_W, BF16, 16), (GROUP_W, F32), (GROUP_W, F32, 4), (GROUP_W, F32, 16)],
                 ts=PERM_TS)
    dqs, dw_qkv = [], []
    for g, (_, dil) in enumerate(ATTN_GROUPS):
        dq = attn_bwd(qkv[g].reshape(dil, s // dil, 3 * GROUP_W), mb[g].reshape(dil, s // dil, GROUP_W),
                      lses[g].reshape(dil, s // dil, GROUP_W), mb[3 + g].reshape(dil, s // dil, GROUP_W),
                      g, f"attn_bwd{g}").reshape(s, 3 * GROUP_W)
        dqs.append(dq)
        dw_qkv.append(mm([(hd[g].reshape(s, D_MODEL), dq)], "tn", BF16, f"mm_dw_qkv{g}"))
    dw_u = mm([(h, du)], "tn", BF16, "mm_dw_u")
    dw_gates = mm([(h, dgates)], "tn", BF16, "mm_dw_gates")
    dw_in = jnp.concatenate(
        [dw_qkv[g][:, o * GROUP_W:(o + 1) * GROUP_W] for o in range(3) for g in range(3)] + [dw_u, dw_gates], axis=1)
    dw_in_split = _split_for_devices(dw_in, "w_in")
    rows = D_MODEL // W_IN_CHUNKS
    chunks = [AllToAll([dw_in_split[:, i * rows:(i + 1) * rows]]) for i in range(W_IN_CHUNKS)]
    dh_parts, got_chunks = [], []
    for g, (_, dil) in enumerate(ATTN_GROUPS):
        dh_g, got = mm([(dqs[g], w_in_t)], "nn", BF16, f"mm_d_h_qkv{g}", tk=GROUP_W, b_window=qkv_rows(g), carry=chunks[g])
        got_chunks.append(got[0])
        dh_parts.append(dh_g if dil == 1 else dh_g.reshape(dil, s // dil, D_MODEL))
    dh_parts.append(mm([(du, w_u_t)], "nn", BF16, "mm_d_h_u"))
    dh_gates, got = mm([(dgates, w_gates_t)], "nn", BF16, "mm_d_h_gates", carry=chunks[3])
    got_chunks.append(got[0])
    dh_parts.append(dh_gates)
    recv["w_in"] = jnp.concatenate(got_chunks, axis=1)

    def in_bwd(r, c):
        dh1 = _permute(_perm_matrix(PERM_TS, 4, True), r[1].astype(BF16))
        dh2_ = _permute(_perm_matrix(PERM_TS, 16, True), r[2].astype(BF16))
        dh = r[0] + dh1 + dh2_ + r[3] + r[4]
        _, n1, r1 = _rms(r[6], c[0])
        dx, dg1_ = _rms_bwd(dh, n1, r1, c[0])
        return [dx + r[5]], [dg1_]

    grad_x, dg1 = rowwise("rms_in_bwd", in_bwd, dh_parts + [dx1, x], [g1], [(D_MODEL, F32)], acc_outs=[(1, D_MODEL)], ts=PERM_TS)

    dsmall = dict(norm_mix_pre=dg1, ssm_a_re=da_re, ssm_a_im=da_im, ssm_log_dt=dldt, ssm_b_re=db_re, ssm_b_im=db_im,
                  ssm_c_re=dc_re, ssm_c_im=dc_im, ssm_d=dd_ssm, norm_mix_post=dg2, norm_ffn_pre=dg3, norm_ffn_post=dg4)
    return loss_part, grad_x, recv, dsmall


def adamw(parts, w, m, v, name, carry=None):
    r, c = w.shape
    tr = r
    while tr > 8 and tr % 2 == 0 and tr * c * (8 * parts.dtype.itemsize + 28) * 2 > 24 * 1024 * 1024:
        tr //= 2
    assert r % tr == 0 and (tr % 8 == 0 or tr == r)
    c1, c2 = 1.0 / (1.0 - ADAM_B1 ** ADAM_STEP), 1.0 / (1.0 - ADAM_B2 ** ADAM_STEP)

    def body(p_ref, w_ref, m_ref, v_ref, g_o, d_o, m_o, v_o):
        g = p_ref[0].astype(F32)
        for i in range(1, N_DEV):
            g = g + p_ref[i].astype(F32)
        mn = ADAM_B1 * m_ref[...] + (1.0 - ADAM_B1) * g
        vn = ADAM_B2 * v_ref[...] + (1.0 - ADAM_B2) * (g * g)
        g_o[...] = g
        m_o[...] = mn
        v_o[...] = vn
        d_o[...] = -ADAM_LR * ((mn * c1) / (jnp.sqrt(vn * c2) + ADAM_EPS) + ADAM_WD * w_ref[...])

    blk = pl.BlockSpec((tr, c), lambda i: (i, 0))
    return _run(
        body, [parts, w, m, v], carry=carry, name=name, grid=(r // tr,),
        in_specs=[pl.BlockSpec((N_DEV, tr, c), lambda i: (0, i, 0)), blk, blk, blk],
        out_specs=[blk] * 4, out_shape=[jax.ShapeDtypeStruct((r, c), F32)] * 4, compiler_params=_cparams(("parallel",)),
    )


PACK_C = 1024
SHARDED = ("w_in", "w_attn_up", "w_glu_v", "w_glu_g", "w_out", "w_ffn_gate", "w_ffn_up", "w_ffn_down")
ROW_SHARDED = ("w_out", "w_ffn_down")
SENT_TRANSPOSED = ("w_in", "w_ffn_gate", "w_ffn_up")
GRAD_TRANSPOSED = ("w_ffn_gate", "w_ffn_up")
SMALL = ("norm_mix_pre", "ssm_a_re", "ssm_a_im", "ssm_log_dt", "ssm_b_re", "ssm_b_im", "ssm_c_re", "ssm_c_im", "ssm_d",
         "norm_mix_post", "norm_ffn_pre", "norm_ffn_post")
WEIGHTS = ("norm_mix_pre", "w_in", "w_attn_up", "ssm_a_re", "ssm_a_im", "ssm_log_dt", "ssm_b_re", "ssm_b_im", "ssm_c_re",
           "ssm_c_im", "ssm_d", "w_glu_v", "w_glu_g", "w_out", "norm_mix_post", "norm_ffn_pre", "w_ffn_gate", "w_ffn_up",
           "w_ffn_down", "norm_ffn_post")


def _pack(arrs, dtype, pad_rows_to=64):
    flat = jnp.concatenate([a.reshape(-1).astype(dtype) for a in arrs])
    n = flat.shape[0]
    rows = -(-n // PACK_C)
    rows = -(-rows // pad_rows_to) * pad_rows_to
    return jnp.pad(flat, (0, rows * PACK_C - n)).reshape(rows, PACK_C)


def _unpack(flat2d, shapes):
    flat = flat2d.reshape(-1)
    out, off = [], 0
    for shp in shapes:
        n = int(np.prod(shp))
        out.append(flat[off:off + n].reshape(shp))
        off += n
    return out


def _full_from_gathered(gathered, name):
    if name in ROW_SHARDED or name in SENT_TRANSPOSED:
        return gathered.reshape(-1, gathered.shape[2])
    return gathered.transpose(1, 0, 2).reshape(gathered.shape[1], -1)


def _split_for_devices(full, name):
    if name in ROW_SHARDED or name in GRAD_TRANSPOSED:
        return full.reshape(N_DEV, -1, full.shape[1])
    return full.reshape(full.shape[0], N_DEV, -1).transpose(1, 0, 2)


def kernel(x, norm_mix_pre, w_in, w_attn_up, ssm_a_re, ssm_a_im, ssm_log_dt, ssm_b_re, ssm_b_im, ssm_c_re, ssm_c_im, ssm_d, w_glu_v, w_glu_g, w_out, norm_mix_post, norm_ffn_pre, w_ffn_gate, w_ffn_up, w_ffn_down, norm_ffn_post, loss_target, m_norm_mix_pre, m_w_in, m_w_attn_up, m_ssm_a_re, m_ssm_a_im, m_ssm_log_dt, m_ssm_b_re, m_ssm_b_im, m_ssm_c_re, m_ssm_c_im, m_ssm_d, m_w_glu_v, m_w_glu_g, m_w_out, m_norm_mix_post, m_norm_ffn_pre, m_w_ffn_gate, m_w_ffn_up, m_w_ffn_down, m_norm_ffn_post, v_norm_mix_pre, v_w_in, v_w_attn_up, v_ssm_a_re, v_ssm_a_im, v_ssm_log_dt, v_ssm_b_re, v_ssm_b_im, v_ssm_c_re, v_ssm_c_im, v_ssm_d, v_w_glu_v, v_w_glu_g, v_w_out, v_norm_mix_post, v_norm_ffn_pre, v_w_ffn_gate, v_w_ffn_up, v_w_ffn_down, v_norm_ffn_post):
    args = dict(locals())
    wv = {n: args[n][0] for n in WEIGHTS}
    mv = {n: args["m_" + n][0] for n in WEIGHTS}
    vv = {n: args["v_" + n][0] for n in WEIGHTS}

    shards = {n: (wv[n].T if n in SENT_TRANSPOSED else wv[n]).astype(BF16) for n in SHARDED}
    small = {n: wv[n] for n in SMALL}
    loss_part, grad_x, recv, dsmall = local_step(x[0], loss_target[0], shards, small)
    for n in GRAD_TRANSPOSED:
        recv[n] = recv[n].transpose(0, 2, 1)

    small_shapes = [wv[n].shape for n in SMALL]
    res = {}
    res["w_in"], (sgather,) = adamw(recv["w_in"], wv["w_in"], mv["w_in"], vv["w_in"], "adamw_w_in",
                                    carry=Gather([_pack([dsmall[n] for n in SMALL], F32)]))
    for n in SHARDED[1:]:
        res[n] = adamw(recv[n], wv[n], mv[n], vv[n], "adamw_" + n)
    sres = adamw(sgather, _pack([wv[n] for n in SMALL], F32), _pack([mv[n] for n in SMALL], F32),
                 _pack([vv[n] for n in SMALL], F32), "adamw_small")
    sun = [_unpack(t, small_shapes) for t in sres]
    for k, n in enumerate(SMALL):
        res[n] = tuple(sun[t][k] for t in range(4))

    loss = lax.psum(loss_part[0, 0], ("x", "y", "c"))
    outs = [loss, grad_x[None]]
    for t in range(4):
        outs += [res[n][t][None] for n in WEIGHTS]
    return tuple(outs)
```

```python
import functools
import math

import numpy as np
import jax
import jax.numpy as jnp
from jax import lax
from jax.experimental import pallas as pl
from jax.experimental.pallas import tpu as pltpu

F32 = jnp.float32
BF16 = jnp.bfloat16

D_MODEL = 2048
HEAD_DIM = 128
HEADS_PER_GROUP = 4
ATTN_GROUPS = ((128, 1), (512, 4), (2048, 16))
N_HEADS = HEADS_PER_GROUP * len(ATTN_GROUPS)
GROUP_W = HEADS_PER_GROUP * HEAD_DIM
HQ = N_HEADS * HEAD_DIM
SSM_W = 1024
SSM_GROUP = 16
SSM_GROUPS = 64
SSM_STATE = 64
STATE_W = SSM_GROUPS * SSM_STATE
D_FF = 5632
EPS = 1e-6
N_DEV = 8
SEGS = 8
BD = 8

ADAM_LR, ADAM_B1, ADAM_B2, ADAM_EPS, ADAM_WD, ADAM_STEP = 0.001, 0.9, 0.999, 1e-08, 0.01, 10

VMEM_LIMIT = 56 * 1024 * 1024
HBM_SPEC = pl.BlockSpec(memory_space=pltpu.HBM)
MESH_ID = pl.DeviceIdType.MESH
NEG = -1e30


def _pcall(body, **kw):
    return pl.pallas_call(body, **kw)


def _cparams(sem=None):
    if sem is None:
        return pltpu.CompilerParams(vmem_limit_bytes=VMEM_LIMIT)
    return pltpu.CompilerParams(vmem_limit_bytes=VMEM_LIMIT, dimension_semantics=sem)


def _my_coords():
    return lax.axis_index("x"), lax.axis_index("y"), lax.axis_index("c")


class Gather:
    def __init__(self, xs):
        self.arrays = list(xs)
        self.out_shapes = [jax.ShapeDtypeStruct((N_DEV,) + x.shape, x.dtype) for x in xs]

    def _ctx(self, out_refs, send_sems, recv_sems):
        mx, my, mc = _my_coords()
        me, sibling = (mx, my, mc), (mx, my, 1 - mc)
        chips = [(1 - mx, my), (mx, 1 - my), (1 - mx, 1 - my)]

        def slot(a, px, py, pc):
            return out_refs[a].at[4 * px + 2 * py + pc]

        def copy(a, k, block, to, src=None):
            return pltpu.make_async_remote_copy(
                src_ref=slot(a, *block) if src is None else src, dst_ref=slot(a, *block),
                send_sem=send_sems.at[7 * a + k], recv_sem=recv_sems.at[7 * a + k], device_id=to, device_id_type=MESH_ID)

        return me, sibling, chips, mc, slot, copy

    def _first(self, a, x_refs, ctx):
        me, sibling, chips, mc, slot, copy = ctx
        return [copy(a, 0, me, sibling, src=x_refs[a])] + [copy(a, 1 + j, me, (*chip, mc), src=x_refs[a]) for j, chip in enumerate(chips)]

    def start(self, x_refs, out_refs, send_sems, recv_sems, local_sems):
        ctx = self._ctx(out_refs, send_sems, recv_sems)
        me, slot = ctx[0], ctx[4]
        for a in range(len(self.arrays)):
            pltpu.make_async_copy(x_refs[a], slot(a, *me), local_sems.at[a]).start()
            for cp in self._first(a, x_refs, ctx):
                cp.start()

    def finish(self, x_refs, out_refs, send_sems, recv_sems, local_sems):
        ctx = self._ctx(out_refs, send_sems, recv_sems)
        me, sibling, chips, mc, slot, copy = ctx
        na = len(self.arrays)
        passed = []
        for a in range(na):
            for j, chip in enumerate(chips):
                copy(a, 1 + j, (*chip, mc), me).wait_recv()
                fwd = copy(a, 4 + j, (*chip, mc), sibling)
                fwd.start()
                passed.append(fwd)
        for a in range(na):
            copy(a, 0, sibling, me).wait_recv()
            for j, chip in enumerate(chips):
                copy(a, 4 + j, (*chip, 1 - mc), me).wait_recv()
        for a in range(na):
            for cp in self._first(a, x_refs, ctx):
                cp.wait_send()
        for cp in passed:
            cp.wait_send()
        for a in range(na):
            pltpu.make_async_copy(x_refs[a], slot(a, *me), local_sems.at[a]).wait()


class AllToAll:
    def __init__(self, ps):
        self.arrays = list(ps)
        self.out_shapes = [jax.ShapeDtypeStruct(p.shape, p.dtype) for p in ps]

    def _copies(self, p_refs, out_refs, send_sems, recv_sems, local_sems):
        mx, my, mc = _my_coords()
        me = 4 * mx + 2 * my + mc
        local, remote = [], []
        for a in range(len(self.arrays)):
            local.append(pltpu.make_async_copy(p_refs[a].at[me], out_refs[a].at[me], local_sems.at[a]))
            for k in range(1, N_DEV):
                px, py, pc = mx ^ ((k >> 2) & 1), my ^ ((k >> 1) & 1), mc ^ (k & 1)
                remote.append(pltpu.make_async_remote_copy(
                    src_ref=p_refs[a].at[4 * px + 2 * py + pc], dst_ref=out_refs[a].at[me],
                    send_sem=send_sems.at[7 * a + k - 1], recv_sem=recv_sems.at[7 * a + k - 1],
                    device_id=(px, py, pc), device_id_type=MESH_ID))
        return local, remote

    def start(self, *refs):
        local, remote = self._copies(*refs)
        for cp in local + remote:
            cp.start()

    def finish(self, *refs):
        local, remote = self._copies(*refs)
        for cp in remote:
            cp.wait_recv()
        for cp in remote:
            cp.wait_send()
        for cp in local:
            cp.wait()


def _run(body, args, carry=None, **kw):
    if carry is None:
        return _pcall(body, **kw)(*args)
    grid = kw["grid"]
    single = not isinstance(kw["out_shape"], (list, tuple))
    in_specs = list(kw["in_specs"])
    out_specs = [kw["out_specs"]] if single else list(kw["out_specs"])
    out_shape = [kw["out_shape"]] if single else list(kw["out_shape"])
    scratch = list(kw.get("scratch_shapes", []))
    na, nin, nout, nscr = len(carry.arrays), len(in_specs), len(out_specs), len(scratch)

    def carried(*refs):
        ins, cin = refs[:nin], refs[nin:nin + na]
        outs, cout = refs[nin + na:nin + na + nout], refs[nin + na + nout:nin + 2 * na + nout]
        scr = refs[nin + 2 * na + nout:nin + 2 * na + nout + nscr]
        sems = refs[nin + 2 * na + nout + nscr:]
        ids = [pl.program_id(i) for i in range(len(grid))]
        first, last = ids[0] == 0, ids[0] == grid[0] - 1
        for i in range(1, len(grid)):
            first = jnp.logical_and(first, ids[i] == 0)
            last = jnp.logical_and(last, ids[i] == grid[i] - 1)

        @pl.when(first)
        def _():
            carry.start(cin, cout, *sems)

        body(*ins, *outs, *scr)

        @pl.when(last)
        def _():
            carry.finish(cin, cout, *sems)

    res = _pcall(
        carried, name=kw["name"], grid=grid, in_specs=in_specs + [HBM_SPEC] * na, out_specs=out_specs + [HBM_SPEC] * na,
        out_shape=out_shape + carry.out_shapes,
        scratch_shapes=scratch + [pltpu.SemaphoreType.DMA((7 * na,)), pltpu.SemaphoreType.DMA((7 * na,)), pltpu.SemaphoreType.DMA((na,))],
        compiler_params=_cparams(("arbitrary",) * len(grid)),
    )(*args, *carry.arrays)
    main = res[:nout]
    return (main[0] if single else main), list(res[nout:])


_DN = {"nn": (((1,), (0,)), ((), ())), "nt": (((1,), (1,)), ((), ())), "tn": (((0,), (0,)), ((), ()))}


LANE = 128
MM_TM, MM_TN, MM_TK = 1024, 1536, 2048


def _tile(n, cap):
    for t in range(min(cap, n) // LANE * LANE, 0, -LANE):
        if n % t == 0:
            return t
    raise ValueError(n)


DW_TM, DW_TN, DW_TK = 512, 512, 8192


def mm(pairs, mode, out_dtype, name, tm=None, tn=None, tk=None, carry=None, epilogue=None, extras=(), b_window=None):
    a0, b0 = pairs[0]
    if mode == "nn":
        (m, k), n = a0.shape, b0.shape[1]
    elif mode == "nt":
        (m, k), n = a0.shape, b0.shape[0]
    else:
        (k, m), n = a0.shape, b0.shape[1]
    if b_window is not None:
        assert mode in ("nn", "nt") and len(pairs) == 1
        if mode == "nt":
            n = b_window[0]
        else:
            assert k == b_window[0]
    caps = (DW_TM, DW_TN, DW_TK) if mode == "tn" else (MM_TM, MM_TN, MM_TK)
    tm, tn, tk = _tile(m, tm or caps[0]), _tile(n, tn or caps[1]), _tile(k, tk or caps[2])
    nk = k // tk
    npairs = len(pairs)
    nex = len(extras)
    fused = epilogue is not None
    assert not fused or nk == 1
    out_dtypes = list(out_dtype) if fused else [out_dtype]

    def body(*refs):
        prods = []
        for p in range(npairs):
            a = refs[2 * p][...].astype(BF16) if (p == 0 or pairs[p][0] is not pairs[p - 1][0]) else a
            b = refs[2 * p + 1][...].astype(BF16)
            prods.append(lax.dot_general(a, b, _DN[mode], preferred_element_type=F32))
        if fused:
            ex = [refs[2 * npairs + e][...].astype(F32) for e in range(nex)]
            for o_ref, val in zip(refs[2 * npairs + nex:], epilogue(prods, ex)):
                o_ref[...] = val.astype(o_ref.dtype)
            return
        o_ref = refs[2 * npairs]
        tot = prods[0]
        for d in prods[1:]:
            tot = tot + d
        if nk == 1:
            o_ref[...] = tot.astype(o_ref.dtype)
            return
        acc = refs[2 * npairs + 1]
        kk = pl.program_id(2)

        @pl.when(kk == 0)
        def _():
            acc[...] = tot

        @pl.when(kk > 0)
        def _():
            acc[...] += tot

        @pl.when(kk == nk - 1)
        def _():
            o_ref[...] = acc[...].astype(o_ref.dtype)

    rows_of = b_window[1] if b_window is not None else (lambda t: t)
    if mode == "nn":
        sp = [pl.BlockSpec((tm, tk), lambda i, j, kk: (i, kk)), pl.BlockSpec((tk, tn), lambda i, j, kk: (rows_of(kk), j))]
    elif mode == "nt":
        sp = [pl.BlockSpec((tm, tk), lambda i, j, kk: (i, kk)), pl.BlockSpec((tn, tk), lambda i, j, kk: (rows_of(j), kk))]
    else:
        sp = [pl.BlockSpec((tk, tm), lambda i, j, kk: (kk, i)), pl.BlockSpec((tk, tn), lambda i, j, kk: (kk, j))]
    o_spec = pl.BlockSpec((tm, tn), lambda i, j, kk: (i, j))
    out_shapes = [jax.ShapeDtypeStruct((m, n), dt) for dt in out_dtypes]
    return _run(
        body, [t for pr in pairs for t in pr] + list(extras), carry=carry, name=name, grid=(m // tm, n // tn, nk),
        in_specs=sp * npairs + [o_spec] * nex,
        out_specs=[o_spec] * len(out_shapes) if fused else o_spec,
        out_shape=out_shapes if fused else out_shapes[0],
        scratch_shapes=[pltpu.VMEM((tm, tn), F32)] if nk > 1 else [],
        compiler_params=_cparams(("parallel", "parallel", "arbitrary")),
    )


def rowwise(name, fn, row_ins, const_ins, row_outs, acc_outs=(), ts=None, carry=None):
    s = row_ins[0].shape[0]
    row_outs = [ro if len(ro) == 3 else (*ro, 1) for ro in row_outs]
    if ts is None:
        per_row = sum(a.shape[-1] * a.dtype.itemsize for a in row_ins) + sum(w * jnp.dtype(dt).itemsize for w, dt, _ in row_outs)
        ts = 512
        while ts > 8 and 2 * ts * per_row > 20 * 1024 * 1024:
            ts //= 2
    ts = min(ts, s)
    assert s % ts == 0
    nr, nc, no, na = len(row_ins), len(const_ins), len(row_outs), len(acc_outs)

    def body(*refs):
        rows = [r[...].reshape(ts, r.shape[-1]).astype(F32) for r in refs[:nr]]
        consts = [r[...] for r in refs[nr:nr + nc]]
        outs, accs = fn(rows, consts)
        for r, v in zip(refs[nr + nc:nr + nc + no], outs):
            r[...] = v.astype(r.dtype).reshape(r.shape)
        if na:
            first = pl.program_id(0) == 0
            for r, v in zip(refs[nr + nc + no:], accs):
                @pl.when(first)
                def _(r=r, v=v):
                    r[...] = v

                @pl.when(jnp.logical_not(first))
                def _(r=r, v=v):
                    r[...] += v

    def tile_spec(w, d):
        if d == 1:
            return pl.BlockSpec((ts, w), lambda i: (i, 0))
        return pl.BlockSpec((d, ts // d, w), lambda i: (0, i, 0))

    in_specs = [tile_spec(a.shape[-1], a.shape[0] if a.ndim == 3 else 1) for a in row_ins]
    in_specs += [pl.BlockSpec(c.shape, lambda i, nd=c.ndim: (0,) * nd) for c in const_ins]
    out_specs = [tile_spec(w, d) for w, _, d in row_outs]
    out_specs += [pl.BlockSpec(shp, lambda i, nd=len(shp): (0,) * nd) for shp in acc_outs]
    out_shape = [jax.ShapeDtypeStruct((s, w) if d == 1 else (d, s // d, w), dt) for w, dt, d in row_outs]
    out_shape += [jax.ShapeDtypeStruct(shp, F32) for shp in acc_outs]
    return _run(
        body, [*row_ins, *const_ins], carry=carry, name=name, grid=(s // ts,), in_specs=in_specs, out_specs=out_specs,
        out_shape=out_shape, compiler_params=_cparams(("arbitrary",)),
    )


PERM_TS = 256


def _perm_matrix(ts, d, inverse):
    i = lax.broadcasted_iota(jnp.int32, (ts, ts), 0)
    k = lax.broadcasted_iota(jnp.int32, (ts, ts), 1)
    per = ts // d
    src = (i % d) * per + i // d if inverse else (i % per) * d + i // per
    return jnp.where(k == src, 1.0, 0.0).astype(BF16)


def _permute(p, x):
    if x.dtype == BF16:
        return jnp.dot(p, x, preferred_element_type=F32)
    hi = x.astype(BF16)
    rest = x - hi.astype(F32)
    mid = rest.astype(BF16)
    lo = (rest - mid.astype(F32)).astype(BF16)
    out = jnp.dot(p, hi, preferred_element_type=F32) + jnp.dot(p, mid, preferred_element_type=F32)
    return out + jnp.dot(p, lo, preferred_element_type=F32)


def _rms(x, gain):
    r = lax.rsqrt(jnp.mean(x * x, axis=-1, keepdims=True) + EPS)
    n = x * r
    return n * gain, n, r


def _rms_bwd(dy, n, r, gain):
    dn = dy * gain
    dx = r * (dn - n * jnp.mean(dn * n, axis=-1, keepdims=True))
    return dx, jnp.sum(dy * n, axis=0, keepdims=True)


def _sigmoid(x):
    return 1.0 / (1.0 + jnp.exp(-x))


_GELU_K = math.sqrt(2.0 / math.pi)


def _gelu(x):
    t = jnp.tanh(_GELU_K * (x + 0.044715 * x * x * x))
    return 0.5 * x * (1.0 + t), t


def _gelu_grad(x, t):
    return 0.5 * (1.0 + t) + 0.5 * x * (1.0 - t * t) * _GELU_K * (1.0 + 3.0 * 0.044715 * x * x)


def _head_sum(x):
    parts = []
    for h in range(HEADS_PER_GROUP):
        sl = x[:, h * HEAD_DIM:(h + 1) * HEAD_DIM]
        parts.append(jnp.broadcast_to(jnp.sum(sl, axis=-1, keepdims=True), sl.shape))
    return jnp.concatenate(parts, axis=-1)


def _mix_weights(l0, l1, l2):
    mx = jnp.maximum(jnp.maximum(l0, l1), l2)
    e0, e1, e2 = jnp.exp(l0 - mx), jnp.exp(l1 - mx), jnp.exp(l2 - mx)
    inv = 1.0 / (e0 + e1 + e2)
    return e0 * inv, e1 * inv, e2 * inv


BLK = 128


def _slopes(g):
    return [2.0 ** (-8.0 * (g * HEADS_PER_GROUP + h + 1) / N_HEADS) for h in range(HEADS_PER_GROUP)]


def _attn_masks(dil):
    qi = lax.broadcasted_iota(jnp.int32, (BLK, BLK), 0)
    ki = lax.broadcasted_iota(jnp.int32, (BLK, BLK), 1)
    dist_c = qi - ki
    dist_p = BLK + qi - ki
    return dist_c >= 0, dist_p <= BLK, (dist_c * dil).astype(F32), (dist_p * dil).astype(F32)


def attn_fwd(qkv, g, name):
    dil, length, _ = qkv.shape
    scale = HEAD_DIM ** -0.5
    slopes = _slopes(g)

    def body(q_ref, kc_ref, vc_ref, kp_ref, vp_ref, o_ref, l_ref):
        n = pl.program_id(1)
        ok_c, ok_p, dc, dp = _attn_masks(dil)
        ok_p = jnp.logical_and(ok_p, n > 0)
        for h in range(HEADS_PER_GROUP):
            sl = slice(h * HEAD_DIM, (h + 1) * HEAD_DIM)
            q = q_ref[:, sl]
            s_c = lax.dot_general(q, kc_ref[:, sl], _DN["nt"], preferred_element_type=F32) * scale - slopes[h] * dc
            s_p = lax.dot_general(q, kp_ref[:, sl], _DN["nt"], preferred_element_type=F32) * scale - slopes[h] * dp
            s_c = jnp.where(ok_c, s_c, NEG)
            s_p = jnp.where(ok_p, s_p, NEG)
            mx = jnp.maximum(jnp.max(s_c, axis=-1, keepdims=True), jnp.max(s_p, axis=-1, keepdims=True))
            p_c = jnp.exp(s_c - mx)
            p_p = jnp.exp(s_p - mx)
            den = jnp.sum(p_c, axis=-1, keepdims=True) + jnp.sum(p_p, axis=-1, keepdims=True)
            acc = jnp.dot(p_c.astype(BF16), vc_ref[:, sl], preferred_element_type=F32)
            acc += jnp.dot(p_p.astype(BF16), vp_ref[:, sl], preferred_element_type=F32)
            o_ref[:, sl] = acc / den
            l_ref[:, sl] = jnp.broadcast_to(mx + jnp.log(den), (BLK, HEAD_DIM))

    def spec(col, prev):
        if prev:
            return pl.BlockSpec((None, BLK, GROUP_W), lambda r, n: (r, jnp.maximum(n - 1, 0), col))
        return pl.BlockSpec((None, BLK, GROUP_W), lambda r, n: (r, n, col))

    out_spec = pl.BlockSpec((None, BLK, GROUP_W), lambda r, n: (r, n, 0))
    return _pcall(
        body, name=name, grid=(dil, length // BLK),
        in_specs=[spec(0, False), spec(1, False), spec(2, False), spec(1, True), spec(2, True)],
        out_specs=[out_spec, out_spec],
        out_shape=[jax.ShapeDtypeStruct((dil, length, GROUP_W), F32)] * 2,
        compiler_params=_cparams(("parallel", "parallel")),
    )(qkv, qkv, qkv, qkv, qkv)


def attn_bwd(qkv, dout, lse, dd, g, name, carry=None):
    dil, length, _ = qkv.shape
    nblk = length // BLK
    scale = HEAD_DIM ** -0.5
    slopes = _slopes(g)

    def body(q_ref, kc_ref, vc_ref, kp_ref, vp_ref, qn_ref, do_ref, don_ref, l_ref, ln_ref, d_ref, dn_ref, o_ref):
        n = pl.program_id(1)
        ok_c, ok_p, dc, dp = _attn_masks(dil)
        ok_prev = jnp.logical_and(ok_p, n > 0)
        ok_next = jnp.logical_and(ok_p, n < nblk - 1)
        for h in range(HEADS_PER_GROUP):
            sl = slice(h * HEAD_DIM, (h + 1) * HEAD_DIM)
            q, kc, vc, kp, vp, qn = q_ref[:, sl], kc_ref[:, sl], vc_ref[:, sl], kp_ref[:, sl], vp_ref[:, sl], qn_ref[:, sl]
            do, don = do_ref[:, sl], don_ref[:, sl]
            lse_q, lse_n, dd_q, dd_n = l_ref[:, sl], ln_ref[:, sl], d_ref[:, sl], dn_ref[:, sl]

            def probs(qq, kk, dist, ok, lse_t):
                s = lax.dot_general(qq, kk, _DN["nt"], preferred_element_type=F32) * scale - slopes[h] * dist
                return jnp.where(ok, jnp.exp(jnp.where(ok, s, NEG) - lse_t), 0.0)

            p_c = probs(q, kc, dc, ok_c, lse_q)
            p_p = probs(q, kp, dp, ok_prev, lse_q)
            p_x = probs(qn, kc, dp, ok_next, lse_n)
            ds_c = p_c * (lax.dot_general(do, vc, _DN["nt"], preferred_element_type=F32) - dd_q)
            ds_p = p_p * (lax.dot_general(do, vp, _DN["nt"], preferred_element_type=F32) - dd_q)
            ds_x = p_x * (lax.dot_general(don, vc, _DN["nt"], preferred_element_type=F32) - dd_n)
            ds_c16, ds_p16, ds_x16 = ds_c.astype(BF16), ds_p.astype(BF16), ds_x.astype(BF16)
            dq = jnp.dot(ds_c16, kc, preferred_element_type=F32) + jnp.dot(ds_p16, kp, preferred_element_type=F32)
            dk = lax.dot_general(ds_c16, q, _DN["tn"], preferred_element_type=F32)
            dk += lax.dot_general(ds_x16, qn, _DN["tn"], preferred_element_type=F32)
            dv = lax.dot_general(p_c.astype(BF16), do, _DN["tn"], preferred_element_type=F32)
            dv += lax.dot_general(p_x.astype(BF16), don, _DN["tn"], preferred_element_type=F32)
            o_ref[:, h * HEAD_DIM:(h + 1) * HEAD_DIM] = (dq * scale).astype(BF16)
            o_ref[:, GROUP_W + h * HEAD_DIM:GROUP_W + (h + 1) * HEAD_DIM] = (dk * scale).astype(BF16)
            o_ref[:, 2 * GROUP_W + h * HEAD_DIM:2 * GROUP_W + (h + 1) * HEAD_DIM] = dv.astype(BF16)

    def spec(col, which):
        if which == "prev":
            return pl.BlockSpec((None, BLK, GROUP_W), lambda r, n: (r, jnp.maximum(n - 1, 0), col))
        if which == "next":
            return pl.BlockSpec((None, BLK, GROUP_W), lambda r, n: (r, jnp.minimum(n + 1, nblk - 1), col))
        return pl.BlockSpec((None, BLK, GROUP_W), lambda r, n: (r, n, col))

    return _run(
        body, [qkv, qkv, qkv, qkv, qkv, qkv, dout, dout, lse, lse, dd, dd], carry=carry, name=name, grid=(dil, nblk),
        in_specs=[spec(0, "cur"), spec(1, "cur"), spec(2, "cur"), spec(1, "prev"), spec(2, "prev"), spec(0, "next"),
                  spec(0, "cur"), spec(0, "next"), spec(0, "cur"), spec(0, "next"), spec(0, "cur"), spec(0, "next")],
        out_specs=pl.BlockSpec((None, BLK, 3 * GROUP_W), lambda r, n: (r, n, 0)),
        out_shape=jax.ShapeDtypeStruct((dil, length, 3 * GROUP_W), BF16),
        compiler_params=_cparams(("parallel", "parallel")),
    )


def _ssm_prep_values(are, aim, logdt):
    dt = jnp.exp(logdt)
    mag = jnp.exp(are * dt)
    lb_re, lb_im = mag * jnp.cos(aim * dt), mag * jnp.sin(aim * dt)
    inv = 1.0 / (are * are + aim * aim)
    n_re, n_im = lb_re - 1.0, lb_im
    f_re = (n_re * are + n_im * aim) * inv
    f_im = (n_im * are - n_re * aim) * inv
    return dt, lb_re, lb_im, f_re, f_im, inv


PREP_G = 8


def _group_specs(are, logdt, bre):
    def spec(a):
        return pl.BlockSpec((PREP_G,) + a.shape[1:], lambda i: (i, 0, 0))
    return spec(are), spec(logdt), spec(bre)


def ssm_prep(are, aim, logdt, bre, bim):
    def body(are_r, aim_r, ldt_r, bre_r, bim_r, lre_o, lim_o, bbre_o, bbim_o):
        _, lb_re, lb_im, f_re, f_im, _ = _ssm_prep_values(are_r[...], aim_r[...], ldt_r[...])
        lre_o[...] = lb_re
        lim_o[...] = lb_im
        bbre_o[...] = f_re * bre_r[...] - f_im * bim_r[...]
        bbim_o[...] = f_re * bim_r[...] + f_im * bre_r[...]

    sh1 = jax.ShapeDtypeStruct(are.shape, F32)
    shb = jax.ShapeDtypeStruct(bre.shape, F32)
    s1, sd, sb = _group_specs(are, logdt, bre)
    return _pcall(body, name="ssm_prep", grid=(SSM_GROUPS // PREP_G,), in_specs=[s1, s1, sd, sb, sb], out_specs=[s1, s1, sb, sb],
                  out_shape=[sh1, sh1, shb, shb], compiler_params=_cparams(("parallel",)))(are, aim, logdt, bre, bim)


def ssm_prep_bwd(are, aim, logdt, bre, bim, dbbre, dbbim, dlre, dlim):
    def body(are_r, aim_r, ldt_r, bre_r, bim_r, dbbre_r, dbbim_r, dlre_r, dlim_r, dare_o, daim_o, dldt_o, dbre_o, dbim_o):
        are_v, aim_v = are_r[...], aim_r[...]
        dt, lb_re, lb_im, f_re, f_im, inv = _ssm_prep_values(are_v, aim_v, ldt_r[...])
        b_re, b_im, g_re, g_im = bre_r[...], bim_r[...], dbbre_r[...], dbbim_r[...]
        dbre_o[...] = f_re * g_re + f_im * g_im
        dbim_o[...] = f_re * g_im - f_im * g_re
        df_re = jnp.sum(b_re * g_re + b_im * g_im, axis=-1, keepdims=True)
        df_im = jnp.sum(b_re * g_im - b_im * g_re, axis=-1, keepdims=True)
        il_re, il_im = are_v * inv, -aim_v * inv
        cl_re = dlre_r[...] + il_re * df_re + il_im * df_im
        cl_im = dlim_r[...] + il_re * df_im - il_im * df_re
        q_re = -(f_re * il_re - f_im * il_im)
        q_im = -(f_re * il_im + f_im * il_re)
        ca_re = q_re * df_re + q_im * df_im
        ca_im = q_re * df_im - q_im * df_re
        cz_re = lb_re * cl_re + lb_im * cl_im
        cz_im = lb_re * cl_im - lb_im * cl_re
        dare_o[...] = ca_re + dt * cz_re
        daim_o[...] = ca_im + dt * cz_im
        dldt_o[...] = dt * jnp.sum(are_v * cz_re + aim_v * cz_im, axis=1, keepdims=True)

    sh1 = jax.ShapeDtypeStruct(are.shape, F32)
    shb = jax.ShapeDtypeStruct(bre.shape, F32)
    s1, sd, sb = _group_specs(are, logdt, bre)
    return _pcall(
        body, name="ssm_prep_bwd", grid=(SSM_GROUPS // PREP_G,), in_specs=[s1, s1, sd, sb, sb, sb, sb, s1, s1],
        out_specs=[s1, s1, sd, sb, sb], out_shape=[sh1, sh1, jax.ShapeDtypeStruct(logdt.shape, F32), shb, shb],
        compiler_params=_cparams(("parallel",)),
    )(are, aim, logdt, bre, bim, dbbre, dbbim, dlre, dlim)


SCAN_WC = 512


def _chain_segments(a_re, a_im, e_re, e_im, nsq, reverse):
    p_re, p_im = a_re, a_im
    for _ in range(nsq):
        p_re, p_im = p_re * p_re - p_im * p_im, 2.0 * p_re * p_im
    row = lax.broadcasted_iota(jnp.int32, e_re.shape, 0)
    edge = (row == SEGS - 1) if reverse else (row == 0)
    shift = SEGS - 1 if reverse else 1
    c_re, c_im = jnp.zeros_like(e_re), jnp.zeros_like(e_im)
    for _ in range(SEGS - 1):
        n_re = p_re * c_re - p_im * c_im + e_re
        n_im = p_re * c_im + p_im * c_re + e_im
        c_re = jnp.where(edge, 0.0, pltpu.roll(n_re, shift, 0))
        c_im = jnp.where(edge, 0.0, pltpu.roll(n_im, shift, 0))
    return c_re, c_im


def _scan_dims(s):
    steps = s // SEGS
    assert steps & (steps - 1) == 0
    tt = min(128, steps)
    return steps, tt, steps // tt, tt * SEGS, int(math.log2(steps))


U_BLK = SSM_W // BD


def ssm_fwd(u_s, dvec, w_bre, w_bim, w_cre, w_cim_neg, lre, lim, name, carry=None):
    s = u_s.shape[0]
    steps, tt, nch, rows, nsq = _scan_dims(s)
    nb, ub_w, wc = w_bre.shape

    def body(u_r, d_r, bre_r, bim_r, cre_r, cim_r, lre_r, lim_r, yg_o, ys_o, hre_o, him_o, hin_re_o, hin_im_o,
             st_re, st_im, x_re, x_im, h_re, h_im):
        ps, ch = pl.program_id(1), pl.program_id(2)
        a_re = jnp.broadcast_to(lre_r[...], (SEGS, wc))
        a_im = jnp.broadcast_to(lim_r[...], (SEGS, wc))
        ub = u_r[...]
        ub16 = ub.astype(BF16)
        x_re[...] = jnp.dot(ub16, bre_r[...], preferred_element_type=F32)
        x_im[...] = jnp.dot(ub16, bim_r[...], preferred_element_type=F32)

        @pl.when(jnp.logical_and(ps == 0, ch == 0))
        def _():
            st_re[...] = jnp.zeros_like(st_re)
            st_im[...] = jnp.zeros_like(st_im)

        @pl.when(jnp.logical_and(ps == 1, ch == 0))
        def _():
            c_re, c_im = _chain_segments(a_re, a_im, st_re[...], st_im[...], nsq, False)
            st_re[...] = c_re
            st_im[...] = c_im
            hin_re_o[...] = c_re
            hin_im_o[...] = c_im

        def run(store):
            def step(t, hc):
                off = pl.multiple_of(t * SEGS, SEGS)
                n_re = a_re * hc[0] - a_im * hc[1] + x_re[pl.ds(off, SEGS), :]
                n_im = a_re * hc[1] + a_im * hc[0] + x_im[pl.ds(off, SEGS), :]
                if store:
                    h_re[pl.ds(off, SEGS), :] = n_re
                    h_im[pl.ds(off, SEGS), :] = n_im
                return n_re, n_im

            fin = lax.fori_loop(0, tt, step, (st_re[...], st_im[...]))
            st_re[...] = fin[0]
            st_im[...] = fin[1]

        @pl.when(ps == 0)
        def _():
            run(False)

        @pl.when(ps == 1)
        def _():
            run(True)
            hr16, hi16 = h_re[...].astype(BF16), h_im[...].astype(BF16)
            hre_o[...] = hr16
            him_o[...] = hi16
            y = jnp.dot(hr16, cre_r[...], preferred_element_type=F32) + jnp.dot(hi16, cim_r[...], preferred_element_type=F32)
            y = y + d_r[...] * ub
            ys_o[...] = y
            yg_o[...] = _gelu(y)[0].astype(BF16)

    def pass1(ps, c):
        return jnp.where(ps == 1, c, 0)

    u_spec = pl.BlockSpec((rows, ub_w), lambda j, ps, c: (c, j))
    d_spec = pl.BlockSpec((1, ub_w), lambda j, ps, c: (0, j))
    b_spec = pl.BlockSpec((None, ub_w, wc), lambda j, ps, c: (j, 0, 0))
    c_spec = pl.BlockSpec((None, wc, ub_w), lambda j, ps, c: (j, 0, 0))
    l_spec = pl.BlockSpec((1, wc), lambda j, ps, c: (0, j))
    y_spec = pl.BlockSpec((rows, ub_w), lambda j, ps, c: (pass1(ps, c), j))
    h_spec = pl.BlockSpec((rows, wc), lambda j, ps, c: (pass1(ps, c), j))
    e_spec = pl.BlockSpec((SEGS, wc), lambda j, ps, c: (0, j))
    return _run(
        body, [u_s, dvec, w_bre, w_bim, w_cre, w_cim_neg, lre, lim], carry=carry, name=name, grid=(nb, 2, nch),
        in_specs=[u_spec, d_spec, b_spec, b_spec, c_spec, c_spec, l_spec, l_spec],
        out_specs=[y_spec, y_spec, h_spec, h_spec, e_spec, e_spec],
        out_shape=[jax.ShapeDtypeStruct((s, SSM_W), BF16), jax.ShapeDtypeStruct((s, SSM_W), F32),
                   jax.ShapeDtypeStruct((s, STATE_W), BF16), jax.ShapeDtypeStruct((s, STATE_W), BF16),
                   jax.ShapeDtypeStruct((SEGS, STATE_W), F32), jax.ShapeDtypeStruct((SEGS, STATE_W), F32)],
        scratch_shapes=[pltpu.VMEM((SEGS, wc), F32)] * 2 + [pltpu.VMEM((rows, wc), F32)] * 4,
        compiler_params=_cparams(("parallel", "arbitrary", "arbitrary")),
    )


def ssm_bwd(dyg_s, ys, u_s, h_re, h_im, hin_re, hin_im, dvec, w_bre_t, w_bim_t, w_cre_t, w_cim_neg_t, lre, lim, name, carry=None):
    s = u_s.shape[0]
    steps, tt, nch, rows, nsq = _scan_dims(s)
    half = 2 * SEGS

    def body(dyg_r, ys_r, u_r, hre_r, him_r, pre_r, pim_r, cin_re_r, cin_im_r, d_r, bre_r, bim_r, cre_r, cim_r, lre_r, lim_r,
             du_o, dbre_o, dbim_o, dcre_o, dcim_o, dlre_o, dlim_o, dd_o, st_re, st_im, x_re, x_im, g_re, g_im, hf_re, hf_im):
        ps, ch = pl.program_id(1), pl.program_id(2)
        a_re = jnp.broadcast_to(lre_r[...], (SEGS, SCAN_WC))
        a_im = -jnp.broadcast_to(lim_r[...], (SEGS, SCAN_WC))
        ub, y = u_r[...], ys_r[...]
        dy = dyg_r[...] * _gelu_grad(y, _gelu(y)[1])
        dy16 = dy.astype(BF16)
        x_re[...] = jnp.dot(dy16, cre_r[...], preferred_element_type=F32)
        x_im[...] = jnp.dot(dy16, cim_r[...], preferred_element_type=F32)

        @pl.when(jnp.logical_and(ps == 0, ch == 0))
        def _():
            st_re[...] = jnp.zeros_like(st_re)
            st_im[...] = jnp.zeros_like(st_im)

        @pl.when(jnp.logical_and(ps == 1, ch == 0))
        def _():
            c_re, c_im = _chain_segments(a_re, a_im, st_re[...], st_im[...], nsq, True)
            st_re[...] = c_re
            st_im[...] = c_im
            dlre_o[...] = jnp.zeros_like(dlre_o)
            dlim_o[...] = jnp.zeros_like(dlim_o)

        @pl.when(ps == 0)
        def _():
            def step(i, hc):
                off = pl.multiple_of((tt - 1 - i) * SEGS, SEGS)
                return (a_re * hc[0] - a_im * hc[1] + x_re[pl.ds(off, SEGS), :],
                        a_re * hc[1] + a_im * hc[0] + x_im[pl.ds(off, SEGS), :])

            fin = lax.fori_loop(0, tt, step, (st_re[...], st_im[...]))
            st_re[...] = fin[0]
            st_im[...] = fin[1]

        @pl.when(ps == 1)
        def _():
            hf_re[...] = hre_r[...].astype(F32)
            hf_im[...] = him_r[...].astype(F32)
            first_chunk = ch == nch - 1
            edge_re = jnp.where(first_chunk, cin_re_r[...], pre_r[...].astype(F32)[SEGS:, :])
            edge_im = jnp.where(first_chunk, cin_im_r[...], pim_r[...].astype(F32)[SEGS:, :])

            def step(i, hc):
                t = tt - 1 - i
                off = pl.multiple_of(t * SEGS, SEGS)
                n_re = a_re * hc[0] - a_im * hc[1] + x_re[pl.ds(off, SEGS), :]
                n_im = a_re * hc[1] + a_im * hc[0] + x_im[pl.ds(off, SEGS), :]
                g_re[pl.ds(off, SEGS), :] = n_re
                g_im[pl.ds(off, SEGS), :] = n_im
                offp = pl.multiple_of(jnp.maximum(t - 1, 0) * SEGS, SEGS)
                hp_re = jnp.where(t == 0, edge_re, hf_re[pl.ds(offp, SEGS), :])
                hp_im = jnp.where(t == 0, edge_im, hf_im[pl.ds(offp, SEGS), :])
                return n_re, n_im, hc[2] + hp_re * n_re + hp_im * n_im, hc[3] + hp_re * n_im - hp_im * n_re

            fin = lax.fori_loop(0, tt, step, (st_re[...], st_im[...], dlre_o[...], dlim_o[...]))
            st_re[...] = fin[0]
            st_im[...] = fin[1]
            dlre_o[...] = fin[2]
            dlim_o[...] = fin[3]

            gr16, gi16 = g_re[...].astype(BF16), g_im[...].astype(BF16)
            du = jnp.dot(gr16, bre_r[...], preferred_element_type=F32) + jnp.dot(gi16, bim_r[...], preferred_element_type=F32)
            du_o[...] = du + d_r[...] * dy
            ub16 = ub.astype(BF16)
            parts = [
                (dbre_o, lax.dot_general(ub16, gr16, _DN["tn"], preferred_element_type=F32)),
                (dbim_o, lax.dot_general(ub16, gi16, _DN["tn"], preferred_element_type=F32)),
                (dcre_o, lax.dot_general(hre_r[...], dy16, _DN["tn"], preferred_element_type=F32)),
                (dcim_o, lax.dot_general(him_r[...], dy16, _DN["tn"], preferred_element_type=F32)),
                (dd_o, jnp.sum(dy * ub, axis=0, keepdims=True)),
            ]
            for ref, val in parts:
                @pl.when(ch == 0)
                def _(ref=ref, val=val):
                    ref[...] = val

                @pl.when(ch > 0)
                def _(ref=ref, val=val):
                    ref[...] += val

    def chunk(c):
        return nch - 1 - c

    def pass1(ps, c):
        return jnp.where(ps == 1, chunk(c), chunk(0))

    u_spec = pl.BlockSpec((rows, U_BLK), lambda j, ps, c: (chunk(c), j))
    h_spec = pl.BlockSpec((rows, SCAN_WC), lambda j, ps, c: (pass1(ps, c), j))
    prev_spec = pl.BlockSpec((half, SCAN_WC), lambda j, ps, c: (jnp.maximum(pass1(ps, c) * (rows // half) - 1, 0), j))
    e_spec = pl.BlockSpec((SEGS, SCAN_WC), lambda j, ps, c: (0, j))
    d_spec = pl.BlockSpec((1, U_BLK), lambda j, ps, c: (0, j))
    bt_spec = pl.BlockSpec((None, SCAN_WC, U_BLK), lambda j, ps, c: (j, 0, 0))
    ct_spec = pl.BlockSpec((None, U_BLK, SCAN_WC), lambda j, ps, c: (j, 0, 0))
    l_spec = pl.BlockSpec((1, SCAN_WC), lambda j, ps, c: (0, j))
    du_spec = pl.BlockSpec((rows, U_BLK), lambda j, ps, c: (pass1(ps, c), j))
    return _run(
        body, [dyg_s, ys, u_s, h_re, h_im, h_re, h_im, hin_re, hin_im, dvec, w_bre_t, w_bim_t, w_cre_t, w_cim_neg_t, lre, lim],
        carry=carry, name=name, grid=(BD, 2, nch),
        in_specs=[u_spec, u_spec, u_spec, h_spec, h_spec, prev_spec, prev_spec, e_spec, e_spec, d_spec, bt_spec, bt_spec,
                  ct_spec, ct_spec, l_spec, l_spec],
        out_specs=[du_spec, ct_spec, ct_spec, bt_spec, bt_spec, e_spec, e_spec, d_spec],
        out_shape=[jax.ShapeDtypeStruct((s, SSM_W), F32)] + [jax.ShapeDtypeStruct((BD, U_BLK, SCAN_WC), F32)] * 2
        + [jax.ShapeDtypeStruct((BD, SCAN_WC, U_BLK), F32)] * 2 + [jax.ShapeDtypeStruct((SEGS, STATE_W), F32)] * 2
        + [jax.ShapeDtypeStruct((1, SSM_W), F32)],
        scratch_shapes=[pltpu.VMEM((SEGS, SCAN_WC), F32)] * 2 + [pltpu.VMEM((rows, SCAN_WC), F32)] * 6,
        compiler_params=_cparams(("parallel", "arbitrary", "arbitrary")),
    )


FWD_BD = 4


def _block_diag(m, nb=BD):
    g, r, c = m.shape
    m = m.reshape(nb, g // nb, r, c)
    eye = jnp.eye(g // nb, dtype=m.dtype)
    return jnp.einsum("jarc,ab->jarbc", m, eye).reshape(nb, (g // nb) * r, (g // nb) * c)


def _block_diag_extract(m, r, c):
    per = m.shape[1] // r
    m = m.reshape(BD, per, r, per, c)
    return jnp.einsum("jarac->jarc", m).reshape(BD * per, r, c)


def to_segments(a):
    s, w = a.shape
    return a.reshape(SEGS, s // SEGS, w).transpose(1, 0, 2).reshape(s, w)


def from_segments(a):
    s, w = a.shape
    return a.reshape(s // SEGS, SEGS, w).transpose(1, 0, 2).reshape(s, w)


W_IN_CHUNK_ROWS = (336, 336, 336, 240, 672, 128)
FFN_GATE_CHUNK_ROWS = (608, 96)


def _row_chunks(blocks, sizes):
    assert sum(sizes) == blocks.shape[1]
    out, at = [], 0
    for n in sizes:
        out.append(AllToAll([blocks[:, at:at + n]]))
        at += n
    return out
FFN_TN = 512


def local_step(x, target, shards, small):
    s = x.shape[0]
    g1, g2, g3, g4 = (small[k].reshape(1, D_MODEL) for k in ("norm_mix_pre", "norm_mix_post", "norm_ffn_pre", "norm_ffn_post"))
    dvec = small["ssm_d"].reshape(1, SSM_W)
    wts, recv = {}, {}

    def gathered(names, blocks):
        for n, b in zip(names, blocks):
            wts[n] = _full_from_gathered(b, n)

    def rms_in_fn(r, c):
        hh = _rms(r[0], c[0])[0].astype(BF16)
        return [hh, _permute(_perm_matrix(PERM_TS, 4, False), hh), _permute(_perm_matrix(PERM_TS, 16, False), hh)], []

    (h, h4, h16), got = rowwise("rms_in", rms_in_fn, [x], [g1], [(D_MODEL, BF16), (D_MODEL, BF16, 4), (D_MODEL, BF16, 16)],
                                ts=PERM_TS, carry=Gather([shards["w_in"]]))
    w_in_t = _full_from_gathered(got[0], "w_in")
    w_u_t, w_gates_t = w_in_t[3 * HQ:3 * HQ + SSM_W], w_in_t[3 * HQ + SSM_W:]

    def qkv_rows(g):
        return 3 * GROUP_W, lambda t: 3 * t + g

    hd = [h.reshape(1, s, D_MODEL), h4, h16]
    qkv = [None] * 3
    names = ("w_attn_up", "w_glu_v", "w_glu_g")
    qkv[0], got = mm([(hd[0].reshape(s, D_MODEL), w_in_t)], "nt", BF16, "mm_qkv0", tn=GROUP_W, b_window=qkv_rows(0),
                     carry=Gather([shards[n] for n in names]))
    gathered(names, got)
    qkv[1], got = mm([(hd[1].reshape(s, D_MODEL), w_in_t)], "nt", BF16, "mm_qkv1", tn=GROUP_W, b_window=qkv_rows(1),
                     carry=Gather([shards["w_out"]]))
    gathered(("w_out",), got)
    qkv[2] = mm([(hd[2].reshape(s, D_MODEL), w_in_t)], "nt", BF16, "mm_qkv2", tn=GROUP_W, b_window=qkv_rows(2))
    u = mm([(h, w_u_t)], "nt", F32, "mm_u")
    gates, got = mm([(h, w_gates_t)], "nt", BF16, "mm_gates", carry=Gather([shards["w_ffn_gate"]]))
    gathered(("w_ffn_gate",), got)

    outs, lses = [], []
    for g, (_, dil) in enumerate(ATTN_GROUPS):
        o, l = attn_fwd(qkv[g].reshape(dil, s // dil, 3 * GROUP_W), g, f"attn_fwd{g}")
        outs.append(o.reshape(s, GROUP_W) if dil == 1 else o)
        lses.append(l.reshape(s, GROUP_W) if dil == 1 else l)

    def natural(r):
        back4, back16 = _perm_matrix(PERM_TS, 4, True), _perm_matrix(PERM_TS, 16, True)
        return r[0], _permute(back4, r[1]), _permute(back16, r[2]), r[3], _permute(back4, r[4]), _permute(back16, r[5])

    def merge_fn(r, c):
        o0, o1, o2, l0, l1, l2 = natural(r)
        w0, w1, w2 = _mix_weights(l0, l1, l2)
        return [w0 * o0 + w1 * o1 + w2 * o2], []

    (attn,) = rowwise("attn_merge", merge_fn, outs + lses, [], [(GROUP_W, BF16)], ts=PERM_TS)
    attn_branch = mm([(attn, wts["w_attn_up"])], "nn", BF16, "mm_up")

    are3 = small["ssm_a_re"].reshape(SSM_GROUPS, SSM_STATE, 1)
    aim3 = small["ssm_a_im"].reshape(SSM_GROUPS, SSM_STATE, 1)
    ldt3 = small["ssm_log_dt"].reshape(SSM_GROUPS, 1, 1)
    bre3 = small["ssm_b_re"].reshape(SSM_GROUPS, SSM_STATE, SSM_GROUP)
    bim3 = small["ssm_b_im"].reshape(SSM_GROUPS, SSM_STATE, SSM_GROUP)
    cre3 = small["ssm_c_re"].reshape(SSM_GROUPS, SSM_GROUP, SSM_STATE)
    cim3 = small["ssm_c_im"].reshape(SSM_GROUPS, SSM_GROUP, SSM_STATE)
    lre3, lim3, bbre, bbim = ssm_prep(are3, aim3, ldt3, bre3, bim3)
    lre, lim = lre3.reshape(1, STATE_W), lim3.reshape(1, STATE_W)
    w_bre = _block_diag(bbre.transpose(0, 2, 1)).astype(BF16)
    w_bim = _block_diag(bbim.transpose(0, 2, 1)).astype(BF16)
    w_cre = _block_diag(cre3.transpose(0, 2, 1)).astype(BF16)
    w_cim = _block_diag(cim3.transpose(0, 2, 1)).astype(BF16)
    u_s = to_segments(u)
    names = ("w_ffn_up", "w_ffn_down")
    fwd_w = [_block_diag(t.transpose(0, 2, 1), FWD_BD).astype(BF16) for t in (bbre, bbim, cre3, -cim3)]
    (yg_s, y_ssm, h_re, h_im, hin_re, hin_im), got = ssm_fwd(
        u_s, dvec, *fwd_w, lre, lim, "ssm_fwd", carry=Gather([shards[n] for n in names]))
    gathered(names, got)
    yg = from_segments(yg_s)
    gv = mm([(yg, wts["w_glu_v"])], "nn", BF16, "mm_glu_v")
    gg = mm([(yg, wts["w_glu_g"])], "nn", BF16, "mm_glu_g")

    def gate_fn(r, c):
        gts, ab, gv_, gg_ = r
        sa, ss = _sigmoid(gts[:, :D_MODEL]), _sigmoid(gts[:, D_MODEL:])
        return [sa * ab + ss * (gv_ * _sigmoid(gg_))], []

    (merged,) = rowwise("gate_merge", gate_fn, [gates, attn_branch, gv, gg], [], [(D_MODEL, BF16)])
    o_mix = mm([(merged, wts["w_out"])], "nn", F32, "mm_out")

    def mid_fn(r, c):
        x1 = r[0] + _rms(r[1], c[0])[0]
        return [x1, _rms(x1, c[1])[0]], []

    x1, h2 = rowwise("rms_mid", mid_fn, [x, o_mix], [g2, g3], [(D_MODEL, F32), (D_MODEL, BF16)])
    fa, fb, fin = mm([(h2, wts["w_ffn_gate"]), (h2, wts["w_ffn_up"])], "nt", [BF16, BF16, BF16], "mm_ffn_in", tn=FFN_TN,
                     epilogue=lambda p, e: [p[0], p[1], p[0] * _sigmoid(p[0]) * p[1]])
    f = mm([(fin, wts["w_ffn_down"])], "nn", F32, "mm_ffn_down", tn=512, tk=D_FF)

    def loss_fn(r, c):
        x1_, f_, tgt = r
        y, n, rr = _rms(f_, c[0])
        err = x1_ + y - tgt
        dout = err * (1.0 / D_MODEL)
        df, dg = _rms_bwd(dout, n, rr, c[0])
        lp = 0.5 * jnp.sum(jnp.sum(err * err, axis=-1, keepdims=True) * (1.0 / D_MODEL), axis=0, keepdims=True)
        return [df, dout], [dg, lp]

    df, dout, dg4, loss_part = rowwise("loss_bwd", loss_fn, [x1, f, target], [g4], [(D_MODEL, BF16), (D_MODEL, F32)],
                                       acc_outs=[(1, D_MODEL), (1, 1)])
    def sent(names, blocks):
        for n, b in zip(names, blocks):
            recv[n] = b

    def to_owners(names, dws):
        return AllToAll([_split_for_devices(d, n) for n, d in zip(names, dws)])

    def swiglu_bwd(p, e):
        dfin_, (a, b) = p[0], e
        sg = _sigmoid(a)
        return [dfin_ * b * (sg * (1.0 + a * (1.0 - sg))), dfin_ * a * sg]

    da, db = mm([(df, wts["w_ffn_down"])], "nt", [BF16, BF16], "mm_d_fin", tn=FFN_TN, epilogue=swiglu_bwd, extras=[fa, fb])
    dw_ffn_down = mm([(fin, df)], "tn", BF16, "mm_dw_ffn_down")
    dh2, got = mm([(da, wts["w_ffn_gate"]), (db, wts["w_ffn_up"])], "nn", F32, "mm_d_h2", tm=512, tn=1024, tk=D_FF // 2,
                  carry=to_owners(["w_ffn_down"], [dw_ffn_down]))
    sent(["w_ffn_down"], got)
    dw_ffn_gate = mm([(da, h2)], "tn", BF16, "mm_dw_ffn_gate")
    gate_chunks = _row_chunks(_split_for_devices(dw_ffn_gate, "w_ffn_gate"), FFN_GATE_CHUNK_ROWS)
    dw_ffn_up, got_gate0 = mm([(db, h2)], "tn", BF16, "mm_dw_ffn_up", carry=gate_chunks[0])

    def mid_bwd(r, c):
        dh2_, dout_, x1_, o_ = r
        _, n3, r3 = _rms(x1_, c[1])
        dx1, dg3_ = _rms_bwd(dh2_, n3, r3, c[1])
        dx1 = dx1 + dout_
        _, n2, r2 = _rms(o_, c[0])
        do_, dg2_ = _rms_bwd(dx1, n2, r2, c[0])
        return [dx1, do_], [dg2_, dg3_]

    (dx1, do_mix, dg2, dg3), got_gate1 = rowwise(
        "rms_mid_bwd", mid_bwd, [dh2, dout, x1, o_mix], [g2, g3], [(D_MODEL, F32), (D_MODEL, BF16)],
        acc_outs=[(1, D_MODEL), (1, D_MODEL)], carry=gate_chunks[1])
    recv["w_ffn_gate"] = jnp.concatenate([got_gate0[0], got_gate1[0]], axis=1)
    dmerged = mm([(do_mix, wts["w_out"])], "nt", BF16, "mm_d_merged")
    dw_out = mm([(merged, do_mix)], "tn", BF16, "mm_dw_out")

    def gate_bwd(r, c):
        dm, gts, ab, gv_, gg_ = r
        sa, ss, sg = _sigmoid(gts[:, :D_MODEL]), _sigmoid(gts[:, D_MODEL:]), _sigmoid(gg_)
        branch = gv_ * sg
        dbranch = dm * ss
        dgates = jnp.concatenate([dm * ab * sa * (1.0 - sa), dm * branch * ss * (1.0 - ss)], axis=-1)
        return [dgates, dm * sa, dbranch * sg, dbranch * gv_ * sg * (1.0 - sg)], []

    dgates, dab, dgv, dgg = rowwise("gate_bwd", gate_bwd, [dmerged, gates, attn_branch, gv, gg], [],
                                    [(2 * D_MODEL, BF16), (D_MODEL, BF16), (D_MODEL, BF16), (D_MODEL, BF16)])
    dattn = mm([(dab, wts["w_attn_up"])], "nt", F32, "mm_d_attn")
    dw_up = mm([(attn, dab)], "tn", BF16, "mm_dw_up")
    dyg = mm([(dgv, wts["w_glu_v"]), (dgg, wts["w_glu_g"])], "nt", F32, "mm_d_yg")
    dw_glu_v = mm([(yg, dgv)], "tn", BF16, "mm_dw_glu_v")
    dw_glu_g = mm([(yg, dgg)], "tn", BF16, "mm_dw_glu_g")

    names = ["w_ffn_up", "w_out", "w_attn_up", "w_glu_v", "w_glu_g"]
    (du_s, dbre_d, dbim_d, dcre_d, dcim_d, dl_re8, dl_im8, dd_ssm), got = ssm_bwd(
        to_segments(dyg), y_ssm, u_s, h_re, h_im, hin_re, hin_im, dvec, w_bre.transpose(0, 2, 1), w_bim.transpose(0, 2, 1),
        w_cre.transpose(0, 2, 1), -w_cim.transpose(0, 2, 1), lre, lim, "ssm_bwd",
        carry=to_owners(names, [dw_ffn_up, dw_out, dw_up, dw_glu_v, dw_glu_g]))
    sent(names, got)
    dbb_re = _block_diag_extract(dbre_d, SSM_GROUP, SSM_STATE).transpose(0, 2, 1)
    dbb_im = _block_diag_extract(dbim_d, SSM_GROUP, SSM_STATE).transpose(0, 2, 1)
    dc_re = _block_diag_extract(dcre_d, SSM_STATE, SSM_GROUP).transpose(0, 2, 1)
    dc_im = -_block_diag_extract(dcim_d, SSM_STATE, SSM_GROUP).transpose(0, 2, 1)

    def fold8(r, c):
        return [], [jnp.sum(r[0], axis=0, keepdims=True), jnp.sum(r[1], axis=0, keepdims=True)]

    dl_re, dl_im = rowwise("ssm_dl_fold", fold8, [dl_re8, dl_im8], [], [], acc_outs=[(1, STATE_W), (1, STATE_W)], ts=SEGS)
    da_re, da_im, dldt, db_re, db_im = ssm_prep_bwd(
        are3, aim3, ldt3, bre3, bim3, dbb_re, dbb_im,
        dl_re.reshape(SSM_GROUPS, SSM_STATE, 1), dl_im.reshape(SSM_GROUPS, SSM_STATE, 1))
    du = from_segments(du_s)

    def merge_bwd(r, c):
        dat = r[0]
        o0, o1, o2, l0, l1, l2 = natural(r[1:])
        w0, w1, w2 = _mix_weights(l0, l1, l2)
        tot = _head_sum(dat * (w0 * o0 + w1 * o1 + w2 * o2))
        to4, to16 = _perm_matrix(PERM_TS, 4, False), _perm_matrix(PERM_TS, 16, False)
        return [w0 * dat, _permute(to4, (w1 * dat).astype(BF16)), _permute(to16, (w2 * dat).astype(BF16)),
                w0 * tot, _permute(to4, w1 * tot), _permute(to16, w2 * tot)], []

    mb = rowwise("attn_merge_bwd", merge_bwd, [dattn] + outs + lses, [],
                 [(GROUP_W, BF16), (GROUP_W, BF16, 4), (GROUP_W, BF16, 16), (GROUP_W, F32), (GROUP_W, F32, 4), (GROUP_W, F32, 16)],
                 ts=PERM_TS)
    dqs, dw_qkv = [], []
    for g, (_, dil) in enumerate(ATTN_GROUPS):
        dq = attn_bwd(qkv[g].reshape(dil, s // dil, 3 * GROUP_W), mb[g].reshape(dil, s // dil, GROUP_W),
                      lses[g].reshape(dil, s // dil, GROUP_W), mb[3 + g].reshape(dil, s // dil, GROUP_W),
                      g, f"attn_bwd{g}").reshape(s, 3 * GROUP_W)
        dqs.append(dq)
        dw_qkv.append(mm([(hd[g].reshape(s, D_MODEL), dq)], "tn", BF16, f"mm_dw_qkv{g}"))
    dw_u = mm([(h, du)], "tn", BF16, "mm_dw_u")
    dw_gates = mm([(h, dgates)], "tn", BF16, "mm_dw_gates")
    dw_in = jnp.concatenate(
        [dw_qkv[g][:, o * GROUP_W:(o + 1) * GROUP_W] for o in range(3) for g in range(3)] + [dw_u, dw_gates], axis=1)
    chunks = _row_chunks(_split_for_devices(dw_in, "w_in"), W_IN_CHUNK_ROWS)
    dh_parts, got_chunks = [], []
    for g, (_, dil) in enumerate(ATTN_GROUPS):
        dh_g, got = mm([(dqs[g], w_in_t)], "nn", BF16, f"mm_d_h_qkv{g}", tk=GROUP_W, b_window=qkv_rows(g), carry=chunks[g])
        got_chunks.append(got[0])
        dh_parts.append(dh_g if dil == 1 else dh_g.reshape(dil, s // dil, D_MODEL))
    dh_u, got = mm([(du, w_u_t)], "nn", BF16, "mm_d_h_u", carry=chunks[3])
    got_chunks.append(got[0])
    dh_gates, got = mm([(dgates, w_gates_t)], "nn", BF16, "mm_d_h_gates", carry=chunks[4])
    got_chunks.append(got[0])
    dh_parts += [dh_u, dh_gates]

    def in_bwd(r, c):
        dh1 = _permute(_perm_matrix(PERM_TS, 4, True), r[1].astype(BF16))
        dh2_ = _permute(_perm_matrix(PERM_TS, 16, True), r[2].astype(BF16))
        dh = r[0] + dh1 + dh2_ + r[3] + r[4]
        _, n1, r1 = _rms(r[6], c[0])
        dx, dg1_ = _rms_bwd(dh, n1, r1, c[0])
        return [dx + r[5]], [dg1_]

    (grad_x, dg1), got = rowwise("rms_in_bwd", in_bwd, dh_parts + [dx1, x], [g1], [(D_MODEL, F32)], acc_outs=[(1, D_MODEL)],
                                 ts=PERM_TS, carry=chunks[5])
    got_chunks.append(got[0])
    recv["w_in"] = jnp.concatenate(got_chunks, axis=1)

    dsmall = dict(norm_mix_pre=dg1, ssm_a_re=da_re, ssm_a_im=da_im, ssm_log_dt=dldt, ssm_b_re=db_re, ssm_b_im=db_im,
                  ssm_c_re=dc_re, ssm_c_im=dc_im, ssm_d=dd_ssm, norm_mix_post=dg2, norm_ffn_pre=dg3, norm_ffn_post=dg4)
    return loss_part, grad_x, recv, dsmall


def adamw(parts, w, m, v, name, carry=None):
    r, c = w.shape
    tr = r
    while tr > 8 and tr % 2 == 0 and tr * c * (8 * parts.dtype.itemsize + 28) * 2 > 24 * 1024 * 1024:
        tr //= 2
    assert r % tr == 0 and (tr % 8 == 0 or tr == r)
    c1, c2 = 1.0 / (1.0 - ADAM_B1 ** ADAM_STEP), 1.0 / (1.0 - ADAM_B2 ** ADAM_STEP)

    def body(p_ref, w_ref, m_ref, v_ref, g_o, d_o, m_o, v_o):
        g = p_ref[0].astype(F32)
        for i in range(1, N_DEV):
            g = g + p_ref[i].astype(F32)
        mn = ADAM_B1 * m_ref[...] + (1.0 - ADAM_B1) * g
        vn = ADAM_B2 * v_ref[...] + (1.0 - ADAM_B2) * (g * g)
        g_o[...] = g
        m_o[...] = mn
        v_o[...] = vn
        d_o[...] = -ADAM_LR * ((mn * c1) / (jnp.sqrt(vn * c2) + ADAM_EPS) + ADAM_WD * w_ref[...])

    blk = pl.BlockSpec((tr, c), lambda i: (i, 0))
    return _run(
        body, [parts, w, m, v], carry=carry, name=name, grid=(r // tr,),
        in_specs=[pl.BlockSpec((N_DEV, tr, c), lambda i: (0, i, 0)), blk, blk, blk],
        out_specs=[blk] * 4, out_shape=[jax.ShapeDtypeStruct((r, c), F32)] * 4, compiler_params=_cparams(("parallel",)),
    )


PACK_C = 1024
SHARDED = ("w_in", "w_attn_up", "w_glu_v", "w_glu_g", "w_out", "w_ffn_gate", "w_ffn_up", "w_ffn_down")
ROW_SHARDED = ("w_out", "w_ffn_down")
SENT_TRANSPOSED = ("w_in", "w_ffn_gate", "w_ffn_up")
GRAD_TRANSPOSED = ("w_ffn_gate", "w_ffn_up")
SMALL = ("norm_mix_pre", "ssm_a_re", "ssm_a_im", "ssm_log_dt", "ssm_b_re", "ssm_b_im", "ssm_c_re", "ssm_c_im", "ssm_d",
         "norm_mix_post", "norm_ffn_pre", "norm_ffn_post")
WEIGHTS = ("norm_mix_pre", "w_in", "w_attn_up", "ssm_a_re", "ssm_a_im", "ssm_log_dt", "ssm_b_re", "ssm_b_im", "ssm_c_re",
           "ssm_c_im", "ssm_d", "w_glu_v", "w_glu_g", "w_out", "norm_mix_post", "norm_ffn_pre", "w_ffn_gate", "w_ffn_up",
           "w_ffn_down", "norm_ffn_post")


def _pack(arrs, dtype, pad_rows_to=64):
    flat = jnp.concatenate([a.reshape(-1).astype(dtype) for a in arrs])
    n = flat.shape[0]
    rows = -(-n // PACK_C)
    rows = -(-rows // pad_rows_to) * pad_rows_to
    return jnp.pad(flat, (0, rows * PACK_C - n)).reshape(rows, PACK_C)


def _unpack(flat2d, shapes):
    flat = flat2d.reshape(-1)
    out, off = [], 0
    for shp in shapes:
        n = int(np.prod(shp))
        out.append(flat[off:off + n].reshape(shp))
        off += n
    return out


def _full_from_gathered(gathered, name):
    if name in ROW_SHARDED or name in SENT_TRANSPOSED:
        return gathered.reshape(-1, gathered.shape[2])
    return gathered.transpose(1, 0, 2).reshape(gathered.shape[1], -1)


def _split_for_devices(full, name):
    if name in ROW_SHARDED or name in GRAD_TRANSPOSED:
        return full.reshape(N_DEV, -1, full.shape[1])
    return full.reshape(full.shape[0], N_DEV, -1).transpose(1, 0, 2)


def kernel(x, norm_mix_pre, w_in, w_attn_up, ssm_a_re, ssm_a_im, ssm_log_dt, ssm_b_re, ssm_b_im, ssm_c_re, ssm_c_im, ssm_d, w_glu_v, w_glu_g, w_out, norm_mix_post, norm_ffn_pre, w_ffn_gate, w_ffn_up, w_ffn_down, norm_ffn_post, loss_target, m_norm_mix_pre, m_w_in, m_w_attn_up, m_ssm_a_re, m_ssm_a_im, m_ssm_log_dt, m_ssm_b_re, m_ssm_b_im, m_ssm_c_re, m_ssm_c_im, m_ssm_d, m_w_glu_v, m_w_glu_g, m_w_out, m_norm_mix_post, m_norm_ffn_pre, m_w_ffn_gate, m_w_ffn_up, m_w_ffn_down, m_norm_ffn_post, v_norm_mix_pre, v_w_in, v_w_attn_up, v_ssm_a_re, v_ssm_a_im, v_ssm_log_dt, v_ssm_b_re, v_ssm_b_im, v_ssm_c_re, v_ssm_c_im, v_ssm_d, v_w_glu_v, v_w_glu_g, v_w_out, v_norm_mix_post, v_norm_ffn_pre, v_w_ffn_gate, v_w_ffn_up, v_w_ffn_down, v_norm_ffn_post):
    args = dict(locals())
    wv = {n: args[n][0] for n in WEIGHTS}
    mv = {n: args["m_" + n][0] for n in WEIGHTS}
    vv = {n: args["v_" + n][0] for n in WEIGHTS}

    shards = {n: (wv[n].T if n in SENT_TRANSPOSED else wv[n]).astype(BF16) for n in SHARDED}
    small = {n: wv[n] for n in SMALL}
    loss_part, grad_x, recv, dsmall = local_step(x[0], loss_target[0], shards, small)
    for n in GRAD_TRANSPOSED:
        recv[n] = recv[n].transpose(0, 2, 1)

    small_shapes = [wv[n].shape for n in SMALL]
    res = {}
    res["w_in"], (sgather,) = adamw(recv["w_in"], wv["w_in"], mv["w_in"], vv["w_in"], "adamw_w_in",
                                    carry=Gather([_pack([dsmall[n] for n in SMALL], F32)]))
    for n in SHARDED[1:]:
        res[n] = adamw(recv[n], wv[n], mv[n], vv[n], "adamw_" + n)
    sres = adamw(sgather, _pack([wv[n] for n in SMALL], F32), _pack([mv[n] for n in SMALL], F32),
                 _pack([vv[n] for n in SMALL], F32), "adamw_small")
    sun = [_unpack(t, small_shapes) for t in sres]
    for k, n in enumerate(SMALL):
        res[n] = tuple(sun[t][k] for t in range(4))

    loss = lax.psum(loss_part[0, 0], ("x", "y", "c"))
    outs = [loss, grad_x[None]]
    for t in range(4):
        outs += [res[n][t][None] for n in WEIGHTS]
    return tuple(outs)
```

```python
import functools
import math

import numpy as np
import jax
import jax.numpy as jnp
from jax import lax
from jax.experimental import pallas as pl
from jax.experimental.pallas import tpu as pltpu

F32 = jnp.float32
BF16 = jnp.bfloat16

D_MODEL = 2048
HEAD_DIM = 128
HEADS_PER_GROUP = 4
ATTN_GROUPS = ((128, 1), (512, 4), (2048, 16))
N_HEADS = HEADS_PER_GROUP * len(ATTN_GROUPS)
GROUP_W = HEADS_PER_GROUP * HEAD_DIM
HQ = N_HEADS * HEAD_DIM
SSM_W = 1024
SSM_GROUP = 16
SSM_GROUPS = 64
SSM_STATE = 64
STATE_W = SSM_GROUPS * SSM_STATE
D_FF = 5632
EPS = 1e-6
N_DEV = 8
SEGS = 8
BD = 8

ADAM_LR, ADAM_B1, ADAM_B2, ADAM_EPS, ADAM_WD, ADAM_STEP = 0.001, 0.9, 0.999, 1e-08, 0.01, 10

VMEM_LIMIT = 56 * 1024 * 1024
HBM_SPEC = pl.BlockSpec(memory_space=pltpu.HBM)
MESH_ID = pl.DeviceIdType.MESH
NEG = -1e30


def _pcall(body, **kw):
    return pl.pallas_call(body, **kw)


def _cparams(sem=None):
    if sem is None:
        return pltpu.CompilerParams(vmem_limit_bytes=VMEM_LIMIT)
    return pltpu.CompilerParams(vmem_limit_bytes=VMEM_LIMIT, dimension_semantics=sem)


def _my_coords():
    return lax.axis_index("x"), lax.axis_index("y"), lax.axis_index("c")


class Gather:
    def __init__(self, xs):
        self.arrays = list(xs)
        self.out_shapes = [jax.ShapeDtypeStruct((N_DEV,) + x.shape, x.dtype) for x in xs]

    def _ctx(self, out_refs, send_sems, recv_sems):
        mx, my, mc = _my_coords()
        me, sibling = (mx, my, mc), (mx, my, 1 - mc)
        chips = [(1 - mx, my), (mx, 1 - my), (1 - mx, 1 - my)]

        def slot(a, px, py, pc):
            return out_refs[a].at[4 * px + 2 * py + pc]

        def copy(a, k, block, to, src=None):
            return pltpu.make_async_remote_copy(
                src_ref=slot(a, *block) if src is None else src, dst_ref=slot(a, *block),
                send_sem=send_sems.at[7 * a + k], recv_sem=recv_sems.at[7 * a + k], device_id=to, device_id_type=MESH_ID)

        return me, sibling, chips, mc, slot, copy

    def _first(self, a, x_refs, ctx):
        me, sibling, chips, mc, slot, copy = ctx
        return [copy(a, 0, me, sibling, src=x_refs[a])] + [copy(a, 1 + j, me, (*chip, mc), src=x_refs[a]) for j, chip in enumerate(chips)]

    def start(self, x_refs, out_refs, send_sems, recv_sems, local_sems):
        ctx = self._ctx(out_refs, send_sems, recv_sems)
        me, slot = ctx[0], ctx[4]
        for a in range(len(self.arrays)):
            pltpu.make_async_copy(x_refs[a], slot(a, *me), local_sems.at[a]).start()
            for cp in self._first(a, x_refs, ctx):
                cp.start()

    def finish(self, x_refs, out_refs, send_sems, recv_sems, local_sems):
        ctx = self._ctx(out_refs, send_sems, recv_sems)
        me, sibling, chips, mc, slot, copy = ctx
        na = len(self.arrays)
        passed = []
        for a in range(na):
            for j, chip in enumerate(chips):
                copy(a, 1 + j, (*chip, mc), me).wait_recv()
                fwd = copy(a, 4 + j, (*chip, mc), sibling)
                fwd.start()
                passed.append(fwd)
        for a in range(na):
            copy(a, 0, sibling, me).wait_recv()
            for j, chip in enumerate(chips):
                copy(a, 4 + j, (*chip, 1 - mc), me).wait_recv()
        for a in range(na):
            for cp in self._first(a, x_refs, ctx):
                cp.wait_send()
        for cp in passed:
            cp.wait_send()
        for a in range(na):
            pltpu.make_async_copy(x_refs[a], slot(a, *me), local_sems.at[a]).wait()


class AllToAll:
    def __init__(self, ps):
        self.arrays = list(ps)
        self.out_shapes = [jax.ShapeDtypeStruct(p.shape, p.dtype) for p in ps]

    def _copies(self, p_refs, out_refs, send_sems, recv_sems, local_sems):
        mx, my, mc = _my_coords()
        me = 4 * mx + 2 * my + mc
        local, remote = [], []
        for a in range(len(self.arrays)):
            local.append(pltpu.make_async_copy(p_refs[a].at[me], out_refs[a].at[me], local_sems.at[a]))
            for k in range(1, N_DEV):
                px, py, pc = mx ^ ((k >> 2) & 1), my ^ ((k >> 1) & 1), mc ^ (k & 1)
                remote.append(pltpu.make_async_remote_copy(
                    src_ref=p_refs[a].at[4 * px + 2 * py + pc], dst_ref=out_refs[a].at[me],
                    send_sem=send_sems.at[7 * a + k - 1], recv_sem=recv_sems.at[7 * a + k - 1],
                    device_id=(px, py, pc), device_id_type=MESH_ID))
        return local, remote

    def start(self, *refs):
        local, remote = self._copies(*refs)
        for cp in local + remote:
            cp.start()

    def finish(self, *refs):
        local, remote = self._copies(*refs)
        for cp in remote:
            cp.wait_recv()
        for cp in remote:
            cp.wait_send()
        for cp in local:
            cp.wait()


def _run(body, args, carry=None, **kw):
    if carry is None:
        return _pcall(body, **kw)(*args)
    grid = kw["grid"]
    single = not isinstance(kw["out_shape"], (list, tuple))
    in_specs = list(kw["in_specs"])
    out_specs = [kw["out_specs"]] if single else list(kw["out_specs"])
    out_shape = [kw["out_shape"]] if single else list(kw["out_shape"])
    scratch = list(kw.get("scratch_shapes", []))
    na, nin, nout, nscr = len(carry.arrays), len(in_specs), len(out_specs), len(scratch)

    def carried(*refs):
        ins, cin = refs[:nin], refs[nin:nin + na]
        outs, cout = refs[nin + na:nin + na + nout], refs[nin + na + nout:nin + 2 * na + nout]
        scr = refs[nin + 2 * na + nout:nin + 2 * na + nout + nscr]
        sems = refs[nin + 2 * na + nout + nscr:]
        ids = [pl.program_id(i) for i in range(len(grid))]
        first, last = ids[0] == 0, ids[0] == grid[0] - 1
        for i in range(1, len(grid)):
            first = jnp.logical_and(first, ids[i] == 0)
            last = jnp.logical_and(last, ids[i] == grid[i] - 1)

        @pl.when(first)
        def _():
            carry.start(cin, cout, *sems)

        body(*ins, *outs, *scr)

        @pl.when(last)
        def _():
            carry.finish(cin, cout, *sems)

    res = _pcall(
        carried, name=kw["name"], grid=grid, in_specs=in_specs + [HBM_SPEC] * na, out_specs=out_specs + [HBM_SPEC] * na,
        out_shape=out_shape + carry.out_shapes,
        scratch_shapes=scratch + [pltpu.SemaphoreType.DMA((7 * na,)), pltpu.SemaphoreType.DMA((7 * na,)), pltpu.SemaphoreType.DMA((na,))],
        compiler_params=_cparams(("arbitrary",) * len(grid)),
    )(*args, *carry.arrays)
    main = res[:nout]
    return (main[0] if single else main), list(res[nout:])


_DN = {"nn": (((1,), (0,)), ((), ())), "nt": (((1,), (1,)), ((), ())), "tn": (((0,), (0,)), ((), ()))}


LANE = 128
MM_TM, MM_TN, MM_TK = 1024, 1536, 2048


def _tile(n, cap):
    for t in range(min(cap, n) // LANE * LANE, 0, -LANE):
        if n % t == 0:
            return t
    raise ValueError(n)


DW_TM, DW_TN, DW_TK = 512, 512, 8192


def mm(pairs, mode, out_dtype, name, tm=None, tn=None, tk=None, carry=None, epilogue=None, extras=(), b_window=None):
    a0, b0 = pairs[0]
    if mode == "nn":
        (m, k), n = a0.shape, b0.shape[1]
    elif mode == "nt":
        (m, k), n = a0.shape, b0.shape[0]
    else:
        (k, m), n = a0.shape, b0.shape[1]
    if b_window is not None:
        assert mode in ("nn", "nt") and len(pairs) == 1
        if mode == "nt":
            n = b_window[0]
        else:
            assert k == b_window[0]
    caps = (DW_TM, DW_TN, DW_TK) if mode == "tn" else (MM_TM, MM_TN, MM_TK)
    tm, tn, tk = _tile(m, tm or caps[0]), _tile(n, tn or caps[1]), _tile(k, tk or caps[2])
    nk = k // tk
    npairs = len(pairs)
    nex = len(extras)
    fused = epilogue is not None
    assert not fused or nk == 1
    out_dtypes = list(out_dtype) if fused else [out_dtype]

    def body(*refs):
        prods = []
        for p in range(npairs):
            a = refs[2 * p][...].astype(BF16) if (p == 0 or pairs[p][0] is not pairs[p - 1][0]) else a
            b = refs[2 * p + 1][...].astype(BF16)
            prods.append(lax.dot_general(a, b, _DN[mode], preferred_element_type=F32))
        if fused:
            ex = [refs[2 * npairs + e][...].astype(F32) for e in range(nex)]
            for o_ref, val in zip(refs[2 * npairs + nex:], epilogue(prods, ex)):
                o_ref[...] = val.astype(o_ref.dtype)
            return
        o_ref = refs[2 * npairs]
        tot = prods[0]
        for d in prods[1:]:
            tot = tot + d
        if nk == 1:
            o_ref[...] = tot.astype(o_ref.dtype)
            return
        acc = refs[2 * npairs + 1]
        kk = pl.program_id(2)

        @pl.when(kk == 0)
        def _():
            acc[...] = tot

        @pl.when(kk > 0)
        def _():
            acc[...] += tot

        @pl.when(kk == nk - 1)
        def _():
            o_ref[...] = acc[...].astype(o_ref.dtype)

    rows_of = b_window[1] if b_window is not None else (lambda t: t)
    if mode == "nn":
        sp = [pl.BlockSpec((tm, tk), lambda i, j, kk: (i, kk)), pl.BlockSpec((tk, tn), lambda i, j, kk: (rows_of(kk), j))]
    elif mode == "nt":
        sp = [pl.BlockSpec((tm, tk), lambda i, j, kk: (i, kk)), pl.BlockSpec((tn, tk), lambda i, j, kk: (rows_of(j), kk))]
    else:
        sp = [pl.BlockSpec((tk, tm), lambda i, j, kk: (kk, i)), pl.BlockSpec((tk, tn), lambda i, j, kk: (kk, j))]
    o_spec = pl.BlockSpec((tm, tn), lambda i, j, kk: (i, j))
    out_shapes = [jax.ShapeDtypeStruct((m, n), dt) for dt in out_dtypes]
    return _run(
        body, [t for pr in pairs for t in pr] + list(extras), carry=carry, name=name, grid=(m // tm, n // tn, nk),
        in_specs=sp * npairs + [o_spec] * nex,
        out_specs=[o_spec] * len(out_shapes) if fused else o_spec,
        out_shape=out_shapes if fused else out_shapes[0],
        scratch_shapes=[pltpu.VMEM((tm, tn), F32)] if nk > 1 else [],
        compiler_params=_cparams(("parallel", "parallel", "arbitrary")),
    )


def rowwise(name, fn, row_ins, const_ins, row_outs, acc_outs=(), ts=None, carry=None):
    s = row_ins[0].shape[0]
    row_outs = [ro if len(ro) == 3 else (*ro, 1) for ro in row_outs]
    if ts is None:
        per_row = sum(a.shape[-1] * a.dtype.itemsize for a in row_ins) + sum(w * jnp.dtype(dt).itemsize for w, dt, _ in row_outs)
        ts = 512
        while ts > 8 and 2 * ts * per_row > 20 * 1024 * 1024:
            ts //= 2
    ts = min(ts, s)
    assert s % ts == 0
    nr, nc, no, na = len(row_ins), len(const_ins), len(row_outs), len(acc_outs)

    def body(*refs):
        rows = [r[...].reshape(ts, r.shape[-1]).astype(F32) for r in refs[:nr]]
        consts = [r[...] for r in refs[nr:nr + nc]]
        outs, accs = fn(rows, consts)
        for r, v in zip(refs[nr + nc:nr + nc + no], outs):
            r[...] = v.astype(r.dtype).reshape(r.shape)
        if na:
            first = pl.program_id(0) == 0
            for r, v in zip(refs[nr + nc + no:], accs):
                @pl.when(first)
                def _(r=r, v=v):
                    r[...] = v

                @pl.when(jnp.logical_not(first))
                def _(r=r, v=v):
                    r[...] += v

    def tile_spec(w, d):
        if d == 1:
            return pl.BlockSpec((ts, w), lambda i: (i, 0))
        return pl.BlockSpec((d, ts // d, w), lambda i: (0, i, 0))

    in_specs = [tile_spec(a.shape[-1], a.shape[0] if a.ndim == 3 else 1) for a in row_ins]
    in_specs += [pl.BlockSpec(c.shape, lambda i, nd=c.ndim: (0,) * nd) for c in const_ins]
    out_specs = [tile_spec(w, d) for w, _, d in row_outs]
    out_specs += [pl.BlockSpec(shp, lambda i, nd=len(shp): (0,) * nd) for shp in acc_outs]
    out_shape = [jax.ShapeDtypeStruct((s, w) if d == 1 else (d, s // d, w), dt) for w, dt, d in row_outs]
    out_shape += [jax.ShapeDtypeStruct(shp, F32) for shp in acc_outs]
    return _run(
        body, [*row_ins, *const_ins], carry=carry, name=name, grid=(s // ts,), in_specs=in_specs, out_specs=out_specs,
        out_shape=out_shape, compiler_params=_cparams(("arbitrary",)),
    )


PERM_TS = 256


def _perm_matrix(ts, d, inverse):
    i = lax.broadcasted_iota(jnp.int32, (ts, ts), 0)
    k = lax.broadcasted_iota(jnp.int32, (ts, ts), 1)
    per = ts // d
    src = (i % d) * per + i // d if inverse else (i % per) * d + i // per
    return jnp.where(k == src, 1.0, 0.0).astype(BF16)


def _permute(p, x):
    if x.dtype == BF16:
        return jnp.dot(p, x, preferred_element_type=F32)
    hi = x.astype(BF16)
    rest = x - hi.astype(F32)
    mid = rest.astype(BF16)
    lo = (rest - mid.astype(F32)).astype(BF16)
    out = jnp.dot(p, hi, preferred_element_type=F32) + jnp.dot(p, mid, preferred_element_type=F32)
    return out + jnp.dot(p, lo, preferred_element_type=F32)


def _rms(x, gain):
    r = lax.rsqrt(jnp.mean(x * x, axis=-1, keepdims=True) + EPS)
    n = x * r
    return n * gain, n, r


def _rms_bwd(dy, n, r, gain):
    dn = dy * gain
    dx = r * (dn - n * jnp.mean(dn * n, axis=-1, keepdims=True))
    return dx, jnp.sum(dy * n, axis=0, keepdims=True)


def _sigmoid(x):
    return 1.0 / (1.0 + jnp.exp(-x))


_GELU_K = math.sqrt(2.0 / math.pi)


def _gelu(x):
    t = jnp.tanh(_GELU_K * (x + 0.044715 * x * x * x))
    return 0.5 * x * (1.0 + t), t


def _gelu_grad(x, t):
    return 0.5 * (1.0 + t) + 0.5 * x * (1.0 - t * t) * _GELU_K * (1.0 + 3.0 * 0.044715 * x * x)


def _head_sum(x):
    parts = []
    for h in range(HEADS_PER_GROUP):
        sl = x[:, h * HEAD_DIM:(h + 1) * HEAD_DIM]
        parts.append(jnp.broadcast_to(jnp.sum(sl, axis=-1, keepdims=True), sl.shape))
    return jnp.concatenate(parts, axis=-1)


def _mix_weights(l0, l1, l2):
    mx = jnp.maximum(jnp.maximum(l0, l1), l2)
    e0, e1, e2 = jnp.exp(l0 - mx), jnp.exp(l1 - mx), jnp.exp(l2 - mx)
    inv = 1.0 / (e0 + e1 + e2)
    return e0 * inv, e1 * inv, e2 * inv


BLK = 128


def _slopes(g):
    return [2.0 ** (-8.0 * (g * HEADS_PER_GROUP + h + 1) / N_HEADS) for h in range(HEADS_PER_GROUP)]


def _attn_masks(dil):
    qi = lax.broadcasted_iota(jnp.int32, (BLK, BLK), 0)
    ki = lax.broadcasted_iota(jnp.int32, (BLK, BLK), 1)
    dist_c = qi - ki
    dist_p = BLK + qi - ki
    return dist_c >= 0, dist_p <= BLK, (dist_c * dil).astype(F32), (dist_p * dil).astype(F32)


def _window_mask(has_prev, dil):
    qi = lax.broadcasted_iota(jnp.int32, (BLK, 2 * BLK), 0)
    ki = lax.broadcasted_iota(jnp.int32, (BLK, 2 * BLK), 1)
    dist = BLK + qi - ki
    ok = jnp.logical_and(jnp.logical_and(dist >= 0, dist <= BLK), jnp.logical_or(ki >= BLK, has_prev))
    return ok, (dist * dil).astype(F32)


def attn_fwd(qkv, g, name):
    dil, length, _ = qkv.shape
    scale = HEAD_DIM ** -0.5
    slopes = _slopes(g)

    def body(q_ref, kc_ref, vc_ref, kp_ref, vp_ref, o_ref, l_ref):
        ok, dist = _window_mask(pl.program_id(1) > 0, dil)
        for h in range(HEADS_PER_GROUP):
            sl = slice(h * HEAD_DIM, (h + 1) * HEAD_DIM)
            k2 = jnp.concatenate([kp_ref[:, sl], kc_ref[:, sl]], axis=0)
            v2 = jnp.concatenate([vp_ref[:, sl], vc_ref[:, sl]], axis=0)
            s = lax.dot_general(q_ref[:, sl], k2, _DN["nt"], preferred_element_type=F32) * scale - slopes[h] * dist
            s = jnp.where(ok, s, NEG)
            mx = jnp.max(s, axis=-1, keepdims=True)
            p = jnp.exp(s - mx)
            den = jnp.sum(p, axis=-1, keepdims=True)
            o_ref[:, sl] = jnp.dot(p.astype(BF16), v2, preferred_element_type=F32) / den
            l_ref[:, sl] = jnp.broadcast_to(mx + jnp.log(den), (BLK, HEAD_DIM))

    def spec(col, prev):
        if prev:
            return pl.BlockSpec((None, BLK, GROUP_W), lambda r, n: (r, jnp.maximum(n - 1, 0), col))
        return pl.BlockSpec((None, BLK, GROUP_W), lambda r, n: (r, n, col))

    out_spec = pl.BlockSpec((None, BLK, GROUP_W), lambda r, n: (r, n, 0))
    return _pcall(
        body, name=name, grid=(dil, length // BLK),
        in_specs=[spec(0, False), spec(1, False), spec(2, False), spec(1, True), spec(2, True)],
        out_specs=[out_spec, out_spec],
        out_shape=[jax.ShapeDtypeStruct((dil, length, GROUP_W), F32)] * 2,
        compiler_params=_cparams(("parallel", "parallel")),
    )(qkv, qkv, qkv, qkv, qkv)


def attn_bwd(qkv, dout, lse, dd, g, name, carry=None):
    dil, length, _ = qkv.shape
    nblk = length // BLK
    scale = HEAD_DIM ** -0.5
    slopes = _slopes(g)

    def body(q_ref, kc_ref, vc_ref, kp_ref, vp_ref, qn_ref, do_ref, don_ref, l_ref, ln_ref, d_ref, dn_ref, o_ref):
        n = pl.program_id(1)
        ok2, dist2 = _window_mask(n > 0, dil)
        _, ok_p, _, dp = _attn_masks(dil)
        ok_next = jnp.logical_and(ok_p, n < nblk - 1)
        for h in range(HEADS_PER_GROUP):
            sl = slice(h * HEAD_DIM, (h + 1) * HEAD_DIM)
            q, kc, vc, qn = q_ref[:, sl], kc_ref[:, sl], vc_ref[:, sl], qn_ref[:, sl]
            k2 = jnp.concatenate([kp_ref[:, sl], kc], axis=0)
            v2 = jnp.concatenate([vp_ref[:, sl], vc], axis=0)
            do, don = do_ref[:, sl], don_ref[:, sl]
            lse_q, lse_n, dd_q, dd_n = l_ref[:, sl], ln_ref[:, sl], d_ref[:, sl], dn_ref[:, sl]

            def probs(qq, kk, dist, ok, lse_t):
                s = lax.dot_general(qq, kk, _DN["nt"], preferred_element_type=F32) * scale - slopes[h] * dist
                return jnp.where(ok, jnp.exp(jnp.where(ok, s, NEG) - lse_t), 0.0)

            p2 = probs(q, k2, dist2, ok2, jnp.concatenate([lse_q, lse_q], axis=1))
            p_x = probs(qn, kc, dp, ok_next, lse_n)
            ds2 = p2 * (lax.dot_general(do, v2, _DN["nt"], preferred_element_type=F32) - jnp.concatenate([dd_q, dd_q], axis=1))
            ds_x = p_x * (lax.dot_general(don, vc, _DN["nt"], preferred_element_type=F32) - dd_n)
            dq = jnp.dot(ds2.astype(BF16), k2, preferred_element_type=F32)
            ds_k = jnp.concatenate([ds2[:, BLK:], ds_x], axis=0).astype(BF16)
            p_k = jnp.concatenate([p2[:, BLK:], p_x], axis=0).astype(BF16)
            dk = lax.dot_general(ds_k, jnp.concatenate([q, qn], axis=0), _DN["tn"], preferred_element_type=F32)
            dv = lax.dot_general(p_k, jnp.concatenate([do, don], axis=0), _DN["tn"], preferred_element_type=F32)
            o_ref[:, h * HEAD_DIM:(h + 1) * HEAD_DIM] = (dq * scale).astype(BF16)
            o_ref[:, GROUP_W + h * HEAD_DIM:GROUP_W + (h + 1) * HEAD_DIM] = (dk * scale).astype(BF16)
            o_ref[:, 2 * GROUP_W + h * HEAD_DIM:2 * GROUP_W + (h + 1) * HEAD_DIM] = dv.astype(BF16)

    def spec(col, which):
        if which == "prev":
            return pl.BlockSpec((None, BLK, GROUP_W), lambda r, n: (r, jnp.maximum(n - 1, 0), col))
        if which == "next":
            return pl.BlockSpec((None, BLK, GROUP_W), lambda r, n: (r, jnp.minimum(n + 1, nblk - 1), col))
        return pl.BlockSpec((None, BLK, GROUP_W), lambda r, n: (r, n, col))

    return _run(
        body, [qkv, qkv, qkv, qkv, qkv, qkv, dout, dout, lse, lse, dd, dd], carry=carry, name=name, grid=(dil, nblk),
        in_specs=[spec(0, "cur"), spec(1, "cur"), spec(2, "cur"), spec(1, "prev"), spec(2, "prev"), spec(0, "next"),
                  spec(0, "cur"), spec(0, "next"), spec(0, "cur"), spec(0, "next"), spec(0, "cur"), spec(0, "next")],
        out_specs=pl.BlockSpec((None, BLK, 3 * GROUP_W), lambda r, n: (r, n, 0)),
        out_shape=jax.ShapeDtypeStruct((dil, length, 3 * GROUP_W), BF16),
        compiler_params=_cparams(("parallel", "parallel")),
    )


def _ssm_prep_values(are, aim, logdt):
    dt = jnp.exp(logdt)
    mag = jnp.exp(are * dt)
    lb_re, lb_im = mag * jnp.cos(aim * dt), mag * jnp.sin(aim * dt)
    inv = 1.0 / (are * are + aim * aim)
    n_re, n_im = lb_re - 1.0, lb_im
    f_re = (n_re * are + n_im * aim) * inv
    f_im = (n_im * are - n_re * aim) * inv
    return dt, lb_re, lb_im, f_re, f_im, inv


PREP_G = 8


def _group_specs(are, logdt, bre):
    def spec(a):
        return pl.BlockSpec((PREP_G,) + a.shape[1:], lambda i: (i, 0, 0))
    return spec(are), spec(logdt), spec(bre)


def ssm_prep(are, aim, logdt, bre, bim):
    def body(are_r, aim_r, ldt_r, bre_r, bim_r, lre_o, lim_o, bbre_o, bbim_o):
        _, lb_re, lb_im, f_re, f_im, _ = _ssm_prep_values(are_r[...], aim_r[...], ldt_r[...])
        lre_o[...] = lb_re
        lim_o[...] = lb_im
        bbre_o[...] = f_re * bre_r[...] - f_im * bim_r[...]
        bbim_o[...] = f_re * bim_r[...] + f_im * bre_r[...]

    sh1 = jax.ShapeDtypeStruct(are.shape, F32)
    shb = jax.ShapeDtypeStruct(bre.shape, F32)
    s1, sd, sb = _group_specs(are, logdt, bre)
    return _pcall(body, name="ssm_prep", grid=(SSM_GROUPS // PREP_G,), in_specs=[s1, s1, sd, sb, sb], out_specs=[s1, s1, sb, sb],
                  out_shape=[sh1, sh1, shb, shb], compiler_params=_cparams(("parallel",)))(are, aim, logdt, bre, bim)


def ssm_prep_bwd(are, aim, logdt, bre, bim, dbbre, dbbim, dlre, dlim):
    def body(are_r, aim_r, ldt_r, bre_r, bim_r, dbbre_r, dbbim_r, dlre_r, dlim_r, dare_o, daim_o, dldt_o, dbre_o, dbim_o):
        are_v, aim_v = are_r[...], aim_r[...]
        dt, lb_re, lb_im, f_re, f_im, inv = _ssm_prep_values(are_v, aim_v, ldt_r[...])
        b_re, b_im, g_re, g_im = bre_r[...], bim_r[...], dbbre_r[...], dbbim_r[...]
        dbre_o[...] = f_re * g_re + f_im * g_im
        dbim_o[...] = f_re * g_im - f_im * g_re
        df_re = jnp.sum(b_re * g_re + b_im * g_im, axis=-1, keepdims=True)
        df_im = jnp.sum(b_re * g_im - b_im * g_re, axis=-1, keepdims=True)
        il_re, il_im = are_v * inv, -aim_v * inv
        cl_re = dlre_r[...] + il_re * df_re + il_im * df_im
        cl_im = dlim_r[...] + il_re * df_im - il_im * df_re
        q_re = -(f_re * il_re - f_im * il_im)
        q_im = -(f_re * il_im + f_im * il_re)
        ca_re = q_re * df_re + q_im * df_im
        ca_im = q_re * df_im - q_im * df_re
        cz_re = lb_re * cl_re + lb_im * cl_im
        cz_im = lb_re * cl_im - lb_im * cl_re
        dare_o[...] = ca_re + dt * cz_re
        daim_o[...] = ca_im + dt * cz_im
        dldt_o[...] = dt * jnp.sum(are_v * cz_re + aim_v * cz_im, axis=1, keepdims=True)

    sh1 = jax.ShapeDtypeStruct(are.shape, F32)
    shb = jax.ShapeDtypeStruct(bre.shape, F32)
    s1, sd, sb = _group_specs(are, logdt, bre)
    return _pcall(
        body, name="ssm_prep_bwd", grid=(SSM_GROUPS // PREP_G,), in_specs=[s1, s1, sd, sb, sb, sb, sb, s1, s1],
        out_specs=[s1, s1, sd, sb, sb], out_shape=[sh1, sh1, jax.ShapeDtypeStruct(logdt.shape, F32), shb, shb],
        compiler_params=_cparams(("parallel",)),
    )(are, aim, logdt, bre, bim, dbbre, dbbim, dlre, dlim)


SCAN_WC = 512


def _chain_segments(a_re, a_im, e_re, e_im, nsq, reverse):
    p_re, p_im = a_re, a_im
    for _ in range(nsq):
        p_re, p_im = p_re * p_re - p_im * p_im, 2.0 * p_re * p_im
    row = lax.broadcasted_iota(jnp.int32, e_re.shape, 0)
    edge = (row == SEGS - 1) if reverse else (row == 0)
    shift = SEGS - 1 if reverse else 1
    c_re, c_im = jnp.zeros_like(e_re), jnp.zeros_like(e_im)
    for _ in range(SEGS - 1):
        n_re = p_re * c_re - p_im * c_im + e_re
        n_im = p_re * c_im + p_im * c_re + e_im
        c_re = jnp.where(edge, 0.0, pltpu.roll(n_re, shift, 0))
        c_im = jnp.where(edge, 0.0, pltpu.roll(n_im, shift, 0))
    return c_re, c_im


def _scan_dims(s):
    steps = s // SEGS
    assert steps & (steps - 1) == 0
    tt = min(128, steps)
    return steps, tt, steps // tt, tt * SEGS, int(math.log2(steps))


U_BLK = SSM_W // BD


def ssm_fwd(u_s, dvec, w_bre, w_bim, w_cre, w_cim_neg, lre, lim, name, carry=None):
    s = u_s.shape[0]
    steps, tt, nch, rows, nsq = _scan_dims(s)
    nb, ub_w, wc = w_bre.shape

    def body(u_r, d_r, bre_r, bim_r, cre_r, cim_r, lre_r, lim_r, yg_o, ys_o, hre_o, him_o, hin_re_o, hin_im_o,
             st_re, st_im, x_re, x_im, h_re, h_im):
        ps, ch = pl.program_id(1), pl.program_id(2)
        a_re = jnp.broadcast_to(lre_r[...], (SEGS, wc))
        a_im = jnp.broadcast_to(lim_r[...], (SEGS, wc))
        ub = u_r[...]
        ub16 = ub.astype(BF16)
        x_re[...] = jnp.dot(ub16, bre_r[...], preferred_element_type=F32)
        x_im[...] = jnp.dot(ub16, bim_r[...], preferred_element_type=F32)

        @pl.when(jnp.logical_and(ps == 0, ch == 0))
        def _():
            st_re[...] = jnp.zeros_like(st_re)
            st_im[...] = jnp.zeros_like(st_im)

        @pl.when(jnp.logical_and(ps == 1, ch == 0))
        def _():
            c_re, c_im = _chain_segments(a_re, a_im, st_re[...], st_im[...], nsq, False)
            st_re[...] = c_re
            st_im[...] = c_im
            hin_re_o[...] = c_re
            hin_im_o[...] = c_im

        def run(store):
            def step(t, hc):
                off = pl.multiple_of(t * SEGS, SEGS)
                n_re = a_re * hc[0] - a_im * hc[1] + x_re[pl.ds(off, SEGS), :]
                n_im = a_re * hc[1] + a_im * hc[0] + x_im[pl.ds(off, SEGS), :]
                if store:
                    h_re[pl.ds(off, SEGS), :] = n_re
                    h_im[pl.ds(off, SEGS), :] = n_im
                return n_re, n_im

            fin = lax.fori_loop(0, tt, step, (st_re[...], st_im[...]))
            st_re[...] = fin[0]
            st_im[...] = fin[1]

        @pl.when(ps == 0)
        def _():
            run(False)

        @pl.when(ps == 1)
        def _():
            run(True)
            hr16, hi16 = h_re[...].astype(BF16), h_im[...].astype(BF16)
            hre_o[...] = hr16
            him_o[...] = hi16
            y = jnp.dot(hr16, cre_r[...], preferred_element_type=F32) + jnp.dot(hi16, cim_r[...], preferred_element_type=F32)
            y = y + d_r[...] * ub
            ys_o[...] = y
            yg_o[...] = _gelu(y)[0].astype(BF16)

    def pass1(ps, c):
        return jnp.where(ps == 1, c, 0)

    u_spec = pl.BlockSpec((rows, ub_w), lambda j, ps, c: (c, j))
    d_spec = pl.BlockSpec((1, ub_w), lambda j, ps, c: (0, j))
    b_spec = pl.BlockSpec((None, ub_w, wc), lambda j, ps, c: (j, 0, 0))
    c_spec = pl.BlockSpec((None, wc, ub_w), lambda j, ps, c: (j, 0, 0))
    l_spec = pl.BlockSpec((1, wc), lambda j, ps, c: (0, j))
    y_spec = pl.BlockSpec((rows, ub_w), lambda j, ps, c: (pass1(ps, c), j))
    h_spec = pl.BlockSpec((rows, wc), lambda j, ps, c: (pass1(ps, c), j))
    e_spec = pl.BlockSpec((SEGS, wc), lambda j, ps, c: (0, j))
    return _run(
        body, [u_s, dvec, w_bre, w_bim, w_cre, w_cim_neg, lre, lim], carry=carry, name=name, grid=(nb, 2, nch),
        in_specs=[u_spec, d_spec, b_spec, b_spec, c_spec, c_spec, l_spec, l_spec],
        out_specs=[y_spec, y_spec, h_spec, h_spec, e_spec, e_spec],
        out_shape=[jax.ShapeDtypeStruct((s, SSM_W), BF16), jax.ShapeDtypeStruct((s, SSM_W), F32),
                   jax.ShapeDtypeStruct((s, STATE_W), BF16), jax.ShapeDtypeStruct((s, STATE_W), BF16),
                   jax.ShapeDtypeStruct((SEGS, STATE_W), F32), jax.ShapeDtypeStruct((SEGS, STATE_W), F32)],
        scratch_shapes=[pltpu.VMEM((SEGS, wc), F32)] * 2 + [pltpu.VMEM((rows, wc), F32)] * 4,
        compiler_params=_cparams(("parallel", "arbitrary", "arbitrary")),
    )


def ssm_bwd(dyg_s, ys, u_s, h_re, h_im, hin_re, hin_im, dvec, w_bre_t, w_bim_t, w_cre_t, w_cim_neg_t, lre, lim, name, carry=None):
    s = u_s.shape[0]
    steps, tt, nch, rows, nsq = _scan_dims(s)
    half = 2 * SEGS

    def body(dyg_r, ys_r, u_r, hre_r, him_r, pre_r, pim_r, cin_re_r, cin_im_r, d_r, bre_r, bim_r, cre_r, cim_r, lre_r, lim_r,
             du_o, dbre_o, dbim_o, dcre_o, dcim_o, dlre_o, dlim_o, dd_o, st_re, st_im, x_re, x_im, g_re, g_im, hf_re, hf_im):
        ps, ch = pl.program_id(1), pl.program_id(2)
        a_re = jnp.broadcast_to(lre_r[...], (SEGS, SCAN_WC))
        a_im = -jnp.broadcast_to(lim_r[...], (SEGS, SCAN_WC))
        ub, y = u_r[...], ys_r[...]
        dy = dyg_r[...] * _gelu_grad(y, _gelu(y)[1])
        dy16 = dy.astype(BF16)
        x_re[...] = jnp.dot(dy16, cre_r[...], preferred_element_type=F32)
        x_im[...] = jnp.dot(dy16, cim_r[...], preferred_element_type=F32)

        @pl.when(jnp.logical_and(ps == 0, ch == 0))
        def _():
            st_re[...] = jnp.zeros_like(st_re)
            st_im[...] = jnp.zeros_like(st_im)

        @pl.when(jnp.logical_and(ps == 1, ch == 0))
        def _():
            c_re, c_im = _chain_segments(a_re, a_im, st_re[...], st_im[...], nsq, True)
            st_re[...] = c_re
            st_im[...] = c_im
            dlre_o[...] = jnp.zeros_like(dlre_o)
            dlim_o[...] = jnp.zeros_like(dlim_o)

        @pl.when(ps == 0)
        def _():
            def step(i, hc):
                off = pl.multiple_of((tt - 1 - i) * SEGS, SEGS)
                return (a_re * hc[0] - a_im * hc[1] + x_re[pl.ds(off, SEGS), :],
                        a_re * hc[1] + a_im * hc[0] + x_im[pl.ds(off, SEGS), :])

            fin = lax.fori_loop(0, tt, step, (st_re[...], st_im[...]))
            st_re[...] = fin[0]
            st_im[...] = fin[1]

        @pl.when(ps == 1)
        def _():
            hf_re[...] = hre_r[...].astype(F32)
            hf_im[...] = him_r[...].astype(F32)
            first_chunk = ch == nch - 1
            edge_re = jnp.where(first_chunk, cin_re_r[...], pre_r[...].astype(F32)[SEGS:, :])
            edge_im = jnp.where(first_chunk, cin_im_r[...], pim_r[...].astype(F32)[SEGS:, :])

            def step(i, hc):
                t = tt - 1 - i
                off = pl.multiple_of(t * SEGS, SEGS)
                n_re = a_re * hc[0] - a_im * hc[1] + x_re[pl.ds(off, SEGS), :]
                n_im = a_re * hc[1] + a_im * hc[0] + x_im[pl.ds(off, SEGS), :]
                g_re[pl.ds(off, SEGS), :] = n_re
                g_im[pl.ds(off, SEGS), :] = n_im
                offp = pl.multiple_of(jnp.maximum(t - 1, 0) * SEGS, SEGS)
                hp_re = jnp.where(t == 0, edge_re, hf_re[pl.ds(offp, SEGS), :])
                hp_im = jnp.where(t == 0, edge_im, hf_im[pl.ds(offp, SEGS), :])
                return n_re, n_im, hc[2] + hp_re * n_re + hp_im * n_im, hc[3] + hp_re * n_im - hp_im * n_re

            fin = lax.fori_loop(0, tt, step, (st_re[...], st_im[...], dlre_o[...], dlim_o[...]))
            st_re[...] = fin[0]
            st_im[...] = fin[1]
            dlre_o[...] = fin[2]
            dlim_o[...] = fin[3]

            gr16, gi16 = g_re[...].astype(BF16), g_im[...].astype(BF16)
            du = jnp.dot(gr16, bre_r[...], preferred_element_type=F32) + jnp.dot(gi16, bim_r[...], preferred_element_type=F32)
            du_o[...] = du + d_r[...] * dy
            ub16 = ub.astype(BF16)
            parts = [
                (dbre_o, lax.dot_general(ub16, gr16, _DN["tn"], preferred_element_type=F32)),
                (dbim_o, lax.dot_general(ub16, gi16, _DN["tn"], preferred_element_type=F32)),
                (dcre_o, lax.dot_general(hre_r[...], dy16, _DN["tn"], preferred_element_type=F32)),
                (dcim_o, lax.dot_general(him_r[...], dy16, _DN["tn"], preferred_element_type=F32)),
                (dd_o, jnp.sum(dy * ub, axis=0, keepdims=True)),
            ]
            for ref, val in parts:
                @pl.when(ch == 0)
                def _(ref=ref, val=val):
                    ref[...] = val

                @pl.when(ch > 0)
                def _(ref=ref, val=val):
                    ref[...] += val

    def chunk(c):
        return nch - 1 - c

    def pass1(ps, c):
        return jnp.where(ps == 1, chunk(c), chunk(0))

    u_spec = pl.BlockSpec((rows, U_BLK), lambda j, ps, c: (chunk(c), j))
    h_spec = pl.BlockSpec((rows, SCAN_WC), lambda j, ps, c: (pass1(ps, c), j))
    prev_spec = pl.BlockSpec((half, SCAN_WC), lambda j, ps, c: (jnp.maximum(pass1(ps, c) * (rows // half) - 1, 0), j))
    e_spec = pl.BlockSpec((SEGS, SCAN_WC), lambda j, ps, c: (0, j))
    d_spec = pl.BlockSpec((1, U_BLK), lambda j, ps, c: (0, j))
    bt_spec = pl.BlockSpec((None, SCAN_WC, U_BLK), lambda j, ps, c: (j, 0, 0))
    ct_spec = pl.BlockSpec((None, U_BLK, SCAN_WC), lambda j, ps, c: (j, 0, 0))
    l_spec = pl.BlockSpec((1, SCAN_WC), lambda j, ps, c: (0, j))
    du_spec = pl.BlockSpec((rows, U_BLK), lambda j, ps, c: (pass1(ps, c), j))
    return _run(
        body, [dyg_s, ys, u_s, h_re, h_im, h_re, h_im, hin_re, hin_im, dvec, w_bre_t, w_bim_t, w_cre_t, w_cim_neg_t, lre, lim],
        carry=carry, name=name, grid=(BD, 2, nch),
        in_specs=[u_spec, u_spec, u_spec, h_spec, h_spec, prev_spec, prev_spec, e_spec, e_spec, d_spec, bt_spec, bt_spec,
                  ct_spec, ct_spec, l_spec, l_spec],
        out_specs=[du_spec, ct_spec, ct_spec, bt_spec, bt_spec, e_spec, e_spec, d_spec],
        out_shape=[jax.ShapeDtypeStruct((s, SSM_W), F32)] + [jax.ShapeDtypeStruct((BD, U_BLK, SCAN_WC), F32)] * 2
        + [jax.ShapeDtypeStruct((BD, SCAN_WC, U_BLK), F32)] * 2 + [jax.ShapeDtypeStruct((SEGS, STATE_W), F32)] * 2
        + [jax.ShapeDtypeStruct((1, SSM_W), F32)],
        scratch_shapes=[pltpu.VMEM((SEGS, SCAN_WC), F32)] * 2 + [pltpu.VMEM((rows, SCAN_WC), F32)] * 6,
        compiler_params=_cparams(("parallel", "arbitrary", "arbitrary")),
    )


FWD_BD = 4


def _block_diag(m, nb=BD):
    g, r, c = m.shape
    m = m.reshape(nb, g // nb, r, c)
    eye = jnp.eye(g // nb, dtype=m.dtype)
    return jnp.einsum("jarc,ab->jarbc", m, eye).reshape(nb, (g // nb) * r, (g // nb) * c)


def _block_diag_extract(m, r, c):
    per = m.shape[1] // r
    m = m.reshape(BD, per, r, per, c)
    return jnp.einsum("jarac->jarc", m).reshape(BD * per, r, c)


def to_segments(a):
    s, w = a.shape
    return a.reshape(SEGS, s // SEGS, w).transpose(1, 0, 2).reshape(s, w)


def from_segments(a):
    s, w = a.shape
    return a.reshape(s // SEGS, SEGS, w).transpose(1, 0, 2).reshape(s, w)


W_IN_CHUNK_ROWS = (512, 512, 512, 512)


def _row_chunks(blocks, sizes):
    assert sum(sizes) == blocks.shape[1]
    out, at = [], 0
    for n in sizes:
        out.append(AllToAll([blocks[:, at:at + n]]))
        at += n
    return out
FFN_TN = 512


def local_step(x, target, shards, small):
    s = x.shape[0]
    g1, g2, g3, g4 = (small[k].reshape(1, D_MODEL) for k in ("norm_mix_pre", "norm_mix_post", "norm_ffn_pre", "norm_ffn_post"))
    dvec = small["ssm_d"].reshape(1, SSM_W)
    wts, recv = {}, {}

    def gathered(names, blocks):
        for n, b in zip(names, blocks):
            wts[n] = _full_from_gathered(b, n)

    def rms_in_fn(r, c):
        hh = _rms(r[0], c[0])[0].astype(BF16)
        return [hh, _permute(_perm_matrix(PERM_TS, 4, False), hh), _permute(_perm_matrix(PERM_TS, 16, False), hh)], []

    (h, h4, h16), got = rowwise("rms_in", rms_in_fn, [x], [g1], [(D_MODEL, BF16), (D_MODEL, BF16, 4), (D_MODEL, BF16, 16)],
                                ts=PERM_TS, carry=Gather([shards["w_in"]]))
    w_in_t = _full_from_gathered(got[0], "w_in")
    w_u_t, w_gates_t = w_in_t[3 * HQ:3 * HQ + SSM_W], w_in_t[3 * HQ + SSM_W:]

    def qkv_rows(g):
        return 3 * GROUP_W, lambda t: 3 * t + g

    hd = [h.reshape(1, s, D_MODEL), h4, h16]
    qkv = [None] * 3
    names = ("w_attn_up", "w_glu_v", "w_glu_g")
    qkv[0], got = mm([(hd[0].reshape(s, D_MODEL), w_in_t)], "nt", BF16, "mm_qkv0", tn=GROUP_W, b_window=qkv_rows(0),
                     carry=Gather([shards[n] for n in names]))
    gathered(names, got)
    qkv[1], got = mm([(hd[1].reshape(s, D_MODEL), w_in_t)], "nt", BF16, "mm_qkv1", tn=GROUP_W, b_window=qkv_rows(1),
                     carry=Gather([shards["w_out"]]))
    gathered(("w_out",), got)
    qkv[2] = mm([(hd[2].reshape(s, D_MODEL), w_in_t)], "nt", BF16, "mm_qkv2", tn=GROUP_W, b_window=qkv_rows(2))
    u = mm([(h, w_u_t)], "nt", F32, "mm_u")
    gates, got = mm([(h, w_gates_t)], "nt", BF16, "mm_gates", carry=Gather([shards["w_ffn_gate"]]))
    gathered(("w_ffn_gate",), got)

    outs, lses = [], []
    for g, (_, dil) in enumerate(ATTN_GROUPS):
        o, l = attn_fwd(qkv[g].reshape(dil, s // dil, 3 * GROUP_W), g, f"attn_fwd{g}")
        outs.append(o.reshape(s, GROUP_W) if dil == 1 else o)
        lses.append(l.reshape(s, GROUP_W) if dil == 1 else l)

    def natural(r):
        back4, back16 = _perm_matrix(PERM_TS, 4, True), _perm_matrix(PERM_TS, 16, True)
        return r[0], _permute(back4, r[1]), _permute(back16, r[2]), r[3], _permute(back4, r[4]), _permute(back16, r[5])

    def merge_fn(r, c):
        o0, o1, o2, l0, l1, l2 = natural(r)
        w0, w1, w2 = _mix_weights(l0, l1, l2)
        return [w0 * o0 + w1 * o1 + w2 * o2], []

    (attn,) = rowwise("attn_merge", merge_fn, outs + lses, [], [(GROUP_W, BF16)], ts=PERM_TS)
    attn_branch = mm([(attn, wts["w_attn_up"])], "nn", BF16, "mm_up")

    are3 = small["ssm_a_re"].reshape(SSM_GROUPS, SSM_STATE, 1)
    aim3 = small["ssm_a_im"].reshape(SSM_GROUPS, SSM_STATE, 1)
    ldt3 = small["ssm_log_dt"].reshape(SSM_GROUPS, 1, 1)
    bre3 = small["ssm_b_re"].reshape(SSM_GROUPS, SSM_STATE, SSM_GROUP)
    bim3 = small["ssm_b_im"].reshape(SSM_GROUPS, SSM_STATE, SSM_GROUP)
    cre3 = small["ssm_c_re"].reshape(SSM_GROUPS, SSM_GROUP, SSM_STATE)
    cim3 = small["ssm_c_im"].reshape(SSM_GROUPS, SSM_GROUP, SSM_STATE)
    lre3, lim3, bbre, bbim = ssm_prep(are3, aim3, ldt3, bre3, bim3)
    lre, lim = lre3.reshape(1, STATE_W), lim3.reshape(1, STATE_W)
    w_bre = _block_diag(bbre.transpose(0, 2, 1)).astype(BF16)
    w_bim = _block_diag(bbim.transpose(0, 2, 1)).astype(BF16)
    w_cre = _block_diag(cre3.transpose(0, 2, 1)).astype(BF16)
    w_cim = _block_diag(cim3.transpose(0, 2, 1)).astype(BF16)
    u_s = to_segments(u)
    names = ("w_ffn_up", "w_ffn_down")
    fwd_w = [_block_diag(t.transpose(0, 2, 1), FWD_BD).astype(BF16) for t in (bbre, bbim, cre3, -cim3)]
    (yg_s, y_ssm, h_re, h_im, hin_re, hin_im), got = ssm_fwd(
        u_s, dvec, *fwd_w, lre, lim, "ssm_fwd", carry=Gather([shards[n] for n in names]))
    gathered(names, got)
    yg = from_segments(yg_s)
    gv = mm([(yg, wts["w_glu_v"])], "nn", BF16, "mm_glu_v")
    gg = mm([(yg, wts["w_glu_g"])], "nn", BF16, "mm_glu_g")

    def gate_fn(r, c):
        gts, ab, gv_, gg_ = r
        sa, ss = _sigmoid(gts[:, :D_MODEL]), _sigmoid(gts[:, D_MODEL:])
        return [sa * ab + ss * (gv_ * _sigmoid(gg_))], []

    (merged,) = rowwise("gate_merge", gate_fn, [gates, attn_branch, gv, gg], [], [(D_MODEL, BF16)])
    o_mix = mm([(merged, wts["w_out"])], "nn", F32, "mm_out")

    def mid_fn(r, c):
        x1 = r[0] + _rms(r[1], c[0])[0]
        return [x1, _rms(x1, c[1])[0]], []

    x1, h2 = rowwise("rms_mid", mid_fn, [x, o_mix], [g2, g3], [(D_MODEL, F32), (D_MODEL, BF16)])
    fa, fb, fin = mm([(h2, wts["w_ffn_gate"]), (h2, wts["w_ffn_up"])], "nt", [BF16, BF16, BF16], "mm_ffn_in", tn=FFN_TN,
                     epilogue=lambda p, e: [p[0], p[1], p[0] * _sigmoid(p[0]) * p[1]])
    f = mm([(fin, wts["w_ffn_down"])], "nn", F32, "mm_ffn_down", tn=512, tk=D_FF)

    def loss_fn(r, c):
        x1_, f_, tgt = r
        y, n, rr = _rms(f_, c[0])
        err = x1_ + y - tgt
        dout = err * (1.0 / D_MODEL)
        df, dg = _rms_bwd(dout, n, rr, c[0])
        lp = 0.5 * jnp.sum(jnp.sum(err * err, axis=-1, keepdims=True) * (1.0 / D_MODEL), axis=0, keepdims=True)
        return [df, dout], [dg, lp]

    df, dout, dg4, loss_part = rowwise("loss_bwd", loss_fn, [x1, f, target], [g4], [(D_MODEL, BF16), (D_MODEL, F32)],
                                       acc_outs=[(1, D_MODEL), (1, 1)])
    def sent(names, blocks):
        for n, b in zip(names, blocks):
            recv[n] = b

    def to_owners(names, dws):
        return AllToAll([_split_for_devices(d, n) for n, d in zip(names, dws)])

    def swiglu_bwd(p, e):
        dfin_, (a, b) = p[0], e
        sg = _sigmoid(a)
        return [dfin_ * b * (sg * (1.0 + a * (1.0 - sg))), dfin_ * a * sg]

    da, db = mm([(df, wts["w_ffn_down"])], "nt", [BF16, BF16], "mm_d_fin", tn=FFN_TN, epilogue=swiglu_bwd, extras=[fa, fb])
    dw_ffn_down = mm([(fin, df)], "tn", BF16, "mm_dw_ffn_down")
    dh2, got = mm([(da, wts["w_ffn_gate"]), (db, wts["w_ffn_up"])], "nn", F32, "mm_d_h2", tm=512, tn=1024, tk=D_FF // 2,
                  carry=to_owners(["w_ffn_down"], [dw_ffn_down]))
    sent(["w_ffn_down"], got)
    dw_ffn_gate = mm([(da, h2)], "tn", BF16, "mm_dw_ffn_gate")
    dw_ffn_up, got = mm([(db, h2)], "tn", BF16, "mm_dw_ffn_up", carry=to_owners(["w_ffn_gate"], [dw_ffn_gate]))
    sent(["w_ffn_gate"], got)

    def mid_bwd(r, c):
        dh2_, dout_, x1_, o_ = r
        _, n3, r3 = _rms(x1_, c[1])
        dx1, dg3_ = _rms_bwd(dh2_, n3, r3, c[1])
        dx1 = dx1 + dout_
        _, n2, r2 = _rms(o_, c[0])
        do_, dg2_ = _rms_bwd(dx1, n2, r2, c[0])
        return [dx1, do_], [dg2_, dg3_]

    dx1, do_mix, dg2, dg3 = rowwise("rms_mid_bwd", mid_bwd, [dh2, dout, x1, o_mix], [g2, g3], [(D_MODEL, F32), (D_MODEL, BF16)],
                                    acc_outs=[(1, D_MODEL), (1, D_MODEL)])
    dmerged = mm([(do_mix, wts["w_out"])], "nt", BF16, "mm_d_merged")
    dw_out = mm([(merged, do_mix)], "tn", BF16, "mm_dw_out")

    def gate_bwd(r, c):
        dm, gts, ab, gv_, gg_ = r
        sa, ss, sg = _sigmoid(gts[:, :D_MODEL]), _sigmoid(gts[:, D_MODEL:]), _sigmoid(gg_)
        branch = gv_ * sg
        dbranch = dm * ss
        dgates = jnp.concatenate([dm * ab * sa * (1.0 - sa), dm * branch * ss * (1.0 - ss)], axis=-1)
        return [dgates, dm * sa, dbranch * sg, dbranch * gv_ * sg * (1.0 - sg)], []

    dgates, dab, dgv, dgg = rowwise("gate_bwd", gate_bwd, [dmerged, gates, attn_branch, gv, gg], [],
                                    [(2 * D_MODEL, BF16), (D_MODEL, BF16), (D_MODEL, BF16), (D_MODEL, BF16)])
    dattn = mm([(dab, wts["w_attn_up"])], "nt", F32, "mm_d_attn")
    dw_up = mm([(attn, dab)], "tn", BF16, "mm_dw_up")
    dyg = mm([(dgv, wts["w_glu_v"]), (dgg, wts["w_glu_g"])], "nt", F32, "mm_d_yg")
    dw_glu_v = mm([(yg, dgv)], "tn", BF16, "mm_dw_glu_v")
    dw_glu_g = mm([(yg, dgg)], "tn", BF16, "mm_dw_glu_g")

    names = ["w_ffn_up", "w_out", "w_attn_up", "w_glu_v", "w_glu_g"]
    (du_s, dbre_d, dbim_d, dcre_d, dcim_d, dl_re8, dl_im8, dd_ssm), got = ssm_bwd(
        to_segments(dyg), y_ssm, u_s, h_re, h_im, hin_re, hin_im, dvec, w_bre.transpose(0, 2, 1), w_bim.transpose(0, 2, 1),
        w_cre.transpose(0, 2, 1), -w_cim.transpose(0, 2, 1), lre, lim, "ssm_bwd",
        carry=to_owners(names, [dw_ffn_up, dw_out, dw_up, dw_glu_v, dw_glu_g]))
    sent(names, got)
    dbb_re = _block_diag_extract(dbre_d, SSM_GROUP, SSM_STATE).transpose(0, 2, 1)
    dbb_im = _block_diag_extract(dbim_d, SSM_GROUP, SSM_STATE).transpose(0, 2, 1)
    dc_re = _block_diag_extract(dcre_d, SSM_STATE, SSM_GROUP).transpose(0, 2, 1)
    dc_im = -_block_diag_extract(dcim_d, SSM_STATE, SSM_GROUP).transpose(0, 2, 1)

    def fold8(r, c):
        return [], [jnp.sum(r[0], axis=0, keepdims=True), jnp.sum(r[1], axis=0, keepdims=True)]

    dl_re, dl_im = rowwise("ssm_dl_fold", fold8, [dl_re8, dl_im8], [], [], acc_outs=[(1, STATE_W), (1, STATE_W)], ts=SEGS)
    da_re, da_im, dldt, db_re, db_im = ssm_prep_bwd(
        are3, aim3, ldt3, bre3, bim3, dbb_re, dbb_im,
        dl_re.reshape(SSM_GROUPS, SSM_STATE, 1), dl_im.reshape(SSM_GROUPS, SSM_STATE, 1))
    du = from_segments(du_s)

    def merge_bwd(r, c):
        dat = r[0]
        o0, o1, o2, l0, l1, l2 = natural(r[1:])
        w0, w1, w2 = _mix_weights(l0, l1, l2)
        tot = _head_sum(dat * (w0 * o0 + w1 * o1 + w2 * o2))
        to4, to16 = _perm_matrix(PERM_TS, 4, False), _perm_matrix(PERM_TS, 16, False)
        return [w0 * dat, _permute(to4, (w1 * dat).astype(BF16)), _permute(to16, (w2 * dat).astype(BF16)),
                w0 * tot, _permute(to4, w1 * tot), _permute(to16, w2 * tot)], []

    mb = rowwise("attn_merge_bwd", merge_bwd, [dattn] + outs + lses, [],
                 [(GROUP_W, BF16), (GROUP_W, BF16, 4), (GROUP_W, BF16, 16), (GROUP_W, F32), (GROUP_W, F32, 4), (GROUP_W, F32, 16)],
                 ts=PERM_TS)
    dqs, dw_qkv = [], []
    for g, (_, dil) in enumerate(ATTN_GROUPS):
        dq = attn_bwd(qkv[g].reshape(dil, s // dil, 3 * GROUP_W), mb[g].reshape(dil, s // dil, GROUP_W),
                      lses[g].reshape(dil, s // dil, GROUP_W), mb[3 + g].reshape(dil, s // dil, GROUP_W),
                      g, f"attn_bwd{g}").reshape(s, 3 * GROUP_W)
        dqs.append(dq)
        dw_qkv.append(mm([(hd[g].reshape(s, D_MODEL), dq)], "tn", BF16, f"mm_dw_qkv{g}"))
    dw_u = mm([(h, du)], "tn", BF16, "mm_dw_u")
    dw_gates = mm([(h, dgates)], "tn", BF16, "mm_dw_gates")
    dw_in = jnp.concatenate(
        [dw_qkv[g][:, o * GROUP_W:(o + 1) * GROUP_W] for o in range(3) for g in range(3)] + [dw_u, dw_gates], axis=1)
    chunks = _row_chunks(_split_for_devices(dw_in, "w_in"), W_IN_CHUNK_ROWS)
    dh_parts, got_chunks = [], []
    for g, (_, dil) in enumerate(ATTN_GROUPS):
        dh_g, got = mm([(dqs[g], w_in_t)], "nn", BF16, f"mm_d_h_qkv{g}", tk=GROUP_W, b_window=qkv_rows(g), carry=chunks[g])
        got_chunks.append(got[0])
        dh_parts.append(dh_g if dil == 1 else dh_g.reshape(dil, s // dil, D_MODEL))
    dh_parts.append(mm([(du, w_u_t)], "nn", BF16, "mm_d_h_u"))
    dh_gates, got = mm([(dgates, w_gates_t)], "nn", BF16, "mm_d_h_gates", carry=chunks[3])
    got_chunks.append(got[0])
    dh_parts.append(dh_gates)
    recv["w_in"] = jnp.concatenate(got_chunks, axis=1)

    def in_bwd(r, c):
        dh1 = _permute(_perm_matrix(PERM_TS, 4, True), r[1].astype(BF16))
        dh2_ = _permute(_perm_matrix(PERM_TS, 16, True), r[2].astype(BF16))
        dh = r[0] + dh1 + dh2_ + r[3] + r[4]
        _, n1, r1 = _rms(r[6], c[0])
        dx, dg1_ = _rms_bwd(dh, n1, r1, c[0])
        return [dx + r[5]], [dg1_]

    grad_x, dg1 = rowwise("rms_in_bwd", in_bwd, dh_parts + [dx1, x], [g1], [(D_MODEL, F32)], acc_outs=[(1, D_MODEL)], ts=PERM_TS)

    dsmall = dict(norm_mix_pre=dg1, ssm_a_re=da_re, ssm_a_im=da_im, ssm_log_dt=dldt, ssm_b_re=db_re, ssm_b_im=db_im,
                  ssm_c_re=dc_re, ssm_c_im=dc_im, ssm_d=dd_ssm, norm_mix_post=dg2, norm_ffn_pre=dg3, norm_ffn_post=dg4)
    return loss_part, grad_x, recv, dsmall


def adamw(parts, w, m, v, name, carry=None):
    r, c = w.shape
    tr = r
    while tr > 8 and tr % 2 == 0 and tr * c * (8 * parts.dtype.itemsize + 28) * 2 > 24 * 1024 * 1024:
        tr //= 2
    assert r % tr == 0 and (tr % 8 == 0 or tr == r)
    c1, c2 = 1.0 / (1.0 - ADAM_B1 ** ADAM_STEP), 1.0 / (1.0 - ADAM_B2 ** ADAM_STEP)

    def body(p_ref, w_ref, m_ref, v_ref, g_o, d_o, m_o, v_o):
        g = p_ref[0].astype(F32)
        for i in range(1, N_DEV):
            g = g + p_ref[i].astype(F32)
        mn = ADAM_B1 * m_ref[...] + (1.0 - ADAM_B1) * g
        vn = ADAM_B2 * v_ref[...] + (1.0 - ADAM_B2) * (g * g)
        g_o[...] = g
        m_o[...] = mn
        v_o[...] = vn
        d_o[...] = -ADAM_LR * ((mn * c1) / (jnp.sqrt(vn * c2) + ADAM_EPS) + ADAM_WD * w_ref[...])

    blk = pl.BlockSpec((tr, c), lambda i: (i, 0))
    return _run(
        body, [parts, w, m, v], carry=carry, name=name, grid=(r // tr,),
        in_specs=[pl.BlockSpec((N_DEV, tr, c), lambda i: (0, i, 0)), blk, blk, blk],
        out_specs=[blk] * 4, out_shape=[jax.ShapeDtypeStruct((r, c), F32)] * 4, compiler_params=_cparams(("parallel",)),
    )


PACK_C = 1024
SHARDED = ("w_in", "w_attn_up", "w_glu_v", "w_glu_g", "w_out", "w_ffn_gate", "w_ffn_up", "w_ffn_down")
ROW_SHARDED = ("w_out", "w_ffn_down")
SENT_TRANSPOSED = ("w_in", "w_ffn_gate", "w_ffn_up")
GRAD_TRANSPOSED = ("w_ffn_gate", "w_ffn_up")
SMALL = ("norm_mix_pre", "ssm_a_re", "ssm_a_im", "ssm_log_dt", "ssm_b_re", "ssm_b_im", "ssm_c_re", "ssm_c_im", "ssm_d",
         "norm_mix_post", "norm_ffn_pre", "norm_ffn_post")
WEIGHTS = ("norm_mix_pre", "w_in", "w_attn_up", "ssm_a_re", "ssm_a_im", "ssm_log_dt", "ssm_b_re", "ssm_b_im", "ssm_c_re",
           "ssm_c_im", "ssm_d", "w_glu_v", "w_glu_g", "w_out", "norm_mix_post", "norm_ffn_pre", "w_ffn_gate", "w_ffn_up",
           "w_ffn_down", "norm_ffn_post")


def _pack(arrs, dtype, pad_rows_to=64):
    flat = jnp.concatenate([a.reshape(-1).astype(dtype) for a in arrs])
    n = flat.shape[0]
    rows = -(-n // PACK_C)
    rows = -(-rows // pad_rows_to) * pad_rows_to
    return jnp.pad(flat, (0, rows * PACK_C - n)).reshape(rows, PACK_C)


def _unpack(flat2d, shapes):
    flat = flat2d.reshape(-1)
    out, off = [], 0
    for shp in shapes:
        n = int(np.prod(shp))
        out.append(flat[off:off + n].reshape(shp))
        off += n
    return out


def _full_from_gathered(gathered, name):
    if name in ROW_SHARDED or name in SENT_TRANSPOSED:
        return gathered.reshape(-1, gathered.shape[2])
    return gathered.transpose(1, 0, 2).reshape(gathered.shape[1], -1)


def _split_for_devices(full, name):
    if name in ROW_SHARDED or name in GRAD_TRANSPOSED:
        return full.reshape(N_DEV, -1, full.shape[1])
    return full.reshape(full.shape[0], N_DEV, -1).transpose(1, 0, 2)


def kernel(x, norm_mix_pre, w_in, w_attn_up, ssm_a_re, ssm_a_im, ssm_log_dt, ssm_b_re, ssm_b_im, ssm_c_re, ssm_c_im, ssm_d, w_glu_v, w_glu_g, w_out, norm_mix_post, norm_ffn_pre, w_ffn_gate, w_ffn_up, w_ffn_down, norm_ffn_post, loss_target, m_norm_mix_pre, m_w_in, m_w_attn_up, m_ssm_a_re, m_ssm_a_im, m_ssm_log_dt, m_ssm_b_re, m_ssm_b_im, m_ssm_c_re, m_ssm_c_im, m_ssm_d, m_w_glu_v, m_w_glu_g, m_w_out, m_norm_mix_post, m_norm_ffn_pre, m_w_ffn_gate, m_w_ffn_up, m_w_ffn_down, m_norm_ffn_post, v_norm_mix_pre, v_w_in, v_w_attn_up, v_ssm_a_re, v_ssm_a_im, v_ssm_log_dt, v_ssm_b_re, v_ssm_b_im, v_ssm_c_re, v_ssm_c_im, v_ssm_d, v_w_glu_v, v_w_glu_g, v_w_out, v_norm_mix_post, v_norm_ffn_pre, v_w_ffn_gate, v_w_ffn_up, v_w_ffn_down, v_norm_ffn_post):
    args = dict(locals())
    wv = {n: args[n][0] for n in WEIGHTS}
    mv = {n: args["m_" + n][0] for n in WEIGHTS}
    vv = {n: args["v_" + n][0] for n in WEIGHTS}

    shards = {n: (wv[n].T if n in SENT_TRANSPOSED else wv[n]).astype(BF16) for n in SHARDED}
    small = {n: wv[n] for n in SMALL}
    loss_part, grad_x, recv, dsmall = local_step(x[0], loss_target[0], shards, small)
    for n in GRAD_TRANSPOSED:
        recv[n] = recv[n].transpose(0, 2, 1)

    small_shapes = [wv[n].shape for n in SMALL]
    res = {}
    res["w_in"], (sgather,) = adamw(recv["w_in"], wv["w_in"], mv["w_in"], vv["w_in"], "adamw_w_in",
                                    carry=Gather([_pack([dsmall[n] for n in SMALL], F32)]))
    for n in SHARDED[1:]:
        res[n] = adamw(recv[n], wv[n], mv[n], vv[n], "adamw_" + n)
    sres = adamw(sgather, _pack([wv[n] for n in SMALL], F32), _pack([mv[n] for n in SMALL], F32),
                 _pack([vv[n] for n in SMALL], F32), "adamw_small")
    sun = [_unpack(t, small_shapes) for t in sres]
    for k, n in enumerate(SMALL):
        res[n] = tuple(sun[t][k] for t in range(4))

    loss = lax.psum(loss_part[0, 0], ("x", "y", "c"))
    outs = [loss, grad_x[None]]
    for t in range(4):
        outs += [res[n][t][None] for n in WEIGHTS]
    return tuple(outs)
```

```python
import functools
import math

import numpy as np
import jax
import jax.numpy as jnp
from jax import lax
from jax.experimental import pallas as pl
from jax.experimental.pallas import tpu as pltpu

F32 = jnp.float32
BF16 = jnp.bfloat16

D_MODEL = 2048
HEAD_DIM = 128
HEADS_PER_GROUP = 4
ATTN_GROUPS = ((128, 1), (512, 4), (2048, 16))
N_HEADS = HEADS_PER_GROUP * len(ATTN_GROUPS)
GROUP_W = HEADS_PER_GROUP * HEAD_DIM
HQ = N_HEADS * HEAD_DIM
SSM_W = 1024
SSM_GROUP = 16
SSM_GROUPS = 64
SSM_STATE = 64
STATE_W = SSM_GROUPS * SSM_STATE
D_FF = 5632
EPS = 1e-6
N_DEV = 8
SEGS = 8
BD = 8

ADAM_LR, ADAM_B1, ADAM_B2, ADAM_EPS, ADAM_WD, ADAM_STEP = 0.001, 0.9, 0.999, 1e-08, 0.01, 10

VMEM_LIMIT = 56 * 1024 * 1024
HBM_SPEC = pl.BlockSpec(memory_space=pltpu.HBM)
MESH_ID = pl.DeviceIdType.MESH
NEG = -1e30


def _pcall(body, **kw):
    return pl.pallas_call(body, **kw)


def _cparams(sem=None):
    if sem is None:
        return pltpu.CompilerParams(vmem_limit_bytes=VMEM_LIMIT)
    return pltpu.CompilerParams(vmem_limit_bytes=VMEM_LIMIT, dimension_semantics=sem)


def _my_coords():
    return lax.axis_index("x"), lax.axis_index("y"), lax.axis_index("c")


class Gather:
    def __init__(self, xs):
        self.arrays = list(xs)
        self.out_shapes = [jax.ShapeDtypeStruct((N_DEV,) + x.shape, x.dtype) for x in xs]

    def _ctx(self, out_refs, send_sems, recv_sems):
        mx, my, mc = _my_coords()
        me, sibling = (mx, my, mc), (mx, my, 1 - mc)
        chips = [(1 - mx, my), (mx, 1 - my), (1 - mx, 1 - my)]

        def slot(a, px, py, pc):
            return out_refs[a].at[4 * px + 2 * py + pc]

        def copy(a, k, block, to, src=None):
            return pltpu.make_async_remote_copy(
                src_ref=slot(a, *block) if src is None else src, dst_ref=slot(a, *block),
                send_sem=send_sems.at[7 * a + k], recv_sem=recv_sems.at[7 * a + k], device_id=to, device_id_type=MESH_ID)

        return me, sibling, chips, mc, slot, copy

    def _first(self, a, x_refs, ctx):
        me, sibling, chips, mc, slot, copy = ctx
        return [copy(a, 0, me, sibling, src=x_refs[a])] + [copy(a, 1 + j, me, (*chip, mc), src=x_refs[a]) for j, chip in enumerate(chips)]

    def start(self, x_refs, out_refs, send_sems, recv_sems, local_sems):
        ctx = self._ctx(out_refs, send_sems, recv_sems)
        me, slot = ctx[0], ctx[4]
        for a in range(len(self.arrays)):
            pltpu.make_async_copy(x_refs[a], slot(a, *me), local_sems.at[a]).start()
            for cp in self._first(a, x_refs, ctx):
                cp.start()

    def finish(self, x_refs, out_refs, send_sems, recv_sems, local_sems):
        ctx = self._ctx(out_refs, send_sems, recv_sems)
        me, sibling, chips, mc, slot, copy = ctx
        na = len(self.arrays)
        passed = []
        for a in range(na):
            for j, chip in enumerate(chips):
                copy(a, 1 + j, (*chip, mc), me).wait_recv()
                fwd = copy(a, 4 + j, (*chip, mc), sibling)
                fwd.start()
                passed.append(fwd)
        for a in range(na):
            copy(a, 0, sibling, me).wait_recv()
            for j, chip in enumerate(chips):
                copy(a, 4 + j, (*chip, 1 - mc), me).wait_recv()
        for a in range(na):
            for cp in self._first(a, x_refs, ctx):
                cp.wait_send()
        for cp in passed:
            cp.wait_send()
        for a in range(na):
            pltpu.make_async_copy(x_refs[a], slot(a, *me), local_sems.at[a]).wait()


class AllToAll:
    def __init__(self, ps):
        self.arrays = list(ps)
        self.out_shapes = [jax.ShapeDtypeStruct(p.shape, p.dtype) for p in ps]

    def _copies(self, p_refs, out_refs, send_sems, recv_sems, local_sems):
        mx, my, mc = _my_coords()
        me = 4 * mx + 2 * my + mc
        local, remote = [], []
        for a in range(len(self.arrays)):
            local.append(pltpu.make_async_copy(p_refs[a].at[me], out_refs[a].at[me], local_sems.at[a]))
            for k in range(1, N_DEV):
                px, py, pc = mx ^ ((k >> 2) & 1), my ^ ((k >> 1) & 1), mc ^ (k & 1)
                remote.append(pltpu.make_async_remote_copy(
                    src_ref=p_refs[a].at[4 * px + 2 * py + pc], dst_ref=out_refs[a].at[me],
                    send_sem=send_sems.at[7 * a + k - 1], recv_sem=recv_sems.at[7 * a + k - 1],
                    device_id=(px, py, pc), device_id_type=MESH_ID))
        return local, remote

    def start(self, *refs):
        local, remote = self._copies(*refs)
        for cp in local + remote:
            cp.start()

    def finish(self, *refs):
        local, remote = self._copies(*refs)
        for cp in remote:
            cp.wait_recv()
        for cp in remote:
            cp.wait_send()
        for cp in local:
            cp.wait()


def _run(body, args, carry=None, **kw):
    if carry is None:
        return _pcall(body, **kw)(*args)
    grid = kw["grid"]
    single = not isinstance(kw["out_shape"], (list, tuple))
    in_specs = list(kw["in_specs"])
    out_specs = [kw["out_specs"]] if single else list(kw["out_specs"])
    out_shape = [kw["out_shape"]] if single else list(kw["out_shape"])
    scratch = list(kw.get("scratch_shapes", []))
    na, nin, nout, nscr = len(carry.arrays), len(in_specs), len(out_specs), len(scratch)

    def carried(*refs):
        ins, cin = refs[:nin], refs[nin:nin + na]
        outs, cout = refs[nin + na:nin + na + nout], refs[nin + na + nout:nin + 2 * na + nout]
        scr = refs[nin + 2 * na + nout:nin + 2 * na + nout + nscr]
        sems = refs[nin + 2 * na + nout + nscr:]
        ids = [pl.program_id(i) for i in range(len(grid))]
        first, last = ids[0] == 0, ids[0] == grid[0] - 1
        for i in range(1, len(grid)):
            first = jnp.logical_and(first, ids[i] == 0)
            last = jnp.logical_and(last, ids[i] == grid[i] - 1)

        @pl.when(first)
        def _():
            carry.start(cin, cout, *sems)

        body(*ins, *outs, *scr)

        @pl.when(last)
        def _():
            carry.finish(cin, cout, *sems)

    res = _pcall(
        carried, name=kw["name"], grid=grid, in_specs=in_specs + [HBM_SPEC] * na, out_specs=out_specs + [HBM_SPEC] * na,
        out_shape=out_shape + carry.out_shapes,
        scratch_shapes=scratch + [pltpu.SemaphoreType.DMA((7 * na,)), pltpu.SemaphoreType.DMA((7 * na,)), pltpu.SemaphoreType.DMA((na,))],
        compiler_params=_cparams(("arbitrary",) * len(grid)),
    )(*args, *carry.arrays)
    main = res[:nout]
    return (main[0] if single else main), list(res[nout:])


_DN = {"nn": (((1,), (0,)), ((), ())), "nt": (((1,), (1,)), ((), ())), "tn": (((0,), (0,)), ((), ()))}


LANE = 128
MM_TM, MM_TN, MM_TK = 1024, 1536, 2048


def _tile(n, cap):
    for t in range(min(cap, n) // LANE * LANE, 0, -LANE):
        if n % t == 0:
            return t
    raise ValueError(n)


DW_TM, DW_TN, DW_TK = 512, 512, 8192


def mm(pairs, mode, out_dtype, name, tm=None, tn=None, tk=None, carry=None, epilogue=None, extras=(), b_window=None):
    a0, b0 = pairs[0]
    if mode == "nn":
        (m, k), n = a0.shape, b0.shape[1]
    elif mode == "nt":
        (m, k), n = a0.shape, b0.shape[0]
    else:
        (k, m), n = a0.shape, b0.shape[1]
    if b_window is not None:
        assert mode in ("nn", "nt") and len(pairs) == 1
        if mode == "nt":
            n = b_window[0]
        else:
            assert k == b_window[0]
    caps = (DW_TM, DW_TN, DW_TK) if mode == "tn" else (MM_TM, MM_TN, MM_TK)
    tm, tn, tk = _tile(m, tm or caps[0]), _tile(n, tn or caps[1]), _tile(k, tk or caps[2])
    nk = k // tk
    npairs = len(pairs)
    nex = len(extras)
    fused = epilogue is not None
    assert not fused or nk == 1
    out_dtypes = list(out_dtype) if fused else [out_dtype]

    def body(*refs):
        prods = []
        for p in range(npairs):
            a = refs[2 * p][...].astype(BF16) if (p == 0 or pairs[p][0] is not pairs[p - 1][0]) else a
            b = refs[2 * p + 1][...].astype(BF16)
            prods.append(lax.dot_general(a, b, _DN[mode], preferred_element_type=F32))
        if fused:
            ex = [refs[2 * npairs + e][...].astype(F32) for e in range(nex)]
            for o_ref, val in zip(refs[2 * npairs + nex:], epilogue(prods, ex)):
                o_ref[...] = val.astype(o_ref.dtype)
            return
        o_ref = refs[2 * npairs]
        tot = prods[0]
        for d in prods[1:]:
            tot = tot + d
        if nk == 1:
            o_ref[...] = tot.astype(o_ref.dtype)
            return
        acc = refs[2 * npairs + 1]
        kk = pl.program_id(2)

        @pl.when(kk == 0)
        def _():
            acc[...] = tot

        @pl.when(kk > 0)
        def _():
            acc[...] += tot

        @pl.when(kk == nk - 1)
        def _():
            o_ref[...] = acc[...].astype(o_ref.dtype)

    rows_of = b_window[1] if b_window is not None else (lambda t: t)
    if mode == "nn":
        sp = [pl.BlockSpec((tm, tk), lambda i, j, kk: (i, kk)), pl.BlockSpec((tk, tn), lambda i, j, kk: (rows_of(kk), j))]
    elif mode == "nt":
        sp = [pl.BlockSpec((tm, tk), lambda i, j, kk: (i, kk)), pl.BlockSpec((tn, tk), lambda i, j, kk: (rows_of(j), kk))]
    else:
        sp = [pl.BlockSpec((tk, tm), lambda i, j, kk: (kk, i)), pl.BlockSpec((tk, tn), lambda i, j, kk: (kk, j))]
    o_spec = pl.BlockSpec((tm, tn), lambda i, j, kk: (i, j))
    out_shapes = [jax.ShapeDtypeStruct((m, n), dt) for dt in out_dtypes]
    return _run(
        body, [t for pr in pairs for t in pr] + list(extras), carry=carry, name=name, grid=(m // tm, n // tn, nk),
        in_specs=sp * npairs + [o_spec] * nex,
        out_specs=[o_spec] * len(out_shapes) if fused else o_spec,
        out_shape=out_shapes if fused else out_shapes[0],
        scratch_shapes=[pltpu.VMEM((tm, tn), F32)] if nk > 1 else [],
        compiler_params=_cparams(("parallel", "parallel", "arbitrary")),
    )


def rowwise(name, fn, row_ins, const_ins, row_outs, acc_outs=(), ts=None, carry=None):
    s = row_ins[0].shape[0]
    row_outs = [ro if len(ro) == 3 else (*ro, 1) for ro in row_outs]
    if ts is None:
        per_row = sum(a.shape[-1] * a.dtype.itemsize for a in row_ins) + sum(w * jnp.dtype(dt).itemsize for w, dt, _ in row_outs)
        ts = 512
        while ts > 8 and 2 * ts * per_row > 20 * 1024 * 1024:
            ts //= 2
    ts = min(ts, s)
    assert s % ts == 0
    nr, nc, no, na = len(row_ins), len(const_ins), len(row_outs), len(acc_outs)

    def body(*refs):
        rows = [r[...].reshape(ts, r.shape[-1]).astype(F32) for r in refs[:nr]]
        consts = [r[...] for r in refs[nr:nr + nc]]
        outs, accs = fn(rows, consts)
        for r, v in zip(refs[nr + nc:nr + nc + no], outs):
            r[...] = v.astype(r.dtype).reshape(r.shape)
        if na:
            first = pl.program_id(0) == 0
            for r, v in zip(refs[nr + nc + no:], accs):
                @pl.when(first)
                def _(r=r, v=v):
                    r[...] = v

                @pl.when(jnp.logical_not(first))
                def _(r=r, v=v):
                    r[...] += v

    def tile_spec(w, d):
        if d == 1:
            return pl.BlockSpec((ts, w), lambda i: (i, 0))
        return pl.BlockSpec((d, ts // d, w), lambda i: (0, i, 0))

    in_specs = [tile_spec(a.shape[-1], a.shape[0] if a.ndim == 3 else 1) for a in row_ins]
    in_specs += [pl.BlockSpec(c.shape, lambda i, nd=c.ndim: (0,) * nd) for c in const_ins]
    out_specs = [tile_spec(w, d) for w, _, d in row_outs]
    out_specs += [pl.BlockSpec(shp, lambda i, nd=len(shp): (0,) * nd) for shp in acc_outs]
    out_shape = [jax.ShapeDtypeStruct((s, w) if d == 1 else (d, s // d, w), dt) for w, dt, d in row_outs]
    out_shape += [jax.ShapeDtypeStruct(shp, F32) for shp in acc_outs]
    return _run(
        body, [*row_ins, *const_ins], carry=carry, name=name, grid=(s // ts,), in_specs=in_specs, out_specs=out_specs,
        out_shape=out_shape, compiler_params=_cparams(("arbitrary",)),
    )


PERM_TS = 256


def _perm_matrix(ts, d, inverse):
    i = lax.broadcasted_iota(jnp.int32, (ts, ts), 0)
    k = lax.broadcasted_iota(jnp.int32, (ts, ts), 1)
    per = ts // d
    src = (i % d) * per + i // d if inverse else (i % per) * d + i // per
    return jnp.where(k == src, 1.0, 0.0).astype(BF16)


def _permute(p, x):
    if x.dtype == BF16:
        return jnp.dot(p, x, preferred_element_type=F32)
    hi = x.astype(BF16)
    rest = x - hi.astype(F32)
    mid = rest.astype(BF16)
    lo = (rest - mid.astype(F32)).astype(BF16)
    out = jnp.dot(p, hi, preferred_element_type=F32) + jnp.dot(p, mid, preferred_element_type=F32)
    return out + jnp.dot(p, lo, preferred_element_type=F32)


def _rms(x, gain):
    r = lax.rsqrt(jnp.mean(x * x, axis=-1, keepdims=True) + EPS)
    n = x * r
    return n * gain, n, r


def _rms_bwd(dy, n, r, gain):
    dn = dy * gain
    dx = r * (dn - n * jnp.mean(dn * n, axis=-1, keepdims=True))
    return dx, jnp.sum(dy * n, axis=0, keepdims=True)


def _sigmoid(x):
    return 1.0 / (1.0 + jnp.exp(-x))


_GELU_K = math.sqrt(2.0 / math.pi)


def _gelu(x):
    t = jnp.tanh(_GELU_K * (x + 0.044715 * x * x * x))
    return 0.5 * x * (1.0 + t), t


def _gelu_grad(x, t):
    return 0.5 * (1.0 + t) + 0.5 * x * (1.0 - t * t) * _GELU_K * (1.0 + 3.0 * 0.044715 * x * x)


def _head_sum(x):
    parts = []
    for h in range(HEADS_PER_GROUP):
        sl = x[:, h * HEAD_DIM:(h + 1) * HEAD_DIM]
        parts.append(jnp.broadcast_to(jnp.sum(sl, axis=-1, keepdims=True), sl.shape))
    return jnp.concatenate(parts, axis=-1)


def _mix_weights(l0, l1, l2):
    mx = jnp.maximum(jnp.maximum(l0, l1), l2)
    e0, e1, e2 = jnp.exp(l0 - mx), jnp.exp(l1 - mx), jnp.exp(l2 - mx)
    inv = 1.0 / (e0 + e1 + e2)
    return e0 * inv, e1 * inv, e2 * inv


BLK = 128


def _slopes(g):
    return [2.0 ** (-8.0 * (g * HEADS_PER_GROUP + h + 1) / N_HEADS) for h in range(HEADS_PER_GROUP)]


def _attn_masks(dil):
    qi = lax.broadcasted_iota(jnp.int32, (BLK, BLK), 0)
    ki = lax.broadcasted_iota(jnp.int32, (BLK, BLK), 1)
    dist_c = qi - ki
    dist_p = BLK + qi - ki
    return dist_c >= 0, dist_p <= BLK, (dist_c * dil).astype(F32), (dist_p * dil).astype(F32)


def _window_mask(has_prev, dil):
    qi = lax.broadcasted_iota(jnp.int32, (BLK, 2 * BLK), 0)
    ki = lax.broadcasted_iota(jnp.int32, (BLK, 2 * BLK), 1)
    dist = BLK + qi - ki
    ok = jnp.logical_and(jnp.logical_and(dist >= 0, dist <= BLK), jnp.logical_or(ki >= BLK, has_prev))
    return ok, (dist * dil).astype(F32)


def attn_fwd(qkv, g, name):
    dil, length, _ = qkv.shape
    scale = HEAD_DIM ** -0.5
    slopes = _slopes(g)

    def body(q_ref, kc_ref, vc_ref, kp_ref, vp_ref, o_ref, l_ref):
        ok, dist = _window_mask(pl.program_id(1) > 0, dil)
        for h in range(HEADS_PER_GROUP):
            sl = slice(h * HEAD_DIM, (h + 1) * HEAD_DIM)
            k2 = jnp.concatenate([kp_ref[:, sl], kc_ref[:, sl]], axis=0)
            v2 = jnp.concatenate([vp_ref[:, sl], vc_ref[:, sl]], axis=0)
            s = lax.dot_general(q_ref[:, sl], k2, _DN["nt"], preferred_element_type=F32) * scale - slopes[h] * dist
            s = jnp.where(ok, s, NEG)
            mx = jnp.max(s, axis=-1, keepdims=True)
            p = jnp.exp(s - mx)
            den = jnp.sum(p, axis=-1, keepdims=True)
            o_ref[:, sl] = (jnp.dot(p.astype(BF16), v2, preferred_element_type=F32) / den).astype(BF16)
            l_ref[:, sl] = jnp.broadcast_to(mx + jnp.log(den), (BLK, HEAD_DIM))

    def spec(col, prev):
        if prev:
            return pl.BlockSpec((None, BLK, GROUP_W), lambda r, n: (r, jnp.maximum(n - 1, 0), col))
        return pl.BlockSpec((None, BLK, GROUP_W), lambda r, n: (r, n, col))

    out_spec = pl.BlockSpec((None, BLK, GROUP_W), lambda r, n: (r, n, 0))
    return _pcall(
        body, name=name, grid=(dil, length // BLK),
        in_specs=[spec(0, False), spec(1, False), spec(2, False), spec(1, True), spec(2, True)],
        out_specs=[out_spec, out_spec],
        out_shape=[jax.ShapeDtypeStruct((dil, length, GROUP_W), BF16), jax.ShapeDtypeStruct((dil, length, GROUP_W), F32)],
        compiler_params=_cparams(("parallel", "parallel")),
    )(qkv, qkv, qkv, qkv, qkv)


def attn_bwd(qkv, dout, lse, dd, g, name, carry=None):
    dil, length, _ = qkv.shape
    nblk = length // BLK
    scale = HEAD_DIM ** -0.5
    slopes = _slopes(g)

    def body(q_ref, kc_ref, vc_ref, kp_ref, vp_ref, qn_ref, do_ref, don_ref, l_ref, ln_ref, d_ref, dn_ref, o_ref):
        n = pl.program_id(1)
        ok2, dist2 = _window_mask(n > 0, dil)
        _, ok_p, _, dp = _attn_masks(dil)
        ok_next = jnp.logical_and(ok_p, n < nblk - 1)
        for h in range(HEADS_PER_GROUP):
            sl = slice(h * HEAD_DIM, (h + 1) * HEAD_DIM)
            q, kc, vc, qn = q_ref[:, sl], kc_ref[:, sl], vc_ref[:, sl], qn_ref[:, sl]
            k2 = jnp.concatenate([kp_ref[:, sl], kc], axis=0)
            v2 = jnp.concatenate([vp_ref[:, sl], vc], axis=0)
            do, don = do_ref[:, sl], don_ref[:, sl]
            lse_q, lse_n, dd_q, dd_n = l_ref[:, sl], ln_ref[:, sl], d_ref[:, sl], dn_ref[:, sl]

            def probs(qq, kk, dist, ok, lse_t):
                s = lax.dot_general(qq, kk, _DN["nt"], preferred_element_type=F32) * scale - slopes[h] * dist
                return jnp.where(ok, jnp.exp(jnp.where(ok, s, NEG) - lse_t), 0.0)

            p2 = probs(q, k2, dist2, ok2, jnp.concatenate([lse_q, lse_q], axis=1))
            p_x = probs(qn, kc, dp, ok_next, lse_n)
            ds2 = p2 * (lax.dot_general(do, v2, _DN["nt"], preferred_element_type=F32) - jnp.concatenate([dd_q, dd_q], axis=1))
            ds_x = p_x * (lax.dot_general(don, vc, _DN["nt"], preferred_element_type=F32) - dd_n)
            dq = jnp.dot(ds2.astype(BF16), k2, preferred_element_type=F32)
            ds_k = jnp.concatenate([ds2[:, BLK:], ds_x], axis=0).astype(BF16)
            p_k = jnp.concatenate([p2[:, BLK:], p_x], axis=0).astype(BF16)
            dk = lax.dot_general(ds_k, jnp.concatenate([q, qn], axis=0), _DN["tn"], preferred_element_type=F32)
            dv = lax.dot_general(p_k, jnp.concatenate([do, don], axis=0), _DN["tn"], preferred_element_type=F32)
            o_ref[:, h * HEAD_DIM:(h + 1) * HEAD_DIM] = (dq * scale).astype(BF16)
            o_ref[:, GROUP_W + h * HEAD_DIM:GROUP_W + (h + 1) * HEAD_DIM] = (dk * scale).astype(BF16)
            o_ref[:, 2 * GROUP_W + h * HEAD_DIM:2 * GROUP_W + (h + 1) * HEAD_DIM] = dv.astype(BF16)

    def spec(col, which):
        if which == "prev":
            return pl.BlockSpec((None, BLK, GROUP_W), lambda r, n: (r, jnp.maximum(n - 1, 0), col))
        if which == "next":
            return pl.BlockSpec((None, BLK, GROUP_W), lambda r, n: (r, jnp.minimum(n + 1, nblk - 1), col))
        return pl.BlockSpec((None, BLK, GROUP_W), lambda r, n: (r, n, col))

    return _run(
        body, [qkv, qkv, qkv, qkv, qkv, qkv, dout, dout, lse, lse, dd, dd], carry=carry, name=name, grid=(dil, nblk),
        in_specs=[spec(0, "cur"), spec(1, "cur"), spec(2, "cur"), spec(1, "prev"), spec(2, "prev"), spec(0, "next"),
                  spec(0, "cur"), spec(0, "next"), spec(0, "cur"), spec(0, "next"), spec(0, "cur"), spec(0, "next")],
        out_specs=pl.BlockSpec((None, BLK, 3 * GROUP_W), lambda r, n: (r, n, 0)),
        out_shape=jax.ShapeDtypeStruct((dil, length, 3 * GROUP_W), BF16),
        compiler_params=_cparams(("parallel", "parallel")),
    )


def _ssm_prep_values(are, aim, logdt):
    dt = jnp.exp(logdt)
    mag = jnp.exp(are * dt)
    lb_re, lb_im = mag * jnp.cos(aim * dt), mag * jnp.sin(aim * dt)
    inv = 1.0 / (are * are + aim * aim)
    n_re, n_im = lb_re - 1.0, lb_im
    f_re = (n_re * are + n_im * aim) * inv
    f_im = (n_im * are - n_re * aim) * inv
    return dt, lb_re, lb_im, f_re, f_im, inv


PREP_G = 8


def _group_specs(are, logdt, bre):
    def spec(a):
        return pl.BlockSpec((PREP_G,) + a.shape[1:], lambda i: (i, 0, 0))
    return spec(are), spec(logdt), spec(bre)


def ssm_prep(are, aim, logdt, bre, bim):
    def body(are_r, aim_r, ldt_r, bre_r, bim_r, lre_o, lim_o, bbre_o, bbim_o):
        _, lb_re, lb_im, f_re, f_im, _ = _ssm_prep_values(are_r[...], aim_r[...], ldt_r[...])
        lre_o[...] = lb_re
        lim_o[...] = lb_im
        bbre_o[...] = f_re * bre_r[...] - f_im * bim_r[...]
        bbim_o[...] = f_re * bim_r[...] + f_im * bre_r[...]

    sh1 = jax.ShapeDtypeStruct(are.shape, F32)
    shb = jax.ShapeDtypeStruct(bre.shape, F32)
    s1, sd, sb = _group_specs(are, logdt, bre)
    return _pcall(body, name="ssm_prep", grid=(SSM_GROUPS // PREP_G,), in_specs=[s1, s1, sd, sb, sb], out_specs=[s1, s1, sb, sb],
                  out_shape=[sh1, sh1, shb, shb], compiler_params=_cparams(("parallel",)))(are, aim, logdt, bre, bim)


def ssm_prep_bwd(are, aim, logdt, bre, bim, dbbre, dbbim, dlre, dlim):
    def body(are_r, aim_r, ldt_r, bre_r, bim_r, dbbre_r, dbbim_r, dlre_r, dlim_r, dare_o, daim_o, dldt_o, dbre_o, dbim_o):
        are_v, aim_v = are_r[...], aim_r[...]
        dt, lb_re, lb_im, f_re, f_im, inv = _ssm_prep_values(are_v, aim_v, ldt_r[...])
        b_re, b_im, g_re, g_im = bre_r[...], bim_r[...], dbbre_r[...], dbbim_r[...]
        dbre_o[...] = f_re * g_re + f_im * g_im
        dbim_o[...] = f_re * g_im - f_im * g_re
        df_re = jnp.sum(b_re * g_re + b_im * g_im, axis=-1, keepdims=True)
        df_im = jnp.sum(b_re * g_im - b_im * g_re, axis=-1, keepdims=True)
        il_re, il_im = are_v * inv, -aim_v * inv
        cl_re = dlre_r[...] + il_re * df_re + il_im * df_im
        cl_im = dlim_r[...] + il_re * df_im - il_im * df_re
        q_re = -(f_re * il_re - f_im * il_im)
        q_im = -(f_re * il_im + f_im * il_re)
        ca_re = q_re * df_re + q_im * df_im
        ca_im = q_re * df_im - q_im * df_re
        cz_re = lb_re * cl_re + lb_im * cl_im
        cz_im = lb_re * cl_im - lb_im * cl_re
        dare_o[...] = ca_re + dt * cz_re
        daim_o[...] = ca_im + dt * cz_im
        dldt_o[...] = dt * jnp.sum(are_v * cz_re + aim_v * cz_im, axis=1, keepdims=True)

    sh1 = jax.ShapeDtypeStruct(are.shape, F32)
    shb = jax.ShapeDtypeStruct(bre.shape, F32)
    s1, sd, sb = _group_specs(are, logdt, bre)
    return _pcall(
        body, name="ssm_prep_bwd", grid=(SSM_GROUPS // PREP_G,), in_specs=[s1, s1, sd, sb, sb, sb, sb, s1, s1],
        out_specs=[s1, s1, sd, sb, sb], out_shape=[sh1, sh1, jax.ShapeDtypeStruct(logdt.shape, F32), shb, shb],
        compiler_params=_cparams(("parallel",)),
    )(are, aim, logdt, bre, bim, dbbre, dbbim, dlre, dlim)


SCAN_WC = 512


def _chain_segments(a_re, a_im, e_re, e_im, nsq, reverse):
    p_re, p_im = a_re, a_im
    for _ in range(nsq):
        p_re, p_im = p_re * p_re - p_im * p_im, 2.0 * p_re * p_im
    row = lax.broadcasted_iota(jnp.int32, e_re.shape, 0)
    edge = (row == SEGS - 1) if reverse else (row == 0)
    shift = SEGS - 1 if reverse else 1
    c_re, c_im = jnp.zeros_like(e_re), jnp.zeros_like(e_im)
    for _ in range(SEGS - 1):
        n_re = p_re * c_re - p_im * c_im + e_re
        n_im = p_re * c_im + p_im * c_re + e_im
        c_re = jnp.where(edge, 0.0, pltpu.roll(n_re, shift, 0))
        c_im = jnp.where(edge, 0.0, pltpu.roll(n_im, shift, 0))
    return c_re, c_im


def _scan_dims(s):
    steps = s // SEGS
    assert steps & (steps - 1) == 0
    tt = min(128, steps)
    return steps, tt, steps // tt, tt * SEGS, int(math.log2(steps))


U_BLK = SSM_W // BD


def ssm_fwd(u_s, dvec, w_bre, w_bim, w_cre, w_cim_neg, lre, lim, name, carry=None):
    s = u_s.shape[0]
    steps, tt, nch, rows, nsq = _scan_dims(s)
    nb, ub_w, wc = w_bre.shape

    def body(u_r, d_r, bre_r, bim_r, cre_r, cim_r, lre_r, lim_r, yg_o, ys_o, hre_o, him_o, hin_re_o, hin_im_o,
             st_re, st_im, x_re, x_im, h_re, h_im):
        ps, ch = pl.program_id(1), pl.program_id(2)
        a_re = jnp.broadcast_to(lre_r[...], (SEGS, wc))
        a_im = jnp.broadcast_to(lim_r[...], (SEGS, wc))
        ub = u_r[...]
        ub16 = ub.astype(BF16)
        x_re[...] = jnp.dot(ub16, bre_r[...], preferred_element_type=F32)
        x_im[...] = jnp.dot(ub16, bim_r[...], preferred_element_type=F32)

        @pl.when(jnp.logical_and(ps == 0, ch == 0))
        def _():
            st_re[...] = jnp.zeros_like(st_re)
            st_im[...] = jnp.zeros_like(st_im)

        @pl.when(jnp.logical_and(ps == 1, ch == 0))
        def _():
            c_re, c_im = _chain_segments(a_re, a_im, st_re[...], st_im[...], nsq, False)
            st_re[...] = c_re
            st_im[...] = c_im
            hin_re_o[...] = c_re
            hin_im_o[...] = c_im

        def run(store):
            def step(t, hc):
                off = pl.multiple_of(t * SEGS, SEGS)
                n_re = a_re * hc[0] - a_im * hc[1] + x_re[pl.ds(off, SEGS), :]
                n_im = a_re * hc[1] + a_im * hc[0] + x_im[pl.ds(off, SEGS), :]
                if store:
                    h_re[pl.ds(off, SEGS), :] = n_re
                    h_im[pl.ds(off, SEGS), :] = n_im
                return n_re, n_im

            fin = lax.fori_loop(0, tt, step, (st_re[...], st_im[...]))
            st_re[...] = fin[0]
            st_im[...] = fin[1]

        @pl.when(ps == 0)
        def _():
            run(False)

        @pl.when(ps == 1)
        def _():
            run(True)
            hr16, hi16 = h_re[...].astype(BF16), h_im[...].astype(BF16)
            hre_o[...] = hr16
            him_o[...] = hi16
            y = jnp.dot(hr16, cre_r[...], preferred_element_type=F32) + jnp.dot(hi16, cim_r[...], preferred_element_type=F32)
            y = y + d_r[...] * ub
            ys_o[...] = y
            yg_o[...] = _gelu(y)[0].astype(BF16)

    def pass1(ps, c):
        return jnp.where(ps == 1, c, 0)

    u_spec = pl.BlockSpec((rows, ub_w), lambda j, ps, c: (c, j))
    d_spec = pl.BlockSpec((1, ub_w), lambda j, ps, c: (0, j))
    b_spec = pl.BlockSpec((None, ub_w, wc), lambda j, ps, c: (j, 0, 0))
    c_spec = pl.BlockSpec((None, wc, ub_w), lambda j, ps, c: (j, 0, 0))
    l_spec = pl.BlockSpec((1, wc), lambda j, ps, c: (0, j))
    y_spec = pl.BlockSpec((rows, ub_w), lambda j, ps, c: (pass1(ps, c), j))
    h_spec = pl.BlockSpec((rows, wc), lambda j, ps, c: (pass1(ps, c), j))
    e_spec = pl.BlockSpec((SEGS, wc), lambda j, ps, c: (0, j))
    return _run(
        body, [u_s, dvec, w_bre, w_bim, w_cre, w_cim_neg, lre, lim], carry=carry, name=name, grid=(nb, 2, nch),
        in_specs=[u_spec, d_spec, b_spec, b_spec, c_spec, c_spec, l_spec, l_spec],
        out_specs=[y_spec, y_spec, h_spec, h_spec, e_spec, e_spec],
        out_shape=[jax.ShapeDtypeStruct((s, SSM_W), BF16), jax.ShapeDtypeStruct((s, SSM_W), F32),
                   jax.ShapeDtypeStruct((s, STATE_W), BF16), jax.ShapeDtypeStruct((s, STATE_W), BF16),
                   jax.ShapeDtypeStruct((SEGS, STATE_W), F32), jax.ShapeDtypeStruct((SEGS, STATE_W), F32)],
        scratch_shapes=[pltpu.VMEM((SEGS, wc), F32)] * 2 + [pltpu.VMEM((rows, wc), F32)] * 4,
        compiler_params=_cparams(("parallel", "arbitrary", "arbitrary")),
    )


def ssm_bwd(dyg_s, ys, u_s, h_re, h_im, hin_re, hin_im, dvec, w_bre_t, w_bim_t, w_cre_t, w_cim_neg_t, lre, lim, name, carry=None):
    s = u_s.shape[0]
    steps, tt, nch, rows, nsq = _scan_dims(s)
    half = 2 * SEGS

    def body(dyg_r, ys_r, u_r, hre_r, him_r, pre_r, pim_r, cin_re_r, cin_im_r, d_r, bre_r, bim_r, cre_r, cim_r, lre_r, lim_r,
             du_o, dbre_o, dbim_o, dcre_o, dcim_o, dlre_o, dlim_o, dd_o, st_re, st_im, x_re, x_im, g_re, g_im, hf_re, hf_im):
        ps, ch = pl.program_id(1), pl.program_id(2)
        a_re = jnp.broadcast_to(lre_r[...], (SEGS, SCAN_WC))
        a_im = -jnp.broadcast_to(lim_r[...], (SEGS, SCAN_WC))
        ub, y = u_r[...], ys_r[...]
        dy = dyg_r[...] * _gelu_grad(y, _gelu(y)[1])
        dy16 = dy.astype(BF16)
        x_re[...] = jnp.dot(dy16, cre_r[...], preferred_element_type=F32)
        x_im[...] = jnp.dot(dy16, cim_r[...], preferred_element_type=F32)

        @pl.when(jnp.logical_and(ps == 0, ch == 0))
        def _():
            st_re[...] = jnp.zeros_like(st_re)
            st_im[...] = jnp.zeros_like(st_im)

        @pl.when(jnp.logical_and(ps == 1, ch == 0))
        def _():
            c_re, c_im = _chain_segments(a_re, a_im, st_re[...], st_im[...], nsq, True)
            st_re[...] = c_re
            st_im[...] = c_im
            dlre_o[...] = jnp.zeros_like(dlre_o)
            dlim_o[...] = jnp.zeros_like(dlim_o)

        @pl.when(ps == 0)
        def _():
            def step(i, hc):
                off = pl.multiple_of((tt - 1 - i) * SEGS, SEGS)
                return (a_re * hc[0] - a_im * hc[1] + x_re[pl.ds(off, SEGS), :],
                        a_re * hc[1] + a_im * hc[0] + x_im[pl.ds(off, SEGS), :])

            fin = lax.fori_loop(0, tt, step, (st_re[...], st_im[...]))
            st_re[...] = fin[0]
            st_im[...] = fin[1]

        @pl.when(ps == 1)
        def _():
            hf_re[...] = hre_r[...].astype(F32)
            hf_im[...] = him_r[...].astype(F32)
            first_chunk = ch == nch - 1
            edge_re = jnp.where(first_chunk, cin_re_r[...], pre_r[...].astype(F32)[SEGS:, :])
            edge_im = jnp.where(first_chunk, cin_im_r[...], pim_r[...].astype(F32)[SEGS:, :])

            def step(i, hc):
                t = tt - 1 - i
                off = pl.multiple_of(t * SEGS, SEGS)
                n_re = a_re * hc[0] - a_im * hc[1] + x_re[pl.ds(off, SEGS), :]
                n_im = a_re * hc[1] + a_im * hc[0] + x_im[pl.ds(off, SEGS), :]
                g_re[pl.ds(off, SEGS), :] = n_re
                g_im[pl.ds(off, SEGS), :] = n_im
                offp = pl.multiple_of(jnp.maximum(t - 1, 0) * SEGS, SEGS)
                hp_re = jnp.where(t == 0, edge_re, hf_re[pl.ds(offp, SEGS), :])
                hp_im = jnp.where(t == 0, edge_im, hf_im[pl.ds(offp, SEGS), :])
                return n_re, n_im, hc[2] + hp_re * n_re + hp_im * n_im, hc[3] + hp_re * n_im - hp_im * n_re

            fin = lax.fori_loop(0, tt, step, (st_re[...], st_im[...], dlre_o[...], dlim_o[...]))
            st_re[...] = fin[0]
            st_im[...] = fin[1]
            dlre_o[...] = fin[2]
            dlim_o[...] = fin[3]

            gr16, gi16 = g_re[...].astype(BF16), g_im[...].astype(BF16)
            du = jnp.dot(gr16, bre_r[...], preferred_element_type=F32) + jnp.dot(gi16, bim_r[...], preferred_element_type=F32)
            du_o[...] = du + d_r[...] * dy
            ub16 = ub.astype(BF16)
            parts = [
                (dbre_o, lax.dot_general(ub16, gr16, _DN["tn"], preferred_element_type=F32)),
                (dbim_o, lax.dot_general(ub16, gi16, _DN["tn"], preferred_element_type=F32)),
                (dcre_o, lax.dot_general(hre_r[...], dy16, _DN["tn"], preferred_element_type=F32)),
                (dcim_o, lax.dot_general(him_r[...], dy16, _DN["tn"], preferred_element_type=F32)),
                (dd_o, jnp.sum(dy * ub, axis=0, keepdims=True)),
            ]
            for ref, val in parts:
                @pl.when(ch == 0)
                def _(ref=ref, val=val):
                    ref[...] = val

                @pl.when(ch > 0)
                def _(ref=ref, val=val):
                    ref[...] += val

    def chunk(c):
        return nch - 1 - c

    def pass1(ps, c):
        return jnp.where(ps == 1, chunk(c), chunk(0))

    u_spec = pl.BlockSpec((rows, U_BLK), lambda j, ps, c: (chunk(c), j))
    h_spec = pl.BlockSpec((rows, SCAN_WC), lambda j, ps, c: (pass1(ps, c), j))
    prev_spec = pl.BlockSpec((half, SCAN_WC), lambda j, ps, c: (jnp.maximum(pass1(ps, c) * (rows // half) - 1, 0), j))
    e_spec = pl.BlockSpec((SEGS, SCAN_WC), lambda j, ps, c: (0, j))
    d_spec = pl.BlockSpec((1, U_BLK), lambda j, ps, c: (0, j))
    bt_spec = pl.BlockSpec((None, SCAN_WC, U_BLK), lambda j, ps, c: (j, 0, 0))
    ct_spec = pl.BlockSpec((None, U_BLK, SCAN_WC), lambda j, ps, c: (j, 0, 0))
    l_spec = pl.BlockSpec((1, SCAN_WC), lambda j, ps, c: (0, j))
    du_spec = pl.BlockSpec((rows, U_BLK), lambda j, ps, c: (pass1(ps, c), j))
    return _run(
        body, [dyg_s, ys, u_s, h_re, h_im, h_re, h_im, hin_re, hin_im, dvec, w_bre_t, w_bim_t, w_cre_t, w_cim_neg_t, lre, lim],
        carry=carry, name=name, grid=(BD, 2, nch),
        in_specs=[u_spec, u_spec, u_spec, h_spec, h_spec, prev_spec, prev_spec, e_spec, e_spec, d_spec, bt_spec, bt_spec,
                  ct_spec, ct_spec, l_spec, l_spec],
        out_specs=[du_spec, ct_spec, ct_spec, bt_spec, bt_spec, e_spec, e_spec, d_spec],
        out_shape=[jax.ShapeDtypeStruct((s, SSM_W), F32)] + [jax.ShapeDtypeStruct((BD, U_BLK, SCAN_WC), F32)] * 2
        + [jax.ShapeDtypeStruct((BD, SCAN_WC, U_BLK), F32)] * 2 + [jax.ShapeDtypeStruct((SEGS, STATE_W), F32)] * 2
        + [jax.ShapeDtypeStruct((1, SSM_W), F32)],
        scratch_shapes=[pltpu.VMEM((SEGS, SCAN_WC), F32)] * 2 + [pltpu.VMEM((rows, SCAN_WC), F32)] * 6,
        compiler_params=_cparams(("parallel", "arbitrary", "arbitrary")),
    )


FWD_BD = 4


def _block_diag(m, nb=BD):
    g, r, c = m.shape
    m = m.reshape(nb, g // nb, r, c)
    eye = jnp.eye(g // nb, dtype=m.dtype)
    return jnp.einsum("jarc,ab->jarbc", m, eye).reshape(nb, (g // nb) * r, (g // nb) * c)


def _block_diag_extract(m, r, c):
    per = m.shape[1] // r
    m = m.reshape(BD, per, r, per, c)
    return jnp.einsum("jarac->jarc", m).reshape(BD * per, r, c)


def to_segments(a):
    s, w = a.shape
    return a.reshape(SEGS, s // SEGS, w).transpose(1, 0, 2).reshape(s, w)


def from_segments(a):
    s, w = a.shape
    return a.reshape(s // SEGS, SEGS, w).transpose(1, 0, 2).reshape(s, w)


W_IN_CHUNK_ROWS = (512, 512, 512, 512)


def _row_chunks(blocks, sizes):
    assert sum(sizes) == blocks.shape[1]
    out, at = [], 0
    for n in sizes:
        out.append(AllToAll([blocks[:, at:at + n]]))
        at += n
    return out
TALL_TM = 2048
FFN_TN = 512


def local_step(x, target, shards, small):
    s = x.shape[0]
    g1, g2, g3, g4 = (small[k].reshape(1, D_MODEL) for k in ("norm_mix_pre", "norm_mix_post", "norm_ffn_pre", "norm_ffn_post"))
    dvec = small["ssm_d"].reshape(1, SSM_W)
    wts, recv = {}, {}

    def gathered(names, blocks):
        for n, b in zip(names, blocks):
            wts[n] = _full_from_gathered(b, n)

    def rms_in_fn(r, c):
        hh = _rms(r[0], c[0])[0].astype(BF16)
        return [hh, _permute(_perm_matrix(PERM_TS, 4, False), hh), _permute(_perm_matrix(PERM_TS, 16, False), hh)], []

    (h, h4, h16), got = rowwise("rms_in", rms_in_fn, [x], [g1], [(D_MODEL, BF16), (D_MODEL, BF16, 4), (D_MODEL, BF16, 16)],
                                ts=PERM_TS, carry=Gather([shards["w_in"]]))
    w_in_t = _full_from_gathered(got[0], "w_in")
    w_u_t, w_gates_t = w_in_t[3 * HQ:3 * HQ + SSM_W], w_in_t[3 * HQ + SSM_W:]

    def qkv_rows(g):
        return 3 * GROUP_W, lambda t: 3 * t + g

    hd = [h.reshape(1, s, D_MODEL), h4, h16]
    qkv = [None] * 3
    names = ("w_attn_up", "w_glu_v", "w_glu_g")
    qkv[0], got = mm([(hd[0].reshape(s, D_MODEL), w_in_t)], "nt", BF16, "mm_qkv0", tm=TALL_TM, tn=GROUP_W, b_window=qkv_rows(0),
                     carry=Gather([shards[n] for n in names]))
    gathered(names, got)
    qkv[1], got = mm([(hd[1].reshape(s, D_MODEL), w_in_t)], "nt", BF16, "mm_qkv1", tm=TALL_TM, tn=GROUP_W, b_window=qkv_rows(1),
                     carry=Gather([shards["w_out"]]))
    gathered(("w_out",), got)
    qkv[2] = mm([(hd[2].reshape(s, D_MODEL), w_in_t)], "nt", BF16, "mm_qkv2", tm=TALL_TM, tn=GROUP_W, b_window=qkv_rows(2))
    u = mm([(h, w_u_t)], "nt", F32, "mm_u")
    gates, got = mm([(h, w_gates_t)], "nt", BF16, "mm_gates", carry=Gather([shards["w_ffn_gate"]]))
    gathered(("w_ffn_gate",), got)

    outs, lses = [], []
    for g, (_, dil) in enumerate(ATTN_GROUPS):
        o, l = attn_fwd(qkv[g].reshape(dil, s // dil, 3 * GROUP_W), g, f"attn_fwd{g}")
        outs.append(o.reshape(s, GROUP_W) if dil == 1 else o)
        lses.append(l.reshape(s, GROUP_W) if dil == 1 else l)

    def natural(r):
        back4, back16 = _perm_matrix(PERM_TS, 4, True), _perm_matrix(PERM_TS, 16, True)
        return (r[0], _permute(back4, r[1].astype(BF16)), _permute(back16, r[2].astype(BF16)),
                r[3], _permute(back4, r[4]), _permute(back16, r[5]))

    def merge_fn(r, c):
        o0, o1, o2, l0, l1, l2 = natural(r)
        w0, w1, w2 = _mix_weights(l0, l1, l2)
        return [w0 * o0 + w1 * o1 + w2 * o2], []

    (attn,) = rowwise("attn_merge", merge_fn, outs + lses, [], [(GROUP_W, BF16)], ts=PERM_TS)
    attn_branch = mm([(attn, wts["w_attn_up"])], "nn", BF16, "mm_up", tm=TALL_TM)

    are3 = small["ssm_a_re"].reshape(SSM_GROUPS, SSM_STATE, 1)
    aim3 = small["ssm_a_im"].reshape(SSM_GROUPS, SSM_STATE, 1)
    ldt3 = small["ssm_log_dt"].reshape(SSM_GROUPS, 1, 1)
    bre3 = small["ssm_b_re"].reshape(SSM_GROUPS, SSM_STATE, SSM_GROUP)
    bim3 = small["ssm_b_im"].reshape(SSM_GROUPS, SSM_STATE, SSM_GROUP)
    cre3 = small["ssm_c_re"].reshape(SSM_GROUPS, SSM_GROUP, SSM_STATE)
    cim3 = small["ssm_c_im"].reshape(SSM_GROUPS, SSM_GROUP, SSM_STATE)
    lre3, lim3, bbre, bbim = ssm_prep(are3, aim3, ldt3, bre3, bim3)
    lre, lim = lre3.reshape(1, STATE_W), lim3.reshape(1, STATE_W)
    w_bre = _block_diag(bbre.transpose(0, 2, 1)).astype(BF16)
    w_bim = _block_diag(bbim.transpose(0, 2, 1)).astype(BF16)
    w_cre = _block_diag(cre3.transpose(0, 2, 1)).astype(BF16)
    w_cim = _block_diag(cim3.transpose(0, 2, 1)).astype(BF16)
    u_s = to_segments(u)
    fwd_w = [_block_diag(t.transpose(0, 2, 1), FWD_BD).astype(BF16) for t in (bbre, bbim, cre3, -cim3)]
    (yg_s, y_ssm, h_re, h_im, hin_re, hin_im), got = ssm_fwd(
        u_s, dvec, *fwd_w, lre, lim, "ssm_fwd", carry=Gather([shards["w_ffn_up"]]))
    gathered(("w_ffn_up",), got)
    yg = from_segments(yg_s)
    gv = mm([(yg, wts["w_glu_v"])], "nn", BF16, "mm_glu_v", tm=TALL_TM)
    gg = mm([(yg, wts["w_glu_g"])], "nn", BF16, "mm_glu_g", tm=TALL_TM)

    def gate_fn(r, c):
        gts, ab, gv_, gg_ = r
        sa, ss = _sigmoid(gts[:, :D_MODEL]), _sigmoid(gts[:, D_MODEL:])
        return [sa * ab + ss * (gv_ * _sigmoid(gg_))], []

    (merged,) = rowwise("gate_merge", gate_fn, [gates, attn_branch, gv, gg], [], [(D_MODEL, BF16)])
    o_mix = mm([(merged, wts["w_out"])], "nn", F32, "mm_out")

    def mid_fn(r, c):
        x1 = r[0] + _rms(r[1], c[0])[0]
        return [x1, _rms(x1, c[1])[0]], []

    x1, h2 = rowwise("rms_mid", mid_fn, [x, o_mix], [g2, g3], [(D_MODEL, F32), (D_MODEL, BF16)])
    (fa, fb, fin), got = mm([(h2, wts["w_ffn_gate"]), (h2, wts["w_ffn_up"])], "nt", [BF16, BF16, BF16], "mm_ffn_in", tn=FFN_TN,
                            epilogue=lambda p, e: [p[0], p[1], p[0] * _sigmoid(p[0]) * p[1]],
                            carry=Gather([shards["w_ffn_down"]]))
    gathered(("w_ffn_down",), got)
    f = mm([(fin, wts["w_ffn_down"])], "nn", F32, "mm_ffn_down", tn=512, tk=D_FF)

    def loss_fn(r, c):
        x1_, f_, tgt = r
        y, n, rr = _rms(f_, c[0])
        err = x1_ + y - tgt
        dout = err * (1.0 / D_MODEL)
        df, dg = _rms_bwd(dout, n, rr, c[0])
        lp = 0.5 * jnp.sum(jnp.sum(err * err, axis=-1, keepdims=True) * (1.0 / D_MODEL), axis=0, keepdims=True)
        return [df, dout], [dg, lp]

    df, dout, dg4, loss_part = rowwise("loss_bwd", loss_fn, [x1, f, target], [g4], [(D_MODEL, BF16), (D_MODEL, BF16)],
                                       acc_outs=[(1, D_MODEL), (1, 1)])
    def sent(names, blocks):
        for n, b in zip(names, blocks):
            recv[n] = b

    def to_owners(names, dws):
        return AllToAll([_split_for_devices(d, n) for n, d in zip(names, dws)])

    def swiglu_bwd(p, e):
        dfin_, (a, b) = p[0], e
        sg = _sigmoid(a)
        return [dfin_ * b * (sg * (1.0 + a * (1.0 - sg))), dfin_ * a * sg]

    da, db = mm([(df, wts["w_ffn_down"])], "nt", [BF16, BF16], "mm_d_fin", tn=FFN_TN, epilogue=swiglu_bwd, extras=[fa, fb])
    dw_ffn_down = mm([(fin, df)], "tn", BF16, "mm_dw_ffn_down")
    dh2, got = mm([(da, wts["w_ffn_gate"]), (db, wts["w_ffn_up"])], "nn", F32, "mm_d_h2", tm=512, tn=1024, tk=D_FF // 2,
                  carry=to_owners(["w_ffn_down"], [dw_ffn_down]))
    sent(["w_ffn_down"], got)
    dw_ffn_gate = mm([(da, h2)], "tn", BF16, "mm_dw_ffn_gate")
    dw_ffn_up, got = mm([(db, h2)], "tn", BF16, "mm_dw_ffn_up", carry=to_owners(["w_ffn_gate"], [dw_ffn_gate]))
    sent(["w_ffn_gate"], got)

    def mid_bwd(r, c):
        dh2_, dout_, x1_, o_ = r
        _, n3, r3 = _rms(x1_, c[1])
        dx1, dg3_ = _rms_bwd(dh2_, n3, r3, c[1])
        dx1 = dx1 + dout_
        _, n2, r2 = _rms(o_, c[0])
        do_, dg2_ = _rms_bwd(dx1, n2, r2, c[0])
        return [dx1, do_], [dg2_, dg3_]

    dx1, do_mix, dg2, dg3 = rowwise("rms_mid_bwd", mid_bwd, [dh2, dout, x1, o_mix], [g2, g3], [(D_MODEL, F32), (D_MODEL, BF16)],
                                    acc_outs=[(1, D_MODEL), (1, D_MODEL)])
    dmerged = mm([(do_mix, wts["w_out"])], "nt", BF16, "mm_d_merged")
    dw_out = mm([(merged, do_mix)], "tn", BF16, "mm_dw_out")

    def gate_bwd(r, c):
        dm, gts, ab, gv_, gg_ = r
        sa, ss, sg = _sigmoid(gts[:, :D_MODEL]), _sigmoid(gts[:, D_MODEL:]), _sigmoid(gg_)
        branch = gv_ * sg
        dbranch = dm * ss
        dgates = jnp.concatenate([dm * ab * sa * (1.0 - sa), dm * branch * ss * (1.0 - ss)], axis=-1)
        return [dgates, dm * sa, dbranch * sg, dbranch * gv_ * sg * (1.0 - sg)], []

    dgates, dab, dgv, dgg = rowwise("gate_bwd", gate_bwd, [dmerged, gates, attn_branch, gv, gg], [],
                                    [(2 * D_MODEL, BF16), (D_MODEL, BF16), (D_MODEL, BF16), (D_MODEL, BF16)])
    dattn = mm([(dab, wts["w_attn_up"])], "nt", F32, "mm_d_attn")
    dw_up = mm([(attn, dab)], "tn", BF16, "mm_dw_up")
    dyg = mm([(dgv, wts["w_glu_v"]), (dgg, wts["w_glu_g"])], "nt", F32, "mm_d_yg")
    dw_glu_v = mm([(yg, dgv)], "tn", BF16, "mm_dw_glu_v")
    dw_glu_g = mm([(yg, dgg)], "tn", BF16, "mm_dw_glu_g")

    names = ["w_ffn_up", "w_out", "w_attn_up", "w_glu_v", "w_glu_g"]
    (du_s, dbre_d, dbim_d, dcre_d, dcim_d, dl_re8, dl_im8, dd_ssm), got = ssm_bwd(
        to_segments(dyg), y_ssm, u_s, h_re, h_im, hin_re, hin_im, dvec, w_bre.transpose(0, 2, 1), w_bim.transpose(0, 2, 1),
        w_cre.transpose(0, 2, 1), -w_cim.transpose(0, 2, 1), lre, lim, "ssm_bwd",
        carry=to_owners(names, [dw_ffn_up, dw_out, dw_up, dw_glu_v, dw_glu_g]))
    sent(names, got)
    dbb_re = _block_diag_extract(dbre_d, SSM_GROUP, SSM_STATE).transpose(0, 2, 1)
    dbb_im = _block_diag_extract(dbim_d, SSM_GROUP, SSM_STATE).transpose(0, 2, 1)
    dc_re = _block_diag_extract(dcre_d, SSM_STATE, SSM_GROUP).transpose(0, 2, 1)
    dc_im = -_block_diag_extract(dcim_d, SSM_STATE, SSM_GROUP).transpose(0, 2, 1)

    def fold8(r, c):
        return [], [jnp.sum(r[0], axis=0, keepdims=True), jnp.sum(r[1], axis=0, keepdims=True)]

    dl_re, dl_im = rowwise("ssm_dl_fold", fold8, [dl_re8, dl_im8], [], [], acc_outs=[(1, STATE_W), (1, STATE_W)], ts=SEGS)
    da_re, da_im, dldt, db_re, db_im = ssm_prep_bwd(
        are3, aim3, ldt3, bre3, bim3, dbb_re, dbb_im,
        dl_re.reshape(SSM_GROUPS, SSM_STATE, 1), dl_im.reshape(SSM_GROUPS, SSM_STATE, 1))
    du = from_segments(du_s)

    def merge_bwd(r, c):
        dat = r[0]
        o0, o1, o2, l0, l1, l2 = natural(r[1:])
        w0, w1, w2 = _mix_weights(l0, l1, l2)
        tot = _head_sum(dat * (w0 * o0 + w1 * o1 + w2 * o2))
        to4, to16 = _perm_matrix(PERM_TS, 4, False), _perm_matrix(PERM_TS, 16, False)
        return [w0 * dat, _permute(to4, (w1 * dat).astype(BF16)), _permute(to16, (w2 * dat).astype(BF16)),
                w0 * tot, _permute(to4, (w1 * tot).astype(BF16)), _permute(to16, (w2 * tot).astype(BF16))], []

    mb = rowwise("attn_merge_bwd", merge_bwd, [dattn] + outs + lses, [],
                 [(GROUP_W, BF16), (GROUP_W, BF16, 4), (GROUP_W, BF16, 16), (GROUP_W, BF16), (GROUP_W, BF16, 4), (GROUP_W, BF16, 16)],
                 ts=PERM_TS)
    dqs, dw_qkv = [], []
    for g, (_, dil) in enumerate(ATTN_GROUPS):
        dq = attn_bwd(qkv[g].reshape(dil, s // dil, 3 * GROUP_W), mb[g].reshape(dil, s // dil, GROUP_W),
                      lses[g].reshape(dil, s // dil, GROUP_W), mb[3 + g].reshape(dil, s // dil, GROUP_W),
                      g, f"attn_bwd{g}").reshape(s, 3 * GROUP_W)
        dqs.append(dq)
        dw_qkv.append(mm([(hd[g].reshape(s, D_MODEL), dq)], "tn", BF16, f"mm_dw_qkv{g}"))
    dw_u = mm([(h, du)], "tn", BF16, "mm_dw_u")
    dw_gates = mm([(h, dgates)], "tn", BF16, "mm_dw_gates")
    dw_in = jnp.concatenate(
        [dw_qkv[g][:, o * GROUP_W:(o + 1) * GROUP_W] for o in range(3) for g in range(3)] + [dw_u, dw_gates], axis=1)
    chunks = _row_chunks(_split_for_devices(dw_in, "w_in"), W_IN_CHUNK_ROWS)
    dh_parts, got_chunks = [], []
    for g, (_, dil) in enumerate(ATTN_GROUPS):
        dh_g, got = mm([(dqs[g], w_in_t)], "nn", BF16, f"mm_d_h_qkv{g}", tk=GROUP_W, b_window=qkv_rows(g), carry=chunks[g])
        got_chunks.append(got[0])
        dh_parts.append(dh_g if dil == 1 else dh_g.reshape(dil, s // dil, D_MODEL))
    dh_parts.append(mm([(du, w_u_t)], "nn", BF16, "mm_d_h_u"))
    dh_gates, got = mm([(dgates, w_gates_t)], "nn", BF16, "mm_d_h_gates", carry=chunks[3])
    got_chunks.append(got[0])
    dh_parts.append(dh_gates)
    recv["w_in"] = jnp.concatenate(got_chunks, axis=1)

    def in_bwd(r, c):
        dh1 = _permute(_perm_matrix(PERM_TS, 4, True), r[1].astype(BF16))
        dh2_ = _permute(_perm_matrix(PERM_TS, 16, True), r[2].astype(BF16))
        dh = r[0] + dh1 + dh2_ + r[3] + r[4]
        _, n1, r1 = _rms(r[6], c[0])
        dx, dg1_ = _rms_bwd(dh, n1, r1, c[0])
        return [dx + r[5]], [dg1_]

    grad_x, dg1 = rowwise("rms_in_bwd", in_bwd, dh_parts + [dx1, x], [g1], [(D_MODEL, F32)], acc_outs=[(1, D_MODEL)], ts=PERM_TS)

    dsmall = dict(norm_mix_pre=dg1, ssm_a_re=da_re, ssm_a_im=da_im, ssm_log_dt=dldt, ssm_b_re=db_re, ssm_b_im=db_im,
                  ssm_c_re=dc_re, ssm_c_im=dc_im, ssm_d=dd_ssm, norm_mix_post=dg2, norm_ffn_pre=dg3, norm_ffn_post=dg4)
    return loss_part, grad_x, recv, dsmall


def adamw(parts, w, m, v, name, carry=None):
    r, c = w.shape
    tr = r
    while tr > 8 and tr % 2 == 0 and tr * c * (8 * parts.dtype.itemsize + 28) * 2 > 24 * 1024 * 1024:
        tr //= 2
    assert r % tr == 0 and (tr % 8 == 0 or tr == r)
    c1, c2 = 1.0 / (1.0 - ADAM_B1 ** ADAM_STEP), 1.0 / (1.0 - ADAM_B2 ** ADAM_STEP)

    def body(p_ref, w_ref, m_ref, v_ref, g_o, d_o, m_o, v_o):
        g = p_ref[0].astype(F32)
        for i in range(1, N_DEV):
            g = g + p_ref[i].astype(F32)
        mn = ADAM_B1 * m_ref[...] + (1.0 - ADAM_B1) * g
        vn = ADAM_B2 * v_ref[...] + (1.0 - ADAM_B2) * (g * g)
        g_o[...] = g
        m_o[...] = mn
        v_o[...] = vn
        d_o[...] = -ADAM_LR * ((mn * c1) / (jnp.sqrt(vn * c2) + ADAM_EPS) + ADAM_WD * w_ref[...])

    blk = pl.BlockSpec((tr, c), lambda i: (i, 0))
    return _run(
        body, [parts, w, m, v], carry=carry, name=name, grid=(r // tr,),
        in_specs=[pl.BlockSpec((N_DEV, tr, c), lambda i: (0, i, 0)), blk, blk, blk],
        out_specs=[blk] * 4, out_shape=[jax.ShapeDtypeStruct((r, c), F32)] * 4, compiler_params=_cparams(("parallel",)),
    )


PACK_C = 1024
SHARDED = ("w_in", "w_attn_up", "w_glu_v", "w_glu_g", "w_out", "w_ffn_gate", "w_ffn_up", "w_ffn_down")
ROW_SHARDED = ("w_out", "w_ffn_down")
SENT_TRANSPOSED = ("w_in", "w_ffn_gate", "w_ffn_up")
GRAD_TRANSPOSED = ("w_ffn_gate", "w_ffn_up")
SMALL = ("norm_mix_pre", "ssm_a_re", "ssm_a_im", "ssm_log_dt", "ssm_b_re", "ssm_b_im", "ssm_c_re", "ssm_c_im", "ssm_d",
         "norm_mix_post", "norm_ffn_pre", "norm_ffn_post")
WEIGHTS = ("norm_mix_pre", "w_in", "w_attn_up", "ssm_a_re", "ssm_a_im", "ssm_log_dt", "ssm_b_re", "ssm_b_im", "ssm_c_re",
           "ssm_c_im", "ssm_d", "w_glu_v", "w_glu_g", "w_out", "norm_mix_post", "norm_ffn_pre", "w_ffn_gate", "w_ffn_up",
           "w_ffn_down", "norm_ffn_post")


def _pack(arrs, dtype, pad_rows_to=64):
    flat = jnp.concatenate([a.reshape(-1).astype(dtype) for a in arrs])
    n = flat.shape[0]
    rows = -(-n // PACK_C)
    rows = -(-rows // pad_rows_to) * pad_rows_to
    return jnp.pad(flat, (0, rows * PACK_C - n)).reshape(rows, PACK_C)


def _unpack(flat2d, shapes):
    flat = flat2d.reshape(-1)
    out, off = [], 0
    for shp in shapes:
        n = int(np.prod(shp))
        out.append(flat[off:off + n].reshape(shp))
        off += n
    return out


def _full_from_gathered(gathered, name):
    if name in ROW_SHARDED or name in SENT_TRANSPOSED:
        return gathered.reshape(-1, gathered.shape[2])
    return gathered.transpose(1, 0, 2).reshape(gathered.shape[1], -1)


def _split_for_devices(full, name):
    if name in ROW_SHARDED or name in GRAD_TRANSPOSED:
        return full.reshape(N_DEV, -1, full.shape[1])
    return full.reshape(full.shape[0], N_DEV, -1).transpose(1, 0, 2)


def kernel(x, norm_mix_pre, w_in, w_attn_up, ssm_a_re, ssm_a_im, ssm_log_dt, ssm_b_re, ssm_b_im, ssm_c_re, ssm_c_im, ssm_d, w_glu_v, w_glu_g, w_out, norm_mix_post, norm_ffn_pre, w_ffn_gate, w_ffn_up, w_ffn_down, norm_ffn_post, loss_target, m_norm_mix_pre, m_w_in, m_w_attn_up, m_ssm_a_re, m_ssm_a_im, m_ssm_log_dt, m_ssm_b_re, m_ssm_b_im, m_ssm_c_re, m_ssm_c_im, m_ssm_d, m_w_glu_v, m_w_glu_g, m_w_out, m_norm_mix_post, m_norm_ffn_pre, m_w_ffn_gate, m_w_ffn_up, m_w_ffn_down, m_norm_ffn_post, v_norm_mix_pre, v_w_in, v_w_attn_up, v_ssm_a_re, v_ssm_a_im, v_ssm_log_dt, v_ssm_b_re, v_ssm_b_im, v_ssm_c_re, v_ssm_c_im, v_ssm_d, v_w_glu_v, v_w_glu_g, v_w_out, v_norm_mix_post, v_norm_ffn_pre, v_w_ffn_gate, v_w_ffn_up, v_w_ffn_down, v_norm_ffn_post):
    args = dict(locals())
    wv = {n: args[n][0] for n in WEIGHTS}
    mv = {n: args["m_" + n][0] for n in WEIGHTS}
    vv = {n: args["v_" + n][0] for n in WEIGHTS}

    shards = {n: (wv[n].T if n in SENT_TRANSPOSED else wv[n]).astype(BF16) for n in SHARDED}
    small = {n: wv[n] for n in SMALL}
    loss_part, grad_x, recv, dsmall = local_step(x[0], loss_target[0], shards, small)
    for n in GRAD_TRANSPOSED:
        recv[n] = recv[n].transpose(0, 2, 1)

    small_shapes = [wv[n].shape for n in SMALL]
    res = {}
    res["w_in"], (sgather,) = adamw(recv["w_in"], wv["w_in"], mv["w_in"], vv["w_in"], "adamw_w_in",
                                    carry=Gather([_pack([dsmall[n] for n in SMALL], F32)]))
    for n in SHARDED[1:]:
        res[n] = adamw(recv[n], wv[n], mv[n], vv[n], "adamw_" + n)
    sres = adamw(sgather, _pack([wv[n] for n in SMALL], F32), _pack([mv[n] for n in SMALL], F32),
                 _pack([vv[n] for n in SMALL], F32), "adamw_small")
    sun = [_unpack(t, small_shapes) for t in sres]
    for k, n in enumerate(SMALL):
        res[n] = tuple(sun[t][k] for t in range(4))

    loss = lax.psum(loss_part[0, 0], ("x", "y", "c"))
    outs = [loss, grad_x[None]]
    for t in range(4):
        outs += [res[n][t][None] for n in WEIGHTS]
    return tuple(outs)
```

```python
import functools
import math

import numpy as np
import jax
import jax.numpy as jnp
from jax import lax
from jax.experimental import pallas as pl
from jax.experimental.pallas import tpu as pltpu

F32 = jnp.float32
BF16 = jnp.bfloat16

D_MODEL = 2048
HEAD_DIM = 128
HEADS_PER_GROUP = 4
ATTN_GROUPS = ((128, 1), (512, 4), (2048, 16))
N_HEADS = HEADS_PER_GROUP * len(ATTN_GROUPS)
GROUP_W = HEADS_PER_GROUP * HEAD_DIM
HQ = N_HEADS * HEAD_DIM
SSM_W = 1024
SSM_GROUP = 16
SSM_GROUPS = 64
SSM_STATE = 64
STATE_W = SSM_GROUPS * SSM_STATE
D_FF = 5632
EPS = 1e-6
N_DEV = 8
SEGS = 8
BD = 8

ADAM_LR, ADAM_B1, ADAM_B2, ADAM_EPS, ADAM_WD, ADAM_STEP = 0.001, 0.9, 0.999, 1e-08, 0.01, 10

VMEM_LIMIT = 56 * 1024 * 1024
HBM_SPEC = pl.BlockSpec(memory_space=pltpu.HBM)
MESH_ID = pl.DeviceIdType.MESH
NEG = -1e30


def _pcall(body, **kw):
    return pl.pallas_call(body, **kw)


def _cparams(sem=None):
    if sem is None:
        return pltpu.CompilerParams(vmem_limit_bytes=VMEM_LIMIT)
    return pltpu.CompilerParams(vmem_limit_bytes=VMEM_LIMIT, dimension_semantics=sem)


def _my_coords():
    return lax.axis_index("x"), lax.axis_index("y"), lax.axis_index("c")


class Gather:
    def __init__(self, xs):
        self.arrays = list(xs)
        self.out_shapes = [jax.ShapeDtypeStruct((N_DEV,) + x.shape, x.dtype) for x in xs]

    def _ctx(self, out_refs, send_sems, recv_sems):
        mx, my, mc = _my_coords()
        me, sibling = (mx, my, mc), (mx, my, 1 - mc)
        chips = [(1 - mx, my), (mx, 1 - my), (1 - mx, 1 - my)]

        def slot(a, px, py, pc):
            return out_refs[a].at[4 * px + 2 * py + pc]

        def copy(a, k, block, to, src=None):
            return pltpu.make_async_remote_copy(
                src_ref=slot(a, *block) if src is None else src, dst_ref=slot(a, *block),
                send_sem=send_sems.at[7 * a + k], recv_sem=recv_sems.at[7 * a + k], device_id=to, device_id_type=MESH_ID)

        return me, sibling, chips, mc, slot, copy

    def _first(self, a, x_refs, ctx):
        me, sibling, chips, mc, slot, copy = ctx
        return [copy(a, 0, me, sibling, src=x_refs[a])] + [copy(a, 1 + j, me, (*chip, mc), src=x_refs[a]) for j, chip in enumerate(chips)]

    def start(self, x_refs, out_refs, send_sems, recv_sems, local_sems):
        ctx = self._ctx(out_refs, send_sems, recv_sems)
        me, slot = ctx[0], ctx[4]
        for a in range(len(self.arrays)):
            pltpu.make_async_copy(x_refs[a], slot(a, *me), local_sems.at[a]).start()
            for cp in self._first(a, x_refs, ctx):
                cp.start()

    def _passed(self, ctx):
        me, sibling, chips, mc, slot, copy = ctx
        return [copy(a, 4 + j, (*chip, mc), sibling) for a in range(len(self.arrays)) for j, chip in enumerate(chips)]

    def middle(self, x_refs, out_refs, send_sems, recv_sems, local_sems):
        ctx = self._ctx(out_refs, send_sems, recv_sems)
        me, sibling, chips, mc, slot, copy = ctx
        for a in range(len(self.arrays)):
            for j, chip in enumerate(chips):
                copy(a, 1 + j, (*chip, mc), me).wait_recv()
                copy(a, 4 + j, (*chip, mc), sibling).start()

    def finish(self, x_refs, out_refs, send_sems, recv_sems, local_sems, passed_on=False):
        if not passed_on:
            self.middle(x_refs, out_refs, send_sems, recv_sems, local_sems)
        ctx = self._ctx(out_refs, send_sems, recv_sems)
        me, sibling, chips, mc, slot, copy = ctx
        na = len(self.arrays)
        passed = self._passed(ctx)
        for a in range(na):
            copy(a, 0, sibling, me).wait_recv()
            for j, chip in enumerate(chips):
                copy(a, 4 + j, (*chip, 1 - mc), me).wait_recv()
        for a in range(na):
            for cp in self._first(a, x_refs, ctx):
                cp.wait_send()
        for cp in passed:
            cp.wait_send()
        for a in range(na):
            pltpu.make_async_copy(x_refs[a], slot(a, *me), local_sems.at[a]).wait()


class AllToAll:
    def __init__(self, ps):
        self.arrays = list(ps)
        self.out_shapes = [jax.ShapeDtypeStruct(p.shape, p.dtype) for p in ps]

    def _copies(self, p_refs, out_refs, send_sems, recv_sems, local_sems):
        mx, my, mc = _my_coords()
        me = 4 * mx + 2 * my + mc
        local, remote = [], []
        for a in range(len(self.arrays)):
            local.append(pltpu.make_async_copy(p_refs[a].at[me], out_refs[a].at[me], local_sems.at[a]))
            for k in range(1, N_DEV):
                px, py, pc = mx ^ ((k >> 2) & 1), my ^ ((k >> 1) & 1), mc ^ (k & 1)
                remote.append(pltpu.make_async_remote_copy(
                    src_ref=p_refs[a].at[4 * px + 2 * py + pc], dst_ref=out_refs[a].at[me],
                    send_sem=send_sems.at[7 * a + k - 1], recv_sem=recv_sems.at[7 * a + k - 1],
                    device_id=(px, py, pc), device_id_type=MESH_ID))
        return local, remote

    def start(self, *refs):
        local, remote = self._copies(*refs)
        for cp in local + remote:
            cp.start()

    def finish(self, *refs):
        local, remote = self._copies(*refs)
        for cp in remote:
            cp.wait_recv()
        for cp in remote:
            cp.wait_send()
        for cp in local:
            cp.wait()


def _run(body, args, carry=None, **kw):
    if carry is None:
        return _pcall(body, **kw)(*args)
    grid = kw["grid"]
    single = not isinstance(kw["out_shape"], (list, tuple))
    in_specs = list(kw["in_specs"])
    out_specs = [kw["out_specs"]] if single else list(kw["out_specs"])
    out_shape = [kw["out_shape"]] if single else list(kw["out_shape"])
    scratch = list(kw.get("scratch_shapes", []))
    na, nin, nout, nscr = len(carry.arrays), len(in_specs), len(out_specs), len(scratch)
    steps = int(np.prod(grid))
    mid_step = (steps * 7) // 10 if hasattr(carry, "middle") and steps >= 4 else None

    def carried(*refs):
        ins, cin = refs[:nin], refs[nin:nin + na]
        outs, cout = refs[nin + na:nin + na + nout], refs[nin + na + nout:nin + 2 * na + nout]
        scr = refs[nin + 2 * na + nout:nin + 2 * na + nout + nscr]
        sems = refs[nin + 2 * na + nout + nscr:]
        step = pl.program_id(0)
        for i in range(1, len(grid)):
            step = step * grid[i] + pl.program_id(i)

        @pl.when(step == 0)
        def _():
            carry.start(cin, cout, *sems)

        if mid_step is not None:
            @pl.when(step == mid_step)
            def _():
                carry.middle(cin, cout, *sems)

        body(*ins, *outs, *scr)

        @pl.when(step == steps - 1)
        def _():
            if mid_step is not None:
                carry.finish(cin, cout, *sems, passed_on=True)
            else:
                carry.finish(cin, cout, *sems)

    res = _pcall(
        carried, name=kw["name"], grid=grid, in_specs=in_specs + [HBM_SPEC] * na, out_specs=out_specs + [HBM_SPEC] * na,
        out_shape=out_shape + carry.out_shapes,
        scratch_shapes=scratch + [pltpu.SemaphoreType.DMA((7 * na,)), pltpu.SemaphoreType.DMA((7 * na,)), pltpu.SemaphoreType.DMA((na,))],
        compiler_params=_cparams(("arbitrary",) * len(grid)),
    )(*args, *carry.arrays)
    main = res[:nout]
    return (main[0] if single else main), list(res[nout:])


_DN = {"nn": (((1,), (0,)), ((), ())), "nt": (((1,), (1,)), ((), ())), "tn": (((0,), (0,)), ((), ()))}


LANE = 128
MM_TM, MM_TN, MM_TK = 1024, 1536, 2048


def _tile(n, cap):
    for t in range(min(cap, n) // LANE * LANE, 0, -LANE):
        if n % t == 0:
            return t
    raise ValueError(n)


DW_TM, DW_TN, DW_TK = 512, 512, 8192


def mm(pairs, mode, out_dtype, name, tm=None, tn=None, tk=None, carry=None, epilogue=None, extras=(), b_window=None):
    a0, b0 = pairs[0]
    if mode == "nn":
        (m, k), n = a0.shape, b0.shape[1]
    elif mode == "nt":
        (m, k), n = a0.shape, b0.shape[0]
    else:
        (k, m), n = a0.shape, b0.shape[1]
    if b_window is not None:
        assert mode in ("nn", "nt") and len(pairs) == 1
        if mode == "nt":
            n = b_window[0]
        else:
            assert k == b_window[0]
    caps = (DW_TM, DW_TN, DW_TK) if mode == "tn" else (MM_TM, MM_TN, MM_TK)
    tm, tn, tk = _tile(m, tm or caps[0]), _tile(n, tn or caps[1]), _tile(k, tk or caps[2])
    nk = k // tk
    npairs = len(pairs)
    nex = len(extras)
    fused = epilogue is not None
    assert not fused or nk == 1
    out_dtypes = list(out_dtype) if fused else [out_dtype]

    def body(*refs):
        prods = []
        for p in range(npairs):
            a = refs[2 * p][...].astype(BF16) if (p == 0 or pairs[p][0] is not pairs[p - 1][0]) else a
            b = refs[2 * p + 1][...].astype(BF16)
            prods.append(lax.dot_general(a, b, _DN[mode], preferred_element_type=F32))
        if fused:
            ex = [refs[2 * npairs + e][...].astype(F32) for e in range(nex)]
            for o_ref, val in zip(refs[2 * npairs + nex:], epilogue(prods, ex)):
                o_ref[...] = val.astype(o_ref.dtype)
            return
        o_ref = refs[2 * npairs]
        tot = prods[0]
        for d in prods[1:]:
            tot = tot + d
        if nk == 1:
            o_ref[...] = tot.astype(o_ref.dtype)
            return
        acc = refs[2 * npairs + 1]
        kk = pl.program_id(2)

        @pl.when(kk == 0)
        def _():
            acc[...] = tot

        @pl.when(kk > 0)
        def _():
            acc[...] += tot

        @pl.when(kk == nk - 1)
        def _():
            o_ref[...] = acc[...].astype(o_ref.dtype)

    rows_of = b_window[1] if b_window is not None else (lambda t: t)
    if mode == "nn":
        sp = [pl.BlockSpec((tm, tk), lambda i, j, kk: (i, kk)), pl.BlockSpec((tk, tn), lambda i, j, kk: (rows_of(kk), j))]
    elif mode == "nt":
        sp = [pl.BlockSpec((tm, tk), lambda i, j, kk: (i, kk)), pl.BlockSpec((tn, tk), lambda i, j, kk: (rows_of(j), kk))]
    else:
        sp = [pl.BlockSpec((tk, tm), lambda i, j, kk: (kk, i)), pl.BlockSpec((tk, tn), lambda i, j, kk: (kk, j))]
    o_spec = pl.BlockSpec((tm, tn), lambda i, j, kk: (i, j))
    out_shapes = [jax.ShapeDtypeStruct((m, n), dt) for dt in out_dtypes]
    return _run(
        body, [t for pr in pairs for t in pr] + list(extras), carry=carry, name=name, grid=(m // tm, n // tn, nk),
        in_specs=sp * npairs + [o_spec] * nex,
        out_specs=[o_spec] * len(out_shapes) if fused else o_spec,
        out_shape=out_shapes if fused else out_shapes[0],
        scratch_shapes=[pltpu.VMEM((tm, tn), F32)] if nk > 1 else [],
        compiler_params=_cparams(("parallel", "parallel", "arbitrary")),
    )


def rowwise(name, fn, row_ins, const_ins, row_outs, acc_outs=(), ts=None, carry=None):
    s = row_ins[0].shape[0]
    row_outs = [ro if len(ro) == 3 else (*ro, 1) for ro in row_outs]
    if ts is None:
        per_row = sum(a.shape[-1] * a.dtype.itemsize for a in row_ins) + sum(w * jnp.dtype(dt).itemsize for w, dt, _ in row_outs)
        ts = 512
        while ts > 8 and 2 * ts * per_row > 20 * 1024 * 1024:
            ts //= 2
    ts = min(ts, s)
    assert s % ts == 0
    nr, nc, no, na = len(row_ins), len(const_ins), len(row_outs), len(acc_outs)

    def body(*refs):
        rows = [r[...].reshape(ts, r.shape[-1]).astype(F32) for r in refs[:nr]]
        consts = [r[...] for r in refs[nr:nr + nc]]
        outs, accs = fn(rows, consts)
        for r, v in zip(refs[nr + nc:nr + nc + no], outs):
            r[...] = v.astype(r.dtype).reshape(r.shape)
        if na:
            first = pl.program_id(0) == 0
            for r, v in zip(refs[nr + nc + no:], accs):
                @pl.when(first)
                def _(r=r, v=v):
                    r[...] = v

                @pl.when(jnp.logical_not(first))
                def _(r=r, v=v):
                    r[...] += v

    def tile_spec(w, d):
        if d == 1:
            return pl.BlockSpec((ts, w), lambda i: (i, 0))
        return pl.BlockSpec((d, ts // d, w), lambda i: (0, i, 0))

    in_specs = [tile_spec(a.shape[-1], a.shape[0] if a.ndim == 3 else 1) for a in row_ins]
    in_specs += [pl.BlockSpec(c.shape, lambda i, nd=c.ndim: (0,) * nd) for c in const_ins]
    out_specs = [tile_spec(w, d) for w, _, d in row_outs]
    out_specs += [pl.BlockSpec(shp, lambda i, nd=len(shp): (0,) * nd) for shp in acc_outs]
    out_shape = [jax.ShapeDtypeStruct((s, w) if d == 1 else (d, s // d, w), dt) for w, dt, d in row_outs]
    out_shape += [jax.ShapeDtypeStruct(shp, F32) for shp in acc_outs]
    return _run(
        body, [*row_ins, *const_ins], carry=carry, name=name, grid=(s // ts,), in_specs=in_specs, out_specs=out_specs,
        out_shape=out_shape, compiler_params=_cparams(("arbitrary",)),
    )


PERM_TS = 256


def _perm_matrix(ts, d, inverse):
    i = lax.broadcasted_iota(jnp.int32, (ts, ts), 0)
    k = lax.broadcasted_iota(jnp.int32, (ts, ts), 1)
    per = ts // d
    src = (i % d) * per + i // d if inverse else (i % per) * d + i // per
    return jnp.where(k == src, 1.0, 0.0).astype(BF16)


def _permute(p, x):
    if x.dtype == BF16:
        return jnp.dot(p, x, preferred_element_type=F32)
    hi = x.astype(BF16)
    rest = x - hi.astype(F32)
    mid = rest.astype(BF16)
    lo = (rest - mid.astype(F32)).astype(BF16)
    out = jnp.dot(p, hi, preferred_element_type=F32) + jnp.dot(p, mid, preferred_element_type=F32)
    return out + jnp.dot(p, lo, preferred_element_type=F32)


def _rms(x, gain):
    r = lax.rsqrt(jnp.mean(x * x, axis=-1, keepdims=True) + EPS)
    n = x * r
    return n * gain, n, r


def _rms_bwd(dy, n, r, gain):
    dn = dy * gain
    dx = r * (dn - n * jnp.mean(dn * n, axis=-1, keepdims=True))
    return dx, jnp.sum(dy * n, axis=0, keepdims=True)


def _sigmoid(x):
    return 1.0 / (1.0 + jnp.exp(-x))


def _sigmoid_to_bf16(x):
    return pl.reciprocal(1.0 + jnp.exp(-x), approx=True)


_GELU_K = math.sqrt(2.0 / math.pi)


def _gelu(x):
    t = jnp.tanh(_GELU_K * (x + 0.044715 * x * x * x))
    return 0.5 * x * (1.0 + t), t


def _gelu_grad(x, t):
    return 0.5 * (1.0 + t) + 0.5 * x * (1.0 - t * t) * _GELU_K * (1.0 + 3.0 * 0.044715 * x * x)


def _head_sum(x):
    parts = []
    for h in range(HEADS_PER_GROUP):
        sl = x[:, h * HEAD_DIM:(h + 1) * HEAD_DIM]
        parts.append(jnp.broadcast_to(jnp.sum(sl, axis=-1, keepdims=True), sl.shape))
    return jnp.concatenate(parts, axis=-1)


def _mix_weights(l0, l1, l2):
    mx = jnp.maximum(jnp.maximum(l0, l1), l2)
    e0, e1, e2 = jnp.exp(l0 - mx), jnp.exp(l1 - mx), jnp.exp(l2 - mx)
    inv = 1.0 / (e0 + e1 + e2)
    return e0 * inv, e1 * inv, e2 * inv


BLK = 128


def _slopes(g):
    return [2.0 ** (-8.0 * (g * HEADS_PER_GROUP + h + 1) / N_HEADS) for h in range(HEADS_PER_GROUP)]


def _attn_masks(dil):
    qi = lax.broadcasted_iota(jnp.int32, (BLK, BLK), 0)
    ki = lax.broadcasted_iota(jnp.int32, (BLK, BLK), 1)
    dist_c = qi - ki
    dist_p = BLK + qi - ki
    return dist_c >= 0, dist_p <= BLK, (dist_c * dil).astype(F32), (dist_p * dil).astype(F32)


def _window_mask(has_prev, dil):
    qi = lax.broadcasted_iota(jnp.int32, (BLK, 2 * BLK), 0)
    ki = lax.broadcasted_iota(jnp.int32, (BLK, 2 * BLK), 1)
    dist = BLK + qi - ki
    ok = jnp.logical_and(jnp.logical_and(dist >= 0, dist <= BLK), jnp.logical_or(ki >= BLK, has_prev))
    return ok, (dist * dil).astype(F32)


def attn_fwd(qkv, g, name):
    dil, length, _ = qkv.shape
    scale = HEAD_DIM ** -0.5
    slopes = _slopes(g)

    def body(q_ref, kc_ref, vc_ref, kp_ref, vp_ref, o_ref, l_ref):
        ok, dist = _window_mask(pl.program_id(1) > 0, dil)
        for h in range(HEADS_PER_GROUP):
            sl = slice(h * HEAD_DIM, (h + 1) * HEAD_DIM)
            k2 = jnp.concatenate([kp_ref[:, sl], kc_ref[:, sl]], axis=0)
            v2 = jnp.concatenate([vp_ref[:, sl], vc_ref[:, sl]], axis=0)
            s = lax.dot_general(q_ref[:, sl], k2, _DN["nt"], preferred_element_type=F32) * scale - slopes[h] * dist
            s = jnp.where(ok, s, NEG)
            mx = jnp.max(s, axis=-1, keepdims=True)
            p = jnp.exp(s - mx)
            den = jnp.sum(p, axis=-1, keepdims=True)
            o_ref[:, sl] = (jnp.dot(p.astype(BF16), v2, preferred_element_type=F32) / den).astype(BF16)
            l_ref[:, sl] = jnp.broadcast_to(mx + jnp.log(den), (BLK, HEAD_DIM))

    def spec(col, prev):
        if prev:
            return pl.BlockSpec((None, BLK, GROUP_W), lambda r, n: (r, jnp.maximum(n - 1, 0), col))
        return pl.BlockSpec((None, BLK, GROUP_W), lambda r, n: (r, n, col))

    out_spec = pl.BlockSpec((None, BLK, GROUP_W), lambda r, n: (r, n, 0))
    return _pcall(
        body, name=name, grid=(dil, length // BLK),
        in_specs=[spec(0, False), spec(1, False), spec(2, False), spec(1, True), spec(2, True)],
        out_specs=[out_spec, out_spec],
        out_shape=[jax.ShapeDtypeStruct((dil, length, GROUP_W), BF16), jax.ShapeDtypeStruct((dil, length, GROUP_W), F32)],
        compiler_params=_cparams(("parallel", "parallel")),
    )(qkv, qkv, qkv, qkv, qkv)


def attn_bwd(qkv, dout, lse, dd, g, name, carry=None):
    dil, length, _ = qkv.shape
    nblk = length // BLK
    scale = HEAD_DIM ** -0.5
    slopes = _slopes(g)

    def body(q_ref, kc_ref, vc_ref, kp_ref, vp_ref, qn_ref, do_ref, don_ref, l_ref, ln_ref, d_ref, dn_ref, o_ref):
        n = pl.program_id(1)
        ok2, dist2 = _window_mask(n > 0, dil)
        _, ok_p, _, dp = _attn_masks(dil)
        ok_next = jnp.logical_and(ok_p, n < nblk - 1)
        for h in range(HEADS_PER_GROUP):
            sl = slice(h * HEAD_DIM, (h + 1) * HEAD_DIM)
            q, kc, vc, qn = q_ref[:, sl], kc_ref[:, sl], vc_ref[:, sl], qn_ref[:, sl]
            k2 = jnp.concatenate([kp_ref[:, sl], kc], axis=0)
            v2 = jnp.concatenate([vp_ref[:, sl], vc], axis=0)
            do, don = do_ref[:, sl], don_ref[:, sl]
            lse_q, lse_n, dd_q, dd_n = l_ref[:, sl], ln_ref[:, sl], d_ref[:, sl], dn_ref[:, sl]

            def probs(qq, kk, dist, ok, lse_t):
                s = lax.dot_general(qq, kk, _DN["nt"], preferred_element_type=F32) * scale - slopes[h] * dist
                return jnp.where(ok, jnp.exp(jnp.where(ok, s, NEG) - lse_t), 0.0)

            p2 = probs(q, k2, dist2, ok2, jnp.concatenate([lse_q, lse_q], axis=1))
            p_x = probs(qn, kc, dp, ok_next, lse_n)
            ds2 = p2 * (lax.dot_general(do, v2, _DN["nt"], preferred_element_type=F32) - jnp.concatenate([dd_q, dd_q], axis=1))
            ds_x = p_x * (lax.dot_general(don, vc, _DN["nt"], preferred_element_type=F32) - dd_n)
            dq = jnp.dot(ds2.astype(BF16), k2, preferred_element_type=F32)
            ds_k = jnp.concatenate([ds2[:, BLK:], ds_x], axis=0).astype(BF16)
            p_k = jnp.concatenate([p2[:, BLK:], p_x], axis=0).astype(BF16)
            dk = lax.dot_general(ds_k, jnp.concatenate([q, qn], axis=0), _DN["tn"], preferred_element_type=F32)
            dv = lax.dot_general(p_k, jnp.concatenate([do, don], axis=0), _DN["tn"], preferred_element_type=F32)
            o_ref[:, h * HEAD_DIM:(h + 1) * HEAD_DIM] = (dq * scale).astype(BF16)
            o_ref[:, GROUP_W + h * HEAD_DIM:GROUP_W + (h + 1) * HEAD_DIM] = (dk * scale).astype(BF16)
            o_ref[:, 2 * GROUP_W + h * HEAD_DIM:2 * GROUP_W + (h + 1) * HEAD_DIM] = dv.astype(BF16)

    def spec(col, which):
        if which == "prev":
            return pl.BlockSpec((None, BLK, GROUP_W), lambda r, n: (r, jnp.maximum(n - 1, 0), col))
        if which == "next":
            return pl.BlockSpec((None, BLK, GROUP_W), lambda r, n: (r, jnp.minimum(n + 1, nblk - 1), col))
        return pl.BlockSpec((None, BLK, GROUP_W), lambda r, n: (r, n, col))

    return _run(
        body, [qkv, qkv, qkv, qkv, qkv, qkv, dout, dout, lse, lse, dd, dd], carry=carry, name=name, grid=(dil, nblk),
        in_specs=[spec(0, "cur"), spec(1, "cur"), spec(2, "cur"), spec(1, "prev"), spec(2, "prev"), spec(0, "next"),
                  spec(0, "cur"), spec(0, "next"), spec(0, "cur"), spec(0, "next"), spec(0, "cur"), spec(0, "next")],
        out_specs=pl.BlockSpec((None, BLK, 3 * GROUP_W), lambda r, n: (r, n, 0)),
        out_shape=jax.ShapeDtypeStruct((dil, length, 3 * GROUP_W), BF16),
        compiler_params=_cparams(("parallel", "parallel")),
    )


def _ssm_prep_values(are, aim, logdt):
    dt = jnp.exp(logdt)
    mag = jnp.exp(are * dt)
    lb_re, lb_im = mag * jnp.cos(aim * dt), mag * jnp.sin(aim * dt)
    inv = 1.0 / (are * are + aim * aim)
    n_re, n_im = lb_re - 1.0, lb_im
    f_re = (n_re * are + n_im * aim) * inv
    f_im = (n_im * are - n_re * aim) * inv
    return dt, lb_re, lb_im, f_re, f_im, inv


PREP_G = 8


def _group_specs(are, logdt, bre):
    def spec(a):
        return pl.BlockSpec((PREP_G,) + a.shape[1:], lambda i: (i, 0, 0))
    return spec(are), spec(logdt), spec(bre)


def ssm_prep(are, aim, logdt, bre, bim):
    def body(are_r, aim_r, ldt_r, bre_r, bim_r, lre_o, lim_o, bbre_o, bbim_o):
        _, lb_re, lb_im, f_re, f_im, _ = _ssm_prep_values(are_r[...], aim_r[...], ldt_r[...])
        lre_o[...] = lb_re
        lim_o[...] = lb_im
        bbre_o[...] = f_re * bre_r[...] - f_im * bim_r[...]
        bbim_o[...] = f_re * bim_r[...] + f_im * bre_r[...]

    sh1 = jax.ShapeDtypeStruct(are.shape, F32)
    shb = jax.ShapeDtypeStruct(bre.shape, F32)
    s1, sd, sb = _group_specs(are, logdt, bre)
    return _pcall(body, name="ssm_prep", grid=(SSM_GROUPS // PREP_G,), in_specs=[s1, s1, sd, sb, sb], out_specs=[s1, s1, sb, sb],
                  out_shape=[sh1, sh1, shb, shb], compiler_params=_cparams(("parallel",)))(are, aim, logdt, bre, bim)


def ssm_prep_bwd(are, aim, logdt, bre, bim, dbbre, dbbim, dlre, dlim):
    def body(are_r, aim_r, ldt_r, bre_r, bim_r, dbbre_r, dbbim_r, dlre_r, dlim_r, dare_o, daim_o, dldt_o, dbre_o, dbim_o):
        are_v, aim_v = are_r[...], aim_r[...]
        dt, lb_re, lb_im, f_re, f_im, inv = _ssm_prep_values(are_v, aim_v, ldt_r[...])
        b_re, b_im, g_re, g_im = bre_r[...], bim_r[...], dbbre_r[...], dbbim_r[...]
        dbre_o[...] = f_re * g_re + f_im * g_im
        dbim_o[...] = f_re * g_im - f_im * g_re
        df_re = jnp.sum(b_re * g_re + b_im * g_im, axis=-1, keepdims=True)
        df_im = jnp.sum(b_re * g_im - b_im * g_re, axis=-1, keepdims=True)
        il_re, il_im = are_v * inv, -aim_v * inv
        cl_re = dlre_r[...] + il_re * df_re + il_im * df_im
        cl_im = dlim_r[...] + il_re * df_im - il_im * df_re
        q_re = -(f_re * il_re - f_im * il_im)
        q_im = -(f_re * il_im + f_im * il_re)
        ca_re = q_re * df_re + q_im * df_im
        ca_im = q_re * df_im - q_im * df_re
        cz_re = lb_re * cl_re + lb_im * cl_im
        cz_im = lb_re * cl_im - lb_im * cl_re
        dare_o[...] = ca_re + dt * cz_re
        daim_o[...] = ca_im + dt * cz_im
        dldt_o[...] = dt * jnp.sum(are_v * cz_re + aim_v * cz_im, axis=1, keepdims=True)

    sh1 = jax.ShapeDtypeStruct(are.shape, F32)
    shb = jax.ShapeDtypeStruct(bre.shape, F32)
    s1, sd, sb = _group_specs(are, logdt, bre)
    return _pcall(
        body, name="ssm_prep_bwd", grid=(SSM_GROUPS // PREP_G,), in_specs=[s1, s1, sd, sb, sb, sb, sb, s1, s1],
        out_specs=[s1, s1, sd, sb, sb], out_shape=[sh1, sh1, jax.ShapeDtypeStruct(logdt.shape, F32), shb, shb],
        compiler_params=_cparams(("parallel",)),
    )(are, aim, logdt, bre, bim, dbbre, dbbim, dlre, dlim)


SCAN_WC = 512


def _chain_segments(a_re, a_im, e_re, e_im, nsq, reverse):
    p_re, p_im = a_re, a_im
    for _ in range(nsq):
        p_re, p_im = p_re * p_re - p_im * p_im, 2.0 * p_re * p_im
    row = lax.broadcasted_iota(jnp.int32, e_re.shape, 0)
    edge = (row == SEGS - 1) if reverse else (row == 0)
    shift = SEGS - 1 if reverse else 1
    c_re, c_im = jnp.zeros_like(e_re), jnp.zeros_like(e_im)
    for _ in range(SEGS - 1):
        n_re = p_re * c_re - p_im * c_im + e_re
        n_im = p_re * c_im + p_im * c_re + e_im
        c_re = jnp.where(edge, 0.0, pltpu.roll(n_re, shift, 0))
        c_im = jnp.where(edge, 0.0, pltpu.roll(n_im, shift, 0))
    return c_re, c_im


def _scan_dims(s):
    steps = s // SEGS
    assert steps & (steps - 1) == 0
    tt = min(128, steps)
    return steps, tt, steps // tt, tt * SEGS, int(math.log2(steps))


U_BLK = SSM_W // BD


def ssm_fwd(u_s, dvec, w_bre, w_bim, w_cre, w_cim_neg, lre, lim, name, carry=None):
    s = u_s.shape[0]
    steps, tt, nch, rows, nsq = _scan_dims(s)
    nb, ub_w, wc = w_bre.shape

    def body(u_r, d_r, bre_r, bim_r, cre_r, cim_r, lre_r, lim_r, yg_o, ys_o, hre_o, him_o, hin_re_o, hin_im_o,
             st_re, st_im, x_re, x_im, h_re, h_im):
        ps, ch = pl.program_id(1), pl.program_id(2)
        a_re = jnp.broadcast_to(lre_r[...], (SEGS, wc))
        a_im = jnp.broadcast_to(lim_r[...], (SEGS, wc))
        ub = u_r[...]
        ub16 = ub.astype(BF16)
        x_re[...] = jnp.dot(ub16, bre_r[...], preferred_element_type=F32)
        x_im[...] = jnp.dot(ub16, bim_r[...], preferred_element_type=F32)

        @pl.when(jnp.logical_and(ps == 0, ch == 0))
        def _():
            st_re[...] = jnp.zeros_like(st_re)
            st_im[...] = jnp.zeros_like(st_im)

        @pl.when(jnp.logical_and(ps == 1, ch == 0))
        def _():
            c_re, c_im = _chain_segments(a_re, a_im, st_re[...], st_im[...], nsq, False)
            st_re[...] = c_re
            st_im[...] = c_im
            hin_re_o[...] = c_re
            hin_im_o[...] = c_im

        def run(store):
            def step(t, hc):
                off = pl.multiple_of(t * SEGS, SEGS)
                n_re = a_re * hc[0] - a_im * hc[1] + x_re[pl.ds(off, SEGS), :]
                n_im = a_re * hc[1] + a_im * hc[0] + x_im[pl.ds(off, SEGS), :]
                if store:
                    h_re[pl.ds(off, SEGS), :] = n_re
                    h_im[pl.ds(off, SEGS), :] = n_im
                return n_re, n_im

            fin = lax.fori_loop(0, tt, step, (st_re[...], st_im[...]))
            st_re[...] = fin[0]
            st_im[...] = fin[1]

        @pl.when(ps == 0)
        def _():
            run(False)

        @pl.when(ps == 1)
        def _():
            run(True)
            hr16, hi16 = h_re[...].astype(BF16), h_im[...].astype(BF16)
            hre_o[...] = hr16
            him_o[...] = hi16
            y = jnp.dot(hr16, cre_r[...], preferred_element_type=F32) + jnp.dot(hi16, cim_r[...], preferred_element_type=F32)
            y = y + d_r[...] * ub
            ys_o[...] = y
            yg_o[...] = _gelu(y)[0].astype(BF16)

    def pass1(ps, c):
        return jnp.where(ps == 1, c, 0)

    u_spec = pl.BlockSpec((rows, ub_w), lambda j, ps, c: (c, j))
    d_spec = pl.BlockSpec((1, ub_w), lambda j, ps, c: (0, j))
    b_spec = pl.BlockSpec((None, ub_w, wc), lambda j, ps, c: (j, 0, 0))
    c_spec = pl.BlockSpec((None, wc, ub_w), lambda j, ps, c: (j, 0, 0))
    l_spec = pl.BlockSpec((1, wc), lambda j, ps, c: (0, j))
    y_spec = pl.BlockSpec((rows, ub_w), lambda j, ps, c: (pass1(ps, c), j))
    h_spec = pl.BlockSpec((rows, wc), lambda j, ps, c: (pass1(ps, c), j))
    e_spec = pl.BlockSpec((SEGS, wc), lambda j, ps, c: (0, j))
    return _run(
        body, [u_s, dvec, w_bre, w_bim, w_cre, w_cim_neg, lre, lim], carry=carry, name=name, grid=(nb, 2, nch),
        in_specs=[u_spec, d_spec, b_spec, b_spec, c_spec, c_spec, l_spec, l_spec],
        out_specs=[y_spec, y_spec, h_spec, h_spec, e_spec, e_spec],
        out_shape=[jax.ShapeDtypeStruct((s, SSM_W), BF16), jax.ShapeDtypeStruct((s, SSM_W), F32),
                   jax.ShapeDtypeStruct((s, STATE_W), BF16), jax.ShapeDtypeStruct((s, STATE_W), BF16),
                   jax.ShapeDtypeStruct((SEGS, STATE_W), F32), jax.ShapeDtypeStruct((SEGS, STATE_W), F32)],
        scratch_shapes=[pltpu.VMEM((SEGS, wc), F32)] * 2 + [pltpu.VMEM((rows, wc), F32)] * 4,
        compiler_params=_cparams(("parallel", "arbitrary", "arbitrary")),
    )


def ssm_bwd(dyg_s, ys, u_s, h_re, h_im, hin_re, hin_im, dvec, w_bre_t, w_bim_t, w_cre_t, w_cim_neg_t, lre, lim, name, carry=None):
    s = u_s.shape[0]
    steps, tt, nch, rows, nsq = _scan_dims(s)
    half = 2 * SEGS

    def body(dyg_r, ys_r, u_r, hre_r, him_r, pre_r, pim_r, cin_re_r, cin_im_r, d_r, bre_r, bim_r, cre_r, cim_r, lre_r, lim_r,
             du_o, dbre_o, dbim_o, dcre_o, dcim_o, dlre_o, dlim_o, dd_o, st_re, st_im, x_re, x_im, g_re, g_im, hf_re, hf_im):
        ps, ch = pl.program_id(1), pl.program_id(2)
        a_re = jnp.broadcast_to(lre_r[...], (SEGS, SCAN_WC))
        a_im = -jnp.broadcast_to(lim_r[...], (SEGS, SCAN_WC))
        ub, y = u_r[...], ys_r[...]
        dy = dyg_r[...] * _gelu_grad(y, _gelu(y)[1])
        dy16 = dy.astype(BF16)
        x_re[...] = jnp.dot(dy16, cre_r[...], preferred_element_type=F32)
        x_im[...] = jnp.dot(dy16, cim_r[...], preferred_element_type=F32)

        @pl.when(jnp.logical_and(ps == 0, ch == 0))
        def _():
            st_re[...] = jnp.zeros_like(st_re)
            st_im[...] = jnp.zeros_like(st_im)

        @pl.when(jnp.logical_and(ps == 1, ch == 0))
        def _():
            c_re, c_im = _chain_segments(a_re, a_im, st_re[...], st_im[...], nsq, True)
            st_re[...] = c_re
            st_im[...] = c_im
            dlre_o[...] = jnp.zeros_like(dlre_o)
            dlim_o[...] = jnp.zeros_like(dlim_o)

        @pl.when(ps == 0)
        def _():
            def step(i, hc):
                off = pl.multiple_of((tt - 1 - i) * SEGS, SEGS)
                return (a_re * hc[0] - a_im * hc[1] + x_re[pl.ds(off, SEGS), :],
                        a_re * hc[1] + a_im * hc[0] + x_im[pl.ds(off, SEGS), :])

            fin = lax.fori_loop(0, tt, step, (st_re[...], st_im[...]))
            st_re[...] = fin[0]
            st_im[...] = fin[1]

        @pl.when(ps == 1)
        def _():
            hf_re[...] = hre_r[...].astype(F32)
            hf_im[...] = him_r[...].astype(F32)
            first_chunk = ch == nch - 1
            edge_re = jnp.where(first_chunk, cin_re_r[...], pre_r[...].astype(F32)[SEGS:, :])
            edge_im = jnp.where(first_chunk, cin_im_r[...], pim_r[...].astype(F32)[SEGS:, :])

            def step(i, hc):
                t = tt - 1 - i
                off = pl.multiple_of(t * SEGS, SEGS)
                n_re = a_re * hc[0] - a_im * hc[1] + x_re[pl.ds(off, SEGS), :]
                n_im = a_re * hc[1] + a_im * hc[0] + x_im[pl.ds(off, SEGS), :]
                g_re[pl.ds(off, SEGS), :] = n_re
                g_im[pl.ds(off, SEGS), :] = n_im
                offp = pl.multiple_of(jnp.maximum(t - 1, 0) * SEGS, SEGS)
                hp_re = jnp.where(t == 0, edge_re, hf_re[pl.ds(offp, SEGS), :])
                hp_im = jnp.where(t == 0, edge_im, hf_im[pl.ds(offp, SEGS), :])
                return n_re, n_im, hc[2] + hp_re * n_re + hp_im * n_im, hc[3] + hp_re * n_im - hp_im * n_re

            fin = lax.fori_loop(0, tt, step, (st_re[...], st_im[...], dlre_o[...], dlim_o[...]))
            st_re[...] = fin[0]
            st_im[...] = fin[1]
            dlre_o[...] = fin[2]
            dlim_o[...] = fin[3]

            gr16, gi16 = g_re[...].astype(BF16), g_im[...].astype(BF16)
            du = jnp.dot(gr16, bre_r[...], preferred_element_type=F32) + jnp.dot(gi16, bim_r[...], preferred_element_type=F32)
            du_o[...] = du + d_r[...] * dy
            ub16 = ub.astype(BF16)
            parts = [
                (dbre_o, lax.dot_general(ub16, gr16, _DN["tn"], preferred_element_type=F32)),
                (dbim_o, lax.dot_general(ub16, gi16, _DN["tn"], preferred_element_type=F32)),
                (dcre_o, lax.dot_general(hre_r[...], dy16, _DN["tn"], preferred_element_type=F32)),
                (dcim_o, lax.dot_general(him_r[...], dy16, _DN["tn"], preferred_element_type=F32)),
                (dd_o, jnp.sum(dy * ub, axis=0, keepdims=True)),
            ]
            for ref, val in parts:
                @pl.when(ch == 0)
                def _(ref=ref, val=val):
                    ref[...] = val

                @pl.when(ch > 0)
                def _(ref=ref, val=val):
                    ref[...] += val

    def chunk(c):
        return nch - 1 - c

    def pass1(ps, c):
        return jnp.where(ps == 1, chunk(c), chunk(0))

    u_spec = pl.BlockSpec((rows, U_BLK), lambda j, ps, c: (chunk(c), j))
    h_spec = pl.BlockSpec((rows, SCAN_WC), lambda j, ps, c: (pass1(ps, c), j))
    prev_spec = pl.BlockSpec((half, SCAN_WC), lambda j, ps, c: (jnp.maximum(pass1(ps, c) * (rows // half) - 1, 0), j))
    e_spec = pl.BlockSpec((SEGS, SCAN_WC), lambda j, ps, c: (0, j))
    d_spec = pl.BlockSpec((1, U_BLK), lambda j, ps, c: (0, j))
    bt_spec = pl.BlockSpec((None, SCAN_WC, U_BLK), lambda j, ps, c: (j, 0, 0))
    ct_spec = pl.BlockSpec((None, U_BLK, SCAN_WC), lambda j, ps, c: (j, 0, 0))
    l_spec = pl.BlockSpec((1, SCAN_WC), lambda j, ps, c: (0, j))
    du_spec = pl.BlockSpec((rows, U_BLK), lambda j, ps, c: (pass1(ps, c), j))
    return _run(
        body, [dyg_s, ys, u_s, h_re, h_im, h_re, h_im, hin_re, hin_im, dvec, w_bre_t, w_bim_t, w_cre_t, w_cim_neg_t, lre, lim],
        carry=carry, name=name, grid=(BD, 2, nch),
        in_specs=[u_spec, u_spec, u_spec, h_spec, h_spec, prev_spec, prev_spec, e_spec, e_spec, d_spec, bt_spec, bt_spec,
                  ct_spec, ct_spec, l_spec, l_spec],
        out_specs=[du_spec, ct_spec, ct_spec, bt_spec, bt_spec, e_spec, e_spec, d_spec],
        out_shape=[jax.ShapeDtypeStruct((s, SSM_W), F32)] + [jax.ShapeDtypeStruct((BD, U_BLK, SCAN_WC), F32)] * 2
        + [jax.ShapeDtypeStruct((BD, SCAN_WC, U_BLK), F32)] * 2 + [jax.ShapeDtypeStruct((SEGS, STATE_W), F32)] * 2
        + [jax.ShapeDtypeStruct((1, SSM_W), F32)],
        scratch_shapes=[pltpu.VMEM((SEGS, SCAN_WC), F32)] * 2 + [pltpu.VMEM((rows, SCAN_WC), F32)] * 6,
        compiler_params=_cparams(("parallel", "arbitrary", "arbitrary")),
    )


FWD_BD = 4


def _block_diag(m, nb=BD):
    g, r, c = m.shape
    m = m.reshape(nb, g // nb, r, c)
    eye = jnp.eye(g // nb, dtype=m.dtype)
    return jnp.einsum("jarc,ab->jarbc", m, eye).reshape(nb, (g // nb) * r, (g // nb) * c)


def _block_diag_extract(m, r, c):
    per = m.shape[1] // r
    m = m.reshape(BD, per, r, per, c)
    return jnp.einsum("jarac->jarc", m).reshape(BD * per, r, c)


def to_segments(a):
    s, w = a.shape
    return a.reshape(SEGS, s // SEGS, w).transpose(1, 0, 2).reshape(s, w)


def from_segments(a):
    s, w = a.shape
    return a.reshape(s // SEGS, SEGS, w).transpose(1, 0, 2).reshape(s, w)


W_IN_CHUNK_ROWS = (512, 512, 512, 512)


def _row_chunks(blocks, sizes):
    assert sum(sizes) == blocks.shape[1]
    out, at = [], 0
    for n in sizes:
        out.append(AllToAll([blocks[:, at:at + n]]))
        at += n
    return out
TALL_TM = 2048
FFN_TN = 512


def local_step(x, target, shards, small):
    s = x.shape[0]
    g1, g2, g3, g4 = (small[k].reshape(1, D_MODEL) for k in ("norm_mix_pre", "norm_mix_post", "norm_ffn_pre", "norm_ffn_post"))
    dvec = small["ssm_d"].reshape(1, SSM_W)
    wts, recv = {}, {}

    def gathered(names, blocks):
        for n, b in zip(names, blocks):
            wts[n] = _full_from_gathered(b, n)

    def rms_in_fn(r, c):
        hh = _rms(r[0], c[0])[0].astype(BF16)
        return [hh, _permute(_perm_matrix(PERM_TS, 4, False), hh), _permute(_perm_matrix(PERM_TS, 16, False), hh)], []

    (h, h4, h16), got = rowwise("rms_in", rms_in_fn, [x], [g1], [(D_MODEL, BF16), (D_MODEL, BF16, 4), (D_MODEL, BF16, 16)],
                                ts=PERM_TS, carry=Gather([shards["w_in"]]))
    w_in_t = _full_from_gathered(got[0], "w_in")
    w_u_t, w_gates_t = w_in_t[3 * HQ:3 * HQ + SSM_W], w_in_t[3 * HQ + SSM_W:]

    def qkv_rows(g):
        return 3 * GROUP_W, lambda t: 3 * t + g

    hd = [h.reshape(1, s, D_MODEL), h4, h16]
    qkv = [None] * 3
    names = ("w_attn_up", "w_glu_v", "w_glu_g")
    qkv[0], got = mm([(hd[0].reshape(s, D_MODEL), w_in_t)], "nt", BF16, "mm_qkv0", tm=TALL_TM, tn=GROUP_W, b_window=qkv_rows(0),
                     carry=Gather([shards[n] for n in names]))
    gathered(names, got)
    qkv[1], got = mm([(hd[1].reshape(s, D_MODEL), w_in_t)], "nt", BF16, "mm_qkv1", tm=TALL_TM, tn=GROUP_W, b_window=qkv_rows(1),
                     carry=Gather([shards["w_out"]]))
    gathered(("w_out",), got)
    qkv[2] = mm([(hd[2].reshape(s, D_MODEL), w_in_t)], "nt", BF16, "mm_qkv2", tm=TALL_TM, tn=GROUP_W, b_window=qkv_rows(2))
    u = mm([(h, w_u_t)], "nt", F32, "mm_u")
    gates, got = mm([(h, w_gates_t)], "nt", BF16, "mm_gates", carry=Gather([shards["w_ffn_gate"]]))
    gathered(("w_ffn_gate",), got)

    outs, lses = [], []
    for g, (_, dil) in enumerate(ATTN_GROUPS):
        o, l = attn_fwd(qkv[g].reshape(dil, s // dil, 3 * GROUP_W), g, f"attn_fwd{g}")
        outs.append(o.reshape(s, GROUP_W) if dil == 1 else o)
        lses.append(l.reshape(s, GROUP_W) if dil == 1 else l)

    def natural(r):
        back4, back16 = _perm_matrix(PERM_TS, 4, True), _perm_matrix(PERM_TS, 16, True)
        return (r[0], _permute(back4, r[1].astype(BF16)), _permute(back16, r[2].astype(BF16)),
                r[3], _permute(back4, r[4]), _permute(back16, r[5]))

    def merge_fn(r, c):
        o0, o1, o2, l0, l1, l2 = natural(r)
        w0, w1, w2 = _mix_weights(l0, l1, l2)
        return [w0 * o0 + w1 * o1 + w2 * o2], []

    (attn,) = rowwise("attn_merge", merge_fn, outs + lses, [], [(GROUP_W, BF16)], ts=PERM_TS)
    attn_branch = mm([(attn, wts["w_attn_up"])], "nn", BF16, "mm_up", tm=TALL_TM)

    are3 = small["ssm_a_re"].reshape(SSM_GROUPS, SSM_STATE, 1)
    aim3 = small["ssm_a_im"].reshape(SSM_GROUPS, SSM_STATE, 1)
    ldt3 = small["ssm_log_dt"].reshape(SSM_GROUPS, 1, 1)
    bre3 = small["ssm_b_re"].reshape(SSM_GROUPS, SSM_STATE, SSM_GROUP)
    bim3 = small["ssm_b_im"].reshape(SSM_GROUPS, SSM_STATE, SSM_GROUP)
    cre3 = small["ssm_c_re"].reshape(SSM_GROUPS, SSM_GROUP, SSM_STATE)
    cim3 = small["ssm_c_im"].reshape(SSM_GROUPS, SSM_GROUP, SSM_STATE)
    lre3, lim3, bbre, bbim = ssm_prep(are3, aim3, ldt3, bre3, bim3)
    lre, lim = lre3.reshape(1, STATE_W), lim3.reshape(1, STATE_W)
    w_bre = _block_diag(bbre.transpose(0, 2, 1)).astype(BF16)
    w_bim = _block_diag(bbim.transpose(0, 2, 1)).astype(BF16)
    w_cre = _block_diag(cre3.transpose(0, 2, 1)).astype(BF16)
    w_cim = _block_diag(cim3.transpose(0, 2, 1)).astype(BF16)
    u_s = to_segments(u)
    fwd_w = [_block_diag(t.transpose(0, 2, 1), FWD_BD).astype(BF16) for t in (bbre, bbim, cre3, -cim3)]
    (yg_s, y_ssm, h_re, h_im, hin_re, hin_im), got = ssm_fwd(
        u_s, dvec, *fwd_w, lre, lim, "ssm_fwd", carry=Gather([shards["w_ffn_up"]]))
    gathered(("w_ffn_up",), got)
    yg = from_segments(yg_s)
    gv = mm([(yg, wts["w_glu_v"])], "nn", BF16, "mm_glu_v", tm=TALL_TM)
    gg = mm([(yg, wts["w_glu_g"])], "nn", BF16, "mm_glu_g", tm=TALL_TM)

    def gate_fn(r, c):
        gts, ab, gv_, gg_ = r
        sa, ss = _sigmoid(gts[:, :D_MODEL]), _sigmoid(gts[:, D_MODEL:])
        return [sa * ab + ss * (gv_ * _sigmoid(gg_))], []

    (merged,) = rowwise("gate_merge", gate_fn, [gates, attn_branch, gv, gg], [], [(D_MODEL, BF16)])
    o_mix = mm([(merged, wts["w_out"])], "nn", F32, "mm_out")

    def mid_fn(r, c):
        x1 = r[0] + _rms(r[1], c[0])[0]
        return [x1, _rms(x1, c[1])[0]], []

    x1, h2 = rowwise("rms_mid", mid_fn, [x, o_mix], [g2, g3], [(D_MODEL, F32), (D_MODEL, BF16)])
    (fa, fb, fin), got = mm([(h2, wts["w_ffn_gate"]), (h2, wts["w_ffn_up"])], "nt", [BF16, BF16, BF16], "mm_ffn_in", tn=FFN_TN,
                            epilogue=lambda p, e: [p[0], p[1], p[0] * _sigmoid_to_bf16(p[0]) * p[1]],
                            carry=Gather([shards["w_ffn_down"]]))
    gathered(("w_ffn_down",), got)
    f = mm([(fin, wts["w_ffn_down"])], "nn", F32, "mm_ffn_down", tn=512, tk=D_FF)

    def loss_fn(r, c):
        x1_, f_, tgt = r
        y, n, rr = _rms(f_, c[0])
        err = x1_ + y - tgt
        dout = err * (1.0 / D_MODEL)
        df, dg = _rms_bwd(dout, n, rr, c[0])
        lp = 0.5 * jnp.sum(jnp.sum(err * err, axis=-1, keepdims=True) * (1.0 / D_MODEL), axis=0, keepdims=True)
        return [df, dout], [dg, lp]

    df, dout, dg4, loss_part = rowwise("loss_bwd", loss_fn, [x1, f, target], [g4], [(D_MODEL, BF16), (D_MODEL, BF16)],
                                       acc_outs=[(1, D_MODEL), (1, 1)])
    def sent(names, blocks):
        for n, b in zip(names, blocks):
            recv[n] = b

    def to_owners(names, dws):
        return AllToAll([_split_for_devices(d, n) for n, d in zip(names, dws)])

    def swiglu_bwd(p, e):
        dfin_, (a, b) = p[0], e
        sg = _sigmoid_to_bf16(a)
        return [dfin_ * b * (sg * (1.0 + a * (1.0 - sg))), dfin_ * a * sg]

    da, db = mm([(df, wts["w_ffn_down"])], "nt", [BF16, BF16], "mm_d_fin", tn=FFN_TN, epilogue=swiglu_bwd, extras=[fa, fb])
    dw_ffn_down = mm([(fin, df)], "tn", BF16, "mm_dw_ffn_down")
    dh2, got = mm([(da, wts["w_ffn_gate"]), (db, wts["w_ffn_up"])], "nn", F32, "mm_d_h2", tm=512, tn=1024, tk=D_FF // 2,
                  carry=to_owners(["w_ffn_down"], [dw_ffn_down]))
    sent(["w_ffn_down"], got)
    dw_ffn_gate = mm([(da, h2)], "tn", BF16, "mm_dw_ffn_gate")
    dw_ffn_up, got = mm([(db, h2)], "tn", BF16, "mm_dw_ffn_up", carry=to_owners(["w_ffn_gate"], [dw_ffn_gate]))
    sent(["w_ffn_gate"], got)

    def mid_bwd(r, c):
        dh2_, dout_, x1_, o_ = r
        _, n3, r3 = _rms(x1_, c[1])
        dx1, dg3_ = _rms_bwd(dh2_, n3, r3, c[1])
        dx1 = dx1 + dout_
        _, n2, r2 = _rms(o_, c[0])
        do_, dg2_ = _rms_bwd(dx1, n2, r2, c[0])
        return [dx1, do_], [dg2_, dg3_]

    dx1, do_mix, dg2, dg3 = rowwise("rms_mid_bwd", mid_bwd, [dh2, dout, x1, o_mix], [g2, g3], [(D_MODEL, F32), (D_MODEL, BF16)],
                                    acc_outs=[(1, D_MODEL), (1, D_MODEL)])
    dmerged = mm([(do_mix, wts["w_out"])], "nt", BF16, "mm_d_merged")
    dw_out = mm([(merged, do_mix)], "tn", BF16, "mm_dw_out")

    def gate_bwd(r, c):
        dm, gts, ab, gv_, gg_ = r
        sa, ss, sg = _sigmoid(gts[:, :D_MODEL]), _sigmoid(gts[:, D_MODEL:]), _sigmoid(gg_)
        branch = gv_ * sg
        dbranch = dm * ss
        dgates = jnp.concatenate([dm * ab * sa * (1.0 - sa), dm * branch * ss * (1.0 - ss)], axis=-1)
        return [dgates, dm * sa, dbranch * sg, dbranch * gv_ * sg * (1.0 - sg)], []

    dgates, dab, dgv, dgg = rowwise("gate_bwd", gate_bwd, [dmerged, gates, attn_branch, gv, gg], [],
                                    [(2 * D_MODEL, BF16), (D_MODEL, BF16), (D_MODEL, BF16), (D_MODEL, BF16)])
    dattn = mm([(dab, wts["w_attn_up"])], "nt", F32, "mm_d_attn")
    dw_up = mm([(attn, dab)], "tn", BF16, "mm_dw_up")
    dyg = mm([(dgv, wts["w_glu_v"]), (dgg, wts["w_glu_g"])], "nt", F32, "mm_d_yg")
    dw_glu_v = mm([(yg, dgv)], "tn", BF16, "mm_dw_glu_v")
    dw_glu_g = mm([(yg, dgg)], "tn", BF16, "mm_dw_glu_g")

    (du_s, dbre_d, dbim_d, dcre_d, dcim_d, dl_re8, dl_im8, dd_ssm), got = ssm_bwd(
        to_segments(dyg), y_ssm, u_s, h_re, h_im, hin_re, hin_im, dvec, w_bre.transpose(0, 2, 1), w_bim.transpose(0, 2, 1),
        w_cre.transpose(0, 2, 1), -w_cim.transpose(0, 2, 1), lre, lim, "ssm_bwd", carry=to_owners(["w_ffn_up"], [dw_ffn_up]))
    sent(["w_ffn_up"], got)
    dbb_re = _block_diag_extract(dbre_d, SSM_GROUP, SSM_STATE).transpose(0, 2, 1)
    dbb_im = _block_diag_extract(dbim_d, SSM_GROUP, SSM_STATE).transpose(0, 2, 1)
    dc_re = _block_diag_extract(dcre_d, SSM_STATE, SSM_GROUP).transpose(0, 2, 1)
    dc_im = -_block_diag_extract(dcim_d, SSM_STATE, SSM_GROUP).transpose(0, 2, 1)

    def fold8(r, c):
        return [], [jnp.sum(r[0], axis=0, keepdims=True), jnp.sum(r[1], axis=0, keepdims=True)]

    dl_re, dl_im = rowwise("ssm_dl_fold", fold8, [dl_re8, dl_im8], [], [], acc_outs=[(1, STATE_W), (1, STATE_W)], ts=SEGS)
    da_re, da_im, dldt, db_re, db_im = ssm_prep_bwd(
        are3, aim3, ldt3, bre3, bim3, dbb_re, dbb_im,
        dl_re.reshape(SSM_GROUPS, SSM_STATE, 1), dl_im.reshape(SSM_GROUPS, SSM_STATE, 1))
    du = from_segments(du_s)

    def merge_bwd(r, c):
        dat = r[0]
        o0, o1, o2, l0, l1, l2 = natural(r[1:])
        w0, w1, w2 = _mix_weights(l0, l1, l2)
        tot = _head_sum(dat * (w0 * o0 + w1 * o1 + w2 * o2))
        to4, to16 = _perm_matrix(PERM_TS, 4, False), _perm_matrix(PERM_TS, 16, False)
        return [w0 * dat, _permute(to4, (w1 * dat).astype(BF16)), _permute(to16, (w2 * dat).astype(BF16)),
                w0 * tot, _permute(to4, (w1 * tot).astype(BF16)), _permute(to16, (w2 * tot).astype(BF16))], []

    mb = rowwise("attn_merge_bwd", merge_bwd, [dattn] + outs + lses, [],
                 [(GROUP_W, BF16), (GROUP_W, BF16, 4), (GROUP_W, BF16, 16), (GROUP_W, BF16), (GROUP_W, BF16, 4), (GROUP_W, BF16, 16)],
                 ts=PERM_TS)
    dqs, dw_qkv = [], []
    behind_attn = [(["w_out"], [dw_out]), (["w_glu_v", "w_glu_g"], [dw_glu_v, dw_glu_g]), (["w_attn_up"], [dw_up])]
    for g, (_, dil) in enumerate(ATTN_GROUPS):
        dq, got = attn_bwd(qkv[g].reshape(dil, s // dil, 3 * GROUP_W), mb[g].reshape(dil, s // dil, GROUP_W),
                           lses[g].reshape(dil, s // dil, GROUP_W), mb[3 + g].reshape(dil, s // dil, GROUP_W),
                           g, f"attn_bwd{g}", carry=to_owners(*behind_attn[g]))
        sent(behind_attn[g][0], got)
        dq = dq.reshape(s, 3 * GROUP_W)
        dqs.append(dq)
        dw_qkv.append(mm([(hd[g].reshape(s, D_MODEL), dq)], "tn", BF16, f"mm_dw_qkv{g}"))
    dw_u = mm([(h, du)], "tn", BF16, "mm_dw_u")
    dw_gates = mm([(h, dgates)], "tn", BF16, "mm_dw_gates")
    dw_in = jnp.concatenate(
        [dw_qkv[g][:, o * GROUP_W:(o + 1) * GROUP_W] for o in range(3) for g in range(3)] + [dw_u, dw_gates], axis=1)
    chunks = _row_chunks(_split_for_devices(dw_in, "w_in"), W_IN_CHUNK_ROWS)
    dh_parts, got_chunks = [], []
    for g, (_, dil) in enumerate(ATTN_GROUPS):
        dh_g, got = mm([(dqs[g], w_in_t)], "nn", BF16, f"mm_d_h_qkv{g}", tk=GROUP_W, b_window=qkv_rows(g), carry=chunks[g])
        got_chunks.append(got[0])
        dh_parts.append(dh_g if dil == 1 else dh_g.reshape(dil, s // dil, D_MODEL))
    dh_parts.append(mm([(du, w_u_t)], "nn", BF16, "mm_d_h_u"))
    dh_gates, got = mm([(dgates, w_gates_t)], "nn", BF16, "mm_d_h_gates", carry=chunks[3])
    got_chunks.append(got[0])
    dh_parts.append(dh_gates)
    recv["w_in"] = jnp.concatenate(got_chunks, axis=1)

    def in_bwd(r, c):
        dh1 = _permute(_perm_matrix(PERM_TS, 4, True), r[1].astype(BF16))
        dh2_ = _permute(_perm_matrix(PERM_TS, 16, True), r[2].astype(BF16))
        dh = r[0] + dh1 + dh2_ + r[3] + r[4]
        _, n1, r1 = _rms(r[6], c[0])
        dx, dg1_ = _rms_bwd(dh, n1, r1, c[0])
        return [dx + r[5]], [dg1_]

    grad_x, dg1 = rowwise("rms_in_bwd", in_bwd, dh_parts + [dx1, x], [g1], [(D_MODEL, F32)], acc_outs=[(1, D_MODEL)], ts=PERM_TS)

    dsmall = dict(norm_mix_pre=dg1, ssm_a_re=da_re, ssm_a_im=da_im, ssm_log_dt=dldt, ssm_b_re=db_re, ssm_b_im=db_im,
                  ssm_c_re=dc_re, ssm_c_im=dc_im, ssm_d=dd_ssm, norm_mix_post=dg2, norm_ffn_pre=dg3, norm_ffn_post=dg4)
    return loss_part, grad_x, recv, dsmall


def adamw(parts, w, m, v, name, carry=None):
    r, c = w.shape
    tr = r
    while tr > 8 and tr % 2 == 0 and tr * c * (8 * parts.dtype.itemsize + 28) * 2 > 24 * 1024 * 1024:
        tr //= 2
    assert r % tr == 0 and (tr % 8 == 0 or tr == r)
    c1, c2 = 1.0 / (1.0 - ADAM_B1 ** ADAM_STEP), 1.0 / (1.0 - ADAM_B2 ** ADAM_STEP)

    def body(p_ref, w_ref, m_ref, v_ref, g_o, d_o, m_o, v_o):
        g = p_ref[0].astype(F32)
        for i in range(1, N_DEV):
            g = g + p_ref[i].astype(F32)
        mn = ADAM_B1 * m_ref[...] + (1.0 - ADAM_B1) * g
        vn = ADAM_B2 * v_ref[...] + (1.0 - ADAM_B2) * (g * g)
        g_o[...] = g
        m_o[...] = mn
        v_o[...] = vn
        d_o[...] = -ADAM_LR * ((mn * c1) / (jnp.sqrt(vn * c2) + ADAM_EPS) + ADAM_WD * w_ref[...])

    blk = pl.BlockSpec((tr, c), lambda i: (i, 0))
    return _run(
        body, [parts, w, m, v], carry=carry, name=name, grid=(r // tr,),
        in_specs=[pl.BlockSpec((N_DEV, tr, c), lambda i: (0, i, 0)), blk, blk, blk],
        out_specs=[blk] * 4, out_shape=[jax.ShapeDtypeStruct((r, c), F32)] * 4, compiler_params=_cparams(("parallel",)),
    )


PACK_C = 1024
SHARDED = ("w_in", "w_attn_up", "w_glu_v", "w_glu_g", "w_out", "w_ffn_gate", "w_ffn_up", "w_ffn_down")
ROW_SHARDED = ("w_out", "w_ffn_down")
SENT_TRANSPOSED = ("w_in", "w_ffn_gate", "w_ffn_up")
GRAD_TRANSPOSED = ("w_ffn_gate", "w_ffn_up")
SMALL = ("norm_mix_pre", "ssm_a_re", "ssm_a_im", "ssm_log_dt", "ssm_b_re", "ssm_b_im", "ssm_c_re", "ssm_c_im", "ssm_d",
         "norm_mix_post", "norm_ffn_pre", "norm_ffn_post")
WEIGHTS = ("norm_mix_pre", "w_in", "w_attn_up", "ssm_a_re", "ssm_a_im", "ssm_log_dt", "ssm_b_re", "ssm_b_im", "ssm_c_re",
           "ssm_c_im", "ssm_d", "w_glu_v", "w_glu_g", "w_out", "norm_mix_post", "norm_ffn_pre", "w_ffn_gate", "w_ffn_up",
           "w_ffn_down", "norm_ffn_post")


def _pack(arrs, dtype, pad_rows_to=64):
    flat = jnp.concatenate([a.reshape(-1).astype(dtype) for a in arrs])
    n = flat.shape[0]
    rows = -(-n // PACK_C)
    rows = -(-rows // pad_rows_to) * pad_rows_to
    return jnp.pad(flat, (0, rows * PACK_C - n)).reshape(rows, PACK_C)


def _unpack(flat2d, shapes):
    flat = flat2d.reshape(-1)
    out, off = [], 0
    for shp in shapes:
        n = int(np.prod(shp))
        out.append(flat[off:off + n].reshape(shp))
        off += n
    return out


def _full_from_gathered(gathered, name):
    if name in ROW_SHARDED or name in SENT_TRANSPOSED:
        return gathered.reshape(-1, gathered.shape[2])
    return gathered.transpose(1, 0, 2).reshape(gathered.shape[1], -1)


def _split_for_devices(full, name):
    if name in ROW_SHARDED or name in GRAD_TRANSPOSED:
        return full.reshape(N_DEV, -1, full.shape[1])
    return full.reshape(full.shape[0], N_DEV, -1).transpose(1, 0, 2)


def kernel(x, norm_mix_pre, w_in, w_attn_up, ssm_a_re, ssm_a_im, ssm_log_dt, ssm_b_re, ssm_b_im, ssm_c_re, ssm_c_im, ssm_d, w_glu_v, w_glu_g, w_out, norm_mix_post, norm_ffn_pre, w_ffn_gate, w_ffn_up, w_ffn_down, norm_ffn_post, loss_target, m_norm_mix_pre, m_w_in, m_w_attn_up, m_ssm_a_re, m_ssm_a_im, m_ssm_log_dt, m_ssm_b_re, m_ssm_b_im, m_ssm_c_re, m_ssm_c_im, m_ssm_d, m_w_glu_v, m_w_glu_g, m_w_out, m_norm_mix_post, m_norm_ffn_pre, m_w_ffn_gate, m_w_ffn_up, m_w_ffn_down, m_norm_ffn_post, v_norm_mix_pre, v_w_in, v_w_attn_up, v_ssm_a_re, v_ssm_a_im, v_ssm_log_dt, v_ssm_b_re, v_ssm_b_im, v_ssm_c_re, v_ssm_c_im, v_ssm_d, v_w_glu_v, v_w_glu_g, v_w_out, v_norm_mix_post, v_norm_ffn_pre, v_w_ffn_gate, v_w_ffn_up, v_w_ffn_down, v_norm_ffn_post):
    args = dict(locals())
    wv = {n: args[n][0] for n in WEIGHTS}
    mv = {n: args["m_" + n][0] for n in WEIGHTS}
    vv = {n: args["v_" + n][0] for n in WEIGHTS}

    shards = {n: (wv[n].T if n in SENT_TRANSPOSED else wv[n]).astype(BF16) for n in SHARDED}
    small = {n: wv[n] for n in SMALL}
    loss_part, grad_x, recv, dsmall = local_step(x[0], loss_target[0], shards, small)
    for n in GRAD_TRANSPOSED:
        recv[n] = recv[n].transpose(0, 2, 1)

    small_shapes = [wv[n].shape for n in SMALL]
    res = {}
    res["w_in"], (sgather,) = adamw(recv["w_in"], wv["w_in"], mv["w_in"], vv["w_in"], "adamw_w_in",
                                    carry=Gather([_pack([dsmall[n] for n in SMALL], F32)]))
    for n in SHARDED[1:]:
        res[n] = adamw(recv[n], wv[n], mv[n], vv[n], "adamw_" + n)
    sres = adamw(sgather, _pack([wv[n] for n in SMALL], F32), _pack([mv[n] for n in SMALL], F32),
                 _pack([vv[n] for n in SMALL], F32), "adamw_small")
    sun = [_unpack(t, small_shapes) for t in sres]
    for k, n in enumerate(SMALL):
        res[n] = tuple(sun[t][k] for t in range(4))

    loss = lax.psum(loss_part[0, 0], ("x", "y", "c"))
    outs = [loss, grad_x[None]]
    for t in range(4):
        outs += [res[n][t][None] for n in WEIGHTS]
    return tuple(outs)
```

```python
import functools
import math

import numpy as np
import jax
import jax.numpy as jnp
from jax import lax
from jax.experimental import pallas as pl
from jax.experimental.pallas import tpu as pltpu

F32 = jnp.float32
BF16 = jnp.bfloat16

D_MODEL = 2048
HEAD_DIM = 128
HEADS_PER_GROUP = 4
ATTN_GROUPS = ((128, 1), (512, 4), (2048, 16))
N_HEADS = HEADS_PER_GROUP * len(ATTN_GROUPS)
GROUP_W = HEADS_PER_GROUP * HEAD_DIM
HQ = N_HEADS * HEAD_DIM
SSM_W = 1024
SSM_GROUP = 16
SSM_GROUPS = 64
SSM_STATE = 64
STATE_W = SSM_GROUPS * SSM_STATE
D_FF = 5632
EPS = 1e-6
N_DEV = 8
SEGS = 8
BD = 8

ADAM_LR, ADAM_B1, ADAM_B2, ADAM_EPS, ADAM_WD, ADAM_STEP = 0.001, 0.9, 0.999, 1e-08, 0.01, 10

VMEM_LIMIT = 56 * 1024 * 1024
HBM_SPEC = pl.BlockSpec(memory_space=pltpu.HBM)
MESH_ID = pl.DeviceIdType.MESH
NEG = -1e30


def _pcall(body, **kw):
    return pl.pallas_call(body, **kw)


def _cparams(sem=None):
    if sem is None:
        return pltpu.CompilerParams(vmem_limit_bytes=VMEM_LIMIT)
    return pltpu.CompilerParams(vmem_limit_bytes=VMEM_LIMIT, dimension_semantics=sem)


def _my_coords():
    return lax.axis_index("x"), lax.axis_index("y"), lax.axis_index("c")


class Gather:
    def __init__(self, xs, pass_early=False):
        self.arrays = list(xs)
        self.out_shapes = [jax.ShapeDtypeStruct((N_DEV,) + x.shape, x.dtype) for x in xs]
        self.pass_early = pass_early

    def _ctx(self, out_refs, send_sems, recv_sems):
        mx, my, mc = _my_coords()
        me, sibling = (mx, my, mc), (mx, my, 1 - mc)
        chips = [(1 - mx, my), (mx, 1 - my), (1 - mx, 1 - my)]

        def slot(a, px, py, pc):
            return out_refs[a].at[4 * px + 2 * py + pc]

        def copy(a, k, block, to, src=None):
            return pltpu.make_async_remote_copy(
                src_ref=slot(a, *block) if src is None else src, dst_ref=slot(a, *block),
                send_sem=send_sems.at[7 * a + k], recv_sem=recv_sems.at[7 * a + k], device_id=to, device_id_type=MESH_ID)

        return me, sibling, chips, mc, slot, copy

    def _first(self, a, x_refs, ctx):
        me, sibling, chips, mc, slot, copy = ctx
        return [copy(a, 0, me, sibling, src=x_refs[a])] + [copy(a, 1 + j, me, (*chip, mc), src=x_refs[a]) for j, chip in enumerate(chips)]

    def start(self, x_refs, out_refs, send_sems, recv_sems, local_sems):
        ctx = self._ctx(out_refs, send_sems, recv_sems)
        me, slot = ctx[0], ctx[4]
        for a in range(len(self.arrays)):
            pltpu.make_async_copy(x_refs[a], slot(a, *me), local_sems.at[a]).start()
            for cp in self._first(a, x_refs, ctx):
                cp.start()

    def _passed(self, ctx):
        me, sibling, chips, mc, slot, copy = ctx
        return [copy(a, 4 + j, (*chip, mc), sibling) for a in range(len(self.arrays)) for j, chip in enumerate(chips)]

    def middle(self, x_refs, out_refs, send_sems, recv_sems, local_sems):
        ctx = self._ctx(out_refs, send_sems, recv_sems)
        me, sibling, chips, mc, slot, copy = ctx
        for a in range(len(self.arrays)):
            for j, chip in enumerate(chips):
                copy(a, 1 + j, (*chip, mc), me).wait_recv()
                copy(a, 4 + j, (*chip, mc), sibling).start()

    def finish(self, x_refs, out_refs, send_sems, recv_sems, local_sems, passed_on=False):
        if not passed_on:
            self.middle(x_refs, out_refs, send_sems, recv_sems, local_sems)
        ctx = self._ctx(out_refs, send_sems, recv_sems)
        me, sibling, chips, mc, slot, copy = ctx
        na = len(self.arrays)
        passed = self._passed(ctx)
        for a in range(na):
            copy(a, 0, sibling, me).wait_recv()
            for j, chip in enumerate(chips):
                copy(a, 4 + j, (*chip, 1 - mc), me).wait_recv()
        for a in range(na):
            for cp in self._first(a, x_refs, ctx):
                cp.wait_send()
        for cp in passed:
            cp.wait_send()
        for a in range(na):
            pltpu.make_async_copy(x_refs[a], slot(a, *me), local_sems.at[a]).wait()


class AllToAll:
    def __init__(self, ps):
        self.arrays = list(ps)
        self.out_shapes = [jax.ShapeDtypeStruct(p.shape, p.dtype) for p in ps]

    def _copies(self, p_refs, out_refs, send_sems, recv_sems, local_sems):
        mx, my, mc = _my_coords()
        me = 4 * mx + 2 * my + mc
        local, remote = [], []
        for a in range(len(self.arrays)):
            local.append(pltpu.make_async_copy(p_refs[a].at[me], out_refs[a].at[me], local_sems.at[a]))
            for k in range(1, N_DEV):
                px, py, pc = mx ^ ((k >> 2) & 1), my ^ ((k >> 1) & 1), mc ^ (k & 1)
                remote.append(pltpu.make_async_remote_copy(
                    src_ref=p_refs[a].at[4 * px + 2 * py + pc], dst_ref=out_refs[a].at[me],
                    send_sem=send_sems.at[7 * a + k - 1], recv_sem=recv_sems.at[7 * a + k - 1],
                    device_id=(px, py, pc), device_id_type=MESH_ID))
        return local, remote

    def start(self, *refs):
        local, remote = self._copies(*refs)
        for cp in local + remote:
            cp.start()

    def finish(self, *refs):
        local, remote = self._copies(*refs)
        for cp in remote:
            cp.wait_recv()
        for cp in remote:
            cp.wait_send()
        for cp in local:
            cp.wait()


def _run(body, args, carry=None, **kw):
    if carry is None:
        return _pcall(body, **kw)(*args)
    grid = kw["grid"]
    single = not isinstance(kw["out_shape"], (list, tuple))
    in_specs = list(kw["in_specs"])
    out_specs = [kw["out_specs"]] if single else list(kw["out_specs"])
    out_shape = [kw["out_shape"]] if single else list(kw["out_shape"])
    scratch = list(kw.get("scratch_shapes", []))
    na, nin, nout, nscr = len(carry.arrays), len(in_specs), len(out_specs), len(scratch)
    steps = int(np.prod(grid))
    mid_step = (steps * 7) // 10 if getattr(carry, "pass_early", False) and steps >= 4 else None

    def carried(*refs):
        ins, cin = refs[:nin], refs[nin:nin + na]
        outs, cout = refs[nin + na:nin + na + nout], refs[nin + na + nout:nin + 2 * na + nout]
        scr = refs[nin + 2 * na + nout:nin + 2 * na + nout + nscr]
        sems = refs[nin + 2 * na + nout + nscr:]
        step = pl.program_id(0)
        for i in range(1, len(grid)):
            step = step * grid[i] + pl.program_id(i)

        @pl.when(step == 0)
        def _():
            carry.start(cin, cout, *sems)

        if mid_step is not None:
            @pl.when(step == mid_step)
            def _():
                carry.middle(cin, cout, *sems)

        body(*ins, *outs, *scr)

        @pl.when(step == steps - 1)
        def _():
            if mid_step is not None:
                carry.finish(cin, cout, *sems, passed_on=True)
            else:
                carry.finish(cin, cout, *sems)

    res = _pcall(
        carried, name=kw["name"], grid=grid, in_specs=in_specs + [HBM_SPEC] * na, out_specs=out_specs + [HBM_SPEC] * na,
        out_shape=out_shape + carry.out_shapes,
        scratch_shapes=scratch + [pltpu.SemaphoreType.DMA((7 * na,)), pltpu.SemaphoreType.DMA((7 * na,)), pltpu.SemaphoreType.DMA((na,))],
        compiler_params=_cparams(("arbitrary",) * len(grid)),
    )(*args, *carry.arrays)
    main = res[:nout]
    return (main[0] if single else main), list(res[nout:])


_DN = {"nn": (((1,), (0,)), ((), ())), "nt": (((1,), (1,)), ((), ())), "tn": (((0,), (0,)), ((), ()))}


LANE = 128
MM_TM, MM_TN, MM_TK = 1024, 1536, 2048


def _tile(n, cap):
    for t in range(min(cap, n) // LANE * LANE, 0, -LANE):
        if n % t == 0:
            return t
    raise ValueError(n)


DW_TM, DW_TN, DW_TK = 512, 512, 8192


def mm(pairs, mode, out_dtype, name, tm=None, tn=None, tk=None, carry=None, epilogue=None, extras=(), b_window=None):
    a0, b0 = pairs[0]
    if mode == "nn":
        (m, k), n = a0.shape, b0.shape[1]
    elif mode == "nt":
        (m, k), n = a0.shape, b0.shape[0]
    else:
        (k, m), n = a0.shape, b0.shape[1]
    if b_window is not None:
        assert mode in ("nn", "nt") and len(pairs) == 1
        if mode == "nt":
            n = b_window[0]
        else:
            assert k == b_window[0]
    caps = (DW_TM, DW_TN, DW_TK) if mode == "tn" else (MM_TM, MM_TN, MM_TK)
    tm, tn, tk = _tile(m, tm or caps[0]), _tile(n, tn or caps[1]), _tile(k, tk or caps[2])
    nk = k // tk
    npairs = len(pairs)
    nex = len(extras)
    fused = epilogue is not None
    assert not fused or nk == 1
    out_dtypes = list(out_dtype) if fused else [out_dtype]

    def body(*refs):
        prods = []
        for p in range(npairs):
            a = refs[2 * p][...].astype(BF16) if (p == 0 or pairs[p][0] is not pairs[p - 1][0]) else a
            b = refs[2 * p + 1][...].astype(BF16)
            prods.append(lax.dot_general(a, b, _DN[mode], preferred_element_type=F32))
        if fused:
            ex = [refs[2 * npairs + e][...].astype(F32) for e in range(nex)]
            for o_ref, val in zip(refs[2 * npairs + nex:], epilogue(prods, ex)):
                o_ref[...] = val.astype(o_ref.dtype)
            return
        o_ref = refs[2 * npairs]
        tot = prods[0]
        for d in prods[1:]:
            tot = tot + d
        if nk == 1:
            o_ref[...] = tot.astype(o_ref.dtype)
            return
        acc = refs[2 * npairs + 1]
        kk = pl.program_id(2)

        @pl.when(kk == 0)
        def _():
            acc[...] = tot

        @pl.when(kk > 0)
        def _():
            acc[...] += tot

        @pl.when(kk == nk - 1)
        def _():
            o_ref[...] = acc[...].astype(o_ref.dtype)

    rows_of = b_window[1] if b_window is not None else (lambda t: t)
    if mode == "nn":
        sp = [pl.BlockSpec((tm, tk), lambda i, j, kk: (i, kk)), pl.BlockSpec((tk, tn), lambda i, j, kk: (rows_of(kk), j))]
    elif mode == "nt":
        sp = [pl.BlockSpec((tm, tk), lambda i, j, kk: (i, kk)), pl.BlockSpec((tn, tk), lambda i, j, kk: (rows_of(j), kk))]
    else:
        sp = [pl.BlockSpec((tk, tm), lambda i, j, kk: (kk, i)), pl.BlockSpec((tk, tn), lambda i, j, kk: (kk, j))]
    o_spec = pl.BlockSpec((tm, tn), lambda i, j, kk: (i, j))
    out_shapes = [jax.ShapeDtypeStruct((m, n), dt) for dt in out_dtypes]
    return _run(
        body, [t for pr in pairs for t in pr] + list(extras), carry=carry, name=name, grid=(m // tm, n // tn, nk),
        in_specs=sp * npairs + [o_spec] * nex,
        out_specs=[o_spec] * len(out_shapes) if fused else o_spec,
        out_shape=out_shapes if fused else out_shapes[0],
        scratch_shapes=[pltpu.VMEM((tm, tn), F32)] if nk > 1 else [],
        compiler_params=_cparams(("parallel", "parallel", "arbitrary")),
    )


def rowwise(name, fn, row_ins, const_ins, row_outs, acc_outs=(), ts=None, carry=None):
    s = row_ins[0].shape[0]
    row_outs = [ro if len(ro) == 3 else (*ro, 1) for ro in row_outs]
    if ts is None:
        per_row = sum(a.shape[-1] * a.dtype.itemsize for a in row_ins) + sum(w * jnp.dtype(dt).itemsize for w, dt, _ in row_outs)
        ts = 512
        while ts > 8 and 2 * ts * per_row > 20 * 1024 * 1024:
            ts //= 2
    ts = min(ts, s)
    assert s % ts == 0
    nr, nc, no, na = len(row_ins), len(const_ins), len(row_outs), len(acc_outs)

    def body(*refs):
        rows = [r[...].reshape(ts, r.shape[-1]).astype(F32) for r in refs[:nr]]
        consts = [r[...] for r in refs[nr:nr + nc]]
        outs, accs = fn(rows, consts)
        for r, v in zip(refs[nr + nc:nr + nc + no], outs):
            r[...] = v.astype(r.dtype).reshape(r.shape)
        if na:
            first = pl.program_id(0) == 0
            for r, v in zip(refs[nr + nc + no:], accs):
                @pl.when(first)
                def _(r=r, v=v):
                    r[...] = v

                @pl.when(jnp.logical_not(first))
                def _(r=r, v=v):
                    r[...] += v

    def tile_spec(w, d):
        if d == 1:
            return pl.BlockSpec((ts, w), lambda i: (i, 0))
        return pl.BlockSpec((d, ts // d, w), lambda i: (0, i, 0))

    in_specs = [tile_spec(a.shape[-1], a.shape[0] if a.ndim == 3 else 1) for a in row_ins]
    in_specs += [pl.BlockSpec(c.shape, lambda i, nd=c.ndim: (0,) * nd) for c in const_ins]
    out_specs = [tile_spec(w, d) for w, _, d in row_outs]
    out_specs += [pl.BlockSpec(shp, lambda i, nd=len(shp): (0,) * nd) for shp in acc_outs]
    out_shape = [jax.ShapeDtypeStruct((s, w) if d == 1 else (d, s // d, w), dt) for w, dt, d in row_outs]
    out_shape += [jax.ShapeDtypeStruct(shp, F32) for shp in acc_outs]
    return _run(
        body, [*row_ins, *const_ins], carry=carry, name=name, grid=(s // ts,), in_specs=in_specs, out_specs=out_specs,
        out_shape=out_shape, compiler_params=_cparams(("arbitrary",)),
    )


PERM_TS = 256


def _perm_matrix(ts, d, inverse):
    i = lax.broadcasted_iota(jnp.int32, (ts, ts), 0)
    k = lax.broadcasted_iota(jnp.int32, (ts, ts), 1)
    per = ts // d
    src = (i % d) * per + i // d if inverse else (i % per) * d + i // per
    return jnp.where(k == src, 1.0, 0.0).astype(BF16)


def _permute(p, x):
    if x.dtype == BF16:
        return jnp.dot(p, x, preferred_element_type=F32)
    hi = x.astype(BF16)
    rest = x - hi.astype(F32)
    mid = rest.astype(BF16)
    lo = (rest - mid.astype(F32)).astype(BF16)
    out = jnp.dot(p, hi, preferred_element_type=F32) + jnp.dot(p, mid, preferred_element_type=F32)
    return out + jnp.dot(p, lo, preferred_element_type=F32)


def _rms(x, gain):
    r = lax.rsqrt(jnp.mean(x * x, axis=-1, keepdims=True) + EPS)
    n = x * r
    return n * gain, n, r


def _rms_bwd(dy, n, r, gain):
    dn = dy * gain
    dx = r * (dn - n * jnp.mean(dn * n, axis=-1, keepdims=True))
    return dx, jnp.sum(dy * n, axis=0, keepdims=True)


def _sigmoid(x):
    return 1.0 / (1.0 + jnp.exp(-x))


_GELU_K = math.sqrt(2.0 / math.pi)


def _gelu(x):
    t = jnp.tanh(_GELU_K * (x + 0.044715 * x * x * x))
    return 0.5 * x * (1.0 + t), t


def _gelu_grad(x, t):
    return 0.5 * (1.0 + t) + 0.5 * x * (1.0 - t * t) * _GELU_K * (1.0 + 3.0 * 0.044715 * x * x)


def _head_sum(x):
    parts = []
    for h in range(HEADS_PER_GROUP):
        sl = x[:, h * HEAD_DIM:(h + 1) * HEAD_DIM]
        parts.append(jnp.broadcast_to(jnp.sum(sl, axis=-1, keepdims=True), sl.shape))
    return jnp.concatenate(parts, axis=-1)


def _mix_weights(l0, l1, l2):
    mx = jnp.maximum(jnp.maximum(l0, l1), l2)
    e0, e1, e2 = jnp.exp(l0 - mx), jnp.exp(l1 - mx), jnp.exp(l2 - mx)
    inv = 1.0 / (e0 + e1 + e2)
    return e0 * inv, e1 * inv, e2 * inv


BLK = 128


def _slopes(g):
    return [2.0 ** (-8.0 * (g * HEADS_PER_GROUP + h + 1) / N_HEADS) for h in range(HEADS_PER_GROUP)]


def _attn_masks(dil):
    qi = lax.broadcasted_iota(jnp.int32, (BLK, BLK), 0)
    ki = lax.broadcasted_iota(jnp.int32, (BLK, BLK), 1)
    dist_c = qi - ki
    dist_p = BLK + qi - ki
    return dist_c >= 0, dist_p <= BLK, (dist_c * dil).astype(F32), (dist_p * dil).astype(F32)


def _window_mask(has_prev, dil):
    qi = lax.broadcasted_iota(jnp.int32, (BLK, 2 * BLK), 0)
    ki = lax.broadcasted_iota(jnp.int32, (BLK, 2 * BLK), 1)
    dist = BLK + qi - ki
    ok = jnp.logical_and(jnp.logical_and(dist >= 0, dist <= BLK), jnp.logical_or(ki >= BLK, has_prev))
    return ok, (dist * dil).astype(F32)


def attn_fwd(qkv, g, name):
    dil, length, _ = qkv.shape
    scale = HEAD_DIM ** -0.5
    slopes = _slopes(g)

    def body(q_ref, kc_ref, vc_ref, kp_ref, vp_ref, o_ref, l_ref):
        ok, dist = _window_mask(pl.program_id(1) > 0, dil)
        for h in range(HEADS_PER_GROUP):
            sl = slice(h * HEAD_DIM, (h + 1) * HEAD_DIM)
            k2 = jnp.concatenate([kp_ref[:, sl], kc_ref[:, sl]], axis=0)
            v2 = jnp.concatenate([vp_ref[:, sl], vc_ref[:, sl]], axis=0)
            s = lax.dot_general(q_ref[:, sl], k2, _DN["nt"], preferred_element_type=F32) * scale - slopes[h] * dist
            s = jnp.where(ok, s, NEG)
            mx = jnp.max(s, axis=-1, keepdims=True)
            p = jnp.exp(s - mx)
            den = jnp.sum(p, axis=-1, keepdims=True)
            o_ref[:, sl] = (jnp.dot(p.astype(BF16), v2, preferred_element_type=F32) / den).astype(BF16)
            l_ref[:, sl] = jnp.broadcast_to(mx + jnp.log(den), (BLK, HEAD_DIM))

    def spec(col, prev):
        if prev:
            return pl.BlockSpec((None, BLK, GROUP_W), lambda r, n: (r, jnp.maximum(n - 1, 0), col))
        return pl.BlockSpec((None, BLK, GROUP_W), lambda r, n: (r, n, col))

    out_spec = pl.BlockSpec((None, BLK, GROUP_W), lambda r, n: (r, n, 0))
    return _pcall(
        body, name=name, grid=(dil, length // BLK),
        in_specs=[spec(0, False), spec(1, False), spec(2, False), spec(1, True), spec(2, True)],
        out_specs=[out_spec, out_spec],
        out_shape=[jax.ShapeDtypeStruct((dil, length, GROUP_W), BF16), jax.ShapeDtypeStruct((dil, length, GROUP_W), F32)],
        compiler_params=_cparams(("parallel", "parallel")),
    )(qkv, qkv, qkv, qkv, qkv)


def attn_bwd(qkv, dout, lse, dd, g, name, carry=None):
    dil, length, _ = qkv.shape
    nblk = length // BLK
    scale = HEAD_DIM ** -0.5
    slopes = _slopes(g)

    def body(q_ref, kc_ref, vc_ref, kp_ref, vp_ref, qn_ref, do_ref, don_ref, l_ref, ln_ref, d_ref, dn_ref, o_ref):
        n = pl.program_id(1)
        ok2, dist2 = _window_mask(n > 0, dil)
        _, ok_p, _, dp = _attn_masks(dil)
        ok_next = jnp.logical_and(ok_p, n < nblk - 1)
        for h in range(HEADS_PER_GROUP):
            sl = slice(h * HEAD_DIM, (h + 1) * HEAD_DIM)
            q, kc, vc, qn = q_ref[:, sl], kc_ref[:, sl], vc_ref[:, sl], qn_ref[:, sl]
            k2 = jnp.concatenate([kp_ref[:, sl], kc], axis=0)
            v2 = jnp.concatenate([vp_ref[:, sl], vc], axis=0)
            do, don = do_ref[:, sl], don_ref[:, sl]
            lse_q, lse_n, dd_q, dd_n = l_ref[:, sl], ln_ref[:, sl], d_ref[:, sl], dn_ref[:, sl]

            def probs(qq, kk, dist, ok, lse_t):
                s = lax.dot_general(qq, kk, _DN["nt"], preferred_element_type=F32) * scale - slopes[h] * dist
                return jnp.where(ok, jnp.exp(jnp.where(ok, s, NEG) - lse_t), 0.0)

            p2 = probs(q, k2, dist2, ok2, jnp.concatenate([lse_q, lse_q], axis=1))
            p_x = probs(qn, kc, dp, ok_next, lse_n)
            ds2 = p2 * (lax.dot_general(do, v2, _DN["nt"], preferred_element_type=F32) - jnp.concatenate([dd_q, dd_q], axis=1))
            ds_x = p_x * (lax.dot_general(don, vc, _DN["nt"], preferred_element_type=F32) - dd_n)
            dq = jnp.dot(ds2.astype(BF16), k2, preferred_element_type=F32)
            ds_k = jnp.concatenate([ds2[:, BLK:], ds_x], axis=0).astype(BF16)
            p_k = jnp.concatenate([p2[:, BLK:], p_x], axis=0).astype(BF16)
            dk = lax.dot_general(ds_k, jnp.concatenate([q, qn], axis=0), _DN["tn"], preferred_element_type=F32)
            dv = lax.dot_general(p_k, jnp.concatenate([do, don], axis=0), _DN["tn"], preferred_element_type=F32)
            o_ref[:, h * HEAD_DIM:(h + 1) * HEAD_DIM] = (dq * scale).astype(BF16)
            o_ref[:, GROUP_W + h * HEAD_DIM:GROUP_W + (h + 1) * HEAD_DIM] = (dk * scale).astype(BF16)
            o_ref[:, 2 * GROUP_W + h * HEAD_DIM:2 * GROUP_W + (h + 1) * HEAD_DIM] = dv.astype(BF16)

    def spec(col, which):
        if which == "prev":
            return pl.BlockSpec((None, BLK, GROUP_W), lambda r, n: (r, jnp.maximum(n - 1, 0), col))
        if which == "next":
            return pl.BlockSpec((None, BLK, GROUP_W), lambda r, n: (r, jnp.minimum(n + 1, nblk - 1), col))
        return pl.BlockSpec((None, BLK, GROUP_W), lambda r, n: (r, n, col))

    return _run(
        body, [qkv, qkv, qkv, qkv, qkv, qkv, dout, dout, lse, lse, dd, dd], carry=carry, name=name, grid=(dil, nblk),
        in_specs=[spec(0, "cur"), spec(1, "cur"), spec(2, "cur"), spec(1, "prev"), spec(2, "prev"), spec(0, "next"),
                  spec(0, "cur"), spec(0, "next"), spec(0, "cur"), spec(0, "next"), spec(0, "cur"), spec(0, "next")],
        out_specs=pl.BlockSpec((None, BLK, 3 * GROUP_W), lambda r, n: (r, n, 0)),
        out_shape=jax.ShapeDtypeStruct((dil, length, 3 * GROUP_W), BF16),
        compiler_params=_cparams(("parallel", "parallel")),
    )


def _ssm_prep_values(are, aim, logdt):
    dt = jnp.exp(logdt)
    mag = jnp.exp(are * dt)
    lb_re, lb_im = mag * jnp.cos(aim * dt), mag * jnp.sin(aim * dt)
    inv = 1.0 / (are * are + aim * aim)
    n_re, n_im = lb_re - 1.0, lb_im
    f_re = (n_re * are + n_im * aim) * inv
    f_im = (n_im * are - n_re * aim) * inv
    return dt, lb_re, lb_im, f_re, f_im, inv


PREP_G = 8


def _group_specs(are, logdt, bre):
    def spec(a):
        return pl.BlockSpec((PREP_G,) + a.shape[1:], lambda i: (i, 0, 0))
    return spec(are), spec(logdt), spec(bre)


def ssm_prep(are, aim, logdt, bre, bim):
    def body(are_r, aim_r, ldt_r, bre_r, bim_r, lre_o, lim_o, bbre_o, bbim_o):
        _, lb_re, lb_im, f_re, f_im, _ = _ssm_prep_values(are_r[...], aim_r[...], ldt_r[...])
        lre_o[...] = lb_re
        lim_o[...] = lb_im
        bbre_o[...] = f_re * bre_r[...] - f_im * bim_r[...]
        bbim_o[...] = f_re * bim_r[...] + f_im * bre_r[...]

    sh1 = jax.ShapeDtypeStruct(are.shape, F32)
    shb = jax.ShapeDtypeStruct(bre.shape, F32)
    s1, sd, sb = _group_specs(are, logdt, bre)
    return _pcall(body, name="ssm_prep", grid=(SSM_GROUPS // PREP_G,), in_specs=[s1, s1, sd, sb, sb], out_specs=[s1, s1, sb, sb],
                  out_shape=[sh1, sh1, shb, shb], compiler_params=_cparams(("parallel",)))(are, aim, logdt, bre, bim)


def ssm_prep_bwd(are, aim, logdt, bre, bim, dbbre, dbbim, dlre, dlim):
    def body(are_r, aim_r, ldt_r, bre_r, bim_r, dbbre_r, dbbim_r, dlre_r, dlim_r, dare_o, daim_o, dldt_o, dbre_o, dbim_o):
        are_v, aim_v = are_r[...], aim_r[...]
        dt, lb_re, lb_im, f_re, f_im, inv = _ssm_prep_values(are_v, aim_v, ldt_r[...])
        b_re, b_im, g_re, g_im = bre_r[...], bim_r[...], dbbre_r[...], dbbim_r[...]
        dbre_o[...] = f_re * g_re + f_im * g_im
        dbim_o[...] = f_re * g_im - f_im * g_re
        df_re = jnp.sum(b_re * g_re + b_im * g_im, axis=-1, keepdims=True)
        df_im = jnp.sum(b_re * g_im - b_im * g_re, axis=-1, keepdims=True)
        il_re, il_im = are_v * inv, -aim_v * inv
        cl_re = dlre_r[...] + il_re * df_re + il_im * df_im
        cl_im = dlim_r[...] + il_re * df_im - il_im * df_re
        q_re = -(f_re * il_re - f_im * il_im)
        q_im = -(f_re * il_im + f_im * il_re)
        ca_re = q_re * df_re + q_im * df_im
        ca_im = q_re * df_im - q_im * df_re
        cz_re = lb_re * cl_re + lb_im * cl_im
        cz_im = lb_re * cl_im - lb_im * cl_re
        dare_o[...] = ca_re + dt * cz_re
        daim_o[...] = ca_im + dt * cz_im
        dldt_o[...] = dt * jnp.sum(are_v * cz_re + aim_v * cz_im, axis=1, keepdims=True)

    sh1 = jax.ShapeDtypeStruct(are.shape, F32)
    shb = jax.ShapeDtypeStruct(bre.shape, F32)
    s1, sd, sb = _group_specs(are, logdt, bre)
    return _pcall(
        body, name="ssm_prep_bwd", grid=(SSM_GROUPS // PREP_G,), in_specs=[s1, s1, sd, sb, sb, sb, sb, s1, s1],
        out_specs=[s1, s1, sd, sb, sb], out_shape=[sh1, sh1, jax.ShapeDtypeStruct(logdt.shape, F32), shb, shb],
        compiler_params=_cparams(("parallel",)),
    )(are, aim, logdt, bre, bim, dbbre, dbbim, dlre, dlim)


SCAN_WC = 512


def _chain_segments(a_re, a_im, e_re, e_im, nsq, reverse):
    p_re, p_im = a_re, a_im
    for _ in range(nsq):
        p_re, p_im = p_re * p_re - p_im * p_im, 2.0 * p_re * p_im
    row = lax.broadcasted_iota(jnp.int32, e_re.shape, 0)
    edge = (row == SEGS - 1) if reverse else (row == 0)
    shift = SEGS - 1 if reverse else 1
    c_re, c_im = jnp.zeros_like(e_re), jnp.zeros_like(e_im)
    for _ in range(SEGS - 1):
        n_re = p_re * c_re - p_im * c_im + e_re
        n_im = p_re * c_im + p_im * c_re + e_im
        c_re = jnp.where(edge, 0.0, pltpu.roll(n_re, shift, 0))
        c_im = jnp.where(edge, 0.0, pltpu.roll(n_im, shift, 0))
    return c_re, c_im


def _scan_dims(s):
    steps = s // SEGS
    assert steps & (steps - 1) == 0
    tt = min(128, steps)
    return steps, tt, steps // tt, tt * SEGS, int(math.log2(steps))


U_BLK = SSM_W // BD


def ssm_fwd(u_s, dvec, w_bre, w_bim, w_cre, w_cim_neg, lre, lim, name, carry=None):
    s = u_s.shape[0]
    steps, tt, nch, rows, nsq = _scan_dims(s)
    nb, ub_w, wc = w_bre.shape

    def body(u_r, d_r, bre_r, bim_r, cre_r, cim_r, lre_r, lim_r, yg_o, ys_o, hre_o, him_o, hin_re_o, hin_im_o,
             st_re, st_im, x_re, x_im, h_re, h_im):
        ps, ch = pl.program_id(1), pl.program_id(2)
        a_re = jnp.broadcast_to(lre_r[...], (SEGS, wc))
        a_im = jnp.broadcast_to(lim_r[...], (SEGS, wc))
        ub = u_r[...]
        ub16 = ub.astype(BF16)
        x_re[...] = jnp.dot(ub16, bre_r[...], preferred_element_type=F32)
        x_im[...] = jnp.dot(ub16, bim_r[...], preferred_element_type=F32)

        @pl.when(jnp.logical_and(ps == 0, ch == 0))
        def _():
            st_re[...] = jnp.zeros_like(st_re)
            st_im[...] = jnp.zeros_like(st_im)

        @pl.when(jnp.logical_and(ps == 1, ch == 0))
        def _():
            c_re, c_im = _chain_segments(a_re, a_im, st_re[...], st_im[...], nsq, False)
            st_re[...] = c_re
            st_im[...] = c_im
            hin_re_o[...] = c_re
            hin_im_o[...] = c_im

        def run(store):
            def step(t, hc):
                off = pl.multiple_of(t * SEGS, SEGS)
                n_re = a_re * hc[0] - a_im * hc[1] + x_re[pl.ds(off, SEGS), :]
                n_im = a_re * hc[1] + a_im * hc[0] + x_im[pl.ds(off, SEGS), :]
                if store:
                    h_re[pl.ds(off, SEGS), :] = n_re
                    h_im[pl.ds(off, SEGS), :] = n_im
                return n_re, n_im

            fin = lax.fori_loop(0, tt, step, (st_re[...], st_im[...]))
            st_re[...] = fin[0]
            st_im[...] = fin[1]

        @pl.when(ps == 0)
        def _():
            run(False)

        @pl.when(ps == 1)
        def _():
            run(True)
            hr16, hi16 = h_re[...].astype(BF16), h_im[...].astype(BF16)
            hre_o[...] = hr16
            him_o[...] = hi16
            y = jnp.dot(hr16, cre_r[...], preferred_element_type=F32) + jnp.dot(hi16, cim_r[...], preferred_element_type=F32)
            y = y + d_r[...] * ub
            ys_o[...] = y
            yg_o[...] = _gelu(y)[0].astype(BF16)

    def pass1(ps, c):
        return jnp.where(ps == 1, c, 0)

    u_spec = pl.BlockSpec((rows, ub_w), lambda j, ps, c: (c, j))
    d_spec = pl.BlockSpec((1, ub_w), lambda j, ps, c: (0, j))
    b_spec = pl.BlockSpec((None, ub_w, wc), lambda j, ps, c: (j, 0, 0))
    c_spec = pl.BlockSpec((None, wc, ub_w), lambda j, ps, c: (j, 0, 0))
    l_spec = pl.BlockSpec((1, wc), lambda j, ps, c: (0, j))
    y_spec = pl.BlockSpec((rows, ub_w), lambda j, ps, c: (pass1(ps, c), j))
    h_spec = pl.BlockSpec((rows, wc), lambda j, ps, c: (pass1(ps, c), j))
    e_spec = pl.BlockSpec((SEGS, wc), lambda j, ps, c: (0, j))
    return _run(
        body, [u_s, dvec, w_bre, w_bim, w_cre, w_cim_neg, lre, lim], carry=carry, name=name, grid=(nb, 2, nch),
        in_specs=[u_spec, d_spec, b_spec, b_spec, c_spec, c_spec, l_spec, l_spec],
        out_specs=[y_spec, y_spec, h_spec, h_spec, e_spec, e_spec],
        out_shape=[jax.ShapeDtypeStruct((s, SSM_W), BF16), jax.ShapeDtypeStruct((s, SSM_W), F32),
                   jax.ShapeDtypeStruct((s, STATE_W), BF16), jax.ShapeDtypeStruct((s, STATE_W), BF16),
                   jax.ShapeDtypeStruct((SEGS, STATE_W), F32), jax.ShapeDtypeStruct((SEGS, STATE_W), F32)],
        scratch_shapes=[pltpu.VMEM((SEGS, wc), F32)] * 2 + [pltpu.VMEM((rows, wc), F32)] * 4,
        compiler_params=_cparams(("parallel", "arbitrary", "arbitrary")),
    )


def ssm_bwd(dyg_s, ys, u_s, h_re, h_im, hin_re, hin_im, gin_re, gin_im, dvec, w_bre_t, w_bim_t, w_cre_t, w_cim_neg_t, lre, lim,
            name, carry=None):
    s = u_s.shape[0]
    steps, tt, nch, rows, nsq = _scan_dims(s)
    half = 2 * SEGS

    def body(dyg_r, ys_r, u_r, hre_r, him_r, pre_r, pim_r, cin_re_r, cin_im_r, gin_re_r, gin_im_r, d_r, bre_r, bim_r, cre_r,
             cim_r, lre_r, lim_r, du_o, dbre_o, dbim_o, dcre_o, dcim_o, dlre_o, dlim_o, dd_o,
             st_re, st_im, x_re, x_im, g_re, g_im, hf_re, hf_im):
        ch = pl.program_id(1)
        a_re = jnp.broadcast_to(lre_r[...], (SEGS, SCAN_WC))
        a_im = -jnp.broadcast_to(lim_r[...], (SEGS, SCAN_WC))
        ub, y = u_r[...], ys_r[...]
        dy = dyg_r[...] * _gelu_grad(y, _gelu(y)[1])
        dy16 = dy.astype(BF16)
        x_re[...] = jnp.dot(dy16, cre_r[...], preferred_element_type=F32)
        x_im[...] = jnp.dot(dy16, cim_r[...], preferred_element_type=F32)

        @pl.when(ch == 0)
        def _():
            st_re[...] = gin_re_r[...]
            st_im[...] = gin_im_r[...]
            dlre_o[...] = jnp.zeros_like(dlre_o)
            dlim_o[...] = jnp.zeros_like(dlim_o)

        hf_re[...] = hre_r[...].astype(F32)
        hf_im[...] = him_r[...].astype(F32)
        first_chunk = ch == nch - 1
        edge_re = jnp.where(first_chunk, cin_re_r[...], pre_r[...].astype(F32)[SEGS:, :])
        edge_im = jnp.where(first_chunk, cin_im_r[...], pim_r[...].astype(F32)[SEGS:, :])

        def step(i, hc):
            t = tt - 1 - i
            off = pl.multiple_of(t * SEGS, SEGS)
            n_re = a_re * hc[0] - a_im * hc[1] + x_re[pl.ds(off, SEGS), :]
            n_im = a_re * hc[1] + a_im * hc[0] + x_im[pl.ds(off, SEGS), :]
            g_re[pl.ds(off, SEGS), :] = n_re
            g_im[pl.ds(off, SEGS), :] = n_im
            offp = pl.multiple_of(jnp.maximum(t - 1, 0) * SEGS, SEGS)
            hp_re = jnp.where(t == 0, edge_re, hf_re[pl.ds(offp, SEGS), :])
            hp_im = jnp.where(t == 0, edge_im, hf_im[pl.ds(offp, SEGS), :])
            return n_re, n_im, hc[2] + hp_re * n_re + hp_im * n_im, hc[3] + hp_re * n_im - hp_im * n_re

        fin = lax.fori_loop(0, tt, step, (st_re[...], st_im[...], dlre_o[...], dlim_o[...]))
        st_re[...] = fin[0]
        st_im[...] = fin[1]
        dlre_o[...] = fin[2]
        dlim_o[...] = fin[3]

        gr16, gi16 = g_re[...].astype(BF16), g_im[...].astype(BF16)
        du = jnp.dot(gr16, bre_r[...], preferred_element_type=F32) + jnp.dot(gi16, bim_r[...], preferred_element_type=F32)
        du_o[...] = du + d_r[...] * dy
        ub16 = ub.astype(BF16)
        parts = [
            (dbre_o, lax.dot_general(ub16, gr16, _DN["tn"], preferred_element_type=F32)),
            (dbim_o, lax.dot_general(ub16, gi16, _DN["tn"], preferred_element_type=F32)),
            (dcre_o, lax.dot_general(hre_r[...], dy16, _DN["tn"], preferred_element_type=F32)),
            (dcim_o, lax.dot_general(him_r[...], dy16, _DN["tn"], preferred_element_type=F32)),
            (dd_o, jnp.sum(dy * ub, axis=0, keepdims=True)),
        ]
        for ref, val in parts:
            @pl.when(ch == 0)
            def _(ref=ref, val=val):
                ref[...] = val

            @pl.when(ch > 0)
            def _(ref=ref, val=val):
                ref[...] += val

    def chunk(c):
        return nch - 1 - c

    u_spec = pl.BlockSpec((rows, U_BLK), lambda j, c: (chunk(c), j))
    h_spec = pl.BlockSpec((rows, SCAN_WC), lambda j, c: (chunk(c), j))
    prev_spec = pl.BlockSpec((half, SCAN_WC), lambda j, c: (jnp.maximum(chunk(c) * (rows // half) - 1, 0), j))
    e_spec = pl.BlockSpec((SEGS, SCAN_WC), lambda j, c: (0, j))
    d_spec = pl.BlockSpec((1, U_BLK), lambda j, c: (0, j))
    bt_spec = pl.BlockSpec((None, SCAN_WC, U_BLK), lambda j, c: (j, 0, 0))
    ct_spec = pl.BlockSpec((None, U_BLK, SCAN_WC), lambda j, c: (j, 0, 0))
    l_spec = pl.BlockSpec((1, SCAN_WC), lambda j, c: (0, j))
    return _run(
        body, [dyg_s, ys, u_s, h_re, h_im, h_re, h_im, hin_re, hin_im, gin_re, gin_im, dvec, w_bre_t, w_bim_t, w_cre_t,
               w_cim_neg_t, lre, lim],
        carry=carry, name=name, grid=(BD, nch),
        in_specs=[u_spec, u_spec, u_spec, h_spec, h_spec, prev_spec, prev_spec, e_spec, e_spec, e_spec, e_spec, d_spec,
                  bt_spec, bt_spec, ct_spec, ct_spec, l_spec, l_spec],
        out_specs=[u_spec, ct_spec, ct_spec, bt_spec, bt_spec, e_spec, e_spec, d_spec],
        out_shape=[jax.ShapeDtypeStruct((s, SSM_W), F32)] + [jax.ShapeDtypeStruct((BD, U_BLK, SCAN_WC), F32)] * 2
        + [jax.ShapeDtypeStruct((BD, SCAN_WC, U_BLK), F32)] * 2 + [jax.ShapeDtypeStruct((SEGS, STATE_W), F32)] * 2
        + [jax.ShapeDtypeStruct((1, SSM_W), F32)],
        scratch_shapes=[pltpu.VMEM((SEGS, SCAN_WC), F32)] * 2 + [pltpu.VMEM((rows, SCAN_WC), F32)] * 6,
        compiler_params=_cparams(("parallel", "arbitrary")),
    )


def ssm_bwd_ends(dyg_s, ys, w_cre_t, w_cim_neg_t, lre, lim, name):
    s = ys.shape[0]
    steps, tt, nch, rows, nsq = _scan_dims(s)
    nb, ub_w, wc = w_cre_t.shape

    def body(dyg_r, ys_r, cre_r, cim_r, lre_r, lim_r, gin_re_o, gin_im_o, st_re, st_im, x_re, x_im):
        ch = pl.program_id(1)
        a_re = jnp.broadcast_to(lre_r[...], (SEGS, wc))
        a_im = -jnp.broadcast_to(lim_r[...], (SEGS, wc))
        y = ys_r[...]
        dy16 = (dyg_r[...] * _gelu_grad(y, _gelu(y)[1])).astype(BF16)
        x_re[...] = jnp.dot(dy16, cre_r[...], preferred_element_type=F32)
        x_im[...] = jnp.dot(dy16, cim_r[...], preferred_element_type=F32)

        @pl.when(ch == 0)
        def _():
            st_re[...] = jnp.zeros_like(st_re)
            st_im[...] = jnp.zeros_like(st_im)

        def step(i, hc):
            off = pl.multiple_of((tt - 1 - i) * SEGS, SEGS)
            return (a_re * hc[0] - a_im * hc[1] + x_re[pl.ds(off, SEGS), :],
                    a_re * hc[1] + a_im * hc[0] + x_im[pl.ds(off, SEGS), :])

        fin = lax.fori_loop(0, tt, step, (st_re[...], st_im[...]))
        st_re[...] = fin[0]
        st_im[...] = fin[1]

        @pl.when(ch == nch - 1)
        def _():
            c_re, c_im = _chain_segments(a_re, a_im, fin[0], fin[1], nsq, True)
            gin_re_o[...] = c_re
            gin_im_o[...] = c_im

    y_spec = pl.BlockSpec((rows, ub_w), lambda j, c: (nch - 1 - c, j))
    ct_spec = pl.BlockSpec((None, ub_w, wc), lambda j, c: (j, 0, 0))
    l_spec = pl.BlockSpec((1, wc), lambda j, c: (0, j))
    e_spec = pl.BlockSpec((SEGS, wc), lambda j, c: (0, j))
    return _pcall(
        body, name=name, grid=(nb, nch), in_specs=[y_spec, y_spec, ct_spec, ct_spec, l_spec, l_spec], out_specs=[e_spec, e_spec],
        out_shape=[jax.ShapeDtypeStruct((SEGS, STATE_W), F32)] * 2,
        scratch_shapes=[pltpu.VMEM((SEGS, wc), F32)] * 2 + [pltpu.VMEM((rows, wc), F32)] * 2,
        compiler_params=_cparams(("parallel", "arbitrary")),
    )(dyg_s, ys, w_cre_t, w_cim_neg_t, lre, lim)


FWD_BD = 4


def _block_diag(m, nb=BD):
    g, r, c = m.shape
    m = m.reshape(nb, g // nb, r, c)
    eye = jnp.eye(g // nb, dtype=m.dtype)
    return jnp.einsum("jarc,ab->jarbc", m, eye).reshape(nb, (g // nb) * r, (g // nb) * c)


def _block_diag_extract(m, r, c):
    per = m.shape[1] // r
    m = m.reshape(BD, per, r, per, c)
    return jnp.einsum("jarac->jarc", m).reshape(BD * per, r, c)


def to_segments(a):
    s, w = a.shape
    return a.reshape(SEGS, s // SEGS, w).transpose(1, 0, 2).reshape(s, w)


def from_segments(a):
    s, w = a.shape
    return a.reshape(s // SEGS, SEGS, w).transpose(1, 0, 2).reshape(s, w)


W_IN_CHUNK_ROWS = (512, 512, 512, 512)


def _row_chunks(blocks, sizes):
    assert sum(sizes) == blocks.shape[1]
    out, at = [], 0
    for n in sizes:
        out.append(AllToAll([blocks[:, at:at + n]]))
        at += n
    return out
TALL_TM = 2048
FFN_TN = 512


def local_step(x, target, shards, small):
    s = x.shape[0]
    g1, g2, g3, g4 = (small[k].reshape(1, D_MODEL) for k in ("norm_mix_pre", "norm_mix_post", "norm_ffn_pre", "norm_ffn_post"))
    dvec = small["ssm_d"].reshape(1, SSM_W)
    wts, recv = {}, {}

    def gathered(names, blocks):
        for n, b in zip(names, blocks):
            wts[n] = _full_from_gathered(b, n)

    def rms_in_fn(r, c):
        hh = _rms(r[0], c[0])[0].astype(BF16)
        return [hh, _permute(_perm_matrix(PERM_TS, 4, False), hh), _permute(_perm_matrix(PERM_TS, 16, False), hh)], []

    (h, h4, h16), got = rowwise("rms_in", rms_in_fn, [x], [g1], [(D_MODEL, BF16), (D_MODEL, BF16, 4), (D_MODEL, BF16, 16)],
                                ts=PERM_TS, carry=Gather([shards["w_in"]]))
    w_in_t = _full_from_gathered(got[0], "w_in")
    w_u_t, w_gates_t = w_in_t[3 * HQ:3 * HQ + SSM_W], w_in_t[3 * HQ + SSM_W:]

    def qkv_rows(g):
        return 3 * GROUP_W, lambda t: 3 * t + g

    hd = [h.reshape(1, s, D_MODEL), h4, h16]
    qkv = [None] * 3
    names = ("w_attn_up", "w_glu_v", "w_glu_g")
    qkv[0], got = mm([(hd[0].reshape(s, D_MODEL), w_in_t)], "nt", BF16, "mm_qkv0", tm=TALL_TM, tn=GROUP_W, b_window=qkv_rows(0),
                     carry=Gather([shards[n] for n in names]))
    gathered(names, got)
    qkv[1], got = mm([(hd[1].reshape(s, D_MODEL), w_in_t)], "nt", BF16, "mm_qkv1", tm=TALL_TM, tn=GROUP_W, b_window=qkv_rows(1),
                     carry=Gather([shards["w_out"]]))
    gathered(("w_out",), got)
    qkv[2] = mm([(hd[2].reshape(s, D_MODEL), w_in_t)], "nt", BF16, "mm_qkv2", tm=TALL_TM, tn=GROUP_W, b_window=qkv_rows(2))
    u = mm([(h, w_u_t)], "nt", F32, "mm_u")
    gates, got = mm([(h, w_gates_t)], "nt", BF16, "mm_gates", carry=Gather([shards["w_ffn_gate"]]))
    gathered(("w_ffn_gate",), got)

    outs, lses = [], []
    for g, (_, dil) in enumerate(ATTN_GROUPS):
        o, l = attn_fwd(qkv[g].reshape(dil, s // dil, 3 * GROUP_W), g, f"attn_fwd{g}")
        outs.append(o.reshape(s, GROUP_W) if dil == 1 else o)
        lses.append(l.reshape(s, GROUP_W) if dil == 1 else l)

    def natural(r):
        back4, back16 = _perm_matrix(PERM_TS, 4, True), _perm_matrix(PERM_TS, 16, True)
        return (r[0], _permute(back4, r[1].astype(BF16)), _permute(back16, r[2].astype(BF16)),
                r[3], _permute(back4, r[4]), _permute(back16, r[5]))

    def merge_fn(r, c):
        o0, o1, o2, l0, l1, l2 = natural(r)
        w0, w1, w2 = _mix_weights(l0, l1, l2)
        return [w0 * o0 + w1 * o1 + w2 * o2], []

    (attn,) = rowwise("attn_merge", merge_fn, outs + lses, [], [(GROUP_W, BF16)], ts=PERM_TS)
    attn_branch = mm([(attn, wts["w_attn_up"])], "nn", BF16, "mm_up", tm=TALL_TM)

    are3 = small["ssm_a_re"].reshape(SSM_GROUPS, SSM_STATE, 1)
    aim3 = small["ssm_a_im"].reshape(SSM_GROUPS, SSM_STATE, 1)
    ldt3 = small["ssm_log_dt"].reshape(SSM_GROUPS, 1, 1)
    bre3 = small["ssm_b_re"].reshape(SSM_GROUPS, SSM_STATE, SSM_GROUP)
    bim3 = small["ssm_b_im"].reshape(SSM_GROUPS, SSM_STATE, SSM_GROUP)
    cre3 = small["ssm_c_re"].reshape(SSM_GROUPS, SSM_GROUP, SSM_STATE)
    cim3 = small["ssm_c_im"].reshape(SSM_GROUPS, SSM_GROUP, SSM_STATE)
    lre3, lim3, bbre, bbim = ssm_prep(are3, aim3, ldt3, bre3, bim3)
    lre, lim = lre3.reshape(1, STATE_W), lim3.reshape(1, STATE_W)
    w_bre = _block_diag(bbre.transpose(0, 2, 1)).astype(BF16)
    w_bim = _block_diag(bbim.transpose(0, 2, 1)).astype(BF16)
    w_cre = _block_diag(cre3.transpose(0, 2, 1)).astype(BF16)
    w_cim = _block_diag(cim3.transpose(0, 2, 1)).astype(BF16)
    u_s = to_segments(u)
    fwd_w = [_block_diag(t.transpose(0, 2, 1), FWD_BD).astype(BF16) for t in (bbre, bbim, cre3, -cim3)]
    (yg_s, y_ssm, h_re, h_im, hin_re, hin_im), got = ssm_fwd(
        u_s, dvec, *fwd_w, lre, lim, "ssm_fwd", carry=Gather([shards["w_ffn_up"]], pass_early=True))
    gathered(("w_ffn_up",), got)
    yg = from_segments(yg_s)
    gv = mm([(yg, wts["w_glu_v"])], "nn", BF16, "mm_glu_v", tm=TALL_TM)
    gg = mm([(yg, wts["w_glu_g"])], "nn", BF16, "mm_glu_g", tm=TALL_TM)

    def gate_fn(r, c):
        gts, ab, gv_, gg_ = r
        sa, ss = _sigmoid(gts[:, :D_MODEL]), _sigmoid(gts[:, D_MODEL:])
        return [sa * ab + ss * (gv_ * _sigmoid(gg_))], []

    (merged,) = rowwise("gate_merge", gate_fn, [gates, attn_branch, gv, gg], [], [(D_MODEL, BF16)])
    o_mix = mm([(merged, wts["w_out"])], "nn", F32, "mm_out")

    def mid_fn(r, c):
        x1 = r[0] + _rms(r[1], c[0])[0]
        return [x1, _rms(x1, c[1])[0]], []

    x1, h2 = rowwise("rms_mid", mid_fn, [x, o_mix], [g2, g3], [(D_MODEL, F32), (D_MODEL, BF16)])
    (fa, fb, fin), got = mm([(h2, wts["w_ffn_gate"]), (h2, wts["w_ffn_up"])], "nt", [BF16, BF16, BF16], "mm_ffn_in", tn=FFN_TN,
                            epilogue=lambda p, e: [p[0], p[1], p[0] * _sigmoid(p[0]) * p[1]],
                            carry=Gather([shards["w_ffn_down"]], pass_early=True))
    gathered(("w_ffn_down",), got)
    f = mm([(fin, wts["w_ffn_down"])], "nn", F32, "mm_ffn_down", tn=512, tk=D_FF)

    def loss_fn(r, c):
        x1_, f_, tgt = r
        y, n, rr = _rms(f_, c[0])
        err = x1_ + y - tgt
        dout = err * (1.0 / D_MODEL)
        df, dg = _rms_bwd(dout, n, rr, c[0])
        lp = 0.5 * jnp.sum(jnp.sum(err * err, axis=-1, keepdims=True) * (1.0 / D_MODEL), axis=0, keepdims=True)
        return [df, dout], [dg, lp]

    df, dout, dg4, loss_part = rowwise("loss_bwd", loss_fn, [x1, f, target], [g4], [(D_MODEL, BF16), (D_MODEL, BF16)],
                                       acc_outs=[(1, D_MODEL), (1, 1)])
    def sent(names, blocks):
        for n, b in zip(names, blocks):
            recv[n] = b

    def to_owners(names, dws):
        return AllToAll([_split_for_devices(d, n) for n, d in zip(names, dws)])

    def swiglu_bwd(p, e):
        dfin_, (a, b) = p[0], e
        sg = _sigmoid(a)
        return [dfin_ * b * (sg * (1.0 + a * (1.0 - sg))), dfin_ * a * sg]

    da, db = mm([(df, wts["w_ffn_down"])], "nt", [BF16, BF16], "mm_d_fin", tn=FFN_TN, epilogue=swiglu_bwd, extras=[fa, fb])
    dw_ffn_down = mm([(fin, df)], "tn", BF16, "mm_dw_ffn_down")
    dh2, got = mm([(da, wts["w_ffn_gate"]), (db, wts["w_ffn_up"])], "nn", F32, "mm_d_h2", tm=512, tn=1024, tk=D_FF // 2,
                  carry=to_owners(["w_ffn_down"], [dw_ffn_down]))
    sent(["w_ffn_down"], got)
    dw_ffn_gate = mm([(da, h2)], "tn", BF16, "mm_dw_ffn_gate")
    dw_ffn_up, got = mm([(db, h2)], "tn", BF16, "mm_dw_ffn_up", carry=to_owners(["w_ffn_gate"], [dw_ffn_gate]))
    sent(["w_ffn_gate"], got)

    def mid_bwd(r, c):
        dh2_, dout_, x1_, o_ = r
        _, n3, r3 = _rms(x1_, c[1])
        dx1, dg3_ = _rms_bwd(dh2_, n3, r3, c[1])
        dx1 = dx1 + dout_
        _, n2, r2 = _rms(o_, c[0])
        do_, dg2_ = _rms_bwd(dx1, n2, r2, c[0])
        return [dx1, do_], [dg2_, dg3_]

    dx1, do_mix, dg2, dg3 = rowwise("rms_mid_bwd", mid_bwd, [dh2, dout, x1, o_mix], [g2, g3], [(D_MODEL, F32), (D_MODEL, BF16)],
                                    acc_outs=[(1, D_MODEL), (1, D_MODEL)])
    dmerged = mm([(do_mix, wts["w_out"])], "nt", BF16, "mm_d_merged")
    dw_out = mm([(merged, do_mix)], "tn", BF16, "mm_dw_out")

    def gate_bwd(r, c):
        dm, gts, ab, gv_, gg_ = r
        sa, ss, sg = _sigmoid(gts[:, :D_MODEL]), _sigmoid(gts[:, D_MODEL:]), _sigmoid(gg_)
        branch = gv_ * sg
        dbranch = dm * ss
        dgates = jnp.concatenate([dm * ab * sa * (1.0 - sa), dm * branch * ss * (1.0 - ss)], axis=-1)
        return [dgates, dm * sa, dbranch * sg, dbranch * gv_ * sg * (1.0 - sg)], []

    dgates, dab, dgv, dgg = rowwise("gate_bwd", gate_bwd, [dmerged, gates, attn_branch, gv, gg], [],
                                    [(2 * D_MODEL, BF16), (D_MODEL, BF16), (D_MODEL, BF16), (D_MODEL, BF16)])
    dattn = mm([(dab, wts["w_attn_up"])], "nt", F32, "mm_d_attn")
    dw_up = mm([(attn, dab)], "tn", BF16, "mm_dw_up")
    dyg = mm([(dgv, wts["w_glu_v"]), (dgg, wts["w_glu_g"])], "nt", F32, "mm_d_yg")
    dw_glu_v = mm([(yg, dgv)], "tn", BF16, "mm_dw_glu_v")
    dw_glu_g = mm([(yg, dgg)], "tn", BF16, "mm_dw_glu_g")

    names = ["w_ffn_up", "w_out", "w_attn_up", "w_glu_v", "w_glu_g"]
    dyg_s = to_segments(dyg)
    gin_re, gin_im = ssm_bwd_ends(dyg_s, y_ssm, fwd_w[2].transpose(0, 2, 1), fwd_w[3].transpose(0, 2, 1), lre, lim, "ssm_bwd_ends")
    (du_s, dbre_d, dbim_d, dcre_d, dcim_d, dl_re8, dl_im8, dd_ssm), got = ssm_bwd(
        dyg_s, y_ssm, u_s, h_re, h_im, hin_re, hin_im, gin_re, gin_im, dvec, w_bre.transpose(0, 2, 1), w_bim.transpose(0, 2, 1),
        w_cre.transpose(0, 2, 1), -w_cim.transpose(0, 2, 1), lre, lim, "ssm_bwd",
        carry=to_owners(names, [dw_ffn_up, dw_out, dw_up, dw_glu_v, dw_glu_g]))
    sent(names, got)
    dbb_re = _block_diag_extract(dbre_d, SSM_GROUP, SSM_STATE).transpose(0, 2, 1)
    dbb_im = _block_diag_extract(dbim_d, SSM_GROUP, SSM_STATE).transpose(0, 2, 1)
    dc_re = _block_diag_extract(dcre_d, SSM_STATE, SSM_GROUP).transpose(0, 2, 1)
    dc_im = -_block_diag_extract(dcim_d, SSM_STATE, SSM_GROUP).transpose(0, 2, 1)

    def fold8(r, c):
        return [], [jnp.sum(r[0], axis=0, keepdims=True), jnp.sum(r[1], axis=0, keepdims=True)]

    dl_re, dl_im = rowwise("ssm_dl_fold", fold8, [dl_re8, dl_im8], [], [], acc_outs=[(1, STATE_W), (1, STATE_W)], ts=SEGS)
    da_re, da_im, dldt, db_re, db_im = ssm_prep_bwd(
        are3, aim3, ldt3, bre3, bim3, dbb_re, dbb_im,
        dl_re.reshape(SSM_GROUPS, SSM_STATE, 1), dl_im.reshape(SSM_GROUPS, SSM_STATE, 1))
    du = from_segments(du_s)

    def merge_bwd(r, c):
        dat = r[0]
        o0, o1, o2, l0, l1, l2 = natural(r[1:])
        w0, w1, w2 = _mix_weights(l0, l1, l2)
        tot = _head_sum(dat * (w0 * o0 + w1 * o1 + w2 * o2))
        to4, to16 = _perm_matrix(PERM_TS, 4, False), _perm_matrix(PERM_TS, 16, False)
        return [w0 * dat, _permute(to4, (w1 * dat).astype(BF16)), _permute(to16, (w2 * dat).astype(BF16)),
                w0 * tot, _permute(to4, (w1 * tot).astype(BF16)), _permute(to16, (w2 * tot).astype(BF16))], []

    mb = rowwise("attn_merge_bwd", merge_bwd, [dattn] + outs + lses, [],
                 [(GROUP_W, BF16), (GROUP_W, BF16, 4), (GROUP_W, BF16, 16), (GROUP_W, BF16), (GROUP_W, BF16, 4), (GROUP_W, BF16, 16)],
                 ts=PERM_TS)
    dqs, dw_qkv = [], []
    for g, (_, dil) in enumerate(ATTN_GROUPS):
        dq = attn_bwd(qkv[g].reshape(dil, s // dil, 3 * GROUP_W), mb[g].reshape(dil, s // dil, GROUP_W),
                      lses[g].reshape(dil, s // dil, GROUP_W), mb[3 + g].reshape(dil, s // dil, GROUP_W),
                      g, f"attn_bwd{g}").reshape(s, 3 * GROUP_W)
        dqs.append(dq)
        dw_qkv.append(mm([(hd[g].reshape(s, D_MODEL), dq)], "tn", BF16, f"mm_dw_qkv{g}"))
    dw_u = mm([(h, du)], "tn", BF16, "mm_dw_u")
    dw_gates = mm([(h, dgates)], "tn", BF16, "mm_dw_gates")
    dw_in = jnp.concatenate(
        [dw_qkv[g][:, o * GROUP_W:(o + 1) * GROUP_W] for o in range(3) for g in range(3)] + [dw_u, dw_gates], axis=1)
    chunks = _row_chunks(_split_for_devices(dw_in, "w_in"), W_IN_CHUNK_ROWS)
    dh_parts, got_chunks = [], []
    for g, (_, dil) in enumerate(ATTN_GROUPS):
        dh_g, got = mm([(dqs[g], w_in_t)], "nn", BF16, f"mm_d_h_qkv{g}", tk=GROUP_W, b_window=qkv_rows(g), carry=chunks[g])
        got_chunks.append(got[0])
        dh_parts.append(dh_g if dil == 1 else dh_g.reshape(dil, s // dil, D_MODEL))
    dh_parts.append(mm([(du, w_u_t)], "nn", BF16, "mm_d_h_u"))
    dh_gates, got = mm([(dgates, w_gates_t)], "nn", BF16, "mm_d_h_gates", carry=chunks[3])
    got_chunks.append(got[0])
    dh_parts.append(dh_gates)
    recv["w_in"] = jnp.concatenate(got_chunks, axis=1)

    def in_bwd(r, c):
        dh1 = _permute(_perm_matrix(PERM_TS, 4, True), r[1].astype(BF16))
        dh2_ = _permute(_perm_matrix(PERM_TS, 16, True), r[2].astype(BF16))
        dh = r[0] + dh1 + dh2_ + r[3] + r[4]
        _, n1, r1 = _rms(r[6], c[0])
        dx, dg1_ = _rms_bwd(dh, n1, r1, c[0])
        return [dx + r[5]], [dg1_]

    grad_x, dg1 = rowwise("rms_in_bwd", in_bwd, dh_parts + [dx1, x], [g1], [(D_MODEL, F32)], acc_outs=[(1, D_MODEL)], ts=PERM_TS)

    dsmall = dict(norm_mix_pre=dg1, ssm_a_re=da_re, ssm_a_im=da_im, ssm_log_dt=dldt, ssm_b_re=db_re, ssm_b_im=db_im,
                  ssm_c_re=dc_re, ssm_c_im=dc_im, ssm_d=dd_ssm, norm_mix_post=dg2, norm_ffn_pre=dg3, norm_ffn_post=dg4)
    return loss_part, grad_x, recv, dsmall


def adamw(parts, w, m, v, name, carry=None):
    r, c = w.shape
    tr = r
    while tr > 8 and tr % 2 == 0 and tr * c * (8 * parts.dtype.itemsize + 28) * 2 > 24 * 1024 * 1024:
        tr //= 2
    assert r % tr == 0 and (tr % 8 == 0 or tr == r)
    c1, c2 = 1.0 / (1.0 - ADAM_B1 ** ADAM_STEP), 1.0 / (1.0 - ADAM_B2 ** ADAM_STEP)

    def body(p_ref, w_ref, m_ref, v_ref, g_o, d_o, m_o, v_o):
        g = p_ref[0].astype(F32)
        for i in range(1, N_DEV):
            g = g + p_ref[i].astype(F32)
        mn = ADAM_B1 * m_ref[...] + (1.0 - ADAM_B1) * g
        vn = ADAM_B2 * v_ref[...] + (1.0 - ADAM_B2) * (g * g)
        g_o[...] = g
        m_o[...] = mn
        v_o[...] = vn
        d_o[...] = -ADAM_LR * ((mn * c1) / (jnp.sqrt(vn * c2) + ADAM_EPS) + ADAM_WD * w_ref[...])

    blk = pl.BlockSpec((tr, c), lambda i: (i, 0))
    return _run(
        body, [parts, w, m, v], carry=carry, name=name, grid=(r // tr,),
        in_specs=[pl.BlockSpec((N_DEV, tr, c), lambda i: (0, i, 0)), blk, blk, blk],
        out_specs=[blk] * 4, out_shape=[jax.ShapeDtypeStruct((r, c), F32)] * 4, compiler_params=_cparams(("parallel",)),
    )


PACK_C = 1024
SHARDED = ("w_in", "w_attn_up", "w_glu_v", "w_glu_g", "w_out", "w_ffn_gate", "w_ffn_up", "w_ffn_down")
ROW_SHARDED = ("w_out", "w_ffn_down")
SENT_TRANSPOSED = ("w_in", "w_ffn_gate", "w_ffn_up")
GRAD_TRANSPOSED = ("w_ffn_gate", "w_ffn_up")
SMALL = ("norm_mix_pre", "ssm_a_re", "ssm_a_im", "ssm_log_dt", "ssm_b_re", "ssm_b_im", "ssm_c_re", "ssm_c_im", "ssm_d",
         "norm_mix_post", "norm_ffn_pre", "norm_ffn_post")
WEIGHTS = ("norm_mix_pre", "w_in", "w_attn_up", "ssm_a_re", "ssm_a_im", "ssm_log_dt", "ssm_b_re", "ssm_b_im", "ssm_c_re",
           "ssm_c_im", "ssm_d", "w_glu_v", "w_glu_g", "w_out", "norm_mix_post", "norm_ffn_pre", "w_ffn_gate", "w_ffn_up",
           "w_ffn_down", "norm_ffn_post")


def _pack(arrs, dtype, pad_rows_to=64):
    flat = jnp.concatenate([a.reshape(-1).astype(dtype) for a in arrs])
    n = flat.shape[0]
    rows = -(-n // PACK_C)
    rows = -(-rows // pad_rows_to) * pad_rows_to
    return jnp.pad(flat, (0, rows * PACK_C - n)).reshape(rows, PACK_C)


def _unpack(flat2d, shapes):
    flat = flat2d.reshape(-1)
    out, off = [], 0
    for shp in shapes:
        n = int(np.prod(shp))
        out.append(flat[off:off + n].reshape(shp))
        off += n
    return out


def _full_from_gathered(gathered, name):
    if name in ROW_SHARDED or name in SENT_TRANSPOSED:
        return gathered.reshape(-1, gathered.shape[2])
    return gathered.transpose(1, 0, 2).reshape(gathered.shape[1], -1)


def _split_for_devices(full, name):
    if name in ROW_SHARDED or name in GRAD_TRANSPOSED:
        return full.reshape(N_DEV, -1, full.shape[1])
    return full.reshape(full.shape[0], N_DEV, -1).transpose(1, 0, 2)


def kernel(x, norm_mix_pre, w_in, w_attn_up, ssm_a_re, ssm_a_im, ssm_log_dt, ssm_b_re, ssm_b_im, ssm_c_re, ssm_c_im, ssm_d, w_glu_v, w_glu_g, w_out, norm_mix_post, norm_ffn_pre, w_ffn_gate, w_ffn_up, w_ffn_down, norm_ffn_post, loss_target, m_norm_mix_pre, m_w_in, m_w_attn_up, m_ssm_a_re, m_ssm_a_im, m_ssm_log_dt, m_ssm_b_re, m_ssm_b_im, m_ssm_c_re, m_ssm_c_im, m_ssm_d, m_w_glu_v, m_w_glu_g, m_w_out, m_norm_mix_post, m_norm_ffn_pre, m_w_ffn_gate, m_w_ffn_up, m_w_ffn_down, m_norm_ffn_post, v_norm_mix_pre, v_w_in, v_w_attn_up, v_ssm_a_re, v_ssm_a_im, v_ssm_log_dt, v_ssm_b_re, v_ssm_b_im, v_ssm_c_re, v_ssm_c_im, v_ssm_d, v_w_glu_v, v_w_glu_g, v_w_out, v_norm_mix_post, v_norm_ffn_pre, v_w_ffn_gate, v_w_ffn_up, v_w_ffn_down, v_norm_ffn_post):
    args = dict(locals())
    wv = {n: args[n][0] for n in WEIGHTS}
    mv = {n: args["m_" + n][0] for n in WEIGHTS}
    vv = {n: args["v_" + n][0] for n in WEIGHTS}

    shards = {n: (wv[n].T if n in SENT_TRANSPOSED else wv[n]).astype(BF16) for n in SHARDED}
    small = {n: wv[n] for n in SMALL}
    loss_part, grad_x, recv, dsmall = local_step(x[0], loss_target[0], shards, small)
    for n in GRAD_TRANSPOSED:
        recv[n] = recv[n].transpose(0, 2, 1)

    small_shapes = [wv[n].shape for n in SMALL]
    res = {}
    res["w_in"], (sgather,) = adamw(recv["w_in"], wv["w_in"], mv["w_in"], vv["w_in"], "adamw_w_in",
                                    carry=Gather([_pack([dsmall[n] for n in SMALL], F32)]))
    for n in SHARDED[1:]:
        res[n] = adamw(recv[n], wv[n], mv[n], vv[n], "adamw_" + n)
    sres = adamw(sgather, _pack([wv[n] for n in SMALL], F32), _pack([mv[n] for n in SMALL], F32),
                 _pack([vv[n] for n in SMALL], F32), "adamw_small")
    sun = [_unpack(t, small_shapes) for t in sres]
    for k, n in enumerate(SMALL):
        res[n] = tuple(sun[t][k] for t in range(4))

    loss = lax.psum(loss_part[0, 0], ("x", "y", "c"))
    outs = [loss, grad_x[None]]
    for t in range(4):
        outs += [res[n][t][None] for n in WEIGHTS]
    return tuple(outs)
```

```python
import functools
import math

import numpy as np
import jax
import jax.numpy as jnp
from jax import lax
from jax.experimental import pallas as pl
from jax.experimental.pallas import tpu as pltpu

F32 = jnp.float32
BF16 = jnp.bfloat16

D_MODEL = 2048
HEAD_DIM = 128
HEADS_PER_GROUP = 4
ATTN_GROUPS = ((128, 1), (512, 4), (2048, 16))
N_HEADS = HEADS_PER_GROUP * len(ATTN_GROUPS)
GROUP_W = HEADS_PER_GROUP * HEAD_DIM
HQ = N_HEADS * HEAD_DIM
SSM_W = 1024
SSM_GROUP = 16
SSM_GROUPS = 64
SSM_STATE = 64
STATE_W = SSM_GROUPS * SSM_STATE
D_FF = 5632
EPS = 1e-6
N_DEV = 8
SEGS = 8
BD = 8

ADAM_LR, ADAM_B1, ADAM_B2, ADAM_EPS, ADAM_WD, ADAM_STEP = 0.001, 0.9, 0.999, 1e-08, 0.01, 10

VMEM_LIMIT = 56 * 1024 * 1024
HBM_SPEC = pl.BlockSpec(memory_space=pltpu.HBM)
MESH_ID = pl.DeviceIdType.MESH
NEG = -1e30


def _pcall(body, **kw):
    return pl.pallas_call(body, **kw)


def _cparams(sem=None):
    if sem is None:
        return pltpu.CompilerParams(vmem_limit_bytes=VMEM_LIMIT)
    return pltpu.CompilerParams(vmem_limit_bytes=VMEM_LIMIT, dimension_semantics=sem)


def _my_coords():
    return lax.axis_index("x"), lax.axis_index("y"), lax.axis_index("c")


class Gather:
    def __init__(self, xs, pass_early=False):
        self.arrays = list(xs)
        self.out_shapes = [jax.ShapeDtypeStruct((N_DEV,) + x.shape, x.dtype) for x in xs]
        self.pass_early = pass_early

    def _ctx(self, out_refs, send_sems, recv_sems):
        mx, my, mc = _my_coords()
        me, sibling = (mx, my, mc), (mx, my, 1 - mc)
        chips = [(1 - mx, my), (mx, 1 - my), (1 - mx, 1 - my)]

        def slot(a, px, py, pc):
            return out_refs[a].at[4 * px + 2 * py + pc]

        def copy(a, k, block, to, src=None):
            return pltpu.make_async_remote_copy(
                src_ref=slot(a, *block) if src is None else src, dst_ref=slot(a, *block),
                send_sem=send_sems.at[7 * a + k], recv_sem=recv_sems.at[7 * a + k], device_id=to, device_id_type=MESH_ID)

        return me, sibling, chips, mc, slot, copy

    def _first(self, a, x_refs, ctx):
        me, sibling, chips, mc, slot, copy = ctx
        return [copy(a, 0, me, sibling, src=x_refs[a])] + [copy(a, 1 + j, me, (*chip, mc), src=x_refs[a]) for j, chip in enumerate(chips)]

    def start(self, x_refs, out_refs, send_sems, recv_sems, local_sems):
        ctx = self._ctx(out_refs, send_sems, recv_sems)
        me, slot = ctx[0], ctx[4]
        for a in range(len(self.arrays)):
            pltpu.make_async_copy(x_refs[a], slot(a, *me), local_sems.at[a]).start()
            for cp in self._first(a, x_refs, ctx):
                cp.start()

    def _passed(self, ctx):
        me, sibling, chips, mc, slot, copy = ctx
        return [copy(a, 4 + j, (*chip, mc), sibling) for a in range(len(self.arrays)) for j, chip in enumerate(chips)]

    def middle(self, x_refs, out_refs, send_sems, recv_sems, local_sems):
        ctx = self._ctx(out_refs, send_sems, recv_sems)
        me, sibling, chips, mc, slot, copy = ctx
        for a in range(len(self.arrays)):
            for j, chip in enumerate(chips):
                copy(a, 1 + j, (*chip, mc), me).wait_recv()
                copy(a, 4 + j, (*chip, mc), sibling).start()

    def finish(self, x_refs, out_refs, send_sems, recv_sems, local_sems, passed_on=False):
        if not passed_on:
            self.middle(x_refs, out_refs, send_sems, recv_sems, local_sems)
        ctx = self._ctx(out_refs, send_sems, recv_sems)
        me, sibling, chips, mc, slot, copy = ctx
        na = len(self.arrays)
        passed = self._passed(ctx)
        for a in range(na):
            copy(a, 0, sibling, me).wait_recv()
            for j, chip in enumerate(chips):
                copy(a, 4 + j, (*chip, 1 - mc), me).wait_recv()
        for a in range(na):
            for cp in self._first(a, x_refs, ctx):
                cp.wait_send()
        for cp in passed:
            cp.wait_send()
        for a in range(na):
            pltpu.make_async_copy(x_refs[a], slot(a, *me), local_sems.at[a]).wait()


class AllToAll:
    def __init__(self, ps):
        self.arrays = list(ps)
        self.out_shapes = [jax.ShapeDtypeStruct(p.shape, p.dtype) for p in ps]

    def _copies(self, p_refs, out_refs, send_sems, recv_sems, local_sems):
        mx, my, mc = _my_coords()
        me = 4 * mx + 2 * my + mc
        local, remote = [], []
        for a in range(len(self.arrays)):
            local.append(pltpu.make_async_copy(p_refs[a].at[me], out_refs[a].at[me], local_sems.at[a]))
            for k in range(1, N_DEV):
                px, py, pc = mx ^ ((k >> 2) & 1), my ^ ((k >> 1) & 1), mc ^ (k & 1)
                remote.append(pltpu.make_async_remote_copy(
                    src_ref=p_refs[a].at[4 * px + 2 * py + pc], dst_ref=out_refs[a].at[me],
                    send_sem=send_sems.at[7 * a + k - 1], recv_sem=recv_sems.at[7 * a + k - 1],
                    device_id=(px, py, pc), device_id_type=MESH_ID))
        return local, remote

    def start(self, *refs):
        local, remote = self._copies(*refs)
        for cp in local + remote:
            cp.start()

    def finish(self, *refs):
        local, remote = self._copies(*refs)
        for cp in remote:
            cp.wait_recv()
        for cp in remote:
            cp.wait_send()
        for cp in local:
            cp.wait()


def _run(body, args, carry=None, **kw):
    if carry is None:
        return _pcall(body, **kw)(*args)
    grid = kw["grid"]
    single = not isinstance(kw["out_shape"], (list, tuple))
    in_specs = list(kw["in_specs"])
    out_specs = [kw["out_specs"]] if single else list(kw["out_specs"])
    out_shape = [kw["out_shape"]] if single else list(kw["out_shape"])
    scratch = list(kw.get("scratch_shapes", []))
    na, nin, nout, nscr = len(carry.arrays), len(in_specs), len(out_specs), len(scratch)
    steps = int(np.prod(grid))
    mid_step = (steps * 7) // 10 if getattr(carry, "pass_early", False) and steps >= 4 else None

    def carried(*refs):
        ins, cin = refs[:nin], refs[nin:nin + na]
        outs, cout = refs[nin + na:nin + na + nout], refs[nin + na + nout:nin + 2 * na + nout]
        scr = refs[nin + 2 * na + nout:nin + 2 * na + nout + nscr]
        sems = refs[nin + 2 * na + nout + nscr:]
        step = pl.program_id(0)
        for i in range(1, len(grid)):
            step = step * grid[i] + pl.program_id(i)

        @pl.when(step == 0)
        def _():
            carry.start(cin, cout, *sems)

        if mid_step is not None:
            @pl.when(step == mid_step)
            def _():
                carry.middle(cin, cout, *sems)

        body(*ins, *outs, *scr)

        @pl.when(step == steps - 1)
        def _():
            if mid_step is not None:
                carry.finish(cin, cout, *sems, passed_on=True)
            else:
                carry.finish(cin, cout, *sems)

    res = _pcall(
        carried, name=kw["name"], grid=grid, in_specs=in_specs + [HBM_SPEC] * na, out_specs=out_specs + [HBM_SPEC] * na,
        out_shape=out_shape + carry.out_shapes,
        scratch_shapes=scratch + [pltpu.SemaphoreType.DMA((7 * na,)), pltpu.SemaphoreType.DMA((7 * na,)), pltpu.SemaphoreType.DMA((na,))],
        compiler_params=_cparams(("arbitrary",) * len(grid)),
    )(*args, *carry.arrays)
    main = res[:nout]
    return (main[0] if single else main), list(res[nout:])


_DN = {"nn": (((1,), (0,)), ((), ())), "nt": (((1,), (1,)), ((), ())), "tn": (((0,), (0,)), ((), ()))}


LANE = 128
MM_TM, MM_TN, MM_TK = 1024, 1536, 2048


def _tile(n, cap):
    for t in range(min(cap, n) // LANE * LANE, 0, -LANE):
        if n % t == 0:
            return t
    raise ValueError(n)


DW_TM, DW_TN, DW_TK = 512, 512, 8192


def mm(pairs, mode, out_dtype, name, tm=None, tn=None, tk=None, carry=None, epilogue=None, extras=(), b_window=None):
    a0, b0 = pairs[0]
    if mode == "nn":
        (m, k), n = a0.shape, b0.shape[1]
    elif mode == "nt":
        (m, k), n = a0.shape, b0.shape[0]
    else:
        (k, m), n = a0.shape, b0.shape[1]
    if b_window is not None:
        assert mode in ("nn", "nt") and len(pairs) == 1
        if mode == "nt":
            n = b_window[0]
        else:
            assert k == b_window[0]
    caps = (DW_TM, DW_TN, DW_TK) if mode == "tn" else (MM_TM, MM_TN, MM_TK)
    tm, tn, tk = _tile(m, tm or caps[0]), _tile(n, tn or caps[1]), _tile(k, tk or caps[2])
    nk = k // tk
    npairs = len(pairs)
    nex = len(extras)
    fused = epilogue is not None
    assert not fused or nk == 1
    out_dtypes = list(out_dtype) if fused else [out_dtype]

    def body(*refs):
        prods = []
        for p in range(npairs):
            a = refs[2 * p][...].astype(BF16) if (p == 0 or pairs[p][0] is not pairs[p - 1][0]) else a
            b = refs[2 * p + 1][...].astype(BF16)
            prods.append(lax.dot_general(a, b, _DN[mode], preferred_element_type=F32))
        if fused:
            ex = [refs[2 * npairs + e][...].astype(F32) for e in range(nex)]
            for o_ref, val in zip(refs[2 * npairs + nex:], epilogue(prods, ex)):
                o_ref[...] = val.astype(o_ref.dtype)
            return
        o_ref = refs[2 * npairs]
        tot = prods[0]
        for d in prods[1:]:
            tot = tot + d
        if nk == 1:
            o_ref[...] = tot.astype(o_ref.dtype)
            return
        acc = refs[2 * npairs + 1]
        kk = pl.program_id(2)

        @pl.when(kk == 0)
        def _():
            acc[...] = tot

        @pl.when(kk > 0)
        def _():
            acc[...] += tot

        @pl.when(kk == nk - 1)
        def _():
            o_ref[...] = acc[...].astype(o_ref.dtype)

    rows_of = b_window[1] if b_window is not None else (lambda t: t)
    if mode == "nn":
        sp = [pl.BlockSpec((tm, tk), lambda i, j, kk: (i, kk)), pl.BlockSpec((tk, tn), lambda i, j, kk: (rows_of(kk), j))]
    elif mode == "nt":
        sp = [pl.BlockSpec((tm, tk), lambda i, j, kk: (i, kk)), pl.BlockSpec((tn, tk), lambda i, j, kk: (rows_of(j), kk))]
    else:
        sp = [pl.BlockSpec((tk, tm), lambda i, j, kk: (kk, i)), pl.BlockSpec((tk, tn), lambda i, j, kk: (kk, j))]
    o_spec = pl.BlockSpec((tm, tn), lambda i, j, kk: (i, j))
    out_shapes = [jax.ShapeDtypeStruct((m, n), dt) for dt in out_dtypes]
    return _run(
        body, [t for pr in pairs for t in pr] + list(extras), carry=carry, name=name, grid=(m // tm, n // tn, nk),
        in_specs=sp * npairs + [o_spec] * nex,
        out_specs=[o_spec] * len(out_shapes) if fused else o_spec,
        out_shape=out_shapes if fused else out_shapes[0],
        scratch_shapes=[pltpu.VMEM((tm, tn), F32)] if nk > 1 else [],
        compiler_params=_cparams(("parallel", "parallel", "arbitrary")),
    )


def rowwise(name, fn, row_ins, const_ins, row_outs, acc_outs=(), ts=None, carry=None):
    s = row_ins[0].shape[0]
    row_outs = [ro if len(ro) == 3 else (*ro, 1) for ro in row_outs]
    if ts is None:
        per_row = sum(a.shape[-1] * a.dtype.itemsize for a in row_ins) + sum(w * jnp.dtype(dt).itemsize for w, dt, _ in row_outs)
        ts = 512
        while ts > 8 and 2 * ts * per_row > 20 * 1024 * 1024:
            ts //= 2
    ts = min(ts, s)
    assert s % ts == 0
    nr, nc, no, na = len(row_ins), len(const_ins), len(row_outs), len(acc_outs)

    def body(*refs):
        rows = [r[...].reshape(ts, r.shape[-1]).astype(F32) for r in refs[:nr]]
        consts = [r[...] for r in refs[nr:nr + nc]]
        outs, accs = fn(rows, consts)
        for r, v in zip(refs[nr + nc:nr + nc + no], outs):
            r[...] = v.astype(r.dtype).reshape(r.shape)
        if na:
            first = pl.program_id(0) == 0
            for r, v in zip(refs[nr + nc + no:], accs):
                @pl.when(first)
                def _(r=r, v=v):
                    r[...] = v

                @pl.when(jnp.logical_not(first))
                def _(r=r, v=v):
                    r[...] += v

    def tile_spec(w, d):
        if d == 1:
            return pl.BlockSpec((ts, w), lambda i: (i, 0))
        return pl.BlockSpec((d, ts // d, w), lambda i: (0, i, 0))

    in_specs = [tile_spec(a.shape[-1], a.shape[0] if a.ndim == 3 else 1) for a in row_ins]
    in_specs += [pl.BlockSpec(c.shape, lambda i, nd=c.ndim: (0,) * nd) for c in const_ins]
    out_specs = [tile_spec(w, d) for w, _, d in row_outs]
    out_specs += [pl.BlockSpec(shp, lambda i, nd=len(shp): (0,) * nd) for shp in acc_outs]
    out_shape = [jax.ShapeDtypeStruct((s, w) if d == 1 else (d, s // d, w), dt) for w, dt, d in row_outs]
    out_shape += [jax.ShapeDtypeStruct(shp, F32) for shp in acc_outs]
    return _run(
        body, [*row_ins, *const_ins], carry=carry, name=name, grid=(s // ts,), in_specs=in_specs, out_specs=out_specs,
        out_shape=out_shape, compiler_params=_cparams(("arbitrary",)),
    )


PERM_TS = 256


def _perm_matrix(ts, d, inverse):
    i = lax.broadcasted_iota(jnp.int32, (ts, ts), 0)
    k = lax.broadcasted_iota(jnp.int32, (ts, ts), 1)
    per = ts // d
    src = (i % d) * per + i // d if inverse else (i % per) * d + i // per
    return jnp.where(k == src, 1.0, 0.0).astype(BF16)


def _permute(p, x):
    if x.dtype == BF16:
        return jnp.dot(p, x, preferred_element_type=F32)
    hi = x.astype(BF16)
    rest = x - hi.astype(F32)
    mid = rest.astype(BF16)
    lo = (rest - mid.astype(F32)).astype(BF16)
    out = jnp.dot(p, hi, preferred_element_type=F32) + jnp.dot(p, mid, preferred_element_type=F32)
    return out + jnp.dot(p, lo, preferred_element_type=F32)


def _rms(x, gain):
    r = lax.rsqrt(jnp.mean(x * x, axis=-1, keepdims=True) + EPS)
    n = x * r
    return n * gain, n, r


def _rms_bwd(dy, n, r, gain):
    dn = dy * gain
    dx = r * (dn - n * jnp.mean(dn * n, axis=-1, keepdims=True))
    return dx, jnp.sum(dy * n, axis=0, keepdims=True)


def _sigmoid(x):
    return 1.0 / (1.0 + jnp.exp(-x))


_GELU_K = math.sqrt(2.0 / math.pi)


def _gelu(x):
    t = jnp.tanh(_GELU_K * (x + 0.044715 * x * x * x))
    return 0.5 * x * (1.0 + t), t


def _gelu_grad(x, t):
    return 0.5 * (1.0 + t) + 0.5 * x * (1.0 - t * t) * _GELU_K * (1.0 + 3.0 * 0.044715 * x * x)


def _head_sum(x):
    parts = []
    for h in range(HEADS_PER_GROUP):
        sl = x[:, h * HEAD_DIM:(h + 1) * HEAD_DIM]
        parts.append(jnp.broadcast_to(jnp.sum(sl, axis=-1, keepdims=True), sl.shape))
    return jnp.concatenate(parts, axis=-1)


def _mix_weights(l0, l1, l2):
    mx = jnp.maximum(jnp.maximum(l0, l1), l2)
    e0, e1, e2 = jnp.exp(l0 - mx), jnp.exp(l1 - mx), jnp.exp(l2 - mx)
    inv = 1.0 / (e0 + e1 + e2)
    return e0 * inv, e1 * inv, e2 * inv


BLK = 128


def _slopes(g):
    return [2.0 ** (-8.0 * (g * HEADS_PER_GROUP + h + 1) / N_HEADS) for h in range(HEADS_PER_GROUP)]


def _attn_masks(dil):
    qi = lax.broadcasted_iota(jnp.int32, (BLK, BLK), 0)
    ki = lax.broadcasted_iota(jnp.int32, (BLK, BLK), 1)
    dist_c = qi - ki
    dist_p = BLK + qi - ki
    return dist_c >= 0, dist_p <= BLK, (dist_c * dil).astype(F32), (dist_p * dil).astype(F32)


def _window_mask(has_prev, dil):
    qi = lax.broadcasted_iota(jnp.int32, (BLK, 2 * BLK), 0)
    ki = lax.broadcasted_iota(jnp.int32, (BLK, 2 * BLK), 1)
    dist = BLK + qi - ki
    ok = jnp.logical_and(jnp.logical_and(dist >= 0, dist <= BLK), jnp.logical_or(ki >= BLK, has_prev))
    return ok, (dist * dil).astype(F32)


def attn_fwd(qkv, g, name):
    dil, length, _ = qkv.shape
    scale = HEAD_DIM ** -0.5
    slopes = _slopes(g)

    def body(q_ref, kc_ref, vc_ref, kp_ref, vp_ref, o_ref, l_ref):
        ok, dist = _window_mask(pl.program_id(1) > 0, dil)
        for h in range(HEADS_PER_GROUP):
            sl = slice(h * HEAD_DIM, (h + 1) * HEAD_DIM)
            k2 = jnp.concatenate([kp_ref[:, sl], kc_ref[:, sl]], axis=0)
            v2 = jnp.concatenate([vp_ref[:, sl], vc_ref[:, sl]], axis=0)
            s = lax.dot_general(q_ref[:, sl], k2, _DN["nt"], preferred_element_type=F32) * scale - slopes[h] * dist
            s = jnp.where(ok, s, NEG)
            mx = jnp.max(s, axis=-1, keepdims=True)
            p = jnp.exp(s - mx)
            den = jnp.sum(p, axis=-1, keepdims=True)
            o_ref[:, sl] = (jnp.dot(p.astype(BF16), v2, preferred_element_type=F32) / den).astype(BF16)
            l_ref[:, sl] = jnp.broadcast_to(mx + jnp.log(den), (BLK, HEAD_DIM))

    def spec(col, prev):
        if prev:
            return pl.BlockSpec((None, BLK, GROUP_W), lambda r, n: (r, jnp.maximum(n - 1, 0), col))
        return pl.BlockSpec((None, BLK, GROUP_W), lambda r, n: (r, n, col))

    out_spec = pl.BlockSpec((None, BLK, GROUP_W), lambda r, n: (r, n, 0))
    return _pcall(
        body, name=name, grid=(dil, length // BLK),
        in_specs=[spec(0, False), spec(1, False), spec(2, False), spec(1, True), spec(2, True)],
        out_specs=[out_spec, out_spec],
        out_shape=[jax.ShapeDtypeStruct((dil, length, GROUP_W), BF16), jax.ShapeDtypeStruct((dil, length, GROUP_W), F32)],
        compiler_params=_cparams(("parallel", "parallel")),
    )(qkv, qkv, qkv, qkv, qkv)


def attn_bwd(qkv, dout, lse, dd, g, name, carry=None):
    dil, length, _ = qkv.shape
    nblk = length // BLK
    scale = HEAD_DIM ** -0.5
    slopes = _slopes(g)

    def body(q_ref, kc_ref, vc_ref, kp_ref, vp_ref, qn_ref, do_ref, don_ref, l_ref, ln_ref, d_ref, dn_ref, o_ref):
        n = pl.program_id(1)
        ok2, dist2 = _window_mask(n > 0, dil)
        _, ok_p, _, dp = _attn_masks(dil)
        ok_next = jnp.logical_and(ok_p, n < nblk - 1)
        for h in range(HEADS_PER_GROUP):
            sl = slice(h * HEAD_DIM, (h + 1) * HEAD_DIM)
            q, kc, vc, qn = q_ref[:, sl], kc_ref[:, sl], vc_ref[:, sl], qn_ref[:, sl]
            k2 = jnp.concatenate([kp_ref[:, sl], kc], axis=0)
            v2 = jnp.concatenate([vp_ref[:, sl], vc], axis=0)
            do, don = do_ref[:, sl], don_ref[:, sl]
            lse_q, lse_n, dd_q, dd_n = l_ref[:, sl], ln_ref[:, sl], d_ref[:, sl], dn_ref[:, sl]

            def probs(qq, kk, dist, ok, lse_t):
                s = lax.dot_general(qq, kk, _DN["nt"], preferred_element_type=F32) * scale - slopes[h] * dist
                return jnp.where(ok, jnp.exp(jnp.where(ok, s, NEG) - lse_t), 0.0)

            p2 = probs(q, k2, dist2, ok2, jnp.concatenate([lse_q, lse_q], axis=1))
            p_x = probs(qn, kc, dp, ok_next, lse_n)
            ds2 = p2 * (lax.dot_general(do, v2, _DN["nt"], preferred_element_type=F32) - jnp.concatenate([dd_q, dd_q], axis=1))
            ds_x = p_x * (lax.dot_general(don, vc, _DN["nt"], preferred_element_type=F32) - dd_n)
            dq = jnp.dot(ds2.astype(BF16), k2, preferred_element_type=F32)
            ds_k = jnp.concatenate([ds2[:, BLK:], ds_x], axis=0).astype(BF16)
            p_k = jnp.concatenate([p2[:, BLK:], p_x], axis=0).astype(BF16)
            dk = lax.dot_general(ds_k, jnp.concatenate([q, qn], axis=0), _DN["tn"], preferred_element_type=F32)
            dv = lax.dot_general(p_k, jnp.concatenate([do, don], axis=0), _DN["tn"], preferred_element_type=F32)
            o_ref[:, h * HEAD_DIM:(h + 1) * HEAD_DIM] = (dq * scale).astype(BF16)
            o_ref[:, GROUP_W + h * HEAD_DIM:GROUP_W + (h + 1) * HEAD_DIM] = (dk * scale).astype(BF16)
            o_ref[:, 2 * GROUP_W + h * HEAD_DIM:2 * GROUP_W + (h + 1) * HEAD_DIM] = dv.astype(BF16)

    def spec(col, which):
        if which == "prev":
            return pl.BlockSpec((None, BLK, GROUP_W), lambda r, n: (r, jnp.maximum(n - 1, 0), col))
        if which == "next":
            return pl.BlockSpec((None, BLK, GROUP_W), lambda r, n: (r, jnp.minimum(n + 1, nblk - 1), col))
        return pl.BlockSpec((None, BLK, GROUP_W), lambda r, n: (r, n, col))

    return _run(
        body, [qkv, qkv, qkv, qkv, qkv, qkv, dout, dout, lse, lse, dd, dd], carry=carry, name=name, grid=(dil, nblk),
        in_specs=[spec(0, "cur"), spec(1, "cur"), spec(2, "cur"), spec(1, "prev"), spec(2, "prev"), spec(0, "next"),
                  spec(0, "cur"), spec(0, "next"), spec(0, "cur"), spec(0, "next"), spec(0, "cur"), spec(0, "next")],
        out_specs=pl.BlockSpec((None, BLK, 3 * GROUP_W), lambda r, n: (r, n, 0)),
        out_shape=jax.ShapeDtypeStruct((dil, length, 3 * GROUP_W), BF16),
        compiler_params=_cparams(("parallel", "parallel")),
    )


def _ssm_prep_values(are, aim, logdt):
    dt = jnp.exp(logdt)
    mag = jnp.exp(are * dt)
    lb_re, lb_im = mag * jnp.cos(aim * dt), mag * jnp.sin(aim * dt)
    inv = 1.0 / (are * are + aim * aim)
    n_re, n_im = lb_re - 1.0, lb_im
    f_re = (n_re * are + n_im * aim) * inv
    f_im = (n_im * are - n_re * aim) * inv
    return dt, lb_re, lb_im, f_re, f_im, inv


PREP_G = 8


def _group_specs(are, logdt, bre):
    def spec(a):
        return pl.BlockSpec((PREP_G,) + a.shape[1:], lambda i: (i, 0, 0))
    return spec(are), spec(logdt), spec(bre)


def ssm_prep(are, aim, logdt, bre, bim):
    def body(are_r, aim_r, ldt_r, bre_r, bim_r, lre_o, lim_o, bbre_o, bbim_o):
        _, lb_re, lb_im, f_re, f_im, _ = _ssm_prep_values(are_r[...], aim_r[...], ldt_r[...])
        lre_o[...] = lb_re
        lim_o[...] = lb_im
        bbre_o[...] = f_re * bre_r[...] - f_im * bim_r[...]
        bbim_o[...] = f_re * bim_r[...] + f_im * bre_r[...]

    sh1 = jax.ShapeDtypeStruct(are.shape, F32)
    shb = jax.ShapeDtypeStruct(bre.shape, F32)
    s1, sd, sb = _group_specs(are, logdt, bre)
    return _pcall(body, name="ssm_prep", grid=(SSM_GROUPS // PREP_G,), in_specs=[s1, s1, sd, sb, sb], out_specs=[s1, s1, sb, sb],
                  out_shape=[sh1, sh1, shb, shb], compiler_params=_cparams(("parallel",)))(are, aim, logdt, bre, bim)


def ssm_prep_bwd(are, aim, logdt, bre, bim, dbbre, dbbim, dlre, dlim):
    def body(are_r, aim_r, ldt_r, bre_r, bim_r, dbbre_r, dbbim_r, dlre_r, dlim_r, dare_o, daim_o, dldt_o, dbre_o, dbim_o):
        are_v, aim_v = are_r[...], aim_r[...]
        dt, lb_re, lb_im, f_re, f_im, inv = _ssm_prep_values(are_v, aim_v, ldt_r[...])
        b_re, b_im, g_re, g_im = bre_r[...], bim_r[...], dbbre_r[...], dbbim_r[...]
        dbre_o[...] = f_re * g_re + f_im * g_im
        dbim_o[...] = f_re * g_im - f_im * g_re
        df_re = jnp.sum(b_re * g_re + b_im * g_im, axis=-1, keepdims=True)
        df_im = jnp.sum(b_re * g_im - b_im * g_re, axis=-1, keepdims=True)
        il_re, il_im = are_v * inv, -aim_v * inv
        cl_re = dlre_r[...] + il_re * df_re + il_im * df_im
        cl_im = dlim_r[...] + il_re * df_im - il_im * df_re
        q_re = -(f_re * il_re - f_im * il_im)
        q_im = -(f_re * il_im + f_im * il_re)
        ca_re = q_re * df_re + q_im * df_im
        ca_im = q_re * df_im - q_im * df_re
        cz_re = lb_re * cl_re + lb_im * cl_im
        cz_im = lb_re * cl_im - lb_im * cl_re
        dare_o[...] = ca_re + dt * cz_re
        daim_o[...] = ca_im + dt * cz_im
        dldt_o[...] = dt * jnp.sum(are_v * cz_re + aim_v * cz_im, axis=1, keepdims=True)

    sh1 = jax.ShapeDtypeStruct(are.shape, F32)
    shb = jax.ShapeDtypeStruct(bre.shape, F32)
    s1, sd, sb = _group_specs(are, logdt, bre)
    return _pcall(
        body, name="ssm_prep_bwd", grid=(SSM_GROUPS // PREP_G,), in_specs=[s1, s1, sd, sb, sb, sb, sb, s1, s1],
        out_specs=[s1, s1, sd, sb, sb], out_shape=[sh1, sh1, jax.ShapeDtypeStruct(logdt.shape, F32), shb, shb],
        compiler_params=_cparams(("parallel",)),
    )(are, aim, logdt, bre, bim, dbbre, dbbim, dlre, dlim)


SCAN_WC = 512


def _chain_segments(a_re, a_im, e_re, e_im, nsq, reverse):
    p_re, p_im = a_re, a_im
    for _ in range(nsq):
        p_re, p_im = p_re * p_re - p_im * p_im, 2.0 * p_re * p_im
    row = lax.broadcasted_iota(jnp.int32, e_re.shape, 0)
    edge = (row == SEGS - 1) if reverse else (row == 0)
    shift = SEGS - 1 if reverse else 1
    c_re, c_im = jnp.zeros_like(e_re), jnp.zeros_like(e_im)
    for _ in range(SEGS - 1):
        n_re = p_re * c_re - p_im * c_im + e_re
        n_im = p_re * c_im + p_im * c_re + e_im
        c_re = jnp.where(edge, 0.0, pltpu.roll(n_re, shift, 0))
        c_im = jnp.where(edge, 0.0, pltpu.roll(n_im, shift, 0))
    return c_re, c_im


def _scan_dims(s):
    steps = s // SEGS
    assert steps & (steps - 1) == 0
    tt = min(128, steps)
    return steps, tt, steps // tt, tt * SEGS, int(math.log2(steps))


U_BLK = SSM_W // BD


def ssm_fwd(u_s, dvec, w_bre, w_bim, w_cre, w_cim_neg, lre, lim, name, carry=None):
    s = u_s.shape[0]
    steps, tt, nch, rows, nsq = _scan_dims(s)
    nb, ub_w, wc = w_bre.shape

    def body(u_r, d_r, bre_r, bim_r, cre_r, cim_r, lre_r, lim_r, yg_o, ys_o, hre_o, him_o, hin_re_o, hin_im_o,
             st_re, st_im, x_re, x_im, h_re, h_im):
        ps, ch = pl.program_id(1), pl.program_id(2)
        a_re = jnp.broadcast_to(lre_r[...], (SEGS, wc))
        a_im = jnp.broadcast_to(lim_r[...], (SEGS, wc))
        ub = u_r[...]
        ub16 = ub.astype(BF16)
        x_re[...] = jnp.dot(ub16, bre_r[...], preferred_element_type=F32)
        x_im[...] = jnp.dot(ub16, bim_r[...], preferred_element_type=F32)

        @pl.when(jnp.logical_and(ps == 0, ch == 0))
        def _():
            st_re[...] = jnp.zeros_like(st_re)
            st_im[...] = jnp.zeros_like(st_im)

        @pl.when(jnp.logical_and(ps == 1, ch == 0))
        def _():
            c_re, c_im = _chain_segments(a_re, a_im, st_re[...], st_im[...], nsq, False)
            st_re[...] = c_re
            st_im[...] = c_im
            hin_re_o[...] = c_re
            hin_im_o[...] = c_im

        def run(store):
            def step(t, hc):
                off = pl.multiple_of(t * SEGS, SEGS)
                n_re = a_re * hc[0] - a_im * hc[1] + x_re[pl.ds(off, SEGS), :]
                n_im = a_re * hc[1] + a_im * hc[0] + x_im[pl.ds(off, SEGS), :]
                if store:
                    h_re[pl.ds(off, SEGS), :] = n_re
                    h_im[pl.ds(off, SEGS), :] = n_im
                return n_re, n_im

            fin = lax.fori_loop(0, tt, step, (st_re[...], st_im[...]))
            st_re[...] = fin[0]
            st_im[...] = fin[1]

        @pl.when(ps == 0)
        def _():
            run(False)

        @pl.when(ps == 1)
        def _():
            run(True)
            hr16, hi16 = h_re[...].astype(BF16), h_im[...].astype(BF16)
            hre_o[...] = hr16
            him_o[...] = hi16
            y = jnp.dot(hr16, cre_r[...], preferred_element_type=F32) + jnp.dot(hi16, cim_r[...], preferred_element_type=F32)
            y = y + d_r[...] * ub
            ys_o[...] = y
            yg_o[...] = _gelu(y)[0].astype(BF16)

    def pass1(ps, c):
        return jnp.where(ps == 1, c, 0)

    u_spec = pl.BlockSpec((rows, ub_w), lambda j, ps, c: (c, j))
    d_spec = pl.BlockSpec((1, ub_w), lambda j, ps, c: (0, j))
    b_spec = pl.BlockSpec((None, ub_w, wc), lambda j, ps, c: (j, 0, 0))
    c_spec = pl.BlockSpec((None, wc, ub_w), lambda j, ps, c: (j, 0, 0))
    l_spec = pl.BlockSpec((1, wc), lambda j, ps, c: (0, j))
    y_spec = pl.BlockSpec((rows, ub_w), lambda j, ps, c: (pass1(ps, c), j))
    h_spec = pl.BlockSpec((rows, wc), lambda j, ps, c: (pass1(ps, c), j))
    e_spec = pl.BlockSpec((SEGS, wc), lambda j, ps, c: (0, j))
    return _run(
        body, [u_s, dvec, w_bre, w_bim, w_cre, w_cim_neg, lre, lim], carry=carry, name=name, grid=(nb, 2, nch),
        in_specs=[u_spec, d_spec, b_spec, b_spec, c_spec, c_spec, l_spec, l_spec],
        out_specs=[y_spec, y_spec, h_spec, h_spec, e_spec, e_spec],
        out_shape=[jax.ShapeDtypeStruct((s, SSM_W), BF16), jax.ShapeDtypeStruct((s, SSM_W), F32),
                   jax.ShapeDtypeStruct((s, STATE_W), BF16), jax.ShapeDtypeStruct((s, STATE_W), BF16),
                   jax.ShapeDtypeStruct((SEGS, STATE_W), F32), jax.ShapeDtypeStruct((SEGS, STATE_W), F32)],
        scratch_shapes=[pltpu.VMEM((SEGS, wc), F32)] * 2 + [pltpu.VMEM((rows, wc), F32)] * 4,
        compiler_params=_cparams(("parallel", "arbitrary", "arbitrary")),
    )


def ssm_bwd(dyg_s, ys, u_s, h_re, h_im, hin_re, hin_im, gin_re, gin_im, dvec, w_bre_t, w_bim_t, w_cre_t, w_cim_neg_t, lre, lim,
            name, carry=None):
    s = u_s.shape[0]
    steps, tt, nch, rows, nsq = _scan_dims(s)
    half = 2 * SEGS

    def body(dyg_r, ys_r, u_r, hre_r, him_r, pre_r, pim_r, cin_re_r, cin_im_r, gin_re_r, gin_im_r, d_r, bre_r, bim_r, cre_r,
             cim_r, lre_r, lim_r, du_o, dbre_o, dbim_o, dcre_o, dcim_o, dlre_o, dlim_o, dd_o,
             st_re, st_im, x_re, x_im, g_re, g_im, hf_re, hf_im):
        ch = pl.program_id(1)
        a_re = jnp.broadcast_to(lre_r[...], (SEGS, SCAN_WC))
        a_im = -jnp.broadcast_to(lim_r[...], (SEGS, SCAN_WC))
        ub, y = u_r[...], ys_r[...]
        dy = dyg_r[...] * _gelu_grad(y, _gelu(y)[1])
        dy16 = dy.astype(BF16)
        x_re[...] = jnp.dot(dy16, cre_r[...], preferred_element_type=F32)
        x_im[...] = jnp.dot(dy16, cim_r[...], preferred_element_type=F32)

        @pl.when(ch == 0)
        def _():
            st_re[...] = gin_re_r[...]
            st_im[...] = gin_im_r[...]
            dlre_o[...] = jnp.zeros_like(dlre_o)
            dlim_o[...] = jnp.zeros_like(dlim_o)

        hf_re[...] = hre_r[...].astype(F32)
        hf_im[...] = him_r[...].astype(F32)
        first_chunk = ch == nch - 1
        edge_re = jnp.where(first_chunk, cin_re_r[...], pre_r[...].astype(F32)[SEGS:, :])
        edge_im = jnp.where(first_chunk, cin_im_r[...], pim_r[...].astype(F32)[SEGS:, :])

        def step(i, hc):
            t = tt - 1 - i
            off = pl.multiple_of(t * SEGS, SEGS)
            n_re = a_re * hc[0] - a_im * hc[1] + x_re[pl.ds(off, SEGS), :]
            n_im = a_re * hc[1] + a_im * hc[0] + x_im[pl.ds(off, SEGS), :]
            g_re[pl.ds(off, SEGS), :] = n_re
            g_im[pl.ds(off, SEGS), :] = n_im
            offp = pl.multiple_of(jnp.maximum(t - 1, 0) * SEGS, SEGS)
            hp_re = jnp.where(t == 0, edge_re, hf_re[pl.ds(offp, SEGS), :])
            hp_im = jnp.where(t == 0, edge_im, hf_im[pl.ds(offp, SEGS), :])
            return n_re, n_im, hc[2] + hp_re * n_re + hp_im * n_im, hc[3] + hp_re * n_im - hp_im * n_re

        fin = lax.fori_loop(0, tt, step, (st_re[...], st_im[...], dlre_o[...], dlim_o[...]))
        st_re[...] = fin[0]
        st_im[...] = fin[1]
        dlre_o[...] = fin[2]
        dlim_o[...] = fin[3]

        gr16, gi16 = g_re[...].astype(BF16), g_im[...].astype(BF16)
        du = jnp.dot(gr16, bre_r[...], preferred_element_type=F32) + jnp.dot(gi16, bim_r[...], preferred_element_type=F32)
        du_o[...] = du + d_r[...] * dy
        ub16 = ub.astype(BF16)
        parts = [
            (dbre_o, lax.dot_general(ub16, gr16, _DN["tn"], preferred_element_type=F32)),
            (dbim_o, lax.dot_general(ub16, gi16, _DN["tn"], preferred_element_type=F32)),
            (dcre_o, lax.dot_general(hre_r[...], dy16, _DN["tn"], preferred_element_type=F32)),
            (dcim_o, lax.dot_general(him_r[...], dy16, _DN["tn"], preferred_element_type=F32)),
            (dd_o, jnp.sum(dy * ub, axis=0, keepdims=True)),
        ]
        for ref, val in parts:
            @pl.when(ch == 0)
            def _(ref=ref, val=val):
                ref[...] = val

            @pl.when(ch > 0)
            def _(ref=ref, val=val):
                ref[...] += val

    def chunk(c):
        return nch - 1 - c

    u_spec = pl.BlockSpec((rows, U_BLK), lambda j, c: (chunk(c), j))
    h_spec = pl.BlockSpec((rows, SCAN_WC), lambda j, c: (chunk(c), j))
    prev_spec = pl.BlockSpec((half, SCAN_WC), lambda j, c: (jnp.maximum(chunk(c) * (rows // half) - 1, 0), j))
    e_spec = pl.BlockSpec((SEGS, SCAN_WC), lambda j, c: (0, j))
    d_spec = pl.BlockSpec((1, U_BLK), lambda j, c: (0, j))
    bt_spec = pl.BlockSpec((None, SCAN_WC, U_BLK), lambda j, c: (j, 0, 0))
    ct_spec = pl.BlockSpec((None, U_BLK, SCAN_WC), lambda j, c: (j, 0, 0))
    l_spec = pl.BlockSpec((1, SCAN_WC), lambda j, c: (0, j))
    return _run(
        body, [dyg_s, ys, u_s, h_re, h_im, h_re, h_im, hin_re, hin_im, gin_re, gin_im, dvec, w_bre_t, w_bim_t, w_cre_t,
               w_cim_neg_t, lre, lim],
        carry=carry, name=name, grid=(BD, nch),
        in_specs=[u_spec, u_spec, u_spec, h_spec, h_spec, prev_spec, prev_spec, e_spec, e_spec, e_spec, e_spec, d_spec,
                  bt_spec, bt_spec, ct_spec, ct_spec, l_spec, l_spec],
        out_specs=[u_spec, ct_spec, ct_spec, bt_spec, bt_spec, e_spec, e_spec, d_spec],
        out_shape=[jax.ShapeDtypeStruct((s, SSM_W), F32)] + [jax.ShapeDtypeStruct((BD, U_BLK, SCAN_WC), F32)] * 2
        + [jax.ShapeDtypeStruct((BD, SCAN_WC, U_BLK), F32)] * 2 + [jax.ShapeDtypeStruct((SEGS, STATE_W), F32)] * 2
        + [jax.ShapeDtypeStruct((1, SSM_W), F32)],
        scratch_shapes=[pltpu.VMEM((SEGS, SCAN_WC), F32)] * 2 + [pltpu.VMEM((rows, SCAN_WC), F32)] * 6,
        compiler_params=_cparams(("parallel", "arbitrary")),
    )


def ssm_bwd_ends(dyg_s, ys, w_cre_t, w_cim_neg_t, lre, lim, name, carry=None):
    s = ys.shape[0]
    steps, tt, nch, rows, nsq = _scan_dims(s)
    nb, ub_w, wc = w_cre_t.shape

    def body(dyg_r, ys_r, cre_r, cim_r, lre_r, lim_r, gin_re_o, gin_im_o, st_re, st_im, x_re, x_im):
        ch = pl.program_id(1)
        a_re = jnp.broadcast_to(lre_r[...], (SEGS, wc))
        a_im = -jnp.broadcast_to(lim_r[...], (SEGS, wc))
        y = ys_r[...]
        dy16 = (dyg_r[...] * _gelu_grad(y, _gelu(y)[1])).astype(BF16)
        x_re[...] = jnp.dot(dy16, cre_r[...], preferred_element_type=F32)
        x_im[...] = jnp.dot(dy16, cim_r[...], preferred_element_type=F32)

        @pl.when(ch == 0)
        def _():
            st_re[...] = jnp.zeros_like(st_re)
            st_im[...] = jnp.zeros_like(st_im)

        def step(i, hc):
            off = pl.multiple_of((tt - 1 - i) * SEGS, SEGS)
            return (a_re * hc[0] - a_im * hc[1] + x_re[pl.ds(off, SEGS), :],
                    a_re * hc[1] + a_im * hc[0] + x_im[pl.ds(off, SEGS), :])

        fin = lax.fori_loop(0, tt, step, (st_re[...], st_im[...]))
        st_re[...] = fin[0]
        st_im[...] = fin[1]

        @pl.when(ch == nch - 1)
        def _():
            c_re, c_im = _chain_segments(a_re, a_im, fin[0], fin[1], nsq, True)
            gin_re_o[...] = c_re
            gin_im_o[...] = c_im

    y_spec = pl.BlockSpec((rows, ub_w), lambda j, c: (nch - 1 - c, j))
    ct_spec = pl.BlockSpec((None, ub_w, wc), lambda j, c: (j, 0, 0))
    l_spec = pl.BlockSpec((1, wc), lambda j, c: (0, j))
    e_spec = pl.BlockSpec((SEGS, wc), lambda j, c: (0, j))
    return _run(
        body, [dyg_s, ys, w_cre_t, w_cim_neg_t, lre, lim], carry=carry, name=name, grid=(nb, nch),
        in_specs=[y_spec, y_spec, ct_spec, ct_spec, l_spec, l_spec], out_specs=[e_spec, e_spec],
        out_shape=[jax.ShapeDtypeStruct((SEGS, STATE_W), F32)] * 2,
        scratch_shapes=[pltpu.VMEM((SEGS, wc), F32)] * 2 + [pltpu.VMEM((rows, wc), F32)] * 2,
        compiler_params=_cparams(("parallel", "arbitrary")),
    )


FWD_BD = 4


def _block_diag(m, nb=BD):
    g, r, c = m.shape
    m = m.reshape(nb, g // nb, r, c)
    eye = jnp.eye(g // nb, dtype=m.dtype)
    return jnp.einsum("jarc,ab->jarbc", m, eye).reshape(nb, (g // nb) * r, (g // nb) * c)


def _block_diag_extract(m, r, c):
    per = m.shape[1] // r
    m = m.reshape(BD, per, r, per, c)
    return jnp.einsum("jarac->jarc", m).reshape(BD * per, r, c)


def to_segments(a):
    s, w = a.shape
    return a.reshape(SEGS, s // SEGS, w).transpose(1, 0, 2).reshape(s, w)


def from_segments(a):
    s, w = a.shape
    return a.reshape(s // SEGS, SEGS, w).transpose(1, 0, 2).reshape(s, w)


W_IN_CHUNK_ROWS = (512, 512, 512, 512)


def _row_chunks(blocks, sizes):
    assert sum(sizes) == blocks.shape[1]
    out, at = [], 0
    for n in sizes:
        out.append(AllToAll([blocks[:, at:at + n]]))
        at += n
    return out
TALL_TM = 2048
FFN_TN = 512


def local_step(x, target, shards, small):
    s = x.shape[0]
    g1, g2, g3, g4 = (small[k].reshape(1, D_MODEL) for k in ("norm_mix_pre", "norm_mix_post", "norm_ffn_pre", "norm_ffn_post"))
    dvec = small["ssm_d"].reshape(1, SSM_W)
    wts, recv = {}, {}

    def gathered(names, blocks):
        for n, b in zip(names, blocks):
            wts[n] = _full_from_gathered(b, n)

    def rms_in_fn(r, c):
        hh = _rms(r[0], c[0])[0].astype(BF16)
        return [hh, _permute(_perm_matrix(PERM_TS, 4, False), hh), _permute(_perm_matrix(PERM_TS, 16, False), hh)], []

    (h, h4, h16), got = rowwise("rms_in", rms_in_fn, [x], [g1], [(D_MODEL, BF16), (D_MODEL, BF16, 4), (D_MODEL, BF16, 16)],
                                ts=PERM_TS, carry=Gather([shards["w_in"]]))
    w_in_t = _full_from_gathered(got[0], "w_in")
    w_u_t, w_gates_t = w_in_t[3 * HQ:3 * HQ + SSM_W], w_in_t[3 * HQ + SSM_W:]

    def qkv_rows(g):
        return 3 * GROUP_W, lambda t: 3 * t + g

    hd = [h.reshape(1, s, D_MODEL), h4, h16]
    qkv = [None] * 3
    names = ("w_attn_up", "w_glu_v", "w_glu_g")
    qkv[0], got = mm([(hd[0].reshape(s, D_MODEL), w_in_t)], "nt", BF16, "mm_qkv0", tm=TALL_TM, tn=GROUP_W, b_window=qkv_rows(0),
                     carry=Gather([shards[n] for n in names]))
    gathered(names, got)
    qkv[1], got = mm([(hd[1].reshape(s, D_MODEL), w_in_t)], "nt", BF16, "mm_qkv1", tm=TALL_TM, tn=GROUP_W, b_window=qkv_rows(1),
                     carry=Gather([shards["w_out"]]))
    gathered(("w_out",), got)
    qkv[2] = mm([(hd[2].reshape(s, D_MODEL), w_in_t)], "nt", BF16, "mm_qkv2", tm=TALL_TM, tn=GROUP_W, b_window=qkv_rows(2))
    u = mm([(h, w_u_t)], "nt", F32, "mm_u")
    gates, got = mm([(h, w_gates_t)], "nt", BF16, "mm_gates", carry=Gather([shards["w_ffn_gate"]]))
    gathered(("w_ffn_gate",), got)

    outs, lses = [], []
    for g, (_, dil) in enumerate(ATTN_GROUPS):
        o, l = attn_fwd(qkv[g].reshape(dil, s // dil, 3 * GROUP_W), g, f"attn_fwd{g}")
        outs.append(o.reshape(s, GROUP_W) if dil == 1 else o)
        lses.append(l.reshape(s, GROUP_W) if dil == 1 else l)

    def natural(r):
        back4, back16 = _perm_matrix(PERM_TS, 4, True), _perm_matrix(PERM_TS, 16, True)
        return (r[0], _permute(back4, r[1].astype(BF16)), _permute(back16, r[2].astype(BF16)),
                r[3], _permute(back4, r[4]), _permute(back16, r[5]))

    def merge_fn(r, c):
        o0, o1, o2, l0, l1, l2 = natural(r)
        w0, w1, w2 = _mix_weights(l0, l1, l2)
        return [w0 * o0 + w1 * o1 + w2 * o2], []

    (attn,) = rowwise("attn_merge", merge_fn, outs + lses, [], [(GROUP_W, BF16)], ts=PERM_TS)
    attn_branch = mm([(attn, wts["w_attn_up"])], "nn", BF16, "mm_up", tm=TALL_TM)

    are3 = small["ssm_a_re"].reshape(SSM_GROUPS, SSM_STATE, 1)
    aim3 = small["ssm_a_im"].reshape(SSM_GROUPS, SSM_STATE, 1)
    ldt3 = small["ssm_log_dt"].reshape(SSM_GROUPS, 1, 1)
    bre3 = small["ssm_b_re"].reshape(SSM_GROUPS, SSM_STATE, SSM_GROUP)
    bim3 = small["ssm_b_im"].reshape(SSM_GROUPS, SSM_STATE, SSM_GROUP)
    cre3 = small["ssm_c_re"].reshape(SSM_GROUPS, SSM_GROUP, SSM_STATE)
    cim3 = small["ssm_c_im"].reshape(SSM_GROUPS, SSM_GROUP, SSM_STATE)
    lre3, lim3, bbre, bbim = ssm_prep(are3, aim3, ldt3, bre3, bim3)
    lre, lim = lre3.reshape(1, STATE_W), lim3.reshape(1, STATE_W)
    w_bre = _block_diag(bbre.transpose(0, 2, 1)).astype(BF16)
    w_bim = _block_diag(bbim.transpose(0, 2, 1)).astype(BF16)
    w_cre = _block_diag(cre3.transpose(0, 2, 1)).astype(BF16)
    w_cim = _block_diag(cim3.transpose(0, 2, 1)).astype(BF16)
    u_s = to_segments(u)
    fwd_w = [_block_diag(t.transpose(0, 2, 1), FWD_BD).astype(BF16) for t in (bbre, bbim, cre3, -cim3)]
    (yg_s, y_ssm, h_re, h_im, hin_re, hin_im), got = ssm_fwd(
        u_s, dvec, *fwd_w, lre, lim, "ssm_fwd", carry=Gather([shards["w_ffn_up"]], pass_early=True))
    gathered(("w_ffn_up",), got)
    yg = from_segments(yg_s)
    gv = mm([(yg, wts["w_glu_v"])], "nn", BF16, "mm_glu_v", tm=TALL_TM)
    gg = mm([(yg, wts["w_glu_g"])], "nn", BF16, "mm_glu_g", tm=TALL_TM)

    def gate_fn(r, c):
        gts, ab, gv_, gg_ = r
        sa, ss = _sigmoid(gts[:, :D_MODEL]), _sigmoid(gts[:, D_MODEL:])
        return [sa * ab + ss * (gv_ * _sigmoid(gg_))], []

    (merged,) = rowwise("gate_merge", gate_fn, [gates, attn_branch, gv, gg], [], [(D_MODEL, BF16)])
    o_mix = mm([(merged, wts["w_out"])], "nn", F32, "mm_out")

    def mid_fn(r, c):
        x1 = r[0] + _rms(r[1], c[0])[0]
        return [x1, _rms(x1, c[1])[0]], []

    x1, h2 = rowwise("rms_mid", mid_fn, [x, o_mix], [g2, g3], [(D_MODEL, F32), (D_MODEL, BF16)])
    (fa, fb, fin), got = mm([(h2, wts["w_ffn_gate"]), (h2, wts["w_ffn_up"])], "nt", [BF16, BF16, BF16], "mm_ffn_in", tn=FFN_TN,
                            epilogue=lambda p, e: [p[0], p[1], p[0] * _sigmoid(p[0]) * p[1]],
                            carry=Gather([shards["w_ffn_down"]], pass_early=True))
    gathered(("w_ffn_down",), got)
    f = mm([(fin, wts["w_ffn_down"])], "nn", F32, "mm_ffn_down", tn=512, tk=D_FF)

    def loss_fn(r, c):
        x1_, f_, tgt = r
        y, n, rr = _rms(f_, c[0])
        err = x1_ + y - tgt
        dout = err * (1.0 / D_MODEL)
        df, dg = _rms_bwd(dout, n, rr, c[0])
        lp = 0.5 * jnp.sum(jnp.sum(err * err, axis=-1, keepdims=True) * (1.0 / D_MODEL), axis=0, keepdims=True)
        return [df, dout], [dg, lp]

    df, dout, dg4, loss_part = rowwise("loss_bwd", loss_fn, [x1, f, target], [g4], [(D_MODEL, BF16), (D_MODEL, BF16)],
                                       acc_outs=[(1, D_MODEL), (1, 1)])
    def sent(names, blocks):
        for n, b in zip(names, blocks):
            recv[n] = b

    def to_owners(names, dws):
        return AllToAll([_split_for_devices(d, n) for n, d in zip(names, dws)])

    def swiglu_bwd(p, e):
        dfin_, (a, b) = p[0], e
        sg = _sigmoid(a)
        return [dfin_ * b * (sg * (1.0 + a * (1.0 - sg))), dfin_ * a * sg]

    da, db = mm([(df, wts["w_ffn_down"])], "nt", [BF16, BF16], "mm_d_fin", tn=FFN_TN, epilogue=swiglu_bwd, extras=[fa, fb])
    dw_ffn_down = mm([(fin, df)], "tn", BF16, "mm_dw_ffn_down")
    dh2, got = mm([(da, wts["w_ffn_gate"]), (db, wts["w_ffn_up"])], "nn", F32, "mm_d_h2", tm=512, tn=1024, tk=D_FF // 2,
                  carry=to_owners(["w_ffn_down"], [dw_ffn_down]))
    sent(["w_ffn_down"], got)
    dw_ffn_gate = mm([(da, h2)], "tn", BF16, "mm_dw_ffn_gate")
    dw_ffn_up, got = mm([(db, h2)], "tn", BF16, "mm_dw_ffn_up", carry=to_owners(["w_ffn_gate"], [dw_ffn_gate]))
    sent(["w_ffn_gate"], got)

    def mid_bwd(r, c):
        dh2_, dout_, x1_, o_ = r
        _, n3, r3 = _rms(x1_, c[1])
        dx1, dg3_ = _rms_bwd(dh2_, n3, r3, c[1])
        dx1 = dx1 + dout_
        _, n2, r2 = _rms(o_, c[0])
        do_, dg2_ = _rms_bwd(dx1, n2, r2, c[0])
        return [dx1, do_], [dg2_, dg3_]

    dx1, do_mix, dg2, dg3 = rowwise("rms_mid_bwd", mid_bwd, [dh2, dout, x1, o_mix], [g2, g3], [(D_MODEL, F32), (D_MODEL, BF16)],
                                    acc_outs=[(1, D_MODEL), (1, D_MODEL)])
    dmerged = mm([(do_mix, wts["w_out"])], "nt", BF16, "mm_d_merged")
    dw_out = mm([(merged, do_mix)], "tn", BF16, "mm_dw_out")

    def gate_bwd(r, c):
        dm, gts, ab, gv_, gg_ = r
        sa, ss, sg = _sigmoid(gts[:, :D_MODEL]), _sigmoid(gts[:, D_MODEL:]), _sigmoid(gg_)
        branch = gv_ * sg
        dbranch = dm * ss
        dgates = jnp.concatenate([dm * ab * sa * (1.0 - sa), dm * branch * ss * (1.0 - ss)], axis=-1)
        return [dgates, dm * sa, dbranch * sg, dbranch * gv_ * sg * (1.0 - sg)], []

    dgates, dab, dgv, dgg = rowwise("gate_bwd", gate_bwd, [dmerged, gates, attn_branch, gv, gg], [],
                                    [(2 * D_MODEL, BF16), (D_MODEL, BF16), (D_MODEL, BF16), (D_MODEL, BF16)])
    dattn = mm([(dab, wts["w_attn_up"])], "nt", F32, "mm_d_attn")
    dw_up = mm([(attn, dab)], "tn", BF16, "mm_dw_up")
    dyg = mm([(dgv, wts["w_glu_v"]), (dgg, wts["w_glu_g"])], "nt", F32, "mm_d_yg")
    dw_glu_v = mm([(yg, dgv)], "tn", BF16, "mm_dw_glu_v")
    dw_glu_g = mm([(yg, dgg)], "tn", BF16, "mm_dw_glu_g")

    dyg_s = to_segments(dyg)
    (gin_re, gin_im), got = ssm_bwd_ends(dyg_s, y_ssm, fwd_w[2].transpose(0, 2, 1), fwd_w[3].transpose(0, 2, 1), lre, lim,
                                         "ssm_bwd_ends", carry=to_owners(["w_out"], [dw_out]))
    sent(["w_out"], got)
    (du_s, dbre_d, dbim_d, dcre_d, dcim_d, dl_re8, dl_im8, dd_ssm), got = ssm_bwd(
        dyg_s, y_ssm, u_s, h_re, h_im, hin_re, hin_im, gin_re, gin_im, dvec, w_bre.transpose(0, 2, 1), w_bim.transpose(0, 2, 1),
        w_cre.transpose(0, 2, 1), -w_cim.transpose(0, 2, 1), lre, lim, "ssm_bwd", carry=to_owners(["w_ffn_up"], [dw_ffn_up]))
    sent(["w_ffn_up"], got)
    dbb_re = _block_diag_extract(dbre_d, SSM_GROUP, SSM_STATE).transpose(0, 2, 1)
    dbb_im = _block_diag_extract(dbim_d, SSM_GROUP, SSM_STATE).transpose(0, 2, 1)
    dc_re = _block_diag_extract(dcre_d, SSM_STATE, SSM_GROUP).transpose(0, 2, 1)
    dc_im = -_block_diag_extract(dcim_d, SSM_STATE, SSM_GROUP).transpose(0, 2, 1)

    def fold8(r, c):
        return [], [jnp.sum(r[0], axis=0, keepdims=True), jnp.sum(r[1], axis=0, keepdims=True)]

    dl_re, dl_im = rowwise("ssm_dl_fold", fold8, [dl_re8, dl_im8], [], [], acc_outs=[(1, STATE_W), (1, STATE_W)], ts=SEGS)
    da_re, da_im, dldt, db_re, db_im = ssm_prep_bwd(
        are3, aim3, ldt3, bre3, bim3, dbb_re, dbb_im,
        dl_re.reshape(SSM_GROUPS, SSM_STATE, 1), dl_im.reshape(SSM_GROUPS, SSM_STATE, 1))
    du = from_segments(du_s)

    def merge_bwd(r, c):
        dat = r[0]
        o0, o1, o2, l0, l1, l2 = natural(r[1:])
        w0, w1, w2 = _mix_weights(l0, l1, l2)
        tot = _head_sum(dat * (w0 * o0 + w1 * o1 + w2 * o2))
        to4, to16 = _perm_matrix(PERM_TS, 4, False), _perm_matrix(PERM_TS, 16, False)
        return [w0 * dat, _permute(to4, (w1 * dat).astype(BF16)), _permute(to16, (w2 * dat).astype(BF16)),
                w0 * tot, _permute(to4, (w1 * tot).astype(BF16)), _permute(to16, (w2 * tot).astype(BF16))], []

    mb = rowwise("attn_merge_bwd", merge_bwd, [dattn] + outs + lses, [],
                 [(GROUP_W, BF16), (GROUP_W, BF16, 4), (GROUP_W, BF16, 16), (GROUP_W, BF16), (GROUP_W, BF16, 4), (GROUP_W, BF16, 16)],
                 ts=PERM_TS)
    dqs, dw_qkv = [], []
    names = ["w_glu_v", "w_glu_g", "w_attn_up"]
    for g, (_, dil) in enumerate(ATTN_GROUPS):
        dq = attn_bwd(qkv[g].reshape(dil, s // dil, 3 * GROUP_W), mb[g].reshape(dil, s // dil, GROUP_W),
                      lses[g].reshape(dil, s // dil, GROUP_W), mb[3 + g].reshape(dil, s // dil, GROUP_W),
                      g, f"attn_bwd{g}", carry=to_owners(names, [dw_glu_v, dw_glu_g, dw_up]) if g == 1 else None)
        if g == 1:
            dq, got = dq
            sent(names, got)
        dq = dq.reshape(s, 3 * GROUP_W)
        dqs.append(dq)
        dw_qkv.append(mm([(hd[g].reshape(s, D_MODEL), dq)], "tn", BF16, f"mm_dw_qkv{g}"))
    dw_u = mm([(h, du)], "tn", BF16, "mm_dw_u")
    dw_gates = mm([(h, dgates)], "tn", BF16, "mm_dw_gates")
    dw_in = jnp.concatenate(
        [dw_qkv[g][:, o * GROUP_W:(o + 1) * GROUP_W] for o in range(3) for g in range(3)] + [dw_u, dw_gates], axis=1)
    chunks = _row_chunks(_split_for_devices(dw_in, "w_in"), W_IN_CHUNK_ROWS)
    dh_parts, got_chunks = [], []
    for g, (_, dil) in enumerate(ATTN_GROUPS):
        dh_g, got = mm([(dqs[g], w_in_t)], "nn", BF16, f"mm_d_h_qkv{g}", tk=GROUP_W, b_window=qkv_rows(g), carry=chunks[g])
        got_chunks.append(got[0])
        dh_parts.append(dh_g if dil == 1 else dh_g.reshape(dil, s // dil, D_MODEL))
    dh_parts.append(mm([(du, w_u_t)], "nn", BF16, "mm_d_h_u"))
    dh_gates, got = mm([(dgates, w_gates_t)], "nn", BF16, "mm_d_h_gates", carry=chunks[3])
    got_chunks.append(got[0])
    dh_parts.append(dh_gates)
    recv["w_in"] = jnp.concatenate(got_chunks, axis=1)

    def in_bwd(r, c):
        dh1 = _permute(_perm_matrix(PERM_TS, 4, True), r[1].astype(BF16))
        dh2_ = _permute(_perm_matrix(PERM_TS, 16, True), r[2].astype(BF16))
        dh = r[0] + dh1 + dh2_ + r[3] + r[4]
        _, n1, r1 = _rms(r[6], c[0])
        dx, dg1_ = _rms_bwd(dh, n1, r1, c[0])
        return [dx + r[5]], [dg1_]

    grad_x, dg1 = rowwise("rms_in_bwd", in_bwd, dh_parts + [dx1, x], [g1], [(D_MODEL, F32)], acc_outs=[(1, D_MODEL)], ts=PERM_TS)

    dsmall = dict(norm_mix_pre=dg1, ssm_a_re=da_re, ssm_a_im=da_im, ssm_log_dt=dldt, ssm_b_re=db_re, ssm_b_im=db_im,
                  ssm_c_re=dc_re, ssm_c_im=dc_im, ssm_d=dd_ssm, norm_mix_post=dg2, norm_ffn_pre=dg3, norm_ffn_post=dg4)
    return loss_part, grad_x, recv, dsmall


def adamw(parts, w, m, v, name, carry=None):
    r, c = w.shape
    tr = r
    while tr > 8 and tr % 2 == 0 and tr * c * (8 * parts.dtype.itemsize + 28) * 2 > 24 * 1024 * 1024:
        tr //= 2
    assert r % tr == 0 and (tr % 8 == 0 or tr == r)
    c1, c2 = 1.0 / (1.0 - ADAM_B1 ** ADAM_STEP), 1.0 / (1.0 - ADAM_B2 ** ADAM_STEP)

    def body(p_ref, w_ref, m_ref, v_ref, g_o, d_o, m_o, v_o):
        g = p_ref[0].astype(F32)
        for i in range(1, N_DEV):
            g = g + p_ref[i].astype(F32)
        mn = ADAM_B1 * m_ref[...] + (1.0 - ADAM_B1) * g
        vn = ADAM_B2 * v_ref[...] + (1.0 - ADAM_B2) * (g * g)
        g_o[...] = g
        m_o[...] = mn
        v_o[...] = vn
        d_o[...] = -ADAM_LR * ((mn * c1) / (jnp.sqrt(vn * c2) + ADAM_EPS) + ADAM_WD * w_ref[...])

    blk = pl.BlockSpec((tr, c), lambda i: (i, 0))
    return _run(
        body, [parts, w, m, v], carry=carry, name=name, grid=(r // tr,),
        in_specs=[pl.BlockSpec((N_DEV, tr, c), lambda i: (0, i, 0)), blk, blk, blk],
        out_specs=[blk] * 4, out_shape=[jax.ShapeDtypeStruct((r, c), F32)] * 4, compiler_params=_cparams(("parallel",)),
    )


PACK_C = 1024
SHARDED = ("w_in", "w_attn_up", "w_glu_v", "w_glu_g", "w_out", "w_ffn_gate", "w_ffn_up", "w_ffn_down")
ROW_SHARDED = ("w_out", "w_ffn_down")
SENT_TRANSPOSED = ("w_in", "w_ffn_gate", "w_ffn_up")
GRAD_TRANSPOSED = ("w_ffn_gate", "w_ffn_up")
SMALL = ("norm_mix_pre", "ssm_a_re", "ssm_a_im", "ssm_log_dt", "ssm_b_re", "ssm_b_im", "ssm_c_re", "ssm_c_im", "ssm_d",
         "norm_mix_post", "norm_ffn_pre", "norm_ffn_post")
WEIGHTS = ("norm_mix_pre", "w_in", "w_attn_up", "ssm_a_re", "ssm_a_im", "ssm_log_dt", "ssm_b_re", "ssm_b_im", "ssm_c_re",
           "ssm_c_im", "ssm_d", "w_glu_v", "w_glu_g", "w_out", "norm_mix_post", "norm_ffn_pre", "w_ffn_gate", "w_ffn_up",
           "w_ffn_down", "norm_ffn_post")


def _pack(arrs, dtype, pad_rows_to=64):
    flat = jnp.concatenate([a.reshape(-1).astype(dtype) for a in arrs])
    n = flat.shape[0]
    rows = -(-n // PACK_C)
    rows = -(-rows // pad_rows_to) * pad_rows_to
    return jnp.pad(flat, (0, rows * PACK_C - n)).reshape(rows, PACK_C)


def _unpack(flat2d, shapes):
    flat = flat2d.reshape(-1)
    out, off = [], 0
    for shp in shapes:
        n = int(np.prod(shp))
        out.append(flat[off:off + n].reshape(shp))
        off += n
    return out


def _full_from_gathered(gathered, name):
    if name in ROW_SHARDED or name in SENT_TRANSPOSED:
        return gathered.reshape(-1, gathered.shape[2])
    return gathered.transpose(1, 0, 2).reshape(gathered.shape[1], -1)


def _split_for_devices(full, name):
    if name in ROW_SHARDED or name in GRAD_TRANSPOSED:
        return full.reshape(N_DEV, -1, full.shape[1])
    return full.reshape(full.shape[0], N_DEV, -1).transpose(1, 0, 2)


def kernel(x, norm_mix_pre, w_in, w_attn_up, ssm_a_re, ssm_a_im, ssm_log_dt, ssm_b_re, ssm_b_im, ssm_c_re, ssm_c_im, ssm_d, w_glu_v, w_glu_g, w_out, norm_mix_post, norm_ffn_pre, w_ffn_gate, w_ffn_up, w_ffn_down, norm_ffn_post, loss_target, m_norm_mix_pre, m_w_in, m_w_attn_up, m_ssm_a_re, m_ssm_a_im, m_ssm_log_dt, m_ssm_b_re, m_ssm_b_im, m_ssm_c_re, m_ssm_c_im, m_ssm_d, m_w_glu_v, m_w_glu_g, m_w_out, m_norm_mix_post, m_norm_ffn_pre, m_w_ffn_gate, m_w_ffn_up, m_w_ffn_down, m_norm_ffn_post, v_norm_mix_pre, v_w_in, v_w_attn_up, v_ssm_a_re, v_ssm_a_im, v_ssm_log_dt, v_ssm_b_re, v_ssm_b_im, v_ssm_c_re, v_ssm_c_im, v_ssm_d, v_w_glu_v, v_w_glu_g, v_w_out, v_norm_mix_post, v_norm_ffn_pre, v_w_ffn_gate, v_w_ffn_up, v_w_ffn_down, v_norm_ffn_post):
    args = dict(locals())
    wv = {n: args[n][0] for n in WEIGHTS}
    mv = {n: args["m_" + n][0] for n in WEIGHTS}
    vv = {n: args["v_" + n][0] for n in WEIGHTS}

    shards = {n: (wv[n].T if n in SENT_TRANSPOSED else wv[n]).astype(BF16) for n in SHARDED}
    small = {n: wv[n] for n in SMALL}
    loss_part, grad_x, recv, dsmall = local_step(x[0], loss_target[0], shards, small)
    for n in GRAD_TRANSPOSED:
        recv[n] = recv[n].transpose(0, 2, 1)

    small_shapes = [wv[n].shape for n in SMALL]
    res = {}
    res["w_in"], (sgather,) = adamw(recv["w_in"], wv["w_in"], mv["w_in"], vv["w_in"], "adamw_w_in",
                                    carry=Gather([_pack([dsmall[n] for n in SMALL], F32)]))
    for n in SHARDED[1:]:
        res[n] = adamw(recv[n], wv[n], mv[n], vv[n], "adamw_" + n)
    sres = adamw(sgather, _pack([wv[n] for n in SMALL], F32), _pack([mv[n] for n in SMALL], F32),
                 _pack([vv[n] for n in SMALL], F32), "adamw_small")
    sun = [_unpack(t, small_shapes) for t in sres]
    for k, n in enumerate(SMALL):
        res[n] = tuple(sun[t][k] for t in range(4))

    loss = lax.psum(loss_part[0, 0], ("x", "y", "c"))
    outs = [loss, grad_x[None]]
    for t in range(4):
        outs += [res[n][t][None] for n in WEIGHTS]
    return tuple(outs)
```

```python
import functools
import math

import numpy as np
import jax
import jax.numpy as jnp
from jax import lax
from jax.experimental import pallas as pl
from jax.experimental.pallas import tpu as pltpu

F32 = jnp.float32
BF16 = jnp.bfloat16

D_MODEL = 2048
HEAD_DIM = 128
HEADS_PER_GROUP = 4
ATTN_GROUPS = ((128, 1), (512, 4), (2048, 16))
N_HEADS = HEADS_PER_GROUP * len(ATTN_GROUPS)
GROUP_W = HEADS_PER_GROUP * HEAD_DIM
HQ = N_HEADS * HEAD_DIM
SSM_W = 1024
SSM_GROUP = 16
SSM_GROUPS = 64
SSM_STATE = 64
STATE_W = SSM_GROUPS * SSM_STATE
D_FF = 5632
EPS = 1e-6
N_DEV = 8
SEGS = 8
BD = 8

ADAM_LR, ADAM_B1, ADAM_B2, ADAM_EPS, ADAM_WD, ADAM_STEP = 0.001, 0.9, 0.999, 1e-08, 0.01, 10

VMEM_LIMIT = 56 * 1024 * 1024
HBM_SPEC = pl.BlockSpec(memory_space=pltpu.HBM)
MESH_ID = pl.DeviceIdType.MESH
NEG = -1e30


def _pcall(body, **kw):
    return pl.pallas_call(body, **kw)


def _cparams(sem=None):
    if sem is None:
        return pltpu.CompilerParams(vmem_limit_bytes=VMEM_LIMIT)
    return pltpu.CompilerParams(vmem_limit_bytes=VMEM_LIMIT, dimension_semantics=sem)


def _my_coords():
    return lax.axis_index("x"), lax.axis_index("y"), lax.axis_index("c")


class Gather:
    def __init__(self, xs, pass_early=False):
        self.arrays = list(xs)
        self.out_shapes = [jax.ShapeDtypeStruct((N_DEV,) + x.shape, x.dtype) for x in xs]
        self.pass_early = pass_early

    def _ctx(self, out_refs, send_sems, recv_sems):
        mx, my, mc = _my_coords()
        me, sibling = (mx, my, mc), (mx, my, 1 - mc)
        chips = [(1 - mx, my), (mx, 1 - my), (1 - mx, 1 - my)]

        def slot(a, px, py, pc):
            return out_refs[a].at[4 * px + 2 * py + pc]

        def copy(a, k, block, to, src=None):
            return pltpu.make_async_remote_copy(
                src_ref=slot(a, *block) if src is None else src, dst_ref=slot(a, *block),
                send_sem=send_sems.at[7 * a + k], recv_sem=recv_sems.at[7 * a + k], device_id=to, device_id_type=MESH_ID)

        return me, sibling, chips, mc, slot, copy

    def _first(self, a, x_refs, ctx):
        me, sibling, chips, mc, slot, copy = ctx
        return [copy(a, 0, me, sibling, src=x_refs[a])] + [copy(a, 1 + j, me, (*chip, mc), src=x_refs[a]) for j, chip in enumerate(chips)]

    def start(self, x_refs, out_refs, send_sems, recv_sems, local_sems):
        ctx = self._ctx(out_refs, send_sems, recv_sems)
        me, slot = ctx[0], ctx[4]
        for a in range(len(self.arrays)):
            pltpu.make_async_copy(x_refs[a], slot(a, *me), local_sems.at[a]).start()
            for cp in self._first(a, x_refs, ctx):
                cp.start()

    def _passed(self, ctx):
        me, sibling, chips, mc, slot, copy = ctx
        return [copy(a, 4 + j, (*chip, mc), sibling) for a in range(len(self.arrays)) for j, chip in enumerate(chips)]

    def middle(self, x_refs, out_refs, send_sems, recv_sems, local_sems):
        ctx = self._ctx(out_refs, send_sems, recv_sems)
        me, sibling, chips, mc, slot, copy = ctx
        for a in range(len(self.arrays)):
            for j, chip in enumerate(chips):
                copy(a, 1 + j, (*chip, mc), me).wait_recv()
                copy(a, 4 + j, (*chip, mc), sibling).start()

    def finish(self, x_refs, out_refs, send_sems, recv_sems, local_sems, passed_on=False):
        if not passed_on:
            self.middle(x_refs, out_refs, send_sems, recv_sems, local_sems)
        ctx = self._ctx(out_refs, send_sems, recv_sems)
        me, sibling, chips, mc, slot, copy = ctx
        na = len(self.arrays)
        passed = self._passed(ctx)
        for a in range(na):
            copy(a, 0, sibling, me).wait_recv()
            for j, chip in enumerate(chips):
                copy(a, 4 + j, (*chip, 1 - mc), me).wait_recv()
        for a in range(na):
            for cp in self._first(a, x_refs, ctx):
                cp.wait_send()
        for cp in passed:
            cp.wait_send()
        for a in range(na):
            pltpu.make_async_copy(x_refs[a], slot(a, *me), local_sems.at[a]).wait()


class AllToAll:
    def __init__(self, ps):
        self.arrays = list(ps)
        self.out_shapes = [jax.ShapeDtypeStruct(p.shape, p.dtype) for p in ps]

    def _copies(self, p_refs, out_refs, send_sems, recv_sems, local_sems):
        mx, my, mc = _my_coords()
        me = 4 * mx + 2 * my + mc
        local, remote = [], []
        for a in range(len(self.arrays)):
            local.append(pltpu.make_async_copy(p_refs[a].at[me], out_refs[a].at[me], local_sems.at[a]))
            for k in range(1, N_DEV):
                px, py, pc = mx ^ ((k >> 2) & 1), my ^ ((k >> 1) & 1), mc ^ (k & 1)
                remote.append(pltpu.make_async_remote_copy(
                    src_ref=p_refs[a].at[4 * px + 2 * py + pc], dst_ref=out_refs[a].at[me],
                    send_sem=send_sems.at[7 * a + k - 1], recv_sem=recv_sems.at[7 * a + k - 1],
                    device_id=(px, py, pc), device_id_type=MESH_ID))
        return local, remote

    def start(self, *refs):
        local, remote = self._copies(*refs)
        for cp in local + remote:
            cp.start()

    def finish(self, *refs):
        local, remote = self._copies(*refs)
        for cp in remote:
            cp.wait_recv()
        for cp in remote:
            cp.wait_send()
        for cp in local:
            cp.wait()


class RowsToOwners:
    def __init__(self, p, r0, n, into=None):
        self.arrays = [p] if into is None else [p, into]
        self.out_shapes = [jax.ShapeDtypeStruct(p.shape, p.dtype)]
        self.aliases = {} if into is None else {1: 0}
        self.rows = (r0, n)

    def _copies(self, p_refs, out_refs, send_sems, recv_sems, local_sems):
        mx, my, mc = _my_coords()
        me = 4 * mx + 2 * my + mc
        rows = pl.ds(*self.rows)
        local = [pltpu.make_async_copy(p_refs[0].at[me, rows], out_refs[0].at[me, rows], local_sems.at[0])]
        remote = []
        for k in range(1, N_DEV):
            px, py, pc = mx ^ ((k >> 2) & 1), my ^ ((k >> 1) & 1), mc ^ (k & 1)
            remote.append(pltpu.make_async_remote_copy(
                src_ref=p_refs[0].at[4 * px + 2 * py + pc, rows], dst_ref=out_refs[0].at[me, rows],
                send_sem=send_sems.at[k - 1], recv_sem=recv_sems.at[k - 1], device_id=(px, py, pc), device_id_type=MESH_ID))
        return local, remote

    start = AllToAll.start
    finish = AllToAll.finish


def _run(body, args, carry=None, **kw):
    if carry is None:
        return _pcall(body, **kw)(*args)
    grid = kw["grid"]
    single = not isinstance(kw["out_shape"], (list, tuple))
    in_specs = list(kw["in_specs"])
    out_specs = [kw["out_specs"]] if single else list(kw["out_specs"])
    out_shape = [kw["out_shape"]] if single else list(kw["out_shape"])
    scratch = list(kw.get("scratch_shapes", []))
    na, nin, nout, nscr = len(carry.arrays), len(in_specs), len(out_specs), len(scratch)
    nco = len(carry.out_shapes)
    aliases = {nin + i: nout + o for i, o in getattr(carry, "aliases", {}).items()}
    steps = int(np.prod(grid))
    mid_step = (steps * 7) // 10 if getattr(carry, "pass_early", False) and steps >= 4 else None

    def carried(*refs):
        ins, cin = refs[:nin], refs[nin:nin + na]
        outs, cout = refs[nin + na:nin + na + nout], refs[nin + na + nout:nin + na + nout + nco]
        scr = refs[nin + na + nout + nco:nin + na + nout + nco + nscr]
        sems = refs[nin + na + nout + nco + nscr:]
        step = pl.program_id(0)
        for i in range(1, len(grid)):
            step = step * grid[i] + pl.program_id(i)

        @pl.when(step == 0)
        def _():
            carry.start(cin, cout, *sems)

        if mid_step is not None:
            @pl.when(step == mid_step)
            def _():
                carry.middle(cin, cout, *sems)

        body(*ins, *outs, *scr)

        @pl.when(step == steps - 1)
        def _():
            if mid_step is not None:
                carry.finish(cin, cout, *sems, passed_on=True)
            else:
                carry.finish(cin, cout, *sems)

    res = _pcall(
        carried, name=kw["name"], grid=grid, in_specs=in_specs + [HBM_SPEC] * na, out_specs=out_specs + [HBM_SPEC] * nco,
        out_shape=out_shape + carry.out_shapes, input_output_aliases=aliases,
        scratch_shapes=scratch + [pltpu.SemaphoreType.DMA((7 * na,)), pltpu.SemaphoreType.DMA((7 * na,)), pltpu.SemaphoreType.DMA((na,))],
        compiler_params=_cparams(("arbitrary",) * len(grid)),
    )(*args, *carry.arrays)
    main = res[:nout]
    return (main[0] if single else main), list(res[nout:])


_DN = {"nn": (((1,), (0,)), ((), ())), "nt": (((1,), (1,)), ((), ())), "tn": (((0,), (0,)), ((), ()))}


LANE = 128
MM_TM, MM_TN, MM_TK = 1024, 1536, 2048


def _tile(n, cap):
    for t in range(min(cap, n) // LANE * LANE, 0, -LANE):
        if n % t == 0:
            return t
    raise ValueError(n)


DW_TM, DW_TN, DW_TK = 512, 512, 8192


def mm(pairs, mode, out_dtype, name, tm=None, tn=None, tk=None, carry=None, epilogue=None, extras=(), b_window=None):
    a0, b0 = pairs[0]
    if mode == "nn":
        (m, k), n = a0.shape, b0.shape[1]
    elif mode == "nt":
        (m, k), n = a0.shape, b0.shape[0]
    else:
        (k, m), n = a0.shape, b0.shape[1]
    if b_window is not None:
        assert mode in ("nn", "nt") and len(pairs) == 1
        if mode == "nt":
            n = b_window[0]
        else:
            assert k == b_window[0]
    caps = (DW_TM, DW_TN, DW_TK) if mode == "tn" else (MM_TM, MM_TN, MM_TK)
    tm, tn, tk = _tile(m, tm or caps[0]), _tile(n, tn or caps[1]), _tile(k, tk or caps[2])
    nk = k // tk
    npairs = len(pairs)
    nex = len(extras)
    fused = epilogue is not None
    assert not fused or nk == 1
    out_dtypes = list(out_dtype) if fused else [out_dtype]

    def body(*refs):
        prods = []
        for p in range(npairs):
            a = refs[2 * p][...].astype(BF16) if (p == 0 or pairs[p][0] is not pairs[p - 1][0]) else a
            b = refs[2 * p + 1][...].astype(BF16)
            prods.append(lax.dot_general(a, b, _DN[mode], preferred_element_type=F32))
        if fused:
            ex = [refs[2 * npairs + e][...].astype(F32) for e in range(nex)]
            for o_ref, val in zip(refs[2 * npairs + nex:], epilogue(prods, ex)):
                o_ref[...] = val.astype(o_ref.dtype)
            return
        o_ref = refs[2 * npairs]
        tot = prods[0]
        for d in prods[1:]:
            tot = tot + d
        if nk == 1:
            o_ref[...] = tot.astype(o_ref.dtype)
            return
        acc = refs[2 * npairs + 1]
        kk = pl.program_id(2)

        @pl.when(kk == 0)
        def _():
            acc[...] = tot

        @pl.when(kk > 0)
        def _():
            acc[...] += tot

        @pl.when(kk == nk - 1)
        def _():
            o_ref[...] = acc[...].astype(o_ref.dtype)

    rows_of = b_window[1] if b_window is not None else (lambda t: t)
    if mode == "nn":
        sp = [pl.BlockSpec((tm, tk), lambda i, j, kk: (i, kk)), pl.BlockSpec((tk, tn), lambda i, j, kk: (rows_of(kk), j))]
    elif mode == "nt":
        sp = [pl.BlockSpec((tm, tk), lambda i, j, kk: (i, kk)), pl.BlockSpec((tn, tk), lambda i, j, kk: (rows_of(j), kk))]
    else:
        sp = [pl.BlockSpec((tk, tm), lambda i, j, kk: (kk, i)), pl.BlockSpec((tk, tn), lambda i, j, kk: (kk, j))]
    o_spec = pl.BlockSpec((tm, tn), lambda i, j, kk: (i, j))
    out_shapes = [jax.ShapeDtypeStruct((m, n), dt) for dt in out_dtypes]
    return _run(
        body, [t for pr in pairs for t in pr] + list(extras), carry=carry, name=name, grid=(m // tm, n // tn, nk),
        in_specs=sp * npairs + [o_spec] * nex,
        out_specs=[o_spec] * len(out_shapes) if fused else o_spec,
        out_shape=out_shapes if fused else out_shapes[0],
        scratch_shapes=[pltpu.VMEM((tm, tn), F32)] if nk > 1 else [],
        compiler_params=_cparams(("parallel", "parallel", "arbitrary")),
    )


def rowwise(name, fn, row_ins, const_ins, row_outs, acc_outs=(), ts=None, carry=None):
    s = row_ins[0].shape[0]
    row_outs = [ro if len(ro) == 3 else (*ro, 1) for ro in row_outs]
    if ts is None:
        per_row = sum(a.shape[-1] * a.dtype.itemsize for a in row_ins) + sum(w * jnp.dtype(dt).itemsize for w, dt, _ in row_outs)
        ts = 512
        while ts > 8 and 2 * ts * per_row > 20 * 1024 * 1024:
            ts //= 2
    ts = min(ts, s)
    assert s % ts == 0
    nr, nc, no, na = len(row_ins), len(const_ins), len(row_outs), len(acc_outs)

    def body(*refs):
        rows = [r[...].reshape(ts, r.shape[-1]).astype(F32) for r in refs[:nr]]
        consts = [r[...] for r in refs[nr:nr + nc]]
        outs, accs = fn(rows, consts)
        for r, v in zip(refs[nr + nc:nr + nc + no], outs):
            r[...] = v.astype(r.dtype).reshape(r.shape)
        if na:
            first = pl.program_id(0) == 0
            for r, v in zip(refs[nr + nc + no:], accs):
                @pl.when(first)
                def _(r=r, v=v):
                    r[...] = v

                @pl.when(jnp.logical_not(first))
                def _(r=r, v=v):
                    r[...] += v

    def tile_spec(w, d):
        if d == 1:
            return pl.BlockSpec((ts, w), lambda i: (i, 0))
        return pl.BlockSpec((d, ts // d, w), lambda i: (0, i, 0))

    in_specs = [tile_spec(a.shape[-1], a.shape[0] if a.ndim == 3 else 1) for a in row_ins]
    in_specs += [pl.BlockSpec(c.shape, lambda i, nd=c.ndim: (0,) * nd) for c in const_ins]
    out_specs = [tile_spec(w, d) for w, _, d in row_outs]
    out_specs += [pl.BlockSpec(shp, lambda i, nd=len(shp): (0,) * nd) for shp in acc_outs]
    out_shape = [jax.ShapeDtypeStruct((s, w) if d == 1 else (d, s // d, w), dt) for w, dt, d in row_outs]
    out_shape += [jax.ShapeDtypeStruct(shp, F32) for shp in acc_outs]
    return _run(
        body, [*row_ins, *const_ins], carry=carry, name=name, grid=(s // ts,), in_specs=in_specs, out_specs=out_specs,
        out_shape=out_shape, compiler_params=_cparams(("arbitrary",)),
    )


PERM_TS = 256


def _perm_matrix(ts, d, inverse):
    i = lax.broadcasted_iota(jnp.int32, (ts, ts), 0)
    k = lax.broadcasted_iota(jnp.int32, (ts, ts), 1)
    per = ts // d
    src = (i % d) * per + i // d if inverse else (i % per) * d + i // per
    return jnp.where(k == src, 1.0, 0.0).astype(BF16)


def _permute(p, x):
    if x.dtype == BF16:
        return jnp.dot(p, x, preferred_element_type=F32)
    hi = x.astype(BF16)
    rest = x - hi.astype(F32)
    mid = rest.astype(BF16)
    lo = (rest - mid.astype(F32)).astype(BF16)
    out = jnp.dot(p, hi, preferred_element_type=F32) + jnp.dot(p, mid, preferred_element_type=F32)
    return out + jnp.dot(p, lo, preferred_element_type=F32)


def _rms(x, gain):
    r = lax.rsqrt(jnp.mean(x * x, axis=-1, keepdims=True) + EPS)
    n = x * r
    return n * gain, n, r


def _rms_bwd(dy, n, r, gain):
    dn = dy * gain
    dx = r * (dn - n * jnp.mean(dn * n, axis=-1, keepdims=True))
    return dx, jnp.sum(dy * n, axis=0, keepdims=True)


def _sigmoid(x):
    return 1.0 / (1.0 + jnp.exp(-x))


_GELU_K = math.sqrt(2.0 / math.pi)


def _gelu(x):
    t = jnp.tanh(_GELU_K * (x + 0.044715 * x * x * x))
    return 0.5 * x * (1.0 + t), t


def _gelu_grad(x, t):
    return 0.5 * (1.0 + t) + 0.5 * x * (1.0 - t * t) * _GELU_K * (1.0 + 3.0 * 0.044715 * x * x)


def _head_sum(x):
    parts = []
    for h in range(HEADS_PER_GROUP):
        sl = x[:, h * HEAD_DIM:(h + 1) * HEAD_DIM]
        parts.append(jnp.broadcast_to(jnp.sum(sl, axis=-1, keepdims=True), sl.shape))
    return jnp.concatenate(parts, axis=-1)


def _mix_weights(l0, l1, l2):
    mx = jnp.maximum(jnp.maximum(l0, l1), l2)
    e0, e1, e2 = jnp.exp(l0 - mx), jnp.exp(l1 - mx), jnp.exp(l2 - mx)
    inv = 1.0 / (e0 + e1 + e2)
    return e0 * inv, e1 * inv, e2 * inv


BLK = 128


def _slopes(g):
    return [2.0 ** (-8.0 * (g * HEADS_PER_GROUP + h + 1) / N_HEADS) for h in range(HEADS_PER_GROUP)]


def _attn_masks(dil):
    qi = lax.broadcasted_iota(jnp.int32, (BLK, BLK), 0)
    ki = lax.broadcasted_iota(jnp.int32, (BLK, BLK), 1)
    dist_c = qi - ki
    dist_p = BLK + qi - ki
    return dist_c >= 0, dist_p <= BLK, (dist_c * dil).astype(F32), (dist_p * dil).astype(F32)


def _window_mask(has_prev, dil):
    qi = lax.broadcasted_iota(jnp.int32, (BLK, 2 * BLK), 0)
    ki = lax.broadcasted_iota(jnp.int32, (BLK, 2 * BLK), 1)
    dist = BLK + qi - ki
    ok = jnp.logical_and(jnp.logical_and(dist >= 0, dist <= BLK), jnp.logical_or(ki >= BLK, has_prev))
    return ok, (dist * dil).astype(F32)


def attn_fwd(qkv, g, name):
    dil, length, _ = qkv.shape
    scale = HEAD_DIM ** -0.5
    slopes = _slopes(g)

    def body(q_ref, kc_ref, vc_ref, kp_ref, vp_ref, o_ref, l_ref):
        ok, dist = _window_mask(pl.program_id(1) > 0, dil)
        for h in range(HEADS_PER_GROUP):
            sl = slice(h * HEAD_DIM, (h + 1) * HEAD_DIM)
            k2 = jnp.concatenate([kp_ref[:, sl], kc_ref[:, sl]], axis=0)
            v2 = jnp.concatenate([vp_ref[:, sl], vc_ref[:, sl]], axis=0)
            s = lax.dot_general(q_ref[:, sl], k2, _DN["nt"], preferred_element_type=F32) * scale - slopes[h] * dist
            s = jnp.where(ok, s, NEG)
            mx = jnp.max(s, axis=-1, keepdims=True)
            p = jnp.exp(s - mx)
            den = jnp.sum(p, axis=-1, keepdims=True)
            o_ref[:, sl] = (jnp.dot(p.astype(BF16), v2, preferred_element_type=F32) / den).astype(BF16)
            l_ref[:, sl] = jnp.broadcast_to(mx + jnp.log(den), (BLK, HEAD_DIM))

    def spec(col, prev):
        if prev:
            return pl.BlockSpec((None, BLK, GROUP_W), lambda r, n: (r, jnp.maximum(n - 1, 0), col))
        return pl.BlockSpec((None, BLK, GROUP_W), lambda r, n: (r, n, col))

    out_spec = pl.BlockSpec((None, BLK, GROUP_W), lambda r, n: (r, n, 0))
    return _pcall(
        body, name=name, grid=(dil, length // BLK),
        in_specs=[spec(0, False), spec(1, False), spec(2, False), spec(1, True), spec(2, True)],
        out_specs=[out_spec, out_spec],
        out_shape=[jax.ShapeDtypeStruct((dil, length, GROUP_W), BF16), jax.ShapeDtypeStruct((dil, length, GROUP_W), F32)],
        compiler_params=_cparams(("parallel", "parallel")),
    )(qkv, qkv, qkv, qkv, qkv)


def attn_bwd(qkv, dout, lse, dd, g, name, carry=None):
    dil, length, _ = qkv.shape
    nblk = length // BLK
    scale = HEAD_DIM ** -0.5
    slopes = _slopes(g)

    def body(q_ref, kc_ref, vc_ref, kp_ref, vp_ref, qn_ref, do_ref, don_ref, l_ref, ln_ref, d_ref, dn_ref, o_ref):
        n = pl.program_id(1)
        ok2, dist2 = _window_mask(n > 0, dil)
        _, ok_p, _, dp = _attn_masks(dil)
        ok_next = jnp.logical_and(ok_p, n < nblk - 1)
        for h in range(HEADS_PER_GROUP):
            sl = slice(h * HEAD_DIM, (h + 1) * HEAD_DIM)
            q, kc, vc, qn = q_ref[:, sl], kc_ref[:, sl], vc_ref[:, sl], qn_ref[:, sl]
            k2 = jnp.concatenate([kp_ref[:, sl], kc], axis=0)
            v2 = jnp.concatenate([vp_ref[:, sl], vc], axis=0)
            do, don = do_ref[:, sl], don_ref[:, sl]
            lse_q, lse_n, dd_q, dd_n = l_ref[:, sl], ln_ref[:, sl], d_ref[:, sl], dn_ref[:, sl]

            def probs(qq, kk, dist, ok, lse_t):
                s = lax.dot_general(qq, kk, _DN["nt"], preferred_element_type=F32) * scale - slopes[h] * dist
                return jnp.where(ok, jnp.exp(jnp.where(ok, s, NEG) - lse_t), 0.0)

            p2 = probs(q, k2, dist2, ok2, jnp.concatenate([lse_q, lse_q], axis=1))
            p_x = probs(qn, kc, dp, ok_next, lse_n)
            ds2 = p2 * (lax.dot_general(do, v2, _DN["nt"], preferred_element_type=F32) - jnp.concatenate([dd_q, dd_q], axis=1))
            ds_x = p_x * (lax.dot_general(don, vc, _DN["nt"], preferred_element_type=F32) - dd_n)
            dq = jnp.dot(ds2.astype(BF16), k2, preferred_element_type=F32)
            ds_k = jnp.concatenate([ds2[:, BLK:], ds_x], axis=0).astype(BF16)
            p_k = jnp.concatenate([p2[:, BLK:], p_x], axis=0).astype(BF16)
            dk = lax.dot_general(ds_k, jnp.concatenate([q, qn], axis=0), _DN["tn"], preferred_element_type=F32)
            dv = lax.dot_general(p_k, jnp.concatenate([do, don], axis=0), _DN["tn"], preferred_element_type=F32)
            o_ref[:, h * HEAD_DIM:(h + 1) * HEAD_DIM] = (dq * scale).astype(BF16)
            o_ref[:, GROUP_W + h * HEAD_DIM:GROUP_W + (h + 1) * HEAD_DIM] = (dk * scale).astype(BF16)
            o_ref[:, 2 * GROUP_W + h * HEAD_DIM:2 * GROUP_W + (h + 1) * HEAD_DIM] = dv.astype(BF16)

    def spec(col, which):
        if which == "prev":
            return pl.BlockSpec((None, BLK, GROUP_W), lambda r, n: (r, jnp.maximum(n - 1, 0), col))
        if which == "next":
            return pl.BlockSpec((None, BLK, GROUP_W), lambda r, n: (r, jnp.minimum(n + 1, nblk - 1), col))
        return pl.BlockSpec((None, BLK, GROUP_W), lambda r, n: (r, n, col))

    return _run(
        body, [qkv, qkv, qkv, qkv, qkv, qkv, dout, dout, lse, lse, dd, dd], carry=carry, name=name, grid=(dil, nblk),
        in_specs=[spec(0, "cur"), spec(1, "cur"), spec(2, "cur"), spec(1, "prev"), spec(2, "prev"), spec(0, "next"),
                  spec(0, "cur"), spec(0, "next"), spec(0, "cur"), spec(0, "next"), spec(0, "cur"), spec(0, "next")],
        out_specs=pl.BlockSpec((None, BLK, 3 * GROUP_W), lambda r, n: (r, n, 0)),
        out_shape=jax.ShapeDtypeStruct((dil, length, 3 * GROUP_W), BF16),
        compiler_params=_cparams(("parallel", "parallel")),
    )


def _ssm_prep_values(are, aim, logdt):
    dt = jnp.exp(logdt)
    mag = jnp.exp(are * dt)
    lb_re, lb_im = mag * jnp.cos(aim * dt), mag * jnp.sin(aim * dt)
    inv = 1.0 / (are * are + aim * aim)
    n_re, n_im = lb_re - 1.0, lb_im
    f_re = (n_re * are + n_im * aim) * inv
    f_im = (n_im * are - n_re * aim) * inv
    return dt, lb_re, lb_im, f_re, f_im, inv


PREP_G = 8


def _group_specs(are, logdt, bre):
    def spec(a):
        return pl.BlockSpec((PREP_G,) + a.shape[1:], lambda i: (i, 0, 0))
    return spec(are), spec(logdt), spec(bre)


def ssm_prep(are, aim, logdt, bre, bim):
    def body(are_r, aim_r, ldt_r, bre_r, bim_r, lre_o, lim_o, bbre_o, bbim_o):
        _, lb_re, lb_im, f_re, f_im, _ = _ssm_prep_values(are_r[...], aim_r[...], ldt_r[...])
        lre_o[...] = lb_re
        lim_o[...] = lb_im
        bbre_o[...] = f_re * bre_r[...] - f_im * bim_r[...]
        bbim_o[...] = f_re * bim_r[...] + f_im * bre_r[...]

    sh1 = jax.ShapeDtypeStruct(are.shape, F32)
    shb = jax.ShapeDtypeStruct(bre.shape, F32)
    s1, sd, sb = _group_specs(are, logdt, bre)
    return _pcall(body, name="ssm_prep", grid=(SSM_GROUPS // PREP_G,), in_specs=[s1, s1, sd, sb, sb], out_specs=[s1, s1, sb, sb],
                  out_shape=[sh1, sh1, shb, shb], compiler_params=_cparams(("parallel",)))(are, aim, logdt, bre, bim)


def ssm_prep_bwd(are, aim, logdt, bre, bim, dbbre, dbbim, dlre, dlim):
    def body(are_r, aim_r, ldt_r, bre_r, bim_r, dbbre_r, dbbim_r, dlre_r, dlim_r, dare_o, daim_o, dldt_o, dbre_o, dbim_o):
        are_v, aim_v = are_r[...], aim_r[...]
        dt, lb_re, lb_im, f_re, f_im, inv = _ssm_prep_values(are_v, aim_v, ldt_r[...])
        b_re, b_im, g_re, g_im = bre_r[...], bim_r[...], dbbre_r[...], dbbim_r[...]
        dbre_o[...] = f_re * g_re + f_im * g_im
        dbim_o[...] = f_re * g_im - f_im * g_re
        df_re = jnp.sum(b_re * g_re + b_im * g_im, axis=-1, keepdims=True)
        df_im = jnp.sum(b_re * g_im - b_im * g_re, axis=-1, keepdims=True)
        il_re, il_im = are_v * inv, -aim_v * inv
        cl_re = dlre_r[...] + il_re * df_re + il_im * df_im
        cl_im = dlim_r[...] + il_re * df_im - il_im * df_re
        q_re = -(f_re * il_re - f_im * il_im)
        q_im = -(f_re * il_im + f_im * il_re)
        ca_re = q_re * df_re + q_im * df_im
        ca_im = q_re * df_im - q_im * df_re
        cz_re = lb_re * cl_re + lb_im * cl_im
        cz_im = lb_re * cl_im - lb_im * cl_re
        dare_o[...] = ca_re + dt * cz_re
        daim_o[...] = ca_im + dt * cz_im
        dldt_o[...] = dt * jnp.sum(are_v * cz_re + aim_v * cz_im, axis=1, keepdims=True)

    sh1 = jax.ShapeDtypeStruct(are.shape, F32)
    shb = jax.ShapeDtypeStruct(bre.shape, F32)
    s1, sd, sb = _group_specs(are, logdt, bre)
    return _pcall(
        body, name="ssm_prep_bwd", grid=(SSM_GROUPS // PREP_G,), in_specs=[s1, s1, sd, sb, sb, sb, sb, s1, s1],
        out_specs=[s1, s1, sd, sb, sb], out_shape=[sh1, sh1, jax.ShapeDtypeStruct(logdt.shape, F32), shb, shb],
        compiler_params=_cparams(("parallel",)),
    )(are, aim, logdt, bre, bim, dbbre, dbbim, dlre, dlim)


SCAN_WC = 512


def _chain_segments(a_re, a_im, e_re, e_im, nsq, reverse):
    p_re, p_im = a_re, a_im
    for _ in range(nsq):
        p_re, p_im = p_re * p_re - p_im * p_im, 2.0 * p_re * p_im
    row = lax.broadcasted_iota(jnp.int32, e_re.shape, 0)
    edge = (row == SEGS - 1) if reverse else (row == 0)
    shift = SEGS - 1 if reverse else 1
    c_re, c_im = jnp.zeros_like(e_re), jnp.zeros_like(e_im)
    for _ in range(SEGS - 1):
        n_re = p_re * c_re - p_im * c_im + e_re
        n_im = p_re * c_im + p_im * c_re + e_im
        c_re = jnp.where(edge, 0.0, pltpu.roll(n_re, shift, 0))
        c_im = jnp.where(edge, 0.0, pltpu.roll(n_im, shift, 0))
    return c_re, c_im


def _scan_dims(s):
    steps = s // SEGS
    assert steps & (steps - 1) == 0
    tt = min(128, steps)
    return steps, tt, steps // tt, tt * SEGS, int(math.log2(steps))


U_BLK = SSM_W // BD


def ssm_fwd(u_s, dvec, w_bre, w_bim, w_cre, w_cim_neg, lre, lim, name, carry=None):
    s = u_s.shape[0]
    steps, tt, nch, rows, nsq = _scan_dims(s)
    nb, ub_w, wc = w_bre.shape

    def body(u_r, d_r, bre_r, bim_r, cre_r, cim_r, lre_r, lim_r, yg_o, ys_o, hre_o, him_o, hin_re_o, hin_im_o,
             st_re, st_im, x_re, x_im, h_re, h_im):
        ps, ch = pl.program_id(1), pl.program_id(2)
        a_re = jnp.broadcast_to(lre_r[...], (SEGS, wc))
        a_im = jnp.broadcast_to(lim_r[...], (SEGS, wc))
        ub = u_r[...]
        ub16 = ub.astype(BF16)
        x_re[...] = jnp.dot(ub16, bre_r[...], preferred_element_type=F32)
        x_im[...] = jnp.dot(ub16, bim_r[...], preferred_element_type=F32)

        @pl.when(jnp.logical_and(ps == 0, ch == 0))
        def _():
            st_re[...] = jnp.zeros_like(st_re)
            st_im[...] = jnp.zeros_like(st_im)

        @pl.when(jnp.logical_and(ps == 1, ch == 0))
        def _():
            c_re, c_im = _chain_segments(a_re, a_im, st_re[...], st_im[...], nsq, False)
            st_re[...] = c_re
            st_im[...] = c_im
            hin_re_o[...] = c_re
            hin_im_o[...] = c_im

        def run(store):
            def step(t, hc):
                off = pl.multiple_of(t * SEGS, SEGS)
                n_re = a_re * hc[0] - a_im * hc[1] + x_re[pl.ds(off, SEGS), :]
                n_im = a_re * hc[1] + a_im * hc[0] + x_im[pl.ds(off, SEGS), :]
                if store:
                    h_re[pl.ds(off, SEGS), :] = n_re
                    h_im[pl.ds(off, SEGS), :] = n_im
                return n_re, n_im

            fin = lax.fori_loop(0, tt, step, (st_re[...], st_im[...]))
            st_re[...] = fin[0]
            st_im[...] = fin[1]

        @pl.when(ps == 0)
        def _():
            run(False)

        @pl.when(ps == 1)
        def _():
            run(True)
            hr16, hi16 = h_re[...].astype(BF16), h_im[...].astype(BF16)
            hre_o[...] = hr16
            him_o[...] = hi16
            y = jnp.dot(hr16, cre_r[...], preferred_element_type=F32) + jnp.dot(hi16, cim_r[...], preferred_element_type=F32)
            y = y + d_r[...] * ub
            ys_o[...] = y
            yg_o[...] = _gelu(y)[0].astype(BF16)

    def pass1(ps, c):
        return jnp.where(ps == 1, c, 0)

    u_spec = pl.BlockSpec((rows, ub_w), lambda j, ps, c: (c, j))
    d_spec = pl.BlockSpec((1, ub_w), lambda j, ps, c: (0, j))
    b_spec = pl.BlockSpec((None, ub_w, wc), lambda j, ps, c: (j, 0, 0))
    c_spec = pl.BlockSpec((None, wc, ub_w), lambda j, ps, c: (j, 0, 0))
    l_spec = pl.BlockSpec((1, wc), lambda j, ps, c: (0, j))
    y_spec = pl.BlockSpec((rows, ub_w), lambda j, ps, c: (pass1(ps, c), j))
    h_spec = pl.BlockSpec((rows, wc), lambda j, ps, c: (pass1(ps, c), j))
    e_spec = pl.BlockSpec((SEGS, wc), lambda j, ps, c: (0, j))
    return _run(
        body, [u_s, dvec, w_bre, w_bim, w_cre, w_cim_neg, lre, lim], carry=carry, name=name, grid=(nb, 2, nch),
        in_specs=[u_spec, d_spec, b_spec, b_spec, c_spec, c_spec, l_spec, l_spec],
        out_specs=[y_spec, y_spec, h_spec, h_spec, e_spec, e_spec],
        out_shape=[jax.ShapeDtypeStruct((s, SSM_W), BF16), jax.ShapeDtypeStruct((s, SSM_W), F32),
                   jax.ShapeDtypeStruct((s, STATE_W), BF16), jax.ShapeDtypeStruct((s, STATE_W), BF16),
                   jax.ShapeDtypeStruct((SEGS, STATE_W), F32), jax.ShapeDtypeStruct((SEGS, STATE_W), F32)],
        scratch_shapes=[pltpu.VMEM((SEGS, wc), F32)] * 2 + [pltpu.VMEM((rows, wc), F32)] * 4,
        compiler_params=_cparams(("parallel", "arbitrary", "arbitrary")),
    )


def ssm_bwd(dyg_s, ys, u_s, h_re, h_im, hin_re, hin_im, gin_re, gin_im, dvec, w_bre_t, w_bim_t, w_cre_t, w_cim_neg_t, lre, lim,
            name, carry=None):
    s = u_s.shape[0]
    steps, tt, nch, rows, nsq = _scan_dims(s)
    half = 2 * SEGS

    def body(dyg_r, ys_r, u_r, hre_r, him_r, pre_r, pim_r, cin_re_r, cin_im_r, gin_re_r, gin_im_r, d_r, bre_r, bim_r, cre_r,
             cim_r, lre_r, lim_r, du_o, dbre_o, dbim_o, dcre_o, dcim_o, dlre_o, dlim_o, dd_o,
             st_re, st_im, x_re, x_im, g_re, g_im, hf_re, hf_im):
        ch = pl.program_id(1)
        a_re = jnp.broadcast_to(lre_r[...], (SEGS, SCAN_WC))
        a_im = -jnp.broadcast_to(lim_r[...], (SEGS, SCAN_WC))
        ub, y = u_r[...], ys_r[...]
        dy = dyg_r[...] * _gelu_grad(y, _gelu(y)[1])
        dy16 = dy.astype(BF16)
        x_re[...] = jnp.dot(dy16, cre_r[...], preferred_element_type=F32)
        x_im[...] = jnp.dot(dy16, cim_r[...], preferred_element_type=F32)

        @pl.when(ch == 0)
        def _():
            st_re[...] = gin_re_r[...]
            st_im[...] = gin_im_r[...]
            dlre_o[...] = jnp.zeros_like(dlre_o)
            dlim_o[...] = jnp.zeros_like(dlim_o)

        hf_re[...] = hre_r[...].astype(F32)
        hf_im[...] = him_r[...].astype(F32)
        first_chunk = ch == nch - 1
        edge_re = jnp.where(first_chunk, cin_re_r[...], pre_r[...].astype(F32)[SEGS:, :])
        edge_im = jnp.where(first_chunk, cin_im_r[...], pim_r[...].astype(F32)[SEGS:, :])

        def step(i, hc):
            t = tt - 1 - i
            off = pl.multiple_of(t * SEGS, SEGS)
            n_re = a_re * hc[0] - a_im * hc[1] + x_re[pl.ds(off, SEGS), :]
            n_im = a_re * hc[1] + a_im * hc[0] + x_im[pl.ds(off, SEGS), :]
            g_re[pl.ds(off, SEGS), :] = n_re
            g_im[pl.ds(off, SEGS), :] = n_im
            offp = pl.multiple_of(jnp.maximum(t - 1, 0) * SEGS, SEGS)
            hp_re = jnp.where(t == 0, edge_re, hf_re[pl.ds(offp, SEGS), :])
            hp_im = jnp.where(t == 0, edge_im, hf_im[pl.ds(offp, SEGS), :])
            return n_re, n_im, hc[2] + hp_re * n_re + hp_im * n_im, hc[3] + hp_re * n_im - hp_im * n_re

        fin = lax.fori_loop(0, tt, step, (st_re[...], st_im[...], dlre_o[...], dlim_o[...]))
        st_re[...] = fin[0]
        st_im[...] = fin[1]
        dlre_o[...] = fin[2]
        dlim_o[...] = fin[3]

        gr16, gi16 = g_re[...].astype(BF16), g_im[...].astype(BF16)
        du = jnp.dot(gr16, bre_r[...], preferred_element_type=F32) + jnp.dot(gi16, bim_r[...], preferred_element_type=F32)
        du_o[...] = du + d_r[...] * dy
        ub16 = ub.astype(BF16)
        parts = [
            (dbre_o, lax.dot_general(ub16, gr16, _DN["tn"], preferred_element_type=F32)),
            (dbim_o, lax.dot_general(ub16, gi16, _DN["tn"], preferred_element_type=F32)),
            (dcre_o, lax.dot_general(hre_r[...], dy16, _DN["tn"], preferred_element_type=F32)),
            (dcim_o, lax.dot_general(him_r[...], dy16, _DN["tn"], preferred_element_type=F32)),
            (dd_o, jnp.sum(dy * ub, axis=0, keepdims=True)),
        ]
        for ref, val in parts:
            @pl.when(ch == 0)
            def _(ref=ref, val=val):
                ref[...] = val

            @pl.when(ch > 0)
            def _(ref=ref, val=val):
                ref[...] += val

    def chunk(c):
        return nch - 1 - c

    u_spec = pl.BlockSpec((rows, U_BLK), lambda j, c: (chunk(c), j))
    h_spec = pl.BlockSpec((rows, SCAN_WC), lambda j, c: (chunk(c), j))
    prev_spec = pl.BlockSpec((half, SCAN_WC), lambda j, c: (jnp.maximum(chunk(c) * (rows // half) - 1, 0), j))
    e_spec = pl.BlockSpec((SEGS, SCAN_WC), lambda j, c: (0, j))
    d_spec = pl.BlockSpec((1, U_BLK), lambda j, c: (0, j))
    bt_spec = pl.BlockSpec((None, SCAN_WC, U_BLK), lambda j, c: (j, 0, 0))
    ct_spec = pl.BlockSpec((None, U_BLK, SCAN_WC), lambda j, c: (j, 0, 0))
    l_spec = pl.BlockSpec((1, SCAN_WC), lambda j, c: (0, j))
    return _run(
        body, [dyg_s, ys, u_s, h_re, h_im, h_re, h_im, hin_re, hin_im, gin_re, gin_im, dvec, w_bre_t, w_bim_t, w_cre_t,
               w_cim_neg_t, lre, lim],
        carry=carry, name=name, grid=(BD, nch),
        in_specs=[u_spec, u_spec, u_spec, h_spec, h_spec, prev_spec, prev_spec, e_spec, e_spec, e_spec, e_spec, d_spec,
                  bt_spec, bt_spec, ct_spec, ct_spec, l_spec, l_spec],
        out_specs=[u_spec, ct_spec, ct_spec, bt_spec, bt_spec, e_spec, e_spec, d_spec],
        out_shape=[jax.ShapeDtypeStruct((s, SSM_W), F32)] + [jax.ShapeDtypeStruct((BD, U_BLK, SCAN_WC), F32)] * 2
        + [jax.ShapeDtypeStruct((BD, SCAN_WC, U_BLK), F32)] * 2 + [jax.ShapeDtypeStruct((SEGS, STATE_W), F32)] * 2
        + [jax.ShapeDtypeStruct((1, SSM_W), F32)],
        scratch_shapes=[pltpu.VMEM((SEGS, SCAN_WC), F32)] * 2 + [pltpu.VMEM((rows, SCAN_WC), F32)] * 6,
        compiler_params=_cparams(("parallel", "arbitrary")),
    )


def ssm_bwd_ends(dyg_s, ys, w_cre_t, w_cim_neg_t, lre, lim, name, carry=None):
    s = ys.shape[0]
    steps, tt, nch, rows, nsq = _scan_dims(s)
    nb, ub_w, wc = w_cre_t.shape

    def body(dyg_r, ys_r, cre_r, cim_r, lre_r, lim_r, gin_re_o, gin_im_o, st_re, st_im, x_re, x_im):
        ch = pl.program_id(1)
        a_re = jnp.broadcast_to(lre_r[...], (SEGS, wc))
        a_im = -jnp.broadcast_to(lim_r[...], (SEGS, wc))
        y = ys_r[...]
        dy16 = (dyg_r[...] * _gelu_grad(y, _gelu(y)[1])).astype(BF16)
        x_re[...] = jnp.dot(dy16, cre_r[...], preferred_element_type=F32)
        x_im[...] = jnp.dot(dy16, cim_r[...], preferred_element_type=F32)

        @pl.when(ch == 0)
        def _():
            st_re[...] = jnp.zeros_like(st_re)
            st_im[...] = jnp.zeros_like(st_im)

        def step(i, hc):
            off = pl.multiple_of((tt - 1 - i) * SEGS, SEGS)
            return (a_re * hc[0] - a_im * hc[1] + x_re[pl.ds(off, SEGS), :],
                    a_re * hc[1] + a_im * hc[0] + x_im[pl.ds(off, SEGS), :])

        fin = lax.fori_loop(0, tt, step, (st_re[...], st_im[...]))
        st_re[...] = fin[0]
        st_im[...] = fin[1]

        @pl.when(ch == nch - 1)
        def _():
            c_re, c_im = _chain_segments(a_re, a_im, fin[0], fin[1], nsq, True)
            gin_re_o[...] = c_re
            gin_im_o[...] = c_im

    y_spec = pl.BlockSpec((rows, ub_w), lambda j, c: (nch - 1 - c, j))
    ct_spec = pl.BlockSpec((None, ub_w, wc), lambda j, c: (j, 0, 0))
    l_spec = pl.BlockSpec((1, wc), lambda j, c: (0, j))
    e_spec = pl.BlockSpec((SEGS, wc), lambda j, c: (0, j))
    return _run(
        body, [dyg_s, ys, w_cre_t, w_cim_neg_t, lre, lim], carry=carry, name=name, grid=(nb, nch),
        in_specs=[y_spec, y_spec, ct_spec, ct_spec, l_spec, l_spec], out_specs=[e_spec, e_spec],
        out_shape=[jax.ShapeDtypeStruct((SEGS, STATE_W), F32)] * 2,
        scratch_shapes=[pltpu.VMEM((SEGS, wc), F32)] * 2 + [pltpu.VMEM((rows, wc), F32)] * 2,
        compiler_params=_cparams(("parallel", "arbitrary")),
    )


FWD_BD = 4


def _block_diag(m, nb=BD):
    g, r, c = m.shape
    m = m.reshape(nb, g // nb, r, c)
    eye = jnp.eye(g // nb, dtype=m.dtype)
    return jnp.einsum("jarc,ab->jarbc", m, eye).reshape(nb, (g // nb) * r, (g // nb) * c)


def _block_diag_extract(m, r, c):
    per = m.shape[1] // r
    m = m.reshape(BD, per, r, per, c)
    return jnp.einsum("jarac->jarc", m).reshape(BD * per, r, c)


def to_segments(a):
    s, w = a.shape
    return a.reshape(SEGS, s // SEGS, w).transpose(1, 0, 2).reshape(s, w)


def from_segments(a):
    s, w = a.shape
    return a.reshape(s // SEGS, SEGS, w).transpose(1, 0, 2).reshape(s, w)


W_IN_CHUNK_ROWS = (320, 320, 320, 240, 576, 272)
TALL_TM = 2048
FFN_TN = 512


def local_step(x, target, shards, small):
    s = x.shape[0]
    g1, g2, g3, g4 = (small[k].reshape(1, D_MODEL) for k in ("norm_mix_pre", "norm_mix_post", "norm_ffn_pre", "norm_ffn_post"))
    dvec = small["ssm_d"].reshape(1, SSM_W)
    wts, recv = {}, {}

    def gathered(names, blocks):
        for n, b in zip(names, blocks):
            wts[n] = _full_from_gathered(b, n)

    def rms_in_fn(r, c):
        hh = _rms(r[0], c[0])[0].astype(BF16)
        return [hh, _permute(_perm_matrix(PERM_TS, 4, False), hh), _permute(_perm_matrix(PERM_TS, 16, False), hh)], []

    (h, h4, h16), got = rowwise("rms_in", rms_in_fn, [x], [g1], [(D_MODEL, BF16), (D_MODEL, BF16, 4), (D_MODEL, BF16, 16)],
                                ts=PERM_TS, carry=Gather([shards["w_in"]]))
    w_in_t = _full_from_gathered(got[0], "w_in")
    w_u_t, w_gates_t = w_in_t[3 * HQ:3 * HQ + SSM_W], w_in_t[3 * HQ + SSM_W:]

    def qkv_rows(g):
        return 3 * GROUP_W, lambda t: 3 * t + g

    hd = [h.reshape(1, s, D_MODEL), h4, h16]
    qkv = [None] * 3
    names = ("w_attn_up", "w_glu_v", "w_glu_g")
    qkv[0], got = mm([(hd[0].reshape(s, D_MODEL), w_in_t)], "nt", BF16, "mm_qkv0", tm=TALL_TM, tn=GROUP_W, b_window=qkv_rows(0),
                     carry=Gather([shards[n] for n in names]))
    gathered(names, got)
    qkv[1], got = mm([(hd[1].reshape(s, D_MODEL), w_in_t)], "nt", BF16, "mm_qkv1", tm=TALL_TM, tn=GROUP_W, b_window=qkv_rows(1),
                     carry=Gather([shards["w_out"]]))
    gathered(("w_out",), got)
    qkv[2] = mm([(hd[2].reshape(s, D_MODEL), w_in_t)], "nt", BF16, "mm_qkv2", tm=TALL_TM, tn=GROUP_W, b_window=qkv_rows(2))
    u = mm([(h, w_u_t)], "nt", F32, "mm_u")
    gates, got = mm([(h, w_gates_t)], "nt", BF16, "mm_gates", carry=Gather([shards["w_ffn_gate"]]))
    gathered(("w_ffn_gate",), got)

    outs, lses = [], []
    for g, (_, dil) in enumerate(ATTN_GROUPS):
        o, l = attn_fwd(qkv[g].reshape(dil, s // dil, 3 * GROUP_W), g, f"attn_fwd{g}")
        outs.append(o.reshape(s, GROUP_W) if dil == 1 else o)
        lses.append(l.reshape(s, GROUP_W) if dil == 1 else l)

    def natural(r):
        back4, back16 = _perm_matrix(PERM_TS, 4, True), _perm_matrix(PERM_TS, 16, True)
        return (r[0], _permute(back4, r[1].astype(BF16)), _permute(back16, r[2].astype(BF16)),
                r[3], _permute(back4, r[4]), _permute(back16, r[5]))

    def merge_fn(r, c):
        o0, o1, o2, l0, l1, l2 = natural(r)
        w0, w1, w2 = _mix_weights(l0, l1, l2)
        return [w0 * o0 + w1 * o1 + w2 * o2], []

    (attn,) = rowwise("attn_merge", merge_fn, outs + lses, [], [(GROUP_W, BF16)], ts=PERM_TS)
    attn_branch = mm([(attn, wts["w_attn_up"])], "nn", BF16, "mm_up", tm=TALL_TM)

    are3 = small["ssm_a_re"].reshape(SSM_GROUPS, SSM_STATE, 1)
    aim3 = small["ssm_a_im"].reshape(SSM_GROUPS, SSM_STATE, 1)
    ldt3 = small["ssm_log_dt"].reshape(SSM_GROUPS, 1, 1)
    bre3 = small["ssm_b_re"].reshape(SSM_GROUPS, SSM_STATE, SSM_GROUP)
    bim3 = small["ssm_b_im"].reshape(SSM_GROUPS, SSM_STATE, SSM_GROUP)
    cre3 = small["ssm_c_re"].reshape(SSM_GROUPS, SSM_GROUP, SSM_STATE)
    cim3 = small["ssm_c_im"].reshape(SSM_GROUPS, SSM_GROUP, SSM_STATE)
    lre3, lim3, bbre, bbim = ssm_prep(are3, aim3, ldt3, bre3, bim3)
    lre, lim = lre3.reshape(1, STATE_W), lim3.reshape(1, STATE_W)
    w_bre = _block_diag(bbre.transpose(0, 2, 1)).astype(BF16)
    w_bim = _block_diag(bbim.transpose(0, 2, 1)).astype(BF16)
    w_cre = _block_diag(cre3.transpose(0, 2, 1)).astype(BF16)
    w_cim = _block_diag(cim3.transpose(0, 2, 1)).astype(BF16)
    u_s = to_segments(u)
    fwd_w = [_block_diag(t.transpose(0, 2, 1), FWD_BD).astype(BF16) for t in (bbre, bbim, cre3, -cim3)]
    (yg_s, y_ssm, h_re, h_im, hin_re, hin_im), got = ssm_fwd(
        u_s, dvec, *fwd_w, lre, lim, "ssm_fwd", carry=Gather([shards["w_ffn_up"]], pass_early=True))
    gathered(("w_ffn_up",), got)
    yg = from_segments(yg_s)
    gv = mm([(yg, wts["w_glu_v"])], "nn", BF16, "mm_glu_v", tm=TALL_TM)
    gg = mm([(yg, wts["w_glu_g"])], "nn", BF16, "mm_glu_g", tm=TALL_TM)

    def gate_fn(r, c):
        gts, ab, gv_, gg_ = r
        sa, ss = _sigmoid(gts[:, :D_MODEL]), _sigmoid(gts[:, D_MODEL:])
        return [sa * ab + ss * (gv_ * _sigmoid(gg_))], []

    (merged,) = rowwise("gate_merge", gate_fn, [gates, attn_branch, gv, gg], [], [(D_MODEL, BF16)])
    o_mix = mm([(merged, wts["w_out"])], "nn", F32, "mm_out")

    def mid_fn(r, c):
        x1 = r[0] + _rms(r[1], c[0])[0]
        return [x1, _rms(x1, c[1])[0]], []

    x1, h2 = rowwise("rms_mid", mid_fn, [x, o_mix], [g2, g3], [(D_MODEL, F32), (D_MODEL, BF16)])
    (fa, fb, fin), got = mm([(h2, wts["w_ffn_gate"]), (h2, wts["w_ffn_up"])], "nt", [BF16, BF16, BF16], "mm_ffn_in", tn=FFN_TN,
                            epilogue=lambda p, e: [p[0], p[1], p[0] * _sigmoid(p[0]) * p[1]],
                            carry=Gather([shards["w_ffn_down"]], pass_early=True))
    gathered(("w_ffn_down",), got)
    f = mm([(fin, wts["w_ffn_down"])], "nn", F32, "mm_ffn_down", tn=512, tk=D_FF)

    def loss_fn(r, c):
        x1_, f_, tgt = r
        y, n, rr = _rms(f_, c[0])
        err = x1_ + y - tgt
        dout = err * (1.0 / D_MODEL)
        df, dg = _rms_bwd(dout, n, rr, c[0])
        lp = 0.5 * jnp.sum(jnp.sum(err * err, axis=-1, keepdims=True) * (1.0 / D_MODEL), axis=0, keepdims=True)
        return [df, dout], [dg, lp]

    df, dout, dg4, loss_part = rowwise("loss_bwd", loss_fn, [x1, f, target], [g4], [(D_MODEL, BF16), (D_MODEL, BF16)],
                                       acc_outs=[(1, D_MODEL), (1, 1)])
    def sent(names, blocks):
        for n, b in zip(names, blocks):
            recv[n] = b

    def to_owners(names, dws):
        return AllToAll([_split_for_devices(d, n) for n, d in zip(names, dws)])

    def swiglu_bwd(p, e):
        dfin_, (a, b) = p[0], e
        sg = _sigmoid(a)
        return [dfin_ * b * (sg * (1.0 + a * (1.0 - sg))), dfin_ * a * sg]

    da, db = mm([(df, wts["w_ffn_down"])], "nt", [BF16, BF16], "mm_d_fin", tn=FFN_TN, epilogue=swiglu_bwd, extras=[fa, fb])
    dw_ffn_down = mm([(fin, df)], "tn", BF16, "mm_dw_ffn_down")
    dh2, got = mm([(da, wts["w_ffn_gate"]), (db, wts["w_ffn_up"])], "nn", F32, "mm_d_h2", tm=512, tn=1024, tk=D_FF // 2,
                  carry=to_owners(["w_ffn_down"], [dw_ffn_down]))
    sent(["w_ffn_down"], got)
    dw_ffn_gate = mm([(da, h2)], "tn", BF16, "mm_dw_ffn_gate")
    dw_ffn_up, got = mm([(db, h2)], "tn", BF16, "mm_dw_ffn_up", carry=to_owners(["w_ffn_gate"], [dw_ffn_gate]))
    sent(["w_ffn_gate"], got)

    def mid_bwd(r, c):
        dh2_, dout_, x1_, o_ = r
        _, n3, r3 = _rms(x1_, c[1])
        dx1, dg3_ = _rms_bwd(dh2_, n3, r3, c[1])
        dx1 = dx1 + dout_
        _, n2, r2 = _rms(o_, c[0])
        do_, dg2_ = _rms_bwd(dx1, n2, r2, c[0])
        return [dx1, do_], [dg2_, dg3_]

    dx1, do_mix, dg2, dg3 = rowwise("rms_mid_bwd", mid_bwd, [dh2, dout, x1, o_mix], [g2, g3], [(D_MODEL, F32), (D_MODEL, BF16)],
                                    acc_outs=[(1, D_MODEL), (1, D_MODEL)])
    dmerged = mm([(do_mix, wts["w_out"])], "nt", BF16, "mm_d_merged")
    dw_out = mm([(merged, do_mix)], "tn", BF16, "mm_dw_out")

    def gate_bwd(r, c):
        dm, gts, ab, gv_, gg_ = r
        sa, ss, sg = _sigmoid(gts[:, :D_MODEL]), _sigmoid(gts[:, D_MODEL:]), _sigmoid(gg_)
        branch = gv_ * sg
        dbranch = dm * ss
        dgates = jnp.concatenate([dm * ab * sa * (1.0 - sa), dm * branch * ss * (1.0 - ss)], axis=-1)
        return [dgates, dm * sa, dbranch * sg, dbranch * gv_ * sg * (1.0 - sg)], []

    dgates, dab, dgv, dgg = rowwise("gate_bwd", gate_bwd, [dmerged, gates, attn_branch, gv, gg], [],
                                    [(2 * D_MODEL, BF16), (D_MODEL, BF16), (D_MODEL, BF16), (D_MODEL, BF16)])
    dattn = mm([(dab, wts["w_attn_up"])], "nt", F32, "mm_d_attn")
    dw_up = mm([(attn, dab)], "tn", BF16, "mm_dw_up")
    dyg = mm([(dgv, wts["w_glu_v"]), (dgg, wts["w_glu_g"])], "nt", F32, "mm_d_yg")
    dw_glu_v = mm([(yg, dgv)], "tn", BF16, "mm_dw_glu_v")
    dw_glu_g = mm([(yg, dgg)], "tn", BF16, "mm_dw_glu_g")

    dyg_s = to_segments(dyg)
    (gin_re, gin_im), got = ssm_bwd_ends(dyg_s, y_ssm, fwd_w[2].transpose(0, 2, 1), fwd_w[3].transpose(0, 2, 1), lre, lim,
                                         "ssm_bwd_ends", carry=to_owners(["w_out"], [dw_out]))
    sent(["w_out"], got)
    (du_s, dbre_d, dbim_d, dcre_d, dcim_d, dl_re8, dl_im8, dd_ssm), got = ssm_bwd(
        dyg_s, y_ssm, u_s, h_re, h_im, hin_re, hin_im, gin_re, gin_im, dvec, w_bre.transpose(0, 2, 1), w_bim.transpose(0, 2, 1),
        w_cre.transpose(0, 2, 1), -w_cim.transpose(0, 2, 1), lre, lim, "ssm_bwd", carry=to_owners(["w_ffn_up"], [dw_ffn_up]))
    sent(["w_ffn_up"], got)
    dbb_re = _block_diag_extract(dbre_d, SSM_GROUP, SSM_STATE).transpose(0, 2, 1)
    dbb_im = _block_diag_extract(dbim_d, SSM_GROUP, SSM_STATE).transpose(0, 2, 1)
    dc_re = _block_diag_extract(dcre_d, SSM_STATE, SSM_GROUP).transpose(0, 2, 1)
    dc_im = -_block_diag_extract(dcim_d, SSM_STATE, SSM_GROUP).transpose(0, 2, 1)

    def fold8(r, c):
        return [], [jnp.sum(r[0], axis=0, keepdims=True), jnp.sum(r[1], axis=0, keepdims=True)]

    dl_re, dl_im = rowwise("ssm_dl_fold", fold8, [dl_re8, dl_im8], [], [], acc_outs=[(1, STATE_W), (1, STATE_W)], ts=SEGS)
    da_re, da_im, dldt, db_re, db_im = ssm_prep_bwd(
        are3, aim3, ldt3, bre3, bim3, dbb_re, dbb_im,
        dl_re.reshape(SSM_GROUPS, SSM_STATE, 1), dl_im.reshape(SSM_GROUPS, SSM_STATE, 1))
    du = from_segments(du_s)

    def merge_bwd(r, c):
        dat = r[0]
        o0, o1, o2, l0, l1, l2 = natural(r[1:])
        w0, w1, w2 = _mix_weights(l0, l1, l2)
        tot = _head_sum(dat * (w0 * o0 + w1 * o1 + w2 * o2))
        to4, to16 = _perm_matrix(PERM_TS, 4, False), _perm_matrix(PERM_TS, 16, False)
        return [w0 * dat, _permute(to4, (w1 * dat).astype(BF16)), _permute(to16, (w2 * dat).astype(BF16)),
                w0 * tot, _permute(to4, (w1 * tot).astype(BF16)), _permute(to16, (w2 * tot).astype(BF16))], []

    mb = rowwise("attn_merge_bwd", merge_bwd, [dattn] + outs + lses, [],
                 [(GROUP_W, BF16), (GROUP_W, BF16, 4), (GROUP_W, BF16, 16), (GROUP_W, BF16), (GROUP_W, BF16, 4), (GROUP_W, BF16, 16)],
                 ts=PERM_TS)
    dqs, dw_qkv = [], []
    names = ["w_glu_v", "w_glu_g", "w_attn_up"]
    for g, (_, dil) in enumerate(ATTN_GROUPS):
        dq = attn_bwd(qkv[g].reshape(dil, s // dil, 3 * GROUP_W), mb[g].reshape(dil, s // dil, GROUP_W),
                      lses[g].reshape(dil, s // dil, GROUP_W), mb[3 + g].reshape(dil, s // dil, GROUP_W),
                      g, f"attn_bwd{g}", carry=to_owners(names, [dw_glu_v, dw_glu_g, dw_up]) if g == 1 else None)
        if g == 1:
            dq, got = dq
            sent(names, got)
        dq = dq.reshape(s, 3 * GROUP_W)
        dqs.append(dq)
        dw_qkv.append(mm([(hd[g].reshape(s, D_MODEL), dq)], "tn", BF16, f"mm_dw_qkv{g}"))
    dw_u = mm([(h, du)], "tn", BF16, "mm_dw_u")
    dw_gates = mm([(h, dgates)], "tn", BF16, "mm_dw_gates")
    dw_in = jnp.concatenate(
        [dw_qkv[g][:, o * GROUP_W:(o + 1) * GROUP_W] for o in range(3) for g in range(3)] + [dw_u, dw_gates], axis=1)
    dw_in_blocks = _split_for_devices(dw_in, "w_in")
    starts = [sum(W_IN_CHUNK_ROWS[:i]) for i in range(len(W_IN_CHUNK_ROWS))]
    landed = None

    def chunk(i):
        return RowsToOwners(dw_in_blocks, starts[i], W_IN_CHUNK_ROWS[i], into=landed)

    dh_parts = []
    for g, (_, dil) in enumerate(ATTN_GROUPS):
        dh_g, (landed,) = mm([(dqs[g], w_in_t)], "nn", BF16, f"mm_d_h_qkv{g}", tk=GROUP_W, b_window=qkv_rows(g), carry=chunk(g))
        dh_parts.append(dh_g if dil == 1 else dh_g.reshape(dil, s // dil, D_MODEL))
    dh_u, (landed,) = mm([(du, w_u_t)], "nn", BF16, "mm_d_h_u", carry=chunk(3))
    dh_gates, (landed,) = mm([(dgates, w_gates_t)], "nn", BF16, "mm_d_h_gates", carry=chunk(4))
    dh_parts += [dh_u, dh_gates]

    def in_bwd(r, c):
        dh1 = _permute(_perm_matrix(PERM_TS, 4, True), r[1].astype(BF16))
        dh2_ = _permute(_perm_matrix(PERM_TS, 16, True), r[2].astype(BF16))
        dh = r[0] + dh1 + dh2_ + r[3] + r[4]
        _, n1, r1 = _rms(r[6], c[0])
        dx, dg1_ = _rms_bwd(dh, n1, r1, c[0])
        return [dx + r[5]], [dg1_]

    (grad_x, dg1), (landed,) = rowwise("rms_in_bwd", in_bwd, dh_parts + [dx1, x], [g1], [(D_MODEL, F32)],
                                       acc_outs=[(1, D_MODEL)], ts=PERM_TS, carry=chunk(5))
    recv["w_in"] = landed

    dsmall = dict(norm_mix_pre=dg1, ssm_a_re=da_re, ssm_a_im=da_im, ssm_log_dt=dldt, ssm_b_re=db_re, ssm_b_im=db_im,
                  ssm_c_re=dc_re, ssm_c_im=dc_im, ssm_d=dd_ssm, norm_mix_post=dg2, norm_ffn_pre=dg3, norm_ffn_post=dg4)
    return loss_part, grad_x, recv, dsmall


def adamw(parts, w, m, v, name, carry=None):
    r, c = w.shape
    tr = r
    while tr > 8 and tr % 2 == 0 and tr * c * (8 * parts.dtype.itemsize + 28) * 2 > 24 * 1024 * 1024:
        tr //= 2
    assert r % tr == 0 and (tr % 8 == 0 or tr == r)
    c1, c2 = 1.0 / (1.0 - ADAM_B1 ** ADAM_STEP), 1.0 / (1.0 - ADAM_B2 ** ADAM_STEP)

    def body(p_ref, w_ref, m_ref, v_ref, g_o, d_o, m_o, v_o):
        g = p_ref[0].astype(F32)
        for i in range(1, N_DEV):
            g = g + p_ref[i].astype(F32)
        mn = ADAM_B1 * m_ref[...] + (1.0 - ADAM_B1) * g
        vn = ADAM_B2 * v_ref[...] + (1.0 - ADAM_B2) * (g * g)
        g_o[...] = g
        m_o[...] = mn
        v_o[...] = vn
        d_o[...] = -ADAM_LR * ((mn * c1) / (jnp.sqrt(vn * c2) + ADAM_EPS) + ADAM_WD * w_ref[...])

    blk = pl.BlockSpec((tr, c), lambda i: (i, 0))
    return _run(
        body, [parts, w, m, v], carry=carry, name=name, grid=(r // tr,),
        in_specs=[pl.BlockSpec((N_DEV, tr, c), lambda i: (0, i, 0)), blk, blk, blk],
        out_specs=[blk] * 4, out_shape=[jax.ShapeDtypeStruct((r, c), F32)] * 4, compiler_params=_cparams(("parallel",)),
    )


PACK_C = 1024
SHARDED = ("w_in", "w_attn_up", "w_glu_v", "w_glu_g", "w_out", "w_ffn_gate", "w_ffn_up", "w_ffn_down")
ROW_SHARDED = ("w_out", "w_ffn_down")
SENT_TRANSPOSED = ("w_in", "w_ffn_gate", "w_ffn_up")
GRAD_TRANSPOSED = ("w_ffn_gate", "w_ffn_up")
SMALL = ("norm_mix_pre", "ssm_a_re", "ssm_a_im", "ssm_log_dt", "ssm_b_re", "ssm_b_im", "ssm_c_re", "ssm_c_im", "ssm_d",
         "norm_mix_post", "norm_ffn_pre", "norm_ffn_post")
WEIGHTS = ("norm_mix_pre", "w_in", "w_attn_up", "ssm_a_re", "ssm_a_im", "ssm_log_dt", "ssm_b_re", "ssm_b_im", "ssm_c_re",
           "ssm_c_im", "ssm_d", "w_glu_v", "w_glu_g", "w_out", "norm_mix_post", "norm_ffn_pre", "w_ffn_gate", "w_ffn_up",
           "w_ffn_down", "norm_ffn_post")


def _pack(arrs, dtype, pad_rows_to=64):
    flat = jnp.concatenate([a.reshape(-1).astype(dtype) for a in arrs])
    n = flat.shape[0]
    rows = -(-n // PACK_C)
    rows = -(-rows // pad_rows_to) * pad_rows_to
    return jnp.pad(flat, (0, rows * PACK_C - n)).reshape(rows, PACK_C)


def _unpack(flat2d, shapes):
    flat = flat2d.reshape(-1)
    out, off = [], 0
    for shp in shapes:
        n = int(np.prod(shp))
        out.append(flat[off:off + n].reshape(shp))
        off += n
    return out


def _full_from_gathered(gathered, name):
    if name in ROW_SHARDED or name in SENT_TRANSPOSED:
        return gathered.reshape(-1, gathered.shape[2])
    return gathered.transpose(1, 0, 2).reshape(gathered.shape[1], -1)


def _split_for_devices(full, name):
    if name in ROW_SHARDED or name in GRAD_TRANSPOSED:
        return full.reshape(N_DEV, -1, full.shape[1])
    return full.reshape(full.shape[0], N_DEV, -1).transpose(1, 0, 2)


def kernel(x, norm_mix_pre, w_in, w_attn_up, ssm_a_re, ssm_a_im, ssm_log_dt, ssm_b_re, ssm_b_im, ssm_c_re, ssm_c_im, ssm_d, w_glu_v, w_glu_g, w_out, norm_mix_post, norm_ffn_pre, w_ffn_gate, w_ffn_up, w_ffn_down, norm_ffn_post, loss_target, m_norm_mix_pre, m_w_in, m_w_attn_up, m_ssm_a_re, m_ssm_a_im, m_ssm_log_dt, m_ssm_b_re, m_ssm_b_im, m_ssm_c_re, m_ssm_c_im, m_ssm_d, m_w_glu_v, m_w_glu_g, m_w_out, m_norm_mix_post, m_norm_ffn_pre, m_w_ffn_gate, m_w_ffn_up, m_w_ffn_down, m_norm_ffn_post, v_norm_mix_pre, v_w_in, v_w_attn_up, v_ssm_a_re, v_ssm_a_im, v_ssm_log_dt, v_ssm_b_re, v_ssm_b_im, v_ssm_c_re, v_ssm_c_im, v_ssm_d, v_w_glu_v, v_w_glu_g, v_w_out, v_norm_mix_post, v_norm_ffn_pre, v_w_ffn_gate, v_w_ffn_up, v_w_ffn_down, v_norm_ffn_post):
    args = dict(locals())
    wv = {n: args[n][0] for n in WEIGHTS}
    mv = {n: args["m_" + n][0] for n in WEIGHTS}
    vv = {n: args["v_" + n][0] for n in WEIGHTS}

    shards = {n: (wv[n].T if n in SENT_TRANSPOSED else wv[n]).astype(BF16) for n in SHARDED}
    small = {n: wv[n] for n in SMALL}
    loss_part, grad_x, recv, dsmall = local_step(x[0], loss_target[0], shards, small)
    for n in GRAD_TRANSPOSED:
        recv[n] = recv[n].transpose(0, 2, 1)

    small_shapes = [wv[n].shape for n in SMALL]
    res = {}
    res["w_in"], (sgather,) = adamw(recv["w_in"], wv["w_in"], mv["w_in"], vv["w_in"], "adamw_w_in",
                                    carry=Gather([_pack([dsmall[n] for n in SMALL], F32)]))
    for n in SHARDED[1:]:
        res[n] = adamw(recv[n], wv[n], mv[n], vv[n], "adamw_" + n)
    sres = adamw(sgather, _pack([wv[n] for n in SMALL], F32), _pack([mv[n] for n in SMALL], F32),
                 _pack([vv[n] for n in SMALL], F32), "adamw_small")
    sun = [_unpack(t, small_shapes) for t in sres]
    for k, n in enumerate(SMALL):
        res[n] = tuple(sun[t][k] for t in range(4))

    loss = lax.psum(loss_part[0, 0], ("x", "y", "c"))
    outs = [loss, grad_x[None]]
    for t in range(4):
        outs += [res[n][t][None] for n in WEIGHTS]
    return tuple(outs)
```

```python
import functools
import math

import numpy as np
import jax
import jax.numpy as jnp
from jax import lax
from jax.experimental import pallas as pl
from jax.experimental.pallas import tpu as pltpu

F32 = jnp.float32
BF16 = jnp.bfloat16

D_MODEL = 2048
HEAD_DIM = 128
HEADS_PER_GROUP = 4
ATTN_GROUPS = ((128, 1), (512, 4), (2048, 16))
N_HEADS = HEADS_PER_GROUP * len(ATTN_GROUPS)
GROUP_W = HEADS_PER_GROUP * HEAD_DIM
HQ = N_HEADS * HEAD_DIM
SSM_W = 1024
SSM_GROUP = 16
SSM_GROUPS = 64
SSM_STATE = 64
STATE_W = SSM_GROUPS * SSM_STATE
D_FF = 5632
EPS = 1e-6
N_DEV = 8
SEGS = 8
BD = 8

ADAM_LR, ADAM_B1, ADAM_B2, ADAM_EPS, ADAM_WD, ADAM_STEP = 0.001, 0.9, 0.999, 1e-08, 0.01, 10

VMEM_LIMIT = 56 * 1024 * 1024
HBM_SPEC = pl.BlockSpec(memory_space=pltpu.HBM)
MESH_ID = pl.DeviceIdType.MESH
NEG = -1e30


def _pcall(body, **kw):
    return pl.pallas_call(body, **kw)


def _cparams(sem=None):
    if sem is None:
        return pltpu.CompilerParams(vmem_limit_bytes=VMEM_LIMIT)
    return pltpu.CompilerParams(vmem_limit_bytes=VMEM_LIMIT, dimension_semantics=sem)


def _my_coords():
    return lax.axis_index("x"), lax.axis_index("y"), lax.axis_index("c")


class Gather:
    def __init__(self, xs, pass_early=False):
        self.arrays = list(xs)
        self.out_shapes = [jax.ShapeDtypeStruct((N_DEV,) + x.shape, x.dtype) for x in xs]
        self.pass_early = pass_early

    def _ctx(self, out_refs, send_sems, recv_sems):
        mx, my, mc = _my_coords()
        me, sibling = (mx, my, mc), (mx, my, 1 - mc)
        chips = [(1 - mx, my), (mx, 1 - my), (1 - mx, 1 - my)]

        def slot(a, px, py, pc):
            return out_refs[a].at[4 * px + 2 * py + pc]

        def copy(a, k, block, to, src=None):
            return pltpu.make_async_remote_copy(
                src_ref=slot(a, *block) if src is None else src, dst_ref=slot(a, *block),
                send_sem=send_sems.at[7 * a + k], recv_sem=recv_sems.at[7 * a + k], device_id=to, device_id_type=MESH_ID)

        return me, sibling, chips, mc, slot, copy

    def _first(self, a, x_refs, ctx):
        me, sibling, chips, mc, slot, copy = ctx
        return [copy(a, 0, me, sibling, src=x_refs[a])] + [copy(a, 1 + j, me, (*chip, mc), src=x_refs[a]) for j, chip in enumerate(chips)]

    def start(self, x_refs, out_refs, send_sems, recv_sems, local_sems):
        ctx = self._ctx(out_refs, send_sems, recv_sems)
        me, slot = ctx[0], ctx[4]
        for a in range(len(self.arrays)):
            pltpu.make_async_copy(x_refs[a], slot(a, *me), local_sems.at[a]).start()
            for cp in self._first(a, x_refs, ctx):
                cp.start()

    def _passed(self, ctx):
        me, sibling, chips, mc, slot, copy = ctx
        return [copy(a, 4 + j, (*chip, mc), sibling) for a in range(len(self.arrays)) for j, chip in enumerate(chips)]

    def middle(self, x_refs, out_refs, send_sems, recv_sems, local_sems):
        ctx = self._ctx(out_refs, send_sems, recv_sems)
        me, sibling, chips, mc, slot, copy = ctx
        for a in range(len(self.arrays)):
            for j, chip in enumerate(chips):
                copy(a, 1 + j, (*chip, mc), me).wait_recv()
                copy(a, 4 + j, (*chip, mc), sibling).start()

    def finish(self, x_refs, out_refs, send_sems, recv_sems, local_sems, passed_on=False):
        if not passed_on:
            self.middle(x_refs, out_refs, send_sems, recv_sems, local_sems)
        ctx = self._ctx(out_refs, send_sems, recv_sems)
        me, sibling, chips, mc, slot, copy = ctx
        na = len(self.arrays)
        passed = self._passed(ctx)
        for a in range(na):
            copy(a, 0, sibling, me).wait_recv()
            for j, chip in enumerate(chips):
                copy(a, 4 + j, (*chip, 1 - mc), me).wait_recv()
        for a in range(na):
            for cp in self._first(a, x_refs, ctx):
                cp.wait_send()
        for cp in passed:
            cp.wait_send()
        for a in range(na):
            pltpu.make_async_copy(x_refs[a], slot(a, *me), local_sems.at[a]).wait()


class AllToAll:
    def __init__(self, ps):
        self.arrays = list(ps)
        self.out_shapes = [jax.ShapeDtypeStruct(p.shape, p.dtype) for p in ps]

    def _copies(self, p_refs, out_refs, send_sems, recv_sems, local_sems):
        mx, my, mc = _my_coords()
        me = 4 * mx + 2 * my + mc
        local, remote = [], []
        for a in range(len(self.arrays)):
            local.append(pltpu.make_async_copy(p_refs[a].at[me], out_refs[a].at[me], local_sems.at[a]))
            for k in range(1, N_DEV):
                px, py, pc = mx ^ ((k >> 2) & 1), my ^ ((k >> 1) & 1), mc ^ (k & 1)
                remote.append(pltpu.make_async_remote_copy(
                    src_ref=p_refs[a].at[4 * px + 2 * py + pc], dst_ref=out_refs[a].at[me],
                    send_sem=send_sems.at[7 * a + k - 1], recv_sem=recv_sems.at[7 * a + k - 1],
                    device_id=(px, py, pc), device_id_type=MESH_ID))
        return local, remote

    def start(self, *refs):
        local, remote = self._copies(*refs)
        for cp in local + remote:
            cp.start()

    def finish(self, *refs):
        local, remote = self._copies(*refs)
        for cp in remote:
            cp.wait_recv()
        for cp in remote:
            cp.wait_send()
        for cp in local:
            cp.wait()


class RowsToOwners:
    def __init__(self, p, r0, n, into=None):
        self.arrays = [p] if into is None else [p, into]
        self.out_shapes = [jax.ShapeDtypeStruct(p.shape, p.dtype)]
        self.aliases = {} if into is None else {1: 0}
        self.rows = (r0, n)

    def _copies(self, p_refs, out_refs, send_sems, recv_sems, local_sems):
        mx, my, mc = _my_coords()
        me = 4 * mx + 2 * my + mc
        rows = pl.ds(*self.rows)
        local = [pltpu.make_async_copy(p_refs[0].at[me, rows], out_refs[0].at[me, rows], local_sems.at[0])]
        remote = []
        for k in range(1, N_DEV):
            px, py, pc = mx ^ ((k >> 2) & 1), my ^ ((k >> 1) & 1), mc ^ (k & 1)
            remote.append(pltpu.make_async_remote_copy(
                src_ref=p_refs[0].at[4 * px + 2 * py + pc, rows], dst_ref=out_refs[0].at[me, rows],
                send_sem=send_sems.at[k - 1], recv_sem=recv_sems.at[k - 1], device_id=(px, py, pc), device_id_type=MESH_ID))
        return local, remote

    start = AllToAll.start
    finish = AllToAll.finish


def _run(body, args, carry=None, **kw):
    if carry is None:
        return _pcall(body, **kw)(*args)
    grid = kw["grid"]
    single = not isinstance(kw["out_shape"], (list, tuple))
    in_specs = list(kw["in_specs"])
    out_specs = [kw["out_specs"]] if single else list(kw["out_specs"])
    out_shape = [kw["out_shape"]] if single else list(kw["out_shape"])
    scratch = list(kw.get("scratch_shapes", []))
    na, nin, nout, nscr = len(carry.arrays), len(in_specs), len(out_specs), len(scratch)
    nco = len(carry.out_shapes)
    aliases = {nin + i: nout + o for i, o in getattr(carry, "aliases", {}).items()}
    steps = int(np.prod(grid))
    mid_step = (steps * 7) // 10 if getattr(carry, "pass_early", False) and steps >= 4 else None

    def carried(*refs):
        ins, cin = refs[:nin], refs[nin:nin + na]
        outs, cout = refs[nin + na:nin + na + nout], refs[nin + na + nout:nin + na + nout + nco]
        scr = refs[nin + na + nout + nco:nin + na + nout + nco + nscr]
        sems = refs[nin + na + nout + nco + nscr:]
        step = pl.program_id(0)
        for i in range(1, len(grid)):
            step = step * grid[i] + pl.program_id(i)

        @pl.when(step == 0)
        def _():
            carry.start(cin, cout, *sems)

        if mid_step is not None:
            @pl.when(step == mid_step)
            def _():
                carry.middle(cin, cout, *sems)

        body(*ins, *outs, *scr)

        @pl.when(step == steps - 1)
        def _():
            if mid_step is not None:
                carry.finish(cin, cout, *sems, passed_on=True)
            else:
                carry.finish(cin, cout, *sems)

    res = _pcall(
        carried, name=kw["name"], grid=grid, in_specs=in_specs + [HBM_SPEC] * na, out_specs=out_specs + [HBM_SPEC] * nco,
        out_shape=out_shape + carry.out_shapes, input_output_aliases=aliases,
        scratch_shapes=scratch + [pltpu.SemaphoreType.DMA((7 * na,)), pltpu.SemaphoreType.DMA((7 * na,)), pltpu.SemaphoreType.DMA((na,))],
        compiler_params=_cparams(("arbitrary",) * len(grid)),
    )(*args, *carry.arrays)
    main = res[:nout]
    return (main[0] if single else main), list(res[nout:])


_DN = {"nn": (((1,), (0,)), ((), ())), "nt": (((1,), (1,)), ((), ())), "tn": (((0,), (0,)), ((), ()))}


LANE = 128
MM_TM, MM_TN, MM_TK = 1024, 1536, 2048


def _tile(n, cap):
    for t in range(min(cap, n) // LANE * LANE, 0, -LANE):
        if n % t == 0:
            return t
    raise ValueError(n)


DW_TM, DW_TN, DW_TK = 512, 512, 8192


def mm(pairs, mode, out_dtype, name, tm=None, tn=None, tk=None, carry=None, epilogue=None, extras=(), b_window=None):
    a0, b0 = pairs[0]
    if mode == "nn":
        (m, k), n = a0.shape, b0.shape[1]
    elif mode == "nt":
        (m, k), n = a0.shape, b0.shape[0]
    else:
        (k, m), n = a0.shape, b0.shape[1]
    if b_window is not None:
        assert mode in ("nn", "nt") and len(pairs) == 1
        if mode == "nt":
            n = b_window[0]
        else:
            assert k == b_window[0]
    caps = (DW_TM, DW_TN, DW_TK) if mode == "tn" else (MM_TM, MM_TN, MM_TK)
    tm, tn, tk = _tile(m, tm or caps[0]), _tile(n, tn or caps[1]), _tile(k, tk or caps[2])
    nk = k // tk
    npairs = len(pairs)
    nex = len(extras)
    fused = epilogue is not None
    assert not fused or nk == 1
    out_dtypes = list(out_dtype) if fused else [out_dtype]

    def body(*refs):
        prods = []
        for p in range(npairs):
            a = refs[2 * p][...].astype(BF16) if (p == 0 or pairs[p][0] is not pairs[p - 1][0]) else a
            b = refs[2 * p + 1][...].astype(BF16)
            prods.append(lax.dot_general(a, b, _DN[mode], preferred_element_type=F32))
        if fused:
            ex = [refs[2 * npairs + e][...].astype(F32) for e in range(nex)]
            for o_ref, val in zip(refs[2 * npairs + nex:], epilogue(prods, ex)):
                o_ref[...] = val.astype(o_ref.dtype)
            return
        o_ref = refs[2 * npairs]
        tot = prods[0]
        for d in prods[1:]:
            tot = tot + d
        if nk == 1:
            o_ref[...] = tot.astype(o_ref.dtype)
            return
        acc = refs[2 * npairs + 1]
        kk = pl.program_id(2)

        @pl.when(kk == 0)
        def _():
            acc[...] = tot

        @pl.when(kk > 0)
        def _():
            acc[...] += tot

        @pl.when(kk == nk - 1)
        def _():
            o_ref[...] = acc[...].astype(o_ref.dtype)

    rows_of = b_window[1] if b_window is not None else (lambda t: t)
    if mode == "nn":
        sp = [pl.BlockSpec((tm, tk), lambda i, j, kk: (i, kk)), pl.BlockSpec((tk, tn), lambda i, j, kk: (rows_of(kk), j))]
    elif mode == "nt":
        sp = [pl.BlockSpec((tm, tk), lambda i, j, kk: (i, kk)), pl.BlockSpec((tn, tk), lambda i, j, kk: (rows_of(j), kk))]
    else:
        sp = [pl.BlockSpec((tk, tm), lambda i, j, kk: (kk, i)), pl.BlockSpec((tk, tn), lambda i, j, kk: (kk, j))]
    o_spec = pl.BlockSpec((tm, tn), lambda i, j, kk: (i, j))
    out_shapes = [jax.ShapeDtypeStruct((m, n), dt) for dt in out_dtypes]
    return _run(
        body, [t for pr in pairs for t in pr] + list(extras), carry=carry, name=name, grid=(m // tm, n // tn, nk),
        in_specs=sp * npairs + [o_spec] * nex,
        out_specs=[o_spec] * len(out_shapes) if fused else o_spec,
        out_shape=out_shapes if fused else out_shapes[0],
        scratch_shapes=[pltpu.VMEM((tm, tn), F32)] if nk > 1 else [],
        compiler_params=_cparams(("parallel", "parallel", "arbitrary")),
    )


def rowwise(name, fn, row_ins, const_ins, row_outs, acc_outs=(), ts=None, carry=None):
    s = row_ins[0].shape[0]
    row_outs = [ro if len(ro) == 3 else (*ro, 1) for ro in row_outs]
    if ts is None:
        per_row = sum(a.shape[-1] * a.dtype.itemsize for a in row_ins) + sum(w * jnp.dtype(dt).itemsize for w, dt, _ in row_outs)
        ts = 512
        while ts > 8 and 2 * ts * per_row > 20 * 1024 * 1024:
            ts //= 2
    ts = min(ts, s)
    assert s % ts == 0
    nr, nc, no, na = len(row_ins), len(const_ins), len(row_outs), len(acc_outs)

    def body(*refs):
        rows = [r[...].reshape(ts, r.shape[-1]).astype(F32) for r in refs[:nr]]
        consts = [r[...] for r in refs[nr:nr + nc]]
        outs, accs = fn(rows, consts)
        for r, v in zip(refs[nr + nc:nr + nc + no], outs):
            r[...] = v.astype(r.dtype).reshape(r.shape)
        if na:
            first = pl.program_id(0) == 0
            for r, v in zip(refs[nr + nc + no:], accs):
                @pl.when(first)
                def _(r=r, v=v):
                    r[...] = v

                @pl.when(jnp.logical_not(first))
                def _(r=r, v=v):
                    r[...] += v

    def tile_spec(w, d):
        if d == 1:
            return pl.BlockSpec((ts, w), lambda i: (i, 0))
        return pl.BlockSpec((d, ts // d, w), lambda i: (0, i, 0))

    in_specs = [tile_spec(a.shape[-1], a.shape[0] if a.ndim == 3 else 1) for a in row_ins]
    in_specs += [pl.BlockSpec(c.shape, lambda i, nd=c.ndim: (0,) * nd) for c in const_ins]
    out_specs = [tile_spec(w, d) for w, _, d in row_outs]
    out_specs += [pl.BlockSpec(shp, lambda i, nd=len(shp): (0,) * nd) for shp in acc_outs]
    out_shape = [jax.ShapeDtypeStruct((s, w) if d == 1 else (d, s // d, w), dt) for w, dt, d in row_outs]
    out_shape += [jax.ShapeDtypeStruct(shp, F32) for shp in acc_outs]
    return _run(
        body, [*row_ins, *const_ins], carry=carry, name=name, grid=(s // ts,), in_specs=in_specs, out_specs=out_specs,
        out_shape=out_shape, compiler_params=_cparams(("arbitrary",)),
    )


PERM_TS = 256


def _perm_matrix(ts, d, inverse):
    i = lax.broadcasted_iota(jnp.int32, (ts, ts), 0)
    k = lax.broadcasted_iota(jnp.int32, (ts, ts), 1)
    per = ts // d
    src = (i % d) * per + i // d if inverse else (i % per) * d + i // per
    return jnp.where(k == src, 1.0, 0.0).astype(BF16)


def _permute(p, x):
    if x.dtype == BF16:
        return jnp.dot(p, x, preferred_element_type=F32)
    hi = x.astype(BF16)
    rest = x - hi.astype(F32)
    mid = rest.astype(BF16)
    lo = (rest - mid.astype(F32)).astype(BF16)
    out = jnp.dot(p, hi, preferred_element_type=F32) + jnp.dot(p, mid, preferred_element_type=F32)
    return out + jnp.dot(p, lo, preferred_element_type=F32)


def _rms(x, gain):
    r = lax.rsqrt(jnp.mean(x * x, axis=-1, keepdims=True) + EPS)
    n = x * r
    return n * gain, n, r


def _rms_bwd(dy, n, r, gain):
    dn = dy * gain
    dx = r * (dn - n * jnp.mean(dn * n, axis=-1, keepdims=True))
    return dx, jnp.sum(dy * n, axis=0, keepdims=True)


def _sigmoid(x):
    return 1.0 / (1.0 + jnp.exp(-x))


_GELU_K = math.sqrt(2.0 / math.pi)


def _gelu(x):
    t = jnp.tanh(_GELU_K * (x + 0.044715 * x * x * x))
    return 0.5 * x * (1.0 + t), t


def _gelu_grad(x, t):
    return 0.5 * (1.0 + t) + 0.5 * x * (1.0 - t * t) * _GELU_K * (1.0 + 3.0 * 0.044715 * x * x)


def _head_sum(x):
    parts = []
    for h in range(HEADS_PER_GROUP):
        sl = x[:, h * HEAD_DIM:(h + 1) * HEAD_DIM]
        parts.append(jnp.broadcast_to(jnp.sum(sl, axis=-1, keepdims=True), sl.shape))
    return jnp.concatenate(parts, axis=-1)


def _mix_weights(l0, l1, l2):
    mx = jnp.maximum(jnp.maximum(l0, l1), l2)
    e0, e1, e2 = jnp.exp(l0 - mx), jnp.exp(l1 - mx), jnp.exp(l2 - mx)
    inv = 1.0 / (e0 + e1 + e2)
    return e0 * inv, e1 * inv, e2 * inv


BLK = 128


def _slopes(g):
    return [2.0 ** (-8.0 * (g * HEADS_PER_GROUP + h + 1) / N_HEADS) for h in range(HEADS_PER_GROUP)]


def _attn_masks(dil):
    qi = lax.broadcasted_iota(jnp.int32, (BLK, BLK), 0)
    ki = lax.broadcasted_iota(jnp.int32, (BLK, BLK), 1)
    dist_c = qi - ki
    dist_p = BLK + qi - ki
    return dist_c >= 0, dist_p <= BLK, (dist_c * dil).astype(F32), (dist_p * dil).astype(F32)


def _window_mask(has_prev, dil):
    qi = lax.broadcasted_iota(jnp.int32, (BLK, 2 * BLK), 0)
    ki = lax.broadcasted_iota(jnp.int32, (BLK, 2 * BLK), 1)
    dist = BLK + qi - ki
    ok = jnp.logical_and(jnp.logical_and(dist >= 0, dist <= BLK), jnp.logical_or(ki >= BLK, has_prev))
    return ok, (dist * dil).astype(F32)


def attn_fwd(qkv, g, name):
    dil, length, _ = qkv.shape
    scale = HEAD_DIM ** -0.5
    slopes = _slopes(g)

    def body(q_ref, kc_ref, vc_ref, kp_ref, vp_ref, o_ref, l_ref):
        ok, dist = _window_mask(pl.program_id(1) > 0, dil)
        for h in range(HEADS_PER_GROUP):
            sl = slice(h * HEAD_DIM, (h + 1) * HEAD_DIM)
            k2 = jnp.concatenate([kp_ref[:, sl], kc_ref[:, sl]], axis=0)
            v2 = jnp.concatenate([vp_ref[:, sl], vc_ref[:, sl]], axis=0)
            s = lax.dot_general(q_ref[:, sl], k2, _DN["nt"], preferred_element_type=F32) * scale - slopes[h] * dist
            s = jnp.where(ok, s, NEG)
            mx = jnp.max(s, axis=-1, keepdims=True)
            p = jnp.exp(s - mx)
            den = jnp.sum(p, axis=-1, keepdims=True)
            o_ref[:, sl] = (jnp.dot(p.astype(BF16), v2, preferred_element_type=F32) / den).astype(BF16)
            l_ref[:, sl] = jnp.broadcast_to(mx + jnp.log(den), (BLK, HEAD_DIM))

    def spec(col, prev):
        if prev:
            return pl.BlockSpec((None, BLK, GROUP_W), lambda r, n: (r, jnp.maximum(n - 1, 0), col))
        return pl.BlockSpec((None, BLK, GROUP_W), lambda r, n: (r, n, col))

    out_spec = pl.BlockSpec((None, BLK, GROUP_W), lambda r, n: (r, n, 0))
    return _pcall(
        body, name=name, grid=(dil, length // BLK),
        in_specs=[spec(0, False), spec(1, False), spec(2, False), spec(1, True), spec(2, True)],
        out_specs=[out_spec, out_spec],
        out_shape=[jax.ShapeDtypeStruct((dil, length, GROUP_W), BF16), jax.ShapeDtypeStruct((dil, length, GROUP_W), F32)],
        compiler_params=_cparams(("parallel", "parallel")),
    )(qkv, qkv, qkv, qkv, qkv)


def attn_bwd(qkv, dout, lse, dd, g, name, carry=None):
    dil, length, _ = qkv.shape
    nblk = length // BLK
    scale = HEAD_DIM ** -0.5
    slopes = _slopes(g)

    def body(q_ref, kc_ref, vc_ref, kp_ref, vp_ref, qn_ref, do_ref, don_ref, l_ref, ln_ref, d_ref, dn_ref, o_ref):
        n = pl.program_id(1)
        ok2, dist2 = _window_mask(n > 0, dil)
        _, ok_p, _, dp = _attn_masks(dil)
        ok_next = jnp.logical_and(ok_p, n < nblk - 1)
        for h in range(HEADS_PER_GROUP):
            sl = slice(h * HEAD_DIM, (h + 1) * HEAD_DIM)
            q, kc, vc, qn = q_ref[:, sl], kc_ref[:, sl], vc_ref[:, sl], qn_ref[:, sl]
            k2 = jnp.concatenate([kp_ref[:, sl], kc], axis=0)
            v2 = jnp.concatenate([vp_ref[:, sl], vc], axis=0)
            do, don = do_ref[:, sl], don_ref[:, sl]
            lse_q, lse_n, dd_q, dd_n = l_ref[:, sl], ln_ref[:, sl], d_ref[:, sl], dn_ref[:, sl]

            def probs(qq, kk, dist, ok, lse_t):
                s = lax.dot_general(qq, kk, _DN["nt"], preferred_element_type=F32) * scale - slopes[h] * dist
                return jnp.where(ok, jnp.exp(jnp.where(ok, s, NEG) - lse_t), 0.0)

            p2 = probs(q, k2, dist2, ok2, jnp.concatenate([lse_q, lse_q], axis=1))
            p_x = probs(qn, kc, dp, ok_next, lse_n)
            ds2 = p2 * (lax.dot_general(do, v2, _DN["nt"], preferred_element_type=F32) - jnp.concatenate([dd_q, dd_q], axis=1))
            ds_x = p_x * (lax.dot_general(don, vc, _DN["nt"], preferred_element_type=F32) - dd_n)
            dq = jnp.dot(ds2.astype(BF16), k2, preferred_element_type=F32)
            ds_k = jnp.concatenate([ds2[:, BLK:], ds_x], axis=0).astype(BF16)
            p_k = jnp.concatenate([p2[:, BLK:], p_x], axis=0).astype(BF16)
            dk = lax.dot_general(ds_k, jnp.concatenate([q, qn], axis=0), _DN["tn"], preferred_element_type=F32)
            dv = lax.dot_general(p_k, jnp.concatenate([do, don], axis=0), _DN["tn"], preferred_element_type=F32)
            o_ref[:, h * HEAD_DIM:(h + 1) * HEAD_DIM] = (dq * scale).astype(BF16)
            o_ref[:, GROUP_W + h * HEAD_DIM:GROUP_W + (h + 1) * HEAD_DIM] = (dk * scale).astype(BF16)
            o_ref[:, 2 * GROUP_W + h * HEAD_DIM:2 * GROUP_W + (h + 1) * HEAD_DIM] = dv.astype(BF16)

    def spec(col, which):
        if which == "prev":
            return pl.BlockSpec((None, BLK, GROUP_W), lambda r, n: (r, jnp.maximum(n - 1, 0), col))
        if which == "next":
            return pl.BlockSpec((None, BLK, GROUP_W), lambda r, n: (r, jnp.minimum(n + 1, nblk - 1), col))
        return pl.BlockSpec((None, BLK, GROUP_W), lambda r, n: (r, n, col))

    return _run(
        body, [qkv, qkv, qkv, qkv, qkv, qkv, dout, dout, lse, lse, dd, dd], carry=carry, name=name, grid=(dil, nblk),
        in_specs=[spec(0, "cur"), spec(1, "cur"), spec(2, "cur"), spec(1, "prev"), spec(2, "prev"), spec(0, "next"),
                  spec(0, "cur"), spec(0, "next"), spec(0, "cur"), spec(0, "next"), spec(0, "cur"), spec(0, "next")],
        out_specs=pl.BlockSpec((None, BLK, 3 * GROUP_W), lambda r, n: (r, n, 0)),
        out_shape=jax.ShapeDtypeStruct((dil, length, 3 * GROUP_W), BF16),
        compiler_params=_cparams(("parallel", "parallel")),
    )


def _ssm_prep_values(are, aim, logdt):
    dt = jnp.exp(logdt)
    mag = jnp.exp(are * dt)
    lb_re, lb_im = mag * jnp.cos(aim * dt), mag * jnp.sin(aim * dt)
    inv = 1.0 / (are * are + aim * aim)
    n_re, n_im = lb_re - 1.0, lb_im
    f_re = (n_re * are + n_im * aim) * inv
    f_im = (n_im * are - n_re * aim) * inv
    return dt, lb_re, lb_im, f_re, f_im, inv


PREP_G = 8


def _group_specs(are, logdt, bre):
    def spec(a):
        return pl.BlockSpec((PREP_G,) + a.shape[1:], lambda i: (i, 0, 0))
    return spec(are), spec(logdt), spec(bre)


def ssm_prep(are, aim, logdt, bre, bim):
    def body(are_r, aim_r, ldt_r, bre_r, bim_r, lre_o, lim_o, bbre_o, bbim_o):
        _, lb_re, lb_im, f_re, f_im, _ = _ssm_prep_values(are_r[...], aim_r[...], ldt_r[...])
        lre_o[...] = lb_re
        lim_o[...] = lb_im
        bbre_o[...] = f_re * bre_r[...] - f_im * bim_r[...]
        bbim_o[...] = f_re * bim_r[...] + f_im * bre_r[...]

    sh1 = jax.ShapeDtypeStruct(are.shape, F32)
    shb = jax.ShapeDtypeStruct(bre.shape, F32)
    s1, sd, sb = _group_specs(are, logdt, bre)
    return _pcall(body, name="ssm_prep", grid=(SSM_GROUPS // PREP_G,), in_specs=[s1, s1, sd, sb, sb], out_specs=[s1, s1, sb, sb],
                  out_shape=[sh1, sh1, shb, shb], compiler_params=_cparams(("parallel",)))(are, aim, logdt, bre, bim)


def ssm_prep_bwd(are, aim, logdt, bre, bim, dbbre, dbbim, dlre, dlim):
    def body(are_r, aim_r, ldt_r, bre_r, bim_r, dbbre_r, dbbim_r, dlre_r, dlim_r, dare_o, daim_o, dldt_o, dbre_o, dbim_o):
        are_v, aim_v = are_r[...], aim_r[...]
        dt, lb_re, lb_im, f_re, f_im, inv = _ssm_prep_values(are_v, aim_v, ldt_r[...])
        b_re, b_im, g_re, g_im = bre_r[...], bim_r[...], dbbre_r[...], dbbim_r[...]
        dbre_o[...] = f_re * g_re + f_im * g_im
        dbim_o[...] = f_re * g_im - f_im * g_re
        df_re = jnp.sum(b_re * g_re + b_im * g_im, axis=-1, keepdims=True)
        df_im = jnp.sum(b_re * g_im - b_im * g_re, axis=-1, keepdims=True)
        il_re, il_im = are_v * inv, -aim_v * inv
        cl_re = dlre_r[...] + il_re * df_re + il_im * df_im
        cl_im = dlim_r[...] + il_re * df_im - il_im * df_re
        q_re = -(f_re * il_re - f_im * il_im)
        q_im = -(f_re * il_im + f_im * il_re)
        ca_re = q_re * df_re + q_im * df_im
        ca_im = q_re * df_im - q_im * df_re
        cz_re = lb_re * cl_re + lb_im * cl_im
        cz_im = lb_re * cl_im - lb_im * cl_re
        dare_o[...] = ca_re + dt * cz_re
        daim_o[...] = ca_im + dt * cz_im
        dldt_o[...] = dt * jnp.sum(are_v * cz_re + aim_v * cz_im, axis=1, keepdims=True)

    sh1 = jax.ShapeDtypeStruct(are.shape, F32)
    shb = jax.ShapeDtypeStruct(bre.shape, F32)
    s1, sd, sb = _group_specs(are, logdt, bre)
    return _pcall(
        body, name="ssm_prep_bwd", grid=(SSM_GROUPS // PREP_G,), in_specs=[s1, s1, sd, sb, sb, sb, sb, s1, s1],
        out_specs=[s1, s1, sd, sb, sb], out_shape=[sh1, sh1, jax.ShapeDtypeStruct(logdt.shape, F32), shb, shb],
        compiler_params=_cparams(("parallel",)),
    )(are, aim, logdt, bre, bim, dbbre, dbbim, dlre, dlim)


SCAN_WC = 512


def _chain_segments(a_re, a_im, e_re, e_im, nsq, reverse):
    p_re, p_im = a_re, a_im
    for _ in range(nsq):
        p_re, p_im = p_re * p_re - p_im * p_im, 2.0 * p_re * p_im
    row = lax.broadcasted_iota(jnp.int32, e_re.shape, 0)
    edge = (row == SEGS - 1) if reverse else (row == 0)
    shift = SEGS - 1 if reverse else 1
    c_re, c_im = jnp.zeros_like(e_re), jnp.zeros_like(e_im)
    for _ in range(SEGS - 1):
        n_re = p_re * c_re - p_im * c_im + e_re
        n_im = p_re * c_im + p_im * c_re + e_im
        c_re = jnp.where(edge, 0.0, pltpu.roll(n_re, shift, 0))
        c_im = jnp.where(edge, 0.0, pltpu.roll(n_im, shift, 0))
    return c_re, c_im


def _scan_dims(s):
    steps = s // SEGS
    assert steps & (steps - 1) == 0
    tt = min(128, steps)
    return steps, tt, steps // tt, tt * SEGS, int(math.log2(steps))


U_BLK = SSM_W // BD


def ssm_fwd(u_s, dvec, w_bre, w_bim, w_cre, w_cim_neg, lre, lim, name, carry=None):
    s = u_s.shape[0]
    steps, tt, nch, rows, nsq = _scan_dims(s)
    nb, ub_w, wc = w_bre.shape

    def body(u_r, d_r, bre_r, bim_r, cre_r, cim_r, lre_r, lim_r, yg_o, ys_o, hre_o, him_o, hin_re_o, hin_im_o,
             st_re, st_im, x_re, x_im, h_re, h_im):
        ps, ch = pl.program_id(1), pl.program_id(2)
        a_re = jnp.broadcast_to(lre_r[...], (SEGS, wc))
        a_im = jnp.broadcast_to(lim_r[...], (SEGS, wc))
        ub = u_r[...]
        ub16 = ub.astype(BF16)
        x_re[...] = jnp.dot(ub16, bre_r[...], preferred_element_type=F32)
        x_im[...] = jnp.dot(ub16, bim_r[...], preferred_element_type=F32)

        @pl.when(jnp.logical_and(ps == 0, ch == 0))
        def _():
            st_re[...] = jnp.zeros_like(st_re)
            st_im[...] = jnp.zeros_like(st_im)

        @pl.when(jnp.logical_and(ps == 1, ch == 0))
        def _():
            c_re, c_im = _chain_segments(a_re, a_im, st_re[...], st_im[...], nsq, False)
            st_re[...] = c_re
            st_im[...] = c_im
            hin_re_o[...] = c_re
            hin_im_o[...] = c_im

        def run(store):
            def step(t, hc):
                off = pl.multiple_of(t * SEGS, SEGS)
                n_re = a_re * hc[0] - a_im * hc[1] + x_re[pl.ds(off, SEGS), :]
                n_im = a_re * hc[1] + a_im * hc[0] + x_im[pl.ds(off, SEGS), :]
                if store:
                    h_re[pl.ds(off, SEGS), :] = n_re
                    h_im[pl.ds(off, SEGS), :] = n_im
                return n_re, n_im

            fin = lax.fori_loop(0, tt, step, (st_re[...], st_im[...]))
            st_re[...] = fin[0]
            st_im[...] = fin[1]

        @pl.when(ps == 0)
        def _():
            run(False)

        @pl.when(ps == 1)
        def _():
            run(True)
            hr16, hi16 = h_re[...].astype(BF16), h_im[...].astype(BF16)
            hre_o[...] = hr16
            him_o[...] = hi16
            y = jnp.dot(hr16, cre_r[...], preferred_element_type=F32) + jnp.dot(hi16, cim_r[...], preferred_element_type=F32)
            y = y + d_r[...] * ub
            ys_o[...] = y
            yg_o[...] = _gelu(y)[0].astype(BF16)

    def pass1(ps, c):
        return jnp.where(ps == 1, c, 0)

    u_spec = pl.BlockSpec((rows, ub_w), lambda j, ps, c: (c, j))
    d_spec = pl.BlockSpec((1, ub_w), lambda j, ps, c: (0, j))
    b_spec = pl.BlockSpec((None, ub_w, wc), lambda j, ps, c: (j, 0, 0))
    c_spec = pl.BlockSpec((None, wc, ub_w), lambda j, ps, c: (j, 0, 0))
    l_spec = pl.BlockSpec((1, wc), lambda j, ps, c: (0, j))
    y_spec = pl.BlockSpec((rows, ub_w), lambda j, ps, c: (pass1(ps, c), j))
    h_spec = pl.BlockSpec((rows, wc), lambda j, ps, c: (pass1(ps, c), j))
    e_spec = pl.BlockSpec((SEGS, wc), lambda j, ps, c: (0, j))
    return _run(
        body, [u_s, dvec, w_bre, w_bim, w_cre, w_cim_neg, lre, lim], carry=carry, name=name, grid=(nb, 2, nch),
        in_specs=[u_spec, d_spec, b_spec, b_spec, c_spec, c_spec, l_spec, l_spec],
        out_specs=[y_spec, y_spec, h_spec, h_spec, e_spec, e_spec],
        out_shape=[jax.ShapeDtypeStruct((s, SSM_W), BF16), jax.ShapeDtypeStruct((s, SSM_W), F32),
                   jax.ShapeDtypeStruct((s, STATE_W), BF16), jax.ShapeDtypeStruct((s, STATE_W), BF16),
                   jax.ShapeDtypeStruct((SEGS, STATE_W), F32), jax.ShapeDtypeStruct((SEGS, STATE_W), F32)],
        scratch_shapes=[pltpu.VMEM((SEGS, wc), F32)] * 2 + [pltpu.VMEM((rows, wc), F32)] * 4,
        compiler_params=_cparams(("parallel", "arbitrary", "arbitrary")),
    )


def ssm_bwd(dyg_s, ys, u_s, h_re, h_im, hin_re, hin_im, gin_re, gin_im, dvec, w_bre_t, w_bim_t, w_cre_t, w_cim_neg_t, lre, lim,
            name, carry=None):
    s = u_s.shape[0]
    steps, tt, nch, rows, nsq = _scan_dims(s)
    half = 2 * SEGS

    def body(dyg_r, ys_r, u_r, hre_r, him_r, pre_r, pim_r, cin_re_r, cin_im_r, gin_re_r, gin_im_r, d_r, bre_r, bim_r, cre_r,
             cim_r, lre_r, lim_r, du_o, dbre_o, dbim_o, dcre_o, dcim_o, dlre_o, dlim_o, dd_o,
             st_re, st_im, x_re, x_im, g_re, g_im, hf_re, hf_im):
        ch = pl.program_id(1)
        a_re = jnp.broadcast_to(lre_r[...], (SEGS, SCAN_WC))
        a_im = -jnp.broadcast_to(lim_r[...], (SEGS, SCAN_WC))
        ub, y = u_r[...], ys_r[...]
        dy = dyg_r[...] * _gelu_grad(y, _gelu(y)[1])
        dy16 = dy.astype(BF16)
        x_re[...] = jnp.dot(dy16, cre_r[...], preferred_element_type=F32)
        x_im[...] = jnp.dot(dy16, cim_r[...], preferred_element_type=F32)

        @pl.when(ch == 0)
        def _():
            st_re[...] = gin_re_r[...]
            st_im[...] = gin_im_r[...]
            dlre_o[...] = jnp.zeros_like(dlre_o)
            dlim_o[...] = jnp.zeros_like(dlim_o)

        hf_re[...] = hre_r[...].astype(F32)
        hf_im[...] = him_r[...].astype(F32)
        first_chunk = ch == nch - 1
        edge_re = jnp.where(first_chunk, cin_re_r[...], pre_r[...].astype(F32)[SEGS:, :])
        edge_im = jnp.where(first_chunk, cin_im_r[...], pim_r[...].astype(F32)[SEGS:, :])

        def step(i, hc):
            t = tt - 1 - i
            off = pl.multiple_of(t * SEGS, SEGS)
            n_re = a_re * hc[0] - a_im * hc[1] + x_re[pl.ds(off, SEGS), :]
            n_im = a_re * hc[1] + a_im * hc[0] + x_im[pl.ds(off, SEGS), :]
            g_re[pl.ds(off, SEGS), :] = n_re
            g_im[pl.ds(off, SEGS), :] = n_im
            offp = pl.multiple_of(jnp.maximum(t - 1, 0) * SEGS, SEGS)
            hp_re = jnp.where(t == 0, edge_re, hf_re[pl.ds(offp, SEGS), :])
            hp_im = jnp.where(t == 0, edge_im, hf_im[pl.ds(offp, SEGS), :])
            return n_re, n_im, hc[2] + hp_re * n_re + hp_im * n_im, hc[3] + hp_re * n_im - hp_im * n_re

        fin = lax.fori_loop(0, tt, step, (st_re[...], st_im[...], dlre_o[...], dlim_o[...]))
        st_re[...] = fin[0]
        st_im[...] = fin[1]
        dlre_o[...] = fin[2]
        dlim_o[...] = fin[3]

        gr16, gi16 = g_re[...].astype(BF16), g_im[...].astype(BF16)
        du = jnp.dot(gr16, bre_r[...], preferred_element_type=F32) + jnp.dot(gi16, bim_r[...], preferred_element_type=F32)
        du_o[...] = du + d_r[...] * dy
        ub16 = ub.astype(BF16)
        parts = [
            (dbre_o, lax.dot_general(ub16, gr16, _DN["tn"], preferred_element_type=F32)),
            (dbim_o, lax.dot_general(ub16, gi16, _DN["tn"], preferred_element_type=F32)),
            (dcre_o, lax.dot_general(hre_r[...], dy16, _DN["tn"], preferred_element_type=F32)),
            (dcim_o, lax.dot_general(him_r[...], dy16, _DN["tn"], preferred_element_type=F32)),
            (dd_o, jnp.sum(dy * ub, axis=0, keepdims=True)),
        ]
        for ref, val in parts:
            @pl.when(ch == 0)
            def _(ref=ref, val=val):
                ref[...] = val

            @pl.when(ch > 0)
            def _(ref=ref, val=val):
                ref[...] += val

    def chunk(c):
        return nch - 1 - c

    u_spec = pl.BlockSpec((rows, U_BLK), lambda j, c: (chunk(c), j))
    h_spec = pl.BlockSpec((rows, SCAN_WC), lambda j, c: (chunk(c), j))
    prev_spec = pl.BlockSpec((half, SCAN_WC), lambda j, c: (jnp.maximum(chunk(c) * (rows // half) - 1, 0), j))
    e_spec = pl.BlockSpec((SEGS, SCAN_WC), lambda j, c: (0, j))
    d_spec = pl.BlockSpec((1, U_BLK), lambda j, c: (0, j))
    bt_spec = pl.BlockSpec((None, SCAN_WC, U_BLK), lambda j, c: (j, 0, 0))
    ct_spec = pl.BlockSpec((None, U_BLK, SCAN_WC), lambda j, c: (j, 0, 0))
    l_spec = pl.BlockSpec((1, SCAN_WC), lambda j, c: (0, j))
    return _run(
        body, [dyg_s, ys, u_s, h_re, h_im, h_re, h_im, hin_re, hin_im, gin_re, gin_im, dvec, w_bre_t, w_bim_t, w_cre_t,
               w_cim_neg_t, lre, lim],
        carry=carry, name=name, grid=(BD, nch),
        in_specs=[u_spec, u_spec, u_spec, h_spec, h_spec, prev_spec, prev_spec, e_spec, e_spec, e_spec, e_spec, d_spec,
                  bt_spec, bt_spec, ct_spec, ct_spec, l_spec, l_spec],
        out_specs=[u_spec, ct_spec, ct_spec, bt_spec, bt_spec, e_spec, e_spec, d_spec],
        out_shape=[jax.ShapeDtypeStruct((s, SSM_W), F32)] + [jax.ShapeDtypeStruct((BD, U_BLK, SCAN_WC), F32)] * 2
        + [jax.ShapeDtypeStruct((BD, SCAN_WC, U_BLK), F32)] * 2 + [jax.ShapeDtypeStruct((SEGS, STATE_W), F32)] * 2
        + [jax.ShapeDtypeStruct((1, SSM_W), F32)],
        scratch_shapes=[pltpu.VMEM((SEGS, SCAN_WC), F32)] * 2 + [pltpu.VMEM((rows, SCAN_WC), F32)] * 6,
        compiler_params=_cparams(("parallel", "arbitrary")),
    )


def ssm_bwd_ends(dyg_s, ys, w_cre_t, w_cim_neg_t, lre, lim, name, carry=None):
    s = ys.shape[0]
    steps, tt, nch, rows, nsq = _scan_dims(s)
    nb, ub_w, wc = w_cre_t.shape

    def body(dyg_r, ys_r, cre_r, cim_r, lre_r, lim_r, gin_re_o, gin_im_o, st_re, st_im, x_re, x_im):
        ch = pl.program_id(1)
        a_re = jnp.broadcast_to(lre_r[...], (SEGS, wc))
        a_im = -jnp.broadcast_to(lim_r[...], (SEGS, wc))
        y = ys_r[...]
        dy16 = (dyg_r[...] * _gelu_grad(y, _gelu(y)[1])).astype(BF16)
        x_re[...] = jnp.dot(dy16, cre_r[...], preferred_element_type=F32)
        x_im[...] = jnp.dot(dy16, cim_r[...], preferred_element_type=F32)

        @pl.when(ch == 0)
        def _():
            st_re[...] = jnp.zeros_like(st_re)
            st_im[...] = jnp.zeros_like(st_im)

        def step(i, hc):
            off = pl.multiple_of((tt - 1 - i) * SEGS, SEGS)
            return (a_re * hc[0] - a_im * hc[1] + x_re[pl.ds(off, SEGS), :],
                    a_re * hc[1] + a_im * hc[0] + x_im[pl.ds(off, SEGS), :])

        fin = lax.fori_loop(0, tt, step, (st_re[...], st_im[...]))
        st_re[...] = fin[0]
        st_im[...] = fin[1]

        @pl.when(ch == nch - 1)
        def _():
            c_re, c_im = _chain_segments(a_re, a_im, fin[0], fin[1], nsq, True)
            gin_re_o[...] = c_re
            gin_im_o[...] = c_im

    y_spec = pl.BlockSpec((rows, ub_w), lambda j, c: (nch - 1 - c, j))
    ct_spec = pl.BlockSpec((None, ub_w, wc), lambda j, c: (j, 0, 0))
    l_spec = pl.BlockSpec((1, wc), lambda j, c: (0, j))
    e_spec = pl.BlockSpec((SEGS, wc), lambda j, c: (0, j))
    return _run(
        body, [dyg_s, ys, w_cre_t, w_cim_neg_t, lre, lim], carry=carry, name=name, grid=(nb, nch),
        in_specs=[y_spec, y_spec, ct_spec, ct_spec, l_spec, l_spec], out_specs=[e_spec, e_spec],
        out_shape=[jax.ShapeDtypeStruct((SEGS, STATE_W), F32)] * 2,
        scratch_shapes=[pltpu.VMEM((SEGS, wc), F32)] * 2 + [pltpu.VMEM((rows, wc), F32)] * 2,
        compiler_params=_cparams(("parallel", "arbitrary")),
    )


FWD_BD = 4


def _block_diag(m, nb=BD):
    g, r, c = m.shape
    m = m.reshape(nb, g // nb, r, c)
    eye = jnp.eye(g // nb, dtype=m.dtype)
    return jnp.einsum("jarc,ab->jarbc", m, eye).reshape(nb, (g // nb) * r, (g // nb) * c)


def _block_diag_extract(m, r, c):
    per = m.shape[1] // r
    m = m.reshape(BD, per, r, per, c)
    return jnp.einsum("jarac->jarc", m).reshape(BD * per, r, c)


def to_segments(a):
    s, w = a.shape
    return a.reshape(SEGS, s // SEGS, w).transpose(1, 0, 2).reshape(s, w)


def from_segments(a):
    s, w = a.shape
    return a.reshape(s // SEGS, SEGS, w).transpose(1, 0, 2).reshape(s, w)


W_IN_CHUNK_ROWS = (320, 320, 320, 240, 576, 272)
FFN_GATE_ROWS_FIRST = 480
TALL_TM = 2048
FFN_TN = 512


def local_step(x, target, shards, small):
    s = x.shape[0]
    g1, g2, g3, g4 = (small[k].reshape(1, D_MODEL) for k in ("norm_mix_pre", "norm_mix_post", "norm_ffn_pre", "norm_ffn_post"))
    dvec = small["ssm_d"].reshape(1, SSM_W)
    wts, recv = {}, {}

    def gathered(names, blocks):
        for n, b in zip(names, blocks):
            wts[n] = _full_from_gathered(b, n)

    def rms_in_fn(r, c):
        hh = _rms(r[0], c[0])[0].astype(BF16)
        return [hh, _permute(_perm_matrix(PERM_TS, 4, False), hh), _permute(_perm_matrix(PERM_TS, 16, False), hh)], []

    (h, h4, h16), got = rowwise("rms_in", rms_in_fn, [x], [g1], [(D_MODEL, BF16), (D_MODEL, BF16, 4), (D_MODEL, BF16, 16)],
                                ts=PERM_TS, carry=Gather([shards["w_in"]]))
    w_in_t = _full_from_gathered(got[0], "w_in")
    w_u_t, w_gates_t = w_in_t[3 * HQ:3 * HQ + SSM_W], w_in_t[3 * HQ + SSM_W:]

    def qkv_rows(g):
        return 3 * GROUP_W, lambda t: 3 * t + g

    hd = [h.reshape(1, s, D_MODEL), h4, h16]
    qkv = [None] * 3
    names = ("w_attn_up", "w_glu_v", "w_glu_g")
    qkv[0], got = mm([(hd[0].reshape(s, D_MODEL), w_in_t)], "nt", BF16, "mm_qkv0", tm=TALL_TM, tn=GROUP_W, b_window=qkv_rows(0),
                     carry=Gather([shards[n] for n in names]))
    gathered(names, got)
    qkv[1], got = mm([(hd[1].reshape(s, D_MODEL), w_in_t)], "nt", BF16, "mm_qkv1", tm=TALL_TM, tn=GROUP_W, b_window=qkv_rows(1),
                     carry=Gather([shards["w_out"]]))
    gathered(("w_out",), got)
    qkv[2] = mm([(hd[2].reshape(s, D_MODEL), w_in_t)], "nt", BF16, "mm_qkv2", tm=TALL_TM, tn=GROUP_W, b_window=qkv_rows(2))
    u = mm([(h, w_u_t)], "nt", F32, "mm_u")
    gates, got = mm([(h, w_gates_t)], "nt", BF16, "mm_gates", carry=Gather([shards["w_ffn_gate"]]))
    gathered(("w_ffn_gate",), got)

    outs, lses = [], []
    for g, (_, dil) in enumerate(ATTN_GROUPS):
        o, l = attn_fwd(qkv[g].reshape(dil, s // dil, 3 * GROUP_W), g, f"attn_fwd{g}")
        outs.append(o.reshape(s, GROUP_W) if dil == 1 else o)
        lses.append(l.reshape(s, GROUP_W) if dil == 1 else l)

    def natural(r):
        back4, back16 = _perm_matrix(PERM_TS, 4, True), _perm_matrix(PERM_TS, 16, True)
        return (r[0], _permute(back4, r[1].astype(BF16)), _permute(back16, r[2].astype(BF16)),
                r[3], _permute(back4, r[4]), _permute(back16, r[5]))

    def merge_fn(r, c):
        o0, o1, o2, l0, l1, l2 = natural(r)
        w0, w1, w2 = _mix_weights(l0, l1, l2)
        return [w0 * o0 + w1 * o1 + w2 * o2], []

    (attn,) = rowwise("attn_merge", merge_fn, outs + lses, [], [(GROUP_W, BF16)], ts=PERM_TS)
    attn_branch = mm([(attn, wts["w_attn_up"])], "nn", BF16, "mm_up", tm=TALL_TM)

    are3 = small["ssm_a_re"].reshape(SSM_GROUPS, SSM_STATE, 1)
    aim3 = small["ssm_a_im"].reshape(SSM_GROUPS, SSM_STATE, 1)
    ldt3 = small["ssm_log_dt"].reshape(SSM_GROUPS, 1, 1)
    bre3 = small["ssm_b_re"].reshape(SSM_GROUPS, SSM_STATE, SSM_GROUP)
    bim3 = small["ssm_b_im"].reshape(SSM_GROUPS, SSM_STATE, SSM_GROUP)
    cre3 = small["ssm_c_re"].reshape(SSM_GROUPS, SSM_GROUP, SSM_STATE)
    cim3 = small["ssm_c_im"].reshape(SSM_GROUPS, SSM_GROUP, SSM_STATE)
    lre3, lim3, bbre, bbim = ssm_prep(are3, aim3, ldt3, bre3, bim3)
    lre, lim = lre3.reshape(1, STATE_W), lim3.reshape(1, STATE_W)
    w_bre = _block_diag(bbre.transpose(0, 2, 1)).astype(BF16)
    w_bim = _block_diag(bbim.transpose(0, 2, 1)).astype(BF16)
    w_cre = _block_diag(cre3.transpose(0, 2, 1)).astype(BF16)
    w_cim = _block_diag(cim3.transpose(0, 2, 1)).astype(BF16)
    u_s = to_segments(u)
    fwd_w = [_block_diag(t.transpose(0, 2, 1), FWD_BD).astype(BF16) for t in (bbre, bbim, cre3, -cim3)]
    (yg_s, y_ssm, h_re, h_im, hin_re, hin_im), got = ssm_fwd(
        u_s, dvec, *fwd_w, lre, lim, "ssm_fwd", carry=Gather([shards["w_ffn_up"]], pass_early=True))
    gathered(("w_ffn_up",), got)
    yg = from_segments(yg_s)
    gv = mm([(yg, wts["w_glu_v"])], "nn", BF16, "mm_glu_v", tm=TALL_TM)
    gg = mm([(yg, wts["w_glu_g"])], "nn", BF16, "mm_glu_g", tm=TALL_TM)

    def gate_fn(r, c):
        gts, ab, gv_, gg_ = r
        sa, ss = _sigmoid(gts[:, :D_MODEL]), _sigmoid(gts[:, D_MODEL:])
        return [sa * ab + ss * (gv_ * _sigmoid(gg_))], []

    (merged,) = rowwise("gate_merge", gate_fn, [gates, attn_branch, gv, gg], [], [(D_MODEL, BF16)])
    o_mix = mm([(merged, wts["w_out"])], "nn", F32, "mm_out")

    def mid_fn(r, c):
        x1 = r[0] + _rms(r[1], c[0])[0]
        return [x1, _rms(x1, c[1])[0]], []

    x1, h2 = rowwise("rms_mid", mid_fn, [x, o_mix], [g2, g3], [(D_MODEL, F32), (D_MODEL, BF16)])
    (fa, fb, fin), got = mm([(h2, wts["w_ffn_gate"]), (h2, wts["w_ffn_up"])], "nt", [BF16, BF16, BF16], "mm_ffn_in", tn=FFN_TN,
                            epilogue=lambda p, e: [p[0], p[1], p[0] * _sigmoid(p[0]) * p[1]],
                            carry=Gather([shards["w_ffn_down"]], pass_early=True))
    gathered(("w_ffn_down",), got)
    f = mm([(fin, wts["w_ffn_down"])], "nn", F32, "mm_ffn_down", tn=512, tk=D_FF)

    def loss_fn(r, c):
        x1_, f_, tgt = r
        y, n, rr = _rms(f_, c[0])
        err = x1_ + y - tgt
        dout = err * (1.0 / D_MODEL)
        df, dg = _rms_bwd(dout, n, rr, c[0])
        lp = 0.5 * jnp.sum(jnp.sum(err * err, axis=-1, keepdims=True) * (1.0 / D_MODEL), axis=0, keepdims=True)
        return [df, dout], [dg, lp]

    df, dout, dg4, loss_part = rowwise("loss_bwd", loss_fn, [x1, f, target], [g4], [(D_MODEL, BF16), (D_MODEL, BF16)],
                                       acc_outs=[(1, D_MODEL), (1, 1)])
    def sent(names, blocks):
        for n, b in zip(names, blocks):
            recv[n] = b

    def to_owners(names, dws):
        return AllToAll([_split_for_devices(d, n) for n, d in zip(names, dws)])

    def swiglu_bwd(p, e):
        dfin_, (a, b) = p[0], e
        sg = _sigmoid(a)
        return [dfin_ * b * (sg * (1.0 + a * (1.0 - sg))), dfin_ * a * sg]

    da, db = mm([(df, wts["w_ffn_down"])], "nt", [BF16, BF16], "mm_d_fin", tn=FFN_TN, epilogue=swiglu_bwd, extras=[fa, fb])
    dw_ffn_down = mm([(fin, df)], "tn", BF16, "mm_dw_ffn_down")
    dh2, got = mm([(da, wts["w_ffn_gate"]), (db, wts["w_ffn_up"])], "nn", F32, "mm_d_h2", tm=512, tn=1024, tk=D_FF // 2,
                  carry=to_owners(["w_ffn_down"], [dw_ffn_down]))
    sent(["w_ffn_down"], got)
    dw_ffn_gate = mm([(da, h2)], "tn", BF16, "mm_dw_ffn_gate")
    gate_blocks = _split_for_devices(dw_ffn_gate, "w_ffn_gate")
    dw_ffn_up, (gate_landed,) = mm([(db, h2)], "tn", BF16, "mm_dw_ffn_up",
                                   carry=RowsToOwners(gate_blocks, 0, FFN_GATE_ROWS_FIRST))

    def mid_bwd(r, c):
        dh2_, dout_, x1_, o_ = r
        _, n3, r3 = _rms(x1_, c[1])
        dx1, dg3_ = _rms_bwd(dh2_, n3, r3, c[1])
        dx1 = dx1 + dout_
        _, n2, r2 = _rms(o_, c[0])
        do_, dg2_ = _rms_bwd(dx1, n2, r2, c[0])
        return [dx1, do_], [dg2_, dg3_]

    rest = gate_blocks.shape[1] - FFN_GATE_ROWS_FIRST
    (dx1, do_mix, dg2, dg3), (gate_landed,) = rowwise(
        "rms_mid_bwd", mid_bwd, [dh2, dout, x1, o_mix], [g2, g3], [(D_MODEL, F32), (D_MODEL, BF16)],
        acc_outs=[(1, D_MODEL), (1, D_MODEL)], carry=RowsToOwners(gate_blocks, FFN_GATE_ROWS_FIRST, rest, into=gate_landed))
    recv["w_ffn_gate"] = gate_landed
    dmerged = mm([(do_mix, wts["w_out"])], "nt", BF16, "mm_d_merged")
    dw_out = mm([(merged, do_mix)], "tn", BF16, "mm_dw_out")

    def gate_bwd(r, c):
        dm, gts, ab, gv_, gg_ = r
        sa, ss, sg = _sigmoid(gts[:, :D_MODEL]), _sigmoid(gts[:, D_MODEL:]), _sigmoid(gg_)
        branch = gv_ * sg
        dbranch = dm * ss
        dgates = jnp.concatenate([dm * ab * sa * (1.0 - sa), dm * branch * ss * (1.0 - ss)], axis=-1)
        return [dgates, dm * sa, dbranch * sg, dbranch * gv_ * sg * (1.0 - sg)], []

    dgates, dab, dgv, dgg = rowwise("gate_bwd", gate_bwd, [dmerged, gates, attn_branch, gv, gg], [],
                                    [(2 * D_MODEL, BF16), (D_MODEL, BF16), (D_MODEL, BF16), (D_MODEL, BF16)])
    dattn = mm([(dab, wts["w_attn_up"])], "nt", F32, "mm_d_attn")
    dw_up = mm([(attn, dab)], "tn", BF16, "mm_dw_up")
    dyg = mm([(dgv, wts["w_glu_v"]), (dgg, wts["w_glu_g"])], "nt", F32, "mm_d_yg")
    dw_glu_v = mm([(yg, dgv)], "tn", BF16, "mm_dw_glu_v")
    dw_glu_g = mm([(yg, dgg)], "tn", BF16, "mm_dw_glu_g")

    dyg_s = to_segments(dyg)
    (gin_re, gin_im), got = ssm_bwd_ends(dyg_s, y_ssm, fwd_w[2].transpose(0, 2, 1), fwd_w[3].transpose(0, 2, 1), lre, lim,
                                         "ssm_bwd_ends", carry=to_owners(["w_out"], [dw_out]))
    sent(["w_out"], got)
    (du_s, dbre_d, dbim_d, dcre_d, dcim_d, dl_re8, dl_im8, dd_ssm), got = ssm_bwd(
        dyg_s, y_ssm, u_s, h_re, h_im, hin_re, hin_im, gin_re, gin_im, dvec, w_bre.transpose(0, 2, 1), w_bim.transpose(0, 2, 1),
        w_cre.transpose(0, 2, 1), -w_cim.transpose(0, 2, 1), lre, lim, "ssm_bwd", carry=to_owners(["w_ffn_up"], [dw_ffn_up]))
    sent(["w_ffn_up"], got)
    dbb_re = _block_diag_extract(dbre_d, SSM_GROUP, SSM_STATE).transpose(0, 2, 1)
    dbb_im = _block_diag_extract(dbim_d, SSM_GROUP, SSM_STATE).transpose(0, 2, 1)
    dc_re = _block_diag_extract(dcre_d, SSM_STATE, SSM_GROUP).transpose(0, 2, 1)
    dc_im = -_block_diag_extract(dcim_d, SSM_STATE, SSM_GROUP).transpose(0, 2, 1)

    def fold8(r, c):
        return [], [jnp.sum(r[0], axis=0, keepdims=True), jnp.sum(r[1], axis=0, keepdims=True)]

    dl_re, dl_im = rowwise("ssm_dl_fold", fold8, [dl_re8, dl_im8], [], [], acc_outs=[(1, STATE_W), (1, STATE_W)], ts=SEGS)
    da_re, da_im, dldt, db_re, db_im = ssm_prep_bwd(
        are3, aim3, ldt3, bre3, bim3, dbb_re, dbb_im,
        dl_re.reshape(SSM_GROUPS, SSM_STATE, 1), dl_im.reshape(SSM_GROUPS, SSM_STATE, 1))
    du = from_segments(du_s)

    def merge_bwd(r, c):
        dat = r[0]
        o0, o1, o2, l0, l1, l2 = natural(r[1:])
        w0, w1, w2 = _mix_weights(l0, l1, l2)
        tot = _head_sum(dat * (w0 * o0 + w1 * o1 + w2 * o2))
        to4, to16 = _perm_matrix(PERM_TS, 4, False), _perm_matrix(PERM_TS, 16, False)
        return [w0 * dat, _permute(to4, (w1 * dat).astype(BF16)), _permute(to16, (w2 * dat).astype(BF16)),
                w0 * tot, _permute(to4, (w1 * tot).astype(BF16)), _permute(to16, (w2 * tot).astype(BF16))], []

    mb = rowwise("attn_merge_bwd", merge_bwd, [dattn] + outs + lses, [],
                 [(GROUP_W, BF16), (GROUP_W, BF16, 4), (GROUP_W, BF16, 16), (GROUP_W, BF16), (GROUP_W, BF16, 4), (GROUP_W, BF16, 16)],
                 ts=PERM_TS)
    dqs, dw_qkv = [], []
    names = ["w_glu_v", "w_glu_g", "w_attn_up"]
    for g, (_, dil) in enumerate(ATTN_GROUPS):
        dq = attn_bwd(qkv[g].reshape(dil, s // dil, 3 * GROUP_W), mb[g].reshape(dil, s // dil, GROUP_W),
                      lses[g].reshape(dil, s // dil, GROUP_W), mb[3 + g].reshape(dil, s // dil, GROUP_W),
                      g, f"attn_bwd{g}", carry=to_owners(names, [dw_glu_v, dw_glu_g, dw_up]) if g == 1 else None)
        if g == 1:
            dq, got = dq
            sent(names, got)
        dq = dq.reshape(s, 3 * GROUP_W)
        dqs.append(dq)
        dw_qkv.append(mm([(hd[g].reshape(s, D_MODEL), dq)], "tn", BF16, f"mm_dw_qkv{g}"))
    dw_u = mm([(h, du)], "tn", BF16, "mm_dw_u")
    dw_gates = mm([(h, dgates)], "tn", BF16, "mm_dw_gates")
    dw_in = jnp.concatenate(
        [dw_qkv[g][:, o * GROUP_W:(o + 1) * GROUP_W] for o in range(3) for g in range(3)] + [dw_u, dw_gates], axis=1)
    dw_in_blocks = _split_for_devices(dw_in, "w_in")
    starts = [sum(W_IN_CHUNK_ROWS[:i]) for i in range(len(W_IN_CHUNK_ROWS))]
    landed = None

    def chunk(i):
        return RowsToOwners(dw_in_blocks, starts[i], W_IN_CHUNK_ROWS[i], into=landed)

    dh_parts = []
    for g, (_, dil) in enumerate(ATTN_GROUPS):
        dh_g, (landed,) = mm([(dqs[g], w_in_t)], "nn", BF16, f"mm_d_h_qkv{g}", tk=GROUP_W, b_window=qkv_rows(g), carry=chunk(g))
        dh_parts.append(dh_g if dil == 1 else dh_g.reshape(dil, s // dil, D_MODEL))
    dh_u, (landed,) = mm([(du, w_u_t)], "nn", BF16, "mm_d_h_u", carry=chunk(3))
    dh_gates, (landed,) = mm([(dgates, w_gates_t)], "nn", BF16, "mm_d_h_gates", carry=chunk(4))
    dh_parts += [dh_u, dh_gates]

    def in_bwd(r, c):
        dh1 = _permute(_perm_matrix(PERM_TS, 4, True), r[1].astype(BF16))
        dh2_ = _permute(_perm_matrix(PERM_TS, 16, True), r[2].astype(BF16))
        dh = r[0] + dh1 + dh2_ + r[3] + r[4]
        _, n1, r1 = _rms(r[6], c[0])
        dx, dg1_ = _rms_bwd(dh, n1, r1, c[0])
        return [dx + r[5]], [dg1_]

    (grad_x, dg1), (landed,) = rowwise("rms_in_bwd", in_bwd, dh_parts + [dx1, x], [g1], [(D_MODEL, F32)],
                                       acc_outs=[(1, D_MODEL)], ts=PERM_TS, carry=chunk(5))
    recv["w_in"] = landed

    dsmall = dict(norm_mix_pre=dg1, ssm_a_re=da_re, ssm_a_im=da_im, ssm_log_dt=dldt, ssm_b_re=db_re, ssm_b_im=db_im,
                  ssm_c_re=dc_re, ssm_c_im=dc_im, ssm_d=dd_ssm, norm_mix_post=dg2, norm_ffn_pre=dg3, norm_ffn_post=dg4)
    return loss_part, grad_x, recv, dsmall


def adamw(parts, w, m, v, name, carry=None):
    r, c = w.shape
    tr = r
    while tr > 8 and tr % 2 == 0 and tr * c * (8 * parts.dtype.itemsize + 28) * 2 > 24 * 1024 * 1024:
        tr //= 2
    assert r % tr == 0 and (tr % 8 == 0 or tr == r)
    c1, c2 = 1.0 / (1.0 - ADAM_B1 ** ADAM_STEP), 1.0 / (1.0 - ADAM_B2 ** ADAM_STEP)

    def body(p_ref, w_ref, m_ref, v_ref, g_o, d_o, m_o, v_o):
        g = p_ref[0].astype(F32)
        for i in range(1, N_DEV):
            g = g + p_ref[i].astype(F32)
        mn = ADAM_B1 * m_ref[...] + (1.0 - ADAM_B1) * g
        vn = ADAM_B2 * v_ref[...] + (1.0 - ADAM_B2) * (g * g)
        g_o[...] = g
        m_o[...] = mn
        v_o[...] = vn
        d_o[...] = -ADAM_LR * ((mn * c1) / (jnp.sqrt(vn * c2) + ADAM_EPS) + ADAM_WD * w_ref[...])

    blk = pl.BlockSpec((tr, c), lambda i: (i, 0))
    return _run(
        body, [parts, w, m, v], carry=carry, name=name, grid=(r // tr,),
        in_specs=[pl.BlockSpec((N_DEV, tr, c), lambda i: (0, i, 0)), blk, blk, blk],
        out_specs=[blk] * 4, out_shape=[jax.ShapeDtypeStruct((r, c), F32)] * 4, compiler_params=_cparams(("parallel",)),
    )


PACK_C = 1024
SHARDED = ("w_in", "w_attn_up", "w_glu_v", "w_glu_g", "w_out", "w_ffn_gate", "w_ffn_up", "w_ffn_down")
ROW_SHARDED = ("w_out", "w_ffn_down")
SENT_TRANSPOSED = ("w_in", "w_ffn_gate", "w_ffn_up")
GRAD_TRANSPOSED = ("w_ffn_gate", "w_ffn_up")
SMALL = ("norm_mix_pre", "ssm_a_re", "ssm_a_im", "ssm_log_dt", "ssm_b_re", "ssm_b_im", "ssm_c_re", "ssm_c_im", "ssm_d",
         "norm_mix_post", "norm_ffn_pre", "norm_ffn_post")
WEIGHTS = ("norm_mix_pre", "w_in", "w_attn_up", "ssm_a_re", "ssm_a_im", "ssm_log_dt", "ssm_b_re", "ssm_b_im", "ssm_c_re",
           "ssm_c_im", "ssm_d", "w_glu_v", "w_glu_g", "w_out", "norm_mix_post", "norm_ffn_pre", "w_ffn_gate", "w_ffn_up",
           "w_ffn_down", "norm_ffn_post")


def _pack(arrs, dtype, pad_rows_to=64):
    flat = jnp.concatenate([a.reshape(-1).astype(dtype) for a in arrs])
    n = flat.shape[0]
    rows = -(-n // PACK_C)
    rows = -(-rows // pad_rows_to) * pad_rows_to
    return jnp.pad(flat, (0, rows * PACK_C - n)).reshape(rows, PACK_C)


def _unpack(flat2d, shapes):
    flat = flat2d.reshape(-1)
    out, off = [], 0
    for shp in shapes:
        n = int(np.prod(shp))
        out.append(flat[off:off + n].reshape(shp))
        off += n
    return out


def _full_from_gathered(gathered, name):
    if name in ROW_SHARDED or name in SENT_TRANSPOSED:
        return gathered.reshape(-1, gathered.shape[2])
    return gathered.transpose(1, 0, 2).reshape(gathered.shape[1], -1)


def _split_for_devices(full, name):
    if name in ROW_SHARDED or name in GRAD_TRANSPOSED:
        return full.reshape(N_DEV, -1, full.shape[1])
    return full.reshape(full.shape[0], N_DEV, -1).transpose(1, 0, 2)


def kernel(x, norm_mix_pre, w_in, w_attn_up, ssm_a_re, ssm_a_im, ssm_log_dt, ssm_b_re, ssm_b_im, ssm_c_re, ssm_c_im, ssm_d, w_glu_v, w_glu_g, w_out, norm_mix_post, norm_ffn_pre, w_ffn_gate, w_ffn_up, w_ffn_down, norm_ffn_post, loss_target, m_norm_mix_pre, m_w_in, m_w_attn_up, m_ssm_a_re, m_ssm_a_im, m_ssm_log_dt, m_ssm_b_re, m_ssm_b_im, m_ssm_c_re, m_ssm_c_im, m_ssm_d, m_w_glu_v, m_w_glu_g, m_w_out, m_norm_mix_post, m_norm_ffn_pre, m_w_ffn_gate, m_w_ffn_up, m_w_ffn_down, m_norm_ffn_post, v_norm_mix_pre, v_w_in, v_w_attn_up, v_ssm_a_re, v_ssm_a_im, v_ssm_log_dt, v_ssm_b_re, v_ssm_b_im, v_ssm_c_re, v_ssm_c_im, v_ssm_d, v_w_glu_v, v_w_glu_g, v_w_out, v_norm_mix_post, v_norm_ffn_pre, v_w_ffn_gate, v_w_ffn_up, v_w_ffn_down, v_norm_ffn_post):
    args = dict(locals())
    wv = {n: args[n][0] for n in WEIGHTS}
    mv = {n: args["m_" + n][0] for n in WEIGHTS}
    vv = {n: args["v_" + n][0] for n in WEIGHTS}

    shards = {n: (wv[n].T if n in SENT_TRANSPOSED else wv[n]).astype(BF16) for n in SHARDED}
    small = {n: wv[n] for n in SMALL}
    loss_part, grad_x, recv, dsmall = local_step(x[0], loss_target[0], shards, small)
    for n in GRAD_TRANSPOSED:
        recv[n] = recv[n].transpose(0, 2, 1)

    small_shapes = [wv[n].shape for n in SMALL]
    res = {}
    res["w_in"], (sgather,) = adamw(recv["w_in"], wv["w_in"], mv["w_in"], vv["w_in"], "adamw_w_in",
                                    carry=Gather([_pack([dsmall[n] for n in SMALL], F32)]))
    for n in SHARDED[1:]:
        res[n] = adamw(recv[n], wv[n], mv[n], vv[n], "adamw_" + n)
    sres = adamw(sgather, _pack([wv[n] for n in SMALL], F32), _pack([mv[n] for n in SMALL], F32),
                 _pack([vv[n] for n in SMALL], F32), "adamw_small")
    sun = [_unpack(t, small_shapes) for t in sres]
    for k, n in enumerate(SMALL):
        res[n] = tuple(sun[t][k] for t in range(4))

    loss = lax.psum(loss_part[0, 0], ("x", "y", "c"))
    outs = [loss, grad_x[None]]
    for t in range(4):
        outs += [res[n][t][None] for n in WEIGHTS]
    return tuple(outs)
```

```python
import math

import numpy as np
import jax
import jax.numpy as jnp
from jax import lax
from jax.experimental import pallas as pl
from jax.experimental.pallas import tpu as pltpu

F32 = jnp.float32
BF16 = jnp.bfloat16

D_MODEL = 2048
HEAD_DIM = 128
HEADS_PER_GROUP = 4
ATTN_GROUPS = ((128, 1), (512, 4), (2048, 16))
N_HEADS = HEADS_PER_GROUP * len(ATTN_GROUPS)
GROUP_W = HEADS_PER_GROUP * HEAD_DIM
HQ = N_HEADS * HEAD_DIM
SSM_W = 1024
SSM_GROUP = 16
SSM_GROUPS = 64
SSM_STATE = 64
STATE_W = SSM_GROUPS * SSM_STATE
D_FF = 5632
EPS = 1e-6
N_DEV = 8
SEGS = 8
BD = 8

ADAM_LR, ADAM_B1, ADAM_B2, ADAM_EPS, ADAM_WD, ADAM_STEP = 0.001, 0.9, 0.999, 1e-08, 0.01, 10

VMEM_LIMIT = 56 * 1024 * 1024
HBM_SPEC = pl.BlockSpec(memory_space=pltpu.HBM)
MESH_ID = pl.DeviceIdType.MESH
NEG = -1e30


def _pcall(body, **kw):
    return pl.pallas_call(body, **kw)


def _cparams(sem=None):
    if sem is None:
        return pltpu.CompilerParams(vmem_limit_bytes=VMEM_LIMIT)
    return pltpu.CompilerParams(vmem_limit_bytes=VMEM_LIMIT, dimension_semantics=sem)


def _my_coords():
    return lax.axis_index("x"), lax.axis_index("y"), lax.axis_index("c")


class Gather:
    def __init__(self, xs, pass_early=False):
        self.arrays = list(xs)
        self.out_shapes = [jax.ShapeDtypeStruct((N_DEV,) + x.shape, x.dtype) for x in xs]
        self.pass_early = pass_early

    def _ctx(self, out_refs, send_sems, recv_sems):
        mx, my, mc = _my_coords()
        me, sibling = (mx, my, mc), (mx, my, 1 - mc)
        chips = [(1 - mx, my), (mx, 1 - my), (1 - mx, 1 - my)]

        def slot(a, px, py, pc):
            return out_refs[a].at[4 * px + 2 * py + pc]

        def copy(a, k, block, to, src=None):
            return pltpu.make_async_remote_copy(
                src_ref=slot(a, *block) if src is None else src, dst_ref=slot(a, *block),
                send_sem=send_sems.at[7 * a + k], recv_sem=recv_sems.at[7 * a + k], device_id=to, device_id_type=MESH_ID)

        return me, sibling, chips, mc, slot, copy

    def _first(self, a, x_refs, ctx):
        me, sibling, chips, mc, slot, copy = ctx
        return [copy(a, 0, me, sibling, src=x_refs[a])] + [copy(a, 1 + j, me, (*chip, mc), src=x_refs[a]) for j, chip in enumerate(chips)]

    def start(self, x_refs, out_refs, send_sems, recv_sems, local_sems):
        ctx = self._ctx(out_refs, send_sems, recv_sems)
        me, slot = ctx[0], ctx[4]
        for a in range(len(self.arrays)):
            pltpu.make_async_copy(x_refs[a], slot(a, *me), local_sems.at[a]).start()
            for cp in self._first(a, x_refs, ctx):
                cp.start()

    def _passed(self, ctx):
        me, sibling, chips, mc, slot, copy = ctx
        return [copy(a, 4 + j, (*chip, mc), sibling) for a in range(len(self.arrays)) for j, chip in enumerate(chips)]

    def middle(self, x_refs, out_refs, send_sems, recv_sems, local_sems):
        ctx = self._ctx(out_refs, send_sems, recv_sems)
        me, sibling, chips, mc, slot, copy = ctx
        for a in range(len(self.arrays)):
            for j, chip in enumerate(chips):
                copy(a, 1 + j, (*chip, mc), me).wait_recv()
                copy(a, 4 + j, (*chip, mc), sibling).start()

    def finish(self, x_refs, out_refs, send_sems, recv_sems, local_sems, passed_on=False):
        if not passed_on:
            self.middle(x_refs, out_refs, send_sems, recv_sems, local_sems)
        ctx = self._ctx(out_refs, send_sems, recv_sems)
        me, sibling, chips, mc, slot, copy = ctx
        na = len(self.arrays)
        passed = self._passed(ctx)
        for a in range(na):
            copy(a, 0, sibling, me).wait_recv()
            for j, chip in enumerate(chips):
                copy(a, 4 + j, (*chip, 1 - mc), me).wait_recv()
        for a in range(na):
            for cp in self._first(a, x_refs, ctx):
                cp.wait_send()
        for cp in passed:
            cp.wait_send()
        for a in range(na):
            pltpu.make_async_copy(x_refs[a], slot(a, *me), local_sems.at[a]).wait()


class AllToAll:
    def __init__(self, ps):
        self.arrays = list(ps)
        self.out_shapes = [jax.ShapeDtypeStruct(p.shape, p.dtype) for p in ps]

    def _copies(self, p_refs, out_refs, send_sems, recv_sems, local_sems):
        mx, my, mc = _my_coords()
        me = 4 * mx + 2 * my + mc
        local, remote = [], []
        for a in range(len(self.arrays)):
            local.append(pltpu.make_async_copy(p_refs[a].at[me], out_refs[a].at[me], local_sems.at[a]))
            for k in range(1, N_DEV):
                px, py, pc = mx ^ ((k >> 2) & 1), my ^ ((k >> 1) & 1), mc ^ (k & 1)
                remote.append(pltpu.make_async_remote_copy(
                    src_ref=p_refs[a].at[4 * px + 2 * py + pc], dst_ref=out_refs[a].at[me],
                    send_sem=send_sems.at[7 * a + k - 1], recv_sem=recv_sems.at[7 * a + k - 1],
                    device_id=(px, py, pc), device_id_type=MESH_ID))
        return local, remote

    def start(self, *refs):
        local, remote = self._copies(*refs)
        for cp in local + remote:
            cp.start()

    def finish(self, *refs):
        local, remote = self._copies(*refs)
        for cp in remote:
            cp.wait_recv()
        for cp in remote:
            cp.wait_send()
        for cp in local:
            cp.wait()


class RowsToOwners:
    def __init__(self, p, r0, n, into=None):
        self.arrays = [p] if into is None else [p, into]
        self.out_shapes = [jax.ShapeDtypeStruct(p.shape, p.dtype)]
        self.aliases = {} if into is None else {1: 0}
        self.rows = (r0, n)

    def _copies(self, p_refs, out_refs, send_sems, recv_sems, local_sems):
        mx, my, mc = _my_coords()
        me = 4 * mx + 2 * my + mc
        rows = pl.ds(*self.rows)
        local = [pltpu.make_async_copy(p_refs[0].at[me, rows], out_refs[0].at[me, rows], local_sems.at[0])]
        remote = []
        for k in range(1, N_DEV):
            px, py, pc = mx ^ ((k >> 2) & 1), my ^ ((k >> 1) & 1), mc ^ (k & 1)
            remote.append(pltpu.make_async_remote_copy(
                src_ref=p_refs[0].at[4 * px + 2 * py + pc, rows], dst_ref=out_refs[0].at[me, rows],
                send_sem=send_sems.at[k - 1], recv_sem=recv_sems.at[k - 1], device_id=(px, py, pc), device_id_type=MESH_ID))
        return local, remote

    start = AllToAll.start
    finish = AllToAll.finish


def _run(body, args, carry=None, **kw):
    if carry is None:
        return _pcall(body, **kw)(*args)
    grid = kw["grid"]
    single = not isinstance(kw["out_shape"], (list, tuple))
    in_specs = list(kw["in_specs"])
    out_specs = [kw["out_specs"]] if single else list(kw["out_specs"])
    out_shape = [kw["out_shape"]] if single else list(kw["out_shape"])
    scratch = list(kw.get("scratch_shapes", []))
    na, nin, nout, nscr = len(carry.arrays), len(in_specs), len(out_specs), len(scratch)
    nco = len(carry.out_shapes)
    aliases = {nin + i: nout + o for i, o in getattr(carry, "aliases", {}).items()}
    steps = int(np.prod(grid))
    mid_step = (steps * 7) // 10 if getattr(carry, "pass_early", False) and steps >= 4 else None

    def carried(*refs):
        ins, cin = refs[:nin], refs[nin:nin + na]
        outs, cout = refs[nin + na:nin + na + nout], refs[nin + na + nout:nin + na + nout + nco]
        scr = refs[nin + na + nout + nco:nin + na + nout + nco + nscr]
        sems = refs[nin + na + nout + nco + nscr:]
        step = pl.program_id(0)
        for i in range(1, len(grid)):
            step = step * grid[i] + pl.program_id(i)

        @pl.when(step == 0)
        def _():
            carry.start(cin, cout, *sems)

        if mid_step is not None:
            @pl.when(step == mid_step)
            def _():
                carry.middle(cin, cout, *sems)

        body(*ins, *outs, *scr)

        @pl.when(step == steps - 1)
        def _():
            if mid_step is not None:
                carry.finish(cin, cout, *sems, passed_on=True)
            else:
                carry.finish(cin, cout, *sems)

    res = _pcall(
        carried, name=kw["name"], grid=grid, in_specs=in_specs + [HBM_SPEC] * na, out_specs=out_specs + [HBM_SPEC] * nco,
        out_shape=out_shape + carry.out_shapes, input_output_aliases=aliases,
        scratch_shapes=scratch + [pltpu.SemaphoreType.DMA((7 * na,)), pltpu.SemaphoreType.DMA((7 * na,)), pltpu.SemaphoreType.DMA((na,))],
        compiler_params=_cparams(("arbitrary",) * len(grid)),
    )(*args, *carry.arrays)
    main = res[:nout]
    return (main[0] if single else main), list(res[nout:])


_DN = {"nn": (((1,), (0,)), ((), ())), "nt": (((1,), (1,)), ((), ())), "tn": (((0,), (0,)), ((), ()))}


LANE = 128
MM_TM, MM_TN, MM_TK = 1024, 1536, 2048


def _tile(n, cap):
    for t in range(min(cap, n) // LANE * LANE, 0, -LANE):
        if n % t == 0:
            return t
    raise ValueError(n)


DW_TM, DW_TN, DW_TK = 512, 512, 8192
EPILOGUE_SPLIT = 2


def mm(pairs, mode, out_dtype, name, tm=None, tn=None, tk=None, carry=None, epilogue=None, extras=(), b_window=None):
    a0, b0 = pairs[0]
    if mode == "nn":
        (m, k), n = a0.shape, b0.shape[1]
    elif mode == "nt":
        (m, k), n = a0.shape, b0.shape[0]
    else:
        (k, m), n = a0.shape, b0.shape[1]
    if b_window is not None:
        assert mode in ("nn", "nt") and len(pairs) == 1
        if mode == "nt":
            n = b_window[0]
        else:
            assert k == b_window[0]
    caps = (DW_TM, DW_TN, DW_TK) if mode == "tn" else (MM_TM, MM_TN, MM_TK)
    tm, tn, tk = _tile(m, tm or caps[0]), _tile(n, tn or caps[1]), _tile(k, tk or caps[2])
    nk = k // tk
    npairs = len(pairs)
    nex = len(extras)
    fused = epilogue is not None
    assert not fused or nk == 1
    out_dtypes = list(out_dtype) if fused else [out_dtype]

    def body(*refs):
        if fused:
            half = tn // EPILOGUE_SPLIT
            for c in range(EPILOGUE_SPLIT):
                cols = slice(c * half, (c + 1) * half)
                prods = []
                for p in range(npairs):
                    a = refs[2 * p][...].astype(BF16)
                    b = (refs[2 * p + 1][cols, :] if mode == "nt" else refs[2 * p + 1][:, cols]).astype(BF16)
                    prods.append(lax.dot_general(a, b, _DN[mode], preferred_element_type=F32))
                ex = [refs[2 * npairs + e][:, cols].astype(F32) for e in range(nex)]
                for o_ref, val in zip(refs[2 * npairs + nex:], epilogue(prods, ex)):
                    o_ref[:, cols] = val.astype(o_ref.dtype)
            return
        prods = []
        for p in range(npairs):
            a = refs[2 * p][...].astype(BF16) if (p == 0 or pairs[p][0] is not pairs[p - 1][0]) else a
            b = refs[2 * p + 1][...].astype(BF16)
            prods.append(lax.dot_general(a, b, _DN[mode], preferred_element_type=F32))
        o_ref = refs[2 * npairs]
        tot = prods[0]
        for d in prods[1:]:
            tot = tot + d
        if nk == 1:
            o_ref[...] = tot.astype(o_ref.dtype)
            return
        acc = refs[2 * npairs + 1]
        kk = pl.program_id(2)

        @pl.when(kk == 0)
        def _():
            acc[...] = tot

        @pl.when(kk > 0)
        def _():
            acc[...] += tot

        @pl.when(kk == nk - 1)
        def _():
            o_ref[...] = acc[...].astype(o_ref.dtype)

    rows_of = b_window[1] if b_window is not None else (lambda t: t)
    if mode == "nn":
        sp = [pl.BlockSpec((tm, tk), lambda i, j, kk: (i, kk)), pl.BlockSpec((tk, tn), lambda i, j, kk: (rows_of(kk), j))]
    elif mode == "nt":
        sp = [pl.BlockSpec((tm, tk), lambda i, j, kk: (i, kk)), pl.BlockSpec((tn, tk), lambda i, j, kk: (rows_of(j), kk))]
    else:
        sp = [pl.BlockSpec((tk, tm), lambda i, j, kk: (kk, i)), pl.BlockSpec((tk, tn), lambda i, j, kk: (kk, j))]
    o_spec = pl.BlockSpec((tm, tn), lambda i, j, kk: (i, j))
    out_shapes = [jax.ShapeDtypeStruct((m, n), dt) for dt in out_dtypes]
    return _run(
        body, [t for pr in pairs for t in pr] + list(extras), carry=carry, name=name, grid=(m // tm, n // tn, nk),
        in_specs=sp * npairs + [o_spec] * nex,
        out_specs=[o_spec] * len(out_shapes) if fused else o_spec,
        out_shape=out_shapes if fused else out_shapes[0],
        scratch_shapes=[pltpu.VMEM((tm, tn), F32)] if nk > 1 else [],
        compiler_params=_cparams(("parallel", "parallel", "arbitrary")),
    )


def rowwise(name, fn, row_ins, const_ins, row_outs, acc_outs=(), ts=None, carry=None):
    s = row_ins[0].shape[0]
    row_outs = [ro if len(ro) == 3 else (*ro, 1) for ro in row_outs]
    if ts is None:
        per_row = sum(a.shape[-1] * a.dtype.itemsize for a in row_ins) + sum(w * jnp.dtype(dt).itemsize for w, dt, _ in row_outs)
        ts = 512
        while ts > 8 and 2 * ts * per_row > 20 * 1024 * 1024:
            ts //= 2
    ts = min(ts, s)
    assert s % ts == 0
    nr, nc, no, na = len(row_ins), len(const_ins), len(row_outs), len(acc_outs)

    def body(*refs):
        rows = [r[...].reshape(ts, r.shape[-1]).astype(F32) for r in refs[:nr]]
        consts = [r[...] for r in refs[nr:nr + nc]]
        outs, accs = fn(rows, consts)
        for r, v in zip(refs[nr + nc:nr + nc + no], outs):
            r[...] = v.astype(r.dtype).reshape(r.shape)
        if na:
            first = pl.program_id(0) == 0
            for r, v in zip(refs[nr + nc + no:], accs):
                @pl.when(first)
                def _(r=r, v=v):
                    r[...] = v

                @pl.when(jnp.logical_not(first))
                def _(r=r, v=v):
                    r[...] += v

    def tile_spec(w, d):
        if d == 1:
            return pl.BlockSpec((ts, w), lambda i: (i, 0))
        return pl.BlockSpec((d, ts // d, w), lambda i: (0, i, 0))

    in_specs = [tile_spec(a.shape[-1], a.shape[0] if a.ndim == 3 else 1) for a in row_ins]
    in_specs += [pl.BlockSpec(c.shape, lambda i, nd=c.ndim: (0,) * nd) for c in const_ins]
    out_specs = [tile_spec(w, d) for w, _, d in row_outs]
    out_specs += [pl.BlockSpec(shp, lambda i, nd=len(shp): (0,) * nd) for shp in acc_outs]
    out_shape = [jax.ShapeDtypeStruct((s, w) if d == 1 else (d, s // d, w), dt) for w, dt, d in row_outs]
    out_shape += [jax.ShapeDtypeStruct(shp, F32) for shp in acc_outs]
    return _run(
        body, [*row_ins, *const_ins], carry=carry, name=name, grid=(s // ts,), in_specs=in_specs, out_specs=out_specs,
        out_shape=out_shape, compiler_params=_cparams(("arbitrary",)),
    )


PERM_TS = 256


def _perm_matrix(ts, d, inverse):
    i = lax.broadcasted_iota(jnp.int32, (ts, ts), 0)
    k = lax.broadcasted_iota(jnp.int32, (ts, ts), 1)
    per = ts // d
    src = (i % d) * per + i // d if inverse else (i % per) * d + i // per
    return jnp.where(k == src, 1.0, 0.0).astype(BF16)


def _permute(p, x):
    if x.dtype == BF16:
        return jnp.dot(p, x, preferred_element_type=F32)
    hi = x.astype(BF16)
    rest = x - hi.astype(F32)
    mid = rest.astype(BF16)
    lo = (rest - mid.astype(F32)).astype(BF16)
    out = jnp.dot(p, hi, preferred_element_type=F32) + jnp.dot(p, mid, preferred_element_type=F32)
    return out + jnp.dot(p, lo, preferred_element_type=F32)


def _rms(x, gain):
    r = lax.rsqrt(jnp.mean(x * x, axis=-1, keepdims=True) + EPS)
    n = x * r
    return n * gain, n, r


def _rms_bwd(dy, n, r, gain):
    dn = dy * gain
    dx = r * (dn - n * jnp.mean(dn * n, axis=-1, keepdims=True))
    return dx, jnp.sum(dy * n, axis=0, keepdims=True)


def _sigmoid(x):
    return 1.0 / (1.0 + jnp.exp(-x))


_GELU_K = math.sqrt(2.0 / math.pi)


def _gelu(x):
    t = jnp.tanh(_GELU_K * (x + 0.044715 * x * x * x))
    return 0.5 * x * (1.0 + t), t


def _gelu_grad(x, t):
    return 0.5 * (1.0 + t) + 0.5 * x * (1.0 - t * t) * _GELU_K * (1.0 + 3.0 * 0.044715 * x * x)


def _head_sum(x):
    parts = []
    for h in range(HEADS_PER_GROUP):
        sl = x[:, h * HEAD_DIM:(h + 1) * HEAD_DIM]
        parts.append(jnp.broadcast_to(jnp.sum(sl, axis=-1, keepdims=True), sl.shape))
    return jnp.concatenate(parts, axis=-1)


def _mix_weights(l0, l1, l2):
    mx = jnp.maximum(jnp.maximum(l0, l1), l2)
    e0, e1, e2 = jnp.exp(l0 - mx), jnp.exp(l1 - mx), jnp.exp(l2 - mx)
    inv = 1.0 / (e0 + e1 + e2)
    return e0 * inv, e1 * inv, e2 * inv


BLK = 128


def _slopes(g):
    return [2.0 ** (-8.0 * (g * HEADS_PER_GROUP + h + 1) / N_HEADS) for h in range(HEADS_PER_GROUP)]


def _attn_masks(dil):
    qi = lax.broadcasted_iota(jnp.int32, (BLK, BLK), 0)
    ki = lax.broadcasted_iota(jnp.int32, (BLK, BLK), 1)
    dist_c = qi - ki
    dist_p = BLK + qi - ki
    return dist_c >= 0, dist_p <= BLK, (dist_c * dil).astype(F32), (dist_p * dil).astype(F32)


def _window_mask(has_prev, dil):
    qi = lax.broadcasted_iota(jnp.int32, (BLK, 2 * BLK), 0)
    ki = lax.broadcasted_iota(jnp.int32, (BLK, 2 * BLK), 1)
    dist = BLK + qi - ki
    ok = jnp.logical_and(jnp.logical_and(dist >= 0, dist <= BLK), jnp.logical_or(ki >= BLK, has_prev))
    return ok, (dist * dil).astype(F32)


def attn_fwd(qkv, g, name):
    dil, length, _ = qkv.shape
    scale = HEAD_DIM ** -0.5
    slopes = _slopes(g)

    def body(q_ref, kc_ref, vc_ref, kp_ref, vp_ref, o_ref, l_ref):
        ok, dist = _window_mask(pl.program_id(1) > 0, dil)
        for h in range(HEADS_PER_GROUP):
            sl = slice(h * HEAD_DIM, (h + 1) * HEAD_DIM)
            k2 = jnp.concatenate([kp_ref[:, sl], kc_ref[:, sl]], axis=0)
            v2 = jnp.concatenate([vp_ref[:, sl], vc_ref[:, sl]], axis=0)
            s = lax.dot_general(q_ref[:, sl], k2, _DN["nt"], preferred_element_type=F32) * scale - slopes[h] * dist
            s = jnp.where(ok, s, NEG)
            mx = jnp.max(s, axis=-1, keepdims=True)
            p = jnp.exp(s - mx)
            den = jnp.sum(p, axis=-1, keepdims=True)
            o_ref[:, sl] = (jnp.dot(p.astype(BF16), v2, preferred_element_type=F32) / den).astype(BF16)
            l_ref[:, sl] = jnp.broadcast_to(mx + jnp.log(den), (BLK, HEAD_DIM))

    def spec(col, prev):
        if prev:
            return pl.BlockSpec((None, BLK, GROUP_W), lambda r, n: (r, jnp.maximum(n - 1, 0), col))
        return pl.BlockSpec((None, BLK, GROUP_W), lambda r, n: (r, n, col))

    out_spec = pl.BlockSpec((None, BLK, GROUP_W), lambda r, n: (r, n, 0))
    return _pcall(
        body, name=name, grid=(dil, length // BLK),
        in_specs=[spec(0, False), spec(1, False), spec(2, False), spec(1, True), spec(2, True)],
        out_specs=[out_spec, out_spec],
        out_shape=[jax.ShapeDtypeStruct((dil, length, GROUP_W), BF16), jax.ShapeDtypeStruct((dil, length, GROUP_W), F32)],
        compiler_params=_cparams(("parallel", "parallel")),
    )(qkv, qkv, qkv, qkv, qkv)


def attn_bwd(qkv, dout, lse, dd, g, name, carry=None):
    dil, length, _ = qkv.shape
    nblk = length // BLK
    scale = HEAD_DIM ** -0.5
    slopes = _slopes(g)

    def body(q_ref, kc_ref, vc_ref, kp_ref, vp_ref, qn_ref, do_ref, don_ref, l_ref, ln_ref, d_ref, dn_ref, o_ref):
        n = pl.program_id(1)
        ok2, dist2 = _window_mask(n > 0, dil)
        _, ok_p, _, dp = _attn_masks(dil)
        ok_next = jnp.logical_and(ok_p, n < nblk - 1)
        for h in range(HEADS_PER_GROUP):
            sl = slice(h * HEAD_DIM, (h + 1) * HEAD_DIM)
            q, kc, vc, qn = q_ref[:, sl], kc_ref[:, sl], vc_ref[:, sl], qn_ref[:, sl]
            k2 = jnp.concatenate([kp_ref[:, sl], kc], axis=0)
            v2 = jnp.concatenate([vp_ref[:, sl], vc], axis=0)
            do, don = do_ref[:, sl], don_ref[:, sl]
            lse_q, lse_n, dd_q, dd_n = l_ref[:, sl], ln_ref[:, sl], d_ref[:, sl], dn_ref[:, sl]

            def probs(qq, kk, dist, ok, lse_t):
                s = lax.dot_general(qq, kk, _DN["nt"], preferred_element_type=F32) * scale - slopes[h] * dist
                return jnp.where(ok, jnp.exp(jnp.where(ok, s, NEG) - lse_t), 0.0)

            p2 = probs(q, k2, dist2, ok2, jnp.concatenate([lse_q, lse_q], axis=1))
            p_x = probs(qn, kc, dp, ok_next, lse_n)
            ds2 = p2 * (lax.dot_general(do, v2, _DN["nt"], preferred_element_type=F32) - jnp.concatenate([dd_q, dd_q], axis=1))
            ds_x = p_x * (lax.dot_general(don, vc, _DN["nt"], preferred_element_type=F32) - dd_n)
            dq = jnp.dot(ds2.astype(BF16), k2, preferred_element_type=F32)
            ds_k = jnp.concatenate([ds2[:, BLK:], ds_x], axis=0).astype(BF16)
            p_k = jnp.concatenate([p2[:, BLK:], p_x], axis=0).astype(BF16)
            dk = lax.dot_general(ds_k, jnp.concatenate([q, qn], axis=0), _DN["tn"], preferred_element_type=F32)
            dv = lax.dot_general(p_k, jnp.concatenate([do, don], axis=0), _DN["tn"], preferred_element_type=F32)
            o_ref[:, h * HEAD_DIM:(h + 1) * HEAD_DIM] = (dq * scale).astype(BF16)
            o_ref[:, GROUP_W + h * HEAD_DIM:GROUP_W + (h + 1) * HEAD_DIM] = (dk * scale).astype(BF16)
            o_ref[:, 2 * GROUP_W + h * HEAD_DIM:2 * GROUP_W + (h + 1) * HEAD_DIM] = dv.astype(BF16)

    def spec(col, which):
        if which == "prev":
            return pl.BlockSpec((None, BLK, GROUP_W), lambda r, n: (r, jnp.maximum(n - 1, 0), col))
        if which == "next":
            return pl.BlockSpec((None, BLK, GROUP_W), lambda r, n: (r, jnp.minimum(n + 1, nblk - 1), col))
        return pl.BlockSpec((None, BLK, GROUP_W), lambda r, n: (r, n, col))

    return _run(
        body, [qkv, qkv, qkv, qkv, qkv, qkv, dout, dout, lse, lse, dd, dd], carry=carry, name=name, grid=(dil, nblk),
        in_specs=[spec(0, "cur"), spec(1, "cur"), spec(2, "cur"), spec(1, "prev"), spec(2, "prev"), spec(0, "next"),
                  spec(0, "cur"), spec(0, "next"), spec(0, "cur"), spec(0, "next"), spec(0, "cur"), spec(0, "next")],
        out_specs=pl.BlockSpec((None, BLK, 3 * GROUP_W), lambda r, n: (r, n, 0)),
        out_shape=jax.ShapeDtypeStruct((dil, length, 3 * GROUP_W), BF16),
        compiler_params=_cparams(("parallel", "parallel")),
    )


def _ssm_prep_values(are, aim, logdt):
    dt = jnp.exp(logdt)
    mag = jnp.exp(are * dt)
    lb_re, lb_im = mag * jnp.cos(aim * dt), mag * jnp.sin(aim * dt)
    inv = 1.0 / (are * are + aim * aim)
    n_re, n_im = lb_re - 1.0, lb_im
    f_re = (n_re * are + n_im * aim) * inv
    f_im = (n_im * are - n_re * aim) * inv
    return dt, lb_re, lb_im, f_re, f_im, inv


PREP_G = 8


def _group_specs(are, logdt, bre):
    def spec(a):
        return pl.BlockSpec((PREP_G,) + a.shape[1:], lambda i: (i, 0, 0))
    return spec(are), spec(logdt), spec(bre)


def ssm_prep(are, aim, logdt, bre, bim):
    def body(are_r, aim_r, ldt_r, bre_r, bim_r, lre_o, lim_o, bbre_o, bbim_o):
        _, lb_re, lb_im, f_re, f_im, _ = _ssm_prep_values(are_r[...], aim_r[...], ldt_r[...])
        lre_o[...] = lb_re
        lim_o[...] = lb_im
        bbre_o[...] = f_re * bre_r[...] - f_im * bim_r[...]
        bbim_o[...] = f_re * bim_r[...] + f_im * bre_r[...]

    sh1 = jax.ShapeDtypeStruct(are.shape, F32)
    shb = jax.ShapeDtypeStruct(bre.shape, F32)
    s1, sd, sb = _group_specs(are, logdt, bre)
    return _pcall(body, name="ssm_prep", grid=(SSM_GROUPS // PREP_G,), in_specs=[s1, s1, sd, sb, sb], out_specs=[s1, s1, sb, sb],
                  out_shape=[sh1, sh1, shb, shb], compiler_params=_cparams(("parallel",)))(are, aim, logdt, bre, bim)


def ssm_prep_bwd(are, aim, logdt, bre, bim, dbbre, dbbim, dlre, dlim):
    def body(are_r, aim_r, ldt_r, bre_r, bim_r, dbbre_r, dbbim_r, dlre_r, dlim_r, dare_o, daim_o, dldt_o, dbre_o, dbim_o):
        are_v, aim_v = are_r[...], aim_r[...]
        dt, lb_re, lb_im, f_re, f_im, inv = _ssm_prep_values(are_v, aim_v, ldt_r[...])
        b_re, b_im, g_re, g_im = bre_r[...], bim_r[...], dbbre_r[...], dbbim_r[...]
        dbre_o[...] = f_re * g_re + f_im * g_im
        dbim_o[...] = f_re * g_im - f_im * g_re
        df_re = jnp.sum(b_re * g_re + b_im * g_im, axis=-1, keepdims=True)
        df_im = jnp.sum(b_re * g_im - b_im * g_re, axis=-1, keepdims=True)
        il_re, il_im = are_v * inv, -aim_v * inv
        cl_re = dlre_r[...] + il_re * df_re + il_im * df_im
        cl_im = dlim_r[...] + il_re * df_im - il_im * df_re
        q_re = -(f_re * il_re - f_im * il_im)
        q_im = -(f_re * il_im + f_im * il_re)
        ca_re = q_re * df_re + q_im * df_im
        ca_im = q_re * df_im - q_im * df_re
        cz_re = lb_re * cl_re + lb_im * cl_im
        cz_im = lb_re * cl_im - lb_im * cl_re
        dare_o[...] = ca_re + dt * cz_re
        daim_o[...] = ca_im + dt * cz_im
        dldt_o[...] = dt * jnp.sum(are_v * cz_re + aim_v * cz_im, axis=1, keepdims=True)

    sh1 = jax.ShapeDtypeStruct(are.shape, F32)
    shb = jax.ShapeDtypeStruct(bre.shape, F32)
    s1, sd, sb = _group_specs(are, logdt, bre)
    return _pcall(
        body, name="ssm_prep_bwd", grid=(SSM_GROUPS // PREP_G,), in_specs=[s1, s1, sd, sb, sb, sb, sb, s1, s1],
        out_specs=[s1, s1, sd, sb, sb], out_shape=[sh1, sh1, jax.ShapeDtypeStruct(logdt.shape, F32), shb, shb],
        compiler_params=_cparams(("parallel",)),
    )(are, aim, logdt, bre, bim, dbbre, dbbim, dlre, dlim)


SCAN_WC = 512


def _chain_segments(a_re, a_im, e_re, e_im, nsq, reverse):
    p_re, p_im = a_re, a_im
    for _ in range(nsq):
        p_re, p_im = p_re * p_re - p_im * p_im, 2.0 * p_re * p_im
    row = lax.broadcasted_iota(jnp.int32, e_re.shape, 0)
    edge = (row == SEGS - 1) if reverse else (row == 0)
    shift = SEGS - 1 if reverse else 1
    c_re, c_im = jnp.zeros_like(e_re), jnp.zeros_like(e_im)
    for _ in range(SEGS - 1):
        n_re = p_re * c_re - p_im * c_im + e_re
        n_im = p_re * c_im + p_im * c_re + e_im
        c_re = jnp.where(edge, 0.0, pltpu.roll(n_re, shift, 0))
        c_im = jnp.where(edge, 0.0, pltpu.roll(n_im, shift, 0))
    return c_re, c_im


def _scan_dims(s):
    steps = s // SEGS
    assert steps & (steps - 1) == 0
    tt = min(128, steps)
    return steps, tt, steps // tt, tt * SEGS, int(math.log2(steps))


U_BLK = SSM_W // BD


def ssm_fwd(u_s, dvec, w_bre, w_bim, w_cre, w_cim_neg, lre, lim, name, carry=None):
    s = u_s.shape[0]
    steps, tt, nch, rows, nsq = _scan_dims(s)
    nb, ub_w, wc = w_bre.shape

    def body(u_r, d_r, bre_r, bim_r, cre_r, cim_r, lre_r, lim_r, yg_o, ys_o, hre_o, him_o, hin_re_o, hin_im_o,
             st_re, st_im, x_re, x_im, h_re, h_im):
        ps, ch = pl.program_id(1), pl.program_id(2)
        a_re = jnp.broadcast_to(lre_r[...], (SEGS, wc))
        a_im = jnp.broadcast_to(lim_r[...], (SEGS, wc))
        ub = u_r[...]
        ub16 = ub.astype(BF16)
        x_re[...] = jnp.dot(ub16, bre_r[...], preferred_element_type=F32)
        x_im[...] = jnp.dot(ub16, bim_r[...], preferred_element_type=F32)

        @pl.when(jnp.logical_and(ps == 0, ch == 0))
        def _():
            st_re[...] = jnp.zeros_like(st_re)
            st_im[...] = jnp.zeros_like(st_im)

        @pl.when(jnp.logical_and(ps == 1, ch == 0))
        def _():
            c_re, c_im = _chain_segments(a_re, a_im, st_re[...], st_im[...], nsq, False)
            st_re[...] = c_re
            st_im[...] = c_im
            hin_re_o[...] = c_re
            hin_im_o[...] = c_im

        def run(store):
            def step(t, hc):
                off = pl.multiple_of(t * SEGS, SEGS)
                n_re = a_re * hc[0] - a_im * hc[1] + x_re[pl.ds(off, SEGS), :]
                n_im = a_re * hc[1] + a_im * hc[0] + x_im[pl.ds(off, SEGS), :]
                if store:
                    h_re[pl.ds(off, SEGS), :] = n_re
                    h_im[pl.ds(off, SEGS), :] = n_im
                return n_re, n_im

            fin = lax.fori_loop(0, tt, step, (st_re[...], st_im[...]))
            st_re[...] = fin[0]
            st_im[...] = fin[1]

        @pl.when(ps == 0)
        def _():
            run(False)

        @pl.when(ps == 1)
        def _():
            run(True)
            hr16, hi16 = h_re[...].astype(BF16), h_im[...].astype(BF16)
            hre_o[...] = hr16
            him_o[...] = hi16
            y = jnp.dot(hr16, cre_r[...], preferred_element_type=F32) + jnp.dot(hi16, cim_r[...], preferred_element_type=F32)
            y = y + d_r[...] * ub
            ys_o[...] = y
            yg_o[...] = _gelu(y)[0].astype(BF16)

    def pass1(ps, c):
        return jnp.where(ps == 1, c, 0)

    u_spec = pl.BlockSpec((rows, ub_w), lambda j, ps, c: (c, j))
    d_spec = pl.BlockSpec((1, ub_w), lambda j, ps, c: (0, j))
    b_spec = pl.BlockSpec((None, ub_w, wc), lambda j, ps, c: (j, 0, 0))
    c_spec = pl.BlockSpec((None, wc, ub_w), lambda j, ps, c: (j, 0, 0))
    l_spec = pl.BlockSpec((1, wc), lambda j, ps, c: (0, j))
    y_spec = pl.BlockSpec((rows, ub_w), lambda j, ps, c: (pass1(ps, c), j))
    h_spec = pl.BlockSpec((rows, wc), lambda j, ps, c: (pass1(ps, c), j))
    e_spec = pl.BlockSpec((SEGS, wc), lambda j, ps, c: (0, j))
    return _run(
        body, [u_s, dvec, w_bre, w_bim, w_cre, w_cim_neg, lre, lim], carry=carry, name=name, grid=(nb, 2, nch),
        in_specs=[u_spec, d_spec, b_spec, b_spec, c_spec, c_spec, l_spec, l_spec],
        out_specs=[y_spec, y_spec, h_spec, h_spec, e_spec, e_spec],
        out_shape=[jax.ShapeDtypeStruct((s, SSM_W), BF16), jax.ShapeDtypeStruct((s, SSM_W), F32),
                   jax.ShapeDtypeStruct((s, STATE_W), BF16), jax.ShapeDtypeStruct((s, STATE_W), BF16),
                   jax.ShapeDtypeStruct((SEGS, STATE_W), F32), jax.ShapeDtypeStruct((SEGS, STATE_W), F32)],
        scratch_shapes=[pltpu.VMEM((SEGS, wc), F32)] * 2 + [pltpu.VMEM((rows, wc), F32)] * 4,
        compiler_params=_cparams(("parallel", "arbitrary", "arbitrary")),
    )


def ssm_bwd(dyg_s, ys, u_s, h_re, h_im, hin_re, hin_im, gin_re, gin_im, dvec, w_bre_t, w_bim_t, w_cre_t, w_cim_neg_t, lre, lim,
            name, carry=None):
    s = u_s.shape[0]
    steps, tt, nch, rows, nsq = _scan_dims(s)
    half = 2 * SEGS

    def body(dyg_r, ys_r, u_r, hre_r, him_r, pre_r, pim_r, cin_re_r, cin_im_r, gin_re_r, gin_im_r, d_r, bre_r, bim_r, cre_r,
             cim_r, lre_r, lim_r, du_o, dbre_o, dbim_o, dcre_o, dcim_o, dlre_o, dlim_o, dd_o,
             st_re, st_im, x_re, x_im, g_re, g_im, hf_re, hf_im):
        ch = pl.program_id(1)
        a_re = jnp.broadcast_to(lre_r[...], (SEGS, SCAN_WC))
        a_im = -jnp.broadcast_to(lim_r[...], (SEGS, SCAN_WC))
        ub, y = u_r[...], ys_r[...]
        dy = dyg_r[...] * _gelu_grad(y, _gelu(y)[1])
        dy16 = dy.astype(BF16)
        x_re[...] = jnp.dot(dy16, cre_r[...], preferred_element_type=F32)
        x_im[...] = jnp.dot(dy16, cim_r[...], preferred_element_type=F32)

        @pl.when(ch == 0)
        def _():
            st_re[...] = gin_re_r[...]
            st_im[...] = gin_im_r[...]
            dlre_o[...] = jnp.zeros_like(dlre_o)
            dlim_o[...] = jnp.zeros_like(dlim_o)

        hf_re[...] = hre_r[...].astype(F32)
        hf_im[...] = him_r[...].astype(F32)
        first_chunk = ch == nch - 1
        edge_re = jnp.where(first_chunk, cin_re_r[...], pre_r[...].astype(F32)[SEGS:, :])
        edge_im = jnp.where(first_chunk, cin_im_r[...], pim_r[...].astype(F32)[SEGS:, :])

        def step(i, hc):
            t = tt - 1 - i
            off = pl.multiple_of(t * SEGS, SEGS)
            n_re = a_re * hc[0] - a_im * hc[1] + x_re[pl.ds(off, SEGS), :]
            n_im = a_re * hc[1] + a_im * hc[0] + x_im[pl.ds(off, SEGS), :]
            g_re[pl.ds(off, SEGS), :] = n_re
            g_im[pl.ds(off, SEGS), :] = n_im
            offp = pl.multiple_of(jnp.maximum(t - 1, 0) * SEGS, SEGS)
            hp_re = jnp.where(t == 0, edge_re, hf_re[pl.ds(offp, SEGS), :])
            hp_im = jnp.where(t == 0, edge_im, hf_im[pl.ds(offp, SEGS), :])
            return n_re, n_im, hc[2] + hp_re * n_re + hp_im * n_im, hc[3] + hp_re * n_im - hp_im * n_re

        fin = lax.fori_loop(0, tt, step, (st_re[...], st_im[...], dlre_o[...], dlim_o[...]))
        st_re[...] = fin[0]
        st_im[...] = fin[1]
        dlre_o[...] = fin[2]
        dlim_o[...] = fin[3]

        gr16, gi16 = g_re[...].astype(BF16), g_im[...].astype(BF16)
        du = jnp.dot(gr16, bre_r[...], preferred_element_type=F32) + jnp.dot(gi16, bim_r[...], preferred_element_type=F32)
        du_o[...] = du + d_r[...] * dy
        ub16 = ub.astype(BF16)
        parts = [
            (dbre_o, lax.dot_general(ub16, gr16, _DN["tn"], preferred_element_type=F32)),
            (dbim_o, lax.dot_general(ub16, gi16, _DN["tn"], preferred_element_type=F32)),
            (dcre_o, lax.dot_general(hre_r[...], dy16, _DN["tn"], preferred_element_type=F32)),
            (dcim_o, lax.dot_general(him_r[...], dy16, _DN["tn"], preferred_element_type=F32)),
            (dd_o, jnp.sum(dy * ub, axis=0, keepdims=True)),
        ]
        for ref, val in parts:
            @pl.when(ch == 0)
            def _(ref=ref, val=val):
                ref[...] = val

            @pl.when(ch > 0)
            def _(ref=ref, val=val):
                ref[...] += val

    def chunk(c):
        return nch - 1 - c

    u_spec = pl.BlockSpec((rows, U_BLK), lambda j, c: (chunk(c), j))
    h_spec = pl.BlockSpec((rows, SCAN_WC), lambda j, c: (chunk(c), j))
    prev_spec = pl.BlockSpec((half, SCAN_WC), lambda j, c: (jnp.maximum(chunk(c) * (rows // half) - 1, 0), j))
    e_spec = pl.BlockSpec((SEGS, SCAN_WC), lambda j, c: (0, j))
    d_spec = pl.BlockSpec((1, U_BLK), lambda j, c: (0, j))
    bt_spec = pl.BlockSpec((None, SCAN_WC, U_BLK), lambda j, c: (j, 0, 0))
    ct_spec = pl.BlockSpec((None, U_BLK, SCAN_WC), lambda j, c: (j, 0, 0))
    l_spec = pl.BlockSpec((1, SCAN_WC), lambda j, c: (0, j))
    return _run(
        body, [dyg_s, ys, u_s, h_re, h_im, h_re, h_im, hin_re, hin_im, gin_re, gin_im, dvec, w_bre_t, w_bim_t, w_cre_t,
               w_cim_neg_t, lre, lim],
        carry=carry, name=name, grid=(BD, nch),
        in_specs=[u_spec, u_spec, u_spec, h_spec, h_spec, prev_spec, prev_spec, e_spec, e_spec, e_spec, e_spec, d_spec,
                  bt_spec, bt_spec, ct_spec, ct_spec, l_spec, l_spec],
        out_specs=[u_spec, ct_spec, ct_spec, bt_spec, bt_spec, e_spec, e_spec, d_spec],
        out_shape=[jax.ShapeDtypeStruct((s, SSM_W), F32)] + [jax.ShapeDtypeStruct((BD, U_BLK, SCAN_WC), F32)] * 2
        + [jax.ShapeDtypeStruct((BD, SCAN_WC, U_BLK), F32)] * 2 + [jax.ShapeDtypeStruct((SEGS, STATE_W), F32)] * 2
        + [jax.ShapeDtypeStruct((1, SSM_W), F32)],
        scratch_shapes=[pltpu.VMEM((SEGS, SCAN_WC), F32)] * 2 + [pltpu.VMEM((rows, SCAN_WC), F32)] * 6,
        compiler_params=_cparams(("parallel", "arbitrary")),
    )


def ssm_bwd_ends(dyg_s, ys, w_cre_t, w_cim_neg_t, lre, lim, name, carry=None):
    s = ys.shape[0]
    steps, tt, nch, rows, nsq = _scan_dims(s)
    nb, ub_w, wc = w_cre_t.shape

    def body(dyg_r, ys_r, cre_r, cim_r, lre_r, lim_r, gin_re_o, gin_im_o, st_re, st_im, x_re, x_im):
        ch = pl.program_id(1)
        a_re = jnp.broadcast_to(lre_r[...], (SEGS, wc))
        a_im = -jnp.broadcast_to(lim_r[...], (SEGS, wc))
        y = ys_r[...]
        dy16 = (dyg_r[...] * _gelu_grad(y, _gelu(y)[1])).astype(BF16)
        x_re[...] = jnp.dot(dy16, cre_r[...], preferred_element_type=F32)
        x_im[...] = jnp.dot(dy16, cim_r[...], preferred_element_type=F32)

        @pl.when(ch == 0)
        def _():
            st_re[...] = jnp.zeros_like(st_re)
            st_im[...] = jnp.zeros_like(st_im)

        def step(i, hc):
            off = pl.multiple_of((tt - 1 - i) * SEGS, SEGS)
            return (a_re * hc[0] - a_im * hc[1] + x_re[pl.ds(off, SEGS), :],
                    a_re * hc[1] + a_im * hc[0] + x_im[pl.ds(off, SEGS), :])

        fin = lax.fori_loop(0, tt, step, (st_re[...], st_im[...]))
        st_re[...] = fin[0]
        st_im[...] = fin[1]

        @pl.when(ch == nch - 1)
        def _():
            c_re, c_im = _chain_segments(a_re, a_im, fin[0], fin[1], nsq, True)
            gin_re_o[...] = c_re
            gin_im_o[...] = c_im

    y_spec = pl.BlockSpec((rows, ub_w), lambda j, c: (nch - 1 - c, j))
    ct_spec = pl.BlockSpec((None, ub_w, wc), lambda j, c: (j, 0, 0))
    l_spec = pl.BlockSpec((1, wc), lambda j, c: (0, j))
    e_spec = pl.BlockSpec((SEGS, wc), lambda j, c: (0, j))
    return _run(
        body, [dyg_s, ys, w_cre_t, w_cim_neg_t, lre, lim], carry=carry, name=name, grid=(nb, nch),
        in_specs=[y_spec, y_spec, ct_spec, ct_spec, l_spec, l_spec], out_specs=[e_spec, e_spec],
        out_shape=[jax.ShapeDtypeStruct((SEGS, STATE_W), F32)] * 2,
        scratch_shapes=[pltpu.VMEM((SEGS, wc), F32)] * 2 + [pltpu.VMEM((rows, wc), F32)] * 2,
        compiler_params=_cparams(("parallel", "arbitrary")),
    )


FWD_BD = 4


def _block_diag(m, nb=BD):
    g, r, c = m.shape
    m = m.reshape(nb, g // nb, r, c)
    eye = jnp.eye(g // nb, dtype=m.dtype)
    return jnp.einsum("jarc,ab->jarbc", m, eye).reshape(nb, (g // nb) * r, (g // nb) * c)


def _block_diag_extract(m, r, c):
    per = m.shape[1] // r
    m = m.reshape(BD, per, r, per, c)
    return jnp.einsum("jarac->jarc", m).reshape(BD * per, r, c)


def to_segments(a):
    s, w = a.shape
    return a.reshape(SEGS, s // SEGS, w).transpose(1, 0, 2).reshape(s, w)


def from_segments(a):
    s, w = a.shape
    return a.reshape(s // SEGS, SEGS, w).transpose(1, 0, 2).reshape(s, w)


W_IN_CHUNK_ROWS = (320, 320, 320, 240, 576, 272)
FFN_GATE_ROWS_FIRST = 480
TALL_TM = 2048
FFN_TN = 512


def local_step(x, target, shards, small):
    s = x.shape[0]
    g1, g2, g3, g4 = (small[k].reshape(1, D_MODEL) for k in ("norm_mix_pre", "norm_mix_post", "norm_ffn_pre", "norm_ffn_post"))
    dvec = small["ssm_d"].reshape(1, SSM_W)
    wts, recv = {}, {}

    def gathered(names, blocks):
        for n, b in zip(names, blocks):
            wts[n] = _full_from_gathered(b, n)

    def rms_in_fn(r, c):
        hh = _rms(r[0], c[0])[0].astype(BF16)
        return [hh, _permute(_perm_matrix(PERM_TS, 4, False), hh), _permute(_perm_matrix(PERM_TS, 16, False), hh)], []

    (h, h4, h16), got = rowwise("rms_in", rms_in_fn, [x], [g1], [(D_MODEL, BF16), (D_MODEL, BF16, 4), (D_MODEL, BF16, 16)],
                                ts=PERM_TS, carry=Gather([shards["w_in"]]))
    w_in_t = _full_from_gathered(got[0], "w_in")
    w_u_t, w_gates_t = w_in_t[3 * HQ:3 * HQ + SSM_W], w_in_t[3 * HQ + SSM_W:]

    def qkv_rows(g):
        return 3 * GROUP_W, lambda t: 3 * t + g

    hd = [h.reshape(1, s, D_MODEL), h4, h16]
    qkv = [None] * 3
    names = ("w_attn_up", "w_glu_v", "w_glu_g")
    qkv[0], got = mm([(hd[0].reshape(s, D_MODEL), w_in_t)], "nt", BF16, "mm_qkv0", tm=TALL_TM, tn=GROUP_W, b_window=qkv_rows(0),
                     carry=Gather([shards[n] for n in names]))
    gathered(names, got)
    qkv[1], got = mm([(hd[1].reshape(s, D_MODEL), w_in_t)], "nt", BF16, "mm_qkv1", tm=TALL_TM, tn=GROUP_W, b_window=qkv_rows(1),
                     carry=Gather([shards["w_out"]]))
    gathered(("w_out",), got)
    qkv[2] = mm([(hd[2].reshape(s, D_MODEL), w_in_t)], "nt", BF16, "mm_qkv2", tm=TALL_TM, tn=GROUP_W, b_window=qkv_rows(2))
    u = mm([(h, w_u_t)], "nt", F32, "mm_u")
    gates, got = mm([(h, w_gates_t)], "nt", BF16, "mm_gates", carry=Gather([shards["w_ffn_gate"]]))
    gathered(("w_ffn_gate",), got)

    outs, lses = [], []
    for g, (_, dil) in enumerate(ATTN_GROUPS):
        o, l = attn_fwd(qkv[g].reshape(dil, s // dil, 3 * GROUP_W), g, f"attn_fwd{g}")
        outs.append(o.reshape(s, GROUP_W) if dil == 1 else o)
        lses.append(l.reshape(s, GROUP_W) if dil == 1 else l)

    def natural(r):
        back4, back16 = _perm_matrix(PERM_TS, 4, True), _perm_matrix(PERM_TS, 16, True)
        return (r[0], _permute(back4, r[1].astype(BF16)), _permute(back16, r[2].astype(BF16)),
                r[3], _permute(back4, r[4]), _permute(back16, r[5]))

    def merge_fn(r, c):
        o0, o1, o2, l0, l1, l2 = natural(r)
        w0, w1, w2 = _mix_weights(l0, l1, l2)
        return [w0 * o0 + w1 * o1 + w2 * o2], []

    (attn,) = rowwise("attn_merge", merge_fn, outs + lses, [], [(GROUP_W, BF16)], ts=PERM_TS)
    attn_branch = mm([(attn, wts["w_attn_up"])], "nn", BF16, "mm_up", tm=TALL_TM)

    are3 = small["ssm_a_re"].reshape(SSM_GROUPS, SSM_STATE, 1)
    aim3 = small["ssm_a_im"].reshape(SSM_GROUPS, SSM_STATE, 1)
    ldt3 = small["ssm_log_dt"].reshape(SSM_GROUPS, 1, 1)
    bre3 = small["ssm_b_re"].reshape(SSM_GROUPS, SSM_STATE, SSM_GROUP)
    bim3 = small["ssm_b_im"].reshape(SSM_GROUPS, SSM_STATE, SSM_GROUP)
    cre3 = small["ssm_c_re"].reshape(SSM_GROUPS, SSM_GROUP, SSM_STATE)
    cim3 = small["ssm_c_im"].reshape(SSM_GROUPS, SSM_GROUP, SSM_STATE)
    lre3, lim3, bbre, bbim = ssm_prep(are3, aim3, ldt3, bre3, bim3)
    lre, lim = lre3.reshape(1, STATE_W), lim3.reshape(1, STATE_W)
    w_bre = _block_diag(bbre.transpose(0, 2, 1)).astype(BF16)
    w_bim = _block_diag(bbim.transpose(0, 2, 1)).astype(BF16)
    w_cre = _block_diag(cre3.transpose(0, 2, 1)).astype(BF16)
    w_cim = _block_diag(cim3.transpose(0, 2, 1)).astype(BF16)
    u_s = to_segments(u)
    fwd_w = [_block_diag(t.transpose(0, 2, 1), FWD_BD).astype(BF16) for t in (bbre, bbim, cre3, -cim3)]
    (yg_s, y_ssm, h_re, h_im, hin_re, hin_im), got = ssm_fwd(
        u_s, dvec, *fwd_w, lre, lim, "ssm_fwd", carry=Gather([shards["w_ffn_up"]], pass_early=True))
    gathered(("w_ffn_up",), got)
    yg = from_segments(yg_s)
    gv = mm([(yg, wts["w_glu_v"])], "nn", BF16, "mm_glu_v", tm=TALL_TM)
    gg = mm([(yg, wts["w_glu_g"])], "nn", BF16, "mm_glu_g", tm=TALL_TM)

    def gate_fn(r, c):
        gts, ab, gv_, gg_ = r
        sa, ss = _sigmoid(gts[:, :D_MODEL]), _sigmoid(gts[:, D_MODEL:])
        return [sa * ab + ss * (gv_ * _sigmoid(gg_))], []

    (merged,) = rowwise("gate_merge", gate_fn, [gates, attn_branch, gv, gg], [], [(D_MODEL, BF16)])
    o_mix = mm([(merged, wts["w_out"])], "nn", F32, "mm_out")

    def mid_fn(r, c):
        x1 = r[0] + _rms(r[1], c[0])[0]
        return [x1, _rms(x1, c[1])[0]], []

    x1, h2 = rowwise("rms_mid", mid_fn, [x, o_mix], [g2, g3], [(D_MODEL, F32), (D_MODEL, BF16)])
    (fa, fb, fin), got = mm([(h2, wts["w_ffn_gate"]), (h2, wts["w_ffn_up"])], "nt", [BF16, BF16, BF16], "mm_ffn_in", tn=FFN_TN,
                            epilogue=lambda p, e: [p[0], p[1], p[0] * _sigmoid(p[0]) * p[1]],
                            carry=Gather([shards["w_ffn_down"]], pass_early=True))
    gathered(("w_ffn_down",), got)
    f = mm([(fin, wts["w_ffn_down"])], "nn", F32, "mm_ffn_down", tn=512, tk=D_FF)

    def loss_fn(r, c):
        x1_, f_, tgt = r
        y, n, rr = _rms(f_, c[0])
        err = x1_ + y - tgt
        dout = err * (1.0 / D_MODEL)
        df, dg = _rms_bwd(dout, n, rr, c[0])
        lp = 0.5 * jnp.sum(jnp.sum(err * err, axis=-1, keepdims=True) * (1.0 / D_MODEL), axis=0, keepdims=True)
        return [df, dout], [dg, lp]

    df, dout, dg4, loss_part = rowwise("loss_bwd", loss_fn, [x1, f, target], [g4], [(D_MODEL, BF16), (D_MODEL, BF16)],
                                       acc_outs=[(1, D_MODEL), (1, 1)])
    def sent(names, blocks):
        for n, b in zip(names, blocks):
            recv[n] = b

    def to_owners(names, dws):
        return AllToAll([_split_for_devices(d, n) for n, d in zip(names, dws)])

    def swiglu_bwd(p, e):
        dfin_, (a, b) = p[0], e
        sg = _sigmoid(a)
        return [dfin_ * b * (sg * (1.0 + a * (1.0 - sg))), dfin_ * a * sg]

    da, db = mm([(df, wts["w_ffn_down"])], "nt", [BF16, BF16], "mm_d_fin", tn=FFN_TN, epilogue=swiglu_bwd, extras=[fa, fb])
    dw_ffn_down = mm([(fin, df)], "tn", BF16, "mm_dw_ffn_down")
    dh2, got = mm([(da, wts["w_ffn_gate"]), (db, wts["w_ffn_up"])], "nn", F32, "mm_d_h2", tm=512, tn=1024, tk=D_FF // 2,
                  carry=to_owners(["w_ffn_down"], [dw_ffn_down]))
    sent(["w_ffn_down"], got)
    dw_ffn_gate = mm([(da, h2)], "tn", BF16, "mm_dw_ffn_gate")
    gate_blocks = _split_for_devices(dw_ffn_gate, "w_ffn_gate")
    dw_ffn_up, (gate_landed,) = mm([(db, h2)], "tn", BF16, "mm_dw_ffn_up",
                                   carry=RowsToOwners(gate_blocks, 0, FFN_GATE_ROWS_FIRST))

    def mid_bwd(r, c):
        dh2_, dout_, x1_, o_ = r
        _, n3, r3 = _rms(x1_, c[1])
        dx1, dg3_ = _rms_bwd(dh2_, n3, r3, c[1])
        dx1 = dx1 + dout_
        _, n2, r2 = _rms(o_, c[0])
        do_, dg2_ = _rms_bwd(dx1, n2, r2, c[0])
        return [dx1, do_], [dg2_, dg3_]

    rest = gate_blocks.shape[1] - FFN_GATE_ROWS_FIRST
    (dx1, do_mix, dg2, dg3), (gate_landed,) = rowwise(
        "rms_mid_bwd", mid_bwd, [dh2, dout, x1, o_mix], [g2, g3], [(D_MODEL, F32), (D_MODEL, BF16)],
        acc_outs=[(1, D_MODEL), (1, D_MODEL)], carry=RowsToOwners(gate_blocks, FFN_GATE_ROWS_FIRST, rest, into=gate_landed))
    recv["w_ffn_gate"] = gate_landed
    dmerged = mm([(do_mix, wts["w_out"])], "nt", BF16, "mm_d_merged")
    dw_out = mm([(merged, do_mix)], "tn", BF16, "mm_dw_out")

    def gate_bwd(r, c):
        dm, gts, ab, gv_, gg_ = r
        sa, ss, sg = _sigmoid(gts[:, :D_MODEL]), _sigmoid(gts[:, D_MODEL:]), _sigmoid(gg_)
        branch = gv_ * sg
        dbranch = dm * ss
        dgates = jnp.concatenate([dm * ab * sa * (1.0 - sa), dm * branch * ss * (1.0 - ss)], axis=-1)
        return [dgates, dm * sa, dbranch * sg, dbranch * gv_ * sg * (1.0 - sg)], []

    dgates, dab, dgv, dgg = rowwise("gate_bwd", gate_bwd, [dmerged, gates, attn_branch, gv, gg], [],
                                    [(2 * D_MODEL, BF16), (D_MODEL, BF16), (D_MODEL, BF16), (D_MODEL, BF16)])
    dattn = mm([(dab, wts["w_attn_up"])], "nt", F32, "mm_d_attn")
    dw_up = mm([(attn, dab)], "tn", BF16, "mm_dw_up")
    dyg = mm([(dgv, wts["w_glu_v"]), (dgg, wts["w_glu_g"])], "nt", F32, "mm_d_yg")
    dw_glu_v = mm([(yg, dgv)], "tn", BF16, "mm_dw_glu_v")
    dw_glu_g = mm([(yg, dgg)], "tn", BF16, "mm_dw_glu_g")

    dyg_s = to_segments(dyg)
    (gin_re, gin_im), got = ssm_bwd_ends(dyg_s, y_ssm, fwd_w[2].transpose(0, 2, 1), fwd_w[3].transpose(0, 2, 1), lre, lim,
                                         "ssm_bwd_ends", carry=to_owners(["w_out"], [dw_out]))
    sent(["w_out"], got)
    (du_s, dbre_d, dbim_d, dcre_d, dcim_d, dl_re8, dl_im8, dd_ssm), got = ssm_bwd(
        dyg_s, y_ssm, u_s, h_re, h_im, hin_re, hin_im, gin_re, gin_im, dvec, w_bre.transpose(0, 2, 1), w_bim.transpose(0, 2, 1),
        w_cre.transpose(0, 2, 1), -w_cim.transpose(0, 2, 1), lre, lim, "ssm_bwd", carry=to_owners(["w_ffn_up"], [dw_ffn_up]))
    sent(["w_ffn_up"], got)
    dbb_re = _block_diag_extract(dbre_d, SSM_GROUP, SSM_STATE).transpose(0, 2, 1)
    dbb_im = _block_diag_extract(dbim_d, SSM_GROUP, SSM_STATE).transpose(0, 2, 1)
    dc_re = _block_diag_extract(dcre_d, SSM_STATE, SSM_GROUP).transpose(0, 2, 1)
    dc_im = -_block_diag_extract(dcim_d, SSM_STATE, SSM_GROUP).transpose(0, 2, 1)

    def fold8(r, c):
        return [], [jnp.sum(r[0], axis=0, keepdims=True), jnp.sum(r[1], axis=0, keepdims=True)]

    dl_re, dl_im = rowwise("ssm_dl_fold", fold8, [dl_re8, dl_im8], [], [], acc_outs=[(1, STATE_W), (1, STATE_W)], ts=SEGS)
    da_re, da_im, dldt, db_re, db_im = ssm_prep_bwd(
        are3, aim3, ldt3, bre3, bim3, dbb_re, dbb_im,
        dl_re.reshape(SSM_GROUPS, SSM_STATE, 1), dl_im.reshape(SSM_GROUPS, SSM_STATE, 1))
    du = from_segments(du_s)

    def merge_bwd(r, c):
        dat = r[0]
        o0, o1, o2, l0, l1, l2 = natural(r[1:])
        w0, w1, w2 = _mix_weights(l0, l1, l2)
        tot = _head_sum(dat * (w0 * o0 + w1 * o1 + w2 * o2))
        to4, to16 = _perm_matrix(PERM_TS, 4, False), _perm_matrix(PERM_TS, 16, False)
        return [w0 * dat, _permute(to4, (w1 * dat).astype(BF16)), _permute(to16, (w2 * dat).astype(BF16)),
                w0 * tot, _permute(to4, (w1 * tot).astype(BF16)), _permute(to16, (w2 * tot).astype(BF16))], []

    mb = rowwise("attn_merge_bwd", merge_bwd, [dattn] + outs + lses, [],
                 [(GROUP_W, BF16), (GROUP_W, BF16, 4), (GROUP_W, BF16, 16), (GROUP_W, BF16), (GROUP_W, BF16, 4), (GROUP_W, BF16, 16)],
                 ts=PERM_TS)
    dqs, dw_qkv = [], []
    names = ["w_glu_v", "w_glu_g", "w_attn_up"]
    for g, (_, dil) in enumerate(ATTN_GROUPS):
        dq = attn_bwd(qkv[g].reshape(dil, s // dil, 3 * GROUP_W), mb[g].reshape(dil, s // dil, GROUP_W),
                      lses[g].reshape(dil, s // dil, GROUP_W), mb[3 + g].reshape(dil, s // dil, GROUP_W),
                      g, f"attn_bwd{g}", carry=to_owners(names, [dw_glu_v, dw_glu_g, dw_up]) if g == 1 else None)
        if g == 1:
            dq, got = dq
            sent(names, got)
        dq = dq.reshape(s, 3 * GROUP_W)
        dqs.append(dq)
        dw_qkv.append(mm([(hd[g].reshape(s, D_MODEL), dq)], "tn", BF16, f"mm_dw_qkv{g}"))
    dw_u = mm([(h, du)], "tn", BF16, "mm_dw_u")
    dw_gates = mm([(h, dgates)], "tn", BF16, "mm_dw_gates")
    dw_in = jnp.concatenate(
        [dw_qkv[g][:, o * GROUP_W:(o + 1) * GROUP_W] for o in range(3) for g in range(3)] + [dw_u, dw_gates], axis=1)
    dw_in_blocks = _split_for_devices(dw_in, "w_in")
    starts = [sum(W_IN_CHUNK_ROWS[:i]) for i in range(len(W_IN_CHUNK_ROWS))]
    landed = None

    def chunk(i):
        return RowsToOwners(dw_in_blocks, starts[i], W_IN_CHUNK_ROWS[i], into=landed)

    dh_parts = []
    for g, (_, dil) in enumerate(ATTN_GROUPS):
        dh_g, (landed,) = mm([(dqs[g], w_in_t)], "nn", BF16, f"mm_d_h_qkv{g}", tk=GROUP_W, b_window=qkv_rows(g), carry=chunk(g))
        dh_parts.append(dh_g if dil == 1 else dh_g.reshape(dil, s // dil, D_MODEL))
    dh_u, (landed,) = mm([(du, w_u_t)], "nn", BF16, "mm_d_h_u", carry=chunk(3))
    dh_gates, (landed,) = mm([(dgates, w_gates_t)], "nn", BF16, "mm_d_h_gates", carry=chunk(4))
    dh_parts += [dh_u, dh_gates]

    def in_bwd(r, c):
        dh1 = _permute(_perm_matrix(PERM_TS, 4, True), r[1].astype(BF16))
        dh2_ = _permute(_perm_matrix(PERM_TS, 16, True), r[2].astype(BF16))
        dh = r[0] + dh1 + dh2_ + r[3] + r[4]
        _, n1, r1 = _rms(r[6], c[0])
        dx, dg1_ = _rms_bwd(dh, n1, r1, c[0])
        return [dx + r[5]], [dg1_]

    (grad_x, dg1), (landed,) = rowwise("rms_in_bwd", in_bwd, dh_parts + [dx1, x], [g1], [(D_MODEL, F32)],
                                       acc_outs=[(1, D_MODEL)], ts=PERM_TS, carry=chunk(5))
    recv["w_in"] = landed

    dsmall = dict(norm_mix_pre=dg1, ssm_a_re=da_re, ssm_a_im=da_im, ssm_log_dt=dldt, ssm_b_re=db_re, ssm_b_im=db_im,
                  ssm_c_re=dc_re, ssm_c_im=dc_im, ssm_d=dd_ssm, norm_mix_post=dg2, norm_ffn_pre=dg3, norm_ffn_post=dg4)
    return loss_part, grad_x, recv, dsmall


def adamw(parts, w, m, v, name, carry=None):
    r, c = w.shape
    tr = r
    while tr > 8 and tr % 2 == 0 and tr * c * (8 * parts.dtype.itemsize + 28) * 2 > 24 * 1024 * 1024:
        tr //= 2
    assert r % tr == 0 and (tr % 8 == 0 or tr == r)
    c1, c2 = 1.0 / (1.0 - ADAM_B1 ** ADAM_STEP), 1.0 / (1.0 - ADAM_B2 ** ADAM_STEP)

    def body(p_ref, w_ref, m_ref, v_ref, g_o, d_o, m_o, v_o):
        g = p_ref[0].astype(F32)
        for i in range(1, N_DEV):
            g = g + p_ref[i].astype(F32)
        mn = ADAM_B1 * m_ref[...] + (1.0 - ADAM_B1) * g
        vn = ADAM_B2 * v_ref[...] + (1.0 - ADAM_B2) * (g * g)
        g_o[...] = g
        m_o[...] = mn
        v_o[...] = vn
        d_o[...] = -ADAM_LR * ((mn * c1) / (jnp.sqrt(vn * c2) + ADAM_EPS) + ADAM_WD * w_ref[...])

    blk = pl.BlockSpec((tr, c), lambda i: (i, 0))
    return _run(
        body, [parts, w, m, v], carry=carry, name=name, grid=(r // tr,),
        in_specs=[pl.BlockSpec((N_DEV, tr, c), lambda i: (0, i, 0)), blk, blk, blk],
        out_specs=[blk] * 4, out_shape=[jax.ShapeDtypeStruct((r, c), F32)] * 4, compiler_params=_cparams(("parallel",)),
    )


PACK_C = 1024
SHARDED = ("w_in", "w_attn_up", "w_glu_v", "w_glu_g", "w_out", "w_ffn_gate", "w_ffn_up", "w_ffn_down")
ROW_SHARDED = ("w_out", "w_ffn_down")
SENT_TRANSPOSED = ("w_in", "w_ffn_gate", "w_ffn_up")
GRAD_TRANSPOSED = ("w_ffn_gate", "w_ffn_up")
SMALL = ("norm_mix_pre", "ssm_a_re", "ssm_a_im", "ssm_log_dt", "ssm_b_re", "ssm_b_im", "ssm_c_re", "ssm_c_im", "ssm_d",
         "norm_mix_post", "norm_ffn_pre", "norm_ffn_post")
WEIGHTS = ("norm_mix_pre", "w_in", "w_attn_up", "ssm_a_re", "ssm_a_im", "ssm_log_dt", "ssm_b_re", "ssm_b_im", "ssm_c_re",
           "ssm_c_im", "ssm_d", "w_glu_v", "w_glu_g", "w_out", "norm_mix_post", "norm_ffn_pre", "w_ffn_gate", "w_ffn_up",
           "w_ffn_down", "norm_ffn_post")


def _pack(arrs, dtype, pad_rows_to=64):
    flat = jnp.concatenate([a.reshape(-1).astype(dtype) for a in arrs])
    n = flat.shape[0]
    rows = -(-n // PACK_C)
    rows = -(-rows // pad_rows_to) * pad_rows_to
    return jnp.pad(flat, (0, rows * PACK_C - n)).reshape(rows, PACK_C)


def _unpack(flat2d, shapes):
    flat = flat2d.reshape(-1)
    out, off = [], 0
    for shp in shapes:
        n = int(np.prod(shp))
        out.append(flat[off:off + n].reshape(shp))
        off += n
    return out


def _full_from_gathered(gathered, name):
    if name in ROW_SHARDED or name in SENT_TRANSPOSED:
        return gathered.reshape(-1, gathered.shape[2])
    return gathered.transpose(1, 0, 2).reshape(gathered.shape[1], -1)


def _split_for_devices(full, name):
    if name in ROW_SHARDED or name in GRAD_TRANSPOSED:
        return full.reshape(N_DEV, -1, full.shape[1])
    return full.reshape(full.shape[0], N_DEV, -1).transpose(1, 0, 2)


def kernel(x, norm_mix_pre, w_in, w_attn_up, ssm_a_re, ssm_a_im, ssm_log_dt, ssm_b_re, ssm_b_im, ssm_c_re, ssm_c_im, ssm_d, w_glu_v, w_glu_g, w_out, norm_mix_post, norm_ffn_pre, w_ffn_gate, w_ffn_up, w_ffn_down, norm_ffn_post, loss_target, m_norm_mix_pre, m_w_in, m_w_attn_up, m_ssm_a_re, m_ssm_a_im, m_ssm_log_dt, m_ssm_b_re, m_ssm_b_im, m_ssm_c_re, m_ssm_c_im, m_ssm_d, m_w_glu_v, m_w_glu_g, m_w_out, m_norm_mix_post, m_norm_ffn_pre, m_w_ffn_gate, m_w_ffn_up, m_w_ffn_down, m_norm_ffn_post, v_norm_mix_pre, v_w_in, v_w_attn_up, v_ssm_a_re, v_ssm_a_im, v_ssm_log_dt, v_ssm_b_re, v_ssm_b_im, v_ssm_c_re, v_ssm_c_im, v_ssm_d, v_w_glu_v, v_w_glu_g, v_w_out, v_norm_mix_post, v_norm_ffn_pre, v_w_ffn_gate, v_w_ffn_up, v_w_ffn_down, v_norm_ffn_post):
    args = dict(locals())
    wv = {n: args[n][0] for n in WEIGHTS}
    mv = {n: args["m_" + n][0] for n in WEIGHTS}
    vv = {n: args["v_" + n][0] for n in WEIGHTS}

    shards = {n: (wv[n].T if n in SENT_TRANSPOSED else wv[n]).astype(BF16) for n in SHARDED}
    small = {n: wv[n] for n in SMALL}
    loss_part, grad_x, recv, dsmall = local_step(x[0], loss_target[0], shards, small)
    for n in GRAD_TRANSPOSED:
        recv[n] = recv[n].transpose(0, 2, 1)

    small_shapes = [wv[n].shape for n in SMALL]
    res = {}
    res["w_in"], (sgather,) = adamw(recv["w_in"], wv["w_in"], mv["w_in"], vv["w_in"], "adamw_w_in",
                                    carry=Gather([_pack([dsmall[n] for n in SMALL], F32)]))
    for n in SHARDED[1:]:
        res[n] = adamw(recv[n], wv[n], mv[n], vv[n], "adamw_" + n)
    sres = adamw(sgather, _pack([wv[n] for n in SMALL], F32), _pack([mv[n] for n in SMALL], F32),
                 _pack([vv[n] for n in SMALL], F32), "adamw_small")
    sun = [_unpack(t, small_shapes) for t in sres]
    for k, n in enumerate(SMALL):
        res[n] = tuple(sun[t][k] for t in range(4))

    loss = lax.psum(loss_part[0, 0], ("x", "y", "c"))
    outs = [loss, grad_x[None]]
    for t in range(4):
        outs += [res[n][t][None] for n in WEIGHTS]
    return tuple(outs)
```

```python
import math

import numpy as np
import jax
import jax.numpy as jnp
from jax import lax
from jax.experimental import pallas as pl
from jax.experimental.pallas import tpu as pltpu

F32 = jnp.float32
BF16 = jnp.bfloat16

D_MODEL = 2048
HEAD_DIM = 128
HEADS_PER_GROUP = 4
ATTN_GROUPS = ((128, 1), (512, 4), (2048, 16))
N_HEADS = HEADS_PER_GROUP * len(ATTN_GROUPS)
GROUP_W = HEADS_PER_GROUP * HEAD_DIM
HQ = N_HEADS * HEAD_DIM
SSM_W = 1024
SSM_GROUP = 16
SSM_GROUPS = 64
SSM_STATE = 64
STATE_W = SSM_GROUPS * SSM_STATE
D_FF = 5632
EPS = 1e-6
N_DEV = 8
SEGS = 8
BD = 8

ADAM_LR, ADAM_B1, ADAM_B2, ADAM_EPS, ADAM_WD, ADAM_STEP = 0.001, 0.9, 0.999, 1e-08, 0.01, 10

VMEM_LIMIT = 56 * 1024 * 1024
HBM_SPEC = pl.BlockSpec(memory_space=pltpu.HBM)
MESH_ID = pl.DeviceIdType.MESH
NEG = -1e30


def _pcall(body, **kw):
    return pl.pallas_call(body, **kw)


def _cparams(sem=None):
    if sem is None:
        return pltpu.CompilerParams(vmem_limit_bytes=VMEM_LIMIT)
    return pltpu.CompilerParams(vmem_limit_bytes=VMEM_LIMIT, dimension_semantics=sem)


def _my_coords():
    return lax.axis_index("x"), lax.axis_index("y"), lax.axis_index("c")


class Gather:
    def __init__(self, xs, pass_early=False):
        self.arrays = list(xs)
        self.out_shapes = [jax.ShapeDtypeStruct((N_DEV,) + x.shape, x.dtype) for x in xs]
        self.pass_early = pass_early

    def _ctx(self, out_refs, send_sems, recv_sems):
        mx, my, mc = _my_coords()
        me, sibling = (mx, my, mc), (mx, my, 1 - mc)
        chips = [(1 - mx, my), (mx, 1 - my), (1 - mx, 1 - my)]

        def slot(a, px, py, pc):
            return out_refs[a].at[4 * px + 2 * py + pc]

        def copy(a, k, block, to, src=None):
            return pltpu.make_async_remote_copy(
                src_ref=slot(a, *block) if src is None else src, dst_ref=slot(a, *block),
                send_sem=send_sems.at[7 * a + k], recv_sem=recv_sems.at[7 * a + k], device_id=to, device_id_type=MESH_ID)

        return me, sibling, chips, mc, slot, copy

    def _first(self, a, x_refs, ctx):
        me, sibling, chips, mc, slot, copy = ctx
        return [copy(a, 0, me, sibling, src=x_refs[a])] + [copy(a, 1 + j, me, (*chip, mc), src=x_refs[a]) for j, chip in enumerate(chips)]

    def start(self, x_refs, out_refs, send_sems, recv_sems, local_sems):
        ctx = self._ctx(out_refs, send_sems, recv_sems)
        me, slot = ctx[0], ctx[4]
        for a in range(len(self.arrays)):
            pltpu.make_async_copy(x_refs[a], slot(a, *me), local_sems.at[a]).start()
            for cp in self._first(a, x_refs, ctx):
                cp.start()

    def _passed(self, ctx):
        me, sibling, chips, mc, slot, copy = ctx
        return [copy(a, 4 + j, (*chip, mc), sibling) for a in range(len(self.arrays)) for j, chip in enumerate(chips)]

    def middle(self, x_refs, out_refs, send_sems, recv_sems, local_sems):
        ctx = self._ctx(out_refs, send_sems, recv_sems)
        me, sibling, chips, mc, slot, copy = ctx
        for a in range(len(self.arrays)):
            for j, chip in enumerate(chips):
                copy(a, 1 + j, (*chip, mc), me).wait_recv()
                copy(a, 4 + j, (*chip, mc), sibling).start()

    def finish(self, x_refs, out_refs, send_sems, recv_sems, local_sems, passed_on=False):
        if not passed_on:
            self.middle(x_refs, out_refs, send_sems, recv_sems, local_sems)
        ctx = self._ctx(out_refs, send_sems, recv_sems)
        me, sibling, chips, mc, slot, copy = ctx
        na = len(self.arrays)
        passed = self._passed(ctx)
        for a in range(na):
            copy(a, 0, sibling, me).wait_recv()
            for j, chip in enumerate(chips):
                copy(a, 4 + j, (*chip, 1 - mc), me).wait_recv()
        for a in range(na):
            for cp in self._first(a, x_refs, ctx):
                cp.wait_send()
        for cp in passed:
            cp.wait_send()
        for a in range(na):
            pltpu.make_async_copy(x_refs[a], slot(a, *me), local_sems.at[a]).wait()


class AllToAll:
    def __init__(self, ps):
        self.arrays = list(ps)
        self.out_shapes = [jax.ShapeDtypeStruct(p.shape, p.dtype) for p in ps]

    def _copies(self, p_refs, out_refs, send_sems, recv_sems, local_sems):
        mx, my, mc = _my_coords()
        me = 4 * mx + 2 * my + mc
        local, remote = [], []
        for a in range(len(self.arrays)):
            local.append(pltpu.make_async_copy(p_refs[a].at[me], out_refs[a].at[me], local_sems.at[a]))
            for k in range(1, N_DEV):
                px, py, pc = mx ^ ((k >> 2) & 1), my ^ ((k >> 1) & 1), mc ^ (k & 1)
                remote.append(pltpu.make_async_remote_copy(
                    src_ref=p_refs[a].at[4 * px + 2 * py + pc], dst_ref=out_refs[a].at[me],
                    send_sem=send_sems.at[7 * a + k - 1], recv_sem=recv_sems.at[7 * a + k - 1],
                    device_id=(px, py, pc), device_id_type=MESH_ID))
        return local, remote

    def start(self, *refs):
        local, remote = self._copies(*refs)
        for cp in local + remote:
            cp.start()

    def finish(self, *refs):
        local, remote = self._copies(*refs)
        for cp in remote:
            cp.wait_recv()
        for cp in remote:
            cp.wait_send()
        for cp in local:
            cp.wait()


class RowsToOwners:
    def __init__(self, p, r0, n, into=None):
        self.arrays = [p] if into is None else [p, into]
        self.out_shapes = [jax.ShapeDtypeStruct(p.shape, p.dtype)]
        self.aliases = {} if into is None else {1: 0}
        self.rows = (r0, n)

    def _copies(self, p_refs, out_refs, send_sems, recv_sems, local_sems):
        mx, my, mc = _my_coords()
        me = 4 * mx + 2 * my + mc
        rows = pl.ds(*self.rows)
        local = [pltpu.make_async_copy(p_refs[0].at[me, rows], out_refs[0].at[me, rows], local_sems.at[0])]
        remote = []
        for k in range(1, N_DEV):
            px, py, pc = mx ^ ((k >> 2) & 1), my ^ ((k >> 1) & 1), mc ^ (k & 1)
            remote.append(pltpu.make_async_remote_copy(
                src_ref=p_refs[0].at[4 * px + 2 * py + pc, rows], dst_ref=out_refs[0].at[me, rows],
                send_sem=send_sems.at[k - 1], recv_sem=recv_sems.at[k - 1], device_id=(px, py, pc), device_id_type=MESH_ID))
        return local, remote

    start = AllToAll.start
    finish = AllToAll.finish


def _run(body, args, carry=None, **kw):
    if carry is None:
        return _pcall(body, **kw)(*args)
    grid = kw["grid"]
    single = not isinstance(kw["out_shape"], (list, tuple))
    in_specs = list(kw["in_specs"])
    out_specs = [kw["out_specs"]] if single else list(kw["out_specs"])
    out_shape = [kw["out_shape"]] if single else list(kw["out_shape"])
    scratch = list(kw.get("scratch_shapes", []))
    na, nin, nout, nscr = len(carry.arrays), len(in_specs), len(out_specs), len(scratch)
    nco = len(carry.out_shapes)
    aliases = {nin + i: nout + o for i, o in getattr(carry, "aliases", {}).items()}
    steps = int(np.prod(grid))
    mid_step = (steps * 7) // 10 if getattr(carry, "pass_early", False) and steps >= 4 else None

    def carried(*refs):
        ins, cin = refs[:nin], refs[nin:nin + na]
        outs, cout = refs[nin + na:nin + na + nout], refs[nin + na + nout:nin + na + nout + nco]
        scr = refs[nin + na + nout + nco:nin + na + nout + nco + nscr]
        sems = refs[nin + na + nout + nco + nscr:]
        step = pl.program_id(0)
        for i in range(1, len(grid)):
            step = step * grid[i] + pl.program_id(i)

        @pl.when(step == 0)
        def _():
            carry.start(cin, cout, *sems)

        if mid_step is not None:
            @pl.when(step == mid_step)
            def _():
                carry.middle(cin, cout, *sems)

        body(*ins, *outs, *scr)

        @pl.when(step == steps - 1)
        def _():
            if mid_step is not None:
                carry.finish(cin, cout, *sems, passed_on=True)
            else:
                carry.finish(cin, cout, *sems)

    res = _pcall(
        carried, name=kw["name"], grid=grid, in_specs=in_specs + [HBM_SPEC] * na, out_specs=out_specs + [HBM_SPEC] * nco,
        out_shape=out_shape + carry.out_shapes, input_output_aliases=aliases,
        scratch_shapes=scratch + [pltpu.SemaphoreType.DMA((7 * na,)), pltpu.SemaphoreType.DMA((7 * na,)), pltpu.SemaphoreType.DMA((na,))],
        compiler_params=_cparams(("arbitrary",) * len(grid)),
    )(*args, *carry.arrays)
    main = res[:nout]
    return (main[0] if single else main), list(res[nout:])


_DN = {"nn": (((1,), (0,)), ((), ())), "nt": (((1,), (1,)), ((), ())), "tn": (((0,), (0,)), ((), ()))}


LANE = 128
MM_TM, MM_TN, MM_TK = 1024, 1536, 2048


def _tile(n, cap):
    for t in range(min(cap, n) // LANE * LANE, 0, -LANE):
        if n % t == 0:
            return t
    raise ValueError(n)


DW_TM, DW_TN, DW_TK = 512, 512, 8192
EPILOGUE_SPLIT = 2


def mm(pairs, mode, out_dtype, name, tm=None, tn=None, tk=None, carry=None, epilogue=None, extras=(), b_window=None):
    a0, b0 = pairs[0]
    if mode == "nn":
        (m, k), n = a0.shape, b0.shape[1]
    elif mode == "nt":
        (m, k), n = a0.shape, b0.shape[0]
    else:
        (k, m), n = a0.shape, b0.shape[1]
    if b_window is not None:
        assert mode in ("nn", "nt") and len(pairs) == 1
        if mode == "nt":
            n = b_window[0]
        else:
            assert k == b_window[0]
    caps = (DW_TM, DW_TN, DW_TK) if mode == "tn" else (MM_TM, MM_TN, MM_TK)
    tm, tn, tk = _tile(m, tm or caps[0]), _tile(n, tn or caps[1]), _tile(k, tk or caps[2])
    nk = k // tk
    npairs = len(pairs)
    nex = len(extras)
    fused = epilogue is not None
    assert not fused or nk == 1
    out_dtypes = list(out_dtype) if fused else [out_dtype]

    def body(*refs):
        if fused:
            half = tn // EPILOGUE_SPLIT
            for c in range(EPILOGUE_SPLIT):
                cols = slice(c * half, (c + 1) * half)
                prods = []
                for p in range(npairs):
                    a = refs[2 * p][...].astype(BF16)
                    b = (refs[2 * p + 1][cols, :] if mode == "nt" else refs[2 * p + 1][:, cols]).astype(BF16)
                    prods.append(lax.dot_general(a, b, _DN[mode], preferred_element_type=F32))
                ex = [refs[2 * npairs + e][:, cols].astype(F32) for e in range(nex)]
                for o_ref, val in zip(refs[2 * npairs + nex:], epilogue(prods, ex)):
                    o_ref[:, cols] = val.astype(o_ref.dtype)
            return
        prods = []
        for p in range(npairs):
            a = refs[2 * p][...].astype(BF16) if (p == 0 or pairs[p][0] is not pairs[p - 1][0]) else a
            b = refs[2 * p + 1][...].astype(BF16)
            prods.append(lax.dot_general(a, b, _DN[mode], preferred_element_type=F32))
        o_ref = refs[2 * npairs]
        tot = prods[0]
        for d in prods[1:]:
            tot = tot + d
        if nk == 1:
            o_ref[...] = tot.astype(o_ref.dtype)
            return
        acc = refs[2 * npairs + 1]
        kk = pl.program_id(2)

        @pl.when(kk == 0)
        def _():
            acc[...] = tot

        @pl.when(kk > 0)
        def _():
            acc[...] += tot

        @pl.when(kk == nk - 1)
        def _():
            o_ref[...] = acc[...].astype(o_ref.dtype)

    rows_of = b_window[1] if b_window is not None else (lambda t: t)
    if mode == "nn":
        sp = [pl.BlockSpec((tm, tk), lambda i, j, kk: (i, kk)), pl.BlockSpec((tk, tn), lambda i, j, kk: (rows_of(kk), j))]
    elif mode == "nt":
        sp = [pl.BlockSpec((tm, tk), lambda i, j, kk: (i, kk)), pl.BlockSpec((tn, tk), lambda i, j, kk: (rows_of(j), kk))]
    else:
        sp = [pl.BlockSpec((tk, tm), lambda i, j, kk: (kk, i)), pl.BlockSpec((tk, tn), lambda i, j, kk: (kk, j))]
    o_spec = pl.BlockSpec((tm, tn), lambda i, j, kk: (i, j))
    out_shapes = [jax.ShapeDtypeStruct((m, n), dt) for dt in out_dtypes]
    return _run(
        body, [t for pr in pairs for t in pr] + list(extras), carry=carry, name=name, grid=(m // tm, n // tn, nk),
        in_specs=sp * npairs + [o_spec] * nex,
        out_specs=[o_spec] * len(out_shapes) if fused else o_spec,
        out_shape=out_shapes if fused else out_shapes[0],
        scratch_shapes=[pltpu.VMEM((tm, tn), F32)] if nk > 1 else [],
        compiler_params=_cparams(("parallel", "parallel", "arbitrary")),
    )


def rowwise(name, fn, row_ins, const_ins, row_outs, acc_outs=(), ts=None, carry=None):
    s = row_ins[0].shape[0]
    row_outs = [ro if len(ro) == 3 else (*ro, 1) for ro in row_outs]
    if ts is None:
        per_row = sum(a.shape[-1] * a.dtype.itemsize for a in row_ins) + sum(w * jnp.dtype(dt).itemsize for w, dt, _ in row_outs)
        ts = 512
        while ts > 8 and 2 * ts * per_row > 20 * 1024 * 1024:
            ts //= 2
    ts = min(ts, s)
    assert s % ts == 0
    nr, nc, no, na = len(row_ins), len(const_ins), len(row_outs), len(acc_outs)

    def body(*refs):
        rows = [r[...].reshape(ts, r.shape[-1]).astype(F32) for r in refs[:nr]]
        consts = [r[...] for r in refs[nr:nr + nc]]
        outs, accs = fn(rows, consts)
        for r, v in zip(refs[nr + nc:nr + nc + no], outs):
            r[...] = v.astype(r.dtype).reshape(r.shape)
        if na:
            first = pl.program_id(0) == 0
            for r, v in zip(refs[nr + nc + no:], accs):
                @pl.when(first)
                def _(r=r, v=v):
                    r[...] = v

                @pl.when(jnp.logical_not(first))
                def _(r=r, v=v):
                    r[...] += v

    def tile_spec(w, d):
        if d == 1:
            return pl.BlockSpec((ts, w), lambda i: (i, 0))
        return pl.BlockSpec((d, ts // d, w), lambda i: (0, i, 0))

    in_specs = [tile_spec(a.shape[-1], a.shape[0] if a.ndim == 3 else 1) for a in row_ins]
    in_specs += [pl.BlockSpec(c.shape, lambda i, nd=c.ndim: (0,) * nd) for c in const_ins]
    out_specs = [tile_spec(w, d) for w, _, d in row_outs]
    out_specs += [pl.BlockSpec(shp, lambda i, nd=len(shp): (0,) * nd) for shp in acc_outs]
    out_shape = [jax.ShapeDtypeStruct((s, w) if d == 1 else (d, s // d, w), dt) for w, dt, d in row_outs]
    out_shape += [jax.ShapeDtypeStruct(shp, F32) for shp in acc_outs]
    return _run(
        body, [*row_ins, *const_ins], carry=carry, name=name, grid=(s // ts,), in_specs=in_specs, out_specs=out_specs,
        out_shape=out_shape, compiler_params=_cparams(("arbitrary",)),
    )


PERM_TS = 256


def _perm_matrix(ts, d, inverse):
    i = lax.broadcasted_iota(jnp.int32, (ts, ts), 0)
    k = lax.broadcasted_iota(jnp.int32, (ts, ts), 1)
    per = ts // d
    src = (i % d) * per + i // d if inverse else (i % per) * d + i // per
    return jnp.where(k == src, 1.0, 0.0).astype(BF16)


def _permute(p, x):
    if x.dtype == BF16:
        return jnp.dot(p, x, preferred_element_type=F32)
    hi = x.astype(BF16)
    rest = x - hi.astype(F32)
    mid = rest.astype(BF16)
    lo = (rest - mid.astype(F32)).astype(BF16)
    out = jnp.dot(p, hi, preferred_element_type=F32) + jnp.dot(p, mid, preferred_element_type=F32)
    return out + jnp.dot(p, lo, preferred_element_type=F32)


def _rms(x, gain):
    r = lax.rsqrt(jnp.mean(x * x, axis=-1, keepdims=True) + EPS)
    n = x * r
    return n * gain, n, r


def _rms_bwd(dy, n, r, gain):
    dn = dy * gain
    dx = r * (dn - n * jnp.mean(dn * n, axis=-1, keepdims=True))
    return dx, jnp.sum(dy * n, axis=0, keepdims=True)


def _sigmoid(x):
    return 1.0 / (1.0 + jnp.exp(-x))


_GELU_K = math.sqrt(2.0 / math.pi)


def _gelu(x):
    t = jnp.tanh(_GELU_K * (x + 0.044715 * x * x * x))
    return 0.5 * x * (1.0 + t), t


def _gelu_grad(x, t):
    return 0.5 * (1.0 + t) + 0.5 * x * (1.0 - t * t) * _GELU_K * (1.0 + 3.0 * 0.044715 * x * x)


def _head_sum(x):
    parts = []
    for h in range(HEADS_PER_GROUP):
        sl = x[:, h * HEAD_DIM:(h + 1) * HEAD_DIM]
        parts.append(jnp.broadcast_to(jnp.sum(sl, axis=-1, keepdims=True), sl.shape))
    return jnp.concatenate(parts, axis=-1)


def _mix_weights(l0, l1, l2):
    mx = jnp.maximum(jnp.maximum(l0, l1), l2)
    e0, e1, e2 = jnp.exp(l0 - mx), jnp.exp(l1 - mx), jnp.exp(l2 - mx)
    inv = 1.0 / (e0 + e1 + e2)
    return e0 * inv, e1 * inv, e2 * inv


BLK = 128


def _slopes(g):
    return [2.0 ** (-8.0 * (g * HEADS_PER_GROUP + h + 1) / N_HEADS) for h in range(HEADS_PER_GROUP)]


def _attn_masks(dil):
    qi = lax.broadcasted_iota(jnp.int32, (BLK, BLK), 0)
    ki = lax.broadcasted_iota(jnp.int32, (BLK, BLK), 1)
    dist_c = qi - ki
    dist_p = BLK + qi - ki
    return dist_c >= 0, dist_p <= BLK, (dist_c * dil).astype(F32), (dist_p * dil).astype(F32)


def _window_mask(has_prev, dil):
    qi = lax.broadcasted_iota(jnp.int32, (BLK, 2 * BLK), 0)
    ki = lax.broadcasted_iota(jnp.int32, (BLK, 2 * BLK), 1)
    dist = BLK + qi - ki
    ok = jnp.logical_and(jnp.logical_and(dist >= 0, dist <= BLK), jnp.logical_or(ki >= BLK, has_prev))
    return ok, (dist * dil).astype(F32)


def attn_fwd(qkv, g, name):
    dil, length, _ = qkv.shape
    scale = HEAD_DIM ** -0.5
    slopes = _slopes(g)

    def body(q_ref, kc_ref, vc_ref, kp_ref, vp_ref, o_ref, l_ref):
        ok, dist = _window_mask(pl.program_id(1) > 0, dil)
        for h in range(HEADS_PER_GROUP):
            sl = slice(h * HEAD_DIM, (h + 1) * HEAD_DIM)
            k2 = jnp.concatenate([kp_ref[:, sl], kc_ref[:, sl]], axis=0)
            v2 = jnp.concatenate([vp_ref[:, sl], vc_ref[:, sl]], axis=0)
            s = lax.dot_general(q_ref[:, sl], k2, _DN["nt"], preferred_element_type=F32) * scale - slopes[h] * dist
            s = jnp.where(ok, s, NEG)
            mx = jnp.max(s, axis=-1, keepdims=True)
            p = jnp.exp(s - mx)
            den = jnp.sum(p, axis=-1, keepdims=True)
            o_ref[:, sl] = (jnp.dot(p.astype(BF16), v2, preferred_element_type=F32) / den).astype(BF16)
            l_ref[:, sl] = jnp.broadcast_to(mx + jnp.log(den), (BLK, HEAD_DIM))

    def spec(col, prev):
        if prev:
            return pl.BlockSpec((None, BLK, GROUP_W), lambda r, n: (r, jnp.maximum(n - 1, 0), col))
        return pl.BlockSpec((None, BLK, GROUP_W), lambda r, n: (r, n, col))

    out_spec = pl.BlockSpec((None, BLK, GROUP_W), lambda r, n: (r, n, 0))
    return _pcall(
        body, name=name, grid=(dil, length // BLK),
        in_specs=[spec(0, False), spec(1, False), spec(2, False), spec(1, True), spec(2, True)],
        out_specs=[out_spec, out_spec],
        out_shape=[jax.ShapeDtypeStruct((dil, length, GROUP_W), BF16), jax.ShapeDtypeStruct((dil, length, GROUP_W), F32)],
        compiler_params=_cparams(("parallel", "parallel")),
    )(qkv, qkv, qkv, qkv, qkv)


def attn_bwd(qkv, dout, lse, dd, g, name, carry=None):
    dil, length, _ = qkv.shape
    nblk = length // BLK
    scale = HEAD_DIM ** -0.5
    slopes = _slopes(g)

    def body(q_ref, kc_ref, vc_ref, kp_ref, vp_ref, qn_ref, do_ref, don_ref, l_ref, ln_ref, d_ref, dn_ref, o_ref):
        n = pl.program_id(1)
        ok2, dist2 = _window_mask(n > 0, dil)
        _, ok_p, _, dp = _attn_masks(dil)
        ok_next = jnp.logical_and(ok_p, n < nblk - 1)
        for h in range(HEADS_PER_GROUP):
            sl = slice(h * HEAD_DIM, (h + 1) * HEAD_DIM)
            q, kc, vc, qn = q_ref[:, sl], kc_ref[:, sl], vc_ref[:, sl], qn_ref[:, sl]
            k2 = jnp.concatenate([kp_ref[:, sl], kc], axis=0)
            v2 = jnp.concatenate([vp_ref[:, sl], vc], axis=0)
            do, don = do_ref[:, sl], don_ref[:, sl]
            lse_q, lse_n, dd_q, dd_n = l_ref[:, sl], ln_ref[:, sl], d_ref[:, sl], dn_ref[:, sl]

            def probs(qq, kk, dist, ok, lse_t):
                s = lax.dot_general(qq, kk, _DN["nt"], preferred_element_type=F32) * scale - slopes[h] * dist
                return jnp.where(ok, jnp.exp(jnp.where(ok, s, NEG) - lse_t), 0.0)

            p2 = probs(q, k2, dist2, ok2, jnp.concatenate([lse_q, lse_q], axis=1))
            p_x = probs(qn, kc, dp, ok_next, lse_n)
            ds2 = p2 * (lax.dot_general(do, v2, _DN["nt"], preferred_element_type=F32) - jnp.concatenate([dd_q, dd_q], axis=1))
            ds_x = p_x * (lax.dot_general(don, vc, _DN["nt"], preferred_element_type=F32) - dd_n)
            dq = jnp.dot(ds2.astype(BF16), k2, preferred_element_type=F32)
            ds_k = jnp.concatenate([ds2[:, BLK:], ds_x], axis=0).astype(BF16)
            p_k = jnp.concatenate([p2[:, BLK:], p_x], axis=0).astype(BF16)
            dk = lax.dot_general(ds_k, jnp.concatenate([q, qn], axis=0), _DN["tn"], preferred_element_type=F32)
            dv = lax.dot_general(p_k, jnp.concatenate([do, don], axis=0), _DN["tn"], preferred_element_type=F32)
            o_ref[:, h * HEAD_DIM:(h + 1) * HEAD_DIM] = (dq * scale).astype(BF16)
            o_ref[:, GROUP_W + h * HEAD_DIM:GROUP_W + (h + 1) * HEAD_DIM] = (dk * scale).astype(BF16)
            o_ref[:, 2 * GROUP_W + h * HEAD_DIM:2 * GROUP_W + (h + 1) * HEAD_DIM] = dv.astype(BF16)

    def spec(col, which):
        if which == "prev":
            return pl.BlockSpec((None, BLK, GROUP_W), lambda r, n: (r, jnp.maximum(n - 1, 0), col))
        if which == "next":
            return pl.BlockSpec((None, BLK, GROUP_W), lambda r, n: (r, jnp.minimum(n + 1, nblk - 1), col))
        return pl.BlockSpec((None, BLK, GROUP_W), lambda r, n: (r, n, col))

    return _run(
        body, [qkv, qkv, qkv, qkv, qkv, qkv, dout, dout, lse, lse, dd, dd], carry=carry, name=name, grid=(dil, nblk),
        in_specs=[spec(0, "cur"), spec(1, "cur"), spec(2, "cur"), spec(1, "prev"), spec(2, "prev"), spec(0, "next"),
                  spec(0, "cur"), spec(0, "next"), spec(0, "cur"), spec(0, "next"), spec(0, "cur"), spec(0, "next")],
        out_specs=pl.BlockSpec((None, BLK, 3 * GROUP_W), lambda r, n: (r, n, 0)),
        out_shape=jax.ShapeDtypeStruct((dil, length, 3 * GROUP_W), BF16),
        compiler_params=_cparams(("parallel", "parallel")),
    )


def _ssm_prep_values(are, aim, logdt):
    dt = jnp.exp(logdt)
    mag = jnp.exp(are * dt)
    lb_re, lb_im = mag * jnp.cos(aim * dt), mag * jnp.sin(aim * dt)
    inv = 1.0 / (are * are + aim * aim)
    n_re, n_im = lb_re - 1.0, lb_im
    f_re = (n_re * are + n_im * aim) * inv
    f_im = (n_im * are - n_re * aim) * inv
    return dt, lb_re, lb_im, f_re, f_im, inv


PREP_G = 8


def _group_specs(are, logdt, bre):
    def spec(a):
        return pl.BlockSpec((PREP_G,) + a.shape[1:], lambda i: (i, 0, 0))
    return spec(are), spec(logdt), spec(bre)


def ssm_prep(are, aim, logdt, bre, bim):
    def body(are_r, aim_r, ldt_r, bre_r, bim_r, lre_o, lim_o, bbre_o, bbim_o):
        _, lb_re, lb_im, f_re, f_im, _ = _ssm_prep_values(are_r[...], aim_r[...], ldt_r[...])
        lre_o[...] = lb_re
        lim_o[...] = lb_im
        bbre_o[...] = f_re * bre_r[...] - f_im * bim_r[...]
        bbim_o[...] = f_re * bim_r[...] + f_im * bre_r[...]

    sh1 = jax.ShapeDtypeStruct(are.shape, F32)
    shb = jax.ShapeDtypeStruct(bre.shape, F32)
    s1, sd, sb = _group_specs(are, logdt, bre)
    return _pcall(body, name="ssm_prep", grid=(SSM_GROUPS // PREP_G,), in_specs=[s1, s1, sd, sb, sb], out_specs=[s1, s1, sb, sb],
                  out_shape=[sh1, sh1, shb, shb], compiler_params=_cparams(("parallel",)))(are, aim, logdt, bre, bim)


def ssm_prep_bwd(are, aim, logdt, bre, bim, dbbre, dbbim, dlre, dlim):
    def body(are_r, aim_r, ldt_r, bre_r, bim_r, dbbre_r, dbbim_r, dlre_r, dlim_r, dare_o, daim_o, dldt_o, dbre_o, dbim_o):
        are_v, aim_v = are_r[...], aim_r[...]
        dt, lb_re, lb_im, f_re, f_im, inv = _ssm_prep_values(are_v, aim_v, ldt_r[...])
        b_re, b_im, g_re, g_im = bre_r[...], bim_r[...], dbbre_r[...], dbbim_r[...]
        dbre_o[...] = f_re * g_re + f_im * g_im
        dbim_o[...] = f_re * g_im - f_im * g_re
        df_re = jnp.sum(b_re * g_re + b_im * g_im, axis=-1, keepdims=True)
        df_im = jnp.sum(b_re * g_im - b_im * g_re, axis=-1, keepdims=True)
        il_re, il_im = are_v * inv, -aim_v * inv
        cl_re = dlre_r[...] + il_re * df_re + il_im * df_im
        cl_im = dlim_r[...] + il_re * df_im - il_im * df_re
        q_re = -(f_re * il_re - f_im * il_im)
        q_im = -(f_re * il_im + f_im * il_re)
        ca_re = q_re * df_re + q_im * df_im
        ca_im = q_re * df_im - q_im * df_re
        cz_re = lb_re * cl_re + lb_im * cl_im
        cz_im = lb_re * cl_im - lb_im * cl_re
        dare_o[...] = ca_re + dt * cz_re
        daim_o[...] = ca_im + dt * cz_im
        dldt_o[...] = dt * jnp.sum(are_v * cz_re + aim_v * cz_im, axis=1, keepdims=True)

    sh1 = jax.ShapeDtypeStruct(are.shape, F32)
    shb = jax.ShapeDtypeStruct(bre.shape, F32)
    s1, sd, sb = _group_specs(are, logdt, bre)
    return _pcall(
        body, name="ssm_prep_bwd", grid=(SSM_GROUPS // PREP_G,), in_specs=[s1, s1, sd, sb, sb, sb, sb, s1, s1],
        out_specs=[s1, s1, sd, sb, sb], out_shape=[sh1, sh1, jax.ShapeDtypeStruct(logdt.shape, F32), shb, shb],
        compiler_params=_cparams(("parallel",)),
    )(are, aim, logdt, bre, bim, dbbre, dbbim, dlre, dlim)


SCAN_WC = 512


def _chain_segments(a_re, a_im, e_re, e_im, nsq, reverse):
    p_re, p_im = a_re, a_im
    for _ in range(nsq):
        p_re, p_im = p_re * p_re - p_im * p_im, 2.0 * p_re * p_im
    row = lax.broadcasted_iota(jnp.int32, e_re.shape, 0)
    edge = (row == SEGS - 1) if reverse else (row == 0)
    shift = SEGS - 1 if reverse else 1
    c_re, c_im = jnp.zeros_like(e_re), jnp.zeros_like(e_im)
    for _ in range(SEGS - 1):
        n_re = p_re * c_re - p_im * c_im + e_re
        n_im = p_re * c_im + p_im * c_re + e_im
        c_re = jnp.where(edge, 0.0, pltpu.roll(n_re, shift, 0))
        c_im = jnp.where(edge, 0.0, pltpu.roll(n_im, shift, 0))
    return c_re, c_im


def _scan_dims(s):
    steps = s // SEGS
    assert steps & (steps - 1) == 0
    tt = min(128, steps)
    return steps, tt, steps // tt, tt * SEGS, int(math.log2(steps))


U_BLK = SSM_W // BD


def ssm_fwd(u_s, dvec, w_bre, w_bim, w_cre, w_cim_neg, lre, lim, name, carry=None):
    s = u_s.shape[0]
    steps, tt, nch, rows, nsq = _scan_dims(s)
    nb, ub_w, wc = w_bre.shape

    def body(u_r, d_r, bre_r, bim_r, cre_r, cim_r, lre_r, lim_r, yg_o, ys_o, hre_o, him_o, hin_re_o, hin_im_o,
             st_re, st_im, x_re, x_im, h_re, h_im):
        ps, ch = pl.program_id(1), pl.program_id(2)
        a_re = jnp.broadcast_to(lre_r[...], (SEGS, wc))
        a_im = jnp.broadcast_to(lim_r[...], (SEGS, wc))
        ub = u_r[...]
        ub16 = ub.astype(BF16)
        x_re[...] = jnp.dot(ub16, bre_r[...], preferred_element_type=F32)
        x_im[...] = jnp.dot(ub16, bim_r[...], preferred_element_type=F32)

        @pl.when(jnp.logical_and(ps == 0, ch == 0))
        def _():
            st_re[...] = jnp.zeros_like(st_re)
            st_im[...] = jnp.zeros_like(st_im)

        @pl.when(jnp.logical_and(ps == 1, ch == 0))
        def _():
            c_re, c_im = _chain_segments(a_re, a_im, st_re[...], st_im[...], nsq, False)
            st_re[...] = c_re
            st_im[...] = c_im
            hin_re_o[...] = c_re
            hin_im_o[...] = c_im

        def run(store):
            def step(t, hc):
                off = pl.multiple_of(t * SEGS, SEGS)
                n_re = a_re * hc[0] - a_im * hc[1] + x_re[pl.ds(off, SEGS), :]
                n_im = a_re * hc[1] + a_im * hc[0] + x_im[pl.ds(off, SEGS), :]
                if store:
                    h_re[pl.ds(off, SEGS), :] = n_re
                    h_im[pl.ds(off, SEGS), :] = n_im
                return n_re, n_im

            fin = lax.fori_loop(0, tt, step, (st_re[...], st_im[...]))
            st_re[...] = fin[0]
            st_im[...] = fin[1]

        @pl.when(ps == 0)
        def _():
            run(False)

        @pl.when(ps == 1)
        def _():
            run(True)
            hr16, hi16 = h_re[...].astype(BF16), h_im[...].astype(BF16)
            hre_o[...] = hr16
            him_o[...] = hi16
            y = jnp.dot(hr16, cre_r[...], preferred_element_type=F32) + jnp.dot(hi16, cim_r[...], preferred_element_type=F32)
            y = y + d_r[...] * ub
            ys_o[...] = y
            yg_o[...] = _gelu(y)[0].astype(BF16)

    def pass1(ps, c):
        return jnp.where(ps == 1, c, 0)

    u_spec = pl.BlockSpec((rows, ub_w), lambda j, ps, c: (c, j))
    d_spec = pl.BlockSpec((1, ub_w), lambda j, ps, c: (0, j))
    b_spec = pl.BlockSpec((None, ub_w, wc), lambda j, ps, c: (j, 0, 0))
    c_spec = pl.BlockSpec((None, wc, ub_w), lambda j, ps, c: (j, 0, 0))
    l_spec = pl.BlockSpec((1, wc), lambda j, ps, c: (0, j))
    y_spec = pl.BlockSpec((rows, ub_w), lambda j, ps, c: (pass1(ps, c), j))
    h_spec = pl.BlockSpec((rows, wc), lambda j, ps, c: (pass1(ps, c), j))
    e_spec = pl.BlockSpec((SEGS, wc), lambda j, ps, c: (0, j))
    return _run(
        body, [u_s, dvec, w_bre, w_bim, w_cre, w_cim_neg, lre, lim], carry=carry, name=name, grid=(nb, 2, nch),
        in_specs=[u_spec, d_spec, b_spec, b_spec, c_spec, c_spec, l_spec, l_spec],
        out_specs=[y_spec, y_spec, h_spec, h_spec, e_spec, e_spec],
        out_shape=[jax.ShapeDtypeStruct((s, SSM_W), BF16), jax.ShapeDtypeStruct((s, SSM_W), F32),
                   jax.ShapeDtypeStruct((s, STATE_W), BF16), jax.ShapeDtypeStruct((s, STATE_W), BF16),
                   jax.ShapeDtypeStruct((SEGS, STATE_W), F32), jax.ShapeDtypeStruct((SEGS, STATE_W), F32)],
        scratch_shapes=[pltpu.VMEM((SEGS, wc), F32)] * 2 + [pltpu.VMEM((rows, wc), F32)] * 4,
        compiler_params=_cparams(("parallel", "arbitrary", "arbitrary")),
    )


def ssm_bwd(dyg_s, ys, u_s, h_re, h_im, hin_re, hin_im, gin_re, gin_im, dvec, w_bre_t, w_bim_t, w_cre_t, w_cim_neg_t, lre, lim,
            name, carry=None):
    s = u_s.shape[0]
    steps, tt, nch, rows, nsq = _scan_dims(s)
    half = 2 * SEGS

    def body(dyg_r, ys_r, u_r, hre_r, him_r, pre_r, pim_r, cin_re_r, cin_im_r, gin_re_r, gin_im_r, d_r, bre_r, bim_r, cre_r,
             cim_r, lre_r, lim_r, du_o, dbre_o, dbim_o, dcre_o, dcim_o, dlre_o, dlim_o, dd_o,
             st_re, st_im, x_re, x_im, g_re, g_im, hf_re, hf_im):
        ch = pl.program_id(1)
        a_re = jnp.broadcast_to(lre_r[...], (SEGS, SCAN_WC))
        a_im = -jnp.broadcast_to(lim_r[...], (SEGS, SCAN_WC))
        ub, y = u_r[...], ys_r[...]
        dy = dyg_r[...] * _gelu_grad(y, _gelu(y)[1])
        dy16 = dy.astype(BF16)
        x_re[...] = jnp.dot(dy16, cre_r[...], preferred_element_type=F32)
        x_im[...] = jnp.dot(dy16, cim_r[...], preferred_element_type=F32)

        @pl.when(ch == 0)
        def _():
            st_re[...] = gin_re_r[...]
            st_im[...] = gin_im_r[...]
            dlre_o[...] = jnp.zeros_like(dlre_o)
            dlim_o[...] = jnp.zeros_like(dlim_o)

        hf_re[...] = hre_r[...].astype(F32)
        hf_im[...] = him_r[...].astype(F32)
        first_chunk = ch == nch - 1
        edge_re = jnp.where(first_chunk, cin_re_r[...], pre_r[...].astype(F32)[SEGS:, :])
        edge_im = jnp.where(first_chunk, cin_im_r[...], pim_r[...].astype(F32)[SEGS:, :])

        def step(i, hc):
            t = tt - 1 - i
            off = pl.multiple_of(t * SEGS, SEGS)
            n_re = a_re * hc[0] - a_im * hc[1] + x_re[pl.ds(off, SEGS), :]
            n_im = a_re * hc[1] + a_im * hc[0] + x_im[pl.ds(off, SEGS), :]
            g_re[pl.ds(off, SEGS), :] = n_re
            g_im[pl.ds(off, SEGS), :] = n_im
            offp = pl.multiple_of(jnp.maximum(t - 1, 0) * SEGS, SEGS)
            hp_re = jnp.where(t == 0, edge_re, hf_re[pl.ds(offp, SEGS), :])
            hp_im = jnp.where(t == 0, edge_im, hf_im[pl.ds(offp, SEGS), :])
            return n_re, n_im, hc[2] + hp_re * n_re + hp_im * n_im, hc[3] + hp_re * n_im - hp_im * n_re

        fin = lax.fori_loop(0, tt, step, (st_re[...], st_im[...], dlre_o[...], dlim_o[...]))
        st_re[...] = fin[0]
        st_im[...] = fin[1]
        dlre_o[...] = fin[2]
        dlim_o[...] = fin[3]

        gr16, gi16 = g_re[...].astype(BF16), g_im[...].astype(BF16)
        du = jnp.dot(gr16, bre_r[...], preferred_element_type=F32) + jnp.dot(gi16, bim_r[...], preferred_element_type=F32)
        du_o[...] = du + d_r[...] * dy
        ub16 = ub.astype(BF16)
        parts = [
            (dbre_o, lax.dot_general(ub16, gr16, _DN["tn"], preferred_element_type=F32)),
            (dbim_o, lax.dot_general(ub16, gi16, _DN["tn"], preferred_element_type=F32)),
            (dcre_o, lax.dot_general(hre_r[...], dy16, _DN["tn"], preferred_element_type=F32)),
            (dcim_o, lax.dot_general(him_r[...], dy16, _DN["tn"], preferred_element_type=F32)),
            (dd_o, jnp.sum(dy * ub, axis=0, keepdims=True)),
        ]
        for ref, val in parts:
            @pl.when(ch == 0)
            def _(ref=ref, val=val):
                ref[...] = val

            @pl.when(ch > 0)
            def _(ref=ref, val=val):
                ref[...] += val

    def chunk(c):
        return nch - 1 - c

    u_spec = pl.BlockSpec((rows, U_BLK), lambda j, c: (chunk(c), j))
    h_spec = pl.BlockSpec((rows, SCAN_WC), lambda j, c: (chunk(c), j))
    prev_spec = pl.BlockSpec((half, SCAN_WC), lambda j, c: (jnp.maximum(chunk(c) * (rows // half) - 1, 0), j))
    e_spec = pl.BlockSpec((SEGS, SCAN_WC), lambda j, c: (0, j))
    d_spec = pl.BlockSpec((1, U_BLK), lambda j, c: (0, j))
    bt_spec = pl.BlockSpec((None, SCAN_WC, U_BLK), lambda j, c: (j, 0, 0))
    ct_spec = pl.BlockSpec((None, U_BLK, SCAN_WC), lambda j, c: (j, 0, 0))
    l_spec = pl.BlockSpec((1, SCAN_WC), lambda j, c: (0, j))
    return _run(
        body, [dyg_s, ys, u_s, h_re, h_im, h_re, h_im, hin_re, hin_im, gin_re, gin_im, dvec, w_bre_t, w_bim_t, w_cre_t,
               w_cim_neg_t, lre, lim],
        carry=carry, name=name, grid=(BD, nch),
        in_specs=[u_spec, u_spec, u_spec, h_spec, h_spec, prev_spec, prev_spec, e_spec, e_spec, e_spec, e_spec, d_spec,
                  bt_spec, bt_spec, ct_spec, ct_spec, l_spec, l_spec],
        out_specs=[u_spec, ct_spec, ct_spec, bt_spec, bt_spec, e_spec, e_spec, d_spec],
        out_shape=[jax.ShapeDtypeStruct((s, SSM_W), F32)] + [jax.ShapeDtypeStruct((BD, U_BLK, SCAN_WC), F32)] * 2
        + [jax.ShapeDtypeStruct((BD, SCAN_WC, U_BLK), F32)] * 2 + [jax.ShapeDtypeStruct((SEGS, STATE_W), F32)] * 2
        + [jax.ShapeDtypeStruct((1, SSM_W), F32)],
        scratch_shapes=[pltpu.VMEM((SEGS, SCAN_WC), F32)] * 2 + [pltpu.VMEM((rows, SCAN_WC), F32)] * 6,
        compiler_params=_cparams(("parallel", "arbitrary")),
    )


def ssm_bwd_ends(dyg_s, ys, w_cre_t, w_cim_neg_t, lre, lim, name, carry=None):
    s = ys.shape[0]
    steps, tt, nch, rows, nsq = _scan_dims(s)
    nb, ub_w, wc = w_cre_t.shape

    def body(dyg_r, ys_r, cre_r, cim_r, lre_r, lim_r, gin_re_o, gin_im_o, st_re, st_im, x_re, x_im):
        ch = pl.program_id(1)
        a_re = jnp.broadcast_to(lre_r[...], (SEGS, wc))
        a_im = -jnp.broadcast_to(lim_r[...], (SEGS, wc))
        y = ys_r[...]
        dy16 = (dyg_r[...] * _gelu_grad(y, _gelu(y)[1])).astype(BF16)
        x_re[...] = jnp.dot(dy16, cre_r[...], preferred_element_type=F32)
        x_im[...] = jnp.dot(dy16, cim_r[...], preferred_element_type=F32)

        @pl.when(ch == 0)
        def _():
            st_re[...] = jnp.zeros_like(st_re)
            st_im[...] = jnp.zeros_like(st_im)

        def step(i, hc):
            off = pl.multiple_of((tt - 1 - i) * SEGS, SEGS)
            return (a_re * hc[0] - a_im * hc[1] + x_re[pl.ds(off, SEGS), :],
                    a_re * hc[1] + a_im * hc[0] + x_im[pl.ds(off, SEGS), :])

        fin = lax.fori_loop(0, tt, step, (st_re[...], st_im[...]))
        st_re[...] = fin[0]
        st_im[...] = fin[1]

        @pl.when(ch == nch - 1)
        def _():
            c_re, c_im = _chain_segments(a_re, a_im, fin[0], fin[1], nsq, True)
            gin_re_o[...] = c_re
            gin_im_o[...] = c_im

    y_spec = pl.BlockSpec((rows, ub_w), lambda j, c: (nch - 1 - c, j))
    ct_spec = pl.BlockSpec((None, ub_w, wc), lambda j, c: (j, 0, 0))
    l_spec = pl.BlockSpec((1, wc), lambda j, c: (0, j))
    e_spec = pl.BlockSpec((SEGS, wc), lambda j, c: (0, j))
    return _run(
        body, [dyg_s, ys, w_cre_t, w_cim_neg_t, lre, lim], carry=carry, name=name, grid=(nb, nch),
        in_specs=[y_spec, y_spec, ct_spec, ct_spec, l_spec, l_spec], out_specs=[e_spec, e_spec],
        out_shape=[jax.ShapeDtypeStruct((SEGS, STATE_W), F32)] * 2,
        scratch_shapes=[pltpu.VMEM((SEGS, wc), F32)] * 2 + [pltpu.VMEM((rows, wc), F32)] * 2,
        compiler_params=_cparams(("parallel", "arbitrary")),
    )


FWD_BD = 4


def _block_diag(m, nb=BD):
    g, r, c = m.shape
    m = m.reshape(nb, g // nb, r, c)
    eye = jnp.eye(g // nb, dtype=m.dtype)
    return jnp.einsum("jarc,ab->jarbc", m, eye).reshape(nb, (g // nb) * r, (g // nb) * c)


def _block_diag_extract(m, r, c):
    per = m.shape[1] // r
    m = m.reshape(BD, per, r, per, c)
    return jnp.einsum("jarac->jarc", m).reshape(BD * per, r, c)


def to_segments(a):
    s, w = a.shape
    return a.reshape(SEGS, s // SEGS, w).transpose(1, 0, 2).reshape(s, w)


def from_segments(a):
    s, w = a.shape
    return a.reshape(s // SEGS, SEGS, w).transpose(1, 0, 2).reshape(s, w)


W_IN_CHUNK_ROWS = (320, 320, 320, 240, 576, 272)
FFN_GATE_ROWS_FIRST = 480
TALL_TM = 2048
FFN_TN = 512


def local_step(x, target, shards, small):
    s = x.shape[0]
    g1, g2, g3, g4 = (small[k].reshape(1, D_MODEL) for k in ("norm_mix_pre", "norm_mix_post", "norm_ffn_pre", "norm_ffn_post"))
    dvec = small["ssm_d"].reshape(1, SSM_W)
    wts, recv = {}, {}

    def gathered(names, blocks):
        for n, b in zip(names, blocks):
            wts[n] = _full_from_gathered(b, n)

    def rms_in_fn(r, c):
        hh = _rms(r[0], c[0])[0].astype(BF16)
        return [hh, _permute(_perm_matrix(PERM_TS, 4, False), hh), _permute(_perm_matrix(PERM_TS, 16, False), hh)], []

    (h, h4, h16), got = rowwise("rms_in", rms_in_fn, [x], [g1], [(D_MODEL, BF16), (D_MODEL, BF16, 4), (D_MODEL, BF16, 16)],
                                ts=PERM_TS, carry=Gather([shards["w_in"]]))
    w_in_t = _full_from_gathered(got[0], "w_in")
    w_u_t, w_gates_t = w_in_t[3 * HQ:3 * HQ + SSM_W], w_in_t[3 * HQ + SSM_W:]

    def qkv_rows(g):
        return 3 * GROUP_W, lambda t: 3 * t + g

    hd = [h.reshape(1, s, D_MODEL), h4, h16]
    qkv = [None] * 3
    names = ("w_attn_up", "w_glu_v", "w_glu_g")
    qkv[0], got = mm([(hd[0].reshape(s, D_MODEL), w_in_t)], "nt", BF16, "mm_qkv0", tm=TALL_TM, tn=GROUP_W, b_window=qkv_rows(0),
                     carry=Gather([shards[n] for n in names]))
    gathered(names, got)
    qkv[1], got = mm([(hd[1].reshape(s, D_MODEL), w_in_t)], "nt", BF16, "mm_qkv1", tm=TALL_TM, tn=GROUP_W, b_window=qkv_rows(1),
                     carry=Gather([shards["w_out"]]))
    gathered(("w_out",), got)
    qkv[2] = mm([(hd[2].reshape(s, D_MODEL), w_in_t)], "nt", BF16, "mm_qkv2", tm=TALL_TM, tn=GROUP_W, b_window=qkv_rows(2))
    u = mm([(h, w_u_t)], "nt", F32, "mm_u")
    gates, got = mm([(h, w_gates_t)], "nt", BF16, "mm_gates", carry=Gather([shards["w_ffn_gate"]]))
    gathered(("w_ffn_gate",), got)

    outs, lses = [], []
    for g, (_, dil) in enumerate(ATTN_GROUPS):
        o, l = attn_fwd(qkv[g].reshape(dil, s // dil, 3 * GROUP_W), g, f"attn_fwd{g}")
        outs.append(o.reshape(s, GROUP_W) if dil == 1 else o)
        lses.append(l.reshape(s, GROUP_W) if dil == 1 else l)

    def natural(r):
        back4, back16 = _perm_matrix(PERM_TS, 4, True), _perm_matrix(PERM_TS, 16, True)
        return (r[0], _permute(back4, r[1].astype(BF16)), _permute(back16, r[2].astype(BF16)),
                r[3], _permute(back4, r[4]), _permute(back16, r[5]))

    def merge_fn(r, c):
        o0, o1, o2, l0, l1, l2 = natural(r)
        w0, w1, w2 = _mix_weights(l0, l1, l2)
        return [w0 * o0 + w1 * o1 + w2 * o2], []

    (attn,) = rowwise("attn_merge", merge_fn, outs + lses, [], [(GROUP_W, BF16)], ts=PERM_TS)
    attn_branch = mm([(attn, wts["w_attn_up"])], "nn", BF16, "mm_up", tm=TALL_TM)

    are3 = small["ssm_a_re"].reshape(SSM_GROUPS, SSM_STATE, 1)
    aim3 = small["ssm_a_im"].reshape(SSM_GROUPS, SSM_STATE, 1)
    ldt3 = small["ssm_log_dt"].reshape(SSM_GROUPS, 1, 1)
    bre3 = small["ssm_b_re"].reshape(SSM_GROUPS, SSM_STATE, SSM_GROUP)
    bim3 = small["ssm_b_im"].reshape(SSM_GROUPS, SSM_STATE, SSM_GROUP)
    cre3 = small["ssm_c_re"].reshape(SSM_GROUPS, SSM_GROUP, SSM_STATE)
    cim3 = small["ssm_c_im"].reshape(SSM_GROUPS, SSM_GROUP, SSM_STATE)
    lre3, lim3, bbre, bbim = ssm_prep(are3, aim3, ldt3, bre3, bim3)
    lre, lim = lre3.reshape(1, STATE_W), lim3.reshape(1, STATE_W)
    w_bre = _block_diag(bbre.transpose(0, 2, 1)).astype(BF16)
    w_bim = _block_diag(bbim.transpose(0, 2, 1)).astype(BF16)
    w_cre = _block_diag(cre3.transpose(0, 2, 1)).astype(BF16)
    w_cim = _block_diag(cim3.transpose(0, 2, 1)).astype(BF16)
    u_s = to_segments(u)
    fwd_w = [_block_diag(t.transpose(0, 2, 1), FWD_BD).astype(BF16) for t in (bbre, bbim, cre3, -cim3)]
    (yg_s, y_ssm, h_re, h_im, hin_re, hin_im), got = ssm_fwd(
        u_s, dvec, *fwd_w, lre, lim, "ssm_fwd", carry=Gather([shards["w_ffn_up"]], pass_early=True))
    gathered(("w_ffn_up",), got)
    yg = from_segments(yg_s)
    gv = mm([(yg, wts["w_glu_v"])], "nn", BF16, "mm_glu_v", tm=TALL_TM)
    gg = mm([(yg, wts["w_glu_g"])], "nn", BF16, "mm_glu_g", tm=TALL_TM)

    def gate_fn(r, c):
        gts, ab, gv_, gg_ = r
        sa, ss = _sigmoid(gts[:, :D_MODEL]), _sigmoid(gts[:, D_MODEL:])
        return [sa * ab + ss * (gv_ * _sigmoid(gg_))], []

    (merged,) = rowwise("gate_merge", gate_fn, [gates, attn_branch, gv, gg], [], [(D_MODEL, BF16)])
    o_mix = mm([(merged, wts["w_out"])], "nn", F32, "mm_out")

    def mid_fn(r, c):
        x1 = r[0] + _rms(r[1], c[0])[0]
        return [x1, _rms(x1, c[1])[0]], []

    x1, h2 = rowwise("rms_mid", mid_fn, [x, o_mix], [g2, g3], [(D_MODEL, F32), (D_MODEL, BF16)])
    (fa, fb, fin), got = mm([(h2, wts["w_ffn_gate"]), (h2, wts["w_ffn_up"])], "nt", [BF16, BF16, BF16], "mm_ffn_in", tn=FFN_TN,
                            epilogue=lambda p, e: [p[0], p[1], p[0] * _sigmoid(p[0]) * p[1]],
                            carry=Gather([shards["w_ffn_down"]], pass_early=True))
    gathered(("w_ffn_down",), got)
    f = mm([(fin, wts["w_ffn_down"])], "nn", F32, "mm_ffn_down", tn=512, tk=D_FF)

    def loss_fn(r, c):
        x1_, f_, tgt = r
        y, n, rr = _rms(f_, c[0])
        err = x1_ + y - tgt
        dout = err * (1.0 / D_MODEL)
        df, dg = _rms_bwd(dout, n, rr, c[0])
        lp = 0.5 * jnp.sum(jnp.sum(err * err, axis=-1, keepdims=True) * (1.0 / D_MODEL), axis=0, keepdims=True)
        return [df, dout], [dg, lp]

    df, dout, dg4, loss_part = rowwise("loss_bwd", loss_fn, [x1, f, target], [g4], [(D_MODEL, BF16), (D_MODEL, BF16)],
                                       acc_outs=[(1, D_MODEL), (1, 1)])
    def sent(names, blocks):
        for n, b in zip(names, blocks):
            recv[n] = b

    def to_owners(names, dws):
        return AllToAll([_split_for_devices(d, n) for n, d in zip(names, dws)])

    def swiglu_bwd(p, e):
        dfin_, (a, b) = p[0], e
        sg = _sigmoid(a)
        return [dfin_ * b * (sg * (1.0 + a * (1.0 - sg))), dfin_ * a * sg]

    da, db = mm([(df, wts["w_ffn_down"])], "nt", [BF16, BF16], "mm_d_fin", tn=FFN_TN, epilogue=swiglu_bwd, extras=[fa, fb])
    dw_ffn_down = mm([(fin, df)], "tn", BF16, "mm_dw_ffn_down")
    dh2, got = mm([(da, wts["w_ffn_gate"]), (db, wts["w_ffn_up"])], "nn", F32, "mm_d_h2", tm=512, tn=512, tk=D_FF,
                  carry=to_owners(["w_ffn_down"], [dw_ffn_down]))
    sent(["w_ffn_down"], got)
    dw_ffn_gate = mm([(da, h2)], "tn", BF16, "mm_dw_ffn_gate")
    gate_blocks = _split_for_devices(dw_ffn_gate, "w_ffn_gate")
    dw_ffn_up, (gate_landed,) = mm([(db, h2)], "tn", BF16, "mm_dw_ffn_up",
                                   carry=RowsToOwners(gate_blocks, 0, FFN_GATE_ROWS_FIRST))

    def mid_bwd(r, c):
        dh2_, dout_, x1_, o_ = r
        _, n3, r3 = _rms(x1_, c[1])
        dx1, dg3_ = _rms_bwd(dh2_, n3, r3, c[1])
        dx1 = dx1 + dout_
        _, n2, r2 = _rms(o_, c[0])
        do_, dg2_ = _rms_bwd(dx1, n2, r2, c[0])
        return [dx1, do_], [dg2_, dg3_]

    rest = gate_blocks.shape[1] - FFN_GATE_ROWS_FIRST
    (dx1, do_mix, dg2, dg3), (gate_landed,) = rowwise(
        "rms_mid_bwd", mid_bwd, [dh2, dout, x1, o_mix], [g2, g3], [(D_MODEL, F32), (D_MODEL, BF16)],
        acc_outs=[(1, D_MODEL), (1, D_MODEL)], carry=RowsToOwners(gate_blocks, FFN_GATE_ROWS_FIRST, rest, into=gate_landed))
    recv["w_ffn_gate"] = gate_landed
    dmerged = mm([(do_mix, wts["w_out"])], "nt", BF16, "mm_d_merged")
    dw_out = mm([(merged, do_mix)], "tn", BF16, "mm_dw_out")

    def gate_bwd(r, c):
        dm, gts, ab, gv_, gg_ = r
        sa, ss, sg = _sigmoid(gts[:, :D_MODEL]), _sigmoid(gts[:, D_MODEL:]), _sigmoid(gg_)
        branch = gv_ * sg
        dbranch = dm * ss
        dgates = jnp.concatenate([dm * ab * sa * (1.0 - sa), dm * branch * ss * (1.0 - ss)], axis=-1)
        return [dgates, dm * sa, dbranch * sg, dbranch * gv_ * sg * (1.0 - sg)], []

    dgates, dab, dgv, dgg = rowwise("gate_bwd", gate_bwd, [dmerged, gates, attn_branch, gv, gg], [],
                                    [(2 * D_MODEL, BF16), (D_MODEL, BF16), (D_MODEL, BF16), (D_MODEL, BF16)])
    dattn = mm([(dab, wts["w_attn_up"])], "nt", F32, "mm_d_attn")
    dw_up = mm([(attn, dab)], "tn", BF16, "mm_dw_up")
    dyg = mm([(dgv, wts["w_glu_v"]), (dgg, wts["w_glu_g"])], "nt", F32, "mm_d_yg")
    dw_glu_v = mm([(yg, dgv)], "tn", BF16, "mm_dw_glu_v")
    dw_glu_g = mm([(yg, dgg)], "tn", BF16, "mm_dw_glu_g")

    dyg_s = to_segments(dyg)
    (gin_re, gin_im), got = ssm_bwd_ends(dyg_s, y_ssm, fwd_w[2].transpose(0, 2, 1), fwd_w[3].transpose(0, 2, 1), lre, lim,
                                         "ssm_bwd_ends", carry=to_owners(["w_out"], [dw_out]))
    sent(["w_out"], got)
    (du_s, dbre_d, dbim_d, dcre_d, dcim_d, dl_re8, dl_im8, dd_ssm), got = ssm_bwd(
        dyg_s, y_ssm, u_s, h_re, h_im, hin_re, hin_im, gin_re, gin_im, dvec, w_bre.transpose(0, 2, 1), w_bim.transpose(0, 2, 1),
        w_cre.transpose(0, 2, 1), -w_cim.transpose(0, 2, 1), lre, lim, "ssm_bwd", carry=to_owners(["w_ffn_up"], [dw_ffn_up]))
    sent(["w_ffn_up"], got)
    dbb_re = _block_diag_extract(dbre_d, SSM_GROUP, SSM_STATE).transpose(0, 2, 1)
    dbb_im = _block_diag_extract(dbim_d, SSM_GROUP, SSM_STATE).transpose(0, 2, 1)
    dc_re = _block_diag_extract(dcre_d, SSM_STATE, SSM_GROUP).transpose(0, 2, 1)
    dc_im = -_block_diag_extract(dcim_d, SSM_STATE, SSM_GROUP).transpose(0, 2, 1)

    def fold8(r, c):
        return [], [jnp.sum(r[0], axis=0, keepdims=True), jnp.sum(r[1], axis=0, keepdims=True)]

    dl_re, dl_im = rowwise("ssm_dl_fold", fold8, [dl_re8, dl_im8], [], [], acc_outs=[(1, STATE_W), (1, STATE_W)], ts=SEGS)
    da_re, da_im, dldt, db_re, db_im = ssm_prep_bwd(
        are3, aim3, ldt3, bre3, bim3, dbb_re, dbb_im,
        dl_re.reshape(SSM_GROUPS, SSM_STATE, 1), dl_im.reshape(SSM_GROUPS, SSM_STATE, 1))
    du = from_segments(du_s)

    def merge_bwd(r, c):
        dat = r[0]
        o0, o1, o2, l0, l1, l2 = natural(r[1:])
        w0, w1, w2 = _mix_weights(l0, l1, l2)
        tot = _head_sum(dat * (w0 * o0 + w1 * o1 + w2 * o2))
        to4, to16 = _perm_matrix(PERM_TS, 4, False), _perm_matrix(PERM_TS, 16, False)
        return [w0 * dat, _permute(to4, (w1 * dat).astype(BF16)), _permute(to16, (w2 * dat).astype(BF16)),
                w0 * tot, _permute(to4, (w1 * tot).astype(BF16)), _permute(to16, (w2 * tot).astype(BF16))], []

    mb = rowwise("attn_merge_bwd", merge_bwd, [dattn] + outs + lses, [],
                 [(GROUP_W, BF16), (GROUP_W, BF16, 4), (GROUP_W, BF16, 16), (GROUP_W, BF16), (GROUP_W, BF16, 4), (GROUP_W, BF16, 16)],
                 ts=PERM_TS)
    dqs, dw_qkv = [], []
    names = ["w_glu_v", "w_glu_g", "w_attn_up"]
    for g, (_, dil) in enumerate(ATTN_GROUPS):
        dq = attn_bwd(qkv[g].reshape(dil, s // dil, 3 * GROUP_W), mb[g].reshape(dil, s // dil, GROUP_W),
                      lses[g].reshape(dil, s // dil, GROUP_W), mb[3 + g].reshape(dil, s // dil, GROUP_W),
                      g, f"attn_bwd{g}", carry=to_owners(names, [dw_glu_v, dw_glu_g, dw_up]) if g == 1 else None)
        if g == 1:
            dq, got = dq
            sent(names, got)
        dq = dq.reshape(s, 3 * GROUP_W)
        dqs.append(dq)
        dw_qkv.append(mm([(hd[g].reshape(s, D_MODEL), dq)], "tn", BF16, f"mm_dw_qkv{g}"))
    dw_u = mm([(h, du)], "tn", BF16, "mm_dw_u")
    dw_gates = mm([(h, dgates)], "tn", BF16, "mm_dw_gates")
    dw_in = jnp.concatenate(
        [dw_qkv[g][:, o * GROUP_W:(o + 1) * GROUP_W] for o in range(3) for g in range(3)] + [dw_u, dw_gates], axis=1)
    dw_in_blocks = _split_for_devices(dw_in, "w_in")
    starts = [sum(W_IN_CHUNK_ROWS[:i]) for i in range(len(W_IN_CHUNK_ROWS))]
    landed = None

    def chunk(i):
        return RowsToOwners(dw_in_blocks, starts[i], W_IN_CHUNK_ROWS[i], into=landed)

    dh_parts = []
    for g, (_, dil) in enumerate(ATTN_GROUPS):
        dh_g, (landed,) = mm([(dqs[g], w_in_t)], "nn", BF16, f"mm_d_h_qkv{g}", tk=GROUP_W, b_window=qkv_rows(g), carry=chunk(g))
        dh_parts.append(dh_g if dil == 1 else dh_g.reshape(dil, s // dil, D_MODEL))
    dh_u, (landed,) = mm([(du, w_u_t)], "nn", BF16, "mm_d_h_u", carry=chunk(3))
    dh_gates, (landed,) = mm([(dgates, w_gates_t)], "nn", BF16, "mm_d_h_gates", carry=chunk(4))
    dh_parts += [dh_u, dh_gates]

    def in_bwd(r, c):
        dh1 = _permute(_perm_matrix(PERM_TS, 4, True), r[1].astype(BF16))
        dh2_ = _permute(_perm_matrix(PERM_TS, 16, True), r[2].astype(BF16))
        dh = r[0] + dh1 + dh2_ + r[3] + r[4]
        _, n1, r1 = _rms(r[6], c[0])
        dx, dg1_ = _rms_bwd(dh, n1, r1, c[0])
        return [dx + r[5]], [dg1_]

    (grad_x, dg1), (landed,) = rowwise("rms_in_bwd", in_bwd, dh_parts + [dx1, x], [g1], [(D_MODEL, F32)],
                                       acc_outs=[(1, D_MODEL)], ts=PERM_TS, carry=chunk(5))
    recv["w_in"] = landed

    dsmall = dict(norm_mix_pre=dg1, ssm_a_re=da_re, ssm_a_im=da_im, ssm_log_dt=dldt, ssm_b_re=db_re, ssm_b_im=db_im,
                  ssm_c_re=dc_re, ssm_c_im=dc_im, ssm_d=dd_ssm, norm_mix_post=dg2, norm_ffn_pre=dg3, norm_ffn_post=dg4)
    return loss_part, grad_x, recv, dsmall


def adamw(parts, w, m, v, name, carry=None):
    r, c = w.shape
    tr = r
    while tr > 8 and tr % 2 == 0 and tr * c * (8 * parts.dtype.itemsize + 28) * 2 > 24 * 1024 * 1024:
        tr //= 2
    assert r % tr == 0 and (tr % 8 == 0 or tr == r)
    c1, c2 = 1.0 / (1.0 - ADAM_B1 ** ADAM_STEP), 1.0 / (1.0 - ADAM_B2 ** ADAM_STEP)

    def body(p_ref, w_ref, m_ref, v_ref, g_o, d_o, m_o, v_o):
        g = p_ref[0].astype(F32)
        for i in range(1, N_DEV):
            g = g + p_ref[i].astype(F32)
        mn = ADAM_B1 * m_ref[...] + (1.0 - ADAM_B1) * g
        vn = ADAM_B2 * v_ref[...] + (1.0 - ADAM_B2) * (g * g)
        g_o[...] = g
        m_o[...] = mn
        v_o[...] = vn
        d_o[...] = -ADAM_LR * ((mn * c1) / (jnp.sqrt(vn * c2) + ADAM_EPS) + ADAM_WD * w_ref[...])

    blk = pl.BlockSpec((tr, c), lambda i: (i, 0))
    return _run(
        body, [parts, w, m, v], carry=carry, name=name, grid=(r // tr,),
        in_specs=[pl.BlockSpec((N_DEV, tr, c), lambda i: (0, i, 0)), blk, blk, blk],
        out_specs=[blk] * 4, out_shape=[jax.ShapeDtypeStruct((r, c), F32)] * 4, compiler_params=_cparams(("parallel",)),
    )


PACK_C = 1024
SHARDED = ("w_in", "w_attn_up", "w_glu_v", "w_glu_g", "w_out", "w_ffn_gate", "w_ffn_up", "w_ffn_down")
ROW_SHARDED = ("w_out", "w_ffn_down")
SENT_TRANSPOSED = ("w_in", "w_ffn_gate", "w_ffn_up")
GRAD_TRANSPOSED = ("w_ffn_gate", "w_ffn_up")
SMALL = ("norm_mix_pre", "ssm_a_re", "ssm_a_im", "ssm_log_dt", "ssm_b_re", "ssm_b_im", "ssm_c_re", "ssm_c_im", "ssm_d",
         "norm_mix_post", "norm_ffn_pre", "norm_ffn_post")
WEIGHTS = ("norm_mix_pre", "w_in", "w_attn_up", "ssm_a_re", "ssm_a_im", "ssm_log_dt", "ssm_b_re", "ssm_b_im", "ssm_c_re",
           "ssm_c_im", "ssm_d", "w_glu_v", "w_glu_g", "w_out", "norm_mix_post", "norm_ffn_pre", "w_ffn_gate", "w_ffn_up",
           "w_ffn_down", "norm_ffn_post")


def _pack(arrs, dtype, pad_rows_to=64):
    flat = jnp.concatenate([a.reshape(-1).astype(dtype) for a in arrs])
    n = flat.shape[0]
    rows = -(-n // PACK_C)
    rows = -(-rows // pad_rows_to) * pad_rows_to
    return jnp.pad(flat, (0, rows * PACK_C - n)).reshape(rows, PACK_C)


def _unpack(flat2d, shapes):
    flat = flat2d.reshape(-1)
    out, off = [], 0
    for shp in shapes:
        n = int(np.prod(shp))
        out.append(flat[off:off + n].reshape(shp))
        off += n
    return out


def _full_from_gathered(gathered, name):
    if name in ROW_SHARDED or name in SENT_TRANSPOSED:
        return gathered.reshape(-1, gathered.shape[2])
    return gathered.transpose(1, 0, 2).reshape(gathered.shape[1], -1)


def _split_for_devices(full, name):
    if name in ROW_SHARDED or name in GRAD_TRANSPOSED:
        return full.reshape(N_DEV, -1, full.shape[1])
    return full.reshape(full.shape[0], N_DEV, -1).transpose(1, 0, 2)


def kernel(x, norm_mix_pre, w_in, w_attn_up, ssm_a_re, ssm_a_im, ssm_log_dt, ssm_b_re, ssm_b_im, ssm_c_re, ssm_c_im, ssm_d, w_glu_v, w_glu_g, w_out, norm_mix_post, norm_ffn_pre, w_ffn_gate, w_ffn_up, w_ffn_down, norm_ffn_post, loss_target, m_norm_mix_pre, m_w_in, m_w_attn_up, m_ssm_a_re, m_ssm_a_im, m_ssm_log_dt, m_ssm_b_re, m_ssm_b_im, m_ssm_c_re, m_ssm_c_im, m_ssm_d, m_w_glu_v, m_w_glu_g, m_w_out, m_norm_mix_post, m_norm_ffn_pre, m_w_ffn_gate, m_w_ffn_up, m_w_ffn_down, m_norm_ffn_post, v_norm_mix_pre, v_w_in, v_w_attn_up, v_ssm_a_re, v_ssm_a_im, v_ssm_log_dt, v_ssm_b_re, v_ssm_b_im, v_ssm_c_re, v_ssm_c_im, v_ssm_d, v_w_glu_v, v_w_glu_g, v_w_out, v_norm_mix_post, v_norm_ffn_pre, v_w_ffn_gate, v_w_ffn_up, v_w_ffn_down, v_norm_ffn_post):
    args = dict(locals())
    wv = {n: args[n][0] for n in WEIGHTS}
    mv = {n: args["m_" + n][0] for n in WEIGHTS}
    vv = {n: args["v_" + n][0] for n in WEIGHTS}

    shards = {n: (wv[n].T if n in SENT_TRANSPOSED else wv[n]).astype(BF16) for n in SHARDED}
    small = {n: wv[n] for n in SMALL}
    loss_part, grad_x, recv, dsmall = local_step(x[0], loss_target[0], shards, small)
    for n in GRAD_TRANSPOSED:
        recv[n] = recv[n].transpose(0, 2, 1)

    small_shapes = [wv[n].shape for n in SMALL]
    res = {}
    res["w_in"], (sgather,) = adamw(recv["w_in"], wv["w_in"], mv["w_in"], vv["w_in"], "adamw_w_in",
                                    carry=Gather([_pack([dsmall[n] for n in SMALL], F32)]))
    for n in SHARDED[1:]:
        res[n] = adamw(recv[n], wv[n], mv[n], vv[n], "adamw_" + n)
    sres = adamw(sgather, _pack([wv[n] for n in SMALL], F32), _pack([mv[n] for n in SMALL], F32),
                 _pack([vv[n] for n in SMALL], F32), "adamw_small")
    sun = [_unpack(t, small_shapes) for t in sres]
    for k, n in enumerate(SMALL):
        res[n] = tuple(sun[t][k] for t in range(4))

    loss = lax.psum(loss_part[0, 0], ("x", "y", "c"))
    outs = [loss, grad_x[None]]
    for t in range(4):
        outs += [res[n][t][None] for n in WEIGHTS]
    return tuple(outs)
```

```python
import math

import numpy as np
import jax
import jax.numpy as jnp
from jax import lax
from jax.experimental import pallas as pl
from jax.experimental.pallas import tpu as pltpu

F32 = jnp.float32
BF16 = jnp.bfloat16

D_MODEL = 2048
HEAD_DIM = 128
HEADS_PER_GROUP = 4
ATTN_GROUPS = ((128, 1), (512, 4), (2048, 16))
N_HEADS = HEADS_PER_GROUP * len(ATTN_GROUPS)
GROUP_W = HEADS_PER_GROUP * HEAD_DIM
HQ = N_HEADS * HEAD_DIM
SSM_W = 1024
SSM_GROUP = 16
SSM_GROUPS = 64
SSM_STATE = 64
STATE_W = SSM_GROUPS * SSM_STATE
D_FF = 5632
EPS = 1e-6
N_DEV = 8
SEGS = 8
BD = 8

ADAM_LR, ADAM_B1, ADAM_B2, ADAM_EPS, ADAM_WD, ADAM_STEP = 0.001, 0.9, 0.999, 1e-08, 0.01, 10

VMEM_LIMIT = 56 * 1024 * 1024
HBM_SPEC = pl.BlockSpec(memory_space=pltpu.HBM)
MESH_ID = pl.DeviceIdType.MESH
NEG = -1e30


def _pcall(body, **kw):
    return pl.pallas_call(body, **kw)


def _cparams(sem=None):
    if sem is None:
        return pltpu.CompilerParams(vmem_limit_bytes=VMEM_LIMIT)
    return pltpu.CompilerParams(vmem_limit_bytes=VMEM_LIMIT, dimension_semantics=sem)


def _my_coords():
    return lax.axis_index("x"), lax.axis_index("y"), lax.axis_index("c")


class Gather:
    def __init__(self, xs, pass_early=False):
        self.arrays = list(xs)
        self.out_shapes = [jax.ShapeDtypeStruct((N_DEV,) + x.shape, x.dtype) for x in xs]
        self.pass_early = pass_early

    def _ctx(self, out_refs, send_sems, recv_sems):
        mx, my, mc = _my_coords()
        me, sibling = (mx, my, mc), (mx, my, 1 - mc)
        chips = [(1 - mx, my), (mx, 1 - my), (1 - mx, 1 - my)]

        def slot(a, px, py, pc):
            return out_refs[a].at[4 * px + 2 * py + pc]

        def copy(a, k, block, to, src=None):
            return pltpu.make_async_remote_copy(
                src_ref=slot(a, *block) if src is None else src, dst_ref=slot(a, *block),
                send_sem=send_sems.at[7 * a + k], recv_sem=recv_sems.at[7 * a + k], device_id=to, device_id_type=MESH_ID)

        return me, sibling, chips, mc, slot, copy

    def _first(self, a, x_refs, ctx):
        me, sibling, chips, mc, slot, copy = ctx
        return [copy(a, 0, me, sibling, src=x_refs[a])] + [copy(a, 1 + j, me, (*chip, mc), src=x_refs[a]) for j, chip in enumerate(chips)]

    def start(self, x_refs, out_refs, send_sems, recv_sems, local_sems):
        ctx = self._ctx(out_refs, send_sems, recv_sems)
        me, slot = ctx[0], ctx[4]
        for a in range(len(self.arrays)):
            pltpu.make_async_copy(x_refs[a], slot(a, *me), local_sems.at[a]).start()
            for cp in self._first(a, x_refs, ctx):
                cp.start()

    def _passed(self, ctx):
        me, sibling, chips, mc, slot, copy = ctx
        return [copy(a, 4 + j, (*chip, mc), sibling) for a in range(len(self.arrays)) for j, chip in enumerate(chips)]

    def middle(self, x_refs, out_refs, send_sems, recv_sems, local_sems):
        ctx = self._ctx(out_refs, send_sems, recv_sems)
        me, sibling, chips, mc, slot, copy = ctx
        for a in range(len(self.arrays)):
            for j, chip in enumerate(chips):
                copy(a, 1 + j, (*chip, mc), me).wait_recv()
                copy(a, 4 + j, (*chip, mc), sibling).start()

    def finish(self, x_refs, out_refs, send_sems, recv_sems, local_sems, passed_on=False):
        if not passed_on:
            self.middle(x_refs, out_refs, send_sems, recv_sems, local_sems)
        ctx = self._ctx(out_refs, send_sems, recv_sems)
        me, sibling, chips, mc, slot, copy = ctx
        na = len(self.arrays)
        passed = self._passed(ctx)
        for a in range(na):
            copy(a, 0, sibling, me).wait_recv()
            for j, chip in enumerate(chips):
                copy(a, 4 + j, (*chip, 1 - mc), me).wait_recv()
        for a in range(na):
            for cp in self._first(a, x_refs, ctx):
                cp.wait_send()
        for cp in passed:
            cp.wait_send()
        for a in range(na):
            pltpu.make_async_copy(x_refs[a], slot(a, *me), local_sems.at[a]).wait()


class AllToAll:
    def __init__(self, ps):
        self.arrays = list(ps)
        self.out_shapes = [jax.ShapeDtypeStruct(p.shape, p.dtype) for p in ps]

    def _copies(self, p_refs, out_refs, send_sems, recv_sems, local_sems):
        mx, my, mc = _my_coords()
        me = 4 * mx + 2 * my + mc
        local, remote = [], []
        for a in range(len(self.arrays)):
            local.append(pltpu.make_async_copy(p_refs[a].at[me], out_refs[a].at[me], local_sems.at[a]))
            for k in range(1, N_DEV):
                px, py, pc = mx ^ ((k >> 2) & 1), my ^ ((k >> 1) & 1), mc ^ (k & 1)
                remote.append(pltpu.make_async_remote_copy(
                    src_ref=p_refs[a].at[4 * px + 2 * py + pc], dst_ref=out_refs[a].at[me],
                    send_sem=send_sems.at[7 * a + k - 1], recv_sem=recv_sems.at[7 * a + k - 1],
                    device_id=(px, py, pc), device_id_type=MESH_ID))
        return local, remote

    def start(self, *refs):
        local, remote = self._copies(*refs)
        for cp in local + remote:
            cp.start()

    def finish(self, *refs):
        local, remote = self._copies(*refs)
        for cp in remote:
            cp.wait_recv()
        for cp in remote:
            cp.wait_send()
        for cp in local:
            cp.wait()


class RowsToOwners:
    def __init__(self, p, r0, n, into=None):
        self.arrays = [p] if into is None else [p, into]
        self.out_shapes = [jax.ShapeDtypeStruct(p.shape, p.dtype)]
        self.aliases = {} if into is None else {1: 0}
        self.rows = (r0, n)

    def _copies(self, p_refs, out_refs, send_sems, recv_sems, local_sems):
        mx, my, mc = _my_coords()
        me = 4 * mx + 2 * my + mc
        rows = pl.ds(*self.rows)
        local = [pltpu.make_async_copy(p_refs[0].at[me, rows], out_refs[0].at[me, rows], local_sems.at[0])]
        remote = []
        for k in range(1, N_DEV):
            px, py, pc = mx ^ ((k >> 2) & 1), my ^ ((k >> 1) & 1), mc ^ (k & 1)
            remote.append(pltpu.make_async_remote_copy(
                src_ref=p_refs[0].at[4 * px + 2 * py + pc, rows], dst_ref=out_refs[0].at[me, rows],
                send_sem=send_sems.at[k - 1], recv_sem=recv_sems.at[k - 1], device_id=(px, py, pc), device_id_type=MESH_ID))
        return local, remote

    start = AllToAll.start
    finish = AllToAll.finish


class ChipRowsToOwners(RowsToOwners):
    def _copies(self, p_refs, out_refs, send_sems, recv_sems, local_sems):
        mx, my, mc = _my_coords()
        me = 2 * mx + my
        rows = pl.ds(*self.rows)
        local = [pltpu.make_async_copy(p_refs[0].at[me, rows], out_refs[0].at[me, rows], local_sems.at[0])]
        remote = []
        for k in range(1, N_DEV // 2):
            px, py = mx ^ ((k >> 1) & 1), my ^ (k & 1)
            remote.append(pltpu.make_async_remote_copy(
                src_ref=p_refs[0].at[2 * px + py, rows], dst_ref=out_refs[0].at[me, rows],
                send_sem=send_sems.at[k - 1], recv_sem=recv_sems.at[k - 1], device_id=(px, py, mc), device_id_type=MESH_ID))
        return local, remote


class SiblingSwap:
    def __init__(self, x):
        self.arrays = [x]
        self.out_shapes = [jax.ShapeDtypeStruct(x.shape, x.dtype)]

    def _copy(self, x_refs, out_refs, send_sems, recv_sems, local_sems):
        mx, my, mc = _my_coords()
        return pltpu.make_async_remote_copy(src_ref=x_refs[0], dst_ref=out_refs[0], send_sem=send_sems.at[0],
                                            recv_sem=recv_sems.at[0], device_id=(mx, my, 1 - mc), device_id_type=MESH_ID)

    def start(self, *refs):
        self._copy(*refs).start()

    def finish(self, *refs):
        self._copy(*refs).wait()


def _run(body, args, carry=None, **kw):
    if carry is None:
        return _pcall(body, **kw)(*args)
    grid = kw["grid"]
    single = not isinstance(kw["out_shape"], (list, tuple))
    in_specs = list(kw["in_specs"])
    out_specs = [kw["out_specs"]] if single else list(kw["out_specs"])
    out_shape = [kw["out_shape"]] if single else list(kw["out_shape"])
    scratch = list(kw.get("scratch_shapes", []))
    na, nin, nout, nscr = len(carry.arrays), len(in_specs), len(out_specs), len(scratch)
    nco = len(carry.out_shapes)
    aliases = {nin + i: nout + o for i, o in getattr(carry, "aliases", {}).items()}
    steps = int(np.prod(grid))
    mid_step = (steps * 7) // 10 if getattr(carry, "pass_early", False) and steps >= 4 else None

    def carried(*refs):
        ins, cin = refs[:nin], refs[nin:nin + na]
        outs, cout = refs[nin + na:nin + na + nout], refs[nin + na + nout:nin + na + nout + nco]
        scr = refs[nin + na + nout + nco:nin + na + nout + nco + nscr]
        sems = refs[nin + na + nout + nco + nscr:]
        step = pl.program_id(0)
        for i in range(1, len(grid)):
            step = step * grid[i] + pl.program_id(i)

        @pl.when(step == 0)
        def _():
            carry.start(cin, cout, *sems)

        if mid_step is not None:
            @pl.when(step == mid_step)
            def _():
                carry.middle(cin, cout, *sems)

        body(*ins, *outs, *scr)

        @pl.when(step == steps - 1)
        def _():
            if mid_step is not None:
                carry.finish(cin, cout, *sems, passed_on=True)
            else:
                carry.finish(cin, cout, *sems)

    res = _pcall(
        carried, name=kw["name"], grid=grid, in_specs=in_specs + [HBM_SPEC] * na, out_specs=out_specs + [HBM_SPEC] * nco,
        out_shape=out_shape + carry.out_shapes, input_output_aliases=aliases,
        scratch_shapes=scratch + [pltpu.SemaphoreType.DMA((7 * na,)), pltpu.SemaphoreType.DMA((7 * na,)), pltpu.SemaphoreType.DMA((na,))],
        compiler_params=_cparams(("arbitrary",) * len(grid)),
    )(*args, *carry.arrays)
    main = res[:nout]
    return (main[0] if single else main), list(res[nout:])


_DN = {"nn": (((1,), (0,)), ((), ())), "nt": (((1,), (1,)), ((), ())), "tn": (((0,), (0,)), ((), ()))}


LANE = 128
MM_TM, MM_TN, MM_TK = 1024, 1536, 2048


def _tile(n, cap):
    for t in range(min(cap, n) // LANE * LANE, 0, -LANE):
        if n % t == 0:
            return t
    raise ValueError(n)


DW_TM, DW_TN, DW_TK = 512, 512, 8192
EPILOGUE_SPLIT = 2


def mm(pairs, mode, out_dtype, name, tm=None, tn=None, tk=None, carry=None, epilogue=None, extras=(), b_window=None):
    a0, b0 = pairs[0]
    if mode == "nn":
        (m, k), n = a0.shape, b0.shape[1]
    elif mode == "nt":
        (m, k), n = a0.shape, b0.shape[0]
    else:
        (k, m), n = a0.shape, b0.shape[1]
    if b_window is not None:
        assert mode in ("nn", "nt") and len(pairs) == 1
        if mode == "nt":
            n = b_window[0]
        else:
            assert k == b_window[0]
    caps = (DW_TM, DW_TN, DW_TK) if mode == "tn" else (MM_TM, MM_TN, MM_TK)
    tm, tn, tk = _tile(m, tm or caps[0]), _tile(n, tn or caps[1]), _tile(k, tk or caps[2])
    nk = k // tk
    npairs = len(pairs)
    nex = len(extras)
    fused = epilogue is not None
    assert not fused or nk == 1
    out_dtypes = list(out_dtype) if fused else [out_dtype]

    def body(*refs):
        if fused:
            half = tn // EPILOGUE_SPLIT
            for c in range(EPILOGUE_SPLIT):
                cols = slice(c * half, (c + 1) * half)
                prods = []
                for p in range(npairs):
                    a = refs[2 * p][...].astype(BF16)
                    b = (refs[2 * p + 1][cols, :] if mode == "nt" else refs[2 * p + 1][:, cols]).astype(BF16)
                    prods.append(lax.dot_general(a, b, _DN[mode], preferred_element_type=F32))
                ex = [refs[2 * npairs + e][:, cols].astype(F32) for e in range(nex)]
                for o_ref, val in zip(refs[2 * npairs + nex:], epilogue(prods, ex)):
                    o_ref[:, cols] = val.astype(o_ref.dtype)
            return
        prods = []
        for p in range(npairs):
            a = refs[2 * p][...].astype(BF16) if (p == 0 or pairs[p][0] is not pairs[p - 1][0]) else a
            b = refs[2 * p + 1][...].astype(BF16)
            prods.append(lax.dot_general(a, b, _DN[mode], preferred_element_type=F32))
        o_ref = refs[2 * npairs]
        tot = prods[0]
        for d in prods[1:]:
            tot = tot + d
        if nk == 1:
            o_ref[...] = tot.astype(o_ref.dtype)
            return
        acc = refs[2 * npairs + 1]
        kk = pl.program_id(2)

        @pl.when(kk == 0)
        def _():
            acc[...] = tot

        @pl.when(kk > 0)
        def _():
            acc[...] += tot

        @pl.when(kk == nk - 1)
        def _():
            o_ref[...] = acc[...].astype(o_ref.dtype)

    rows_of = b_window[1] if b_window is not None else (lambda t: t)
    if mode == "nn":
        sp = [pl.BlockSpec((tm, tk), lambda i, j, kk: (i, kk)), pl.BlockSpec((tk, tn), lambda i, j, kk: (rows_of(kk), j))]
    elif mode == "nt":
        sp = [pl.BlockSpec((tm, tk), lambda i, j, kk: (i, kk)), pl.BlockSpec((tn, tk), lambda i, j, kk: (rows_of(j), kk))]
    else:
        sp = [pl.BlockSpec((tk, tm), lambda i, j, kk: (kk, i)), pl.BlockSpec((tk, tn), lambda i, j, kk: (kk, j))]
    o_spec = pl.BlockSpec((tm, tn), lambda i, j, kk: (i, j))
    out_shapes = [jax.ShapeDtypeStruct((m, n), dt) for dt in out_dtypes]
    return _run(
        body, [t for pr in pairs for t in pr] + list(extras), carry=carry, name=name, grid=(m // tm, n // tn, nk),
        in_specs=sp * npairs + [o_spec] * nex,
        out_specs=[o_spec] * len(out_shapes) if fused else o_spec,
        out_shape=out_shapes if fused else out_shapes[0],
        scratch_shapes=[pltpu.VMEM((tm, tn), F32)] if nk > 1 else [],
        compiler_params=_cparams(("parallel", "parallel", "arbitrary")),
    )


def rowwise(name, fn, row_ins, const_ins, row_outs, acc_outs=(), ts=None, carry=None):
    s = row_ins[0].shape[0]
    row_outs = [ro if len(ro) == 3 else (*ro, 1) for ro in row_outs]
    if ts is None:
        per_row = sum(a.shape[-1] * a.dtype.itemsize for a in row_ins) + sum(w * jnp.dtype(dt).itemsize for w, dt, _ in row_outs)
        ts = 512
        while ts > 8 and 2 * ts * per_row > 20 * 1024 * 1024:
            ts //= 2
    ts = min(ts, s)
    assert s % ts == 0
    nr, nc, no, na = len(row_ins), len(const_ins), len(row_outs), len(acc_outs)

    def body(*refs):
        rows = [r[...].reshape(ts, r.shape[-1]).astype(F32) for r in refs[:nr]]
        consts = [r[...] for r in refs[nr:nr + nc]]
        outs, accs = fn(rows, consts)
        for r, v in zip(refs[nr + nc:nr + nc + no], outs):
            r[...] = v.astype(r.dtype).reshape(r.shape)
        if na:
            first = pl.program_id(0) == 0
            for r, v in zip(refs[nr + nc + no:], accs):
                @pl.when(first)
                def _(r=r, v=v):
                    r[...] = v

                @pl.when(jnp.logical_not(first))
                def _(r=r, v=v):
                    r[...] += v

    def tile_spec(w, d):
        if d == 1:
            return pl.BlockSpec((ts, w), lambda i: (i, 0))
        return pl.BlockSpec((d, ts // d, w), lambda i: (0, i, 0))

    in_specs = [tile_spec(a.shape[-1], a.shape[0] if a.ndim == 3 else 1) for a in row_ins]
    in_specs += [pl.BlockSpec(c.shape, lambda i, nd=c.ndim: (0,) * nd) for c in const_ins]
    out_specs = [tile_spec(w, d) for w, _, d in row_outs]
    out_specs += [pl.BlockSpec(shp, lambda i, nd=len(shp): (0,) * nd) for shp in acc_outs]
    out_shape = [jax.ShapeDtypeStruct((s, w) if d == 1 else (d, s // d, w), dt) for w, dt, d in row_outs]
    out_shape += [jax.ShapeDtypeStruct(shp, F32) for shp in acc_outs]
    return _run(
        body, [*row_ins, *const_ins], carry=carry, name=name, grid=(s // ts,), in_specs=in_specs, out_specs=out_specs,
        out_shape=out_shape, compiler_params=_cparams(("arbitrary",)),
    )


PERM_TS = 256


def _perm_matrix(ts, d, inverse):
    i = lax.broadcasted_iota(jnp.int32, (ts, ts), 0)
    k = lax.broadcasted_iota(jnp.int32, (ts, ts), 1)
    per = ts // d
    src = (i % d) * per + i // d if inverse else (i % per) * d + i // per
    return jnp.where(k == src, 1.0, 0.0).astype(BF16)


def _permute(p, x):
    if x.dtype == BF16:
        return jnp.dot(p, x, preferred_element_type=F32)
    hi = x.astype(BF16)
    rest = x - hi.astype(F32)
    mid = rest.astype(BF16)
    lo = (rest - mid.astype(F32)).astype(BF16)
    out = jnp.dot(p, hi, preferred_element_type=F32) + jnp.dot(p, mid, preferred_element_type=F32)
    return out + jnp.dot(p, lo, preferred_element_type=F32)


def _rms(x, gain):
    r = lax.rsqrt(jnp.mean(x * x, axis=-1, keepdims=True) + EPS)
    n = x * r
    return n * gain, n, r


def _rms_bwd(dy, n, r, gain):
    dn = dy * gain
    dx = r * (dn - n * jnp.mean(dn * n, axis=-1, keepdims=True))
    return dx, jnp.sum(dy * n, axis=0, keepdims=True)


def _sigmoid(x):
    return 1.0 / (1.0 + jnp.exp(-x))


_GELU_K = math.sqrt(2.0 / math.pi)


def _gelu(x):
    t = jnp.tanh(_GELU_K * (x + 0.044715 * x * x * x))
    return 0.5 * x * (1.0 + t), t


def _gelu_grad(x, t):
    return 0.5 * (1.0 + t) + 0.5 * x * (1.0 - t * t) * _GELU_K * (1.0 + 3.0 * 0.044715 * x * x)


def _head_sum(x):
    parts = []
    for h in range(HEADS_PER_GROUP):
        sl = x[:, h * HEAD_DIM:(h + 1) * HEAD_DIM]
        parts.append(jnp.broadcast_to(jnp.sum(sl, axis=-1, keepdims=True), sl.shape))
    return jnp.concatenate(parts, axis=-1)


def _mix_weights(l0, l1, l2):
    mx = jnp.maximum(jnp.maximum(l0, l1), l2)
    e0, e1, e2 = jnp.exp(l0 - mx), jnp.exp(l1 - mx), jnp.exp(l2 - mx)
    inv = 1.0 / (e0 + e1 + e2)
    return e0 * inv, e1 * inv, e2 * inv


BLK = 128


def _slopes(g):
    return [2.0 ** (-8.0 * (g * HEADS_PER_GROUP + h + 1) / N_HEADS) for h in range(HEADS_PER_GROUP)]


def _attn_masks(dil):
    qi = lax.broadcasted_iota(jnp.int32, (BLK, BLK), 0)
    ki = lax.broadcasted_iota(jnp.int32, (BLK, BLK), 1)
    dist_c = qi - ki
    dist_p = BLK + qi - ki
    return dist_c >= 0, dist_p <= BLK, (dist_c * dil).astype(F32), (dist_p * dil).astype(F32)


def _window_mask(has_prev, dil):
    qi = lax.broadcasted_iota(jnp.int32, (BLK, 2 * BLK), 0)
    ki = lax.broadcasted_iota(jnp.int32, (BLK, 2 * BLK), 1)
    dist = BLK + qi - ki
    ok = jnp.logical_and(jnp.logical_and(dist >= 0, dist <= BLK), jnp.logical_or(ki >= BLK, has_prev))
    return ok, (dist * dil).astype(F32)


def attn_fwd(qkv, g, name):
    dil, length, _ = qkv.shape
    scale = HEAD_DIM ** -0.5
    slopes = _slopes(g)

    def body(q_ref, kc_ref, vc_ref, kp_ref, vp_ref, o_ref, l_ref):
        ok, dist = _window_mask(pl.program_id(1) > 0, dil)
        for h in range(HEADS_PER_GROUP):
            sl = slice(h * HEAD_DIM, (h + 1) * HEAD_DIM)
            k2 = jnp.concatenate([kp_ref[:, sl], kc_ref[:, sl]], axis=0)
            v2 = jnp.concatenate([vp_ref[:, sl], vc_ref[:, sl]], axis=0)
            s = lax.dot_general(q_ref[:, sl], k2, _DN["nt"], preferred_element_type=F32) * scale - slopes[h] * dist
            s = jnp.where(ok, s, NEG)
            mx = jnp.max(s, axis=-1, keepdims=True)
            p = jnp.exp(s - mx)
            den = jnp.sum(p, axis=-1, keepdims=True)
            o_ref[:, sl] = (jnp.dot(p.astype(BF16), v2, preferred_element_type=F32) / den).astype(BF16)
            l_ref[:, sl] = jnp.broadcast_to(mx + jnp.log(den), (BLK, HEAD_DIM))

    def spec(col, prev):
        if prev:
            return pl.BlockSpec((None, BLK, GROUP_W), lambda r, n: (r, jnp.maximum(n - 1, 0), col))
        return pl.BlockSpec((None, BLK, GROUP_W), lambda r, n: (r, n, col))

    out_spec = pl.BlockSpec((None, BLK, GROUP_W), lambda r, n: (r, n, 0))
    return _pcall(
        body, name=name, grid=(dil, length // BLK),
        in_specs=[spec(0, False), spec(1, False), spec(2, False), spec(1, True), spec(2, True)],
        out_specs=[out_spec, out_spec],
        out_shape=[jax.ShapeDtypeStruct((dil, length, GROUP_W), BF16), jax.ShapeDtypeStruct((dil, length, GROUP_W), F32)],
        compiler_params=_cparams(("parallel", "parallel")),
    )(qkv, qkv, qkv, qkv, qkv)


def attn_bwd(qkv, dout, lse, dd, g, name, carry=None):
    dil, length, _ = qkv.shape
    nblk = length // BLK
    scale = HEAD_DIM ** -0.5
    slopes = _slopes(g)

    def body(q_ref, kc_ref, vc_ref, kp_ref, vp_ref, qn_ref, do_ref, don_ref, l_ref, ln_ref, d_ref, dn_ref, o_ref):
        n = pl.program_id(1)
        ok2, dist2 = _window_mask(n > 0, dil)
        _, ok_p, _, dp = _attn_masks(dil)
        ok_next = jnp.logical_and(ok_p, n < nblk - 1)
        for h in range(HEADS_PER_GROUP):
            sl = slice(h * HEAD_DIM, (h + 1) * HEAD_DIM)
            q, kc, vc, qn = q_ref[:, sl], kc_ref[:, sl], vc_ref[:, sl], qn_ref[:, sl]
            k2 = jnp.concatenate([kp_ref[:, sl], kc], axis=0)
            v2 = jnp.concatenate([vp_ref[:, sl], vc], axis=0)
            do, don = do_ref[:, sl], don_ref[:, sl]
            lse_q, lse_n, dd_q, dd_n = l_ref[:, sl], ln_ref[:, sl], d_ref[:, sl], dn_ref[:, sl]

            def probs(qq, kk, dist, ok, lse_t):
                s = lax.dot_general(qq, kk, _DN["nt"], preferred_element_type=F32) * scale - slopes[h] * dist
                return jnp.where(ok, jnp.exp(jnp.where(ok, s, NEG) - lse_t), 0.0)

            p2 = probs(q, k2, dist2, ok2, jnp.concatenate([lse_q, lse_q], axis=1))
            p_x = probs(qn, kc, dp, ok_next, lse_n)
            ds2 = p2 * (lax.dot_general(do, v2, _DN["nt"], preferred_element_type=F32) - jnp.concatenate([dd_q, dd_q], axis=1))
            ds_x = p_x * (lax.dot_general(don, vc, _DN["nt"], preferred_element_type=F32) - dd_n)
            dq = jnp.dot(ds2.astype(BF16), k2, preferred_element_type=F32)
            ds_k = jnp.concatenate([ds2[:, BLK:], ds_x], axis=0).astype(BF16)
            p_k = jnp.concatenate([p2[:, BLK:], p_x], axis=0).astype(BF16)
            dk = lax.dot_general(ds_k, jnp.concatenate([q, qn], axis=0), _DN["tn"], preferred_element_type=F32)
            dv = lax.dot_general(p_k, jnp.concatenate([do, don], axis=0), _DN["tn"], preferred_element_type=F32)
            o_ref[:, h * HEAD_DIM:(h + 1) * HEAD_DIM] = (dq * scale).astype(BF16)
            o_ref[:, GROUP_W + h * HEAD_DIM:GROUP_W + (h + 1) * HEAD_DIM] = (dk * scale).astype(BF16)
            o_ref[:, 2 * GROUP_W + h * HEAD_DIM:2 * GROUP_W + (h + 1) * HEAD_DIM] = dv.astype(BF16)

    def spec(col, which):
        if which == "prev":
            return pl.BlockSpec((None, BLK, GROUP_W), lambda r, n: (r, jnp.maximum(n - 1, 0), col))
        if which == "next":
            return pl.BlockSpec((None, BLK, GROUP_W), lambda r, n: (r, jnp.minimum(n + 1, nblk - 1), col))
        return pl.BlockSpec((None, BLK, GROUP_W), lambda r, n: (r, n, col))

    return _run(
        body, [qkv, qkv, qkv, qkv, qkv, qkv, dout, dout, lse, lse, dd, dd], carry=carry, name=name, grid=(dil, nblk),
        in_specs=[spec(0, "cur"), spec(1, "cur"), spec(2, "cur"), spec(1, "prev"), spec(2, "prev"), spec(0, "next"),
                  spec(0, "cur"), spec(0, "next"), spec(0, "cur"), spec(0, "next"), spec(0, "cur"), spec(0, "next")],
        out_specs=pl.BlockSpec((None, BLK, 3 * GROUP_W), lambda r, n: (r, n, 0)),
        out_shape=jax.ShapeDtypeStruct((dil, length, 3 * GROUP_W), BF16),
        compiler_params=_cparams(("parallel", "parallel")),
    )


def _ssm_prep_values(are, aim, logdt):
    dt = jnp.exp(logdt)
    mag = jnp.exp(are * dt)
    lb_re, lb_im = mag * jnp.cos(aim * dt), mag * jnp.sin(aim * dt)
    inv = 1.0 / (are * are + aim * aim)
    n_re, n_im = lb_re - 1.0, lb_im
    f_re = (n_re * are + n_im * aim) * inv
    f_im = (n_im * are - n_re * aim) * inv
    return dt, lb_re, lb_im, f_re, f_im, inv


PREP_G = 8


def _group_specs(are, logdt, bre):
    def spec(a):
        return pl.BlockSpec((PREP_G,) + a.shape[1:], lambda i: (i, 0, 0))
    return spec(are), spec(logdt), spec(bre)


def ssm_prep(are, aim, logdt, bre, bim):
    def body(are_r, aim_r, ldt_r, bre_r, bim_r, lre_o, lim_o, bbre_o, bbim_o):
        _, lb_re, lb_im, f_re, f_im, _ = _ssm_prep_values(are_r[...], aim_r[...], ldt_r[...])
        lre_o[...] = lb_re
        lim_o[...] = lb_im
        bbre_o[...] = f_re * bre_r[...] - f_im * bim_r[...]
        bbim_o[...] = f_re * bim_r[...] + f_im * bre_r[...]

    sh1 = jax.ShapeDtypeStruct(are.shape, F32)
    shb = jax.ShapeDtypeStruct(bre.shape, F32)
    s1, sd, sb = _group_specs(are, logdt, bre)
    return _pcall(body, name="ssm_prep", grid=(SSM_GROUPS // PREP_G,), in_specs=[s1, s1, sd, sb, sb], out_specs=[s1, s1, sb, sb],
                  out_shape=[sh1, sh1, shb, shb], compiler_params=_cparams(("parallel",)))(are, aim, logdt, bre, bim)


def ssm_prep_bwd(are, aim, logdt, bre, bim, dbbre, dbbim, dlre, dlim):
    def body(are_r, aim_r, ldt_r, bre_r, bim_r, dbbre_r, dbbim_r, dlre_r, dlim_r, dare_o, daim_o, dldt_o, dbre_o, dbim_o):
        are_v, aim_v = are_r[...], aim_r[...]
        dt, lb_re, lb_im, f_re, f_im, inv = _ssm_prep_values(are_v, aim_v, ldt_r[...])
        b_re, b_im, g_re, g_im = bre_r[...], bim_r[...], dbbre_r[...], dbbim_r[...]
        dbre_o[...] = f_re * g_re + f_im * g_im
        dbim_o[...] = f_re * g_im - f_im * g_re
        df_re = jnp.sum(b_re * g_re + b_im * g_im, axis=-1, keepdims=True)
        df_im = jnp.sum(b_re * g_im - b_im * g_re, axis=-1, keepdims=True)
        il_re, il_im = are_v * inv, -aim_v * inv
        cl_re = dlre_r[...] + il_re * df_re + il_im * df_im
        cl_im = dlim_r[...] + il_re * df_im - il_im * df_re
        q_re = -(f_re * il_re - f_im * il_im)
        q_im = -(f_re * il_im + f_im * il_re)
        ca_re = q_re * df_re + q_im * df_im
        ca_im = q_re * df_im - q_im * df_re
        cz_re = lb_re * cl_re + lb_im * cl_im
        cz_im = lb_re * cl_im - lb_im * cl_re
        dare_o[...] = ca_re + dt * cz_re
        daim_o[...] = ca_im + dt * cz_im
        dldt_o[...] = dt * jnp.sum(are_v * cz_re + aim_v * cz_im, axis=1, keepdims=True)

    sh1 = jax.ShapeDtypeStruct(are.shape, F32)
    shb = jax.ShapeDtypeStruct(bre.shape, F32)
    s1, sd, sb = _group_specs(are, logdt, bre)
    return _pcall(
        body, name="ssm_prep_bwd", grid=(SSM_GROUPS // PREP_G,), in_specs=[s1, s1, sd, sb, sb, sb, sb, s1, s1],
        out_specs=[s1, s1, sd, sb, sb], out_shape=[sh1, sh1, jax.ShapeDtypeStruct(logdt.shape, F32), shb, shb],
        compiler_params=_cparams(("parallel",)),
    )(are, aim, logdt, bre, bim, dbbre, dbbim, dlre, dlim)


SCAN_WC = 512


def _chain_segments(a_re, a_im, e_re, e_im, nsq, reverse):
    p_re, p_im = a_re, a_im
    for _ in range(nsq):
        p_re, p_im = p_re * p_re - p_im * p_im, 2.0 * p_re * p_im
    row = lax.broadcasted_iota(jnp.int32, e_re.shape, 0)
    edge = (row == SEGS - 1) if reverse else (row == 0)
    shift = SEGS - 1 if reverse else 1
    c_re, c_im = jnp.zeros_like(e_re), jnp.zeros_like(e_im)
    for _ in range(SEGS - 1):
        n_re = p_re * c_re - p_im * c_im + e_re
        n_im = p_re * c_im + p_im * c_re + e_im
        c_re = jnp.where(edge, 0.0, pltpu.roll(n_re, shift, 0))
        c_im = jnp.where(edge, 0.0, pltpu.roll(n_im, shift, 0))
    return c_re, c_im


def _scan_dims(s):
    steps = s // SEGS
    assert steps & (steps - 1) == 0
    tt = min(128, steps)
    return steps, tt, steps // tt, tt * SEGS, int(math.log2(steps))


U_BLK = SSM_W // BD


def ssm_fwd(u_s, dvec, w_bre, w_bim, w_cre, w_cim_neg, lre, lim, name, carry=None):
    s = u_s.shape[0]
    steps, tt, nch, rows, nsq = _scan_dims(s)
    nb, ub_w, wc = w_bre.shape

    def body(u_r, d_r, bre_r, bim_r, cre_r, cim_r, lre_r, lim_r, yg_o, ys_o, hre_o, him_o, hin_re_o, hin_im_o,
             st_re, st_im, x_re, x_im, h_re, h_im):
        ps, ch = pl.program_id(1), pl.program_id(2)
        a_re = jnp.broadcast_to(lre_r[...], (SEGS, wc))
        a_im = jnp.broadcast_to(lim_r[...], (SEGS, wc))
        ub = u_r[...]
        ub16 = ub.astype(BF16)
        x_re[...] = jnp.dot(ub16, bre_r[...], preferred_element_type=F32)
        x_im[...] = jnp.dot(ub16, bim_r[...], preferred_element_type=F32)

        @pl.when(jnp.logical_and(ps == 0, ch == 0))
        def _():
            st_re[...] = jnp.zeros_like(st_re)
            st_im[...] = jnp.zeros_like(st_im)

        @pl.when(jnp.logical_and(ps == 1, ch == 0))
        def _():
            c_re, c_im = _chain_segments(a_re, a_im, st_re[...], st_im[...], nsq, False)
            st_re[...] = c_re
            st_im[...] = c_im
            hin_re_o[...] = c_re
            hin_im_o[...] = c_im

        def run(store):
            def step(t, hc):
                off = pl.multiple_of(t * SEGS, SEGS)
                n_re = a_re * hc[0] - a_im * hc[1] + x_re[pl.ds(off, SEGS), :]
                n_im = a_re * hc[1] + a_im * hc[0] + x_im[pl.ds(off, SEGS), :]
                if store:
                    h_re[pl.ds(off, SEGS), :] = n_re
                    h_im[pl.ds(off, SEGS), :] = n_im
                return n_re, n_im

            fin = lax.fori_loop(0, tt, step, (st_re[...], st_im[...]))
            st_re[...] = fin[0]
            st_im[...] = fin[1]

        @pl.when(ps == 0)
        def _():
            run(False)

        @pl.when(ps == 1)
        def _():
            run(True)
            hr16, hi16 = h_re[...].astype(BF16), h_im[...].astype(BF16)
            hre_o[...] = hr16
            him_o[...] = hi16
            y = jnp.dot(hr16, cre_r[...], preferred_element_type=F32) + jnp.dot(hi16, cim_r[...], preferred_element_type=F32)
            y = y + d_r[...] * ub
            ys_o[...] = y
            yg_o[...] = _gelu(y)[0].astype(BF16)

    def pass1(ps, c):
        return jnp.where(ps == 1, c, 0)

    u_spec = pl.BlockSpec((rows, ub_w), lambda j, ps, c: (c, j))
    d_spec = pl.BlockSpec((1, ub_w), lambda j, ps, c: (0, j))
    b_spec = pl.BlockSpec((None, ub_w, wc), lambda j, ps, c: (j, 0, 0))
    c_spec = pl.BlockSpec((None, wc, ub_w), lambda j, ps, c: (j, 0, 0))
    l_spec = pl.BlockSpec((1, wc), lambda j, ps, c: (0, j))
    y_spec = pl.BlockSpec((rows, ub_w), lambda j, ps, c: (pass1(ps, c), j))
    h_spec = pl.BlockSpec((rows, wc), lambda j, ps, c: (pass1(ps, c), j))
    e_spec = pl.BlockSpec((SEGS, wc), lambda j, ps, c: (0, j))
    return _run(
        body, [u_s, dvec, w_bre, w_bim, w_cre, w_cim_neg, lre, lim], carry=carry, name=name, grid=(nb, 2, nch),
        in_specs=[u_spec, d_spec, b_spec, b_spec, c_spec, c_spec, l_spec, l_spec],
        out_specs=[y_spec, y_spec, h_spec, h_spec, e_spec, e_spec],
        out_shape=[jax.ShapeDtypeStruct((s, SSM_W), BF16), jax.ShapeDtypeStruct((s, SSM_W), F32),
                   jax.ShapeDtypeStruct((s, STATE_W), BF16), jax.ShapeDtypeStruct((s, STATE_W), BF16),
                   jax.ShapeDtypeStruct((SEGS, STATE_W), F32), jax.ShapeDtypeStruct((SEGS, STATE_W), F32)],
        scratch_shapes=[pltpu.VMEM((SEGS, wc), F32)] * 2 + [pltpu.VMEM((rows, wc), F32)] * 4,
        compiler_params=_cparams(("parallel", "arbitrary", "arbitrary")),
    )


def ssm_bwd(dyg_s, ys, u_s, h_re, h_im, hin_re, hin_im, gin_re, gin_im, dvec, w_bre_t, w_bim_t, w_cre_t, w_cim_neg_t, lre, lim,
            name, carry=None):
    s = u_s.shape[0]
    steps, tt, nch, rows, nsq = _scan_dims(s)
    half = 2 * SEGS

    def body(dyg_r, ys_r, u_r, hre_r, him_r, pre_r, pim_r, cin_re_r, cin_im_r, gin_re_r, gin_im_r, d_r, bre_r, bim_r, cre_r,
             cim_r, lre_r, lim_r, du_o, dbre_o, dbim_o, dcre_o, dcim_o, dlre_o, dlim_o, dd_o,
             st_re, st_im, x_re, x_im, g_re, g_im, hf_re, hf_im):
        ch = pl.program_id(1)
        a_re = jnp.broadcast_to(lre_r[...], (SEGS, SCAN_WC))
        a_im = -jnp.broadcast_to(lim_r[...], (SEGS, SCAN_WC))
        ub, y = u_r[...], ys_r[...]
        dy = dyg_r[...] * _gelu_grad(y, _gelu(y)[1])
        dy16 = dy.astype(BF16)
        x_re[...] = jnp.dot(dy16, cre_r[...], preferred_element_type=F32)
        x_im[...] = jnp.dot(dy16, cim_r[...], preferred_element_type=F32)

        @pl.when(ch == 0)
        def _():
            st_re[...] = gin_re_r[...]
            st_im[...] = gin_im_r[...]
            dlre_o[...] = jnp.zeros_like(dlre_o)
            dlim_o[...] = jnp.zeros_like(dlim_o)

        hf_re[...] = hre_r[...].astype(F32)
        hf_im[...] = him_r[...].astype(F32)
        first_chunk = ch == nch - 1
        edge_re = jnp.where(first_chunk, cin_re_r[...], pre_r[...].astype(F32)[SEGS:, :])
        edge_im = jnp.where(first_chunk, cin_im_r[...], pim_r[...].astype(F32)[SEGS:, :])

        def step(i, hc):
            t = tt - 1 - i
            off = pl.multiple_of(t * SEGS, SEGS)
            n_re = a_re * hc[0] - a_im * hc[1] + x_re[pl.ds(off, SEGS), :]
            n_im = a_re * hc[1] + a_im * hc[0] + x_im[pl.ds(off, SEGS), :]
            g_re[pl.ds(off, SEGS), :] = n_re
            g_im[pl.ds(off, SEGS), :] = n_im
            offp = pl.multiple_of(jnp.maximum(t - 1, 0) * SEGS, SEGS)
            hp_re = jnp.where(t == 0, edge_re, hf_re[pl.ds(offp, SEGS), :])
            hp_im = jnp.where(t == 0, edge_im, hf_im[pl.ds(offp, SEGS), :])
            return n_re, n_im, hc[2] + hp_re * n_re + hp_im * n_im, hc[3] + hp_re * n_im - hp_im * n_re

        fin = lax.fori_loop(0, tt, step, (st_re[...], st_im[...], dlre_o[...], dlim_o[...]))
        st_re[...] = fin[0]
        st_im[...] = fin[1]
        dlre_o[...] = fin[2]
        dlim_o[...] = fin[3]

        gr16, gi16 = g_re[...].astype(BF16), g_im[...].astype(BF16)
        du = jnp.dot(gr16, bre_r[...], preferred_element_type=F32) + jnp.dot(gi16, bim_r[...], preferred_element_type=F32)
        du_o[...] = du + d_r[...] * dy
        ub16 = ub.astype(BF16)
        parts = [
            (dbre_o, lax.dot_general(ub16, gr16, _DN["tn"], preferred_element_type=F32)),
            (dbim_o, lax.dot_general(ub16, gi16, _DN["tn"], preferred_element_type=F32)),
            (dcre_o, lax.dot_general(hre_r[...], dy16, _DN["tn"], preferred_element_type=F32)),
            (dcim_o, lax.dot_general(him_r[...], dy16, _DN["tn"], preferred_element_type=F32)),
            (dd_o, jnp.sum(dy * ub, axis=0, keepdims=True)),
        ]
        for ref, val in parts:
            @pl.when(ch == 0)
            def _(ref=ref, val=val):
                ref[...] = val

            @pl.when(ch > 0)
            def _(ref=ref, val=val):
                ref[...] += val

    def chunk(c):
        return nch - 1 - c

    u_spec = pl.BlockSpec((rows, U_BLK), lambda j, c: (chunk(c), j))
    h_spec = pl.BlockSpec((rows, SCAN_WC), lambda j, c: (chunk(c), j))
    prev_spec = pl.BlockSpec((half, SCAN_WC), lambda j, c: (jnp.maximum(chunk(c) * (rows // half) - 1, 0), j))
    e_spec = pl.BlockSpec((SEGS, SCAN_WC), lambda j, c: (0, j))
    d_spec = pl.BlockSpec((1, U_BLK), lambda j, c: (0, j))
    bt_spec = pl.BlockSpec((None, SCAN_WC, U_BLK), lambda j, c: (j, 0, 0))
    ct_spec = pl.BlockSpec((None, U_BLK, SCAN_WC), lambda j, c: (j, 0, 0))
    l_spec = pl.BlockSpec((1, SCAN_WC), lambda j, c: (0, j))
    return _run(
        body, [dyg_s, ys, u_s, h_re, h_im, h_re, h_im, hin_re, hin_im, gin_re, gin_im, dvec, w_bre_t, w_bim_t, w_cre_t,
               w_cim_neg_t, lre, lim],
        carry=carry, name=name, grid=(BD, nch),
        in_specs=[u_spec, u_spec, u_spec, h_spec, h_spec, prev_spec, prev_spec, e_spec, e_spec, e_spec, e_spec, d_spec,
                  bt_spec, bt_spec, ct_spec, ct_spec, l_spec, l_spec],
        out_specs=[u_spec, ct_spec, ct_spec, bt_spec, bt_spec, e_spec, e_spec, d_spec],
        out_shape=[jax.ShapeDtypeStruct((s, SSM_W), F32)] + [jax.ShapeDtypeStruct((BD, U_BLK, SCAN_WC), F32)] * 2
        + [jax.ShapeDtypeStruct((BD, SCAN_WC, U_BLK), F32)] * 2 + [jax.ShapeDtypeStruct((SEGS, STATE_W), F32)] * 2
        + [jax.ShapeDtypeStruct((1, SSM_W), F32)],
        scratch_shapes=[pltpu.VMEM((SEGS, SCAN_WC), F32)] * 2 + [pltpu.VMEM((rows, SCAN_WC), F32)] * 6,
        compiler_params=_cparams(("parallel", "arbitrary")),
    )


def ssm_bwd_ends(dyg_s, ys, w_cre_t, w_cim_neg_t, lre, lim, name, carry=None):
    s = ys.shape[0]
    steps, tt, nch, rows, nsq = _scan_dims(s)
    nb, ub_w, wc = w_cre_t.shape

    def body(dyg_r, ys_r, cre_r, cim_r, lre_r, lim_r, gin_re_o, gin_im_o, st_re, st_im, x_re, x_im):
        ch = pl.program_id(1)
        a_re = jnp.broadcast_to(lre_r[...], (SEGS, wc))
        a_im = -jnp.broadcast_to(lim_r[...], (SEGS, wc))
        y = ys_r[...]
        dy16 = (dyg_r[...] * _gelu_grad(y, _gelu(y)[1])).astype(BF16)
        x_re[...] = jnp.dot(dy16, cre_r[...], preferred_element_type=F32)
        x_im[...] = jnp.dot(dy16, cim_r[...], preferred_element_type=F32)

        @pl.when(ch == 0)
        def _():
            st_re[...] = jnp.zeros_like(st_re)
            st_im[...] = jnp.zeros_like(st_im)

        def step(i, hc):
            off = pl.multiple_of((tt - 1 - i) * SEGS, SEGS)
            return (a_re * hc[0] - a_im * hc[1] + x_re[pl.ds(off, SEGS), :],
                    a_re * hc[1] + a_im * hc[0] + x_im[pl.ds(off, SEGS), :])

        fin = lax.fori_loop(0, tt, step, (st_re[...], st_im[...]))
        st_re[...] = fin[0]
        st_im[...] = fin[1]

        @pl.when(ch == nch - 1)
        def _():
            c_re, c_im = _chain_segments(a_re, a_im, fin[0], fin[1], nsq, True)
            gin_re_o[...] = c_re
            gin_im_o[...] = c_im

    y_spec = pl.BlockSpec((rows, ub_w), lambda j, c: (nch - 1 - c, j))
    ct_spec = pl.BlockSpec((None, ub_w, wc), lambda j, c: (j, 0, 0))
    l_spec = pl.BlockSpec((1, wc), lambda j, c: (0, j))
    e_spec = pl.BlockSpec((SEGS, wc), lambda j, c: (0, j))
    return _run(
        body, [dyg_s, ys, w_cre_t, w_cim_neg_t, lre, lim], carry=carry, name=name, grid=(nb, nch),
        in_specs=[y_spec, y_spec, ct_spec, ct_spec, l_spec, l_spec], out_specs=[e_spec, e_spec],
        out_shape=[jax.ShapeDtypeStruct((SEGS, STATE_W), F32)] * 2,
        scratch_shapes=[pltpu.VMEM((SEGS, wc), F32)] * 2 + [pltpu.VMEM((rows, wc), F32)] * 2,
        compiler_params=_cparams(("parallel", "arbitrary")),
    )


FWD_BD = 4


def _block_diag(m, nb=BD):
    g, r, c = m.shape
    m = m.reshape(nb, g // nb, r, c)
    eye = jnp.eye(g // nb, dtype=m.dtype)
    return jnp.einsum("jarc,ab->jarbc", m, eye).reshape(nb, (g // nb) * r, (g // nb) * c)


def _block_diag_extract(m, r, c):
    per = m.shape[1] // r
    m = m.reshape(BD, per, r, per, c)
    return jnp.einsum("jarac->jarc", m).reshape(BD * per, r, c)


def to_segments(a):
    s, w = a.shape
    return a.reshape(SEGS, s // SEGS, w).transpose(1, 0, 2).reshape(s, w)


def from_segments(a):
    s, w = a.shape
    return a.reshape(s // SEGS, SEGS, w).transpose(1, 0, 2).reshape(s, w)


W_IN_CHUNK_ROWS = (304, 304, 176, 720, 544)
FFN_GATE_ROWS_FIRST = 480
TALL_TM = 2048
FFN_TN = 512


def local_step(x, target, shards, small):
    s = x.shape[0]
    g1, g2, g3, g4 = (small[k].reshape(1, D_MODEL) for k in ("norm_mix_pre", "norm_mix_post", "norm_ffn_pre", "norm_ffn_post"))
    dvec = small["ssm_d"].reshape(1, SSM_W)
    wts, recv = {}, {}

    def gathered(names, blocks):
        for n, b in zip(names, blocks):
            wts[n] = _full_from_gathered(b, n)

    def rms_in_fn(r, c):
        hh = _rms(r[0], c[0])[0].astype(BF16)
        return [hh, _permute(_perm_matrix(PERM_TS, 4, False), hh), _permute(_perm_matrix(PERM_TS, 16, False), hh)], []

    (h, h4, h16), got = rowwise("rms_in", rms_in_fn, [x], [g1], [(D_MODEL, BF16), (D_MODEL, BF16, 4), (D_MODEL, BF16, 16)],
                                ts=PERM_TS, carry=Gather([shards["w_in"]]))
    w_in_t = _full_from_gathered(got[0], "w_in")
    w_u_t, w_gates_t = w_in_t[3 * HQ:3 * HQ + SSM_W], w_in_t[3 * HQ + SSM_W:]

    def qkv_rows(g):
        return 3 * GROUP_W, lambda t: 3 * t + g

    hd = [h.reshape(1, s, D_MODEL), h4, h16]
    qkv = [None] * 3
    names = ("w_attn_up", "w_glu_v", "w_glu_g")
    qkv[0], got = mm([(hd[0].reshape(s, D_MODEL), w_in_t)], "nt", BF16, "mm_qkv0", tm=TALL_TM, tn=GROUP_W, b_window=qkv_rows(0),
                     carry=Gather([shards[n] for n in names]))
    gathered(names, got)
    qkv[1], got = mm([(hd[1].reshape(s, D_MODEL), w_in_t)], "nt", BF16, "mm_qkv1", tm=TALL_TM, tn=GROUP_W, b_window=qkv_rows(1),
                     carry=Gather([shards["w_out"]]))
    gathered(("w_out",), got)
    qkv[2] = mm([(hd[2].reshape(s, D_MODEL), w_in_t)], "nt", BF16, "mm_qkv2", tm=TALL_TM, tn=GROUP_W, b_window=qkv_rows(2))
    u = mm([(h, w_u_t)], "nt", F32, "mm_u")
    gates, got = mm([(h, w_gates_t)], "nt", BF16, "mm_gates", carry=Gather([shards["w_ffn_gate"]]))
    gathered(("w_ffn_gate",), got)

    outs, lses = [], []
    for g, (_, dil) in enumerate(ATTN_GROUPS):
        o, l = attn_fwd(qkv[g].reshape(dil, s // dil, 3 * GROUP_W), g, f"attn_fwd{g}")
        outs.append(o.reshape(s, GROUP_W) if dil == 1 else o)
        lses.append(l.reshape(s, GROUP_W) if dil == 1 else l)

    def natural(r):
        back4, back16 = _perm_matrix(PERM_TS, 4, True), _perm_matrix(PERM_TS, 16, True)
        return (r[0], _permute(back4, r[1].astype(BF16)), _permute(back16, r[2].astype(BF16)),
                r[3], _permute(back4, r[4]), _permute(back16, r[5]))

    def merge_fn(r, c):
        o0, o1, o2, l0, l1, l2 = natural(r)
        w0, w1, w2 = _mix_weights(l0, l1, l2)
        return [w0 * o0 + w1 * o1 + w2 * o2], []

    (attn,) = rowwise("attn_merge", merge_fn, outs + lses, [], [(GROUP_W, BF16)], ts=PERM_TS)
    attn_branch = mm([(attn, wts["w_attn_up"])], "nn", BF16, "mm_up", tm=TALL_TM)

    are3 = small["ssm_a_re"].reshape(SSM_GROUPS, SSM_STATE, 1)
    aim3 = small["ssm_a_im"].reshape(SSM_GROUPS, SSM_STATE, 1)
    ldt3 = small["ssm_log_dt"].reshape(SSM_GROUPS, 1, 1)
    bre3 = small["ssm_b_re"].reshape(SSM_GROUPS, SSM_STATE, SSM_GROUP)
    bim3 = small["ssm_b_im"].reshape(SSM_GROUPS, SSM_STATE, SSM_GROUP)
    cre3 = small["ssm_c_re"].reshape(SSM_GROUPS, SSM_GROUP, SSM_STATE)
    cim3 = small["ssm_c_im"].reshape(SSM_GROUPS, SSM_GROUP, SSM_STATE)
    lre3, lim3, bbre, bbim = ssm_prep(are3, aim3, ldt3, bre3, bim3)
    lre, lim = lre3.reshape(1, STATE_W), lim3.reshape(1, STATE_W)
    w_bre = _block_diag(bbre.transpose(0, 2, 1)).astype(BF16)
    w_bim = _block_diag(bbim.transpose(0, 2, 1)).astype(BF16)
    w_cre = _block_diag(cre3.transpose(0, 2, 1)).astype(BF16)
    w_cim = _block_diag(cim3.transpose(0, 2, 1)).astype(BF16)
    u_s = to_segments(u)
    fwd_w = [_block_diag(t.transpose(0, 2, 1), FWD_BD).astype(BF16) for t in (bbre, bbim, cre3, -cim3)]
    (yg_s, y_ssm, h_re, h_im, hin_re, hin_im), got = ssm_fwd(
        u_s, dvec, *fwd_w, lre, lim, "ssm_fwd", carry=Gather([shards["w_ffn_up"]], pass_early=True))
    gathered(("w_ffn_up",), got)
    yg = from_segments(yg_s)
    gv = mm([(yg, wts["w_glu_v"])], "nn", BF16, "mm_glu_v", tm=TALL_TM)
    gg = mm([(yg, wts["w_glu_g"])], "nn", BF16, "mm_glu_g", tm=TALL_TM)

    def gate_fn(r, c):
        gts, ab, gv_, gg_ = r
        sa, ss = _sigmoid(gts[:, :D_MODEL]), _sigmoid(gts[:, D_MODEL:])
        return [sa * ab + ss * (gv_ * _sigmoid(gg_))], []

    (merged,) = rowwise("gate_merge", gate_fn, [gates, attn_branch, gv, gg], [], [(D_MODEL, BF16)])
    o_mix = mm([(merged, wts["w_out"])], "nn", F32, "mm_out")

    def mid_fn(r, c):
        x1 = r[0] + _rms(r[1], c[0])[0]
        return [x1, _rms(x1, c[1])[0]], []

    x1, h2 = rowwise("rms_mid", mid_fn, [x, o_mix], [g2, g3], [(D_MODEL, F32), (D_MODEL, BF16)])
    (fa, fb, fin), got = mm([(h2, wts["w_ffn_gate"]), (h2, wts["w_ffn_up"])], "nt", [BF16, BF16, BF16], "mm_ffn_in", tn=FFN_TN,
                            epilogue=lambda p, e: [p[0], p[1], p[0] * _sigmoid(p[0]) * p[1]],
                            carry=Gather([shards["w_ffn_down"]], pass_early=True))
    gathered(("w_ffn_down",), got)
    f = mm([(fin, wts["w_ffn_down"])], "nn", F32, "mm_ffn_down", tn=512, tk=D_FF)

    def loss_fn(r, c):
        x1_, f_, tgt = r
        y, n, rr = _rms(f_, c[0])
        err = x1_ + y - tgt
        dout = err * (1.0 / D_MODEL)
        df, dg = _rms_bwd(dout, n, rr, c[0])
        lp = 0.5 * jnp.sum(jnp.sum(err * err, axis=-1, keepdims=True) * (1.0 / D_MODEL), axis=0, keepdims=True)
        return [df, dout], [dg, lp]

    df, dout, dg4, loss_part = rowwise("loss_bwd", loss_fn, [x1, f, target], [g4], [(D_MODEL, BF16), (D_MODEL, BF16)],
                                       acc_outs=[(1, D_MODEL), (1, 1)])
    def sent(names, blocks):
        for n, b in zip(names, blocks):
            recv[n] = b

    def to_owners(names, dws):
        return AllToAll([_split_for_devices(d, n) for n, d in zip(names, dws)])

    def swiglu_bwd(p, e):
        dfin_, (a, b) = p[0], e
        sg = _sigmoid(a)
        return [dfin_ * b * (sg * (1.0 + a * (1.0 - sg))), dfin_ * a * sg]

    da, db = mm([(df, wts["w_ffn_down"])], "nt", [BF16, BF16], "mm_d_fin", tn=FFN_TN, epilogue=swiglu_bwd, extras=[fa, fb])
    dw_ffn_down = mm([(fin, df)], "tn", BF16, "mm_dw_ffn_down")
    dh2, got = mm([(da, wts["w_ffn_gate"]), (db, wts["w_ffn_up"])], "nn", F32, "mm_d_h2", tm=512, tn=512, tk=D_FF,
                  carry=to_owners(["w_ffn_down"], [dw_ffn_down]))
    sent(["w_ffn_down"], got)
    dw_ffn_gate = mm([(da, h2)], "tn", BF16, "mm_dw_ffn_gate")
    gate_blocks = _split_for_devices(dw_ffn_gate, "w_ffn_gate")
    dw_ffn_up, (gate_landed,) = mm([(db, h2)], "tn", BF16, "mm_dw_ffn_up",
                                   carry=RowsToOwners(gate_blocks, 0, FFN_GATE_ROWS_FIRST))

    def mid_bwd(r, c):
        dh2_, dout_, x1_, o_ = r
        _, n3, r3 = _rms(x1_, c[1])
        dx1, dg3_ = _rms_bwd(dh2_, n3, r3, c[1])
        dx1 = dx1 + dout_
        _, n2, r2 = _rms(o_, c[0])
        do_, dg2_ = _rms_bwd(dx1, n2, r2, c[0])
        return [dx1, do_], [dg2_, dg3_]

    rest = gate_blocks.shape[1] - FFN_GATE_ROWS_FIRST
    (dx1, do_mix, dg2, dg3), (gate_landed,) = rowwise(
        "rms_mid_bwd", mid_bwd, [dh2, dout, x1, o_mix], [g2, g3], [(D_MODEL, F32), (D_MODEL, BF16)],
        acc_outs=[(1, D_MODEL), (1, D_MODEL)], carry=RowsToOwners(gate_blocks, FFN_GATE_ROWS_FIRST, rest, into=gate_landed))
    recv["w_ffn_gate"] = gate_landed
    dmerged = mm([(do_mix, wts["w_out"])], "nt", BF16, "mm_d_merged")
    dw_out = mm([(merged, do_mix)], "tn", BF16, "mm_dw_out")

    def gate_bwd(r, c):
        dm, gts, ab, gv_, gg_ = r
        sa, ss, sg = _sigmoid(gts[:, :D_MODEL]), _sigmoid(gts[:, D_MODEL:]), _sigmoid(gg_)
        branch = gv_ * sg
        dbranch = dm * ss
        dgates = jnp.concatenate([dm * ab * sa * (1.0 - sa), dm * branch * ss * (1.0 - ss)], axis=-1)
        return [dgates, dm * sa, dbranch * sg, dbranch * gv_ * sg * (1.0 - sg)], []

    dgates, dab, dgv, dgg = rowwise("gate_bwd", gate_bwd, [dmerged, gates, attn_branch, gv, gg], [],
                                    [(2 * D_MODEL, BF16), (D_MODEL, BF16), (D_MODEL, BF16), (D_MODEL, BF16)])
    dattn = mm([(dab, wts["w_attn_up"])], "nt", F32, "mm_d_attn")
    dw_up = mm([(attn, dab)], "tn", BF16, "mm_dw_up")
    dyg = mm([(dgv, wts["w_glu_v"]), (dgg, wts["w_glu_g"])], "nt", F32, "mm_d_yg")
    dw_glu_v = mm([(yg, dgv)], "tn", BF16, "mm_dw_glu_v")
    dw_glu_g = mm([(yg, dgg)], "tn", BF16, "mm_dw_glu_g")

    dyg_s = to_segments(dyg)
    (gin_re, gin_im), got = ssm_bwd_ends(dyg_s, y_ssm, fwd_w[2].transpose(0, 2, 1), fwd_w[3].transpose(0, 2, 1), lre, lim,
                                         "ssm_bwd_ends", carry=to_owners(["w_out"], [dw_out]))
    sent(["w_out"], got)
    (du_s, dbre_d, dbim_d, dcre_d, dcim_d, dl_re8, dl_im8, dd_ssm), got = ssm_bwd(
        dyg_s, y_ssm, u_s, h_re, h_im, hin_re, hin_im, gin_re, gin_im, dvec, w_bre.transpose(0, 2, 1), w_bim.transpose(0, 2, 1),
        w_cre.transpose(0, 2, 1), -w_cim.transpose(0, 2, 1), lre, lim, "ssm_bwd", carry=to_owners(["w_ffn_up"], [dw_ffn_up]))
    sent(["w_ffn_up"], got)
    dbb_re = _block_diag_extract(dbre_d, SSM_GROUP, SSM_STATE).transpose(0, 2, 1)
    dbb_im = _block_diag_extract(dbim_d, SSM_GROUP, SSM_STATE).transpose(0, 2, 1)
    dc_re = _block_diag_extract(dcre_d, SSM_STATE, SSM_GROUP).transpose(0, 2, 1)
    dc_im = -_block_diag_extract(dcim_d, SSM_STATE, SSM_GROUP).transpose(0, 2, 1)

    def fold8(r, c):
        return [], [jnp.sum(r[0], axis=0, keepdims=True), jnp.sum(r[1], axis=0, keepdims=True)]

    dl_re, dl_im = rowwise("ssm_dl_fold", fold8, [dl_re8, dl_im8], [], [], acc_outs=[(1, STATE_W), (1, STATE_W)], ts=SEGS)
    da_re, da_im, dldt, db_re, db_im = ssm_prep_bwd(
        are3, aim3, ldt3, bre3, bim3, dbb_re, dbb_im,
        dl_re.reshape(SSM_GROUPS, SSM_STATE, 1), dl_im.reshape(SSM_GROUPS, SSM_STATE, 1))
    du = from_segments(du_s)

    def merge_bwd(r, c):
        dat = r[0]
        o0, o1, o2, l0, l1, l2 = natural(r[1:])
        w0, w1, w2 = _mix_weights(l0, l1, l2)
        tot = _head_sum(dat * (w0 * o0 + w1 * o1 + w2 * o2))
        to4, to16 = _perm_matrix(PERM_TS, 4, False), _perm_matrix(PERM_TS, 16, False)
        return [w0 * dat, _permute(to4, (w1 * dat).astype(BF16)), _permute(to16, (w2 * dat).astype(BF16)),
                w0 * tot, _permute(to4, (w1 * tot).astype(BF16)), _permute(to16, (w2 * tot).astype(BF16))], []

    mb = rowwise("attn_merge_bwd", merge_bwd, [dattn] + outs + lses, [],
                 [(GROUP_W, BF16), (GROUP_W, BF16, 4), (GROUP_W, BF16, 16), (GROUP_W, BF16), (GROUP_W, BF16, 4), (GROUP_W, BF16, 16)],
                 ts=PERM_TS)
    dqs, dw_qkv = [], []
    names = ["w_glu_v", "w_glu_g", "w_attn_up"]
    for g, (_, dil) in enumerate(ATTN_GROUPS):
        dq = attn_bwd(qkv[g].reshape(dil, s // dil, 3 * GROUP_W), mb[g].reshape(dil, s // dil, GROUP_W),
                      lses[g].reshape(dil, s // dil, GROUP_W), mb[3 + g].reshape(dil, s // dil, GROUP_W),
                      g, f"attn_bwd{g}", carry=to_owners(names, [dw_glu_v, dw_glu_g, dw_up]) if g == 1 else None)
        if g == 1:
            dq, got = dq
            sent(names, got)
        dq = dq.reshape(s, 3 * GROUP_W)
        dqs.append(dq)
        dw_qkv.append(mm([(hd[g].reshape(s, D_MODEL), dq)], "tn", BF16, f"mm_dw_qkv{g}"))
    dw_u = mm([(h, du)], "tn", BF16, "mm_dw_u")
    dw_gates = mm([(h, dgates)], "tn", BF16, "mm_dw_gates")
    dw_in = jnp.concatenate(
        [dw_qkv[g][:, o * GROUP_W:(o + 1) * GROUP_W] for o in range(3) for g in range(3)] + [dw_u, dw_gates], axis=1)
    per_chip = _split_for_devices(dw_in, "w_in").reshape(N_DEV // 2, 2, D_MODEL, -1)
    my_core = lax.axis_index("c")
    for_my_core = lax.dynamic_index_in_dim(per_chip, my_core, axis=1, keepdims=False)
    for_sibling = lax.dynamic_index_in_dim(per_chip, 1 - my_core, axis=1, keepdims=False)
    shard_w = for_my_core.shape[-1]
    starts = [sum(W_IN_CHUNK_ROWS[:i]) for i in range(len(W_IN_CHUNK_ROWS))]
    landed = None

    def chunk(i):
        return ChipRowsToOwners(pair_sums, starts[i], W_IN_CHUNK_ROWS[i], into=landed)

    dh_parts = []
    for g, (_, dil) in enumerate(ATTN_GROUPS):
        if g == 0:
            dh_g, (from_sibling,) = mm([(dqs[g], w_in_t)], "nn", BF16, "mm_d_h_qkv0", tk=GROUP_W, b_window=qkv_rows(g),
                                       carry=SiblingSwap(for_sibling))
            (pair_sums,) = rowwise("dw_in_pair_sum", lambda r, c: ([r[0] + r[1]], []),
                                   [for_my_core.reshape(-1, shard_w), from_sibling.reshape(-1, shard_w)], [], [(shard_w, BF16)])
            pair_sums = pair_sums.reshape(for_my_core.shape)
        else:
            dh_g, (landed,) = mm([(dqs[g], w_in_t)], "nn", BF16, f"mm_d_h_qkv{g}", tk=GROUP_W, b_window=qkv_rows(g),
                                 carry=chunk(g - 1))
        dh_parts.append(dh_g if dil == 1 else dh_g.reshape(dil, s // dil, D_MODEL))
    dh_u, (landed,) = mm([(du, w_u_t)], "nn", BF16, "mm_d_h_u", carry=chunk(2))
    dh_gates, (landed,) = mm([(dgates, w_gates_t)], "nn", BF16, "mm_d_h_gates", carry=chunk(3))
    dh_parts += [dh_u, dh_gates]

    def in_bwd(r, c):
        dh1 = _permute(_perm_matrix(PERM_TS, 4, True), r[1].astype(BF16))
        dh2_ = _permute(_perm_matrix(PERM_TS, 16, True), r[2].astype(BF16))
        dh = r[0] + dh1 + dh2_ + r[3] + r[4]
        _, n1, r1 = _rms(r[6], c[0])
        dx, dg1_ = _rms_bwd(dh, n1, r1, c[0])
        return [dx + r[5]], [dg1_]

    (grad_x, dg1), (landed,) = rowwise("rms_in_bwd", in_bwd, dh_parts + [dx1, x], [g1], [(D_MODEL, F32)],
                                       acc_outs=[(1, D_MODEL)], ts=PERM_TS, carry=chunk(4))
    recv["w_in"] = landed

    dsmall = dict(norm_mix_pre=dg1, ssm_a_re=da_re, ssm_a_im=da_im, ssm_log_dt=dldt, ssm_b_re=db_re, ssm_b_im=db_im,
                  ssm_c_re=dc_re, ssm_c_im=dc_im, ssm_d=dd_ssm, norm_mix_post=dg2, norm_ffn_pre=dg3, norm_ffn_post=dg4)
    return loss_part, grad_x, recv, dsmall


def adamw(parts, w, m, v, name, carry=None):
    r, c = w.shape
    nparts = parts.shape[0]
    tr = r
    while tr > 8 and tr % 2 == 0 and tr * c * (nparts * parts.dtype.itemsize + 28) * 2 > 24 * 1024 * 1024:
        tr //= 2
    assert r % tr == 0 and (tr % 8 == 0 or tr == r)
    c1, c2 = 1.0 / (1.0 - ADAM_B1 ** ADAM_STEP), 1.0 / (1.0 - ADAM_B2 ** ADAM_STEP)

    def body(p_ref, w_ref, m_ref, v_ref, g_o, d_o, m_o, v_o):
        g = p_ref[0].astype(F32)
        for i in range(1, nparts):
            g = g + p_ref[i].astype(F32)
        mn = ADAM_B1 * m_ref[...] + (1.0 - ADAM_B1) * g
        vn = ADAM_B2 * v_ref[...] + (1.0 - ADAM_B2) * (g * g)
        g_o[...] = g
        m_o[...] = mn
        v_o[...] = vn
        d_o[...] = -ADAM_LR * ((mn * c1) / (jnp.sqrt(vn * c2) + ADAM_EPS) + ADAM_WD * w_ref[...])

    blk = pl.BlockSpec((tr, c), lambda i: (i, 0))
    return _run(
        body, [parts, w, m, v], carry=carry, name=name, grid=(r // tr,),
        in_specs=[pl.BlockSpec((nparts, tr, c), lambda i: (0, i, 0)), blk, blk, blk],
        out_specs=[blk] * 4, out_shape=[jax.ShapeDtypeStruct((r, c), F32)] * 4, compiler_params=_cparams(("parallel",)),
    )


PACK_C = 1024
SHARDED = ("w_in", "w_attn_up", "w_glu_v", "w_glu_g", "w_out", "w_ffn_gate", "w_ffn_up", "w_ffn_down")
ROW_SHARDED = ("w_out", "w_ffn_down")
SENT_TRANSPOSED = ("w_in", "w_ffn_gate", "w_ffn_up")
GRAD_TRANSPOSED = ("w_ffn_gate", "w_ffn_up")
SMALL = ("norm_mix_pre", "ssm_a_re", "ssm_a_im", "ssm_log_dt", "ssm_b_re", "ssm_b_im", "ssm_c_re", "ssm_c_im", "ssm_d",
         "norm_mix_post", "norm_ffn_pre", "norm_ffn_post")
WEIGHTS = ("norm_mix_pre", "w_in", "w_attn_up", "ssm_a_re", "ssm_a_im", "ssm_log_dt", "ssm_b_re", "ssm_b_im", "ssm_c_re",
           "ssm_c_im", "ssm_d", "w_glu_v", "w_glu_g", "w_out", "norm_mix_post", "norm_ffn_pre", "w_ffn_gate", "w_ffn_up",
           "w_ffn_down", "norm_ffn_post")


def _pack(arrs, dtype, pad_rows_to=64):
    flat = jnp.concatenate([a.reshape(-1).astype(dtype) for a in arrs])
    n = flat.shape[0]
    rows = -(-n // PACK_C)
    rows = -(-rows // pad_rows_to) * pad_rows_to
    return jnp.pad(flat, (0, rows * PACK_C - n)).reshape(rows, PACK_C)


def _unpack(flat2d, shapes):
    flat = flat2d.reshape(-1)
    out, off = [], 0
    for shp in shapes:
        n = int(np.prod(shp))
        out.append(flat[off:off + n].reshape(shp))
        off += n
    return out


def _full_from_gathered(gathered, name):
    if name in ROW_SHARDED or name in SENT_TRANSPOSED:
        return gathered.reshape(-1, gathered.shape[2])
    return gathered.transpose(1, 0, 2).reshape(gathered.shape[1], -1)


def _split_for_devices(full, name):
    if name in ROW_SHARDED or name in GRAD_TRANSPOSED:
        return full.reshape(N_DEV, -1, full.shape[1])
    return full.reshape(full.shape[0], N_DEV, -1).transpose(1, 0, 2)


def kernel(x, norm_mix_pre, w_in, w_attn_up, ssm_a_re, ssm_a_im, ssm_log_dt, ssm_b_re, ssm_b_im, ssm_c_re, ssm_c_im, ssm_d, w_glu_v, w_glu_g, w_out, norm_mix_post, norm_ffn_pre, w_ffn_gate, w_ffn_up, w_ffn_down, norm_ffn_post, loss_target, m_norm_mix_pre, m_w_in, m_w_attn_up, m_ssm_a_re, m_ssm_a_im, m_ssm_log_dt, m_ssm_b_re, m_ssm_b_im, m_ssm_c_re, m_ssm_c_im, m_ssm_d, m_w_glu_v, m_w_glu_g, m_w_out, m_norm_mix_post, m_norm_ffn_pre, m_w_ffn_gate, m_w_ffn_up, m_w_ffn_down, m_norm_ffn_post, v_norm_mix_pre, v_w_in, v_w_attn_up, v_ssm_a_re, v_ssm_a_im, v_ssm_log_dt, v_ssm_b_re, v_ssm_b_im, v_ssm_c_re, v_ssm_c_im, v_ssm_d, v_w_glu_v, v_w_glu_g, v_w_out, v_norm_mix_post, v_norm_ffn_pre, v_w_ffn_gate, v_w_ffn_up, v_w_ffn_down, v_norm_ffn_post):
    args = dict(locals())
    wv = {n: args[n][0] for n in WEIGHTS}
    mv = {n: args["m_" + n][0] for n in WEIGHTS}
    vv = {n: args["v_" + n][0] for n in WEIGHTS}

    shards = {n: (wv[n].T if n in SENT_TRANSPOSED else wv[n]).astype(BF16) for n in SHARDED}
    small = {n: wv[n] for n in SMALL}
    loss_part, grad_x, recv, dsmall = local_step(x[0], loss_target[0], shards, small)
    for n in GRAD_TRANSPOSED:
        recv[n] = recv[n].transpose(0, 2, 1)

    small_shapes = [wv[n].shape for n in SMALL]
    res = {}
    res["w_in"], (sgather,) = adamw(recv["w_in"], wv["w_in"], mv["w_in"], vv["w_in"], "adamw_w_in",
                                    carry=Gather([_pack([dsmall[n] for n in SMALL], F32)]))
    for n in SHARDED[1:]:
        res[n] = adamw(recv[n], wv[n], mv[n], vv[n], "adamw_" + n)
    sres = adamw(sgather, _pack([wv[n] for n in SMALL], F32), _pack([mv[n] for n in SMALL], F32),
                 _pack([vv[n] for n in SMALL], F32), "adamw_small")
    sun = [_unpack(t, small_shapes) for t in sres]
    for k, n in enumerate(SMALL):
        res[n] = tuple(sun[t][k] for t in range(4))

    loss = lax.psum(loss_part[0, 0], ("x", "y", "c"))
    outs = [loss, grad_x[None]]
    for t in range(4):
        outs += [res[n][t][None] for n in WEIGHTS]
    return tuple(outs)
```

```python
import math

import numpy as np
import jax
import jax.numpy as jnp
from jax import lax
from jax.experimental import pallas as pl
from jax.experimental.pallas import tpu as pltpu

F32 = jnp.float32
BF16 = jnp.bfloat16

D_MODEL = 2048
HEAD_DIM = 128
HEADS_PER_GROUP = 4
ATTN_GROUPS = ((128, 1), (512, 4), (2048, 16))
N_HEADS = HEADS_PER_GROUP * len(ATTN_GROUPS)
GROUP_W = HEADS_PER_GROUP * HEAD_DIM
HQ = N_HEADS * HEAD_DIM
SSM_W = 1024
SSM_GROUP = 16
SSM_GROUPS = 64
SSM_STATE = 64
STATE_W = SSM_GROUPS * SSM_STATE
D_FF = 5632
EPS = 1e-6
N_DEV = 8
SEGS = 8
BD = 8

ADAM_LR, ADAM_B1, ADAM_B2, ADAM_EPS, ADAM_WD, ADAM_STEP = 0.001, 0.9, 0.999, 1e-08, 0.01, 10

VMEM_LIMIT = 56 * 1024 * 1024
HBM_SPEC = pl.BlockSpec(memory_space=pltpu.HBM)
MESH_ID = pl.DeviceIdType.MESH
NEG = -1e30


def _pcall(body, **kw):
    return pl.pallas_call(body, **kw)


def _cparams(sem=None):
    if sem is None:
        return pltpu.CompilerParams(vmem_limit_bytes=VMEM_LIMIT)
    return pltpu.CompilerParams(vmem_limit_bytes=VMEM_LIMIT, dimension_semantics=sem)


def _my_coords():
    return lax.axis_index("x"), lax.axis_index("y"), lax.axis_index("c")


class Gather:
    def __init__(self, xs, pass_early=False):
        self.arrays = list(xs)
        self.out_shapes = [jax.ShapeDtypeStruct((N_DEV,) + x.shape, x.dtype) for x in xs]
        self.pass_early = pass_early

    def _ctx(self, out_refs, send_sems, recv_sems):
        mx, my, mc = _my_coords()
        me, sibling = (mx, my, mc), (mx, my, 1 - mc)
        chips = [(1 - mx, my), (mx, 1 - my), (1 - mx, 1 - my)]

        def slot(a, px, py, pc):
            return out_refs[a].at[4 * px + 2 * py + pc]

        def copy(a, k, block, to, src=None):
            return pltpu.make_async_remote_copy(
                src_ref=slot(a, *block) if src is None else src, dst_ref=slot(a, *block),
                send_sem=send_sems.at[7 * a + k], recv_sem=recv_sems.at[7 * a + k], device_id=to, device_id_type=MESH_ID)

        return me, sibling, chips, mc, slot, copy

    def _first(self, a, x_refs, ctx):
        me, sibling, chips, mc, slot, copy = ctx
        return [copy(a, 0, me, sibling, src=x_refs[a])] + [copy(a, 1 + j, me, (*chip, mc), src=x_refs[a]) for j, chip in enumerate(chips)]

    def start(self, x_refs, out_refs, send_sems, recv_sems, local_sems):
        ctx = self._ctx(out_refs, send_sems, recv_sems)
        me, slot = ctx[0], ctx[4]
        for a in range(len(self.arrays)):
            pltpu.make_async_copy(x_refs[a], slot(a, *me), local_sems.at[a]).start()
            for cp in self._first(a, x_refs, ctx):
                cp.start()

    def _passed(self, ctx):
        me, sibling, chips, mc, slot, copy = ctx
        return [copy(a, 4 + j, (*chip, mc), sibling) for a in range(len(self.arrays)) for j, chip in enumerate(chips)]

    def middle(self, x_refs, out_refs, send_sems, recv_sems, local_sems):
        ctx = self._ctx(out_refs, send_sems, recv_sems)
        me, sibling, chips, mc, slot, copy = ctx
        for a in range(len(self.arrays)):
            for j, chip in enumerate(chips):
                copy(a, 1 + j, (*chip, mc), me).wait_recv()
                copy(a, 4 + j, (*chip, mc), sibling).start()

    def finish(self, x_refs, out_refs, send_sems, recv_sems, local_sems, passed_on=False):
        if not passed_on:
            self.middle(x_refs, out_refs, send_sems, recv_sems, local_sems)
        ctx = self._ctx(out_refs, send_sems, recv_sems)
        me, sibling, chips, mc, slot, copy = ctx
        na = len(self.arrays)
        passed = self._passed(ctx)
        for a in range(na):
            copy(a, 0, sibling, me).wait_recv()
            for j, chip in enumerate(chips):
                copy(a, 4 + j, (*chip, 1 - mc), me).wait_recv()
        for a in range(na):
            for cp in self._first(a, x_refs, ctx):
                cp.wait_send()
        for cp in passed:
            cp.wait_send()
        for a in range(na):
            pltpu.make_async_copy(x_refs[a], slot(a, *me), local_sems.at[a]).wait()


class AllToAll:
    def __init__(self, ps):
        self.arrays = list(ps)
        self.out_shapes = [jax.ShapeDtypeStruct(p.shape, p.dtype) for p in ps]

    def _copies(self, p_refs, out_refs, send_sems, recv_sems, local_sems):
        mx, my, mc = _my_coords()
        me = 4 * mx + 2 * my + mc
        local, remote = [], []
        for a in range(len(self.arrays)):
            local.append(pltpu.make_async_copy(p_refs[a].at[me], out_refs[a].at[me], local_sems.at[a]))
            for k in range(1, N_DEV):
                px, py, pc = mx ^ ((k >> 2) & 1), my ^ ((k >> 1) & 1), mc ^ (k & 1)
                remote.append(pltpu.make_async_remote_copy(
                    src_ref=p_refs[a].at[4 * px + 2 * py + pc], dst_ref=out_refs[a].at[me],
                    send_sem=send_sems.at[7 * a + k - 1], recv_sem=recv_sems.at[7 * a + k - 1],
                    device_id=(px, py, pc), device_id_type=MESH_ID))
        return local, remote

    def start(self, *refs):
        local, remote = self._copies(*refs)
        for cp in local + remote:
            cp.start()

    def finish(self, *refs):
        local, remote = self._copies(*refs)
        for cp in remote:
            cp.wait_recv()
        for cp in remote:
            cp.wait_send()
        for cp in local:
            cp.wait()


class RowsToOwners:
    def __init__(self, p, r0, n, into=None):
        self.arrays = [p] if into is None else [p, into]
        self.out_shapes = [jax.ShapeDtypeStruct(p.shape, p.dtype)]
        self.aliases = {} if into is None else {1: 0}
        self.rows = (r0, n)

    def _copies(self, p_refs, out_refs, send_sems, recv_sems, local_sems):
        mx, my, mc = _my_coords()
        me = 4 * mx + 2 * my + mc
        rows = pl.ds(*self.rows)
        local = [pltpu.make_async_copy(p_refs[0].at[me, rows], out_refs[0].at[me, rows], local_sems.at[0])]
        remote = []
        for k in range(1, N_DEV):
            px, py, pc = mx ^ ((k >> 2) & 1), my ^ ((k >> 1) & 1), mc ^ (k & 1)
            remote.append(pltpu.make_async_remote_copy(
                src_ref=p_refs[0].at[4 * px + 2 * py + pc, rows], dst_ref=out_refs[0].at[me, rows],
                send_sem=send_sems.at[k - 1], recv_sem=recv_sems.at[k - 1], device_id=(px, py, pc), device_id_type=MESH_ID))
        return local, remote

    start = AllToAll.start
    finish = AllToAll.finish


def _run(body, args, carry=None, **kw):
    if carry is None:
        return _pcall(body, **kw)(*args)
    grid = kw["grid"]
    single = not isinstance(kw["out_shape"], (list, tuple))
    in_specs = list(kw["in_specs"])
    out_specs = [kw["out_specs"]] if single else list(kw["out_specs"])
    out_shape = [kw["out_shape"]] if single else list(kw["out_shape"])
    scratch = list(kw.get("scratch_shapes", []))
    na, nin, nout, nscr = len(carry.arrays), len(in_specs), len(out_specs), len(scratch)
    nco = len(carry.out_shapes)
    aliases = {nin + i: nout + o for i, o in getattr(carry, "aliases", {}).items()}
    steps = int(np.prod(grid))
    mid_step = (steps * 7) // 10 if getattr(carry, "pass_early", False) and steps >= 4 else None

    def carried(*refs):
        ins, cin = refs[:nin], refs[nin:nin + na]
        outs, cout = refs[nin + na:nin + na + nout], refs[nin + na + nout:nin + na + nout + nco]
        scr = refs[nin + na + nout + nco:nin + na + nout + nco + nscr]
        sems = refs[nin + na + nout + nco + nscr:]
        step = pl.program_id(0)
        for i in range(1, len(grid)):
            step = step * grid[i] + pl.program_id(i)

        @pl.when(step == 0)
        def _():
            carry.start(cin, cout, *sems)

        if mid_step is not None:
            @pl.when(step == mid_step)
            def _():
                carry.middle(cin, cout, *sems)

        body(*ins, *outs, *scr)

        @pl.when(step == steps - 1)
        def _():
            if mid_step is not None:
                carry.finish(cin, cout, *sems, passed_on=True)
            else:
                carry.finish(cin, cout, *sems)

    res = _pcall(
        carried, name=kw["name"], grid=grid, in_specs=in_specs + [HBM_SPEC] * na, out_specs=out_specs + [HBM_SPEC] * nco,
        out_shape=out_shape + carry.out_shapes, input_output_aliases=aliases,
        scratch_shapes=scratch + [pltpu.SemaphoreType.DMA((7 * na,)), pltpu.SemaphoreType.DMA((7 * na,)), pltpu.SemaphoreType.DMA((na,))],
        compiler_params=_cparams(("arbitrary",) * len(grid)),
    )(*args, *carry.arrays)
    main = res[:nout]
    return (main[0] if single else main), list(res[nout:])


_DN = {"nn": (((1,), (0,)), ((), ())), "nt": (((1,), (1,)), ((), ())), "tn": (((0,), (0,)), ((), ()))}


LANE = 128
MM_TM, MM_TN, MM_TK = 1024, 1536, 2048


def _tile(n, cap):
    for t in range(min(cap, n) // LANE * LANE, 0, -LANE):
        if n % t == 0:
            return t
    raise ValueError(n)


DW_TM, DW_TN, DW_TK = 512, 512, 8192
EPILOGUE_SPLIT = 2


def mm(pairs, mode, out_dtype, name, tm=None, tn=None, tk=None, carry=None, epilogue=None, extras=(), b_window=None):
    a0, b0 = pairs[0]
    if mode == "nn":
        (m, k), n = a0.shape, b0.shape[1]
    elif mode == "nt":
        (m, k), n = a0.shape, b0.shape[0]
    else:
        (k, m), n = a0.shape, b0.shape[1]
    if b_window is not None:
        assert mode in ("nn", "nt") and len(pairs) == 1
        if mode == "nt":
            n = b_window[0]
        else:
            assert k == b_window[0]
    caps = (DW_TM, DW_TN, DW_TK) if mode == "tn" else (MM_TM, MM_TN, MM_TK)
    tm, tn, tk = _tile(m, tm or caps[0]), _tile(n, tn or caps[1]), _tile(k, tk or caps[2])
    nk = k // tk
    npairs = len(pairs)
    nex = len(extras)
    fused = epilogue is not None
    assert not fused or nk == 1
    out_dtypes = list(out_dtype) if fused else [out_dtype]

    def body(*refs):
        if fused:
            half = tn // EPILOGUE_SPLIT
            for c in range(EPILOGUE_SPLIT):
                cols = slice(c * half, (c + 1) * half)
                prods = []
                for p in range(npairs):
                    a = refs[2 * p][...].astype(BF16)
                    b = (refs[2 * p + 1][cols, :] if mode == "nt" else refs[2 * p + 1][:, cols]).astype(BF16)
                    prods.append(lax.dot_general(a, b, _DN[mode], preferred_element_type=F32))
                ex = [refs[2 * npairs + e][:, cols].astype(F32) for e in range(nex)]
                for o_ref, val in zip(refs[2 * npairs + nex:], epilogue(prods, ex)):
                    o_ref[:, cols] = val.astype(o_ref.dtype)
            return
        prods = []
        for p in range(npairs):
            a = refs[2 * p][...].astype(BF16) if (p == 0 or pairs[p][0] is not pairs[p - 1][0]) else a
            b = refs[2 * p + 1][...].astype(BF16)
            prods.append(lax.dot_general(a, b, _DN[mode], preferred_element_type=F32))
        o_ref = refs[2 * npairs]
        tot = prods[0]
        for d in prods[1:]:
            tot = tot + d
        if nk == 1:
            o_ref[...] = tot.astype(o_ref.dtype)
            return
        acc = refs[2 * npairs + 1]
        kk = pl.program_id(2)

        @pl.when(kk == 0)
        def _():
            acc[...] = tot

        @pl.when(kk > 0)
        def _():
            acc[...] += tot

        @pl.when(kk == nk - 1)
        def _():
            o_ref[...] = acc[...].astype(o_ref.dtype)

    rows_of = b_window[1] if b_window is not None else (lambda t: t)
    if mode == "nn":
        sp = [pl.BlockSpec((tm, tk), lambda i, j, kk: (i, kk)), pl.BlockSpec((tk, tn), lambda i, j, kk: (rows_of(kk), j))]
    elif mode == "nt":
        sp = [pl.BlockSpec((tm, tk), lambda i, j, kk: (i, kk)), pl.BlockSpec((tn, tk), lambda i, j, kk: (rows_of(j), kk))]
    else:
        sp = [pl.BlockSpec((tk, tm), lambda i, j, kk: (kk, i)), pl.BlockSpec((tk, tn), lambda i, j, kk: (kk, j))]
    o_spec = pl.BlockSpec((tm, tn), lambda i, j, kk: (i, j))
    out_shapes = [jax.ShapeDtypeStruct((m, n), dt) for dt in out_dtypes]
    return _run(
        body, [t for pr in pairs for t in pr] + list(extras), carry=carry, name=name, grid=(m // tm, n // tn, nk),
        in_specs=sp * npairs + [o_spec] * nex,
        out_specs=[o_spec] * len(out_shapes) if fused else o_spec,
        out_shape=out_shapes if fused else out_shapes[0],
        scratch_shapes=[pltpu.VMEM((tm, tn), F32)] if nk > 1 else [],
        compiler_params=_cparams(("parallel", "parallel", "arbitrary")),
    )


def rowwise(name, fn, row_ins, const_ins, row_outs, acc_outs=(), ts=None, carry=None):
    s = row_ins[0].shape[0]
    row_outs = [ro if len(ro) == 3 else (*ro, 1) for ro in row_outs]
    if ts is None:
        per_row = sum(a.shape[-1] * a.dtype.itemsize for a in row_ins) + sum(w * jnp.dtype(dt).itemsize for w, dt, _ in row_outs)
        ts = 512
        while ts > 8 and 2 * ts * per_row > 20 * 1024 * 1024:
            ts //= 2
    ts = min(ts, s)
    assert s % ts == 0
    nr, nc, no, na = len(row_ins), len(const_ins), len(row_outs), len(acc_outs)

    def body(*refs):
        rows = [r[...].reshape(ts, r.shape[-1]).astype(F32) for r in refs[:nr]]
        consts = [r[...] for r in refs[nr:nr + nc]]
        outs, accs = fn(rows, consts)
        for r, v in zip(refs[nr + nc:nr + nc + no], outs):
            r[...] = v.astype(r.dtype).reshape(r.shape)
        if na:
            first = pl.program_id(0) == 0
            for r, v in zip(refs[nr + nc + no:], accs):
                @pl.when(first)
                def _(r=r, v=v):
                    r[...] = v

                @pl.when(jnp.logical_not(first))
                def _(r=r, v=v):
                    r[...] += v

    def tile_spec(w, d):
        if d == 1:
            return pl.BlockSpec((ts, w), lambda i: (i, 0))
        return pl.BlockSpec((d, ts // d, w), lambda i: (0, i, 0))

    in_specs = [tile_spec(a.shape[-1], a.shape[0] if a.ndim == 3 else 1) for a in row_ins]
    in_specs += [pl.BlockSpec(c.shape, lambda i, nd=c.ndim: (0,) * nd) for c in const_ins]
    out_specs = [tile_spec(w, d) for w, _, d in row_outs]
    out_specs += [pl.BlockSpec(shp, lambda i, nd=len(shp): (0,) * nd) for shp in acc_outs]
    out_shape = [jax.ShapeDtypeStruct((s, w) if d == 1 else (d, s // d, w), dt) for w, dt, d in row_outs]
    out_shape += [jax.ShapeDtypeStruct(shp, F32) for shp in acc_outs]
    return _run(
        body, [*row_ins, *const_ins], carry=carry, name=name, grid=(s // ts,), in_specs=in_specs, out_specs=out_specs,
        out_shape=out_shape, compiler_params=_cparams(("arbitrary",)),
    )


PERM_TS = 256


def _perm_matrix(ts, d, inverse):
    i = lax.broadcasted_iota(jnp.int32, (ts, ts), 0)
    k = lax.broadcasted_iota(jnp.int32, (ts, ts), 1)
    per = ts // d
    src = (i % d) * per + i // d if inverse else (i % per) * d + i // per
    return jnp.where(k == src, 1.0, 0.0).astype(BF16)


def _permute(p, x):
    if x.dtype == BF16:
        return jnp.dot(p, x, preferred_element_type=F32)
    hi = x.astype(BF16)
    rest = x - hi.astype(F32)
    mid = rest.astype(BF16)
    lo = (rest - mid.astype(F32)).astype(BF16)
    out = jnp.dot(p, hi, preferred_element_type=F32) + jnp.dot(p, mid, preferred_element_type=F32)
    return out + jnp.dot(p, lo, preferred_element_type=F32)


def _rms(x, gain):
    r = lax.rsqrt(jnp.mean(x * x, axis=-1, keepdims=True) + EPS)
    n = x * r
    return n * gain, n, r


def _rms_bwd(dy, n, r, gain):
    dn = dy * gain
    dx = r * (dn - n * jnp.mean(dn * n, axis=-1, keepdims=True))
    return dx, jnp.sum(dy * n, axis=0, keepdims=True)


def _sigmoid(x):
    return 1.0 / (1.0 + jnp.exp(-x))


_GELU_K = math.sqrt(2.0 / math.pi)


def _gelu(x):
    t = jnp.tanh(_GELU_K * (x + 0.044715 * x * x * x))
    return 0.5 * x * (1.0 + t), t


def _gelu_grad(x, t):
    return 0.5 * (1.0 + t) + 0.5 * x * (1.0 - t * t) * _GELU_K * (1.0 + 3.0 * 0.044715 * x * x)


def _head_sum(x):
    parts = []
    for h in range(HEADS_PER_GROUP):
        sl = x[:, h * HEAD_DIM:(h + 1) * HEAD_DIM]
        parts.append(jnp.broadcast_to(jnp.sum(sl, axis=-1, keepdims=True), sl.shape))
    return jnp.concatenate(parts, axis=-1)


def _mix_weights(l0, l1, l2):
    mx = jnp.maximum(jnp.maximum(l0, l1), l2)
    e0, e1, e2 = jnp.exp(l0 - mx), jnp.exp(l1 - mx), jnp.exp(l2 - mx)
    inv = 1.0 / (e0 + e1 + e2)
    return e0 * inv, e1 * inv, e2 * inv


BLK = 128


def _slopes(g):
    return [2.0 ** (-8.0 * (g * HEADS_PER_GROUP + h + 1) / N_HEADS) for h in range(HEADS_PER_GROUP)]


def _attn_masks(dil):
    qi = lax.broadcasted_iota(jnp.int32, (BLK, BLK), 0)
    ki = lax.broadcasted_iota(jnp.int32, (BLK, BLK), 1)
    dist_c = qi - ki
    dist_p = BLK + qi - ki
    return dist_c >= 0, dist_p <= BLK, (dist_c * dil).astype(F32), (dist_p * dil).astype(F32)


def _window_mask(has_prev, dil):
    qi = lax.broadcasted_iota(jnp.int32, (BLK, 2 * BLK), 0)
    ki = lax.broadcasted_iota(jnp.int32, (BLK, 2 * BLK), 1)
    dist = BLK + qi - ki
    ok = jnp.logical_and(jnp.logical_and(dist >= 0, dist <= BLK), jnp.logical_or(ki >= BLK, has_prev))
    return ok, (dist * dil).astype(F32)


def attn_fwd(qkv, g, name):
    dil, length, _ = qkv.shape
    scale = HEAD_DIM ** -0.5
    slopes = _slopes(g)

    def body(q_ref, kc_ref, vc_ref, kp_ref, vp_ref, o_ref, l_ref):
        ok, dist = _window_mask(pl.program_id(1) > 0, dil)
        for h in range(HEADS_PER_GROUP):
            sl = slice(h * HEAD_DIM, (h + 1) * HEAD_DIM)
            k2 = jnp.concatenate([kp_ref[:, sl], kc_ref[:, sl]], axis=0)
            v2 = jnp.concatenate([vp_ref[:, sl], vc_ref[:, sl]], axis=0)
            s = lax.dot_general(q_ref[:, sl], k2, _DN["nt"], preferred_element_type=F32) * scale - slopes[h] * dist
            s = jnp.where(ok, s, NEG)
            mx = jnp.max(s, axis=-1, keepdims=True)
            p = jnp.exp(s - mx)
            den = jnp.sum(p, axis=-1, keepdims=True)
            o_ref[:, sl] = (jnp.dot(p.astype(BF16), v2, preferred_element_type=F32) / den).astype(BF16)
            l_ref[:, sl] = jnp.broadcast_to(mx + jnp.log(den), (BLK, HEAD_DIM))

    def spec(col, prev):
        if prev:
            return pl.BlockSpec((None, BLK, GROUP_W), lambda r, n: (r, jnp.maximum(n - 1, 0), col))
        return pl.BlockSpec((None, BLK, GROUP_W), lambda r, n: (r, n, col))

    out_spec = pl.BlockSpec((None, BLK, GROUP_W), lambda r, n: (r, n, 0))
    return _pcall(
        body, name=name, grid=(dil, length // BLK),
        in_specs=[spec(0, False), spec(1, False), spec(2, False), spec(1, True), spec(2, True)],
        out_specs=[out_spec, out_spec],
        out_shape=[jax.ShapeDtypeStruct((dil, length, GROUP_W), BF16), jax.ShapeDtypeStruct((dil, length, GROUP_W), F32)],
        compiler_params=_cparams(("parallel", "parallel")),
    )(qkv, qkv, qkv, qkv, qkv)


def attn_bwd(qkv, dout, lse, dd, g, name, carry=None):
    dil, length, _ = qkv.shape
    nblk = length // BLK
    scale = HEAD_DIM ** -0.5
    slopes = _slopes(g)

    def body(q_ref, kc_ref, vc_ref, kp_ref, vp_ref, qn_ref, do_ref, don_ref, l_ref, ln_ref, d_ref, dn_ref, o_ref):
        n = pl.program_id(1)
        ok2, dist2 = _window_mask(n > 0, dil)
        _, ok_p, _, dp = _attn_masks(dil)
        ok_next = jnp.logical_and(ok_p, n < nblk - 1)
        for h in range(HEADS_PER_GROUP):
            sl = slice(h * HEAD_DIM, (h + 1) * HEAD_DIM)
            q, kc, vc, qn = q_ref[:, sl], kc_ref[:, sl], vc_ref[:, sl], qn_ref[:, sl]
            k2 = jnp.concatenate([kp_ref[:, sl], kc], axis=0)
            v2 = jnp.concatenate([vp_ref[:, sl], vc], axis=0)
            do, don = do_ref[:, sl], don_ref[:, sl]
            lse_q, lse_n, dd_q, dd_n = l_ref[:, sl], ln_ref[:, sl], d_ref[:, sl], dn_ref[:, sl]

            def probs(qq, kk, dist, ok, lse_t):
                s = lax.dot_general(qq, kk, _DN["nt"], preferred_element_type=F32) * scale - slopes[h] * dist
                return jnp.where(ok, jnp.exp(jnp.where(ok, s, NEG) - lse_t), 0.0)

            p2 = probs(q, k2, dist2, ok2, jnp.concatenate([lse_q, lse_q], axis=1))
            p_x = probs(qn, kc, dp, ok_next, lse_n)
            ds2 = p2 * (lax.dot_general(do, v2, _DN["nt"], preferred_element_type=F32) - jnp.concatenate([dd_q, dd_q], axis=1))
            ds_x = p_x * (lax.dot_general(don, vc, _DN["nt"], preferred_element_type=F32) - dd_n)
            dq = jnp.dot(ds2.astype(BF16), k2, preferred_element_type=F32)
            ds_k = jnp.concatenate([ds2[:, BLK:], ds_x], axis=0).astype(BF16)
            p_k = jnp.concatenate([p2[:, BLK:], p_x], axis=0).astype(BF16)
            dk = lax.dot_general(ds_k, jnp.concatenate([q, qn], axis=0), _DN["tn"], preferred_element_type=F32)
            dv = lax.dot_general(p_k, jnp.concatenate([do, don], axis=0), _DN["tn"], preferred_element_type=F32)
            o_ref[:, h * HEAD_DIM:(h + 1) * HEAD_DIM] = (dq * scale).astype(BF16)
            o_ref[:, GROUP_W + h * HEAD_DIM:GROUP_W + (h + 1) * HEAD_DIM] = (dk * scale).astype(BF16)
            o_ref[:, 2 * GROUP_W + h * HEAD_DIM:2 * GROUP_W + (h + 1) * HEAD_DIM] = dv.astype(BF16)

    def spec(col, which):
        if which == "prev":
            return pl.BlockSpec((None, BLK, GROUP_W), lambda r, n: (r, jnp.maximum(n - 1, 0), col))
        if which == "next":
            return pl.BlockSpec((None, BLK, GROUP_W), lambda r, n: (r, jnp.minimum(n + 1, nblk - 1), col))
        return pl.BlockSpec((None, BLK, GROUP_W), lambda r, n: (r, n, col))

    return _run(
        body, [qkv, qkv, qkv, qkv, qkv, qkv, dout, dout, lse, lse, dd, dd], carry=carry, name=name, grid=(dil, nblk),
        in_specs=[spec(0, "cur"), spec(1, "cur"), spec(2, "cur"), spec(1, "prev"), spec(2, "prev"), spec(0, "next"),
                  spec(0, "cur"), spec(0, "next"), spec(0, "cur"), spec(0, "next"), spec(0, "cur"), spec(0, "next")],
        out_specs=pl.BlockSpec((None, BLK, 3 * GROUP_W), lambda r, n: (r, n, 0)),
        out_shape=jax.ShapeDtypeStruct((dil, length, 3 * GROUP_W), BF16),
        compiler_params=_cparams(("parallel", "parallel")),
    )


def _ssm_prep_values(are, aim, logdt):
    dt = jnp.exp(logdt)
    mag = jnp.exp(are * dt)
    lb_re, lb_im = mag * jnp.cos(aim * dt), mag * jnp.sin(aim * dt)
    inv = 1.0 / (are * are + aim * aim)
    n_re, n_im = lb_re - 1.0, lb_im
    f_re = (n_re * are + n_im * aim) * inv
    f_im = (n_im * are - n_re * aim) * inv
    return dt, lb_re, lb_im, f_re, f_im, inv


PREP_G = 8


def _group_specs(are, logdt, bre):
    def spec(a):
        return pl.BlockSpec((PREP_G,) + a.shape[1:], lambda i: (i, 0, 0))
    return spec(are), spec(logdt), spec(bre)


def ssm_prep(are, aim, logdt, bre, bim):
    def body(are_r, aim_r, ldt_r, bre_r, bim_r, lre_o, lim_o, bbre_o, bbim_o):
        _, lb_re, lb_im, f_re, f_im, _ = _ssm_prep_values(are_r[...], aim_r[...], ldt_r[...])
        lre_o[...] = lb_re
        lim_o[...] = lb_im
        bbre_o[...] = f_re * bre_r[...] - f_im * bim_r[...]
        bbim_o[...] = f_re * bim_r[...] + f_im * bre_r[...]

    sh1 = jax.ShapeDtypeStruct(are.shape, F32)
    shb = jax.ShapeDtypeStruct(bre.shape, F32)
    s1, sd, sb = _group_specs(are, logdt, bre)
    return _pcall(body, name="ssm_prep", grid=(SSM_GROUPS // PREP_G,), in_specs=[s1, s1, sd, sb, sb], out_specs=[s1, s1, sb, sb],
                  out_shape=[sh1, sh1, shb, shb], compiler_params=_cparams(("parallel",)))(are, aim, logdt, bre, bim)


def ssm_prep_bwd(are, aim, logdt, bre, bim, dbbre, dbbim, dlre, dlim):
    def body(are_r, aim_r, ldt_r, bre_r, bim_r, dbbre_r, dbbim_r, dlre_r, dlim_r, dare_o, daim_o, dldt_o, dbre_o, dbim_o):
        are_v, aim_v = are_r[...], aim_r[...]
        dt, lb_re, lb_im, f_re, f_im, inv = _ssm_prep_values(are_v, aim_v, ldt_r[...])
        b_re, b_im, g_re, g_im = bre_r[...], bim_r[...], dbbre_r[...], dbbim_r[...]
        dbre_o[...] = f_re * g_re + f_im * g_im
        dbim_o[...] = f_re * g_im - f_im * g_re
        df_re = jnp.sum(b_re * g_re + b_im * g_im, axis=-1, keepdims=True)
        df_im = jnp.sum(b_re * g_im - b_im * g_re, axis=-1, keepdims=True)
        il_re, il_im = are_v * inv, -aim_v * inv
        cl_re = dlre_r[...] + il_re * df_re + il_im * df_im
        cl_im = dlim_r[...] + il_re * df_im - il_im * df_re
        q_re = -(f_re * il_re - f_im * il_im)
        q_im = -(f_re * il_im + f_im * il_re)
        ca_re = q_re * df_re + q_im * df_im
        ca_im = q_re * df_im - q_im * df_re
        cz_re = lb_re * cl_re + lb_im * cl_im
        cz_im = lb_re * cl_im - lb_im * cl_re
        dare_o[...] = ca_re + dt * cz_re
        daim_o[...] = ca_im + dt * cz_im
        dldt_o[...] = dt * jnp.sum(are_v * cz_re + aim_v * cz_im, axis=1, keepdims=True)

    sh1 = jax.ShapeDtypeStruct(are.shape, F32)
    shb = jax.ShapeDtypeStruct(bre.shape, F32)
    s1, sd, sb = _group_specs(are, logdt, bre)
    return _pcall(
        body, name="ssm_prep_bwd", grid=(SSM_GROUPS // PREP_G,), in_specs=[s1, s1, sd, sb, sb, sb, sb, s1, s1],
        out_specs=[s1, s1, sd, sb, sb], out_shape=[sh1, sh1, jax.ShapeDtypeStruct(logdt.shape, F32), shb, shb],
        compiler_params=_cparams(("parallel",)),
    )(are, aim, logdt, bre, bim, dbbre, dbbim, dlre, dlim)


SCAN_WC = 512


def _chain_segments(a_re, a_im, e_re, e_im, nsq, reverse):
    p_re, p_im = a_re, a_im
    for _ in range(nsq):
        p_re, p_im = p_re * p_re - p_im * p_im, 2.0 * p_re * p_im
    row = lax.broadcasted_iota(jnp.int32, e_re.shape, 0)
    edge = (row == SEGS - 1) if reverse else (row == 0)
    shift = SEGS - 1 if reverse else 1
    c_re, c_im = jnp.zeros_like(e_re), jnp.zeros_like(e_im)
    for _ in range(SEGS - 1):
        n_re = p_re * c_re - p_im * c_im + e_re
        n_im = p_re * c_im + p_im * c_re + e_im
        c_re = jnp.where(edge, 0.0, pltpu.roll(n_re, shift, 0))
        c_im = jnp.where(edge, 0.0, pltpu.roll(n_im, shift, 0))
    return c_re, c_im


def _scan_dims(s):
    steps = s // SEGS
    assert steps & (steps - 1) == 0
    tt = min(128, steps)
    return steps, tt, steps // tt, tt * SEGS, int(math.log2(steps))


U_BLK = SSM_W // BD


def ssm_fwd(u_s, dvec, w_bre, w_bim, w_cre, w_cim_neg, lre, lim, name, carry=None):
    s = u_s.shape[0]
    steps, tt, nch, rows, nsq = _scan_dims(s)
    nb, ub_w, wc = w_bre.shape

    def body(u_r, d_r, bre_r, bim_r, cre_r, cim_r, lre_r, lim_r, yg_o, ys_o, hre_o, him_o, hin_re_o, hin_im_o,
             st_re, st_im, x_re, x_im, h_re, h_im):
        ps, ch = pl.program_id(1), pl.program_id(2)
        a_re = jnp.broadcast_to(lre_r[...], (SEGS, wc))
        a_im = jnp.broadcast_to(lim_r[...], (SEGS, wc))
        ub = u_r[...]
        ub16 = ub.astype(BF16)
        x_re[...] = jnp.dot(ub16, bre_r[...], preferred_element_type=F32)
        x_im[...] = jnp.dot(ub16, bim_r[...], preferred_element_type=F32)

        @pl.when(jnp.logical_and(ps == 0, ch == 0))
        def _():
            st_re[...] = jnp.zeros_like(st_re)
            st_im[...] = jnp.zeros_like(st_im)

        @pl.when(jnp.logical_and(ps == 1, ch == 0))
        def _():
            c_re, c_im = _chain_segments(a_re, a_im, st_re[...], st_im[...], nsq, False)
            st_re[...] = c_re
            st_im[...] = c_im
            hin_re_o[...] = c_re
            hin_im_o[...] = c_im

        def run(store):
            def step(t, hc):
                off = pl.multiple_of(t * SEGS, SEGS)
                n_re = a_re * hc[0] - a_im * hc[1] + x_re[pl.ds(off, SEGS), :]
                n_im = a_re * hc[1] + a_im * hc[0] + x_im[pl.ds(off, SEGS), :]
                if store:
                    h_re[pl.ds(off, SEGS), :] = n_re
                    h_im[pl.ds(off, SEGS), :] = n_im
                return n_re, n_im

            fin = lax.fori_loop(0, tt, step, (st_re[...], st_im[...]))
            st_re[...] = fin[0]
            st_im[...] = fin[1]

        @pl.when(ps == 0)
        def _():
            run(False)

        @pl.when(ps == 1)
        def _():
            run(True)
            hr16, hi16 = h_re[...].astype(BF16), h_im[...].astype(BF16)
            hre_o[...] = hr16
            him_o[...] = hi16
            y = jnp.dot(hr16, cre_r[...], preferred_element_type=F32) + jnp.dot(hi16, cim_r[...], preferred_element_type=F32)
            y = y + d_r[...] * ub
            ys_o[...] = y
            yg_o[...] = _gelu(y)[0].astype(BF16)

    def pass1(ps, c):
        return jnp.where(ps == 1, c, 0)

    u_spec = pl.BlockSpec((rows, ub_w), lambda j, ps, c: (c, j))
    d_spec = pl.BlockSpec((1, ub_w), lambda j, ps, c: (0, j))
    b_spec = pl.BlockSpec((None, ub_w, wc), lambda j, ps, c: (j, 0, 0))
    c_spec = pl.BlockSpec((None, wc, ub_w), lambda j, ps, c: (j, 0, 0))
    l_spec = pl.BlockSpec((1, wc), lambda j, ps, c: (0, j))
    y_spec = pl.BlockSpec((rows, ub_w), lambda j, ps, c: (pass1(ps, c), j))
    h_spec = pl.BlockSpec((rows, wc), lambda j, ps, c: (pass1(ps, c), j))
    e_spec = pl.BlockSpec((SEGS, wc), lambda j, ps, c: (0, j))
    return _run(
        body, [u_s, dvec, w_bre, w_bim, w_cre, w_cim_neg, lre, lim], carry=carry, name=name, grid=(nb, 2, nch),
        in_specs=[u_spec, d_spec, b_spec, b_spec, c_spec, c_spec, l_spec, l_spec],
        out_specs=[y_spec, y_spec, h_spec, h_spec, e_spec, e_spec],
        out_shape=[jax.ShapeDtypeStruct((s, SSM_W), BF16), jax.ShapeDtypeStruct((s, SSM_W), F32),
                   jax.ShapeDtypeStruct((s, STATE_W), BF16), jax.ShapeDtypeStruct((s, STATE_W), BF16),
                   jax.ShapeDtypeStruct((SEGS, STATE_W), F32), jax.ShapeDtypeStruct((SEGS, STATE_W), F32)],
        scratch_shapes=[pltpu.VMEM((SEGS, wc), F32)] * 2 + [pltpu.VMEM((rows, wc), F32)] * 4,
        compiler_params=_cparams(("parallel", "arbitrary", "arbitrary")),
    )


def ssm_bwd(dyg_s, ys, u_s, h_re, h_im, hin_re, hin_im, gin_re, gin_im, dvec, w_bre_t, w_bim_t, w_cre_t, w_cim_neg_t, lre, lim,
            name, carry=None):
    s = u_s.shape[0]
    steps, tt, nch, rows, nsq = _scan_dims(s)
    half = 2 * SEGS

    def body(dyg_r, ys_r, u_r, hre_r, him_r, pre_r, pim_r, cin_re_r, cin_im_r, gin_re_r, gin_im_r, d_r, bre_r, bim_r, cre_r,
             cim_r, lre_r, lim_r, du_o, dbre_o, dbim_o, dcre_o, dcim_o, dlre_o, dlim_o, dd_o,
             st_re, st_im, x_re, x_im, g_re, g_im, hf_re, hf_im):
        ch = pl.program_id(1)
        a_re = jnp.broadcast_to(lre_r[...], (SEGS, SCAN_WC))
        a_im = -jnp.broadcast_to(lim_r[...], (SEGS, SCAN_WC))
        ub, y = u_r[...], ys_r[...]
        dy = dyg_r[...] * _gelu_grad(y, _gelu(y)[1])
        dy16 = dy.astype(BF16)
        x_re[...] = jnp.dot(dy16, cre_r[...], preferred_element_type=F32)
        x_im[...] = jnp.dot(dy16, cim_r[...], preferred_element_type=F32)

        @pl.when(ch == 0)
        def _():
            st_re[...] = gin_re_r[...]
            st_im[...] = gin_im_r[...]
            dlre_o[...] = jnp.zeros_like(dlre_o)
            dlim_o[...] = jnp.zeros_like(dlim_o)

        hf_re[...] = hre_r[...].astype(F32)
        hf_im[...] = him_r[...].astype(F32)
        first_chunk = ch == nch - 1
        edge_re = jnp.where(first_chunk, cin_re_r[...], pre_r[...].astype(F32)[SEGS:, :])
        edge_im = jnp.where(first_chunk, cin_im_r[...], pim_r[...].astype(F32)[SEGS:, :])

        def step(i, hc):
            t = tt - 1 - i
            off = pl.multiple_of(t * SEGS, SEGS)
            n_re = a_re * hc[0] - a_im * hc[1] + x_re[pl.ds(off, SEGS), :]
            n_im = a_re * hc[1] + a_im * hc[0] + x_im[pl.ds(off, SEGS), :]
            g_re[pl.ds(off, SEGS), :] = n_re
            g_im[pl.ds(off, SEGS), :] = n_im
            offp = pl.multiple_of(jnp.maximum(t - 1, 0) * SEGS, SEGS)
            hp_re = jnp.where(t == 0, edge_re, hf_re[pl.ds(offp, SEGS), :])
            hp_im = jnp.where(t == 0, edge_im, hf_im[pl.ds(offp, SEGS), :])
            return n_re, n_im, hc[2] + hp_re * n_re + hp_im * n_im, hc[3] + hp_re * n_im - hp_im * n_re

        fin = lax.fori_loop(0, tt, step, (st_re[...], st_im[...], dlre_o[...], dlim_o[...]))
        st_re[...] = fin[0]
        st_im[...] = fin[1]
        dlre_o[...] = fin[2]
        dlim_o[...] = fin[3]

        gr16, gi16 = g_re[...].astype(BF16), g_im[...].astype(BF16)
        du = jnp.dot(gr16, bre_r[...], preferred_element_type=F32) + jnp.dot(gi16, bim_r[...], preferred_element_type=F32)
        du_o[...] = du + d_r[...] * dy
        ub16 = ub.astype(BF16)
        parts = [
            (dbre_o, lax.dot_general(ub16, gr16, _DN["tn"], preferred_element_type=F32)),
            (dbim_o, lax.dot_general(ub16, gi16, _DN["tn"], preferred_element_type=F32)),
            (dcre_o, lax.dot_general(hre_r[...], dy16, _DN["tn"], preferred_element_type=F32)),
            (dcim_o, lax.dot_general(him_r[...], dy16, _DN["tn"], preferred_element_type=F32)),
            (dd_o, jnp.sum(dy * ub, axis=0, keepdims=True)),
        ]
        for ref, val in parts:
            @pl.when(ch == 0)
            def _(ref=ref, val=val):
                ref[...] = val

            @pl.when(ch > 0)
            def _(ref=ref, val=val):
                ref[...] += val

    def chunk(c):
        return nch - 1 - c

    u_spec = pl.BlockSpec((rows, U_BLK), lambda j, c: (chunk(c), j))
    h_spec = pl.BlockSpec((rows, SCAN_WC), lambda j, c: (chunk(c), j))
    prev_spec = pl.BlockSpec((half, SCAN_WC), lambda j, c: (jnp.maximum(chunk(c) * (rows // half) - 1, 0), j))
    e_spec = pl.BlockSpec((SEGS, SCAN_WC), lambda j, c: (0, j))
    d_spec = pl.BlockSpec((1, U_BLK), lambda j, c: (0, j))
    bt_spec = pl.BlockSpec((None, SCAN_WC, U_BLK), lambda j, c: (j, 0, 0))
    ct_spec = pl.BlockSpec((None, U_BLK, SCAN_WC), lambda j, c: (j, 0, 0))
    l_spec = pl.BlockSpec((1, SCAN_WC), lambda j, c: (0, j))
    return _run(
        body, [dyg_s, ys, u_s, h_re, h_im, h_re, h_im, hin_re, hin_im, gin_re, gin_im, dvec, w_bre_t, w_bim_t, w_cre_t,
               w_cim_neg_t, lre, lim],
        carry=carry, name=name, grid=(BD, nch),
        in_specs=[u_spec, u_spec, u_spec, h_spec, h_spec, prev_spec, prev_spec, e_spec, e_spec, e_spec, e_spec, d_spec,
                  bt_spec, bt_spec, ct_spec, ct_spec, l_spec, l_spec],
        out_specs=[u_spec, ct_spec, ct_spec, bt_spec, bt_spec, e_spec, e_spec, d_spec],
        out_shape=[jax.ShapeDtypeStruct((s, SSM_W), F32)] + [jax.ShapeDtypeStruct((BD, U_BLK, SCAN_WC), F32)] * 2
        + [jax.ShapeDtypeStruct((BD, SCAN_WC, U_BLK), F32)] * 2 + [jax.ShapeDtypeStruct((SEGS, STATE_W), F32)] * 2
        + [jax.ShapeDtypeStruct((1, SSM_W), F32)],
        scratch_shapes=[pltpu.VMEM((SEGS, SCAN_WC), F32)] * 2 + [pltpu.VMEM((rows, SCAN_WC), F32)] * 6,
        compiler_params=_cparams(("parallel", "arbitrary")),
    )


def ssm_bwd_ends(dyg_s, ys, w_cre_t, w_cim_neg_t, lre, lim, name, carry=None):
    s = ys.shape[0]
    steps, tt, nch, rows, nsq = _scan_dims(s)
    nb, ub_w, wc = w_cre_t.shape

    def body(dyg_r, ys_r, cre_r, cim_r, lre_r, lim_r, gin_re_o, gin_im_o, st_re, st_im, x_re, x_im):
        ch = pl.program_id(1)
        a_re = jnp.broadcast_to(lre_r[...], (SEGS, wc))
        a_im = -jnp.broadcast_to(lim_r[...], (SEGS, wc))
        y = ys_r[...]
        dy16 = (dyg_r[...] * _gelu_grad(y, _gelu(y)[1])).astype(BF16)
        x_re[...] = jnp.dot(dy16, cre_r[...], preferred_element_type=F32)
        x_im[...] = jnp.dot(dy16, cim_r[...], preferred_element_type=F32)

        @pl.when(ch == 0)
        def _():
            st_re[...] = jnp.zeros_like(st_re)
            st_im[...] = jnp.zeros_like(st_im)

        def step(i, hc):
            off = pl.multiple_of((tt - 1 - i) * SEGS, SEGS)
            return (a_re * hc[0] - a_im * hc[1] + x_re[pl.ds(off, SEGS), :],
                    a_re * hc[1] + a_im * hc[0] + x_im[pl.ds(off, SEGS), :])

        fin = lax.fori_loop(0, tt, step, (st_re[...], st_im[...]))
        st_re[...] = fin[0]
        st_im[...] = fin[1]

        @pl.when(ch == nch - 1)
        def _():
            c_re, c_im = _chain_segments(a_re, a_im, fin[0], fin[1], nsq, True)
            gin_re_o[...] = c_re
            gin_im_o[...] = c_im

    y_spec = pl.BlockSpec((rows, ub_w), lambda j, c: (nch - 1 - c, j))
    ct_spec = pl.BlockSpec((None, ub_w, wc), lambda j, c: (j, 0, 0))
    l_spec = pl.BlockSpec((1, wc), lambda j, c: (0, j))
    e_spec = pl.BlockSpec((SEGS, wc), lambda j, c: (0, j))
    return _run(
        body, [dyg_s, ys, w_cre_t, w_cim_neg_t, lre, lim], carry=carry, name=name, grid=(nb, nch),
        in_specs=[y_spec, y_spec, ct_spec, ct_spec, l_spec, l_spec], out_specs=[e_spec, e_spec],
        out_shape=[jax.ShapeDtypeStruct((SEGS, STATE_W), F32)] * 2,
        scratch_shapes=[pltpu.VMEM((SEGS, wc), F32)] * 2 + [pltpu.VMEM((rows, wc), F32)] * 2,
        compiler_params=_cparams(("parallel", "arbitrary")),
    )


FWD_BD = 4


def _block_diag(m, nb=BD):
    g, r, c = m.shape
    m = m.reshape(nb, g // nb, r, c)
    eye = jnp.eye(g // nb, dtype=m.dtype)
    return jnp.einsum("jarc,ab->jarbc", m, eye).reshape(nb, (g // nb) * r, (g // nb) * c)


def _block_diag_extract(m, r, c):
    per = m.shape[1] // r
    m = m.reshape(BD, per, r, per, c)
    return jnp.einsum("jarac->jarc", m).reshape(BD * per, r, c)


def to_segments(a):
    s, w = a.shape
    return a.reshape(SEGS, s // SEGS, w).transpose(1, 0, 2).reshape(s, w)


def from_segments(a):
    s, w = a.shape
    return a.reshape(s // SEGS, SEGS, w).transpose(1, 0, 2).reshape(s, w)


W_IN_CHUNK_ROWS = (320, 320, 320, 240, 576, 272)
FFN_GATE_ROWS_FIRST = 480
BEHIND_SSM_FWD = ("w_ffn_up", "w_attn_up", "w_glu_v", "w_glu_g", "w_out")
TALL_TM = 2048
FFN_TN = 512


def local_step(x, target, shards, small):
    s = x.shape[0]
    g1, g2, g3, g4 = (small[k].reshape(1, D_MODEL) for k in ("norm_mix_pre", "norm_mix_post", "norm_ffn_pre", "norm_ffn_post"))
    dvec = small["ssm_d"].reshape(1, SSM_W)
    wts, recv = {}, {}

    def gathered(names, blocks):
        for n, b in zip(names, blocks):
            wts[n] = _full_from_gathered(b, n)

    def rms_in_fn(r, c):
        hh = _rms(r[0], c[0])[0].astype(BF16)
        return [hh, _permute(_perm_matrix(PERM_TS, 4, False), hh), _permute(_perm_matrix(PERM_TS, 16, False), hh)], []

    (h, h4, h16), got = rowwise("rms_in", rms_in_fn, [x], [g1], [(D_MODEL, BF16), (D_MODEL, BF16, 4), (D_MODEL, BF16, 16)],
                                ts=PERM_TS, carry=Gather([shards["w_in"]]))
    w_in_t = _full_from_gathered(got[0], "w_in")
    w_u_t, w_gates_t = w_in_t[3 * HQ:3 * HQ + SSM_W], w_in_t[3 * HQ + SSM_W:]

    def qkv_rows(g):
        return 3 * GROUP_W, lambda t: 3 * t + g

    hd = [h.reshape(1, s, D_MODEL), h4, h16]
    qkv = [None] * 3
    qkv[0] = mm([(hd[0].reshape(s, D_MODEL), w_in_t)], "nt", BF16, "mm_qkv0", tm=TALL_TM, tn=GROUP_W, b_window=qkv_rows(0))
    qkv[1] = mm([(hd[1].reshape(s, D_MODEL), w_in_t)], "nt", BF16, "mm_qkv1", tm=TALL_TM, tn=GROUP_W, b_window=qkv_rows(1))
    qkv[2] = mm([(hd[2].reshape(s, D_MODEL), w_in_t)], "nt", BF16, "mm_qkv2", tm=TALL_TM, tn=GROUP_W, b_window=qkv_rows(2))
    u = mm([(h, w_u_t)], "nt", F32, "mm_u")
    gates, got = mm([(h, w_gates_t)], "nt", BF16, "mm_gates", carry=Gather([shards["w_ffn_gate"]]))
    gathered(("w_ffn_gate",), got)

    outs, lses = [], []
    for g, (_, dil) in enumerate(ATTN_GROUPS):
        o, l = attn_fwd(qkv[g].reshape(dil, s // dil, 3 * GROUP_W), g, f"attn_fwd{g}")
        outs.append(o.reshape(s, GROUP_W) if dil == 1 else o)
        lses.append(l.reshape(s, GROUP_W) if dil == 1 else l)

    def natural(r):
        back4, back16 = _perm_matrix(PERM_TS, 4, True), _perm_matrix(PERM_TS, 16, True)
        return (r[0], _permute(back4, r[1].astype(BF16)), _permute(back16, r[2].astype(BF16)),
                r[3], _permute(back4, r[4]), _permute(back16, r[5]))

    def merge_fn(r, c):
        o0, o1, o2, l0, l1, l2 = natural(r)
        w0, w1, w2 = _mix_weights(l0, l1, l2)
        return [w0 * o0 + w1 * o1 + w2 * o2], []

    (attn,) = rowwise("attn_merge", merge_fn, outs + lses, [], [(GROUP_W, BF16)], ts=PERM_TS)

    are3 = small["ssm_a_re"].reshape(SSM_GROUPS, SSM_STATE, 1)
    aim3 = small["ssm_a_im"].reshape(SSM_GROUPS, SSM_STATE, 1)
    ldt3 = small["ssm_log_dt"].reshape(SSM_GROUPS, 1, 1)
    bre3 = small["ssm_b_re"].reshape(SSM_GROUPS, SSM_STATE, SSM_GROUP)
    bim3 = small["ssm_b_im"].reshape(SSM_GROUPS, SSM_STATE, SSM_GROUP)
    cre3 = small["ssm_c_re"].reshape(SSM_GROUPS, SSM_GROUP, SSM_STATE)
    cim3 = small["ssm_c_im"].reshape(SSM_GROUPS, SSM_GROUP, SSM_STATE)
    lre3, lim3, bbre, bbim = ssm_prep(are3, aim3, ldt3, bre3, bim3)
    lre, lim = lre3.reshape(1, STATE_W), lim3.reshape(1, STATE_W)
    w_bre = _block_diag(bbre.transpose(0, 2, 1)).astype(BF16)
    w_bim = _block_diag(bbim.transpose(0, 2, 1)).astype(BF16)
    w_cre = _block_diag(cre3.transpose(0, 2, 1)).astype(BF16)
    w_cim = _block_diag(cim3.transpose(0, 2, 1)).astype(BF16)
    u_s = to_segments(u)
    fwd_w = [_block_diag(t.transpose(0, 2, 1), FWD_BD).astype(BF16) for t in (bbre, bbim, cre3, -cim3)]
    (yg_s, y_ssm, h_re, h_im, hin_re, hin_im), got = ssm_fwd(
        u_s, dvec, *fwd_w, lre, lim, "ssm_fwd", carry=Gather([shards[n] for n in BEHIND_SSM_FWD], pass_early=True))
    gathered(BEHIND_SSM_FWD, got)
    attn_branch = mm([(attn, wts["w_attn_up"])], "nn", BF16, "mm_up", tm=TALL_TM)
    yg = from_segments(yg_s)
    gv = mm([(yg, wts["w_glu_v"])], "nn", BF16, "mm_glu_v", tm=TALL_TM)
    gg = mm([(yg, wts["w_glu_g"])], "nn", BF16, "mm_glu_g", tm=TALL_TM)

    def gate_fn(r, c):
        gts, ab, gv_, gg_ = r
        sa, ss = _sigmoid(gts[:, :D_MODEL]), _sigmoid(gts[:, D_MODEL:])
        return [sa * ab + ss * (gv_ * _sigmoid(gg_))], []

    (merged,) = rowwise("gate_merge", gate_fn, [gates, attn_branch, gv, gg], [], [(D_MODEL, BF16)])
    o_mix = mm([(merged, wts["w_out"])], "nn", F32, "mm_out")

    def mid_fn(r, c):
        x1 = r[0] + _rms(r[1], c[0])[0]
        return [x1, _rms(x1, c[1])[0]], []

    x1, h2 = rowwise("rms_mid", mid_fn, [x, o_mix], [g2, g3], [(D_MODEL, F32), (D_MODEL, BF16)])
    (fa, fb, fin), got = mm([(h2, wts["w_ffn_gate"]), (h2, wts["w_ffn_up"])], "nt", [BF16, BF16, BF16], "mm_ffn_in", tn=FFN_TN,
                            epilogue=lambda p, e: [p[0], p[1], p[0] * _sigmoid(p[0]) * p[1]],
                            carry=Gather([shards["w_ffn_down"]], pass_early=True))
    gathered(("w_ffn_down",), got)
    f = mm([(fin, wts["w_ffn_down"])], "nn", F32, "mm_ffn_down", tn=512, tk=D_FF)

    def loss_fn(r, c):
        x1_, f_, tgt = r
        y, n, rr = _rms(f_, c[0])
        err = x1_ + y - tgt
        dout = err * (1.0 / D_MODEL)
        df, dg = _rms_bwd(dout, n, rr, c[0])
        lp = 0.5 * jnp.sum(jnp.sum(err * err, axis=-1, keepdims=True) * (1.0 / D_MODEL), axis=0, keepdims=True)
        return [df, dout], [dg, lp]

    df, dout, dg4, loss_part = rowwise("loss_bwd", loss_fn, [x1, f, target], [g4], [(D_MODEL, BF16), (D_MODEL, BF16)],
                                       acc_outs=[(1, D_MODEL), (1, 1)])
    def sent(names, blocks):
        for n, b in zip(names, blocks):
            recv[n] = b

    def to_owners(names, dws):
        return AllToAll([_split_for_devices(d, n) for n, d in zip(names, dws)])

    def swiglu_bwd(p, e):
        dfin_, (a, b) = p[0], e
        sg = _sigmoid(a)
        return [dfin_ * b * (sg * (1.0 + a * (1.0 - sg))), dfin_ * a * sg]

    da, db = mm([(df, wts["w_ffn_down"])], "nt", [BF16, BF16], "mm_d_fin", tn=FFN_TN, epilogue=swiglu_bwd, extras=[fa, fb])
    dw_ffn_down = mm([(fin, df)], "tn", BF16, "mm_dw_ffn_down")
    dh2, got = mm([(da, wts["w_ffn_gate"]), (db, wts["w_ffn_up"])], "nn", F32, "mm_d_h2", tm=512, tn=512, tk=D_FF,
                  carry=to_owners(["w_ffn_down"], [dw_ffn_down]))
    sent(["w_ffn_down"], got)
    dw_ffn_gate = mm([(da, h2)], "tn", BF16, "mm_dw_ffn_gate")
    gate_blocks = _split_for_devices(dw_ffn_gate, "w_ffn_gate")
    dw_ffn_up, (gate_landed,) = mm([(db, h2)], "tn", BF16, "mm_dw_ffn_up",
                                   carry=RowsToOwners(gate_blocks, 0, FFN_GATE_ROWS_FIRST))

    def mid_bwd(r, c):
        dh2_, dout_, x1_, o_ = r
        _, n3, r3 = _rms(x1_, c[1])
        dx1, dg3_ = _rms_bwd(dh2_, n3, r3, c[1])
        dx1 = dx1 + dout_
        _, n2, r2 = _rms(o_, c[0])
        do_, dg2_ = _rms_bwd(dx1, n2, r2, c[0])
        return [dx1, do_], [dg2_, dg3_]

    rest = gate_blocks.shape[1] - FFN_GATE_ROWS_FIRST
    (dx1, do_mix, dg2, dg3), (gate_landed,) = rowwise(
        "rms_mid_bwd", mid_bwd, [dh2, dout, x1, o_mix], [g2, g3], [(D_MODEL, F32), (D_MODEL, BF16)],
        acc_outs=[(1, D_MODEL), (1, D_MODEL)], carry=RowsToOwners(gate_blocks, FFN_GATE_ROWS_FIRST, rest, into=gate_landed))
    recv["w_ffn_gate"] = gate_landed
    dmerged = mm([(do_mix, wts["w_out"])], "nt", BF16, "mm_d_merged")
    dw_out = mm([(merged, do_mix)], "tn", BF16, "mm_dw_out")

    def gate_bwd(r, c):
        dm, gts, ab, gv_, gg_ = r
        sa, ss, sg = _sigmoid(gts[:, :D_MODEL]), _sigmoid(gts[:, D_MODEL:]), _sigmoid(gg_)
        branch = gv_ * sg
        dbranch = dm * ss
        dgates = jnp.concatenate([dm * ab * sa * (1.0 - sa), dm * branch * ss * (1.0 - ss)], axis=-1)
        return [dgates, dm * sa, dbranch * sg, dbranch * gv_ * sg * (1.0 - sg)], []

    dgates, dab, dgv, dgg = rowwise("gate_bwd", gate_bwd, [dmerged, gates, attn_branch, gv, gg], [],
                                    [(2 * D_MODEL, BF16), (D_MODEL, BF16), (D_MODEL, BF16), (D_MODEL, BF16)])
    dattn = mm([(dab, wts["w_attn_up"])], "nt", F32, "mm_d_attn")
    dw_up = mm([(attn, dab)], "tn", BF16, "mm_dw_up")
    dyg = mm([(dgv, wts["w_glu_v"]), (dgg, wts["w_glu_g"])], "nt", F32, "mm_d_yg")
    dw_glu_v = mm([(yg, dgv)], "tn", BF16, "mm_dw_glu_v")
    dw_glu_g = mm([(yg, dgg)], "tn", BF16, "mm_dw_glu_g")

    dyg_s = to_segments(dyg)
    (gin_re, gin_im), got = ssm_bwd_ends(dyg_s, y_ssm, fwd_w[2].transpose(0, 2, 1), fwd_w[3].transpose(0, 2, 1), lre, lim,
                                         "ssm_bwd_ends", carry=to_owners(["w_out"], [dw_out]))
    sent(["w_out"], got)
    (du_s, dbre_d, dbim_d, dcre_d, dcim_d, dl_re8, dl_im8, dd_ssm), got = ssm_bwd(
        dyg_s, y_ssm, u_s, h_re, h_im, hin_re, hin_im, gin_re, gin_im, dvec, w_bre.transpose(0, 2, 1), w_bim.transpose(0, 2, 1),
        w_cre.transpose(0, 2, 1), -w_cim.transpose(0, 2, 1), lre, lim, "ssm_bwd", carry=to_owners(["w_ffn_up"], [dw_ffn_up]))
    sent(["w_ffn_up"], got)
    dbb_re = _block_diag_extract(dbre_d, SSM_GROUP, SSM_STATE).transpose(0, 2, 1)
    dbb_im = _block_diag_extract(dbim_d, SSM_GROUP, SSM_STATE).transpose(0, 2, 1)
    dc_re = _block_diag_extract(dcre_d, SSM_STATE, SSM_GROUP).transpose(0, 2, 1)
    dc_im = -_block_diag_extract(dcim_d, SSM_STATE, SSM_GROUP).transpose(0, 2, 1)

    def fold8(r, c):
        return [], [jnp.sum(r[0], axis=0, keepdims=True), jnp.sum(r[1], axis=0, keepdims=True)]

    dl_re, dl_im = rowwise("ssm_dl_fold", fold8, [dl_re8, dl_im8], [], [], acc_outs=[(1, STATE_W), (1, STATE_W)], ts=SEGS)
    da_re, da_im, dldt, db_re, db_im = ssm_prep_bwd(
        are3, aim3, ldt3, bre3, bim3, dbb_re, dbb_im,
        dl_re.reshape(SSM_GROUPS, SSM_STATE, 1), dl_im.reshape(SSM_GROUPS, SSM_STATE, 1))
    du = from_segments(du_s)

    def merge_bwd(r, c):
        dat = r[0]
        o0, o1, o2, l0, l1, l2 = natural(r[1:])
        w0, w1, w2 = _mix_weights(l0, l1, l2)
        tot = _head_sum(dat * (w0 * o0 + w1 * o1 + w2 * o2))
        to4, to16 = _perm_matrix(PERM_TS, 4, False), _perm_matrix(PERM_TS, 16, False)
        return [w0 * dat, _permute(to4, (w1 * dat).astype(BF16)), _permute(to16, (w2 * dat).astype(BF16)),
                w0 * tot, _permute(to4, (w1 * tot).astype(BF16)), _permute(to16, (w2 * tot).astype(BF16))], []

    mb = rowwise("attn_merge_bwd", merge_bwd, [dattn] + outs + lses, [],
                 [(GROUP_W, BF16), (GROUP_W, BF16, 4), (GROUP_W, BF16, 16), (GROUP_W, BF16), (GROUP_W, BF16, 4), (GROUP_W, BF16, 16)],
                 ts=PERM_TS)
    dqs, dw_qkv = [], []
    names = ["w_glu_v", "w_glu_g", "w_attn_up"]
    for g, (_, dil) in enumerate(ATTN_GROUPS):
        dq = attn_bwd(qkv[g].reshape(dil, s // dil, 3 * GROUP_W), mb[g].reshape(dil, s // dil, GROUP_W),
                      lses[g].reshape(dil, s // dil, GROUP_W), mb[3 + g].reshape(dil, s // dil, GROUP_W),
                      g, f"attn_bwd{g}", carry=to_owners(names, [dw_glu_v, dw_glu_g, dw_up]) if g == 1 else None)
        if g == 1:
            dq, got = dq
            sent(names, got)
        dq = dq.reshape(s, 3 * GROUP_W)
        dqs.append(dq)
        dw_qkv.append(mm([(hd[g].reshape(s, D_MODEL), dq)], "tn", BF16, f"mm_dw_qkv{g}"))
    dw_u = mm([(h, du)], "tn", BF16, "mm_dw_u")
    dw_gates = mm([(h, dgates)], "tn", BF16, "mm_dw_gates")
    dw_in = jnp.concatenate(
        [dw_qkv[g][:, o * GROUP_W:(o + 1) * GROUP_W] for o in range(3) for g in range(3)] + [dw_u, dw_gates], axis=1)
    dw_in_blocks = _split_for_devices(dw_in, "w_in")
    starts = [sum(W_IN_CHUNK_ROWS[:i]) for i in range(len(W_IN_CHUNK_ROWS))]
    landed = None

    def chunk(i):
        return RowsToOwners(dw_in_blocks, starts[i], W_IN_CHUNK_ROWS[i], into=landed)

    dh_parts = []
    for g, (_, dil) in enumerate(ATTN_GROUPS):
        dh_g, (landed,) = mm([(dqs[g], w_in_t)], "nn", BF16, f"mm_d_h_qkv{g}", tk=GROUP_W, b_window=qkv_rows(g), carry=chunk(g))
        dh_parts.append(dh_g if dil == 1 else dh_g.reshape(dil, s // dil, D_MODEL))
    dh_u, (landed,) = mm([(du, w_u_t)], "nn", BF16, "mm_d_h_u", carry=chunk(3))
    dh_gates, (landed,) = mm([(dgates, w_gates_t)], "nn", BF16, "mm_d_h_gates", carry=chunk(4))
    dh_parts += [dh_u, dh_gates]

    def in_bwd(r, c):
        dh1 = _permute(_perm_matrix(PERM_TS, 4, True), r[1].astype(BF16))
        dh2_ = _permute(_perm_matrix(PERM_TS, 16, True), r[2].astype(BF16))
        dh = r[0] + dh1 + dh2_ + r[3] + r[4]
        _, n1, r1 = _rms(r[6], c[0])
        dx, dg1_ = _rms_bwd(dh, n1, r1, c[0])
        return [dx + r[5]], [dg1_]

    (grad_x, dg1), (landed,) = rowwise("rms_in_bwd", in_bwd, dh_parts + [dx1, x], [g1], [(D_MODEL, F32)],
                                       acc_outs=[(1, D_MODEL)], ts=PERM_TS, carry=chunk(5))
    recv["w_in"] = landed

    dsmall = dict(norm_mix_pre=dg1, ssm_a_re=da_re, ssm_a_im=da_im, ssm_log_dt=dldt, ssm_b_re=db_re, ssm_b_im=db_im,
                  ssm_c_re=dc_re, ssm_c_im=dc_im, ssm_d=dd_ssm, norm_mix_post=dg2, norm_ffn_pre=dg3, norm_ffn_post=dg4)
    return loss_part, grad_x, recv, dsmall


def adamw(parts, w, m, v, name, carry=None):
    r, c = w.shape
    tr = r
    while tr > 8 and tr % 2 == 0 and tr * c * (8 * parts.dtype.itemsize + 28) * 2 > 24 * 1024 * 1024:
        tr //= 2
    assert r % tr == 0 and (tr % 8 == 0 or tr == r)
    c1, c2 = 1.0 / (1.0 - ADAM_B1 ** ADAM_STEP), 1.0 / (1.0 - ADAM_B2 ** ADAM_STEP)

    def body(p_ref, w_ref, m_ref, v_ref, g_o, d_o, m_o, v_o):
        g = p_ref[0].astype(F32)
        for i in range(1, N_DEV):
            g = g + p_ref[i].astype(F32)
        mn = ADAM_B1 * m_ref[...] + (1.0 - ADAM_B1) * g
        vn = ADAM_B2 * v_ref[...] + (1.0 - ADAM_B2) * (g * g)
        g_o[...] = g
        m_o[...] = mn
        v_o[...] = vn
        d_o[...] = -ADAM_LR * ((mn * c1) / (jnp.sqrt(vn * c2) + ADAM_EPS) + ADAM_WD * w_ref[...])

    blk = pl.BlockSpec((tr, c), lambda i: (i, 0))
    return _run(
        body, [parts, w, m, v], carry=carry, name=name, grid=(r // tr,),
        in_specs=[pl.BlockSpec((N_DEV, tr, c), lambda i: (0, i, 0)), blk, blk, blk],
        out_specs=[blk] * 4, out_shape=[jax.ShapeDtypeStruct((r, c), F32)] * 4, compiler_params=_cparams(("parallel",)),
    )


PACK_C = 1024
SHARDED = ("w_in", "w_attn_up", "w_glu_v", "w_glu_g", "w_out", "w_ffn_gate", "w_ffn_up", "w_ffn_down")
ROW_SHARDED = ("w_out", "w_ffn_down")
SENT_TRANSPOSED = ("w_in", "w_ffn_gate", "w_ffn_up")
GRAD_TRANSPOSED = ("w_ffn_gate", "w_ffn_up")
SMALL = ("norm_mix_pre", "ssm_a_re", "ssm_a_im", "ssm_log_dt", "ssm_b_re", "ssm_b_im", "ssm_c_re", "ssm_c_im", "ssm_d",
         "norm_mix_post", "norm_ffn_pre", "norm_ffn_post")
WEIGHTS = ("norm_mix_pre", "w_in", "w_attn_up", "ssm_a_re", "ssm_a_im", "ssm_log_dt", "ssm_b_re", "ssm_b_im", "ssm_c_re",
           "ssm_c_im", "ssm_d", "w_glu_v", "w_glu_g", "w_out", "norm_mix_post", "norm_ffn_pre", "w_ffn_gate", "w_ffn_up",
           "w_ffn_down", "norm_ffn_post")


def _pack(arrs, dtype, pad_rows_to=64):
    flat = jnp.concatenate([a.reshape(-1).astype(dtype) for a in arrs])
    n = flat.shape[0]
    rows = -(-n // PACK_C)
    rows = -(-rows // pad_rows_to) * pad_rows_to
    return jnp.pad(flat, (0, rows * PACK_C - n)).reshape(rows, PACK_C)


def _unpack(flat2d, shapes):
    flat = flat2d.reshape(-1)
    out, off = [], 0
    for shp in shapes:
        n = int(np.prod(shp))
        out.append(flat[off:off + n].reshape(shp))
        off += n
    return out


def _full_from_gathered(gathered, name):
    if name in ROW_SHARDED or name in SENT_TRANSPOSED:
        return gathered.reshape(-1, gathered.shape[2])
    return gathered.transpose(1, 0, 2).reshape(gathered.shape[1], -1)


def _split_for_devices(full, name):
    if name in ROW_SHARDED or name in GRAD_TRANSPOSED:
        return full.reshape(N_DEV, -1, full.shape[1])
    return full.reshape(full.shape[0], N_DEV, -1).transpose(1, 0, 2)


def kernel(x, norm_mix_pre, w_in, w_attn_up, ssm_a_re, ssm_a_im, ssm_log_dt, ssm_b_re, ssm_b_im, ssm_c_re, ssm_c_im, ssm_d, w_glu_v, w_glu_g, w_out, norm_mix_post, norm_ffn_pre, w_ffn_gate, w_ffn_up, w_ffn_down, norm_ffn_post, loss_target, m_norm_mix_pre, m_w_in, m_w_attn_up, m_ssm_a_re, m_ssm_a_im, m_ssm_log_dt, m_ssm_b_re, m_ssm_b_im, m_ssm_c_re, m_ssm_c_im, m_ssm_d, m_w_glu_v, m_w_glu_g, m_w_out, m_norm_mix_post, m_norm_ffn_pre, m_w_ffn_gate, m_w_ffn_up, m_w_ffn_down, m_norm_ffn_post, v_norm_mix_pre, v_w_in, v_w_attn_up, v_ssm_a_re, v_ssm_a_im, v_ssm_log_dt, v_ssm_b_re, v_ssm_b_im, v_ssm_c_re, v_ssm_c_im, v_ssm_d, v_w_glu_v, v_w_glu_g, v_w_out, v_norm_mix_post, v_norm_ffn_pre, v_w_ffn_gate, v_w_ffn_up, v_w_ffn_down, v_norm_ffn_post):
    args = dict(locals())
    wv = {n: args[n][0] for n in WEIGHTS}
    mv = {n: args["m_" + n][0] for n in WEIGHTS}
    vv = {n: args["v_" + n][0] for n in WEIGHTS}

    shards = {n: (wv[n].T if n in SENT_TRANSPOSED else wv[n]).astype(BF16) for n in SHARDED}
    small = {n: wv[n] for n in SMALL}
    loss_part, grad_x, recv, dsmall = local_step(x[0], loss_target[0], shards, small)
    for n in GRAD_TRANSPOSED:
        recv[n] = recv[n].transpose(0, 2, 1)

    small_shapes = [wv[n].shape for n in SMALL]
    res = {}
    res["w_in"], (sgather,) = adamw(recv["w_in"], wv["w_in"], mv["w_in"], vv["w_in"], "adamw_w_in",
                                    carry=Gather([_pack([dsmall[n] for n in SMALL], F32)]))
    for n in SHARDED[1:]:
        res[n] = adamw(recv[n], wv[n], mv[n], vv[n], "adamw_" + n)
    sres = adamw(sgather, _pack([wv[n] for n in SMALL], F32), _pack([mv[n] for n in SMALL], F32),
                 _pack([vv[n] for n in SMALL], F32), "adamw_small")
    sun = [_unpack(t, small_shapes) for t in sres]
    for k, n in enumerate(SMALL):
        res[n] = tuple(sun[t][k] for t in range(4))

    loss = lax.psum(loss_part[0, 0], ("x", "y", "c"))
    outs = [loss, grad_x[None]]
    for t in range(4):
        outs += [res[n][t][None] for n in WEIGHTS]
    return tuple(outs)
```

```python
import math

import numpy as np
import jax
import jax.numpy as jnp
from jax import lax
from jax.experimental import pallas as pl
from jax.experimental.pallas import tpu as pltpu

F32 = jnp.float32
BF16 = jnp.bfloat16

D_MODEL = 2048
HEAD_DIM = 128
HEADS_PER_GROUP = 4
ATTN_GROUPS = ((128, 1), (512, 4), (2048, 16))
N_HEADS = HEADS_PER_GROUP * len(ATTN_GROUPS)
GROUP_W = HEADS_PER_GROUP * HEAD_DIM
HQ = N_HEADS * HEAD_DIM
SSM_W = 1024
SSM_GROUP = 16
SSM_GROUPS = 64
SSM_STATE = 64
STATE_W = SSM_GROUPS * SSM_STATE
D_FF = 5632
EPS = 1e-6
N_DEV = 8
SEGS = 8
BD = 8

ADAM_LR, ADAM_B1, ADAM_B2, ADAM_EPS, ADAM_WD, ADAM_STEP = 0.001, 0.9, 0.999, 1e-08, 0.01, 10

VMEM_LIMIT = 56 * 1024 * 1024
HBM_SPEC = pl.BlockSpec(memory_space=pltpu.HBM)
MESH_ID = pl.DeviceIdType.MESH
NEG = -1e30


def _pcall(body, **kw):
    return pl.pallas_call(body, **kw)


def _cparams(sem=None):
    if sem is None:
        return pltpu.CompilerParams(vmem_limit_bytes=VMEM_LIMIT)
    return pltpu.CompilerParams(vmem_limit_bytes=VMEM_LIMIT, dimension_semantics=sem)


def _my_coords():
    return lax.axis_index("x"), lax.axis_index("y"), lax.axis_index("c")


class Gather:
    def __init__(self, xs, pass_early=False):
        self.arrays = list(xs)
        self.out_shapes = [jax.ShapeDtypeStruct((N_DEV,) + x.shape, x.dtype) for x in xs]
        self.pass_early = pass_early

    def _ctx(self, out_refs, send_sems, recv_sems):
        mx, my, mc = _my_coords()
        me, sibling = (mx, my, mc), (mx, my, 1 - mc)
        chips = [(1 - mx, my), (mx, 1 - my), (1 - mx, 1 - my)]

        def slot(a, px, py, pc):
            return out_refs[a].at[4 * px + 2 * py + pc]

        def copy(a, k, block, to, src=None):
            return pltpu.make_async_remote_copy(
                src_ref=slot(a, *block) if src is None else src, dst_ref=slot(a, *block),
                send_sem=send_sems.at[7 * a + k], recv_sem=recv_sems.at[7 * a + k], device_id=to, device_id_type=MESH_ID)

        return me, sibling, chips, mc, slot, copy

    def _first(self, a, x_refs, ctx):
        me, sibling, chips, mc, slot, copy = ctx
        return [copy(a, 0, me, sibling, src=x_refs[a])] + [copy(a, 1 + j, me, (*chip, mc), src=x_refs[a]) for j, chip in enumerate(chips)]

    def start(self, x_refs, out_refs, send_sems, recv_sems, local_sems):
        ctx = self._ctx(out_refs, send_sems, recv_sems)
        me, slot = ctx[0], ctx[4]
        for a in range(len(self.arrays)):
            pltpu.make_async_copy(x_refs[a], slot(a, *me), local_sems.at[a]).start()
            for cp in self._first(a, x_refs, ctx):
                cp.start()

    def _passed(self, ctx):
        me, sibling, chips, mc, slot, copy = ctx
        return [copy(a, 4 + j, (*chip, mc), sibling) for a in range(len(self.arrays)) for j, chip in enumerate(chips)]

    def middle(self, x_refs, out_refs, send_sems, recv_sems, local_sems):
        ctx = self._ctx(out_refs, send_sems, recv_sems)
        me, sibling, chips, mc, slot, copy = ctx
        for a in range(len(self.arrays)):
            for j, chip in enumerate(chips):
                copy(a, 1 + j, (*chip, mc), me).wait_recv()
                copy(a, 4 + j, (*chip, mc), sibling).start()

    def finish(self, x_refs, out_refs, send_sems, recv_sems, local_sems, passed_on=False):
        if not passed_on:
            self.middle(x_refs, out_refs, send_sems, recv_sems, local_sems)
        ctx = self._ctx(out_refs, send_sems, recv_sems)
        me, sibling, chips, mc, slot, copy = ctx
        na = len(self.arrays)
        passed = self._passed(ctx)
        for a in range(na):
            copy(a, 0, sibling, me).wait_recv()
            for j, chip in enumerate(chips):
                copy(a, 4 + j, (*chip, 1 - mc), me).wait_recv()
        for a in range(na):
            for cp in self._first(a, x_refs, ctx):
                cp.wait_send()
        for cp in passed:
            cp.wait_send()
        for a in range(na):
            pltpu.make_async_copy(x_refs[a], slot(a, *me), local_sems.at[a]).wait()


class AllToAll:
    def __init__(self, ps):
        self.arrays = list(ps)
        self.out_shapes = [jax.ShapeDtypeStruct(p.shape, p.dtype) for p in ps]

    def _copies(self, p_refs, out_refs, send_sems, recv_sems, local_sems):
        mx, my, mc = _my_coords()
        me = 4 * mx + 2 * my + mc
        local, remote = [], []
        for a in range(len(self.arrays)):
            local.append(pltpu.make_async_copy(p_refs[a].at[me], out_refs[a].at[me], local_sems.at[a]))
            for k in range(1, N_DEV):
                px, py, pc = mx ^ ((k >> 2) & 1), my ^ ((k >> 1) & 1), mc ^ (k & 1)
                remote.append(pltpu.make_async_remote_copy(
                    src_ref=p_refs[a].at[4 * px + 2 * py + pc], dst_ref=out_refs[a].at[me],
                    send_sem=send_sems.at[7 * a + k - 1], recv_sem=recv_sems.at[7 * a + k - 1],
                    device_id=(px, py, pc), device_id_type=MESH_ID))
        return local, remote

    def start(self, *refs):
        local, remote = self._copies(*refs)
        for cp in local + remote:
            cp.start()

    def finish(self, *refs):
        local, remote = self._copies(*refs)
        for cp in remote:
            cp.wait_recv()
        for cp in remote:
            cp.wait_send()
        for cp in local:
            cp.wait()


class RowsToOwners:
    def __init__(self, p, r0, n, into=None):
        self.arrays = [p] if into is None else [p, into]
        self.out_shapes = [jax.ShapeDtypeStruct(p.shape, p.dtype)]
        self.aliases = {} if into is None else {1: 0}
        self.rows = (r0, n)

    def _copies(self, p_refs, out_refs, send_sems, recv_sems, local_sems):
        mx, my, mc = _my_coords()
        me = 4 * mx + 2 * my + mc
        rows = pl.ds(*self.rows)
        local = [pltpu.make_async_copy(p_refs[0].at[me, rows], out_refs[0].at[me, rows], local_sems.at[0])]
        remote = []
        for k in range(1, N_DEV):
            px, py, pc = mx ^ ((k >> 2) & 1), my ^ ((k >> 1) & 1), mc ^ (k & 1)
            remote.append(pltpu.make_async_remote_copy(
                src_ref=p_refs[0].at[4 * px + 2 * py + pc, rows], dst_ref=out_refs[0].at[me, rows],
                send_sem=send_sems.at[k - 1], recv_sem=recv_sems.at[k - 1], device_id=(px, py, pc), device_id_type=MESH_ID))
        return local, remote

    start = AllToAll.start
    finish = AllToAll.finish


def _run(body, args, carry=None, **kw):
    if carry is None:
        return _pcall(body, **kw)(*args)
    grid = kw["grid"]
    single = not isinstance(kw["out_shape"], (list, tuple))
    in_specs = list(kw["in_specs"])
    out_specs = [kw["out_specs"]] if single else list(kw["out_specs"])
    out_shape = [kw["out_shape"]] if single else list(kw["out_shape"])
    scratch = list(kw.get("scratch_shapes", []))
    na, nin, nout, nscr = len(carry.arrays), len(in_specs), len(out_specs), len(scratch)
    nco = len(carry.out_shapes)
    aliases = {nin + i: nout + o for i, o in getattr(carry, "aliases", {}).items()}
    steps = int(np.prod(grid))
    mid_step = (steps * 7) // 10 if getattr(carry, "pass_early", False) and steps >= 4 else None

    def carried(*refs):
        ins, cin = refs[:nin], refs[nin:nin + na]
        outs, cout = refs[nin + na:nin + na + nout], refs[nin + na + nout:nin + na + nout + nco]
        scr = refs[nin + na + nout + nco:nin + na + nout + nco + nscr]
        sems = refs[nin + na + nout + nco + nscr:]
        step = pl.program_id(0)
        for i in range(1, len(grid)):
            step = step * grid[i] + pl.program_id(i)

        @pl.when(step == 0)
        def _():
            carry.start(cin, cout, *sems)

        if mid_step is not None:
            @pl.when(step == mid_step)
            def _():
                carry.middle(cin, cout, *sems)

        body(*ins, *outs, *scr)

        @pl.when(step == steps - 1)
        def _():
            if mid_step is not None:
                carry.finish(cin, cout, *sems, passed_on=True)
            else:
                carry.finish(cin, cout, *sems)

    res = _pcall(
        carried, name=kw["name"], grid=grid, in_specs=in_specs + [HBM_SPEC] * na, out_specs=out_specs + [HBM_SPEC] * nco,
        out_shape=out_shape + carry.out_shapes, input_output_aliases=aliases,
        scratch_shapes=scratch + [pltpu.SemaphoreType.DMA((7 * na,)), pltpu.SemaphoreType.DMA((7 * na,)), pltpu.SemaphoreType.DMA((na,))],
        compiler_params=_cparams(("arbitrary",) * len(grid)),
    )(*args, *carry.arrays)
    main = res[:nout]
    return (main[0] if single else main), list(res[nout:])


_DN = {"nn": (((1,), (0,)), ((), ())), "nt": (((1,), (1,)), ((), ())), "tn": (((0,), (0,)), ((), ()))}


LANE = 128
MM_TM, MM_TN, MM_TK = 1024, 1536, 2048


def _tile(n, cap):
    for t in range(min(cap, n) // LANE * LANE, 0, -LANE):
        if n % t == 0:
            return t
    raise ValueError(n)


DW_TM, DW_TN, DW_TK = 512, 512, 8192
EPILOGUE_SPLIT = 2


def mm(pairs, mode, out_dtype, name, tm=None, tn=None, tk=None, carry=None, epilogue=None, extras=(), b_window=None):
    a0, b0 = pairs[0]
    if mode == "nn":
        (m, k), n = a0.shape, b0.shape[1]
    elif mode == "nt":
        (m, k), n = a0.shape, b0.shape[0]
    else:
        (k, m), n = a0.shape, b0.shape[1]
    if b_window is not None:
        assert mode in ("nn", "nt") and len(pairs) == 1
        if mode == "nt":
            n = b_window[0]
        else:
            assert k == b_window[0]
    caps = (DW_TM, DW_TN, DW_TK) if mode == "tn" else (MM_TM, MM_TN, MM_TK)
    tm, tn, tk = _tile(m, tm or caps[0]), _tile(n, tn or caps[1]), _tile(k, tk or caps[2])
    nk = k // tk
    npairs = len(pairs)
    nex = len(extras)
    fused = epilogue is not None
    assert not fused or nk == 1
    out_dtypes = list(out_dtype) if fused else [out_dtype]

    def body(*refs):
        if fused:
            half = tn // EPILOGUE_SPLIT
            for c in range(EPILOGUE_SPLIT):
                cols = slice(c * half, (c + 1) * half)
                prods = []
                for p in range(npairs):
                    a = refs[2 * p][...].astype(BF16)
                    b = (refs[2 * p + 1][cols, :] if mode == "nt" else refs[2 * p + 1][:, cols]).astype(BF16)
                    prods.append(lax.dot_general(a, b, _DN[mode], preferred_element_type=F32))
                ex = [refs[2 * npairs + e][:, cols].astype(F32) for e in range(nex)]
                for o_ref, val in zip(refs[2 * npairs + nex:], epilogue(prods, ex)):
                    o_ref[:, cols] = val.astype(o_ref.dtype)
            return
        prods = []
        for p in range(npairs):
            a = refs[2 * p][...].astype(BF16) if (p == 0 or pairs[p][0] is not pairs[p - 1][0]) else a
            b = refs[2 * p + 1][...].astype(BF16)
            prods.append(lax.dot_general(a, b, _DN[mode], preferred_element_type=F32))
        o_ref = refs[2 * npairs]
        tot = prods[0]
        for d in prods[1:]:
            tot = tot + d
        if nk == 1:
            o_ref[...] = tot.astype(o_ref.dtype)
            return
        acc = refs[2 * npairs + 1]
        kk = pl.program_id(2)

        @pl.when(kk == 0)
        def _():
            acc[...] = tot

        @pl.when(kk > 0)
        def _():
            acc[...] += tot

        @pl.when(kk == nk - 1)
        def _():
            o_ref[...] = acc[...].astype(o_ref.dtype)

    rows_of = b_window[1] if b_window is not None else (lambda t: t)
    if mode == "nn":
        sp = [pl.BlockSpec((tm, tk), lambda i, j, kk: (i, kk)), pl.BlockSpec((tk, tn), lambda i, j, kk: (rows_of(kk), j))]
    elif mode == "nt":
        sp = [pl.BlockSpec((tm, tk), lambda i, j, kk: (i, kk)), pl.BlockSpec((tn, tk), lambda i, j, kk: (rows_of(j), kk))]
    else:
        sp = [pl.BlockSpec((tk, tm), lambda i, j, kk: (kk, i)), pl.BlockSpec((tk, tn), lambda i, j, kk: (kk, j))]
    o_spec = pl.BlockSpec((tm, tn), lambda i, j, kk: (i, j))
    out_shapes = [jax.ShapeDtypeStruct((m, n), dt) for dt in out_dtypes]
    return _run(
        body, [t for pr in pairs for t in pr] + list(extras), carry=carry, name=name, grid=(m // tm, n // tn, nk),
        in_specs=sp * npairs + [o_spec] * nex,
        out_specs=[o_spec] * len(out_shapes) if fused else o_spec,
        out_shape=out_shapes if fused else out_shapes[0],
        scratch_shapes=[pltpu.VMEM((tm, tn), F32)] if nk > 1 else [],
        compiler_params=_cparams(("parallel", "parallel", "arbitrary")),
    )


def rowwise(name, fn, row_ins, const_ins, row_outs, acc_outs=(), ts=None, carry=None):
    s = row_ins[0].shape[0]
    row_outs = [ro if len(ro) == 3 else (*ro, 1) for ro in row_outs]
    if ts is None:
        per_row = sum(a.shape[-1] * a.dtype.itemsize for a in row_ins) + sum(w * jnp.dtype(dt).itemsize for w, dt, _ in row_outs)
        ts = 512
        while ts > 8 and 2 * ts * per_row > 20 * 1024 * 1024:
            ts //= 2
    ts = min(ts, s)
    assert s % ts == 0
    nr, nc, no, na = len(row_ins), len(const_ins), len(row_outs), len(acc_outs)

    def body(*refs):
        rows = [r[...].reshape(ts, r.shape[-1]).astype(F32) for r in refs[:nr]]
        consts = [r[...] for r in refs[nr:nr + nc]]
        outs, accs = fn(rows, consts)
        for r, v in zip(refs[nr + nc:nr + nc + no], outs):
            r[...] = v.astype(r.dtype).reshape(r.shape)
        if na:
            first = pl.program_id(0) == 0
            for r, v in zip(refs[nr + nc + no:], accs):
                @pl.when(first)
                def _(r=r, v=v):
                    r[...] = v

                @pl.when(jnp.logical_not(first))
                def _(r=r, v=v):
                    r[...] += v

    def tile_spec(w, d):
        if d == 1:
            return pl.BlockSpec((ts, w), lambda i: (i, 0))
        return pl.BlockSpec((d, ts // d, w), lambda i: (0, i, 0))

    in_specs = [tile_spec(a.shape[-1], a.shape[0] if a.ndim == 3 else 1) for a in row_ins]
    in_specs += [pl.BlockSpec(c.shape, lambda i, nd=c.ndim: (0,) * nd) for c in const_ins]
    out_specs = [tile_spec(w, d) for w, _, d in row_outs]
    out_specs += [pl.BlockSpec(shp, lambda i, nd=len(shp): (0,) * nd) for shp in acc_outs]
    out_shape = [jax.ShapeDtypeStruct((s, w) if d == 1 else (d, s // d, w), dt) for w, dt, d in row_outs]
    out_shape += [jax.ShapeDtypeStruct(shp, F32) for shp in acc_outs]
    return _run(
        body, [*row_ins, *const_ins], carry=carry, name=name, grid=(s // ts,), in_specs=in_specs, out_specs=out_specs,
        out_shape=out_shape, compiler_params=_cparams(("arbitrary",)),
    )


PERM_TS = 256


def _perm_matrix(ts, d, inverse):
    i = lax.broadcasted_iota(jnp.int32, (ts, ts), 0)
    k = lax.broadcasted_iota(jnp.int32, (ts, ts), 1)
    per = ts // d
    src = (i % d) * per + i // d if inverse else (i % per) * d + i // per
    return jnp.where(k == src, 1.0, 0.0).astype(BF16)


def _permute(p, x):
    if x.dtype == BF16:
        return jnp.dot(p, x, preferred_element_type=F32)
    hi = x.astype(BF16)
    rest = x - hi.astype(F32)
    mid = rest.astype(BF16)
    lo = (rest - mid.astype(F32)).astype(BF16)
    out = jnp.dot(p, hi, preferred_element_type=F32) + jnp.dot(p, mid, preferred_element_type=F32)
    return out + jnp.dot(p, lo, preferred_element_type=F32)


def _rms(x, gain):
    r = lax.rsqrt(jnp.mean(x * x, axis=-1, keepdims=True) + EPS)
    n = x * r
    return n * gain, n, r


def _rms_bwd(dy, n, r, gain):
    dn = dy * gain
    dx = r * (dn - n * jnp.mean(dn * n, axis=-1, keepdims=True))
    return dx, jnp.sum(dy * n, axis=0, keepdims=True)


def _sigmoid(x):
    return 1.0 / (1.0 + jnp.exp(-x))


_GELU_K = math.sqrt(2.0 / math.pi)


def _gelu(x):
    t = jnp.tanh(_GELU_K * (x + 0.044715 * x * x * x))
    return 0.5 * x * (1.0 + t), t


def _gelu_grad(x, t):
    return 0.5 * (1.0 + t) + 0.5 * x * (1.0 - t * t) * _GELU_K * (1.0 + 3.0 * 0.044715 * x * x)


def _head_sum(x):
    parts = []
    for h in range(HEADS_PER_GROUP):
        sl = x[:, h * HEAD_DIM:(h + 1) * HEAD_DIM]
        parts.append(jnp.broadcast_to(jnp.sum(sl, axis=-1, keepdims=True), sl.shape))
    return jnp.concatenate(parts, axis=-1)


def _mix_weights(l0, l1, l2):
    mx = jnp.maximum(jnp.maximum(l0, l1), l2)
    e0, e1, e2 = jnp.exp(l0 - mx), jnp.exp(l1 - mx), jnp.exp(l2 - mx)
    inv = 1.0 / (e0 + e1 + e2)
    return e0 * inv, e1 * inv, e2 * inv


BLK = 128


def _slopes(g):
    return [2.0 ** (-8.0 * (g * HEADS_PER_GROUP + h + 1) / N_HEADS) for h in range(HEADS_PER_GROUP)]


def _attn_masks(dil):
    qi = lax.broadcasted_iota(jnp.int32, (BLK, BLK), 0)
    ki = lax.broadcasted_iota(jnp.int32, (BLK, BLK), 1)
    dist_c = qi - ki
    dist_p = BLK + qi - ki
    return dist_c >= 0, dist_p <= BLK, (dist_c * dil).astype(F32), (dist_p * dil).astype(F32)


def _window_mask(has_prev, dil):
    qi = lax.broadcasted_iota(jnp.int32, (BLK, 2 * BLK), 0)
    ki = lax.broadcasted_iota(jnp.int32, (BLK, 2 * BLK), 1)
    dist = BLK + qi - ki
    ok = jnp.logical_and(jnp.logical_and(dist >= 0, dist <= BLK), jnp.logical_or(ki >= BLK, has_prev))
    return ok, (dist * dil).astype(F32)


def attn_fwd(qkv, g, name):
    dil, length, _ = qkv.shape
    scale = HEAD_DIM ** -0.5
    slopes = _slopes(g)

    def body(q_ref, kc_ref, vc_ref, kp_ref, vp_ref, o_ref, l_ref):
        ok, dist = _window_mask(pl.program_id(1) > 0, dil)
        for h in range(HEADS_PER_GROUP):
            sl = slice(h * HEAD_DIM, (h + 1) * HEAD_DIM)
            k2 = jnp.concatenate([kp_ref[:, sl], kc_ref[:, sl]], axis=0)
            v2 = jnp.concatenate([vp_ref[:, sl], vc_ref[:, sl]], axis=0)
            s = lax.dot_general(q_ref[:, sl], k2, _DN["nt"], preferred_element_type=F32) * scale - slopes[h] * dist
            s = jnp.where(ok, s, NEG)
            mx = jnp.max(s, axis=-1, keepdims=True)
            p = jnp.exp(s - mx)
            den = jnp.sum(p, axis=-1, keepdims=True)
            o_ref[:, sl] = (jnp.dot(p.astype(BF16), v2, preferred_element_type=F32) / den).astype(BF16)
            l_ref[:, sl] = jnp.broadcast_to(mx + jnp.log(den), (BLK, HEAD_DIM))

    def spec(col, prev):
        if prev:
            return pl.BlockSpec((None, BLK, GROUP_W), lambda r, n: (r, jnp.maximum(n - 1, 0), col))
        return pl.BlockSpec((None, BLK, GROUP_W), lambda r, n: (r, n, col))

    out_spec = pl.BlockSpec((None, BLK, GROUP_W), lambda r, n: (r, n, 0))
    return _pcall(
        body, name=name, grid=(dil, length // BLK),
        in_specs=[spec(0, False), spec(1, False), spec(2, False), spec(1, True), spec(2, True)],
        out_specs=[out_spec, out_spec],
        out_shape=[jax.ShapeDtypeStruct((dil, length, GROUP_W), BF16), jax.ShapeDtypeStruct((dil, length, GROUP_W), F32)],
        compiler_params=_cparams(("parallel", "parallel")),
    )(qkv, qkv, qkv, qkv, qkv)


def attn_bwd(qkv, dout, lse, dd, g, name, carry=None):
    dil, length, _ = qkv.shape
    nblk = length // BLK
    scale = HEAD_DIM ** -0.5
    slopes = _slopes(g)

    def body(q_ref, kc_ref, vc_ref, kp_ref, vp_ref, qn_ref, do_ref, don_ref, l_ref, ln_ref, d_ref, dn_ref, o_ref):
        n = pl.program_id(1)
        ok2, dist2 = _window_mask(n > 0, dil)
        _, ok_p, _, dp = _attn_masks(dil)
        ok_next = jnp.logical_and(ok_p, n < nblk - 1)
        for h in range(HEADS_PER_GROUP):
            sl = slice(h * HEAD_DIM, (h + 1) * HEAD_DIM)
            q, kc, vc, qn = q_ref[:, sl], kc_ref[:, sl], vc_ref[:, sl], qn_ref[:, sl]
            k2 = jnp.concatenate([kp_ref[:, sl], kc], axis=0)
            v2 = jnp.concatenate([vp_ref[:, sl], vc], axis=0)
            do, don = do_ref[:, sl], don_ref[:, sl]
            lse_q, lse_n, dd_q, dd_n = l_ref[:, sl], ln_ref[:, sl], d_ref[:, sl], dn_ref[:, sl]

            def probs(qq, kk, dist, ok, lse_t):
                s = lax.dot_general(qq, kk, _DN["nt"], preferred_element_type=F32) * scale - slopes[h] * dist
                return jnp.where(ok, jnp.exp(jnp.where(ok, s, NEG) - lse_t), 0.0)

            p2 = probs(q, k2, dist2, ok2, jnp.concatenate([lse_q, lse_q], axis=1))
            p_x = probs(qn, kc, dp, ok_next, lse_n)
            ds2 = p2 * (lax.dot_general(do, v2, _DN["nt"], preferred_element_type=F32) - jnp.concatenate([dd_q, dd_q], axis=1))
            ds_x = p_x * (lax.dot_general(don, vc, _DN["nt"], preferred_element_type=F32) - dd_n)
            dq = jnp.dot(ds2.astype(BF16), k2, preferred_element_type=F32)
            ds_k = jnp.concatenate([ds2[:, BLK:], ds_x], axis=0).astype(BF16)
            p_k = jnp.concatenate([p2[:, BLK:], p_x], axis=0).astype(BF16)
            dk = lax.dot_general(ds_k, jnp.concatenate([q, qn], axis=0), _DN["tn"], preferred_element_type=F32)
            dv = lax.dot_general(p_k, jnp.concatenate([do, don], axis=0), _DN["tn"], preferred_element_type=F32)
            o_ref[:, h * HEAD_DIM:(h + 1) * HEAD_DIM] = (dq * scale).astype(BF16)
            o_ref[:, GROUP_W + h * HEAD_DIM:GROUP_W + (h + 1) * HEAD_DIM] = (dk * scale).astype(BF16)
            o_ref[:, 2 * GROUP_W + h * HEAD_DIM:2 * GROUP_W + (h + 1) * HEAD_DIM] = dv.astype(BF16)

    def spec(col, which):
        if which == "prev":
            return pl.BlockSpec((None, BLK, GROUP_W), lambda r, n: (r, jnp.maximum(n - 1, 0), col))
        if which == "next":
            return pl.BlockSpec((None, BLK, GROUP_W), lambda r, n: (r, jnp.minimum(n + 1, nblk - 1), col))
        return pl.BlockSpec((None, BLK, GROUP_W), lambda r, n: (r, n, col))

    return _run(
        body, [qkv, qkv, qkv, qkv, qkv, qkv, dout, dout, lse, lse, dd, dd], carry=carry, name=name, grid=(dil, nblk),
        in_specs=[spec(0, "cur"), spec(1, "cur"), spec(2, "cur"), spec(1, "prev"), spec(2, "prev"), spec(0, "next"),
                  spec(0, "cur"), spec(0, "next"), spec(0, "cur"), spec(0, "next"), spec(0, "cur"), spec(0, "next")],
        out_specs=pl.BlockSpec((None, BLK, 3 * GROUP_W), lambda r, n: (r, n, 0)),
        out_shape=jax.ShapeDtypeStruct((dil, length, 3 * GROUP_W), BF16),
        compiler_params=_cparams(("parallel", "parallel")),
    )


def _ssm_prep_values(are, aim, logdt):
    dt = jnp.exp(logdt)
    mag = jnp.exp(are * dt)
    lb_re, lb_im = mag * jnp.cos(aim * dt), mag * jnp.sin(aim * dt)
    inv = 1.0 / (are * are + aim * aim)
    n_re, n_im = lb_re - 1.0, lb_im
    f_re = (n_re * are + n_im * aim) * inv
    f_im = (n_im * are - n_re * aim) * inv
    return dt, lb_re, lb_im, f_re, f_im, inv


PREP_G = 8


def _group_specs(are, logdt, bre):
    def spec(a):
        return pl.BlockSpec((PREP_G,) + a.shape[1:], lambda i: (i, 0, 0))
    return spec(are), spec(logdt), spec(bre)


def ssm_prep(are, aim, logdt, bre, bim):
    def body(are_r, aim_r, ldt_r, bre_r, bim_r, lre_o, lim_o, bbre_o, bbim_o):
        _, lb_re, lb_im, f_re, f_im, _ = _ssm_prep_values(are_r[...], aim_r[...], ldt_r[...])
        lre_o[...] = lb_re
        lim_o[...] = lb_im
        bbre_o[...] = f_re * bre_r[...] - f_im * bim_r[...]
        bbim_o[...] = f_re * bim_r[...] + f_im * bre_r[...]

    sh1 = jax.ShapeDtypeStruct(are.shape, F32)
    shb = jax.ShapeDtypeStruct(bre.shape, F32)
    s1, sd, sb = _group_specs(are, logdt, bre)
    return _pcall(body, name="ssm_prep", grid=(SSM_GROUPS // PREP_G,), in_specs=[s1, s1, sd, sb, sb], out_specs=[s1, s1, sb, sb],
                  out_shape=[sh1, sh1, shb, shb], compiler_params=_cparams(("parallel",)))(are, aim, logdt, bre, bim)


def ssm_prep_bwd(are, aim, logdt, bre, bim, dbbre, dbbim, dlre, dlim):
    def body(are_r, aim_r, ldt_r, bre_r, bim_r, dbbre_r, dbbim_r, dlre_r, dlim_r, dare_o, daim_o, dldt_o, dbre_o, dbim_o):
        are_v, aim_v = are_r[...], aim_r[...]
        dt, lb_re, lb_im, f_re, f_im, inv = _ssm_prep_values(are_v, aim_v, ldt_r[...])
        b_re, b_im, g_re, g_im = bre_r[...], bim_r[...], dbbre_r[...], dbbim_r[...]
        dbre_o[...] = f_re * g_re + f_im * g_im
        dbim_o[...] = f_re * g_im - f_im * g_re
        df_re = jnp.sum(b_re * g_re + b_im * g_im, axis=-1, keepdims=True)
        df_im = jnp.sum(b_re * g_im - b_im * g_re, axis=-1, keepdims=True)
        il_re, il_im = are_v * inv, -aim_v * inv
        cl_re = dlre_r[...] + il_re * df_re + il_im * df_im
        cl_im = dlim_r[...] + il_re * df_im - il_im * df_re
        q_re = -(f_re * il_re - f_im * il_im)
        q_im = -(f_re * il_im + f_im * il_re)
        ca_re = q_re * df_re + q_im * df_im
        ca_im = q_re * df_im - q_im * df_re
        cz_re = lb_re * cl_re + lb_im * cl_im
        cz_im = lb_re * cl_im - lb_im * cl_re
        dare_o[...] = ca_re + dt * cz_re
        daim_o[...] = ca_im + dt * cz_im
        dldt_o[...] = dt * jnp.sum(are_v * cz_re + aim_v * cz_im, axis=1, keepdims=True)

    sh1 = jax.ShapeDtypeStruct(are.shape, F32)
    shb = jax.ShapeDtypeStruct(bre.shape, F32)
    s1, sd, sb = _group_specs(are, logdt, bre)
    return _pcall(
        body, name="ssm_prep_bwd", grid=(SSM_GROUPS // PREP_G,), in_specs=[s1, s1, sd, sb, sb, sb, sb, s1, s1],
        out_specs=[s1, s1, sd, sb, sb], out_shape=[sh1, sh1, jax.ShapeDtypeStruct(logdt.shape, F32), shb, shb],
        compiler_params=_cparams(("parallel",)),
    )(are, aim, logdt, bre, bim, dbbre, dbbim, dlre, dlim)


SCAN_WC = 512


def _chain_segments(a_re, a_im, e_re, e_im, nsq, reverse):
    p_re, p_im = a_re, a_im
    for _ in range(nsq):
        p_re, p_im = p_re * p_re - p_im * p_im, 2.0 * p_re * p_im
    row = lax.broadcasted_iota(jnp.int32, e_re.shape, 0)
    edge = (row == SEGS - 1) if reverse else (row == 0)
    shift = SEGS - 1 if reverse else 1
    c_re, c_im = jnp.zeros_like(e_re), jnp.zeros_like(e_im)
    for _ in range(SEGS - 1):
        n_re = p_re * c_re - p_im * c_im + e_re
        n_im = p_re * c_im + p_im * c_re + e_im
        c_re = jnp.where(edge, 0.0, pltpu.roll(n_re, shift, 0))
        c_im = jnp.where(edge, 0.0, pltpu.roll(n_im, shift, 0))
    return c_re, c_im


def _scan_dims(s):
    steps = s // SEGS
    assert steps & (steps - 1) == 0
    tt = min(128, steps)
    return steps, tt, steps // tt, tt * SEGS, int(math.log2(steps))


U_BLK = SSM_W // BD


def ssm_fwd(u_s, dvec, w_bre, w_bim, w_cre, w_cim_neg, lre, lim, name, carry=None):
    s = u_s.shape[0]
    steps, tt, nch, rows, nsq = _scan_dims(s)
    nb, ub_w, wc = w_bre.shape

    def body(u_r, d_r, bre_r, bim_r, cre_r, cim_r, lre_r, lim_r, yg_o, ys_o, hre_o, him_o, hin_re_o, hin_im_o,
             st_re, st_im, x_re, x_im, h_re, h_im):
        ps, ch = pl.program_id(1), pl.program_id(2)
        a_re = jnp.broadcast_to(lre_r[...], (SEGS, wc))
        a_im = jnp.broadcast_to(lim_r[...], (SEGS, wc))
        ub = u_r[...]
        ub16 = ub.astype(BF16)
        x_re[...] = jnp.dot(ub16, bre_r[...], preferred_element_type=F32)
        x_im[...] = jnp.dot(ub16, bim_r[...], preferred_element_type=F32)

        @pl.when(jnp.logical_and(ps == 0, ch == 0))
        def _():
            st_re[...] = jnp.zeros_like(st_re)
            st_im[...] = jnp.zeros_like(st_im)

        @pl.when(jnp.logical_and(ps == 1, ch == 0))
        def _():
            c_re, c_im = _chain_segments(a_re, a_im, st_re[...], st_im[...], nsq, False)
            st_re[...] = c_re
            st_im[...] = c_im
            hin_re_o[...] = c_re
            hin_im_o[...] = c_im

        def run(store):
            def step(t, hc):
                off = pl.multiple_of(t * SEGS, SEGS)
                n_re = a_re * hc[0] - a_im * hc[1] + x_re[pl.ds(off, SEGS), :]
                n_im = a_re * hc[1] + a_im * hc[0] + x_im[pl.ds(off, SEGS), :]
                if store:
                    h_re[pl.ds(off, SEGS), :] = n_re
                    h_im[pl.ds(off, SEGS), :] = n_im
                return n_re, n_im

            fin = lax.fori_loop(0, tt, step, (st_re[...], st_im[...]))
            st_re[...] = fin[0]
            st_im[...] = fin[1]

        @pl.when(ps == 0)
        def _():
            run(False)

        @pl.when(ps == 1)
        def _():
            run(True)
            hr16, hi16 = h_re[...].astype(BF16), h_im[...].astype(BF16)
            hre_o[...] = hr16
            him_o[...] = hi16
            y = jnp.dot(hr16, cre_r[...], preferred_element_type=F32) + jnp.dot(hi16, cim_r[...], preferred_element_type=F32)
            y = y + d_r[...] * ub
            ys_o[...] = y
            yg_o[...] = _gelu(y)[0].astype(BF16)

    def pass1(ps, c):
        return jnp.where(ps == 1, c, 0)

    u_spec = pl.BlockSpec((rows, ub_w), lambda j, ps, c: (c, j))
    d_spec = pl.BlockSpec((1, ub_w), lambda j, ps, c: (0, j))
    b_spec = pl.BlockSpec((None, ub_w, wc), lambda j, ps, c: (j, 0, 0))
    c_spec = pl.BlockSpec((None, wc, ub_w), lambda j, ps, c: (j, 0, 0))
    l_spec = pl.BlockSpec((1, wc), lambda j, ps, c: (0, j))
    y_spec = pl.BlockSpec((rows, ub_w), lambda j, ps, c: (pass1(ps, c), j))
    h_spec = pl.BlockSpec((rows, wc), lambda j, ps, c: (pass1(ps, c), j))
    e_spec = pl.BlockSpec((SEGS, wc), lambda j, ps, c: (0, j))
    return _run(
        body, [u_s, dvec, w_bre, w_bim, w_cre, w_cim_neg, lre, lim], carry=carry, name=name, grid=(nb, 2, nch),
        in_specs=[u_spec, d_spec, b_spec, b_spec, c_spec, c_spec, l_spec, l_spec],
        out_specs=[y_spec, y_spec, h_spec, h_spec, e_spec, e_spec],
        out_shape=[jax.ShapeDtypeStruct((s, SSM_W), BF16), jax.ShapeDtypeStruct((s, SSM_W), F32),
                   jax.ShapeDtypeStruct((s, STATE_W), BF16), jax.ShapeDtypeStruct((s, STATE_W), BF16),
                   jax.ShapeDtypeStruct((SEGS, STATE_W), F32), jax.ShapeDtypeStruct((SEGS, STATE_W), F32)],
        scratch_shapes=[pltpu.VMEM((SEGS, wc), F32)] * 2 + [pltpu.VMEM((rows, wc), F32)] * 4,
        compiler_params=_cparams(("parallel", "arbitrary", "arbitrary")),
    )


def ssm_bwd(dyg_s, ys, u_s, h_re, h_im, hin_re, hin_im, gin_re, gin_im, dvec, w_bre_t, w_bim_t, w_cre_t, w_cim_neg_t, lre, lim,
            name, carry=None):
    s = u_s.shape[0]
    steps, tt, nch, rows, nsq = _scan_dims(s)
    half = 2 * SEGS

    def body(dyg_r, ys_r, u_r, hre_r, him_r, pre_r, pim_r, cin_re_r, cin_im_r, gin_re_r, gin_im_r, d_r, bre_r, bim_r, cre_r,
             cim_r, lre_r, lim_r, du_o, dbre_o, dbim_o, dcre_o, dcim_o, dlre_o, dlim_o, dd_o,
             st_re, st_im, x_re, x_im, g_re, g_im, hf_re, hf_im):
        ch = pl.program_id(1)
        a_re = jnp.broadcast_to(lre_r[...], (SEGS, SCAN_WC))
        a_im = -jnp.broadcast_to(lim_r[...], (SEGS, SCAN_WC))
        ub, y = u_r[...], ys_r[...]
        dy = dyg_r[...] * _gelu_grad(y, _gelu(y)[1])
        dy16 = dy.astype(BF16)
        x_re[...] = jnp.dot(dy16, cre_r[...], preferred_element_type=F32)
        x_im[...] = jnp.dot(dy16, cim_r[...], preferred_element_type=F32)

        @pl.when(ch == 0)
        def _():
            st_re[...] = gin_re_r[...]
            st_im[...] = gin_im_r[...]
            dlre_o[...] = jnp.zeros_like(dlre_o)
            dlim_o[...] = jnp.zeros_like(dlim_o)

        hf_re[...] = hre_r[...].astype(F32)
        hf_im[...] = him_r[...].astype(F32)
        first_chunk = ch == nch - 1
        edge_re = jnp.where(first_chunk, cin_re_r[...], pre_r[...].astype(F32)[SEGS:, :])
        edge_im = jnp.where(first_chunk, cin_im_r[...], pim_r[...].astype(F32)[SEGS:, :])

        def step(i, hc):
            t = tt - 1 - i
            off = pl.multiple_of(t * SEGS, SEGS)
            n_re = a_re * hc[0] - a_im * hc[1] + x_re[pl.ds(off, SEGS), :]
            n_im = a_re * hc[1] + a_im * hc[0] + x_im[pl.ds(off, SEGS), :]
            g_re[pl.ds(off, SEGS), :] = n_re
            g_im[pl.ds(off, SEGS), :] = n_im
            offp = pl.multiple_of(jnp.maximum(t - 1, 0) * SEGS, SEGS)
            hp_re = jnp.where(t == 0, edge_re, hf_re[pl.ds(offp, SEGS), :])
            hp_im = jnp.where(t == 0, edge_im, hf_im[pl.ds(offp, SEGS), :])
            return n_re, n_im, hc[2] + hp_re * n_re + hp_im * n_im, hc[3] + hp_re * n_im - hp_im * n_re

        fin = lax.fori_loop(0, tt, step, (st_re[...], st_im[...], dlre_o[...], dlim_o[...]))
        st_re[...] = fin[0]
        st_im[...] = fin[1]
        dlre_o[...] = fin[2]
        dlim_o[...] = fin[3]

        gr16, gi16 = g_re[...].astype(BF16), g_im[...].astype(BF16)
        du = jnp.dot(gr16, bre_r[...], preferred_element_type=F32) + jnp.dot(gi16, bim_r[...], preferred_element_type=F32)
        du_o[...] = du + d_r[...] * dy
        ub16 = ub.astype(BF16)
        parts = [
            (dbre_o, lax.dot_general(ub16, gr16, _DN["tn"], preferred_element_type=F32)),
            (dbim_o, lax.dot_general(ub16, gi16, _DN["tn"], preferred_element_type=F32)),
            (dcre_o, lax.dot_general(hre_r[...], dy16, _DN["tn"], preferred_element_type=F32)),
            (dcim_o, lax.dot_general(him_r[...], dy16, _DN["tn"], preferred_element_type=F32)),
            (dd_o, jnp.sum(dy * ub, axis=0, keepdims=True)),
        ]
        for ref, val in parts:
            @pl.when(ch == 0)
            def _(ref=ref, val=val):
                ref[...] = val

            @pl.when(ch > 0)
            def _(ref=ref, val=val):
                ref[...] += val

    def chunk(c):
        return nch - 1 - c

    u_spec = pl.BlockSpec((rows, U_BLK), lambda j, c: (chunk(c), j))
    h_spec = pl.BlockSpec((rows, SCAN_WC), lambda j, c: (chunk(c), j))
    prev_spec = pl.BlockSpec((half, SCAN_WC), lambda j, c: (jnp.maximum(chunk(c) * (rows // half) - 1, 0), j))
    e_spec = pl.BlockSpec((SEGS, SCAN_WC), lambda j, c: (0, j))
    d_spec = pl.BlockSpec((1, U_BLK), lambda j, c: (0, j))
    bt_spec = pl.BlockSpec((None, SCAN_WC, U_BLK), lambda j, c: (j, 0, 0))
    ct_spec = pl.BlockSpec((None, U_BLK, SCAN_WC), lambda j, c: (j, 0, 0))
    l_spec = pl.BlockSpec((1, SCAN_WC), lambda j, c: (0, j))
    return _run(
        body, [dyg_s, ys, u_s, h_re, h_im, h_re, h_im, hin_re, hin_im, gin_re, gin_im, dvec, w_bre_t, w_bim_t, w_cre_t,
               w_cim_neg_t, lre, lim],
        carry=carry, name=name, grid=(BD, nch),
        in_specs=[u_spec, u_spec, u_spec, h_spec, h_spec, prev_spec, prev_spec, e_spec, e_spec, e_spec, e_spec, d_spec,
                  bt_spec, bt_spec, ct_spec, ct_spec, l_spec, l_spec],
        out_specs=[u_spec, ct_spec, ct_spec, bt_spec, bt_spec, e_spec, e_spec, d_spec],
        out_shape=[jax.ShapeDtypeStruct((s, SSM_W), F32)] + [jax.ShapeDtypeStruct((BD, U_BLK, SCAN_WC), F32)] * 2
        + [jax.ShapeDtypeStruct((BD, SCAN_WC, U_BLK), F32)] * 2 + [jax.ShapeDtypeStruct((SEGS, STATE_W), F32)] * 2
        + [jax.ShapeDtypeStruct((1, SSM_W), F32)],
        scratch_shapes=[pltpu.VMEM((SEGS, SCAN_WC), F32)] * 2 + [pltpu.VMEM((rows, SCAN_WC), F32)] * 6,
        compiler_params=_cparams(("parallel", "arbitrary")),
    )


def ssm_bwd_ends(dyg_s, ys, w_cre_t, w_cim_neg_t, lre, lim, name, carry=None):
    s = ys.shape[0]
    steps, tt, nch, rows, nsq = _scan_dims(s)
    nb, ub_w, wc = w_cre_t.shape

    def body(dyg_r, ys_r, cre_r, cim_r, lre_r, lim_r, gin_re_o, gin_im_o, st_re, st_im, x_re, x_im):
        ch = pl.program_id(1)
        a_re = jnp.broadcast_to(lre_r[...], (SEGS, wc))
        a_im = -jnp.broadcast_to(lim_r[...], (SEGS, wc))
        y = ys_r[...]
        dy16 = (dyg_r[...] * _gelu_grad(y, _gelu(y)[1])).astype(BF16)
        x_re[...] = jnp.dot(dy16, cre_r[...], preferred_element_type=F32)
        x_im[...] = jnp.dot(dy16, cim_r[...], preferred_element_type=F32)

        @pl.when(ch == 0)
        def _():
            st_re[...] = jnp.zeros_like(st_re)
            st_im[...] = jnp.zeros_like(st_im)

        def step(i, hc):
            off = pl.multiple_of((tt - 1 - i) * SEGS, SEGS)
            return (a_re * hc[0] - a_im * hc[1] + x_re[pl.ds(off, SEGS), :],
                    a_re * hc[1] + a_im * hc[0] + x_im[pl.ds(off, SEGS), :])

        fin = lax.fori_loop(0, tt, step, (st_re[...], st_im[...]))
        st_re[...] = fin[0]
        st_im[...] = fin[1]

        @pl.when(ch == nch - 1)
        def _():
            c_re, c_im = _chain_segments(a_re, a_im, fin[0], fin[1], nsq, True)
            gin_re_o[...] = c_re
            gin_im_o[...] = c_im

    y_spec = pl.BlockSpec((rows, ub_w), lambda j, c: (nch - 1 - c, j))
    ct_spec = pl.BlockSpec((None, ub_w, wc), lambda j, c: (j, 0, 0))
    l_spec = pl.BlockSpec((1, wc), lambda j, c: (0, j))
    e_spec = pl.BlockSpec((SEGS, wc), lambda j, c: (0, j))
    return _run(
        body, [dyg_s, ys, w_cre_t, w_cim_neg_t, lre, lim], carry=carry, name=name, grid=(nb, nch),
        in_specs=[y_spec, y_spec, ct_spec, ct_spec, l_spec, l_spec], out_specs=[e_spec, e_spec],
        out_shape=[jax.ShapeDtypeStruct((SEGS, STATE_W), F32)] * 2,
        scratch_shapes=[pltpu.VMEM((SEGS, wc), F32)] * 2 + [pltpu.VMEM((rows, wc), F32)] * 2,
        compiler_params=_cparams(("parallel", "arbitrary")),
    )


FWD_BD = 4


def _block_diag(m, nb=BD):
    g, r, c = m.shape
    m = m.reshape(nb, g // nb, r, c)
    eye = jnp.eye(g // nb, dtype=m.dtype)
    return jnp.einsum("jarc,ab->jarbc", m, eye).reshape(nb, (g // nb) * r, (g // nb) * c)


def _block_diag_extract(m, r, c):
    per = m.shape[1] // r
    m = m.reshape(BD, per, r, per, c)
    return jnp.einsum("jarac->jarc", m).reshape(BD * per, r, c)


def to_segments(a):
    s, w = a.shape
    return a.reshape(SEGS, s // SEGS, w).transpose(1, 0, 2).reshape(s, w)


def from_segments(a):
    s, w = a.shape
    return a.reshape(s // SEGS, SEGS, w).transpose(1, 0, 2).reshape(s, w)


W_IN_CHUNK_ROWS = (320, 320, 320, 240, 576, 272)
FFN_GATE_ROWS_FIRST = 480
TALL_TM = 2048
FFN_TN = 512


def local_step(x, target, shards, small):
    s = x.shape[0]
    g1, g2, g3, g4 = (small[k].reshape(1, D_MODEL) for k in ("norm_mix_pre", "norm_mix_post", "norm_ffn_pre", "norm_ffn_post"))
    dvec = small["ssm_d"].reshape(1, SSM_W)
    wts, recv = {}, {}

    def gathered(names, blocks):
        for n, b in zip(names, blocks):
            wts[n] = _full_from_gathered(b, n)

    def rms_in_fn(r, c):
        hh = _rms(r[0], c[0])[0].astype(BF16)
        return [hh, _permute(_perm_matrix(PERM_TS, 4, False), hh), _permute(_perm_matrix(PERM_TS, 16, False), hh)], []

    (h, h4, h16), got = rowwise("rms_in", rms_in_fn, [x], [g1], [(D_MODEL, BF16), (D_MODEL, BF16, 4), (D_MODEL, BF16, 16)],
                                ts=PERM_TS, carry=Gather([shards["w_in"]]))
    w_in_t = _full_from_gathered(got[0], "w_in")
    w_u_t, w_gates_t = w_in_t[3 * HQ:3 * HQ + SSM_W], w_in_t[3 * HQ + SSM_W:]

    def qkv_rows(g):
        return 3 * GROUP_W, lambda t: 3 * t + g

    hd = [h.reshape(1, s, D_MODEL), h4, h16]
    qkv = [None] * 3
    names = ("w_attn_up", "w_glu_v", "w_glu_g")
    qkv[0], got = mm([(hd[0].reshape(s, D_MODEL), w_in_t)], "nt", BF16, "mm_qkv0", tm=TALL_TM, tn=GROUP_W, b_window=qkv_rows(0),
                     carry=Gather([shards[n] for n in names]))
    gathered(names, got)
    qkv[1] = mm([(hd[1].reshape(s, D_MODEL), w_in_t)], "nt", BF16, "mm_qkv1", tm=TALL_TM, tn=GROUP_W, b_window=qkv_rows(1))
    qkv[2] = mm([(hd[2].reshape(s, D_MODEL), w_in_t)], "nt", BF16, "mm_qkv2", tm=TALL_TM, tn=GROUP_W, b_window=qkv_rows(2))
    u = mm([(h, w_u_t)], "nt", F32, "mm_u")
    gates, got = mm([(h, w_gates_t)], "nt", BF16, "mm_gates", carry=Gather([shards["w_ffn_gate"]]))
    gathered(("w_ffn_gate",), got)

    outs, lses = [], []
    for g, (_, dil) in enumerate(ATTN_GROUPS):
        o, l = attn_fwd(qkv[g].reshape(dil, s // dil, 3 * GROUP_W), g, f"attn_fwd{g}")
        outs.append(o.reshape(s, GROUP_W) if dil == 1 else o)
        lses.append(l.reshape(s, GROUP_W) if dil == 1 else l)

    def natural(r):
        back4, back16 = _perm_matrix(PERM_TS, 4, True), _perm_matrix(PERM_TS, 16, True)
        return (r[0], _permute(back4, r[1].astype(BF16)), _permute(back16, r[2].astype(BF16)),
                r[3], _permute(back4, r[4]), _permute(back16, r[5]))

    def merge_fn(r, c):
        o0, o1, o2, l0, l1, l2 = natural(r)
        w0, w1, w2 = _mix_weights(l0, l1, l2)
        return [w0 * o0 + w1 * o1 + w2 * o2], []

    (attn,) = rowwise("attn_merge", merge_fn, outs + lses, [], [(GROUP_W, BF16)], ts=PERM_TS)
    attn_branch = mm([(attn, wts["w_attn_up"])], "nn", BF16, "mm_up", tm=TALL_TM)

    are3 = small["ssm_a_re"].reshape(SSM_GROUPS, SSM_STATE, 1)
    aim3 = small["ssm_a_im"].reshape(SSM_GROUPS, SSM_STATE, 1)
    ldt3 = small["ssm_log_dt"].reshape(SSM_GROUPS, 1, 1)
    bre3 = small["ssm_b_re"].reshape(SSM_GROUPS, SSM_STATE, SSM_GROUP)
    bim3 = small["ssm_b_im"].reshape(SSM_GROUPS, SSM_STATE, SSM_GROUP)
    cre3 = small["ssm_c_re"].reshape(SSM_GROUPS, SSM_GROUP, SSM_STATE)
    cim3 = small["ssm_c_im"].reshape(SSM_GROUPS, SSM_GROUP, SSM_STATE)
    lre3, lim3, bbre, bbim = ssm_prep(are3, aim3, ldt3, bre3, bim3)
    lre, lim = lre3.reshape(1, STATE_W), lim3.reshape(1, STATE_W)
    w_bre = _block_diag(bbre.transpose(0, 2, 1)).astype(BF16)
    w_bim = _block_diag(bbim.transpose(0, 2, 1)).astype(BF16)
    w_cre = _block_diag(cre3.transpose(0, 2, 1)).astype(BF16)
    w_cim = _block_diag(cim3.transpose(0, 2, 1)).astype(BF16)
    u_s = to_segments(u)
    fwd_w = [_block_diag(t.transpose(0, 2, 1), FWD_BD).astype(BF16) for t in (bbre, bbim, cre3, -cim3)]
    (yg_s, y_ssm, h_re, h_im, hin_re, hin_im), got = ssm_fwd(
        u_s, dvec, *fwd_w, lre, lim, "ssm_fwd", carry=Gather([shards["w_ffn_up"], shards["w_out"]], pass_early=True))
    gathered(("w_ffn_up", "w_out"), got)
    yg = from_segments(yg_s)
    gv = mm([(yg, wts["w_glu_v"])], "nn", BF16, "mm_glu_v", tm=TALL_TM)
    gg = mm([(yg, wts["w_glu_g"])], "nn", BF16, "mm_glu_g", tm=TALL_TM)

    def gate_fn(r, c):
        gts, ab, gv_, gg_ = r
        sa, ss = _sigmoid(gts[:, :D_MODEL]), _sigmoid(gts[:, D_MODEL:])
        return [sa * ab + ss * (gv_ * _sigmoid(gg_))], []

    (merged,) = rowwise("gate_merge", gate_fn, [gates, attn_branch, gv, gg], [], [(D_MODEL, BF16)])
    o_mix = mm([(merged, wts["w_out"])], "nn", F32, "mm_out")

    def mid_fn(r, c):
        x1 = r[0] + _rms(r[1], c[0])[0]
        return [x1, _rms(x1, c[1])[0]], []

    x1, h2 = rowwise("rms_mid", mid_fn, [x, o_mix], [g2, g3], [(D_MODEL, F32), (D_MODEL, BF16)])
    (fa, fb, fin), got = mm([(h2, wts["w_ffn_gate"]), (h2, wts["w_ffn_up"])], "nt", [BF16, BF16, BF16], "mm_ffn_in", tn=FFN_TN,
                            epilogue=lambda p, e: [p[0], p[1], p[0] * _sigmoid(p[0]) * p[1]],
                            carry=Gather([shards["w_ffn_down"]], pass_early=True))
    gathered(("w_ffn_down",), got)
    f = mm([(fin, wts["w_ffn_down"])], "nn", F32, "mm_ffn_down", tn=512, tk=D_FF)

    def loss_fn(r, c):
        x1_, f_, tgt = r
        y, n, rr = _rms(f_, c[0])
        err = x1_ + y - tgt
        dout = err * (1.0 / D_MODEL)
        df, dg = _rms_bwd(dout, n, rr, c[0])
        lp = 0.5 * jnp.sum(jnp.sum(err * err, axis=-1, keepdims=True) * (1.0 / D_MODEL), axis=0, keepdims=True)
        return [df, dout], [dg, lp]

    df, dout, dg4, loss_part = rowwise("loss_bwd", loss_fn, [x1, f, target], [g4], [(D_MODEL, BF16), (D_MODEL, BF16)],
                                       acc_outs=[(1, D_MODEL), (1, 1)])
    def sent(names, blocks):
        for n, b in zip(names, blocks):
            recv[n] = b

    def to_owners(names, dws):
        return AllToAll([_split_for_devices(d, n) for n, d in zip(names, dws)])

    def swiglu_bwd(p, e):
        dfin_, (a, b) = p[0], e
        sg = _sigmoid(a)
        return [dfin_ * b * (sg * (1.0 + a * (1.0 - sg))), dfin_ * a * sg]

    da, db = mm([(df, wts["w_ffn_down"])], "nt", [BF16, BF16], "mm_d_fin", tn=FFN_TN, epilogue=swiglu_bwd, extras=[fa, fb])
    dw_ffn_down = mm([(fin, df)], "tn", BF16, "mm_dw_ffn_down")
    dh2, got = mm([(da, wts["w_ffn_gate"]), (db, wts["w_ffn_up"])], "nn", F32, "mm_d_h2", tm=512, tn=512, tk=D_FF,
                  carry=to_owners(["w_ffn_down"], [dw_ffn_down]))
    sent(["w_ffn_down"], got)
    dw_ffn_gate = mm([(da, h2)], "tn", BF16, "mm_dw_ffn_gate")
    gate_blocks = _split_for_devices(dw_ffn_gate, "w_ffn_gate")
    dw_ffn_up, (gate_landed,) = mm([(db, h2)], "tn", BF16, "mm_dw_ffn_up",
                                   carry=RowsToOwners(gate_blocks, 0, FFN_GATE_ROWS_FIRST))

    def mid_bwd(r, c):
        dh2_, dout_, x1_, o_ = r
        _, n3, r3 = _rms(x1_, c[1])
        dx1, dg3_ = _rms_bwd(dh2_, n3, r3, c[1])
        dx1 = dx1 + dout_
        _, n2, r2 = _rms(o_, c[0])
        do_, dg2_ = _rms_bwd(dx1, n2, r2, c[0])
        return [dx1, do_], [dg2_, dg3_]

    rest = gate_blocks.shape[1] - FFN_GATE_ROWS_FIRST
    (dx1, do_mix, dg2, dg3), (gate_landed,) = rowwise(
        "rms_mid_bwd", mid_bwd, [dh2, dout, x1, o_mix], [g2, g3], [(D_MODEL, F32), (D_MODEL, BF16)],
        acc_outs=[(1, D_MODEL), (1, D_MODEL)], carry=RowsToOwners(gate_blocks, FFN_GATE_ROWS_FIRST, rest, into=gate_landed))
    recv["w_ffn_gate"] = gate_landed
    dmerged = mm([(do_mix, wts["w_out"])], "nt", BF16, "mm_d_merged")
    dw_out = mm([(merged, do_mix)], "tn", BF16, "mm_dw_out")

    def gate_bwd(r, c):
        dm, gts, ab, gv_, gg_ = r
        sa, ss, sg = _sigmoid(gts[:, :D_MODEL]), _sigmoid(gts[:, D_MODEL:]), _sigmoid(gg_)
        branch = gv_ * sg
        dbranch = dm * ss
        dgates = jnp.concatenate([dm * ab * sa * (1.0 - sa), dm * branch * ss * (1.0 - ss)], axis=-1)
        return [dgates, dm * sa, dbranch * sg, dbranch * gv_ * sg * (1.0 - sg)], []

    dgates, dab, dgv, dgg = rowwise("gate_bwd", gate_bwd, [dmerged, gates, attn_branch, gv, gg], [],
                                    [(2 * D_MODEL, BF16), (D_MODEL, BF16), (D_MODEL, BF16), (D_MODEL, BF16)])
    dattn = mm([(dab, wts["w_attn_up"])], "nt", F32, "mm_d_attn")
    dw_up = mm([(attn, dab)], "tn", BF16, "mm_dw_up")
    dyg = mm([(dgv, wts["w_glu_v"]), (dgg, wts["w_glu_g"])], "nt", F32, "mm_d_yg")
    dw_glu_v = mm([(yg, dgv)], "tn", BF16, "mm_dw_glu_v")
    dw_glu_g = mm([(yg, dgg)], "tn", BF16, "mm_dw_glu_g")

    dyg_s = to_segments(dyg)
    (gin_re, gin_im), got = ssm_bwd_ends(dyg_s, y_ssm, fwd_w[2].transpose(0, 2, 1), fwd_w[3].transpose(0, 2, 1), lre, lim,
                                         "ssm_bwd_ends", carry=to_owners(["w_out"], [dw_out]))
    sent(["w_out"], got)
    (du_s, dbre_d, dbim_d, dcre_d, dcim_d, dl_re8, dl_im8, dd_ssm), got = ssm_bwd(
        dyg_s, y_ssm, u_s, h_re, h_im, hin_re, hin_im, gin_re, gin_im, dvec, w_bre.transpose(0, 2, 1), w_bim.transpose(0, 2, 1),
        w_cre.transpose(0, 2, 1), -w_cim.transpose(0, 2, 1), lre, lim, "ssm_bwd", carry=to_owners(["w_ffn_up"], [dw_ffn_up]))
    sent(["w_ffn_up"], got)
    dbb_re = _block_diag_extract(dbre_d, SSM_GROUP, SSM_STATE).transpose(0, 2, 1)
    dbb_im = _block_diag_extract(dbim_d, SSM_GROUP, SSM_STATE).transpose(0, 2, 1)
    dc_re = _block_diag_extract(dcre_d, SSM_STATE, SSM_GROUP).transpose(0, 2, 1)
    dc_im = -_block_diag_extract(dcim_d, SSM_STATE, SSM_GROUP).transpose(0, 2, 1)

    def fold8(r, c):
        return [], [jnp.sum(r[0], axis=0, keepdims=True), jnp.sum(r[1], axis=0, keepdims=True)]

    dl_re, dl_im = rowwise("ssm_dl_fold", fold8, [dl_re8, dl_im8], [], [], acc_outs=[(1, STATE_W), (1, STATE_W)], ts=SEGS)
    da_re, da_im, dldt, db_re, db_im = ssm_prep_bwd(
        are3, aim3, ldt3, bre3, bim3, dbb_re, dbb_im,
        dl_re.reshape(SSM_GROUPS, SSM_STATE, 1), dl_im.reshape(SSM_GROUPS, SSM_STATE, 1))
    du = from_segments(du_s)

    def merge_bwd(r, c):
        dat = r[0]
        o0, o1, o2, l0, l1, l2 = natural(r[1:])
        w0, w1, w2 = _mix_weights(l0, l1, l2)
        tot = _head_sum(dat * (w0 * o0 + w1 * o1 + w2 * o2))
        to4, to16 = _perm_matrix(PERM_TS, 4, False), _perm_matrix(PERM_TS, 16, False)
        return [w0 * dat, _permute(to4, (w1 * dat).astype(BF16)), _permute(to16, (w2 * dat).astype(BF16)),
                w0 * tot, _permute(to4, (w1 * tot).astype(BF16)), _permute(to16, (w2 * tot).astype(BF16))], []

    mb = rowwise("attn_merge_bwd", merge_bwd, [dattn] + outs + lses, [],
                 [(GROUP_W, BF16), (GROUP_W, BF16, 4), (GROUP_W, BF16, 16), (GROUP_W, BF16), (GROUP_W, BF16, 4), (GROUP_W, BF16, 16)],
                 ts=PERM_TS)
    dqs, dw_qkv = [], []
    names = ["w_glu_v", "w_glu_g", "w_attn_up"]
    for g, (_, dil) in enumerate(ATTN_GROUPS):
        dq = attn_bwd(qkv[g].reshape(dil, s // dil, 3 * GROUP_W), mb[g].reshape(dil, s // dil, GROUP_W),
                      lses[g].reshape(dil, s // dil, GROUP_W), mb[3 + g].reshape(dil, s // dil, GROUP_W),
                      g, f"attn_bwd{g}", carry=to_owners(names, [dw_glu_v, dw_glu_g, dw_up]) if g == 1 else None)
        if g == 1:
            dq, got = dq
            sent(names, got)
        dq = dq.reshape(s, 3 * GROUP_W)
        dqs.append(dq)
        dw_qkv.append(mm([(hd[g].reshape(s, D_MODEL), dq)], "tn", BF16, f"mm_dw_qkv{g}"))
    dw_u = mm([(h, du)], "tn", BF16, "mm_dw_u")
    dw_gates = mm([(h, dgates)], "tn", BF16, "mm_dw_gates")
    dw_in = jnp.concatenate(
        [dw_qkv[g][:, o * GROUP_W:(o + 1) * GROUP_W] for o in range(3) for g in range(3)] + [dw_u, dw_gates], axis=1)
    dw_in_blocks = _split_for_devices(dw_in, "w_in")
    starts = [sum(W_IN_CHUNK_ROWS[:i]) for i in range(len(W_IN_CHUNK_ROWS))]
    landed = None

    def chunk(i):
        return RowsToOwners(dw_in_blocks, starts[i], W_IN_CHUNK_ROWS[i], into=landed)

    dh_parts = []
    for g, (_, dil) in enumerate(ATTN_GROUPS):
        dh_g, (landed,) = mm([(dqs[g], w_in_t)], "nn", BF16, f"mm_d_h_qkv{g}", tk=GROUP_W, b_window=qkv_rows(g), carry=chunk(g))
        dh_parts.append(dh_g if dil == 1 else dh_g.reshape(dil, s // dil, D_MODEL))
    dh_u, (landed,) = mm([(du, w_u_t)], "nn", BF16, "mm_d_h_u", carry=chunk(3))
    dh_gates, (landed,) = mm([(dgates, w_gates_t)], "nn", BF16, "mm_d_h_gates", carry=chunk(4))
    dh_parts += [dh_u, dh_gates]

    def in_bwd(r, c):
        dh1 = _permute(_perm_matrix(PERM_TS, 4, True), r[1].astype(BF16))
        dh2_ = _permute(_perm_matrix(PERM_TS, 16, True), r[2].astype(BF16))
        dh = r[0] + dh1 + dh2_ + r[3] + r[4]
        _, n1, r1 = _rms(r[6], c[0])
        dx, dg1_ = _rms_bwd(dh, n1, r1, c[0])
        return [dx + r[5]], [dg1_]

    (grad_x, dg1), (landed,) = rowwise("rms_in_bwd", in_bwd, dh_parts + [dx1, x], [g1], [(D_MODEL, F32)],
                                       acc_outs=[(1, D_MODEL)], ts=PERM_TS, carry=chunk(5))
    recv["w_in"] = landed

    dsmall = dict(norm_mix_pre=dg1, ssm_a_re=da_re, ssm_a_im=da_im, ssm_log_dt=dldt, ssm_b_re=db_re, ssm_b_im=db_im,
                  ssm_c_re=dc_re, ssm_c_im=dc_im, ssm_d=dd_ssm, norm_mix_post=dg2, norm_ffn_pre=dg3, norm_ffn_post=dg4)
    return loss_part, grad_x, recv, dsmall


def adamw(parts, w, m, v, name, carry=None):
    r, c = w.shape
    tr = r
    while tr > 8 and tr % 2 == 0 and tr * c * (8 * parts.dtype.itemsize + 28) * 2 > 24 * 1024 * 1024:
        tr //= 2
    assert r % tr == 0 and (tr % 8 == 0 or tr == r)
    c1, c2 = 1.0 / (1.0 - ADAM_B1 ** ADAM_STEP), 1.0 / (1.0 - ADAM_B2 ** ADAM_STEP)

    def body(p_ref, w_ref, m_ref, v_ref, g_o, d_o, m_o, v_o):
        g = p_ref[0].astype(F32)
        for i in range(1, N_DEV):
            g = g + p_ref[i].astype(F32)
        mn = ADAM_B1 * m_ref[...] + (1.0 - ADAM_B1) * g
        vn = ADAM_B2 * v_ref[...] + (1.0 - ADAM_B2) * (g * g)
        g_o[...] = g
        m_o[...] = mn
        v_o[...] = vn
        d_o[...] = -ADAM_LR * ((mn * c1) / (jnp.sqrt(vn * c2) + ADAM_EPS) + ADAM_WD * w_ref[...])

    blk = pl.BlockSpec((tr, c), lambda i: (i, 0))
    return _run(
        body, [parts, w, m, v], carry=carry, name=name, grid=(r // tr,),
        in_specs=[pl.BlockSpec((N_DEV, tr, c), lambda i: (0, i, 0)), blk, blk, blk],
        out_specs=[blk] * 4, out_shape=[jax.ShapeDtypeStruct((r, c), F32)] * 4, compiler_params=_cparams(("parallel",)),
    )


PACK_C = 1024
SHARDED = ("w_in", "w_attn_up", "w_glu_v", "w_glu_g", "w_out", "w_ffn_gate", "w_ffn_up", "w_ffn_down")
ROW_SHARDED = ("w_out", "w_ffn_down")
SENT_TRANSPOSED = ("w_in", "w_ffn_gate", "w_ffn_up")
GRAD_TRANSPOSED = ("w_ffn_gate", "w_ffn_up")
SMALL = ("norm_mix_pre", "ssm_a_re", "ssm_a_im", "ssm_log_dt", "ssm_b_re", "ssm_b_im", "ssm_c_re", "ssm_c_im", "ssm_d",
         "norm_mix_post", "norm_ffn_pre", "norm_ffn_post")
WEIGHTS = ("norm_mix_pre", "w_in", "w_attn_up", "ssm_a_re", "ssm_a_im", "ssm_log_dt", "ssm_b_re", "ssm_b_im", "ssm_c_re",
           "ssm_c_im", "ssm_d", "w_glu_v", "w_glu_g", "w_out", "norm_mix_post", "norm_ffn_pre", "w_ffn_gate", "w_ffn_up",
           "w_ffn_down", "norm_ffn_post")


def _pack(arrs, dtype, pad_rows_to=64):
    flat = jnp.concatenate([a.reshape(-1).astype(dtype) for a in arrs])
    n = flat.shape[0]
    rows = -(-n // PACK_C)
    rows = -(-rows // pad_rows_to) * pad_rows_to
    return jnp.pad(flat, (0, rows * PACK_C - n)).reshape(rows, PACK_C)


def _unpack(flat2d, shapes):
    flat = flat2d.reshape(-1)
    out, off = [], 0
    for shp in shapes:
        n = int(np.prod(shp))
        out.append(flat[off:off + n].reshape(shp))
        off += n
    return out


def _full_from_gathered(gathered, name):
    if name in ROW_SHARDED or name in SENT_TRANSPOSED:
        return gathered.reshape(-1, gathered.shape[2])
    return gathered.transpose(1, 0, 2).reshape(gathered.shape[1], -1)


def _split_for_devices(full, name):
    if name in ROW_SHARDED or name in GRAD_TRANSPOSED:
        return full.reshape(N_DEV, -1, full.shape[1])
    return full.reshape(full.shape[0], N_DEV, -1).transpose(1, 0, 2)


def kernel(x, norm_mix_pre, w_in, w_attn_up, ssm_a_re, ssm_a_im, ssm_log_dt, ssm_b_re, ssm_b_im, ssm_c_re, ssm_c_im, ssm_d, w_glu_v, w_glu_g, w_out, norm_mix_post, norm_ffn_pre, w_ffn_gate, w_ffn_up, w_ffn_down, norm_ffn_post, loss_target, m_norm_mix_pre, m_w_in, m_w_attn_up, m_ssm_a_re, m_ssm_a_im, m_ssm_log_dt, m_ssm_b_re, m_ssm_b_im, m_ssm_c_re, m_ssm_c_im, m_ssm_d, m_w_glu_v, m_w_glu_g, m_w_out, m_norm_mix_post, m_norm_ffn_pre, m_w_ffn_gate, m_w_ffn_up, m_w_ffn_down, m_norm_ffn_post, v_norm_mix_pre, v_w_in, v_w_attn_up, v_ssm_a_re, v_ssm_a_im, v_ssm_log_dt, v_ssm_b_re, v_ssm_b_im, v_ssm_c_re, v_ssm_c_im, v_ssm_d, v_w_glu_v, v_w_glu_g, v_w_out, v_norm_mix_post, v_norm_ffn_pre, v_w_ffn_gate, v_w_ffn_up, v_w_ffn_down, v_norm_ffn_post):
    args = dict(locals())
    wv = {n: args[n][0] for n in WEIGHTS}
    mv = {n: args["m_" + n][0] for n in WEIGHTS}
    vv = {n: args["v_" + n][0] for n in WEIGHTS}

    shards = {n: (wv[n].T if n in SENT_TRANSPOSED else wv[n]).astype(BF16) for n in SHARDED}
    small = {n: wv[n] for n in SMALL}
    loss_part, grad_x, recv, dsmall = local_step(x[0], loss_target[0], shards, small)
    for n in GRAD_TRANSPOSED:
        recv[n] = recv[n].transpose(0, 2, 1)

    small_shapes = [wv[n].shape for n in SMALL]
    res = {}
    res["w_in"], (sgather,) = adamw(recv["w_in"], wv["w_in"], mv["w_in"], vv["w_in"], "adamw_w_in",
                                    carry=Gather([_pack([dsmall[n] for n in SMALL], F32)]))
    for n in SHARDED[1:]:
        res[n] = adamw(recv[n], wv[n], mv[n], vv[n], "adamw_" + n)
    sres = adamw(sgather, _pack([wv[n] for n in SMALL], F32), _pack([mv[n] for n in SMALL], F32),
                 _pack([vv[n] for n in SMALL], F32), "adamw_small")
    sun = [_unpack(t, small_shapes) for t in sres]
    for k, n in enumerate(SMALL):
        res[n] = tuple(sun[t][k] for t in range(4))

    loss = lax.psum(loss_part[0, 0], ("x", "y", "c"))
    outs = [loss, grad_x[None]]
    for t in range(4):
        outs += [res[n][t][None] for n in WEIGHTS]
    return tuple(outs)
```
